```python
import math
import jax, jax.numpy as jnp
from jax import lax
import numpy as np

D_MODEL = 1024
BATCH = 8
SEQ = 4096
DEPTH = 1

A_HEADS = 8
A_HEAD_DIM = 64
A_WIDTH = A_HEADS * A_HEAD_DIM
MOBA_BLOCK = 256
MOBA_TOPK = 3
MOBA_Q_CHUNK = 16
REL_BUCKETS = 32
REL_MAX_DIST = 128
B_HEADS = 8
QK_NOPE = 64
QK_ROPE = 32
V_HEAD = 64
B_WIDTH = B_HEADS * V_HEAD
Q_LORA = 384
KV_LORA = 256
ROPE_THETA = 10000.0
MLA_Q_BLOCK = 128
D_FF = 4 * D_MODEL
EPS = 1e-6
NEG = -1e30
IN_SPLITS = (A_WIDTH, A_WIDTH, A_WIDTH, Q_LORA, KV_LORA, QK_ROPE, D_MODEL, D_MODEL)
IN_COLS = sum(IN_SPLITS)

kernel_name = "hybrid_moba_mla_gated_block"


def rmsnorm(x, g):
    xf = x.astype(jnp.float32)
    y = xf * lax.rsqrt(jnp.mean(xf * xf, axis=-1, keepdims=True) + EPS)
    return (y * g.astype(jnp.float32)).astype(x.dtype)


def rope(x, pos):
    half = QK_ROPE // 2
    inv_freq = ROPE_THETA ** (-jnp.arange(half, dtype=jnp.float32) / half)
    ang = pos.astype(jnp.float32)[:, None] * inv_freq[None, :]
    cos = jnp.cos(ang)[None, :, None, :]
    sin = jnp.sin(ang)[None, :, None, :]
    xf = x.astype(jnp.float32)
    x1, x2 = xf[..., :half], xf[..., half:]
    return jnp.concatenate([x1 * cos - x2 * sin, x2 * cos + x1 * sin], axis=-1).astype(x.dtype)


def t5_bucket(dist):
    dist = jnp.maximum(dist, 0)
    max_exact = REL_BUCKETS // 2
    d = jnp.maximum(dist, 1).astype(jnp.float32)
    large = max_exact + (jnp.log(d / max_exact) / math.log(REL_MAX_DIST / max_exact)
                         * (REL_BUCKETS - max_exact)).astype(jnp.int32)
    large = jnp.minimum(large, REL_BUCKETS - 1)
    return jnp.where(dist < max_exact, dist, large)


def moba_attention(q, k, v, rel_bias):
    B, S, H, D = q.shape
    scale = D ** -0.5
    nblk = -(-S // MOBA_BLOCK)
    pad = nblk * MOBA_BLOCK - S
    kb = jnp.pad(k, ((0, 0), (0, pad), (0, 0), (0, 0))).reshape(B, nblk, MOBA_BLOCK, H, D).transpose(0, 3, 1, 2, 4)
    vb = jnp.pad(v, ((0, 0), (0, pad), (0, 0), (0, 0))).reshape(B, nblk, MOBA_BLOCK, H, D).transpose(0, 3, 1, 2, 4)
    k_mean = jnp.mean(kb, axis=3)
    pos = jnp.arange(S)
    q_blk = pos // MOBA_BLOCK
    gate = jnp.einsum('bshd,bhnd->bhsn', q, k_mean).astype(jnp.float32)
    past = jnp.arange(nblk)[None, :] < q_blk[:, None]
    gate = jnp.where(past[None, None], gate, NEG)
    n_sel = min(MOBA_TOPK, nblk)
    _, sel = lax.top_k(gate, n_sel)
    sel_valid = jnp.arange(n_sel)[None, :] < q_blk[:, None]
    bias_table = rel_bias.T
    nq = S // MOBA_Q_CHUNK
    QC = MOBA_Q_CHUNK
    qs = q.reshape(B, nq, QC, H, D).swapaxes(0, 1)
    sels = sel.reshape(B, H, nq, QC, n_sel).transpose(2, 0, 1, 3, 4)
    valids = sel_valid.reshape(nq, QC, n_sel)
    bi = jnp.arange(B)[:, None, None, None]
    hi = jnp.arange(H)[None, :, None, None]
    offs = jnp.arange(MOBA_BLOCK)

    def chunk_fn(args):
        c, qc, sc, vc = args
        q_pos = c * QC + jnp.arange(QC)
        own = (c * QC) // MOBA_BLOCK
        k_own = lax.dynamic_index_in_dim(kb, own, axis=2, keepdims=False)
        v_own = lax.dynamic_index_in_dim(vb, own, axis=2, keepdims=False)
        k_g = kb[bi, hi, sc]
        v_g = vb[bi, hi, sc]
        kpos_past = sc[..., None] * MOBA_BLOCK + offs
        bias_past = bias_table[hi[..., None], t5_bucket(q_pos[:, None, None] - kpos_past)]
        logit_past = jnp.einsum('bqhd,bhqrkd->bhqrk', qc, k_g).astype(jnp.float32) * scale + bias_past
        logit_past = jnp.where(vc[None, None, :, :, None], logit_past, NEG)
        kpos_own = own * MOBA_BLOCK + offs
        bias_own = bias_table[:, t5_bucket(q_pos[:, None] - kpos_own[None, :])]
        logit_own = jnp.einsum('bqhd,bhkd->bhqk', qc, k_own).astype(jnp.float32) * scale + bias_own
        logit_own = jnp.where((kpos_own[None, :] <= q_pos[:, None])[None, None], logit_own, NEG)
        logits = jnp.concatenate([logit_past.reshape(B, H, QC, n_sel * MOBA_BLOCK), logit_own], axis=-1)
        p = jax.nn.softmax(logits, axis=-1).astype(v.dtype)
        p_past = p[..., :n_sel * MOBA_BLOCK].reshape(B, H, QC, n_sel, MOBA_BLOCK)
        p_own = p[..., n_sel * MOBA_BLOCK:]
        return (jnp.einsum('bhqrk,bhqrkd->bqhd', p_past, v_g)
                + jnp.einsum('bhqk,bhkd->bqhd', p_own, v_own))

    out = lax.map(chunk_fn, (jnp.arange(nq), qs, sels, valids))
    return out.swapaxes(0, 1).reshape(B, S, H * D)


def mla_attention(q_nope, q_rope, k_nope, k_rope, v):
    B, S, H, _ = q_nope.shape
    scale = (QK_NOPE + QK_ROPE) ** -0.5
    nqb = S // MLA_Q_BLOCK
    k_pos = jnp.arange(S)
    qn = q_nope.reshape(B, nqb, MLA_Q_BLOCK, H, QK_NOPE).swapaxes(0, 1)
    qr = q_rope.reshape(B, nqb, MLA_Q_BLOCK, H, QK_ROPE).swapaxes(0, 1)

    def block_fn(args):
        i, qn_b, qr_b = args
        q_pos = i * MLA_Q_BLOCK + jnp.arange(MLA_Q_BLOCK)
        logits = (jnp.einsum('bqhd,bkhd->bhqk', qn_b, k_nope)
                  + jnp.einsum('bqhd,bkd->bhqk', qr_b, k_rope)).astype(jnp.float32) * scale
        mask = k_pos[None, :] <= q_pos[:, None]
        logits = jnp.where(mask[None, None], logits, NEG)
        p = jax.nn.softmax(logits, axis=-1).astype(v.dtype)
        return jnp.einsum('bhqk,bkhd->bqhd', p, v)

    out = lax.map(block_fn, (jnp.arange(nqb), qn, qr))
    return out.swapaxes(0, 1).reshape(B, S, H * V_HEAD)


def setup_inputs(seed: int = 0) -> dict:
    key = jax.random.key(seed)
    ks = jax.random.split(key, 17)

    def w(k, shape, fan_in):
        return jax.random.normal(k, shape, jnp.float32) * fan_in ** -0.5

    def gain(k, shape):
        return 1.0 + 0.05 * jax.random.normal(k, shape, jnp.float32)

    L = DEPTH
    return {
        'x': jax.random.normal(ks[0], (BATCH, SEQ, D_MODEL), jnp.float32),
        'w_in': w(ks[1], (L, D_MODEL, IN_COLS), D_MODEL),
        'rel_bias': 0.5 * jax.random.normal(ks[2], (REL_BUCKETS, A_HEADS), jnp.float32),
        'mla_q_norm': gain(ks[3], (L, Q_LORA)),
        'w_uq': w(ks[4], (L, Q_LORA, B_HEADS * (QK_NOPE + QK_ROPE)), Q_LORA),
        'mla_kv_norm': gain(ks[5], (L, KV_LORA)),
        'w_uk': w(ks[6], (L, KV_LORA, B_HEADS * QK_NOPE), KV_LORA),
        'w_uv': w(ks[7], (L, KV_LORA, B_HEADS * V_HEAD), KV_LORA),
        'w_proj_a': w(ks[8], (L, A_WIDTH, D_MODEL), A_WIDTH),
        'w_proj_b': w(ks[9], (L, B_WIDTH, D_MODEL), B_WIDTH),
        'w_out': w(ks[10], (L, D_MODEL, D_MODEL), D_MODEL),
        'norm_attn': gain(ks[11], (L, D_MODEL)),
        'norm_mlp': gain(ks[12], (L, D_MODEL)),
        'w_mlp_up': w(ks[13], (L, D_MODEL, D_FF), D_MODEL),
        'w_mlp_down': w(ks[14], (L, D_FF, D_MODEL), D_FF),
        'norm_final': gain(ks[15], (D_MODEL,)),
    }


def reference(x, w_in, rel_bias, mla_q_norm, w_uq, mla_kv_norm, w_uk, w_uv, w_proj_a, w_proj_b,
              w_out, norm_attn, norm_mlp, w_mlp_up, w_mlp_down, norm_final):
    B, S, _ = x.shape
    pos = jnp.arange(S)
    split_pts = list(np.cumsum(IN_SPLITS)[:-1])
    h = x
    for l in range(DEPTH):
        n = rmsnorm(h, norm_attn[l])
        proj = n @ w_in[l]
        qa, ka, va, cq, ckv, kr, ga, gb = jnp.split(proj, split_pts, axis=-1)
        oa = moba_attention(qa.reshape(B, S, A_HEADS, A_HEAD_DIM),
                            ka.reshape(B, S, A_HEADS, A_HEAD_DIM),
                            va.reshape(B, S, A_HEADS, A_HEAD_DIM), rel_bias)
        cq = rmsnorm(cq, mla_q_norm[l])
        q = (cq @ w_uq[l]).reshape(B, S, B_HEADS, QK_NOPE + QK_ROPE)
        q_nope, q_rope = q[..., :QK_NOPE], rope(q[..., QK_NOPE:], pos)
        ckv = rmsnorm(ckv, mla_kv_norm[l])
        k_nope = (ckv @ w_uk[l]).reshape(B, S, B_HEADS, QK_NOPE)
        v_b = (ckv @ w_uv[l]).reshape(B, S, B_HEADS, V_HEAD)
        k_rope = rope(kr[:, :, None, :], pos)[:, :, 0, :]
        ob = mla_attention(q_nope, q_rope, k_nope, k_rope, v_b)
        merged = jax.nn.sigmoid(ga) * (oa @ w_proj_a[l]) + jax.nn.sigmoid(gb) * (ob @ w_proj_b[l])
        h = h + merged @ w_out[l]
        m = rmsnorm(h, norm_mlp[l])
        h = h + jnp.square(jax.nn.relu(m @ w_mlp_up[l])) @ w_mlp_down[l]
    return rmsnorm(h, norm_final)
```

```python
import functools
import math

import jax
import jax.numpy as jnp
import numpy as np
from jax import lax
from jax.experimental import pallas as pl
from jax.experimental.pallas import tpu as pltpu

D_MODEL = 1024
A_HEADS = 8
A_HEAD_DIM = 64
A_WIDTH = A_HEADS * A_HEAD_DIM
MOBA_BLOCK = 256
MOBA_TOPK = 3
REL_BUCKETS = 32
REL_MAX_DIST = 128
B_HEADS = 8
QK_NOPE = 64
QK_ROPE = 32
V_HEAD = 64
B_WIDTH = B_HEADS * V_HEAD
Q_LORA = 384
KV_LORA = 256
ROPE_THETA = 10000.0
D_FF = 4 * D_MODEL
EPS = 1e-6
NEG = -1e30

LANES = 128
HEAD_PAIRS = A_HEADS // 2
MLA_HEAD_PAD = 128
ATTN_TILE = 256
ROW_TILE = 512
FF_CHUNK = 1024
VMEM_LIMIT = 56 * 1024 * 1024

F32 = jnp.float32
BF16 = jnp.bfloat16
NT_DIMS = (((1,), (1,)), ((), ()))


def _rms(xf, g):
    y = xf * lax.rsqrt(jnp.mean(xf * xf, axis=-1, keepdims=True) + EPS)
    return y * g


def _const_spec(shape):
    zeros = (0,) * len(shape)
    return pl.BlockSpec(shape, lambda *_: zeros, pipeline_mode=pl.Buffered(1))


def _bias_kernel(rel_ref, o_ref):
    h = pl.program_id(0)
    shape = (MOBA_BLOCK, 2 * MOBA_BLOCK)
    r = lax.broadcasted_iota(jnp.int32, shape, 0)
    c = lax.broadcasted_iota(jnp.int32, shape, 1)
    d = jnp.where(c < MOBA_BLOCK, r - c, r - c + 2 * MOBA_BLOCK)
    dist = jnp.maximum(d, 0)
    max_exact = REL_BUCKETS // 2
    df = jnp.maximum(dist, 1).astype(F32)
    large = max_exact + (jnp.log(df / max_exact) / math.log(REL_MAX_DIST / max_exact)
                         * (REL_BUCKETS - max_exact)).astype(jnp.int32)
    large = jnp.minimum(large, REL_BUCKETS - 1)
    bucket = jnp.where(dist < max_exact, dist, large)
    val = jnp.zeros(shape, F32)
    for b in range(REL_BUCKETS):
        val = jnp.where(bucket == b, rel_ref[b, h], val)
    o_ref[...] = jnp.where(d >= 0, val - rel_ref[REL_BUCKETS - 1, h], NEG)


def _bias_tiles(rel_bias):
    return pl.pallas_call(
        _bias_kernel,
        grid=(A_HEADS,),
        in_specs=[pl.BlockSpec(memory_space=pltpu.SMEM)],
        out_specs=pl.BlockSpec((None, MOBA_BLOCK, 2 * MOBA_BLOCK), lambda h: (h, 0, 0)),
        out_shape=jax.ShapeDtypeStruct((A_HEADS, MOBA_BLOCK, 2 * MOBA_BLOCK), F32),
        name="moba_bias_tiles",
    )(rel_bias)


def _inproj_kernel(x_ref, g_ref, wqkv_ref, wc_ref, qn_ref, kvn_ref, wuq_ref, wuqr_ref,
                   wuk_ref, wuv_ref, cq_ref, sq_ref, ck_ref, sk_ref,
                   qa_ref, ka_ref, va_ref, kmean_ref, qm_ref, km_ref, vb_ref):
    n = _rms(x_ref[...], g_ref[...]).astype(BF16)
    qkv = jnp.dot(n, wqkv_ref[...], preferred_element_type=F32)
    k = qkv[:, A_WIDTH:2 * A_WIDTH]
    qa_ref[...] = qkv[:, :A_WIDTH].astype(BF16)
    ka_ref[...] = k.astype(BF16)
    va_ref[...] = qkv[:, 2 * A_WIDTH:].astype(BF16)
    nblk = ROW_TILE // MOBA_BLOCK
    kmean_ref[0] = jnp.sum(k.reshape(nblk, MOBA_BLOCK, A_WIDTH), axis=1) * (1.0 / MOBA_BLOCK)

    c = jnp.dot(n, wc_ref[...], preferred_element_type=F32)
    cq = _rms(c[:, :Q_LORA], qn_ref[...]).astype(BF16)
    ckv = _rms(c[:, Q_LORA:Q_LORA + KV_LORA], kvn_ref[...]).astype(BF16)
    kr = c[:, Q_LORA + KV_LORA:Q_LORA + KV_LORA + LANES]
    krr = c[:, Q_LORA + KV_LORA + LANES:]
    q = jnp.dot(cq, wuq_ref[...], preferred_element_type=F32)
    qr = jnp.dot(cq, wuqr_ref[...], preferred_element_type=F32)
    kn = jnp.dot(ckv, wuk_ref[...], preferred_element_type=F32)
    vb_ref[...] = jnp.dot(ckv, wuv_ref[...], preferred_element_type=F32).astype(BF16)
    cq_t, sq_t = cq_ref[...], sq_ref[...]
    k_rope = kr * ck_ref[...] + krr * sk_ref[...]
    for h in range(B_HEADS):
        sl = slice(h * MLA_HEAD_PAD, (h + 1) * MLA_HEAD_PAD)
        qm_ref[:, sl] = (q[:, sl] * cq_t + qr[:, sl] * sq_t).astype(BF16)
        km_ref[:, sl] = (kn[:, sl] + k_rope).astype(BF16)


def _inproj(x2, g_attn, wqkv, wc, qn, kvn, wuq, wuqr, wuk, wuv, cq_t, sq_t, ck_t, sk_t, seq):
    n_rows = x2.shape[0]
    tm = ROW_TILE
    steps = n_rows // tm
    per_seq = seq // tm
    row = lambda i: (i, 0)
    tab = lambda i: (i % per_seq, 0)
    bf = lambda w: jax.ShapeDtypeStruct((n_rows, w), BF16)
    return pl.pallas_call(
        _inproj_kernel,
        grid=(steps,),
        in_specs=[
            pl.BlockSpec((tm, D_MODEL), row),
            _const_spec(g_attn.shape), _const_spec(wqkv.shape), _const_spec(wc.shape),
            _const_spec(qn.shape), _const_spec(kvn.shape), _const_spec(wuq.shape),
            _const_spec(wuqr.shape), _const_spec(wuk.shape), _const_spec(wuv.shape),
            pl.BlockSpec((tm, LANES), tab), pl.BlockSpec((tm, LANES), tab),
            pl.BlockSpec((tm, LANES), tab), pl.BlockSpec((tm, LANES), tab),
        ],
        out_specs=[
            pl.BlockSpec((tm, A_WIDTH), row), pl.BlockSpec((tm, A_WIDTH), row),
            pl.BlockSpec((tm, A_WIDTH), row),
            pl.BlockSpec((1, tm // MOBA_BLOCK, A_WIDTH), lambda i: (i, 0, 0)),
            pl.BlockSpec((tm, B_HEADS * MLA_HEAD_PAD), row),
            pl.BlockSpec((tm, B_HEADS * MLA_HEAD_PAD), row),
            pl.BlockSpec((tm, B_WIDTH), row),
        ],
        out_shape=[
            bf(A_WIDTH), bf(A_WIDTH), bf(A_WIDTH),
            jax.ShapeDtypeStruct((steps, tm // MOBA_BLOCK, A_WIDTH), F32),
            bf(B_HEADS * MLA_HEAD_PAD), bf(B_HEADS * MLA_HEAD_PAD), bf(B_WIDTH),
        ],
        compiler_params=pltpu.CompilerParams(
            dimension_semantics=("arbitrary",), vmem_limit_bytes=VMEM_LIMIT),
        name="in_proj",
    )(x2, g_attn, wqkv, wc, qn, kvn, wuq, wuqr, wuk, wuv, cq_t, sq_t, ck_t, sk_t)


def _softmax_init(s, v_tile, m_ref, l_ref, acc_ref):
    m = jnp.max(s, axis=-1, keepdims=True)
    p = jnp.exp(s - m)
    m_ref[...] = m
    l_ref[...] = jnp.sum(p, axis=-1, keepdims=True)
    acc_ref[...] = jnp.dot(p.astype(BF16), v_tile, preferred_element_type=F32)


def _softmax_update(s, v_tile, m_ref, l_ref, acc_ref):
    m_old = m_ref[...]
    m_new = jnp.maximum(m_old, jnp.max(s, axis=-1, keepdims=True))
    alpha = jnp.exp(m_old - m_new)
    p = jnp.exp(s - m_new)
    m_ref[...] = m_new
    l_ref[...] = alpha * l_ref[...] + jnp.sum(p, axis=-1, keepdims=True)
    acc_ref[...] = alpha * acc_ref[...] + jnp.dot(p.astype(BF16), v_tile,
                                                  preferred_element_type=F32)


def _tile_rows(j):
    return pl.ds(pl.multiple_of(j * ATTN_TILE, ATTN_TILE), ATTN_TILE)


def _moba_kernel(q_ref, k_ref, v_ref, kmean_ref, bias_ref, o_ref,
                 kaug_ref, m_ref, l_ref, acc_ref, *, seq):
    i = pl.program_id(2)
    nblk = seq // MOBA_BLOCK

    @pl.when(i == 0)
    def _():
        kaug_ref[:, :LANES] = k_ref[...]
        blk = lax.broadcasted_iota(jnp.int32, (seq, LANES), 0) // MOBA_BLOCK
        lane = lax.broadcasted_iota(jnp.int32, (seq, LANES), 1)
        kaug_ref[:, LANES:] = (blk == lane).astype(BF16)

    q = q_ref[...] * (A_HEAD_DIM ** -0.5)
    kmean = kmean_ref[...].astype(BF16)
    lane = lax.broadcasted_iota(jnp.int32, (ATTN_TILE, LANES), 1)
    row = lax.broadcasted_iota(jnp.int32, (nblk, ATTN_TILE), 0)
    past = row < i

    for h in range(2):
        in_head = (lane >= h * A_HEAD_DIM) & (lane < (h + 1) * A_HEAD_DIM)
        qh = jnp.where(in_head, q, jnp.zeros_like(q))
        gate = lax.dot_general(kmean, qh, NT_DIMS, preferred_element_type=F32)
        gate = jnp.where(past, gate, -jnp.inf)
        rank = jnp.zeros((nblk, ATTN_TILE), F32)
        for c in range(nblk):
            gc = gate[c:c + 1, :]
            beats = (gc > gate) | ((gc == gate) & (c < row))
            rank = rank + beats.astype(F32)
        keep = (past & (rank < MOBA_TOPK)) | (row == i)
        mask_t = jnp.where(keep, 0.0, NEG)
        mask_t = jnp.concatenate(
            [mask_t, jnp.zeros((LANES - nblk, ATTN_TILE), F32)], axis=0)
        q_aug = jnp.concatenate([qh, mask_t.T.astype(BF16)], axis=1)

        def logits(j):
            return lax.dot_general(q_aug, kaug_ref[_tile_rows(j), :], NT_DIMS,
                                   preferred_element_type=F32)

        state = (m_ref.at[h], l_ref.at[h], acc_ref.at[h])
        _softmax_init(logits(i) + bias_ref[h, :, :MOBA_BLOCK], v_ref[_tile_rows(i), :], *state)

        @pl.when(i >= 1)
        def _():
            _softmax_update(logits(i - 1) + bias_ref[h, :, MOBA_BLOCK:],
                            v_ref[_tile_rows(i - 1), :], *state)

        def far_block(j, carry):
            _softmax_update(logits(j), v_ref[_tile_rows(j), :], *state)
            return carry

        lax.fori_loop(0, jnp.maximum(i - 1, 0), far_block, 0)

    out = jnp.where(lane < A_HEAD_DIM, acc_ref[0] / l_ref[0], acc_ref[1] / l_ref[1])
    o_ref[...] = out.astype(o_ref.dtype)


def _moba(qa, ka, va, kmean, bias):
    b, seq, _ = qa.shape
    nq = seq // ATTN_TILE
    nblk = seq // MOBA_BLOCK
    full = lambda bi, hp, i: (bi, 0, hp)
    tile = lambda bi, hp, i: (bi, i, hp)
    return pl.pallas_call(
        functools.partial(_moba_kernel, seq=seq),
        grid=(b, HEAD_PAIRS, nq),
        in_specs=[
            pl.BlockSpec((None, ATTN_TILE, LANES), tile),
            pl.BlockSpec((None, seq, LANES), full),
            pl.BlockSpec((None, seq, LANES), full),
            pl.BlockSpec((None, nblk, LANES), full),
            pl.BlockSpec((2, MOBA_BLOCK, 2 * MOBA_BLOCK), lambda bi, hp, i: (hp, 0, 0)),
        ],
        out_specs=pl.BlockSpec((None, ATTN_TILE, LANES), tile),
        out_shape=jax.ShapeDtypeStruct((b, seq, A_WIDTH), BF16),
        scratch_shapes=[
            pltpu.VMEM((seq, 2 * LANES), BF16),
            pltpu.VMEM((2, ATTN_TILE, 1), F32),
            pltpu.VMEM((2, ATTN_TILE, 1), F32),
            pltpu.VMEM((2, ATTN_TILE, LANES), F32),
        ],
        compiler_params=pltpu.CompilerParams(
            dimension_semantics=("arbitrary", "arbitrary", "arbitrary"),
            vmem_limit_bytes=VMEM_LIMIT),
        name="moba_attention",
    )(qa, ka, va, kmean, bias)


def _mla_kernel(q_ref, k_ref, v_ref, o_ref, m_ref, l_ref, acc_ref):
    i = pl.program_id(2)
    lane = lax.broadcasted_iota(jnp.int32, (ATTN_TILE, LANES), 1)
    r = lax.broadcasted_iota(jnp.int32, (ATTN_TILE, ATTN_TILE), 0)
    c = lax.broadcasted_iota(jnp.int32, (ATTN_TILE, ATTN_TILE), 1)
    causal = c <= r

    for h in range(2):
        sl = slice(h * MLA_HEAD_PAD, (h + 1) * MLA_HEAD_PAD)
        qh = q_ref[:, sl]

        def logits(j):
            return lax.dot_general(qh, k_ref[_tile_rows(j), sl], NT_DIMS,
                                   preferred_element_type=F32)

        state = (m_ref.at[h], l_ref.at[h], acc_ref.at[h])
        _softmax_init(jnp.where(causal, logits(i), NEG), v_ref[_tile_rows(i), :], *state)

        def past_block(j, carry):
            _softmax_update(logits(j), v_ref[_tile_rows(j), :], *state)
            return carry

        lax.fori_loop(0, i, past_block, 0)

    out = jnp.where(lane < V_HEAD, acc_ref[0] / l_ref[0], acc_ref[1] / l_ref[1])
    o_ref[...] = out.astype(o_ref.dtype)


def _mla(qm, km, vb):
    b, seq, _ = qm.shape
    nq = seq // ATTN_TILE
    full = lambda bi, hp, i: (bi, 0, hp)
    tile = lambda bi, hp, i: (bi, i, hp)
    return pl.pallas_call(
        _mla_kernel,
        grid=(b, HEAD_PAIRS, nq),
        in_specs=[
            pl.BlockSpec((None, ATTN_TILE, 2 * MLA_HEAD_PAD), tile),
            pl.BlockSpec((None, seq, 2 * MLA_HEAD_PAD), full),
            pl.BlockSpec((None, seq, LANES), full),
        ],
        out_specs=pl.BlockSpec((None, ATTN_TILE, LANES), tile),
        out_shape=jax.ShapeDtypeStruct((b, seq, B_WIDTH), BF16),
        scratch_shapes=[
            pltpu.VMEM((2, ATTN_TILE, 1), F32),
            pltpu.VMEM((2, ATTN_TILE, 1), F32),
            pltpu.VMEM((2, ATTN_TILE, LANES), F32),
        ],
        compiler_params=pltpu.CompilerParams(
            dimension_semantics=("arbitrary", "arbitrary", "arbitrary"),
            vmem_limit_bytes=VMEM_LIMIT),
        name="mla_attention",
    )(qm, km, vb)


def _out_kernel(x_ref, oa_ref, ob_ref, gattn_ref, wg_ref, wpa_ref, wpb_ref, wout_ref,
                gmlp_ref, wup_ref, wdn_ref, gfin_ref, o_ref):
    x = x_ref[...]
    n = _rms(x, gattn_ref[...]).astype(BF16)
    gates = jnp.dot(n, wg_ref[...], preferred_element_type=F32)
    pa = jnp.dot(oa_ref[...], wpa_ref[...], preferred_element_type=F32)
    pb = jnp.dot(ob_ref[...], wpb_ref[...], preferred_element_type=F32)
    merged = (jax.nn.sigmoid(gates[:, :D_MODEL]) * pa
              + jax.nn.sigmoid(gates[:, D_MODEL:]) * pb).astype(BF16)
    h = x + jnp.dot(merged, wout_ref[...], preferred_element_type=F32)
    m = _rms(h, gmlp_ref[...]).astype(BF16)
    for c in range(D_FF // FF_CHUNK):
        cols = slice(c * FF_CHUNK, (c + 1) * FF_CHUNK)
        up = jnp.dot(m, wup_ref[:, cols], preferred_element_type=F32)
        act = jnp.square(jnp.maximum(up, 0.0)).astype(BF16)
        h = h + jnp.dot(act, wdn_ref[cols, :], preferred_element_type=F32)
    o_ref[...] = _rms(h, gfin_ref[...])


def _out_block(x2, oa, ob, g_attn, wg, wpa, wpb, wout, g_mlp, wup, wdn, g_fin):
    n_rows = x2.shape[0]
    tm = ROW_TILE
    row = lambda i: (i, 0)
    return pl.pallas_call(
        _out_kernel,
        grid=(n_rows // tm,),
        in_specs=[
            pl.BlockSpec((tm, D_MODEL), row),
            pl.BlockSpec((tm, A_WIDTH), row), pl.BlockSpec((tm, B_WIDTH), row),
            _const_spec(g_attn.shape), _const_spec(wg.shape), _const_spec(wpa.shape),
            _const_spec(wpb.shape), _const_spec(wout.shape), _const_spec(g_mlp.shape),
            _const_spec(wup.shape), _const_spec(wdn.shape), _const_spec(g_fin.shape),
        ],
        out_specs=pl.BlockSpec((tm, D_MODEL), row),
        out_shape=jax.ShapeDtypeStruct((n_rows, D_MODEL), F32),
        compiler_params=pltpu.CompilerParams(
            dimension_semantics=("arbitrary",), vmem_limit_bytes=VMEM_LIMIT),
        name="out_mlp",
    )(x2, oa, ob, g_attn, wg, wpa, wpb, wout, g_mlp, wup, wdn, g_fin)


def _rope_tables(seq):
    half = QK_ROPE // 2
    inv_freq = ROPE_THETA ** (-jnp.arange(half, dtype=F32) / half)
    ang = jnp.arange(seq).astype(F32)[:, None] * inv_freq[None, :]
    cos2 = jnp.tile(jnp.cos(ang), (1, 2))
    sin2 = jnp.tile(jnp.sin(ang), (1, 2))
    pad = jnp.zeros((seq, MLA_HEAD_PAD - QK_NOPE - QK_ROPE), F32)
    cos_t = jnp.concatenate([jnp.ones((seq, QK_NOPE), F32), cos2, pad], axis=1)
    sin_t = jnp.concatenate([jnp.zeros((seq, QK_NOPE), F32), sin2, pad], axis=1)
    return cos_t, sin_t


def _rotate_half_cols(w):
    half = QK_ROPE // 2
    return jnp.concatenate([-w[..., half:], w[..., :half]], axis=-1)


def _pad_heads(w_nope, w_rope):
    k, h, _ = w_nope.shape
    if w_rope is None:
        w_rope = jnp.zeros((k, h, QK_ROPE), w_nope.dtype)
    pad = jnp.zeros((k, h, MLA_HEAD_PAD - QK_NOPE - QK_ROPE), w_nope.dtype)
    return jnp.concatenate([w_nope, w_rope, pad], axis=-1).reshape(k, h * MLA_HEAD_PAD)


def kernel(x, w_in, rel_bias, mla_q_norm, w_uq, mla_kv_norm, w_uk, w_uv, w_proj_a, w_proj_b,
           w_out, norm_attn, norm_mlp, w_mlp_up, w_mlp_down, norm_final):
    b, seq, d = x.shape
    assert d == D_MODEL and seq % ROW_TILE == 0 and seq // MOBA_BLOCK <= LANES
    assert w_in.shape[0] == 1, "single-layer block"
    x2 = x.reshape(b * seq, d)

    w = w_in[0]
    o_cq = 3 * A_WIDTH
    o_ckv = o_cq + Q_LORA
    o_kr = o_ckv + KV_LORA
    o_g = o_kr + QK_ROPE
    wqkv = w[:, :o_cq].astype(BF16)
    w_kr = w[:, o_kr:o_g]
    lane_pad = lambda a: jnp.pad(a, ((0, 0), (QK_NOPE, MLA_HEAD_PAD - QK_NOPE - QK_ROPE)))
    wc = jnp.concatenate(
        [w[:, o_cq:o_kr], lane_pad(w_kr), lane_pad(_rotate_half_cols(w_kr))], axis=1).astype(BF16)
    wg = w[:, o_g:].astype(BF16)

    uq = w_uq[0].reshape(Q_LORA, B_HEADS, QK_NOPE + QK_ROPE)
    uq_nope, uq_rope = uq[..., :QK_NOPE], uq[..., QK_NOPE:]
    wuq = _pad_heads(uq_nope, uq_rope).astype(BF16)
    wuqr = _pad_heads(jnp.zeros_like(uq_nope), _rotate_half_cols(uq_rope)).astype(BF16)
    wuk = _pad_heads(w_uk[0].reshape(KV_LORA, B_HEADS, QK_NOPE), None).astype(BF16)
    wuv = w_uv[0].astype(BF16)

    cos_t, sin_t = _rope_tables(seq)
    q_scale = (QK_NOPE + QK_ROPE) ** -0.5
    row2 = lambda a: a.reshape(1, -1)

    qa, ka, va, kmean, qm, km, vb = _inproj(
        x2, row2(norm_attn[0]), wqkv, wc, row2(mla_q_norm[0]), row2(mla_kv_norm[0]),
        wuq, wuqr, wuk, wuv, cos_t * q_scale, sin_t * q_scale, cos_t, sin_t, seq)

    as_seq = lambda a: a.reshape(b, seq, a.shape[-1])
    bias = _bias_tiles(rel_bias)
    oa = _moba(as_seq(qa), as_seq(ka), as_seq(va),
               kmean.reshape(b, seq // MOBA_BLOCK, A_WIDTH), bias)
    ob = _mla(as_seq(qm), as_seq(km), as_seq(vb))

    out = _out_block(
        x2, oa.reshape(b * seq, A_WIDTH), ob.reshape(b * seq, B_WIDTH), row2(norm_attn[0]),
        wg, w_proj_a[0].astype(BF16), w_proj_b[0].astype(BF16), w_out[0].astype(BF16),
        row2(norm_mlp[0]), w_mlp_up[0].astype(BF16), w_mlp_down[0].astype(BF16),
        row2(norm_final))
    return out.reshape(b, seq, d)
```

```python
import functools
import math

import jax
import jax.numpy as jnp
from jax import lax
from jax.experimental import pallas as pl
from jax.experimental.pallas import tpu as pltpu

D_MODEL = 1024
A_HEADS = 8
A_HEAD_DIM = 64
A_WIDTH = A_HEADS * A_HEAD_DIM
MOBA_BLOCK = 256
MOBA_TOPK = 3
REL_BUCKETS = 32
REL_MAX_DIST = 128
B_HEADS = 8
QK_NOPE = 64
QK_ROPE = 32
V_HEAD = 64
B_WIDTH = B_HEADS * V_HEAD
Q_LORA = 384
KV_LORA = 256
ROPE_THETA = 10000.0
D_FF = 4 * D_MODEL
EPS = 1e-6
NEG = -1e30
LOG2E = math.log2(math.e)

LANES = 128
MLA_HEAD_PAD = 128
ATTN_TILE = 256
HEAD_GROUP = 4
SUM_ROWS = 16
ROW_TILE = 512
FF_CHUNK = 1024
VMEM_LIMIT = 56 * 1024 * 1024

F32 = jnp.float32
BF16 = jnp.bfloat16
NT_DIMS = (((1,), (1,)), ((), ()))


def _rms(xf, g):
    y = xf * lax.rsqrt(jnp.mean(xf * xf, axis=-1, keepdims=True) + EPS)
    return y * g


def _const_spec(shape):
    zeros = (0,) * len(shape)
    return pl.BlockSpec(shape, lambda *_: zeros, pipeline_mode=pl.Buffered(1))


def _bias_kernel(rel_ref, o_ref):
    h = pl.program_id(0)
    shape = (MOBA_BLOCK, 2 * MOBA_BLOCK)
    r = lax.broadcasted_iota(jnp.int32, shape, 0)
    c = lax.broadcasted_iota(jnp.int32, shape, 1)
    d = c - r
    dist = jnp.maximum(d, 0)
    max_exact = REL_BUCKETS // 2
    df = jnp.maximum(dist, 1).astype(F32)
    large = max_exact + (jnp.log(df / max_exact) / math.log(REL_MAX_DIST / max_exact)
                         * (REL_BUCKETS - max_exact)).astype(jnp.int32)
    large = jnp.minimum(large, REL_BUCKETS - 1)
    bucket = jnp.where(dist < max_exact, dist, large)
    val = jnp.zeros(shape, F32)
    for b in range(REL_BUCKETS):
        val = jnp.where(bucket == b, rel_ref[b, h], val)
    o_ref[...] = jnp.where(d >= 0, (val - rel_ref[REL_BUCKETS - 1, h]) * LOG2E, NEG)


def _bias_tiles(rel_bias):
    return pl.pallas_call(
        _bias_kernel,
        grid=(A_HEADS,),
        in_specs=[pl.BlockSpec(memory_space=pltpu.SMEM)],
        out_specs=pl.BlockSpec((None, MOBA_BLOCK, 2 * MOBA_BLOCK), lambda h: (h, 0, 0)),
        out_shape=jax.ShapeDtypeStruct((A_HEADS, MOBA_BLOCK, 2 * MOBA_BLOCK), F32),
        name="moba_bias_tiles",
    )(rel_bias)


def _inproj_kernel(x_ref, g_ref, wk_ref, wqvt_ref, wc_ref, qn_ref, kvn_ref, wuqt_ref, wuqrt_ref,
                   wuk_ref, wuvt_ref, cqt_ref, sqt_ref, ck_ref, sk_ref,
                   qat_ref, ka_ref, vat_ref, kmean_ref, qmt_ref, km_ref, vbt_ref):
    n = _rms(x_ref[...], g_ref[...]).astype(BF16)
    k = jnp.dot(n, wk_ref[...], preferred_element_type=F32)
    ka_ref[...] = k.astype(BF16)
    nblk = ROW_TILE // ATTN_TILE
    kmean_ref[0] = jnp.sum(k.reshape(nblk, MOBA_BLOCK, A_WIDTH), axis=1) * (1.0 / MOBA_BLOCK)

    c = jnp.dot(n, wc_ref[...], preferred_element_type=F32)
    cq = _rms(c[:, :Q_LORA], qn_ref[...]).astype(BF16)
    ckv = _rms(c[:, Q_LORA:Q_LORA + KV_LORA], kvn_ref[...]).astype(BF16)
    kr = c[:, Q_LORA + KV_LORA:Q_LORA + KV_LORA + LANES]
    krr = c[:, Q_LORA + KV_LORA + LANES:]
    for t in range(nblk):
        rows = slice(t * ATTN_TILE, (t + 1) * ATTN_TILE)
        qv_t = lax.dot_general(wqvt_ref[...], n[rows], NT_DIMS,
                               preferred_element_type=F32)
        qat_ref[t] = (qv_t[:A_WIDTH] * (A_HEAD_DIM ** -0.5 * LOG2E)).astype(BF16)
        vat_ref[t] = qv_t[A_WIDTH:].astype(BF16)
        vbt_ref[t] = lax.dot_general(wuvt_ref[...], ckv[rows], NT_DIMS,
                                     preferred_element_type=F32).astype(BF16)
        q_t = lax.dot_general(wuqt_ref[...], cq[rows], NT_DIMS,
                              preferred_element_type=F32)
        qr_t = lax.dot_general(wuqrt_ref[...], cq[rows], NT_DIMS, preferred_element_type=F32)
        cq_t, sq_t = cqt_ref[:, rows], sqt_ref[:, rows]
        for h in range(B_HEADS):
            sl = slice(h * MLA_HEAD_PAD, (h + 1) * MLA_HEAD_PAD)
            qmt_ref[t, sl, :] = (q_t[sl] * cq_t + qr_t[sl] * sq_t).astype(BF16)
    kn = jnp.dot(ckv, wuk_ref[...], preferred_element_type=F32)
    k_rope = kr * ck_ref[...] + krr * sk_ref[...]
    for h in range(B_HEADS):
        sl = slice(h * MLA_HEAD_PAD, (h + 1) * MLA_HEAD_PAD)
        km_ref[:, sl] = (kn[:, sl] + k_rope).astype(BF16)


def _inproj(x2, g_attn, wk, wqvt, wc, qn, kvn, wuqt, wuqrt, wuk, wuvt, cq_tt, sq_tt, ck_t, sk_t,
            batch, seq):
    n_rows = x2.shape[0]
    tm = ROW_TILE
    steps = n_rows // tm
    per_seq = seq // tm
    nblk = tm // ATTN_TILE
    row = lambda i: (i, 0)
    tab = lambda i: (i % per_seq, 0)
    tab_t = lambda i: (0, i % per_seq)
    tile_map = lambda i: (i // per_seq, i % per_seq, 0, 0)
    bf = lambda w: jax.ShapeDtypeStruct((n_rows, w), BF16)
    tiles = lambda w: jax.ShapeDtypeStruct((batch, seq // ATTN_TILE, w, ATTN_TILE), BF16)
    tile_spec = lambda w: pl.BlockSpec((None, nblk, w, ATTN_TILE), tile_map)
    return pl.pallas_call(
        _inproj_kernel,
        grid=(steps,),
        in_specs=[
            pl.BlockSpec((tm, D_MODEL), row),
            _const_spec(g_attn.shape), _const_spec(wk.shape), _const_spec(wqvt.shape),
            _const_spec(wc.shape), _const_spec(qn.shape), _const_spec(kvn.shape),
            _const_spec(wuqt.shape), _const_spec(wuqrt.shape), _const_spec(wuk.shape),
            _const_spec(wuvt.shape),
            pl.BlockSpec((LANES, tm), tab_t), pl.BlockSpec((LANES, tm), tab_t),
            pl.BlockSpec((tm, LANES), tab), pl.BlockSpec((tm, LANES), tab),
        ],
        out_specs=[
            tile_spec(A_WIDTH), pl.BlockSpec((tm, A_WIDTH), row), tile_spec(A_WIDTH),
            pl.BlockSpec((1, nblk, A_WIDTH), lambda i: (i, 0, 0)),
            tile_spec(B_HEADS * MLA_HEAD_PAD),
            pl.BlockSpec((tm, B_HEADS * MLA_HEAD_PAD), row),
            tile_spec(B_WIDTH),
        ],
        out_shape=[
            tiles(A_WIDTH), bf(A_WIDTH), tiles(A_WIDTH),
            jax.ShapeDtypeStruct((steps, nblk, A_WIDTH), F32),
            tiles(B_HEADS * MLA_HEAD_PAD), bf(B_HEADS * MLA_HEAD_PAD), tiles(B_WIDTH),
        ],
        compiler_params=pltpu.CompilerParams(
            dimension_semantics=("arbitrary",), vmem_limit_bytes=VMEM_LIMIT),
        name="in_proj",
    )(x2, g_attn, wk, wqvt, wc, qn, kvn, wuqt, wuqrt, wuk, wuvt, cq_tt, sq_tt, ck_t, sk_t)


def _sum_rows():
    r = lax.broadcasted_iota(jnp.int32, (SUM_ROWS, ATTN_TILE), 0)
    return (r == 0).astype(BF16)


def _softmax_step(s_t, v_t, m_ref, acc_ref, first):
    m_tile = jnp.max(s_t, axis=0, keepdims=True)
    v_aug = jnp.concatenate([v_t, _sum_rows()], axis=0)
    if first:
        m_new = m_tile
    else:
        m_old = m_ref[...]
        m_new = jnp.maximum(m_old, m_tile)
    p_t = jnp.exp2(s_t - m_new).astype(BF16)
    pv = jnp.dot(v_aug, p_t, preferred_element_type=F32)
    if first:
        acc_ref[...] = pv
    else:
        acc_ref[...] = jnp.exp2(m_old - m_new) * acc_ref[...] + pv
    m_ref[...] = m_new


def _attend(scores, values, m_ref, acc_ref, first):
    s = [scores(h) for h in range(HEAD_GROUP)]
    for h in range(HEAD_GROUP):
        _softmax_step(s[h], values(h), m_ref.at[h], acc_ref.at[h], first)


def _write_heads(acc_ref, o_ref, head_dim):
    outs = []
    for h in range(HEAD_GROUP):
        acc = acc_ref[h]
        outs.append(acc[:head_dim] / acc[head_dim:head_dim + 1])
    o_ref[...] = jnp.concatenate(outs, axis=0).T.astype(o_ref.dtype)


def _tile_rows(j):
    return pl.ds(pl.multiple_of(j * ATTN_TILE, ATTN_TILE), ATTN_TILE)


def _moba_kernel(qt_ref, k_ref, vt_ref, kmean_ref, bias_ref, o_ref,
                 kaug_ref, qaug_ref, m_ref, acc_ref, *, seq):
    i = pl.program_id(2)
    nblk = seq // MOBA_BLOCK
    pairs = HEAD_GROUP // 2

    @pl.when(i == 0)
    def _():
        blk = lax.broadcasted_iota(jnp.int32, (seq, LANES), 0) // MOBA_BLOCK
        lane = lax.broadcasted_iota(jnp.int32, (seq, LANES), 1)
        onehot = (blk == lane).astype(BF16)
        for pr in range(pairs):
            kaug_ref[pr, :, :LANES] = k_ref[:, pr * LANES:(pr + 1) * LANES]
            kaug_ref[pr, :, LANES:] = onehot

    feat = lax.broadcasted_iota(jnp.int32, (LANES, ATTN_TILE), 0)
    row = lax.broadcasted_iota(jnp.int32, (nblk, ATTN_TILE), 0)
    past = row < i
    for h in range(HEAD_GROUP):
        pr, sub = divmod(h, 2)
        q_t = qt_ref[pr * LANES:(pr + 1) * LANES, :]
        in_head = (feat >= sub * A_HEAD_DIM) & (feat < (sub + 1) * A_HEAD_DIM)
        qh_t = jnp.where(in_head, q_t, jnp.zeros_like(q_t))
        kmean = kmean_ref[:, pr * LANES:(pr + 1) * LANES].astype(BF16)
        gate = jnp.dot(kmean, qh_t, preferred_element_type=F32)
        gate = jnp.where(past, gate, -jnp.inf)
        rank = jnp.zeros((nblk, ATTN_TILE), F32)
        for c in range(nblk):
            gc = gate[c:c + 1, :]
            beats = (gc > gate) | ((gc == gate) & (c < row))
            rank = rank + beats.astype(F32)
        keep = (past & (rank < MOBA_TOPK)) | (row == i)
        mask_t = jnp.where(keep, 0.0, NEG).astype(BF16)
        qaug_ref[h] = jnp.concatenate(
            [qh_t, mask_t, jnp.zeros((LANES - nblk, ATTN_TILE), BF16)], axis=0)

    def scores(j):
        return lambda h: jnp.dot(kaug_ref[h // 2, _tile_rows(j), :], qaug_ref[h],
                                 preferred_element_type=F32)

    def values(j):
        return lambda h: vt_ref[j, h * A_HEAD_DIM:(h + 1) * A_HEAD_DIM, :]

    own = scores(i)
    _attend(lambda h: own(h) + bias_ref[h, :, :MOBA_BLOCK], values(i), m_ref, acc_ref, True)

    @pl.when(i >= 1)
    def _():
        prev = scores(i - 1)
        _attend(lambda h: prev(h) + bias_ref[h, :, MOBA_BLOCK:], values(i - 1),
                m_ref, acc_ref, False)

    def far_block(j, carry):
        _attend(scores(j), values(j), m_ref, acc_ref, False)
        return carry

    lax.fori_loop(0, jnp.maximum(i - 1, 0), far_block, 0)
    _write_heads(acc_ref, o_ref, A_HEAD_DIM)


def _moba(qat, ka, vat, kmean, bias):
    b, seq, _ = ka.shape
    nq = seq // ATTN_TILE
    nblk = seq // MOBA_BLOCK
    gw = HEAD_GROUP * A_HEAD_DIM
    return pl.pallas_call(
        functools.partial(_moba_kernel, seq=seq),
        grid=(b, A_HEADS // HEAD_GROUP, nq),
        in_specs=[
            pl.BlockSpec((None, None, gw, ATTN_TILE), lambda bi, g, i: (bi, i, g, 0)),
            pl.BlockSpec((None, seq, gw), lambda bi, g, i: (bi, 0, g)),
            pl.BlockSpec((None, nblk, gw, MOBA_BLOCK), lambda bi, g, i: (bi, 0, g, 0)),
            pl.BlockSpec((None, nblk, gw), lambda bi, g, i: (bi, 0, g)),
            pl.BlockSpec((HEAD_GROUP, MOBA_BLOCK, 2 * MOBA_BLOCK), lambda bi, g, i: (g, 0, 0)),
        ],
        out_specs=pl.BlockSpec((None, ATTN_TILE, gw), lambda bi, g, i: (bi, i, g)),
        out_shape=jax.ShapeDtypeStruct((b, seq, A_WIDTH), BF16),
        scratch_shapes=[
            pltpu.VMEM((HEAD_GROUP // 2, seq, 2 * LANES), BF16),
            pltpu.VMEM((HEAD_GROUP, 2 * LANES, ATTN_TILE), BF16),
            pltpu.VMEM((HEAD_GROUP, 1, ATTN_TILE), F32),
            pltpu.VMEM((HEAD_GROUP, A_HEAD_DIM + SUM_ROWS, ATTN_TILE), F32),
        ],
        compiler_params=pltpu.CompilerParams(
            dimension_semantics=("arbitrary", "arbitrary", "arbitrary"),
            vmem_limit_bytes=VMEM_LIMIT),
        name="moba_attention",
    )(qat, ka, vat, kmean, bias)


def _mla_kernel(qt_ref, k_ref, vt_ref, o_ref, m_ref, acc_ref):
    i = pl.program_id(2)
    key = lax.broadcasted_iota(jnp.int32, (ATTN_TILE, ATTN_TILE), 0)
    qry = lax.broadcasted_iota(jnp.int32, (ATTN_TILE, ATTN_TILE), 1)
    causal = key <= qry

    def scores(j):
        def one(h):
            sl = slice(h * MLA_HEAD_PAD, (h + 1) * MLA_HEAD_PAD)
            return jnp.dot(k_ref[_tile_rows(j), sl], qt_ref[sl, :],
                           preferred_element_type=F32)
        return one

    def values(j):
        return lambda h: vt_ref[j, h * V_HEAD:(h + 1) * V_HEAD, :]

    own = scores(i)
    _attend(lambda h: jnp.where(causal, own(h), NEG), values(i), m_ref, acc_ref, True)

    def past_block(j, carry):
        _attend(scores(j), values(j), m_ref, acc_ref, False)
        return carry

    lax.fori_loop(0, i, past_block, 0)
    _write_heads(acc_ref, o_ref, V_HEAD)


def _mla(qmt, km, vbt):
    b, seq, _ = km.shape
    nq = seq // ATTN_TILE
    qw = HEAD_GROUP * MLA_HEAD_PAD
    vw = HEAD_GROUP * V_HEAD
    return pl.pallas_call(
        _mla_kernel,
        grid=(b, B_HEADS // HEAD_GROUP, nq),
        in_specs=[
            pl.BlockSpec((None, None, qw, ATTN_TILE), lambda bi, g, i: (bi, i, g, 0)),
            pl.BlockSpec((None, seq, qw), lambda bi, g, i: (bi, 0, g)),
            pl.BlockSpec((None, nq, vw, ATTN_TILE), lambda bi, g, i: (bi, 0, g, 0)),
        ],
        out_specs=pl.BlockSpec((None, ATTN_TILE, vw), lambda bi, g, i: (bi, i, g)),
        out_shape=jax.ShapeDtypeStruct((b, seq, B_WIDTH), BF16),
        scratch_shapes=[
            pltpu.VMEM((HEAD_GROUP, 1, ATTN_TILE), F32),
            pltpu.VMEM((HEAD_GROUP, V_HEAD + SUM_ROWS, ATTN_TILE), F32),
        ],
        compiler_params=pltpu.CompilerParams(
            dimension_semantics=("arbitrary", "arbitrary", "arbitrary"),
            vmem_limit_bytes=VMEM_LIMIT),
        name="mla_attention",
    )(qmt, km, vbt)


def _out_kernel(x_ref, oa_ref, ob_ref, gattn_ref, wg_ref, wpa_ref, wpb_ref, wout_ref,
                gmlp_ref, wup_ref, wdn_ref, gfin_ref, o_ref):
    x = x_ref[...]
    n = _rms(x, gattn_ref[...]).astype(BF16)
    gates = jnp.dot(n, wg_ref[...], preferred_element_type=F32)
    pa = jnp.dot(oa_ref[...], wpa_ref[...], preferred_element_type=F32)
    pb = jnp.dot(ob_ref[...], wpb_ref[...], preferred_element_type=F32)
    merged = (jax.nn.sigmoid(gates[:, :D_MODEL]) * pa
              + jax.nn.sigmoid(gates[:, D_MODEL:]) * pb).astype(BF16)
    h = x + jnp.dot(merged, wout_ref[...], preferred_element_type=F32)
    m = _rms(h, gmlp_ref[...]).astype(BF16)
    for c in range(D_FF // FF_CHUNK):
        cols = slice(c * FF_CHUNK, (c + 1) * FF_CHUNK)
        up = jnp.dot(m, wup_ref[:, cols], preferred_element_type=F32)
        act = jnp.square(jnp.maximum(up, 0.0)).astype(BF16)
        h = h + jnp.dot(act, wdn_ref[cols, :], preferred_element_type=F32)
    o_ref[...] = _rms(h, gfin_ref[...])


def _out_block(x2, oa, ob, g_attn, wg, wpa, wpb, wout, g_mlp, wup, wdn, g_fin):
    n_rows = x2.shape[0]
    tm = ROW_TILE
    row = lambda i: (i, 0)
    return pl.pallas_call(
        _out_kernel,
        grid=(n_rows // tm,),
        in_specs=[
            pl.BlockSpec((tm, D_MODEL), row),
            pl.BlockSpec((tm, A_WIDTH), row), pl.BlockSpec((tm, B_WIDTH), row),
            _const_spec(g_attn.shape), _const_spec(wg.shape), _const_spec(wpa.shape),
            _const_spec(wpb.shape), _const_spec(wout.shape), _const_spec(g_mlp.shape),
            _const_spec(wup.shape), _const_spec(wdn.shape), _const_spec(g_fin.shape),
        ],
        out_specs=pl.BlockSpec((tm, D_MODEL), row),
        out_shape=jax.ShapeDtypeStruct((n_rows, D_MODEL), F32),
        compiler_params=pltpu.CompilerParams(
            dimension_semantics=("arbitrary",), vmem_limit_bytes=VMEM_LIMIT),
        name="out_mlp",
    )(x2, oa, ob, g_attn, wg, wpa, wpb, wout, g_mlp, wup, wdn, g_fin)


def _rope_tables(seq):
    half = QK_ROPE // 2
    inv_freq = ROPE_THETA ** (-jnp.arange(half, dtype=F32) / half)
    ang = jnp.arange(seq).astype(F32)[:, None] * inv_freq[None, :]
    cos2 = jnp.tile(jnp.cos(ang), (1, 2))
    sin2 = jnp.tile(jnp.sin(ang), (1, 2))
    pad = jnp.zeros((seq, MLA_HEAD_PAD - QK_NOPE - QK_ROPE), F32)
    cos_t = jnp.concatenate([jnp.ones((seq, QK_NOPE), F32), cos2, pad], axis=1)
    sin_t = jnp.concatenate([jnp.zeros((seq, QK_NOPE), F32), sin2, pad], axis=1)
    return cos_t, sin_t


def _rotate_half_cols(w):
    half = QK_ROPE // 2
    return jnp.concatenate([-w[..., half:], w[..., :half]], axis=-1)


def _pad_heads(w_nope, w_rope):
    k, h, _ = w_nope.shape
    if w_rope is None:
        w_rope = jnp.zeros((k, h, QK_ROPE), w_nope.dtype)
    pad = jnp.zeros((k, h, MLA_HEAD_PAD - QK_NOPE - QK_ROPE), w_nope.dtype)
    return jnp.concatenate([w_nope, w_rope, pad], axis=-1).reshape(k, h * MLA_HEAD_PAD)


def kernel(x, w_in, rel_bias, mla_q_norm, w_uq, mla_kv_norm, w_uk, w_uv, w_proj_a, w_proj_b,
           w_out, norm_attn, norm_mlp, w_mlp_up, w_mlp_down, norm_final):
    b, seq, d = x.shape
    assert d == D_MODEL and seq % ROW_TILE == 0 and seq // MOBA_BLOCK <= LANES
    assert w_in.shape[0] == 1, "single-layer block"
    x2 = x.reshape(b * seq, d)

    w = w_in[0]
    o_k = A_WIDTH
    o_v = 2 * A_WIDTH
    o_cq = 3 * A_WIDTH
    o_ckv = o_cq + Q_LORA
    o_kr = o_ckv + KV_LORA
    o_g = o_kr + QK_ROPE
    wk = w[:, o_k:o_v].astype(BF16)
    wqvt = jnp.concatenate([w[:, :o_k], w[:, o_v:o_cq]], axis=1).T.astype(BF16)
    w_kr = w[:, o_kr:o_g]
    lane_pad = lambda a: jnp.pad(a, ((0, 0), (QK_NOPE, MLA_HEAD_PAD - QK_NOPE - QK_ROPE)))
    wc = jnp.concatenate(
        [w[:, o_cq:o_kr], lane_pad(w_kr), lane_pad(_rotate_half_cols(w_kr))], axis=1).astype(BF16)
    wg = w[:, o_g:].astype(BF16)

    uq = w_uq[0].reshape(Q_LORA, B_HEADS, QK_NOPE + QK_ROPE)
    uq_nope, uq_rope = uq[..., :QK_NOPE], uq[..., QK_NOPE:]
    wuqt = _pad_heads(uq_nope, uq_rope).T.astype(BF16)
    wuqrt = _pad_heads(jnp.zeros_like(uq_nope), _rotate_half_cols(uq_rope)).T.astype(BF16)
    wuk = _pad_heads(w_uk[0].reshape(KV_LORA, B_HEADS, QK_NOPE), None).astype(BF16)
    wuvt = w_uv[0].T.astype(BF16)

    cos_t, sin_t = _rope_tables(seq)
    q_scale = (QK_NOPE + QK_ROPE) ** -0.5 * LOG2E
    row2 = lambda a: a.reshape(1, -1)

    qat, ka, vat, kmean, qmt, km, vbt = _inproj(
        x2, row2(norm_attn[0]), wk, wqvt, wc, row2(mla_q_norm[0]), row2(mla_kv_norm[0]),
        wuqt, wuqrt, wuk, wuvt, (cos_t * q_scale).T, (sin_t * q_scale).T, cos_t, sin_t, b, seq)

    as_seq = lambda a: a.reshape(b, seq, a.shape[-1])
    bias = _bias_tiles(rel_bias)
    oa = _moba(qat, as_seq(ka), vat, kmean.reshape(b, seq // MOBA_BLOCK, A_WIDTH), bias)
    ob = _mla(qmt, as_seq(km), vbt)

    out = _out_block(
        x2, oa.reshape(b * seq, A_WIDTH), ob.reshape(b * seq, B_WIDTH), row2(norm_attn[0]),
        wg, w_proj_a[0].astype(BF16), w_proj_b[0].astype(BF16), w_out[0].astype(BF16),
        row2(norm_mlp[0]), w_mlp_up[0].astype(BF16), w_mlp_down[0].astype(BF16),
        row2(norm_final))
    return out.reshape(b, seq, d)
```

```python
import functools
import math

import jax
import jax.numpy as jnp
from jax import lax
from jax.experimental import pallas as pl
from jax.experimental.pallas import tpu as pltpu

D_MODEL = 1024
A_HEADS = 8
A_HEAD_DIM = 64
A_WIDTH = A_HEADS * A_HEAD_DIM
MOBA_BLOCK = 256
MOBA_TOPK = 3
REL_BUCKETS = 32
REL_MAX_DIST = 128
B_HEADS = 8
QK_NOPE = 64
QK_ROPE = 32
V_HEAD = 64
B_WIDTH = B_HEADS * V_HEAD
Q_LORA = 384
KV_LORA = 256
ROPE_THETA = 10000.0
D_FF = 4 * D_MODEL
EPS = 1e-6
NEG = -1e30
LOG2E = math.log2(math.e)

LANES = 128
MLA_HEAD_PAD = 128
ATTN_TILE = 256
HEAD_GROUP = 4
SUM_ROWS = 16
ROW_TILE = 512
FF_CHUNK = 1024
VMEM_LIMIT = 56 * 1024 * 1024

F32 = jnp.float32
BF16 = jnp.bfloat16
NT_DIMS = (((1,), (1,)), ((), ()))


def _rms(xf, g):
    y = xf * lax.rsqrt(jnp.mean(xf * xf, axis=-1, keepdims=True) + EPS)
    return y * g


def _const_spec(shape):
    zeros = (0,) * len(shape)
    return pl.BlockSpec(shape, lambda *_: zeros, pipeline_mode=pl.Buffered(1))


def _bias_kernel(rel_ref, o_ref):
    h = pl.program_id(0)
    shape = (MOBA_BLOCK, 2 * MOBA_BLOCK)
    r = lax.broadcasted_iota(jnp.int32, shape, 0)
    c = lax.broadcasted_iota(jnp.int32, shape, 1)
    d = c - r
    dist = jnp.maximum(d, 0)
    max_exact = REL_BUCKETS // 2
    df = jnp.maximum(dist, 1).astype(F32)
    large = max_exact + (jnp.log(df / max_exact) / math.log(REL_MAX_DIST / max_exact)
                         * (REL_BUCKETS - max_exact)).astype(jnp.int32)
    large = jnp.minimum(large, REL_BUCKETS - 1)
    bucket = jnp.where(dist < max_exact, dist, large)
    val = jnp.zeros(shape, F32)
    for b in range(REL_BUCKETS):
        val = jnp.where(bucket == b, rel_ref[b, h], val)
    o_ref[...] = jnp.where(d >= 0, (val - rel_ref[REL_BUCKETS - 1, h]) * LOG2E, NEG)


def _bias_tiles(rel_bias):
    return pl.pallas_call(
        _bias_kernel,
        grid=(A_HEADS,),
        in_specs=[pl.BlockSpec(memory_space=pltpu.SMEM)],
        out_specs=pl.BlockSpec((None, MOBA_BLOCK, 2 * MOBA_BLOCK), lambda h: (h, 0, 0)),
        out_shape=jax.ShapeDtypeStruct((A_HEADS, MOBA_BLOCK, 2 * MOBA_BLOCK), F32),
        name="moba_bias_tiles",
    )(rel_bias)


def _inproj_kernel(x_ref, g_ref, wk_ref, wqvt_ref, wc_ref, qn_ref, kvn_ref, wuqt_ref, wuqrt_ref,
                   wuk_ref, wuvt_ref, cqt_ref, sqt_ref, ck_ref, sk_ref,
                   qat_ref, ka_ref, vat_ref, kmean_ref, qmt_ref, km_ref, vbt_ref):
    n = _rms(x_ref[...], g_ref[...]).astype(BF16)
    k = jnp.dot(n, wk_ref[...], preferred_element_type=F32)
    ka_ref[...] = k.astype(BF16)
    nblk = ROW_TILE // ATTN_TILE
    kmean_ref[0] = jnp.sum(k.reshape(nblk, MOBA_BLOCK, A_WIDTH), axis=1) * (1.0 / MOBA_BLOCK)

    c = jnp.dot(n, wc_ref[...], preferred_element_type=F32)
    cq = _rms(c[:, :Q_LORA], qn_ref[...]).astype(BF16)
    ckv = _rms(c[:, Q_LORA:Q_LORA + KV_LORA], kvn_ref[...]).astype(BF16)
    kr = c[:, Q_LORA + KV_LORA:Q_LORA + KV_LORA + LANES]
    krr = c[:, Q_LORA + KV_LORA + LANES:]
    for t in range(nblk):
        rows = slice(t * ATTN_TILE, (t + 1) * ATTN_TILE)
        qv_t = lax.dot_general(wqvt_ref[...], n[rows], NT_DIMS,
                               preferred_element_type=F32)
        qat_ref[t] = (qv_t[:A_WIDTH] * (A_HEAD_DIM ** -0.5 * LOG2E)).astype(BF16)
        vat_ref[t] = qv_t[A_WIDTH:].astype(BF16)
        vbt_ref[t] = lax.dot_general(wuvt_ref[...], ckv[rows], NT_DIMS,
                                     preferred_element_type=F32).astype(BF16)
        q_t = lax.dot_general(wuqt_ref[...], cq[rows], NT_DIMS,
                              preferred_element_type=F32)
        qr_t = lax.dot_general(wuqrt_ref[...], cq[rows], NT_DIMS, preferred_element_type=F32)
        cq_t, sq_t = cqt_ref[:, rows], sqt_ref[:, rows]
        for h in range(B_HEADS):
            sl = slice(h * MLA_HEAD_PAD, (h + 1) * MLA_HEAD_PAD)
            qmt_ref[t, sl, :] = (q_t[sl] * cq_t + qr_t[sl] * sq_t).astype(BF16)
    kn = jnp.dot(ckv, wuk_ref[...], preferred_element_type=F32)
    k_rope = kr * ck_ref[...] + krr * sk_ref[...]
    for h in range(B_HEADS):
        sl = slice(h * MLA_HEAD_PAD, (h + 1) * MLA_HEAD_PAD)
        km_ref[:, sl] = (kn[:, sl] + k_rope).astype(BF16)


def _inproj(x2, g_attn, wk, wqvt, wc, qn, kvn, wuqt, wuqrt, wuk, wuvt, cq_tt, sq_tt, ck_t, sk_t,
            batch, seq):
    n_rows = x2.shape[0]
    tm = ROW_TILE
    steps = n_rows // tm
    per_seq = seq // tm
    nblk = tm // ATTN_TILE
    row = lambda i: (i, 0)
    tab = lambda i: (i % per_seq, 0)
    tab_t = lambda i: (0, i % per_seq)
    tile_map = lambda i: (i // per_seq, i % per_seq, 0, 0)
    bf = lambda w: jax.ShapeDtypeStruct((n_rows, w), BF16)
    tiles = lambda w: jax.ShapeDtypeStruct((batch, seq // ATTN_TILE, w, ATTN_TILE), BF16)
    tile_spec = lambda w: pl.BlockSpec((None, nblk, w, ATTN_TILE), tile_map)
    return pl.pallas_call(
        _inproj_kernel,
        grid=(steps,),
        in_specs=[
            pl.BlockSpec((tm, D_MODEL), row),
            _const_spec(g_attn.shape), _const_spec(wk.shape), _const_spec(wqvt.shape),
            _const_spec(wc.shape), _const_spec(qn.shape), _const_spec(kvn.shape),
            _const_spec(wuqt.shape), _const_spec(wuqrt.shape), _const_spec(wuk.shape),
            _const_spec(wuvt.shape),
            pl.BlockSpec((LANES, tm), tab_t), pl.BlockSpec((LANES, tm), tab_t),
            pl.BlockSpec((tm, LANES), tab), pl.BlockSpec((tm, LANES), tab),
        ],
        out_specs=[
            tile_spec(A_WIDTH), pl.BlockSpec((tm, A_WIDTH), row), tile_spec(A_WIDTH),
            pl.BlockSpec((1, nblk, A_WIDTH), lambda i: (i, 0, 0)),
            tile_spec(B_HEADS * MLA_HEAD_PAD),
            pl.BlockSpec((tm, B_HEADS * MLA_HEAD_PAD), row),
            tile_spec(B_WIDTH),
        ],
        out_shape=[
            tiles(A_WIDTH), bf(A_WIDTH), tiles(A_WIDTH),
            jax.ShapeDtypeStruct((steps, nblk, A_WIDTH), F32),
            tiles(B_HEADS * MLA_HEAD_PAD), bf(B_HEADS * MLA_HEAD_PAD), tiles(B_WIDTH),
        ],
        compiler_params=pltpu.CompilerParams(
            dimension_semantics=("arbitrary",), vmem_limit_bytes=VMEM_LIMIT),
        name="in_proj",
    )(x2, g_attn, wk, wqvt, wc, qn, kvn, wuqt, wuqrt, wuk, wuvt, cq_tt, sq_tt, ck_t, sk_t)


def _sum_rows():
    r = lax.broadcasted_iota(jnp.int32, (SUM_ROWS, ATTN_TILE), 0)
    return (r == 0).astype(BF16)


def _softmax_init(m_ref, acc_ref):
    m_ref[...] = jnp.full(m_ref.shape, NEG, F32)
    acc_ref[...] = jnp.zeros(acc_ref.shape, F32)


def _softmax_step(s_ref, v_t, m_ref, acc_ref):
    m_old = m_ref[...]
    m_new = jnp.maximum(m_old, jnp.max(s_ref[...], axis=0, keepdims=True))
    v_aug = jnp.concatenate([v_t, _sum_rows()], axis=0)
    p_t = jnp.exp2(s_ref[...] - m_new).astype(BF16)
    pv = jnp.dot(v_aug, p_t, preferred_element_type=F32)
    acc_ref[...] = jnp.exp2(m_old - m_new) * acc_ref[...] + pv
    m_ref[...] = m_new


def _pipeline(n_rest, first_tile, rest_tile, score, attend, bufs):
    b0, b1 = bufs

    def pair(p, carry):
        score(rest_tile(2 * p), b1)
        attend(jnp.where(p == 0, first_tile, rest_tile(2 * p - 1)), b0)
        score(rest_tile(2 * p + 1), b0)
        attend(rest_tile(2 * p), b1)
        return carry

    odd = n_rest & 1
    lax.fori_loop(0, lax.shift_right_logical(n_rest, 1), pair, 0)
    in_b0 = jnp.where(n_rest < 2, first_tile, rest_tile(n_rest - odd - 1))

    @pl.when(odd == 1)
    def _():
        score(rest_tile(n_rest - 1), b1)
        attend(in_b0, b0)
        attend(rest_tile(n_rest - 1), b1)

    @pl.when(odd == 0)
    def _():
        attend(in_b0, b0)


def _write_heads(acc_ref, o_ref, head_dim):
    outs = []
    for h in range(HEAD_GROUP):
        acc = acc_ref[h]
        outs.append(acc[:head_dim] / acc[head_dim:head_dim + 1])
    o_ref[...] = jnp.concatenate(outs, axis=0).T.astype(o_ref.dtype)


def _tile_rows(j):
    return pl.ds(pl.multiple_of(j * ATTN_TILE, ATTN_TILE), ATTN_TILE)


def _moba_kernel(qt_ref, k_ref, vt_ref, kmean_ref, bias_ref, o_ref,
                 kaug_ref, qaug_ref, s0_ref, s1_ref, m_ref, acc_ref, *, seq):
    i = pl.program_id(2)
    nblk = seq // MOBA_BLOCK
    pairs = HEAD_GROUP // 2

    @pl.when(i == 0)
    def _():
        blk = lax.broadcasted_iota(jnp.int32, (seq, LANES), 0) // MOBA_BLOCK
        lane = lax.broadcasted_iota(jnp.int32, (seq, LANES), 1)
        onehot = (blk == lane).astype(BF16)
        for pr in range(pairs):
            kaug_ref[pr, :, :LANES] = k_ref[:, pr * LANES:(pr + 1) * LANES]
            kaug_ref[pr, :, LANES:] = onehot

    feat = lax.broadcasted_iota(jnp.int32, (LANES, ATTN_TILE), 0)
    row = lax.broadcasted_iota(jnp.int32, (nblk, ATTN_TILE), 0)
    past = row < i
    for h in range(HEAD_GROUP):
        pr, sub = divmod(h, 2)
        q_t = qt_ref[pr * LANES:(pr + 1) * LANES, :]
        in_head = (feat >= sub * A_HEAD_DIM) & (feat < (sub + 1) * A_HEAD_DIM)
        qh_t = jnp.where(in_head, q_t, jnp.zeros_like(q_t))
        kmean = kmean_ref[:, pr * LANES:(pr + 1) * LANES].astype(BF16)
        gate = jnp.dot(kmean, qh_t, preferred_element_type=F32)
        gate = jnp.where(past, gate, -jnp.inf)
        rank = jnp.zeros((nblk, ATTN_TILE), F32)
        for c in range(nblk):
            gc = gate[c:c + 1, :]
            beats = (gc > gate) | ((gc == gate) & (c < row))
            rank = rank + beats.astype(F32)
        keep = (past & (rank < MOBA_TOPK)) | (row == i)
        mask_t = jnp.where(keep, 0.0, NEG).astype(BF16)
        qaug_ref[h] = jnp.concatenate(
            [qh_t, mask_t, jnp.zeros((LANES - nblk, ATTN_TILE), BF16)], axis=0)

    def score(j, buf, bias_cols=None):
        for h in range(HEAD_GROUP):
            s_t = jnp.dot(kaug_ref[h // 2, _tile_rows(j), :], qaug_ref[h],
                          preferred_element_type=F32)
            if bias_cols is not None:
                s_t = s_t + bias_ref[h, :, bias_cols]
            buf[h] = s_t

    def attend(j, buf):
        for h in range(HEAD_GROUP):
            _softmax_step(buf.at[h], vt_ref[j, h * A_HEAD_DIM:(h + 1) * A_HEAD_DIM, :],
                          m_ref.at[h], acc_ref.at[h])

    _softmax_init(m_ref, acc_ref)
    score(i, s0_ref, slice(0, MOBA_BLOCK))

    @pl.when(i == 0)
    def _():
        attend(i, s0_ref)

    @pl.when(i >= 1)
    def _():
        score(i - 1, s1_ref, slice(MOBA_BLOCK, 2 * MOBA_BLOCK))
        attend(i, s0_ref)
        _pipeline(i - 1, i - 1, lambda k: k, score, attend, (s1_ref, s0_ref))

    _write_heads(acc_ref, o_ref, A_HEAD_DIM)


def _moba(qat, ka, vat, kmean, bias):
    b, seq, _ = ka.shape
    nq = seq // ATTN_TILE
    nblk = seq // MOBA_BLOCK
    gw = HEAD_GROUP * A_HEAD_DIM
    return pl.pallas_call(
        functools.partial(_moba_kernel, seq=seq),
        grid=(b, A_HEADS // HEAD_GROUP, nq),
        in_specs=[
            pl.BlockSpec((None, None, gw, ATTN_TILE), lambda bi, g, i: (bi, i, g, 0)),
            pl.BlockSpec((None, seq, gw), lambda bi, g, i: (bi, 0, g)),
            pl.BlockSpec((None, nblk, gw, MOBA_BLOCK), lambda bi, g, i: (bi, 0, g, 0)),
            pl.BlockSpec((None, nblk, gw), lambda bi, g, i: (bi, 0, g)),
            pl.BlockSpec((HEAD_GROUP, MOBA_BLOCK, 2 * MOBA_BLOCK), lambda bi, g, i: (g, 0, 0)),
        ],
        out_specs=pl.BlockSpec((None, ATTN_TILE, gw), lambda bi, g, i: (bi, i, g)),
        out_shape=jax.ShapeDtypeStruct((b, seq, A_WIDTH), BF16),
        scratch_shapes=[
            pltpu.VMEM((HEAD_GROUP // 2, seq, 2 * LANES), BF16),
            pltpu.VMEM((HEAD_GROUP, 2 * LANES, ATTN_TILE), BF16),
            pltpu.VMEM((HEAD_GROUP, ATTN_TILE, ATTN_TILE), F32),
            pltpu.VMEM((HEAD_GROUP, ATTN_TILE, ATTN_TILE), F32),
            pltpu.VMEM((HEAD_GROUP, 1, ATTN_TILE), F32),
            pltpu.VMEM((HEAD_GROUP, A_HEAD_DIM + SUM_ROWS, ATTN_TILE), F32),
        ],
        compiler_params=pltpu.CompilerParams(
            dimension_semantics=("arbitrary", "arbitrary", "arbitrary"),
            vmem_limit_bytes=VMEM_LIMIT),
        name="moba_attention",
    )(qat, ka, vat, kmean, bias)


def _mla_kernel(qt_ref, k_ref, vt_ref, o_ref, s0_ref, s1_ref, m_ref, acc_ref):
    i = pl.program_id(2)

    def score(j, buf, diagonal=False):
        for h in range(HEAD_GROUP):
            sl = slice(h * MLA_HEAD_PAD, (h + 1) * MLA_HEAD_PAD)
            s_t = jnp.dot(k_ref[_tile_rows(j), sl], qt_ref[sl, :],
                          preferred_element_type=F32)
            if diagonal:
                key = lax.broadcasted_iota(jnp.int32, (ATTN_TILE, ATTN_TILE), 0)
                qry = lax.broadcasted_iota(jnp.int32, (ATTN_TILE, ATTN_TILE), 1)
                s_t = jnp.where(key <= qry, s_t, NEG)
            buf[h] = s_t

    def attend(j, buf):
        for h in range(HEAD_GROUP):
            _softmax_step(buf.at[h], vt_ref[j, h * V_HEAD:(h + 1) * V_HEAD, :],
                          m_ref.at[h], acc_ref.at[h])

    _softmax_init(m_ref, acc_ref)
    score(i, s0_ref, diagonal=True)
    _pipeline(i, i, lambda k: k, score, attend, (s0_ref, s1_ref))
    _write_heads(acc_ref, o_ref, V_HEAD)


def _mla(qmt, km, vbt):
    b, seq, _ = km.shape
    nq = seq // ATTN_TILE
    qw = HEAD_GROUP * MLA_HEAD_PAD
    vw = HEAD_GROUP * V_HEAD
    return pl.pallas_call(
        _mla_kernel,
        grid=(b, B_HEADS // HEAD_GROUP, nq),
        in_specs=[
            pl.BlockSpec((None, None, qw, ATTN_TILE), lambda bi, g, i: (bi, i, g, 0)),
            pl.BlockSpec((None, seq, qw), lambda bi, g, i: (bi, 0, g)),
            pl.BlockSpec((None, nq, vw, ATTN_TILE), lambda bi, g, i: (bi, 0, g, 0)),
        ],
        out_specs=pl.BlockSpec((None, ATTN_TILE, vw), lambda bi, g, i: (bi, i, g)),
        out_shape=jax.ShapeDtypeStruct((b, seq, B_WIDTH), BF16),
        scratch_shapes=[
            pltpu.VMEM((HEAD_GROUP, ATTN_TILE, ATTN_TILE), F32),
            pltpu.VMEM((HEAD_GROUP, ATTN_TILE, ATTN_TILE), F32),
            pltpu.VMEM((HEAD_GROUP, 1, ATTN_TILE), F32),
            pltpu.VMEM((HEAD_GROUP, V_HEAD + SUM_ROWS, ATTN_TILE), F32),
        ],
        compiler_params=pltpu.CompilerParams(
            dimension_semantics=("arbitrary", "arbitrary", "arbitrary"),
            vmem_limit_bytes=VMEM_LIMIT),
        name="mla_attention",
    )(qmt, km, vbt)


def _out_kernel(x_ref, oa_ref, ob_ref, gattn_ref, wg_ref, wpa_ref, wpb_ref, wout_ref,
                gmlp_ref, wup_ref, wdn_ref, gfin_ref, o_ref):
    x = x_ref[...]
    n = _rms(x, gattn_ref[...]).astype(BF16)
    gates = jnp.dot(n, wg_ref[...], preferred_element_type=F32)
    pa = jnp.dot(oa_ref[...], wpa_ref[...], preferred_element_type=F32)
    pb = jnp.dot(ob_ref[...], wpb_ref[...], preferred_element_type=F32)
    merged = (jax.nn.sigmoid(gates[:, :D_MODEL]) * pa
              + jax.nn.sigmoid(gates[:, D_MODEL:]) * pb).astype(BF16)
    h = x + jnp.dot(merged, wout_ref[...], preferred_element_type=F32)
    m = _rms(h, gmlp_ref[...]).astype(BF16)
    for c in range(D_FF // FF_CHUNK):
        cols = slice(c * FF_CHUNK, (c + 1) * FF_CHUNK)
        up = jnp.dot(m, wup_ref[:, cols], preferred_element_type=F32)
        act = jnp.square(jnp.maximum(up, 0.0)).astype(BF16)
        h = h + jnp.dot(act, wdn_ref[cols, :], preferred_element_type=F32)
    o_ref[...] = _rms(h, gfin_ref[...])


def _out_block(x2, oa, ob, g_attn, wg, wpa, wpb, wout, g_mlp, wup, wdn, g_fin):
    n_rows = x2.shape[0]
    tm = ROW_TILE
    row = lambda i: (i, 0)
    return pl.pallas_call(
        _out_kernel,
        grid=(n_rows // tm,),
        in_specs=[
            pl.BlockSpec((tm, D_MODEL), row),
            pl.BlockSpec((tm, A_WIDTH), row), pl.BlockSpec((tm, B_WIDTH), row),
            _const_spec(g_attn.shape), _const_spec(wg.shape), _const_spec(wpa.shape),
            _const_spec(wpb.shape), _const_spec(wout.shape), _const_spec(g_mlp.shape),
            _const_spec(wup.shape), _const_spec(wdn.shape), _const_spec(g_fin.shape),
        ],
        out_specs=pl.BlockSpec((tm, D_MODEL), row),
        out_shape=jax.ShapeDtypeStruct((n_rows, D_MODEL), F32),
        compiler_params=pltpu.CompilerParams(
            dimension_semantics=("arbitrary",), vmem_limit_bytes=VMEM_LIMIT),
        name="out_mlp",
    )(x2, oa, ob, g_attn, wg, wpa, wpb, wout, g_mlp, wup, wdn, g_fin)


def _rope_tables(seq):
    half = QK_ROPE // 2
    inv_freq = ROPE_THETA ** (-jnp.arange(half, dtype=F32) / half)
    ang = jnp.arange(seq).astype(F32)[:, None] * inv_freq[None, :]
    cos2 = jnp.tile(jnp.cos(ang), (1, 2))
    sin2 = jnp.tile(jnp.sin(ang), (1, 2))
    pad = jnp.zeros((seq, MLA_HEAD_PAD - QK_NOPE - QK_ROPE), F32)
    cos_t = jnp.concatenate([jnp.ones((seq, QK_NOPE), F32), cos2, pad], axis=1)
    sin_t = jnp.concatenate([jnp.zeros((seq, QK_NOPE), F32), sin2, pad], axis=1)
    return cos_t, sin_t


def _rotate_half_cols(w):
    half = QK_ROPE // 2
    return jnp.concatenate([-w[..., half:], w[..., :half]], axis=-1)


def _pad_heads(w_nope, w_rope):
    k, h, _ = w_nope.shape
    if w_rope is None:
        w_rope = jnp.zeros((k, h, QK_ROPE), w_nope.dtype)
    pad = jnp.zeros((k, h, MLA_HEAD_PAD - QK_NOPE - QK_ROPE), w_nope.dtype)
    return jnp.concatenate([w_nope, w_rope, pad], axis=-1).reshape(k, h * MLA_HEAD_PAD)


def kernel(x, w_in, rel_bias, mla_q_norm, w_uq, mla_kv_norm, w_uk, w_uv, w_proj_a, w_proj_b,
           w_out, norm_attn, norm_mlp, w_mlp_up, w_mlp_down, norm_final):
    b, seq, d = x.shape
    assert d == D_MODEL and seq % ROW_TILE == 0 and seq // MOBA_BLOCK <= LANES
    assert w_in.shape[0] == 1, "single-layer block"
    x2 = x.reshape(b * seq, d)

    w = w_in[0]
    o_k = A_WIDTH
    o_v = 2 * A_WIDTH
    o_cq = 3 * A_WIDTH
    o_ckv = o_cq + Q_LORA
    o_kr = o_ckv + KV_LORA
    o_g = o_kr + QK_ROPE
    wk = w[:, o_k:o_v].astype(BF16)
    wqvt = jnp.concatenate([w[:, :o_k], w[:, o_v:o_cq]], axis=1).T.astype(BF16)
    w_kr = w[:, o_kr:o_g]
    lane_pad = lambda a: jnp.pad(a, ((0, 0), (QK_NOPE, MLA_HEAD_PAD - QK_NOPE - QK_ROPE)))
    wc = jnp.concatenate(
        [w[:, o_cq:o_kr], lane_pad(w_kr), lane_pad(_rotate_half_cols(w_kr))], axis=1).astype(BF16)
    wg = w[:, o_g:].astype(BF16)

    uq = w_uq[0].reshape(Q_LORA, B_HEADS, QK_NOPE + QK_ROPE)
    uq_nope, uq_rope = uq[..., :QK_NOPE], uq[..., QK_NOPE:]
    wuqt = _pad_heads(uq_nope, uq_rope).T.astype(BF16)
    wuqrt = _pad_heads(jnp.zeros_like(uq_nope), _rotate_half_cols(uq_rope)).T.astype(BF16)
    wuk = _pad_heads(w_uk[0].reshape(KV_LORA, B_HEADS, QK_NOPE), None).astype(BF16)
    wuvt = w_uv[0].T.astype(BF16)

    cos_t, sin_t = _rope_tables(seq)
    q_scale = (QK_NOPE + QK_ROPE) ** -0.5 * LOG2E
    row2 = lambda a: a.reshape(1, -1)

    qat, ka, vat, kmean, qmt, km, vbt = _inproj(
        x2, row2(norm_attn[0]), wk, wqvt, wc, row2(mla_q_norm[0]), row2(mla_kv_norm[0]),
        wuqt, wuqrt, wuk, wuvt, (cos_t * q_scale).T, (sin_t * q_scale).T, cos_t, sin_t, b, seq)

    as_seq = lambda a: a.reshape(b, seq, a.shape[-1])
    bias = _bias_tiles(rel_bias)
    oa = _moba(qat, as_seq(ka), vat, kmean.reshape(b, seq // MOBA_BLOCK, A_WIDTH), bias)
    ob = _mla(qmt, as_seq(km), vbt)

    out = _out_block(
        x2, oa.reshape(b * seq, A_WIDTH), ob.reshape(b * seq, B_WIDTH), row2(norm_attn[0]),
        wg, w_proj_a[0].astype(BF16), w_proj_b[0].astype(BF16), w_out[0].astype(BF16),
        row2(norm_mlp[0]), w_mlp_up[0].astype(BF16), w_mlp_down[0].astype(BF16),
        row2(norm_final))
    return out.reshape(b, seq, d)
```

```python
import functools
import math

import jax
import jax.numpy as jnp
from jax import lax
from jax.experimental import pallas as pl
from jax.experimental.pallas import tpu as pltpu

D_MODEL = 1024
A_HEADS = 8
A_HEAD_DIM = 64
A_WIDTH = A_HEADS * A_HEAD_DIM
MOBA_BLOCK = 256
MOBA_TOPK = 3
REL_BUCKETS = 32
REL_MAX_DIST = 128
B_HEADS = 8
QK_NOPE = 64
QK_ROPE = 32
V_HEAD = 64
B_WIDTH = B_HEADS * V_HEAD
Q_LORA = 384
KV_LORA = 256
ROPE_THETA = 10000.0
D_FF = 4 * D_MODEL
EPS = 1e-6
NEG = -1e30
LOG2E = math.log2(math.e)

LANES = 128
MLA_HEAD_PAD = 128
ATTN_TILE = 256
HEAD_GROUP = 4
SUM_ROWS = 16
ROW_TILE = 512
FF_CHUNK = 1024
VMEM_LIMIT = 56 * 1024 * 1024

F32 = jnp.float32
BF16 = jnp.bfloat16
NT_DIMS = (((1,), (1,)), ((), ()))


def _rms(xf, g):
    y = xf * lax.rsqrt(jnp.mean(xf * xf, axis=-1, keepdims=True) + EPS)
    return y * g


def _const_spec(shape):
    zeros = (0,) * len(shape)
    return pl.BlockSpec(shape, lambda *_: zeros, pipeline_mode=pl.Buffered(1))


def _bias_kernel(rel_ref, o_ref):
    h = pl.program_id(0)
    shape = (MOBA_BLOCK, 2 * MOBA_BLOCK)
    r = lax.broadcasted_iota(jnp.int32, shape, 0)
    c = lax.broadcasted_iota(jnp.int32, shape, 1)
    d = c - r
    dist = jnp.maximum(d, 0)
    max_exact = REL_BUCKETS // 2
    df = jnp.maximum(dist, 1).astype(F32)
    large = max_exact + (jnp.log(df / max_exact) / math.log(REL_MAX_DIST / max_exact)
                         * (REL_BUCKETS - max_exact)).astype(jnp.int32)
    large = jnp.minimum(large, REL_BUCKETS - 1)
    bucket = jnp.where(dist < max_exact, dist, large)
    val = jnp.zeros(shape, F32)
    for b in range(REL_BUCKETS):
        val = jnp.where(bucket == b, rel_ref[b, h], val)
    o_ref[...] = jnp.where(d >= 0, (val - rel_ref[REL_BUCKETS - 1, h]) * LOG2E, NEG)


def _bias_tiles(rel_bias):
    return pl.pallas_call(
        _bias_kernel,
        grid=(A_HEADS,),
        in_specs=[pl.BlockSpec(memory_space=pltpu.SMEM)],
        out_specs=pl.BlockSpec((None, MOBA_BLOCK, 2 * MOBA_BLOCK), lambda h: (h, 0, 0)),
        out_shape=jax.ShapeDtypeStruct((A_HEADS, MOBA_BLOCK, 2 * MOBA_BLOCK), F32),
        name="moba_bias_tiles",
    )(rel_bias)


def _inproj_kernel(x_ref, g_ref, wk_ref, wqvt_ref, wc_ref, qn_ref, kvn_ref, wuqt_ref, wuqrt_ref,
                   wuk_ref, wuvt_ref, cqt_ref, sqt_ref, ck_ref, sk_ref,
                   qat_ref, ka_ref, vat_ref, kmean_ref, qmt_ref, km_ref, vbt_ref):
    n = _rms(x_ref[...], g_ref[...]).astype(BF16)
    k = jnp.dot(n, wk_ref[...], preferred_element_type=F32)
    ka_ref[...] = k.astype(BF16)
    nblk = ROW_TILE // ATTN_TILE
    kmean_ref[0] = jnp.sum(k.reshape(nblk, MOBA_BLOCK, A_WIDTH), axis=1) * (1.0 / MOBA_BLOCK)

    c = jnp.dot(n, wc_ref[...], preferred_element_type=F32)
    cq = _rms(c[:, :Q_LORA], qn_ref[...]).astype(BF16)
    ckv = _rms(c[:, Q_LORA:Q_LORA + KV_LORA], kvn_ref[...]).astype(BF16)
    kr = c[:, Q_LORA + KV_LORA:Q_LORA + KV_LORA + LANES]
    krr = c[:, Q_LORA + KV_LORA + LANES:]
    for t in range(nblk):
        rows = slice(t * ATTN_TILE, (t + 1) * ATTN_TILE)
        qv_t = lax.dot_general(wqvt_ref[...], n[rows], NT_DIMS,
                               preferred_element_type=F32)
        qat_ref[t] = (qv_t[:A_WIDTH] * (A_HEAD_DIM ** -0.5 * LOG2E)).astype(BF16)
        vat_ref[t] = qv_t[A_WIDTH:].astype(BF16)
        vbt_ref[t] = lax.dot_general(wuvt_ref[...], ckv[rows], NT_DIMS,
                                     preferred_element_type=F32).astype(BF16)
        q_t = lax.dot_general(wuqt_ref[...], cq[rows], NT_DIMS,
                              preferred_element_type=F32)
        qr_t = lax.dot_general(wuqrt_ref[...], cq[rows], NT_DIMS, preferred_element_type=F32)
        cq_t, sq_t = cqt_ref[:, rows], sqt_ref[:, rows]
        for h in range(B_HEADS):
            sl = slice(h * MLA_HEAD_PAD, (h + 1) * MLA_HEAD_PAD)
            qmt_ref[t, sl, :] = (q_t[sl] * cq_t + qr_t[sl] * sq_t).astype(BF16)
    kn = jnp.dot(ckv, wuk_ref[...], preferred_element_type=F32)
    k_rope = kr * ck_ref[...] + krr * sk_ref[...]
    for h in range(B_HEADS):
        sl = slice(h * MLA_HEAD_PAD, (h + 1) * MLA_HEAD_PAD)
        km_ref[:, sl] = (kn[:, sl] + k_rope).astype(BF16)


def _inproj(x2, g_attn, wk, wqvt, wc, qn, kvn, wuqt, wuqrt, wuk, wuvt, cq_tt, sq_tt, ck_t, sk_t,
            batch, seq):
    n_rows = x2.shape[0]
    tm = ROW_TILE
    steps = n_rows // tm
    per_seq = seq // tm
    nblk = tm // ATTN_TILE
    row = lambda i: (i, 0)
    tab = lambda i: (i % per_seq, 0)
    tab_t = lambda i: (0, i % per_seq)
    tile_map = lambda i: (i // per_seq, i % per_seq, 0, 0)
    bf = lambda w: jax.ShapeDtypeStruct((n_rows, w), BF16)
    tiles = lambda w: jax.ShapeDtypeStruct((batch, seq // ATTN_TILE, w, ATTN_TILE), BF16)
    tile_spec = lambda w: pl.BlockSpec((None, nblk, w, ATTN_TILE), tile_map)
    return pl.pallas_call(
        _inproj_kernel,
        grid=(steps,),
        in_specs=[
            pl.BlockSpec((tm, D_MODEL), row),
            _const_spec(g_attn.shape), _const_spec(wk.shape), _const_spec(wqvt.shape),
            _const_spec(wc.shape), _const_spec(qn.shape), _const_spec(kvn.shape),
            _const_spec(wuqt.shape), _const_spec(wuqrt.shape), _const_spec(wuk.shape),
            _const_spec(wuvt.shape),
            pl.BlockSpec((LANES, tm), tab_t), pl.BlockSpec((LANES, tm), tab_t),
            pl.BlockSpec((tm, LANES), tab), pl.BlockSpec((tm, LANES), tab),
        ],
        out_specs=[
            tile_spec(A_WIDTH), pl.BlockSpec((tm, A_WIDTH), row), tile_spec(A_WIDTH),
            pl.BlockSpec((1, nblk, A_WIDTH), lambda i: (i, 0, 0)),
            tile_spec(B_HEADS * MLA_HEAD_PAD),
            pl.BlockSpec((tm, B_HEADS * MLA_HEAD_PAD), row),
            tile_spec(B_WIDTH),
        ],
        out_shape=[
            tiles(A_WIDTH), bf(A_WIDTH), tiles(A_WIDTH),
            jax.ShapeDtypeStruct((steps, nblk, A_WIDTH), F32),
            tiles(B_HEADS * MLA_HEAD_PAD), bf(B_HEADS * MLA_HEAD_PAD), tiles(B_WIDTH),
        ],
        compiler_params=pltpu.CompilerParams(
            dimension_semantics=("arbitrary",), vmem_limit_bytes=VMEM_LIMIT),
        name="in_proj",
    )(x2, g_attn, wk, wqvt, wc, qn, kvn, wuqt, wuqrt, wuk, wuvt, cq_tt, sq_tt, ck_t, sk_t)


def _sum_rows():
    r = lax.broadcasted_iota(jnp.int32, (SUM_ROWS, ATTN_TILE), 0)
    return (r == 0).astype(BF16)


def _softmax_step(s_ref, v_t, m_ref, acc_ref, first):
    s_max = jnp.max(s_ref[...], axis=0, keepdims=True)
    if first:
        m_new = s_max
    else:
        m_old = m_ref[...]
        m_new = jnp.maximum(m_old, s_max)
    v_aug = jnp.concatenate([v_t, _sum_rows()], axis=0)
    p_t = jnp.exp2(s_ref[...] - m_new).astype(BF16)
    pv = jnp.dot(v_aug, p_t, preferred_element_type=F32)
    if first:
        acc_ref[...] = pv
    else:
        acc_ref[...] = jnp.exp2(m_old - m_new) * acc_ref[...] + pv
    m_ref[...] = m_new


def _pipeline(n_items, score, attend, bufs):
    b0, b1 = bufs
    score(0, b0)
    n_pairs = (n_items - 1) // 2

    def pair(p, carry):
        score(2 * p + 1, b1)
        attend(2 * p, b0)
        score(2 * p + 2, b0)
        attend(2 * p + 1, b1)
        return carry

    lax.fori_loop(0, n_pairs, pair, 0)
    last = 2 * n_pairs
    if n_items - last == 2:
        score(last + 1, b1)
        attend(last, b0)
        attend(last + 1, b1)
    else:
        attend(last, b0)


def _tile_rows(j):
    return pl.ds(pl.multiple_of(j * ATTN_TILE, ATTN_TILE), ATTN_TILE)


def _write_heads(acc_ref, o_ref, nq, head_dim):
    def one(qi, carry):
        outs = []
        for h in range(HEAD_GROUP):
            acc = acc_ref[qi, h]
            outs.append(acc[:head_dim] / acc[head_dim:head_dim + 1])
        o_ref[_tile_rows(qi), :] = jnp.concatenate(outs, axis=0).T.astype(o_ref.dtype)
        return carry

    lax.fori_loop(0, nq, one, 0)


def _causal_items(nq, first_past):
    phases = [[(i, i - d) for i in range(d, nq)] for d in range(first_past)]
    phases.append([(i, j) for i in range(first_past, nq) for j in range(i - first_past + 1)])
    flat = [it for ph in phases for it in ph]
    tq = jnp.asarray([it[0] for it in flat], jnp.int32)
    tk = jnp.asarray([it[1] for it in flat], jnp.int32)
    return tq, tk, [len(ph) for ph in phases]


def _moba_kernel(tq_ref, tk_ref, qt_ref, k_ref, vt_ref, kmean_ref, bias_ref, o_ref,
                 kaug_ref, qaug_ref, s0_ref, s1_ref, m_ref, acc_ref, *, seq, phases):
    nblk = seq // MOBA_BLOCK
    blk = lax.broadcasted_iota(jnp.int32, (seq, LANES), 0) // MOBA_BLOCK
    onehot = (blk == lax.broadcasted_iota(jnp.int32, (seq, LANES), 1)).astype(BF16)
    for pr in range(HEAD_GROUP // 2):
        kaug_ref[pr, :, :LANES] = k_ref[:, pr * LANES:(pr + 1) * LANES]
        kaug_ref[pr, :, LANES:] = onehot

    def select_blocks(i, carry):
        feat = lax.broadcasted_iota(jnp.int32, (LANES, ATTN_TILE), 0)
        row = lax.broadcasted_iota(jnp.int32, (nblk, ATTN_TILE), 0)
        past = row < i
        for h in range(HEAD_GROUP):
            pr, sub = divmod(h, 2)
            q_t = qt_ref[i, pr * LANES:(pr + 1) * LANES, :]
            in_head = (feat >= sub * A_HEAD_DIM) & (feat < (sub + 1) * A_HEAD_DIM)
            qh_t = jnp.where(in_head, q_t, jnp.zeros_like(q_t))
            kmean = kmean_ref[:, pr * LANES:(pr + 1) * LANES].astype(BF16)
            gate = jnp.dot(kmean, qh_t, preferred_element_type=F32)
            gate = jnp.where(past, gate, -jnp.inf)
            rank = jnp.zeros((nblk, ATTN_TILE), F32)
            for c in range(nblk):
                gc = gate[c:c + 1, :]
                beats = (gc > gate) | ((gc == gate) & (c < row))
                rank = rank + beats.astype(F32)
            keep = (past & (rank < MOBA_TOPK)) | (row == i)
            mask_t = jnp.where(keep, 0.0, NEG).astype(BF16)
            qaug_ref[i, h] = jnp.concatenate(
                [qh_t, mask_t, jnp.zeros((LANES - nblk, ATTN_TILE), BF16)], axis=0)
        return carry

    lax.fori_loop(0, nblk, select_blocks, 0)

    def stage(offset, bias_cols, first):
        def score(n, buf):
            qi, kj = tq_ref[offset + n], tk_ref[offset + n]
            for h in range(HEAD_GROUP):
                s_t = jnp.dot(kaug_ref[h // 2, _tile_rows(kj), :], qaug_ref[qi, h],
                              preferred_element_type=F32)
                if bias_cols is not None:
                    s_t = s_t + bias_ref[h, :, bias_cols]
                buf[h] = s_t

        def attend(n, buf):
            qi, kj = tq_ref[offset + n], tk_ref[offset + n]
            for h in range(HEAD_GROUP):
                _softmax_step(buf.at[h], vt_ref[kj, h * A_HEAD_DIM:(h + 1) * A_HEAD_DIM, :],
                              m_ref.at[qi, h], acc_ref.at[qi, h], first)

        return score, attend

    bufs = (s0_ref, s1_ref)
    n_own, n_prev, n_far = phases
    _pipeline(n_own, *stage(0, slice(0, MOBA_BLOCK), True), bufs)
    _pipeline(n_prev, *stage(n_own, slice(MOBA_BLOCK, 2 * MOBA_BLOCK), False), bufs)
    _pipeline(n_far, *stage(n_own + n_prev, None, False), bufs)
    _write_heads(acc_ref, o_ref, nblk, A_HEAD_DIM)


def _moba(qat, ka, vat, kmean, bias):
    b, seq, _ = ka.shape
    nq = seq // ATTN_TILE
    gw = HEAD_GROUP * A_HEAD_DIM
    tq, tk, phases = _causal_items(nq, 2)
    per_group = lambda bi, g, *_: (bi, 0, g)
    tiles = lambda bi, g, *_: (bi, 0, g, 0)
    return pl.pallas_call(
        functools.partial(_moba_kernel, seq=seq, phases=phases),
        grid_spec=pltpu.PrefetchScalarGridSpec(
            num_scalar_prefetch=2,
            grid=(b, A_HEADS // HEAD_GROUP),
            in_specs=[
                pl.BlockSpec((None, nq, gw, ATTN_TILE), tiles),
                pl.BlockSpec((None, seq, gw), per_group),
                pl.BlockSpec((None, nq, gw, MOBA_BLOCK), tiles),
                pl.BlockSpec((None, nq, gw), per_group),
                pl.BlockSpec((HEAD_GROUP, MOBA_BLOCK, 2 * MOBA_BLOCK), lambda bi, g, *_: (g, 0, 0)),
            ],
            out_specs=pl.BlockSpec((None, seq, gw), per_group),
            scratch_shapes=[
                pltpu.VMEM((HEAD_GROUP // 2, seq, 2 * LANES), BF16),
                pltpu.VMEM((nq, HEAD_GROUP, 2 * LANES, ATTN_TILE), BF16),
                pltpu.VMEM((HEAD_GROUP, ATTN_TILE, ATTN_TILE), F32),
                pltpu.VMEM((HEAD_GROUP, ATTN_TILE, ATTN_TILE), F32),
                pltpu.VMEM((nq, HEAD_GROUP, 1, ATTN_TILE), F32),
                pltpu.VMEM((nq, HEAD_GROUP, A_HEAD_DIM + SUM_ROWS, ATTN_TILE), F32),
            ]),
        out_shape=jax.ShapeDtypeStruct((b, seq, A_WIDTH), BF16),
        compiler_params=pltpu.CompilerParams(
            dimension_semantics=("arbitrary", "arbitrary"), vmem_limit_bytes=VMEM_LIMIT),
        name="moba_attention",
    )(tq, tk, qat, ka, vat, kmean, bias)


def _mla_kernel(tq_ref, tk_ref, qt_ref, k_ref, vt_ref, o_ref, s0_ref, s1_ref, m_ref, acc_ref,
                *, nq, phases):
    def stage(offset, diagonal):
        def score(n, buf):
            qi, kj = tq_ref[offset + n], tk_ref[offset + n]
            for h in range(HEAD_GROUP):
                sl = slice(h * MLA_HEAD_PAD, (h + 1) * MLA_HEAD_PAD)
                s_t = jnp.dot(k_ref[_tile_rows(kj), sl], qt_ref[qi, sl, :],
                              preferred_element_type=F32)
                if diagonal:
                    key = lax.broadcasted_iota(jnp.int32, (ATTN_TILE, ATTN_TILE), 0)
                    qry = lax.broadcasted_iota(jnp.int32, (ATTN_TILE, ATTN_TILE), 1)
                    s_t = jnp.where(key <= qry, s_t, NEG)
                buf[h] = s_t

        def attend(n, buf):
            qi, kj = tq_ref[offset + n], tk_ref[offset + n]
            for h in range(HEAD_GROUP):
                _softmax_step(buf.at[h], vt_ref[kj, h * V_HEAD:(h + 1) * V_HEAD, :],
                              m_ref.at[qi, h], acc_ref.at[qi, h], diagonal)

        return score, attend

    bufs = (s0_ref, s1_ref)
    n_diag, n_past = phases
    _pipeline(n_diag, *stage(0, True), bufs)
    _pipeline(n_past, *stage(n_diag, False), bufs)
    _write_heads(acc_ref, o_ref, nq, V_HEAD)


def _mla(qmt, km, vbt):
    b, seq, _ = km.shape
    nq = seq // ATTN_TILE
    qw = HEAD_GROUP * MLA_HEAD_PAD
    vw = HEAD_GROUP * V_HEAD
    tq, tk, phases = _causal_items(nq, 1)
    per_group = lambda bi, g, *_: (bi, 0, g)
    tiles = lambda bi, g, *_: (bi, 0, g, 0)
    return pl.pallas_call(
        functools.partial(_mla_kernel, nq=nq, phases=phases),
        grid_spec=pltpu.PrefetchScalarGridSpec(
            num_scalar_prefetch=2,
            grid=(b, B_HEADS // HEAD_GROUP),
            in_specs=[
                pl.BlockSpec((None, nq, qw, ATTN_TILE), tiles),
                pl.BlockSpec((None, seq, qw), per_group),
                pl.BlockSpec((None, nq, vw, ATTN_TILE), tiles),
            ],
            out_specs=pl.BlockSpec((None, seq, vw), per_group),
            scratch_shapes=[
                pltpu.VMEM((HEAD_GROUP, ATTN_TILE, ATTN_TILE), F32),
                pltpu.VMEM((HEAD_GROUP, ATTN_TILE, ATTN_TILE), F32),
                pltpu.VMEM((nq, HEAD_GROUP, 1, ATTN_TILE), F32),
                pltpu.VMEM((nq, HEAD_GROUP, V_HEAD + SUM_ROWS, ATTN_TILE), F32),
            ]),
        out_shape=jax.ShapeDtypeStruct((b, seq, B_WIDTH), BF16),
        compiler_params=pltpu.CompilerParams(
            dimension_semantics=("arbitrary", "arbitrary"), vmem_limit_bytes=VMEM_LIMIT),
        name="mla_attention",
    )(tq, tk, qmt, km, vbt)


def _out_kernel(x_ref, oa_ref, ob_ref, gattn_ref, wg_ref, wpa_ref, wpb_ref, wout_ref,
                gmlp_ref, wup_ref, wdn_ref, gfin_ref, o_ref):
    x = x_ref[...]
    n = _rms(x, gattn_ref[...]).astype(BF16)
    gates = jnp.dot(n, wg_ref[...], preferred_element_type=F32)
    pa = jnp.dot(oa_ref[...], wpa_ref[...], preferred_element_type=F32)
    pb = jnp.dot(ob_ref[...], wpb_ref[...], preferred_element_type=F32)
    merged = (jax.nn.sigmoid(gates[:, :D_MODEL]) * pa
              + jax.nn.sigmoid(gates[:, D_MODEL:]) * pb).astype(BF16)
    h = x + jnp.dot(merged, wout_ref[...], preferred_element_type=F32)
    m = _rms(h, gmlp_ref[...]).astype(BF16)
    for c in range(D_FF // FF_CHUNK):
        cols = slice(c * FF_CHUNK, (c + 1) * FF_CHUNK)
        up = jnp.dot(m, wup_ref[:, cols], preferred_element_type=F32)
        act = jnp.square(jnp.maximum(up, 0.0)).astype(BF16)
        h = h + jnp.dot(act, wdn_ref[cols, :], preferred_element_type=F32)
    o_ref[...] = _rms(h, gfin_ref[...])


def _out_block(x2, oa, ob, g_attn, wg, wpa, wpb, wout, g_mlp, wup, wdn, g_fin):
    n_rows = x2.shape[0]
    tm = ROW_TILE
    row = lambda i: (i, 0)
    return pl.pallas_call(
        _out_kernel,
        grid=(n_rows // tm,),
        in_specs=[
            pl.BlockSpec((tm, D_MODEL), row),
            pl.BlockSpec((tm, A_WIDTH), row), pl.BlockSpec((tm, B_WIDTH), row),
            _const_spec(g_attn.shape), _const_spec(wg.shape), _const_spec(wpa.shape),
            _const_spec(wpb.shape), _const_spec(wout.shape), _const_spec(g_mlp.shape),
            _const_spec(wup.shape), _const_spec(wdn.shape), _const_spec(g_fin.shape),
        ],
        out_specs=pl.BlockSpec((tm, D_MODEL), row),
        out_shape=jax.ShapeDtypeStruct((n_rows, D_MODEL), F32),
        compiler_params=pltpu.CompilerParams(
            dimension_semantics=("arbitrary",), vmem_limit_bytes=VMEM_LIMIT),
        name="out_mlp",
    )(x2, oa, ob, g_attn, wg, wpa, wpb, wout, g_mlp, wup, wdn, g_fin)


def _rope_tables(seq):
    half = QK_ROPE // 2
    inv_freq = ROPE_THETA ** (-jnp.arange(half, dtype=F32) / half)
    ang = jnp.arange(seq).astype(F32)[:, None] * inv_freq[None, :]
    cos2 = jnp.tile(jnp.cos(ang), (1, 2))
    sin2 = jnp.tile(jnp.sin(ang), (1, 2))
    pad = jnp.zeros((seq, MLA_HEAD_PAD - QK_NOPE - QK_ROPE), F32)
    cos_t = jnp.concatenate([jnp.ones((seq, QK_NOPE), F32), cos2, pad], axis=1)
    sin_t = jnp.concatenate([jnp.zeros((seq, QK_NOPE), F32), sin2, pad], axis=1)
    return cos_t, sin_t


def _rotate_half_cols(w):
    half = QK_ROPE // 2
    return jnp.concatenate([-w[..., half:], w[..., :half]], axis=-1)


def _pad_heads(w_nope, w_rope):
    k, h, _ = w_nope.shape
    if w_rope is None:
        w_rope = jnp.zeros((k, h, QK_ROPE), w_nope.dtype)
    pad = jnp.zeros((k, h, MLA_HEAD_PAD - QK_NOPE - QK_ROPE), w_nope.dtype)
    return jnp.concatenate([w_nope, w_rope, pad], axis=-1).reshape(k, h * MLA_HEAD_PAD)


def kernel(x, w_in, rel_bias, mla_q_norm, w_uq, mla_kv_norm, w_uk, w_uv, w_proj_a, w_proj_b,
           w_out, norm_attn, norm_mlp, w_mlp_up, w_mlp_down, norm_final):
    b, seq, d = x.shape
    assert d == D_MODEL and seq % ROW_TILE == 0 and seq // MOBA_BLOCK <= LANES
    assert w_in.shape[0] == 1, "single-layer block"
    x2 = x.reshape(b * seq, d)

    w = w_in[0]
    o_k = A_WIDTH
    o_v = 2 * A_WIDTH
    o_cq = 3 * A_WIDTH
    o_ckv = o_cq + Q_LORA
    o_kr = o_ckv + KV_LORA
    o_g = o_kr + QK_ROPE
    wk = w[:, o_k:o_v].astype(BF16)
    wqvt = jnp.concatenate([w[:, :o_k], w[:, o_v:o_cq]], axis=1).T.astype(BF16)
    w_kr = w[:, o_kr:o_g]
    lane_pad = lambda a: jnp.pad(a, ((0, 0), (QK_NOPE, MLA_HEAD_PAD - QK_NOPE - QK_ROPE)))
    wc = jnp.concatenate(
        [w[:, o_cq:o_kr], lane_pad(w_kr), lane_pad(_rotate_half_cols(w_kr))], axis=1).astype(BF16)
    wg = w[:, o_g:].astype(BF16)

    uq = w_uq[0].reshape(Q_LORA, B_HEADS, QK_NOPE + QK_ROPE)
    uq_nope, uq_rope = uq[..., :QK_NOPE], uq[..., QK_NOPE:]
    wuqt = _pad_heads(uq_nope, uq_rope).T.astype(BF16)
    wuqrt = _pad_heads(jnp.zeros_like(uq_nope), _rotate_half_cols(uq_rope)).T.astype(BF16)
    wuk = _pad_heads(w_uk[0].reshape(KV_LORA, B_HEADS, QK_NOPE), None).astype(BF16)
    wuvt = w_uv[0].T.astype(BF16)

    cos_t, sin_t = _rope_tables(seq)
    q_scale = (QK_NOPE + QK_ROPE) ** -0.5 * LOG2E
    row2 = lambda a: a.reshape(1, -1)

    qat, ka, vat, kmean, qmt, km, vbt = _inproj(
        x2, row2(norm_attn[0]), wk, wqvt, wc, row2(mla_q_norm[0]), row2(mla_kv_norm[0]),
        wuqt, wuqrt, wuk, wuvt, (cos_t * q_scale).T, (sin_t * q_scale).T, cos_t, sin_t, b, seq)

    as_seq = lambda a: a.reshape(b, seq, a.shape[-1])
    bias = _bias_tiles(rel_bias)
    oa = _moba(qat, as_seq(ka), vat, kmean.reshape(b, seq // MOBA_BLOCK, A_WIDTH), bias)
    ob = _mla(qmt, as_seq(km), vbt)

    out = _out_block(
        x2, oa.reshape(b * seq, A_WIDTH), ob.reshape(b * seq, B_WIDTH), row2(norm_attn[0]),
        wg, w_proj_a[0].astype(BF16), w_proj_b[0].astype(BF16), w_out[0].astype(BF16),
        row2(norm_mlp[0]), w_mlp_up[0].astype(BF16), w_mlp_down[0].astype(BF16),
        row2(norm_final))
    return out.reshape(b, seq, d)
```

```python
import functools
import math

import jax
import jax.numpy as jnp
from jax import lax
from jax.experimental import pallas as pl
from jax.experimental.pallas import tpu as pltpu

D_MODEL = 1024
A_HEADS = 8
A_HEAD_DIM = 64
A_WIDTH = A_HEADS * A_HEAD_DIM
MOBA_BLOCK = 256
MOBA_TOPK = 3
REL_BUCKETS = 32
REL_MAX_DIST = 128
B_HEADS = 8
QK_NOPE = 64
QK_ROPE = 32
V_HEAD = 64
B_WIDTH = B_HEADS * V_HEAD
Q_LORA = 384
KV_LORA = 256
ROPE_THETA = 10000.0
D_FF = 4 * D_MODEL
EPS = 1e-6
NEG = -1e30
LOG2E = math.log2(math.e)

LANES = 128
MLA_HEAD_PAD = 128
ATTN_TILE = 256
HEAD_GROUP = 4
SUM_ROWS = 16
ROW_TILE = 512
FF_CHUNK = 1024
VMEM_LIMIT = 56 * 1024 * 1024

F32 = jnp.float32
BF16 = jnp.bfloat16
NT_DIMS = (((1,), (1,)), ((), ()))


def _rms(xf, g):
    y = xf * lax.rsqrt(jnp.mean(xf * xf, axis=-1, keepdims=True) + EPS)
    return y * g


def _const_spec(shape):
    zeros = (0,) * len(shape)
    return pl.BlockSpec(shape, lambda *_: zeros, pipeline_mode=pl.Buffered(1))


def _bias_kernel(rel_ref, o_ref):
    h = pl.program_id(0)
    shape = (MOBA_BLOCK, 2 * MOBA_BLOCK)
    r = lax.broadcasted_iota(jnp.int32, shape, 0)
    c = lax.broadcasted_iota(jnp.int32, shape, 1)
    d = c - r
    dist = jnp.maximum(d, 0)
    max_exact = REL_BUCKETS // 2
    df = jnp.maximum(dist, 1).astype(F32)
    large = max_exact + (jnp.log(df / max_exact) / math.log(REL_MAX_DIST / max_exact)
                         * (REL_BUCKETS - max_exact)).astype(jnp.int32)
    large = jnp.minimum(large, REL_BUCKETS - 1)
    bucket = jnp.where(dist < max_exact, dist, large)
    val = jnp.zeros(shape, F32)
    for b in range(REL_BUCKETS):
        val = jnp.where(bucket == b, rel_ref[b, h], val)
    o_ref[...] = jnp.where(d >= 0, (val - rel_ref[REL_BUCKETS - 1, h]) * LOG2E, NEG)


def _bias_tiles(rel_bias):
    return pl.pallas_call(
        _bias_kernel,
        grid=(A_HEADS,),
        in_specs=[pl.BlockSpec(memory_space=pltpu.SMEM)],
        out_specs=pl.BlockSpec((None, MOBA_BLOCK, 2 * MOBA_BLOCK), lambda h: (h, 0, 0)),
        out_shape=jax.ShapeDtypeStruct((A_HEADS, MOBA_BLOCK, 2 * MOBA_BLOCK), F32),
        name="moba_bias_tiles",
    )(rel_bias)


def _inproj_kernel(x_ref, g_ref, wk_ref, wqvt_ref, wc_ref, qn_ref, kvn_ref, wuqt_ref, wuqrt_ref,
                   wuk_ref, wuvt_ref, cqt_ref, sqt_ref, ck_ref, sk_ref,
                   qat_ref, ka_ref, vat_ref, kmean_ref, qmt_ref, km_ref, vbt_ref):
    n = _rms(x_ref[...], g_ref[...]).astype(BF16)
    k = jnp.dot(n, wk_ref[...], preferred_element_type=F32)
    ka_ref[...] = k.astype(BF16)
    nblk = ROW_TILE // ATTN_TILE
    kmean_ref[0] = jnp.sum(k.reshape(nblk, MOBA_BLOCK, A_WIDTH), axis=1) * (1.0 / MOBA_BLOCK)

    c = jnp.dot(n, wc_ref[...], preferred_element_type=F32)
    cq = _rms(c[:, :Q_LORA], qn_ref[...]).astype(BF16)
    ckv = _rms(c[:, Q_LORA:Q_LORA + KV_LORA], kvn_ref[...]).astype(BF16)
    kr = c[:, Q_LORA + KV_LORA:Q_LORA + KV_LORA + LANES]
    krr = c[:, Q_LORA + KV_LORA + LANES:]
    for t in range(nblk):
        rows = slice(t * ATTN_TILE, (t + 1) * ATTN_TILE)
        qv_t = lax.dot_general(wqvt_ref[...], n[rows], NT_DIMS,
                               preferred_element_type=F32)
        qat_ref[t] = (qv_t[:A_WIDTH] * (A_HEAD_DIM ** -0.5 * LOG2E)).astype(BF16)
        vat_ref[t] = qv_t[A_WIDTH:].astype(BF16)
        vbt_ref[t] = lax.dot_general(wuvt_ref[...], ckv[rows], NT_DIMS,
                                     preferred_element_type=F32).astype(BF16)
        q_t = lax.dot_general(wuqt_ref[...], cq[rows], NT_DIMS,
                              preferred_element_type=F32)
        qr_t = lax.dot_general(wuqrt_ref[...], cq[rows], NT_DIMS, preferred_element_type=F32)
        cq_t, sq_t = cqt_ref[:, rows], sqt_ref[:, rows]
        for h in range(B_HEADS):
            sl = slice(h * MLA_HEAD_PAD, (h + 1) * MLA_HEAD_PAD)
            qmt_ref[t, sl, :] = (q_t[sl] * cq_t + qr_t[sl] * sq_t).astype(BF16)
    kn = jnp.dot(ckv, wuk_ref[...], preferred_element_type=F32)
    k_rope = kr * ck_ref[...] + krr * sk_ref[...]
    for h in range(B_HEADS):
        sl = slice(h * MLA_HEAD_PAD, (h + 1) * MLA_HEAD_PAD)
        km_ref[:, sl] = (kn[:, sl] + k_rope).astype(BF16)


def _inproj(x2, g_attn, wk, wqvt, wc, qn, kvn, wuqt, wuqrt, wuk, wuvt, cq_tt, sq_tt, ck_t, sk_t,
            batch, seq):
    n_rows = x2.shape[0]
    tm = ROW_TILE
    steps = n_rows // tm
    per_seq = seq // tm
    nblk = tm // ATTN_TILE
    row = lambda i: (i, 0)
    tab = lambda i: (i % per_seq, 0)
    tab_t = lambda i: (0, i % per_seq)
    tile_map = lambda i: (i // per_seq, i % per_seq, 0, 0)
    bf = lambda w: jax.ShapeDtypeStruct((n_rows, w), BF16)
    tiles = lambda w: jax.ShapeDtypeStruct((batch, seq // ATTN_TILE, w, ATTN_TILE), BF16)
    tile_spec = lambda w: pl.BlockSpec((None, nblk, w, ATTN_TILE), tile_map)
    return pl.pallas_call(
        _inproj_kernel,
        grid=(steps,),
        in_specs=[
            pl.BlockSpec((tm, D_MODEL), row),
            _const_spec(g_attn.shape), _const_spec(wk.shape), _const_spec(wqvt.shape),
            _const_spec(wc.shape), _const_spec(qn.shape), _const_spec(kvn.shape),
            _const_spec(wuqt.shape), _const_spec(wuqrt.shape), _const_spec(wuk.shape),
            _const_spec(wuvt.shape),
            pl.BlockSpec((LANES, tm), tab_t), pl.BlockSpec((LANES, tm), tab_t),
            pl.BlockSpec((tm, LANES), tab), pl.BlockSpec((tm, LANES), tab),
        ],
        out_specs=[
            tile_spec(A_WIDTH), pl.BlockSpec((tm, A_WIDTH), row), tile_spec(A_WIDTH),
            pl.BlockSpec((1, nblk, A_WIDTH), lambda i: (i, 0, 0)),
            tile_spec(B_HEADS * MLA_HEAD_PAD),
            pl.BlockSpec((tm, B_HEADS * MLA_HEAD_PAD), row),
            tile_spec(B_WIDTH),
        ],
        out_shape=[
            tiles(A_WIDTH), bf(A_WIDTH), tiles(A_WIDTH),
            jax.ShapeDtypeStruct((steps, nblk, A_WIDTH), F32),
            tiles(B_HEADS * MLA_HEAD_PAD), bf(B_HEADS * MLA_HEAD_PAD), tiles(B_WIDTH),
        ],
        compiler_params=pltpu.CompilerParams(
            dimension_semantics=("arbitrary",), vmem_limit_bytes=VMEM_LIMIT),
        name="in_proj",
    )(x2, g_attn, wk, wqvt, wc, qn, kvn, wuqt, wuqrt, wuk, wuvt, cq_tt, sq_tt, ck_t, sk_t)


def _sum_rows():
    r = lax.broadcasted_iota(jnp.int32, (SUM_ROWS, ATTN_TILE), 0)
    return (r == 0).astype(BF16)


def _store_scores(buf, h, s_t):
    s_ref, max_ref = buf
    s_ref[h] = s_t
    max_ref[h] = jnp.max(s_t, axis=0, keepdims=True)


def _softmax_step(buf, h, v_t, m_ref, acc_ref, first):
    s_ref, max_ref = buf[0].at[h], buf[1].at[h]
    s_max = max_ref[...]
    if first:
        m_new = s_max
    else:
        m_old = m_ref[...]
        m_new = jnp.maximum(m_old, s_max)
    v_aug = jnp.concatenate([v_t, _sum_rows()], axis=0)
    p_t = jnp.exp2(s_ref[...] - m_new).astype(BF16)
    pv = jnp.dot(v_aug, p_t, preferred_element_type=F32)
    if first:
        acc_ref[...] = pv
    else:
        acc_ref[...] = jnp.exp2(m_old - m_new) * acc_ref[...] + pv
    m_ref[...] = m_new


def _pipeline(n_items, score, attend, bufs):
    b0, b1 = bufs
    score(0, b0)
    n_pairs = (n_items - 1) // 2

    def pair(p, carry):
        score(2 * p + 1, b1)
        attend(2 * p, b0)
        score(2 * p + 2, b0)
        attend(2 * p + 1, b1)
        return carry

    lax.fori_loop(0, n_pairs, pair, 0)
    last = 2 * n_pairs
    if n_items - last == 2:
        score(last + 1, b1)
        attend(last, b0)
        attend(last + 1, b1)
    else:
        attend(last, b0)


def _tile_rows(j):
    return pl.ds(pl.multiple_of(j * ATTN_TILE, ATTN_TILE), ATTN_TILE)


def _write_heads(acc_ref, o_ref, nq, head_dim):
    def one(qi, carry):
        outs = []
        for h in range(HEAD_GROUP):
            acc = acc_ref[qi, h]
            outs.append(acc[:head_dim] / acc[head_dim:head_dim + 1])
        o_ref[_tile_rows(qi), :] = jnp.concatenate(outs, axis=0).T.astype(o_ref.dtype)
        return carry

    lax.fori_loop(0, nq, one, 0)


def _causal_items(nq, first_past):
    phases = [[(i, i - d) for i in range(d, nq)] for d in range(first_past)]
    phases.append([(i, j) for i in range(first_past, nq) for j in range(i - first_past + 1)])
    flat = [it for ph in phases for it in ph]
    tq = jnp.asarray([it[0] for it in flat], jnp.int32)
    tk = jnp.asarray([it[1] for it in flat], jnp.int32)
    return tq, tk, [len(ph) for ph in phases]


def _moba_kernel(tq_ref, tk_ref, qt_ref, k_ref, vt_ref, kmean_ref, bias_ref, o_ref,
                 kaug_ref, qaug_ref, s0_ref, s1_ref, t0_ref, t1_ref, m_ref, acc_ref,
                 *, seq, phases):
    nblk = seq // MOBA_BLOCK
    blk = lax.broadcasted_iota(jnp.int32, (seq, LANES), 0) // MOBA_BLOCK
    onehot = (blk == lax.broadcasted_iota(jnp.int32, (seq, LANES), 1)).astype(BF16)
    for pr in range(HEAD_GROUP // 2):
        kaug_ref[pr, :, :LANES] = k_ref[:, pr * LANES:(pr + 1) * LANES]
        kaug_ref[pr, :, LANES:] = onehot

    def select_blocks(i, carry):
        feat = lax.broadcasted_iota(jnp.int32, (LANES, ATTN_TILE), 0)
        row = lax.broadcasted_iota(jnp.int32, (nblk, ATTN_TILE), 0)
        past = row < i
        for h in range(HEAD_GROUP):
            pr, sub = divmod(h, 2)
            q_t = qt_ref[i, pr * LANES:(pr + 1) * LANES, :]
            in_head = (feat >= sub * A_HEAD_DIM) & (feat < (sub + 1) * A_HEAD_DIM)
            qh_t = jnp.where(in_head, q_t, jnp.zeros_like(q_t))
            kmean = kmean_ref[:, pr * LANES:(pr + 1) * LANES].astype(BF16)
            gate = jnp.dot(kmean, qh_t, preferred_element_type=F32)
            gate = jnp.where(past, gate, -jnp.inf)
            rank = jnp.zeros((nblk, ATTN_TILE), F32)
            for c in range(nblk):
                gc = gate[c:c + 1, :]
                beats = (gc > gate) | ((gc == gate) & (c < row))
                rank = rank + beats.astype(F32)
            keep = (past & (rank < MOBA_TOPK)) | (row == i)
            mask_t = jnp.where(keep, 0.0, NEG).astype(BF16)
            qaug_ref[i, h] = jnp.concatenate(
                [qh_t, mask_t, jnp.zeros((LANES - nblk, ATTN_TILE), BF16)], axis=0)
        return carry

    lax.fori_loop(0, nblk, select_blocks, 0)

    def stage(offset, bias_cols, first):
        def score(n, buf):
            qi, kj = tq_ref[offset + n], tk_ref[offset + n]
            for h in range(HEAD_GROUP):
                s_t = jnp.dot(kaug_ref[h // 2, _tile_rows(kj), :], qaug_ref[qi, h],
                              preferred_element_type=F32)
                if bias_cols is not None:
                    s_t = s_t + bias_ref[h, :, bias_cols]
                _store_scores(buf, h, s_t)

        def attend(n, buf):
            qi, kj = tq_ref[offset + n], tk_ref[offset + n]
            for h in range(HEAD_GROUP):
                _softmax_step(buf, h, vt_ref[kj, h * A_HEAD_DIM:(h + 1) * A_HEAD_DIM, :],
                              m_ref.at[qi, h], acc_ref.at[qi, h], first)

        return score, attend

    bufs = ((s0_ref, t0_ref), (s1_ref, t1_ref))
    n_own, n_prev, n_far = phases
    _pipeline(n_own, *stage(0, slice(0, MOBA_BLOCK), True), bufs)
    _pipeline(n_prev, *stage(n_own, slice(MOBA_BLOCK, 2 * MOBA_BLOCK), False), bufs)
    _pipeline(n_far, *stage(n_own + n_prev, None, False), bufs)
    _write_heads(acc_ref, o_ref, nblk, A_HEAD_DIM)


def _moba(qat, ka, vat, kmean, bias):
    b, seq, _ = ka.shape
    nq = seq // ATTN_TILE
    gw = HEAD_GROUP * A_HEAD_DIM
    tq, tk, phases = _causal_items(nq, 2)
    per_group = lambda bi, g, *_: (bi, 0, g)
    tiles = lambda bi, g, *_: (bi, 0, g, 0)
    return pl.pallas_call(
        functools.partial(_moba_kernel, seq=seq, phases=phases),
        grid_spec=pltpu.PrefetchScalarGridSpec(
            num_scalar_prefetch=2,
            grid=(b, A_HEADS // HEAD_GROUP),
            in_specs=[
                pl.BlockSpec((None, nq, gw, ATTN_TILE), tiles),
                pl.BlockSpec((None, seq, gw), per_group),
                pl.BlockSpec((None, nq, gw, MOBA_BLOCK), tiles),
                pl.BlockSpec((None, nq, gw), per_group),
                pl.BlockSpec((HEAD_GROUP, MOBA_BLOCK, 2 * MOBA_BLOCK), lambda bi, g, *_: (g, 0, 0)),
            ],
            out_specs=pl.BlockSpec((None, seq, gw), per_group),
            scratch_shapes=[
                pltpu.VMEM((HEAD_GROUP // 2, seq, 2 * LANES), BF16),
                pltpu.VMEM((nq, HEAD_GROUP, 2 * LANES, ATTN_TILE), BF16),
                pltpu.VMEM((HEAD_GROUP, ATTN_TILE, ATTN_TILE), F32),
                pltpu.VMEM((HEAD_GROUP, ATTN_TILE, ATTN_TILE), F32),
                pltpu.VMEM((HEAD_GROUP, 1, ATTN_TILE), F32),
                pltpu.VMEM((HEAD_GROUP, 1, ATTN_TILE), F32),
                pltpu.VMEM((nq, HEAD_GROUP, 1, ATTN_TILE), F32),
                pltpu.VMEM((nq, HEAD_GROUP, A_HEAD_DIM + SUM_ROWS, ATTN_TILE), F32),
            ]),
        out_shape=jax.ShapeDtypeStruct((b, seq, A_WIDTH), BF16),
        compiler_params=pltpu.CompilerParams(
            dimension_semantics=("arbitrary", "arbitrary"), vmem_limit_bytes=VMEM_LIMIT),
        name="moba_attention",
    )(tq, tk, qat, ka, vat, kmean, bias)


def _mla_kernel(tq_ref, tk_ref, qt_ref, k_ref, vt_ref, o_ref, s0_ref, s1_ref, t0_ref, t1_ref,
                m_ref, acc_ref, *, nq, phases):
    def stage(offset, diagonal):
        def score(n, buf):
            qi, kj = tq_ref[offset + n], tk_ref[offset + n]
            for h in range(HEAD_GROUP):
                sl = slice(h * MLA_HEAD_PAD, (h + 1) * MLA_HEAD_PAD)
                s_t = jnp.dot(k_ref[_tile_rows(kj), sl], qt_ref[qi, sl, :],
                              preferred_element_type=F32)
                if diagonal:
                    key = lax.broadcasted_iota(jnp.int32, (ATTN_TILE, ATTN_TILE), 0)
                    qry = lax.broadcasted_iota(jnp.int32, (ATTN_TILE, ATTN_TILE), 1)
                    s_t = jnp.where(key <= qry, s_t, NEG)
                _store_scores(buf, h, s_t)

        def attend(n, buf):
            qi, kj = tq_ref[offset + n], tk_ref[offset + n]
            for h in range(HEAD_GROUP):
                _softmax_step(buf, h, vt_ref[kj, h * V_HEAD:(h + 1) * V_HEAD, :],
                              m_ref.at[qi, h], acc_ref.at[qi, h], diagonal)

        return score, attend

    bufs = ((s0_ref, t0_ref), (s1_ref, t1_ref))
    n_diag, n_past = phases
    _pipeline(n_diag, *stage(0, True), bufs)
    _pipeline(n_past, *stage(n_diag, False), bufs)
    _write_heads(acc_ref, o_ref, nq, V_HEAD)


def _mla(qmt, km, vbt):
    b, seq, _ = km.shape
    nq = seq // ATTN_TILE
    qw = HEAD_GROUP * MLA_HEAD_PAD
    vw = HEAD_GROUP * V_HEAD
    tq, tk, phases = _causal_items(nq, 1)
    per_group = lambda bi, g, *_: (bi, 0, g)
    tiles = lambda bi, g, *_: (bi, 0, g, 0)
    return pl.pallas_call(
        functools.partial(_mla_kernel, nq=nq, phases=phases),
        grid_spec=pltpu.PrefetchScalarGridSpec(
            num_scalar_prefetch=2,
            grid=(b, B_HEADS // HEAD_GROUP),
            in_specs=[
                pl.BlockSpec((None, nq, qw, ATTN_TILE), tiles),
                pl.BlockSpec((None, seq, qw), per_group),
                pl.BlockSpec((None, nq, vw, ATTN_TILE), tiles),
            ],
            out_specs=pl.BlockSpec((None, seq, vw), per_group),
            scratch_shapes=[
                pltpu.VMEM((HEAD_GROUP, ATTN_TILE, ATTN_TILE), F32),
                pltpu.VMEM((HEAD_GROUP, ATTN_TILE, ATTN_TILE), F32),
                pltpu.VMEM((HEAD_GROUP, 1, ATTN_TILE), F32),
                pltpu.VMEM((HEAD_GROUP, 1, ATTN_TILE), F32),
                pltpu.VMEM((nq, HEAD_GROUP, 1, ATTN_TILE), F32),
                pltpu.VMEM((nq, HEAD_GROUP, V_HEAD + SUM_ROWS, ATTN_TILE), F32),
            ]),
        out_shape=jax.ShapeDtypeStruct((b, seq, B_WIDTH), BF16),
        compiler_params=pltpu.CompilerParams(
            dimension_semantics=("arbitrary", "arbitrary"), vmem_limit_bytes=VMEM_LIMIT),
        name="mla_attention",
    )(tq, tk, qmt, km, vbt)


def _out_kernel(x_ref, oa_ref, ob_ref, gattn_ref, wg_ref, wpa_ref, wpb_ref, wout_ref,
                gmlp_ref, wup_ref, wdn_ref, gfin_ref, o_ref):
    x = x_ref[...]
    n = _rms(x, gattn_ref[...]).astype(BF16)
    gates = jnp.dot(n, wg_ref[...], preferred_element_type=F32)
    pa = jnp.dot(oa_ref[...], wpa_ref[...], preferred_element_type=F32)
    pb = jnp.dot(ob_ref[...], wpb_ref[...], preferred_element_type=F32)
    merged = (jax.nn.sigmoid(gates[:, :D_MODEL]) * pa
              + jax.nn.sigmoid(gates[:, D_MODEL:]) * pb).astype(BF16)
    h = x + jnp.dot(merged, wout_ref[...], preferred_element_type=F32)
    m = _rms(h, gmlp_ref[...]).astype(BF16)
    for c in range(D_FF // FF_CHUNK):
        cols = slice(c * FF_CHUNK, (c + 1) * FF_CHUNK)
        up = jnp.dot(m, wup_ref[:, cols], preferred_element_type=F32)
        act = jnp.square(jnp.maximum(up, 0.0)).astype(BF16)
        h = h + jnp.dot(act, wdn_ref[cols, :], preferred_element_type=F32)
    o_ref[...] = _rms(h, gfin_ref[...])


def _out_block(x2, oa, ob, g_attn, wg, wpa, wpb, wout, g_mlp, wup, wdn, g_fin):
    n_rows = x2.shape[0]
    tm = ROW_TILE
    row = lambda i: (i, 0)
    return pl.pallas_call(
        _out_kernel,
        grid=(n_rows // tm,),
        in_specs=[
            pl.BlockSpec((tm, D_MODEL), row),
            pl.BlockSpec((tm, A_WIDTH), row), pl.BlockSpec((tm, B_WIDTH), row),
            _const_spec(g_attn.shape), _const_spec(wg.shape), _const_spec(wpa.shape),
            _const_spec(wpb.shape), _const_spec(wout.shape), _const_spec(g_mlp.shape),
            _const_spec(wup.shape), _const_spec(wdn.shape), _const_spec(g_fin.shape),
        ],
        out_specs=pl.BlockSpec((tm, D_MODEL), row),
        out_shape=jax.ShapeDtypeStruct((n_rows, D_MODEL), F32),
        compiler_params=pltpu.CompilerParams(
            dimension_semantics=("arbitrary",), vmem_limit_bytes=VMEM_LIMIT),
        name="out_mlp",
    )(x2, oa, ob, g_attn, wg, wpa, wpb, wout, g_mlp, wup, wdn, g_fin)


def _rope_tables(seq):
    half = QK_ROPE // 2
    inv_freq = ROPE_THETA ** (-jnp.arange(half, dtype=F32) / half)
    ang = jnp.arange(seq).astype(F32)[:, None] * inv_freq[None, :]
    cos2 = jnp.tile(jnp.cos(ang), (1, 2))
    sin2 = jnp.tile(jnp.sin(ang), (1, 2))
    pad = jnp.zeros((seq, MLA_HEAD_PAD - QK_NOPE - QK_ROPE), F32)
    cos_t = jnp.concatenate([jnp.ones((seq, QK_NOPE), F32), cos2, pad], axis=1)
    sin_t = jnp.concatenate([jnp.zeros((seq, QK_NOPE), F32), sin2, pad], axis=1)
    return cos_t, sin_t


def _rotate_half_cols(w):
    half = QK_ROPE // 2
    return jnp.concatenate([-w[..., half:], w[..., :half]], axis=-1)


def _pad_heads(w_nope, w_rope):
    k, h, _ = w_nope.shape
    if w_rope is None:
        w_rope = jnp.zeros((k, h, QK_ROPE), w_nope.dtype)
    pad = jnp.zeros((k, h, MLA_HEAD_PAD - QK_NOPE - QK_ROPE), w_nope.dtype)
    return jnp.concatenate([w_nope, w_rope, pad], axis=-1).reshape(k, h * MLA_HEAD_PAD)


def kernel(x, w_in, rel_bias, mla_q_norm, w_uq, mla_kv_norm, w_uk, w_uv, w_proj_a, w_proj_b,
           w_out, norm_attn, norm_mlp, w_mlp_up, w_mlp_down, norm_final):
    b, seq, d = x.shape
    assert d == D_MODEL and seq % ROW_TILE == 0 and seq // MOBA_BLOCK <= LANES
    assert w_in.shape[0] == 1, "single-layer block"
    x2 = x.reshape(b * seq, d)

    w = w_in[0]
    o_k = A_WIDTH
    o_v = 2 * A_WIDTH
    o_cq = 3 * A_WIDTH
    o_ckv = o_cq + Q_LORA
    o_kr = o_ckv + KV_LORA
    o_g = o_kr + QK_ROPE
    wk = w[:, o_k:o_v].astype(BF16)
    wqvt = jnp.concatenate([w[:, :o_k], w[:, o_v:o_cq]], axis=1).T.astype(BF16)
    w_kr = w[:, o_kr:o_g]
    lane_pad = lambda a: jnp.pad(a, ((0, 0), (QK_NOPE, MLA_HEAD_PAD - QK_NOPE - QK_ROPE)))
    wc = jnp.concatenate(
        [w[:, o_cq:o_kr], lane_pad(w_kr), lane_pad(_rotate_half_cols(w_kr))], axis=1).astype(BF16)
    wg = w[:, o_g:].astype(BF16)

    uq = w_uq[0].reshape(Q_LORA, B_HEADS, QK_NOPE + QK_ROPE)
    uq_nope, uq_rope = uq[..., :QK_NOPE], uq[..., QK_NOPE:]
    wuqt = _pad_heads(uq_nope, uq_rope).T.astype(BF16)
    wuqrt = _pad_heads(jnp.zeros_like(uq_nope), _rotate_half_cols(uq_rope)).T.astype(BF16)
    wuk = _pad_heads(w_uk[0].reshape(KV_LORA, B_HEADS, QK_NOPE), None).astype(BF16)
    wuvt = w_uv[0].T.astype(BF16)

    cos_t, sin_t = _rope_tables(seq)
    q_scale = (QK_NOPE + QK_ROPE) ** -0.5 * LOG2E
    row2 = lambda a: a.reshape(1, -1)

    qat, ka, vat, kmean, qmt, km, vbt = _inproj(
        x2, row2(norm_attn[0]), wk, wqvt, wc, row2(mla_q_norm[0]), row2(mla_kv_norm[0]),
        wuqt, wuqrt, wuk, wuvt, (cos_t * q_scale).T, (sin_t * q_scale).T, cos_t, sin_t, b, seq)

    as_seq = lambda a: a.reshape(b, seq, a.shape[-1])
    bias = _bias_tiles(rel_bias)
    oa = _moba(qat, as_seq(ka), vat, kmean.reshape(b, seq // MOBA_BLOCK, A_WIDTH), bias)
    ob = _mla(qmt, as_seq(km), vbt)

    out = _out_block(
        x2, oa.reshape(b * seq, A_WIDTH), ob.reshape(b * seq, B_WIDTH), row2(norm_attn[0]),
        wg, w_proj_a[0].astype(BF16), w_proj_b[0].astype(BF16), w_out[0].astype(BF16),
        row2(norm_mlp[0]), w_mlp_up[0].astype(BF16), w_mlp_down[0].astype(BF16),
        row2(norm_final))
    return out.reshape(b, seq, d)
```

```python
import functools
import math

import jax
import jax.numpy as jnp
from jax import lax
from jax.experimental import pallas as pl
from jax.experimental.pallas import tpu as pltpu

D_MODEL = 1024
A_HEADS = 8
A_HEAD_DIM = 64
A_WIDTH = A_HEADS * A_HEAD_DIM
MOBA_BLOCK = 256
MOBA_TOPK = 3
REL_BUCKETS = 32
REL_MAX_DIST = 128
B_HEADS = 8
QK_NOPE = 64
QK_ROPE = 32
V_HEAD = 64
B_WIDTH = B_HEADS * V_HEAD
Q_LORA = 384
KV_LORA = 256
ROPE_THETA = 10000.0
D_FF = 4 * D_MODEL
EPS = 1e-6
NEG = -1e30
LOG2E = math.log2(math.e)

LANES = 128
MLA_HEAD_PAD = 128
ATTN_TILE = 256
HEAD_GROUP = 4
SUM_ROWS = 16
PIPELINE_UNROLL = 4
ROW_TILE = 512
FF_CHUNK = 1024
VMEM_LIMIT = 56 * 1024 * 1024

F32 = jnp.float32
BF16 = jnp.bfloat16
NT_DIMS = (((1,), (1,)), ((), ()))


def _rms(xf, g):
    y = xf * lax.rsqrt(jnp.mean(xf * xf, axis=-1, keepdims=True) + EPS)
    return y * g


def _const_spec(shape):
    zeros = (0,) * len(shape)
    return pl.BlockSpec(shape, lambda *_: zeros, pipeline_mode=pl.Buffered(1))


def _bias_kernel(rel_ref, o_ref):
    h = pl.program_id(0)
    shape = (MOBA_BLOCK, 2 * MOBA_BLOCK)
    r = lax.broadcasted_iota(jnp.int32, shape, 0)
    c = lax.broadcasted_iota(jnp.int32, shape, 1)
    d = c - r
    dist = jnp.maximum(d, 0)
    max_exact = REL_BUCKETS // 2
    df = jnp.maximum(dist, 1).astype(F32)
    large = max_exact + (jnp.log(df / max_exact) / math.log(REL_MAX_DIST / max_exact)
                         * (REL_BUCKETS - max_exact)).astype(jnp.int32)
    large = jnp.minimum(large, REL_BUCKETS - 1)
    bucket = jnp.where(dist < max_exact, dist, large)
    val = jnp.zeros(shape, F32)
    for b in range(REL_BUCKETS):
        val = jnp.where(bucket == b, rel_ref[b, h], val)
    o_ref[...] = jnp.where(d >= 0, (val - rel_ref[REL_BUCKETS - 1, h]) * LOG2E, NEG)


def _bias_tiles(rel_bias):
    return pl.pallas_call(
        _bias_kernel,
        grid=(A_HEADS,),
        in_specs=[pl.BlockSpec(memory_space=pltpu.SMEM)],
        out_specs=pl.BlockSpec((None, MOBA_BLOCK, 2 * MOBA_BLOCK), lambda h: (h, 0, 0)),
        out_shape=jax.ShapeDtypeStruct((A_HEADS, MOBA_BLOCK, 2 * MOBA_BLOCK), F32),
        name="moba_bias_tiles",
    )(rel_bias)


def _inproj_kernel(x_ref, g_ref, wk_ref, wqvt_ref, wc_ref, qn_ref, kvn_ref, wuqt_ref, wuqrt_ref,
                   wuk_ref, wuvt_ref, cqt_ref, sqt_ref, ck_ref, sk_ref,
                   qat_ref, ka_ref, vat_ref, kmean_ref, qmt_ref, km_ref, vbt_ref):
    n = _rms(x_ref[...], g_ref[...]).astype(BF16)
    k = jnp.dot(n, wk_ref[...], preferred_element_type=F32)
    ka_ref[...] = k.astype(BF16)
    nblk = ROW_TILE // ATTN_TILE
    kmean_ref[0] = jnp.sum(k.reshape(nblk, MOBA_BLOCK, A_WIDTH), axis=1) * (1.0 / MOBA_BLOCK)

    c = jnp.dot(n, wc_ref[...], preferred_element_type=F32)
    cq = _rms(c[:, :Q_LORA], qn_ref[...]).astype(BF16)
    ckv = _rms(c[:, Q_LORA:Q_LORA + KV_LORA], kvn_ref[...]).astype(BF16)
    kr = c[:, Q_LORA + KV_LORA:Q_LORA + KV_LORA + LANES]
    krr = c[:, Q_LORA + KV_LORA + LANES:]
    for t in range(nblk):
        rows = slice(t * ATTN_TILE, (t + 1) * ATTN_TILE)
        qv_t = lax.dot_general(wqvt_ref[...], n[rows], NT_DIMS,
                               preferred_element_type=F32)
        qat_ref[t] = (qv_t[:A_WIDTH] * (A_HEAD_DIM ** -0.5 * LOG2E)).astype(BF16)
        vat_ref[t] = qv_t[A_WIDTH:].astype(BF16)
        vbt_ref[t] = lax.dot_general(wuvt_ref[...], ckv[rows], NT_DIMS,
                                     preferred_element_type=F32).astype(BF16)
        q_t = lax.dot_general(wuqt_ref[...], cq[rows], NT_DIMS,
                              preferred_element_type=F32)
        qr_t = lax.dot_general(wuqrt_ref[...], cq[rows], NT_DIMS, preferred_element_type=F32)
        cq_t, sq_t = cqt_ref[:, rows], sqt_ref[:, rows]
        for h in range(B_HEADS):
            sl = slice(h * MLA_HEAD_PAD, (h + 1) * MLA_HEAD_PAD)
            qmt_ref[t, sl, :] = (q_t[sl] * cq_t + qr_t[sl] * sq_t).astype(BF16)
    kn = jnp.dot(ckv, wuk_ref[...], preferred_element_type=F32)
    k_rope = kr * ck_ref[...] + krr * sk_ref[...]
    for h in range(B_HEADS):
        sl = slice(h * MLA_HEAD_PAD, (h + 1) * MLA_HEAD_PAD)
        km_ref[:, sl] = (kn[:, sl] + k_rope).astype(BF16)


def _inproj(x2, g_attn, wk, wqvt, wc, qn, kvn, wuqt, wuqrt, wuk, wuvt, cq_tt, sq_tt, ck_t, sk_t,
            batch, seq):
    n_rows = x2.shape[0]
    tm = ROW_TILE
    steps = n_rows // tm
    per_seq = seq // tm
    nblk = tm // ATTN_TILE
    row = lambda i: (i, 0)
    tab = lambda i: (i % per_seq, 0)
    tab_t = lambda i: (0, i % per_seq)
    tile_map = lambda i: (i // per_seq, i % per_seq, 0, 0)
    bf = lambda w: jax.ShapeDtypeStruct((n_rows, w), BF16)
    tiles = lambda w: jax.ShapeDtypeStruct((batch, seq // ATTN_TILE, w, ATTN_TILE), BF16)
    tile_spec = lambda w: pl.BlockSpec((None, nblk, w, ATTN_TILE), tile_map)
    return pl.pallas_call(
        _inproj_kernel,
        grid=(steps,),
        in_specs=[
            pl.BlockSpec((tm, D_MODEL), row),
            _const_spec(g_attn.shape), _const_spec(wk.shape), _const_spec(wqvt.shape),
            _const_spec(wc.shape), _const_spec(qn.shape), _const_spec(kvn.shape),
            _const_spec(wuqt.shape), _const_spec(wuqrt.shape), _const_spec(wuk.shape),
            _const_spec(wuvt.shape),
            pl.BlockSpec((LANES, tm), tab_t), pl.BlockSpec((LANES, tm), tab_t),
            pl.BlockSpec((tm, LANES), tab), pl.BlockSpec((tm, LANES), tab),
        ],
        out_specs=[
            tile_spec(A_WIDTH), pl.BlockSpec((tm, A_WIDTH), row), tile_spec(A_WIDTH),
            pl.BlockSpec((1, nblk, A_WIDTH), lambda i: (i, 0, 0)),
            tile_spec(B_HEADS * MLA_HEAD_PAD),
            pl.BlockSpec((tm, B_HEADS * MLA_HEAD_PAD), row),
            tile_spec(B_WIDTH),
        ],
        out_shape=[
            tiles(A_WIDTH), bf(A_WIDTH), tiles(A_WIDTH),
            jax.ShapeDtypeStruct((steps, nblk, A_WIDTH), F32),
            tiles(B_HEADS * MLA_HEAD_PAD), bf(B_HEADS * MLA_HEAD_PAD), tiles(B_WIDTH),
        ],
        compiler_params=pltpu.CompilerParams(
            dimension_semantics=("arbitrary",), vmem_limit_bytes=VMEM_LIMIT),
        name="in_proj",
    )(x2, g_attn, wk, wqvt, wc, qn, kvn, wuqt, wuqrt, wuk, wuvt, cq_tt, sq_tt, ck_t, sk_t)


def _sum_rows():
    r = lax.broadcasted_iota(jnp.int32, (SUM_ROWS, ATTN_TILE), 0)
    return (r == 0).astype(BF16)


def _store_scores(buf, h, s_t):
    s_ref, max_ref = buf
    s_ref[h] = s_t
    max_ref[h] = jnp.max(s_t, axis=0, keepdims=True)


def _softmax_step(buf, h, v_t, m_ref, acc_ref, first):
    s_ref, max_ref = buf[0].at[h], buf[1].at[h]
    s_max = max_ref[...]
    if first:
        m_new = s_max
    else:
        m_old = m_ref[...]
        m_new = jnp.maximum(m_old, s_max)
    v_aug = jnp.concatenate([v_t, _sum_rows()], axis=0)
    p_t = jnp.exp2(s_ref[...] - m_new).astype(BF16)
    pv = jnp.dot(v_aug, p_t, preferred_element_type=F32)
    if first:
        acc_ref[...] = pv
    else:
        acc_ref[...] = jnp.exp2(m_old - m_new) * acc_ref[...] + pv
    m_ref[...] = m_new


def _pipeline(n_items, score, attend, bufs):
    score(0, bufs[0])
    n_loops = (n_items - 1) // PIPELINE_UNROLL

    def body(p, carry):
        for t in range(PIPELINE_UNROLL):
            n = PIPELINE_UNROLL * p + t
            score(n + 1, bufs[(t + 1) % 2])
            attend(n, bufs[t % 2])
        return carry

    lax.fori_loop(0, n_loops, body, 0)
    done = PIPELINE_UNROLL * n_loops
    for n in range(done, n_items):
        if n + 1 < n_items:
            score(n + 1, bufs[(n + 1 - done) % 2])
        attend(n, bufs[(n - done) % 2])


def _tile_rows(j):
    return pl.ds(pl.multiple_of(j * ATTN_TILE, ATTN_TILE), ATTN_TILE)


def _write_heads(acc_ref, o_ref, nq, head_dim):
    def one(qi, carry):
        outs = []
        for h in range(HEAD_GROUP):
            acc = acc_ref[qi, h]
            outs.append(acc[:head_dim] / acc[head_dim:head_dim + 1])
        o_ref[_tile_rows(qi), :] = jnp.concatenate(outs, axis=0).T.astype(o_ref.dtype)
        return carry

    lax.fori_loop(0, nq, one, 0)


def _causal_items(nq, first_past):
    phases = [[(i, i - d) for i in range(d, nq)] for d in range(first_past)]
    phases.append([(i, j) for i in range(first_past, nq) for j in range(i - first_past + 1)])
    flat = [it for ph in phases for it in ph]
    tq = jnp.asarray([it[0] for it in flat], jnp.int32)
    tk = jnp.asarray([it[1] for it in flat], jnp.int32)
    return tq, tk, [len(ph) for ph in phases]


def _moba_kernel(tq_ref, tk_ref, qt_ref, k_ref, vt_ref, kmean_ref, bias_ref, o_ref,
                 kaug_ref, qaug_ref, s0_ref, s1_ref, t0_ref, t1_ref, m_ref, acc_ref,
                 *, seq, phases):
    nblk = seq // MOBA_BLOCK
    blk = lax.broadcasted_iota(jnp.int32, (seq, LANES), 0) // MOBA_BLOCK
    onehot = (blk == lax.broadcasted_iota(jnp.int32, (seq, LANES), 1)).astype(BF16)
    for pr in range(HEAD_GROUP // 2):
        kaug_ref[pr, :, :LANES] = k_ref[:, pr * LANES:(pr + 1) * LANES]
        kaug_ref[pr, :, LANES:] = onehot

    def select_blocks(i, carry):
        feat = lax.broadcasted_iota(jnp.int32, (LANES, ATTN_TILE), 0)
        row = lax.broadcasted_iota(jnp.int32, (nblk, ATTN_TILE), 0)
        past = row < i
        for h in range(HEAD_GROUP):
            pr, sub = divmod(h, 2)
            q_t = qt_ref[i, pr * LANES:(pr + 1) * LANES, :]
            in_head = (feat >= sub * A_HEAD_DIM) & (feat < (sub + 1) * A_HEAD_DIM)
            qh_t = jnp.where(in_head, q_t, jnp.zeros_like(q_t))
            kmean = kmean_ref[:, pr * LANES:(pr + 1) * LANES].astype(BF16)
            gate = jnp.dot(kmean, qh_t, preferred_element_type=F32)
            gate = jnp.where(past, gate, -jnp.inf)
            rank = jnp.zeros((nblk, ATTN_TILE), F32)
            for c in range(nblk):
                gc = gate[c:c + 1, :]
                beats = (gc > gate) | ((gc == gate) & (c < row))
                rank = rank + beats.astype(F32)
            keep = (past & (rank < MOBA_TOPK)) | (row == i)
            mask_t = jnp.where(keep, 0.0, NEG).astype(BF16)
            qaug_ref[i, h] = jnp.concatenate(
                [qh_t, mask_t, jnp.zeros((LANES - nblk, ATTN_TILE), BF16)], axis=0)
        return carry

    lax.fori_loop(0, nblk, select_blocks, 0)

    def stage(offset, bias_cols, first):
        def score(n, buf):
            qi, kj = tq_ref[offset + n], tk_ref[offset + n]
            for h in range(HEAD_GROUP):
                s_t = jnp.dot(kaug_ref[h // 2, _tile_rows(kj), :], qaug_ref[qi, h],
                              preferred_element_type=F32)
                if bias_cols is not None:
                    s_t = s_t + bias_ref[h, :, bias_cols]
                _store_scores(buf, h, s_t)

        def attend(n, buf):
            qi, kj = tq_ref[offset + n], tk_ref[offset + n]
            for h in range(HEAD_GROUP):
                _softmax_step(buf, h, vt_ref[kj, h * A_HEAD_DIM:(h + 1) * A_HEAD_DIM, :],
                              m_ref.at[qi, h], acc_ref.at[qi, h], first)

        return score, attend

    bufs = ((s0_ref, t0_ref), (s1_ref, t1_ref))
    n_own, n_prev, n_far = phases
    _pipeline(n_own, *stage(0, slice(0, MOBA_BLOCK), True), bufs)
    _pipeline(n_prev, *stage(n_own, slice(MOBA_BLOCK, 2 * MOBA_BLOCK), False), bufs)
    _pipeline(n_far, *stage(n_own + n_prev, None, False), bufs)
    _write_heads(acc_ref, o_ref, nblk, A_HEAD_DIM)


def _moba(qat, ka, vat, kmean, bias):
    b, seq, _ = ka.shape
    nq = seq // ATTN_TILE
    gw = HEAD_GROUP * A_HEAD_DIM
    tq, tk, phases = _causal_items(nq, 2)
    per_group = lambda bi, g, *_: (bi, 0, g)
    tiles = lambda bi, g, *_: (bi, 0, g, 0)
    return pl.pallas_call(
        functools.partial(_moba_kernel, seq=seq, phases=phases),
        grid_spec=pltpu.PrefetchScalarGridSpec(
            num_scalar_prefetch=2,
            grid=(b, A_HEADS // HEAD_GROUP),
            in_specs=[
                pl.BlockSpec((None, nq, gw, ATTN_TILE), tiles),
                pl.BlockSpec((None, seq, gw), per_group),
                pl.BlockSpec((None, nq, gw, MOBA_BLOCK), tiles),
                pl.BlockSpec((None, nq, gw), per_group),
                pl.BlockSpec((HEAD_GROUP, MOBA_BLOCK, 2 * MOBA_BLOCK), lambda bi, g, *_: (g, 0, 0)),
            ],
            out_specs=pl.BlockSpec((None, seq, gw), per_group),
            scratch_shapes=[
                pltpu.VMEM((HEAD_GROUP // 2, seq, 2 * LANES), BF16),
                pltpu.VMEM((nq, HEAD_GROUP, 2 * LANES, ATTN_TILE), BF16),
                pltpu.VMEM((HEAD_GROUP, ATTN_TILE, ATTN_TILE), F32),
                pltpu.VMEM((HEAD_GROUP, ATTN_TILE, ATTN_TILE), F32),
                pltpu.VMEM((HEAD_GROUP, 1, ATTN_TILE), F32),
                pltpu.VMEM((HEAD_GROUP, 1, ATTN_TILE), F32),
                pltpu.VMEM((nq, HEAD_GROUP, 1, ATTN_TILE), F32),
                pltpu.VMEM((nq, HEAD_GROUP, A_HEAD_DIM + SUM_ROWS, ATTN_TILE), F32),
            ]),
        out_shape=jax.ShapeDtypeStruct((b, seq, A_WIDTH), BF16),
        compiler_params=pltpu.CompilerParams(
            dimension_semantics=("arbitrary", "arbitrary"), vmem_limit_bytes=VMEM_LIMIT),
        name="moba_attention",
    )(tq, tk, qat, ka, vat, kmean, bias)


def _mla_kernel(tq_ref, tk_ref, qt_ref, k_ref, vt_ref, o_ref, s0_ref, s1_ref, t0_ref, t1_ref,
                m_ref, acc_ref, *, nq, phases):
    def stage(offset, diagonal):
        def score(n, buf):
            qi, kj = tq_ref[offset + n], tk_ref[offset + n]
            for h in range(HEAD_GROUP):
                sl = slice(h * MLA_HEAD_PAD, (h + 1) * MLA_HEAD_PAD)
                s_t = jnp.dot(k_ref[_tile_rows(kj), sl], qt_ref[qi, sl, :],
                              preferred_element_type=F32)
                if diagonal:
                    key = lax.broadcasted_iota(jnp.int32, (ATTN_TILE, ATTN_TILE), 0)
                    qry = lax.broadcasted_iota(jnp.int32, (ATTN_TILE, ATTN_TILE), 1)
                    s_t = jnp.where(key <= qry, s_t, NEG)
                _store_scores(buf, h, s_t)

        def attend(n, buf):
            qi, kj = tq_ref[offset + n], tk_ref[offset + n]
            for h in range(HEAD_GROUP):
                _softmax_step(buf, h, vt_ref[kj, h * V_HEAD:(h + 1) * V_HEAD, :],
                              m_ref.at[qi, h], acc_ref.at[qi, h], diagonal)

        return score, attend

    bufs = ((s0_ref, t0_ref), (s1_ref, t1_ref))
    n_diag, n_past = phases
    _pipeline(n_diag, *stage(0, True), bufs)
    _pipeline(n_past, *stage(n_diag, False), bufs)
    _write_heads(acc_ref, o_ref, nq, V_HEAD)


def _mla(qmt, km, vbt):
    b, seq, _ = km.shape
    nq = seq // ATTN_TILE
    qw = HEAD_GROUP * MLA_HEAD_PAD
    vw = HEAD_GROUP * V_HEAD
    tq, tk, phases = _causal_items(nq, 1)
    per_group = lambda bi, g, *_: (bi, 0, g)
    tiles = lambda bi, g, *_: (bi, 0, g, 0)
    return pl.pallas_call(
        functools.partial(_mla_kernel, nq=nq, phases=phases),
        grid_spec=pltpu.PrefetchScalarGridSpec(
            num_scalar_prefetch=2,
            grid=(b, B_HEADS // HEAD_GROUP),
            in_specs=[
                pl.BlockSpec((None, nq, qw, ATTN_TILE), tiles),
                pl.BlockSpec((None, seq, qw), per_group),
                pl.BlockSpec((None, nq, vw, ATTN_TILE), tiles),
            ],
            out_specs=pl.BlockSpec((None, seq, vw), per_group),
            scratch_shapes=[
                pltpu.VMEM((HEAD_GROUP, ATTN_TILE, ATTN_TILE), F32),
                pltpu.VMEM((HEAD_GROUP, ATTN_TILE, ATTN_TILE), F32),
                pltpu.VMEM((HEAD_GROUP, 1, ATTN_TILE), F32),
                pltpu.VMEM((HEAD_GROUP, 1, ATTN_TILE), F32),
                pltpu.VMEM((nq, HEAD_GROUP, 1, ATTN_TILE), F32),
                pltpu.VMEM((nq, HEAD_GROUP, V_HEAD + SUM_ROWS, ATTN_TILE), F32),
            ]),
        out_shape=jax.ShapeDtypeStruct((b, seq, B_WIDTH), BF16),
        compiler_params=pltpu.CompilerParams(
            dimension_semantics=("arbitrary", "arbitrary"), vmem_limit_bytes=VMEM_LIMIT),
        name="mla_attention",
    )(tq, tk, qmt, km, vbt)


def _out_kernel(x_ref, oa_ref, ob_ref, gattn_ref, wg_ref, wpa_ref, wpb_ref, wout_ref,
                gmlp_ref, wup_ref, wdn_ref, gfin_ref, o_ref):
    x = x_ref[...]
    n = _rms(x, gattn_ref[...]).astype(BF16)
    gates = jnp.dot(n, wg_ref[...], preferred_element_type=F32)
    pa = jnp.dot(oa_ref[...], wpa_ref[...], preferred_element_type=F32)
    pb = jnp.dot(ob_ref[...], wpb_ref[...], preferred_element_type=F32)
    merged = (jax.nn.sigmoid(gates[:, :D_MODEL]) * pa
              + jax.nn.sigmoid(gates[:, D_MODEL:]) * pb).astype(BF16)
    h = x + jnp.dot(merged, wout_ref[...], preferred_element_type=F32)
    m = _rms(h, gmlp_ref[...]).astype(BF16)
    for c in range(D_FF // FF_CHUNK):
        cols = slice(c * FF_CHUNK, (c + 1) * FF_CHUNK)
        up = jnp.dot(m, wup_ref[:, cols], preferred_element_type=F32)
        act = jnp.square(jnp.maximum(up, 0.0)).astype(BF16)
        h = h + jnp.dot(act, wdn_ref[cols, :], preferred_element_type=F32)
    o_ref[...] = _rms(h, gfin_ref[...])


def _out_block(x2, oa, ob, g_attn, wg, wpa, wpb, wout, g_mlp, wup, wdn, g_fin):
    n_rows = x2.shape[0]
    tm = ROW_TILE
    row = lambda i: (i, 0)
    return pl.pallas_call(
        _out_kernel,
        grid=(n_rows // tm,),
        in_specs=[
            pl.BlockSpec((tm, D_MODEL), row),
            pl.BlockSpec((tm, A_WIDTH), row), pl.BlockSpec((tm, B_WIDTH), row),
            _const_spec(g_attn.shape), _const_spec(wg.shape), _const_spec(wpa.shape),
            _const_spec(wpb.shape), _const_spec(wout.shape), _const_spec(g_mlp.shape),
            _const_spec(wup.shape), _const_spec(wdn.shape), _const_spec(g_fin.shape),
        ],
        out_specs=pl.BlockSpec((tm, D_MODEL), row),
        out_shape=jax.ShapeDtypeStruct((n_rows, D_MODEL), F32),
        compiler_params=pltpu.CompilerParams(
            dimension_semantics=("arbitrary",), vmem_limit_bytes=VMEM_LIMIT),
        name="out_mlp",
    )(x2, oa, ob, g_attn, wg, wpa, wpb, wout, g_mlp, wup, wdn, g_fin)


def _rope_tables(seq):
    half = QK_ROPE // 2
    inv_freq = ROPE_THETA ** (-jnp.arange(half, dtype=F32) / half)
    ang = jnp.arange(seq).astype(F32)[:, None] * inv_freq[None, :]
    cos2 = jnp.tile(jnp.cos(ang), (1, 2))
    sin2 = jnp.tile(jnp.sin(ang), (1, 2))
    pad = jnp.zeros((seq, MLA_HEAD_PAD - QK_NOPE - QK_ROPE), F32)
    cos_t = jnp.concatenate([jnp.ones((seq, QK_NOPE), F32), cos2, pad], axis=1)
    sin_t = jnp.concatenate([jnp.zeros((seq, QK_NOPE), F32), sin2, pad], axis=1)
    return cos_t, sin_t


def _rotate_half_cols(w):
    half = QK_ROPE // 2
    return jnp.concatenate([-w[..., half:], w[..., :half]], axis=-1)


def _pad_heads(w_nope, w_rope):
    k, h, _ = w_nope.shape
    if w_rope is None:
        w_rope = jnp.zeros((k, h, QK_ROPE), w_nope.dtype)
    pad = jnp.zeros((k, h, MLA_HEAD_PAD - QK_NOPE - QK_ROPE), w_nope.dtype)
    return jnp.concatenate([w_nope, w_rope, pad], axis=-1).reshape(k, h * MLA_HEAD_PAD)


def kernel(x, w_in, rel_bias, mla_q_norm, w_uq, mla_kv_norm, w_uk, w_uv, w_proj_a, w_proj_b,
           w_out, norm_attn, norm_mlp, w_mlp_up, w_mlp_down, norm_final):
    b, seq, d = x.shape
    assert d == D_MODEL and seq % ROW_TILE == 0 and seq // MOBA_BLOCK <= LANES
    assert w_in.shape[0] == 1, "single-layer block"
    x2 = x.reshape(b * seq, d)

    w = w_in[0]
    o_k = A_WIDTH
    o_v = 2 * A_WIDTH
    o_cq = 3 * A_WIDTH
    o_ckv = o_cq + Q_LORA
    o_kr = o_ckv + KV_LORA
    o_g = o_kr + QK_ROPE
    wk = w[:, o_k:o_v].astype(BF16)
    wqvt = jnp.concatenate([w[:, :o_k], w[:, o_v:o_cq]], axis=1).T.astype(BF16)
    w_kr = w[:, o_kr:o_g]
    lane_pad = lambda a: jnp.pad(a, ((0, 0), (QK_NOPE, MLA_HEAD_PAD - QK_NOPE - QK_ROPE)))
    wc = jnp.concatenate(
        [w[:, o_cq:o_kr], lane_pad(w_kr), lane_pad(_rotate_half_cols(w_kr))], axis=1).astype(BF16)
    wg = w[:, o_g:].astype(BF16)

    uq = w_uq[0].reshape(Q_LORA, B_HEADS, QK_NOPE + QK_ROPE)
    uq_nope, uq_rope = uq[..., :QK_NOPE], uq[..., QK_NOPE:]
    wuqt = _pad_heads(uq_nope, uq_rope).T.astype(BF16)
    wuqrt = _pad_heads(jnp.zeros_like(uq_nope), _rotate_half_cols(uq_rope)).T.astype(BF16)
    wuk = _pad_heads(w_uk[0].reshape(KV_LORA, B_HEADS, QK_NOPE), None).astype(BF16)
    wuvt = w_uv[0].T.astype(BF16)

    cos_t, sin_t = _rope_tables(seq)
    q_scale = (QK_NOPE + QK_ROPE) ** -0.5 * LOG2E
    row2 = lambda a: a.reshape(1, -1)

    qat, ka, vat, kmean, qmt, km, vbt = _inproj(
        x2, row2(norm_attn[0]), wk, wqvt, wc, row2(mla_q_norm[0]), row2(mla_kv_norm[0]),
        wuqt, wuqrt, wuk, wuvt, (cos_t * q_scale).T, (sin_t * q_scale).T, cos_t, sin_t, b, seq)

    as_seq = lambda a: a.reshape(b, seq, a.shape[-1])
    bias = _bias_tiles(rel_bias)
    oa = _moba(qat, as_seq(ka), vat, kmean.reshape(b, seq // MOBA_BLOCK, A_WIDTH), bias)
    ob = _mla(qmt, as_seq(km), vbt)

    out = _out_block(
        x2, oa.reshape(b * seq, A_WIDTH), ob.reshape(b * seq, B_WIDTH), row2(norm_attn[0]),
        wg, w_proj_a[0].astype(BF16), w_proj_b[0].astype(BF16), w_out[0].astype(BF16),
        row2(norm_mlp[0]), w_mlp_up[0].astype(BF16), w_mlp_down[0].astype(BF16),
        row2(norm_final))
    return out.reshape(b, seq, d)
```

```python
import functools
import math

import jax
import jax.numpy as jnp
from jax import lax
from jax.experimental import pallas as pl
from jax.experimental.pallas import tpu as pltpu

D_MODEL = 1024
A_HEADS = 8
A_HEAD_DIM = 64
A_WIDTH = A_HEADS * A_HEAD_DIM
MOBA_BLOCK = 256
MOBA_TOPK = 3
REL_BUCKETS = 32
REL_MAX_DIST = 128
B_HEADS = 8
QK_NOPE = 64
QK_ROPE = 32
V_HEAD = 64
B_WIDTH = B_HEADS * V_HEAD
Q_LORA = 384
KV_LORA = 256
ROPE_THETA = 10000.0
D_FF = 4 * D_MODEL
EPS = 1e-6
NEG = -1e30
LOG2E = math.log2(math.e)

LANES = 128
MLA_HEAD_PAD = 128
ATTN_TILE = 256
HEAD_GROUP = 4
SUM_ROWS = 16
PIPELINE_UNROLL = 8
ROW_TILE = 512
FF_CHUNK = 1024
VMEM_LIMIT = 56 * 1024 * 1024

F32 = jnp.float32
BF16 = jnp.bfloat16
NT_DIMS = (((1,), (1,)), ((), ()))


def _rms(xf, g):
    y = xf * lax.rsqrt(jnp.mean(xf * xf, axis=-1, keepdims=True) + EPS)
    return y * g


def _const_spec(shape):
    zeros = (0,) * len(shape)
    return pl.BlockSpec(shape, lambda *_: zeros, pipeline_mode=pl.Buffered(1))


def _bias_kernel(rel_ref, o_ref):
    h = pl.program_id(0)
    shape = (MOBA_BLOCK, 2 * MOBA_BLOCK)
    r = lax.broadcasted_iota(jnp.int32, shape, 0)
    c = lax.broadcasted_iota(jnp.int32, shape, 1)
    d = c - r
    dist = jnp.maximum(d, 0)
    max_exact = REL_BUCKETS // 2
    df = jnp.maximum(dist, 1).astype(F32)
    large = max_exact + (jnp.log(df / max_exact) / math.log(REL_MAX_DIST / max_exact)
                         * (REL_BUCKETS - max_exact)).astype(jnp.int32)
    large = jnp.minimum(large, REL_BUCKETS - 1)
    bucket = jnp.where(dist < max_exact, dist, large)
    val = jnp.zeros(shape, F32)
    for b in range(REL_BUCKETS):
        val = jnp.where(bucket == b, rel_ref[b, h], val)
    o_ref[...] = jnp.where(d >= 0, (val - rel_ref[REL_BUCKETS - 1, h]) * LOG2E, NEG)


def _bias_tiles(rel_bias):
    return pl.pallas_call(
        _bias_kernel,
        grid=(A_HEADS,),
        in_specs=[pl.BlockSpec(memory_space=pltpu.SMEM)],
        out_specs=pl.BlockSpec((None, MOBA_BLOCK, 2 * MOBA_BLOCK), lambda h: (h, 0, 0)),
        out_shape=jax.ShapeDtypeStruct((A_HEADS, MOBA_BLOCK, 2 * MOBA_BLOCK), F32),
        name="moba_bias_tiles",
    )(rel_bias)


def _inproj_kernel(x_ref, g_ref, wk_ref, wqvt_ref, wc_ref, qn_ref, kvn_ref, wuqt_ref, wuqrt_ref,
                   wuk_ref, wuvt_ref, cqt_ref, sqt_ref, ck_ref, sk_ref,
                   qat_ref, ka_ref, vat_ref, kmean_ref, qmt_ref, km_ref, vbt_ref):
    n = _rms(x_ref[...], g_ref[...]).astype(BF16)
    k = jnp.dot(n, wk_ref[...], preferred_element_type=F32)
    ka_ref[...] = k.astype(BF16)
    nblk = ROW_TILE // ATTN_TILE
    kmean_ref[0] = jnp.sum(k.reshape(nblk, MOBA_BLOCK, A_WIDTH), axis=1) * (1.0 / MOBA_BLOCK)

    c = jnp.dot(n, wc_ref[...], preferred_element_type=F32)
    cq = _rms(c[:, :Q_LORA], qn_ref[...]).astype(BF16)
    ckv = _rms(c[:, Q_LORA:Q_LORA + KV_LORA], kvn_ref[...]).astype(BF16)
    kr = c[:, Q_LORA + KV_LORA:Q_LORA + KV_LORA + LANES]
    krr = c[:, Q_LORA + KV_LORA + LANES:]
    for t in range(nblk):
        rows = slice(t * ATTN_TILE, (t + 1) * ATTN_TILE)
        qv_t = lax.dot_general(wqvt_ref[...], n[rows], NT_DIMS,
                               preferred_element_type=F32)
        qat_ref[t] = (qv_t[:A_WIDTH] * (A_HEAD_DIM ** -0.5 * LOG2E)).astype(BF16)
        vat_ref[t] = qv_t[A_WIDTH:].astype(BF16)
        vbt_ref[t] = lax.dot_general(wuvt_ref[...], ckv[rows], NT_DIMS,
                                     preferred_element_type=F32).astype(BF16)
        q_t = lax.dot_general(wuqt_ref[...], cq[rows], NT_DIMS,
                              preferred_element_type=F32)
        qr_t = lax.dot_general(wuqrt_ref[...], cq[rows], NT_DIMS, preferred_element_type=F32)
        cq_t, sq_t = cqt_ref[:, rows], sqt_ref[:, rows]
        for h in range(B_HEADS):
            sl = slice(h * MLA_HEAD_PAD, (h + 1) * MLA_HEAD_PAD)
            qmt_ref[t, sl, :] = (q_t[sl] * cq_t + qr_t[sl] * sq_t).astype(BF16)
    kn = jnp.dot(ckv, wuk_ref[...], preferred_element_type=F32)
    k_rope = kr * ck_ref[...] + krr * sk_ref[...]
    for h in range(B_HEADS):
        sl = slice(h * MLA_HEAD_PAD, (h + 1) * MLA_HEAD_PAD)
        km_ref[:, sl] = (kn[:, sl] + k_rope).astype(BF16)


def _inproj(x2, g_attn, wk, wqvt, wc, qn, kvn, wuqt, wuqrt, wuk, wuvt, cq_tt, sq_tt, ck_t, sk_t,
            batch, seq):
    n_rows = x2.shape[0]
    tm = ROW_TILE
    steps = n_rows // tm
    per_seq = seq // tm
    nblk = tm // ATTN_TILE
    row = lambda i: (i, 0)
    tab = lambda i: (i % per_seq, 0)
    tab_t = lambda i: (0, i % per_seq)
    tile_map = lambda i: (i // per_seq, i % per_seq, 0, 0)
    bf = lambda w: jax.ShapeDtypeStruct((n_rows, w), BF16)
    tiles = lambda w: jax.ShapeDtypeStruct((batch, seq // ATTN_TILE, w, ATTN_TILE), BF16)
    tile_spec = lambda w: pl.BlockSpec((None, nblk, w, ATTN_TILE), tile_map)
    return pl.pallas_call(
        _inproj_kernel,
        grid=(steps,),
        in_specs=[
            pl.BlockSpec((tm, D_MODEL), row),
            _const_spec(g_attn.shape), _const_spec(wk.shape), _const_spec(wqvt.shape),
            _const_spec(wc.shape), _const_spec(qn.shape), _const_spec(kvn.shape),
            _const_spec(wuqt.shape), _const_spec(wuqrt.shape), _const_spec(wuk.shape),
            _const_spec(wuvt.shape),
            pl.BlockSpec((LANES, tm), tab_t), pl.BlockSpec((LANES, tm), tab_t),
            pl.BlockSpec((tm, LANES), tab), pl.BlockSpec((tm, LANES), tab),
        ],
        out_specs=[
            tile_spec(A_WIDTH), pl.BlockSpec((tm, A_WIDTH), row), tile_spec(A_WIDTH),
            pl.BlockSpec((1, nblk, A_WIDTH), lambda i: (i, 0, 0)),
            tile_spec(B_HEADS * MLA_HEAD_PAD),
            pl.BlockSpec((tm, B_HEADS * MLA_HEAD_PAD), row),
            tile_spec(B_WIDTH),
        ],
        out_shape=[
            tiles(A_WIDTH), bf(A_WIDTH), tiles(A_WIDTH),
            jax.ShapeDtypeStruct((steps, nblk, A_WIDTH), F32),
            tiles(B_HEADS * MLA_HEAD_PAD), bf(B_HEADS * MLA_HEAD_PAD), tiles(B_WIDTH),
        ],
        compiler_params=pltpu.CompilerParams(
            dimension_semantics=("arbitrary",), vmem_limit_bytes=VMEM_LIMIT),
        name="in_proj",
    )(x2, g_attn, wk, wqvt, wc, qn, kvn, wuqt, wuqrt, wuk, wuvt, cq_tt, sq_tt, ck_t, sk_t)


def _sum_rows():
    r = lax.broadcasted_iota(jnp.int32, (SUM_ROWS, ATTN_TILE), 0)
    return (r == 0).astype(BF16)


def _store_scores(buf, h, s_t):
    s_ref, max_ref = buf
    s_ref[h] = s_t
    max_ref[h] = jnp.max(s_t, axis=0, keepdims=True)


def _softmax_step(buf, h, v_t, m_ref, acc_ref, first):
    s_ref, max_ref = buf[0].at[h], buf[1].at[h]
    s_max = max_ref[...]
    if first:
        m_new = s_max
    else:
        m_old = m_ref[...]
        m_new = jnp.maximum(m_old, s_max)
    v_aug = jnp.concatenate([v_t, _sum_rows()], axis=0)
    p_t = jnp.exp2(s_ref[...] - m_new).astype(BF16)
    pv = jnp.dot(v_aug, p_t, preferred_element_type=F32)
    if first:
        acc_ref[...] = pv
    else:
        acc_ref[...] = jnp.exp2(m_old - m_new) * acc_ref[...] + pv
    m_ref[...] = m_new


def _pipeline(n_items, score, attend, bufs):
    score(0, bufs[0])
    n_loops = (n_items - 1) // PIPELINE_UNROLL

    def body(p, carry):
        for t in range(PIPELINE_UNROLL):
            n = PIPELINE_UNROLL * p + t
            score(n + 1, bufs[(t + 1) % 2])
            attend(n, bufs[t % 2])
        return carry

    lax.fori_loop(0, n_loops, body, 0)
    done = PIPELINE_UNROLL * n_loops
    for n in range(done, n_items):
        if n + 1 < n_items:
            score(n + 1, bufs[(n + 1 - done) % 2])
        attend(n, bufs[(n - done) % 2])


def _tile_rows(j):
    return pl.ds(pl.multiple_of(j * ATTN_TILE, ATTN_TILE), ATTN_TILE)


def _write_heads(acc_ref, o_ref, nq, head_dim):
    def one(qi, carry):
        outs = []
        for h in range(HEAD_GROUP):
            acc = acc_ref[qi, h]
            outs.append(acc[:head_dim] / acc[head_dim:head_dim + 1])
        o_ref[_tile_rows(qi), :] = jnp.concatenate(outs, axis=0).T.astype(o_ref.dtype)
        return carry

    lax.fori_loop(0, nq, one, 0)


def _causal_items(nq, first_past):
    phases = [[(i, i - d) for i in range(d, nq)] for d in range(first_past)]
    phases.append([(i, j) for i in range(first_past, nq) for j in range(i - first_past + 1)])
    flat = [it for ph in phases for it in ph]
    tq = jnp.asarray([it[0] for it in flat], jnp.int32)
    tk = jnp.asarray([it[1] for it in flat], jnp.int32)
    return tq, tk, [len(ph) for ph in phases]


def _moba_kernel(tq_ref, tk_ref, qt_ref, k_ref, vt_ref, kmean_ref, bias_ref, o_ref,
                 kaug_ref, qaug_ref, s0_ref, s1_ref, t0_ref, t1_ref, m_ref, acc_ref,
                 *, seq, phases):
    nblk = seq // MOBA_BLOCK
    blk = lax.broadcasted_iota(jnp.int32, (seq, LANES), 0) // MOBA_BLOCK
    onehot = (blk == lax.broadcasted_iota(jnp.int32, (seq, LANES), 1)).astype(BF16)
    for pr in range(HEAD_GROUP // 2):
        kaug_ref[pr, :, :LANES] = k_ref[:, pr * LANES:(pr + 1) * LANES]
        kaug_ref[pr, :, LANES:] = onehot

    def select_blocks(i, carry):
        feat = lax.broadcasted_iota(jnp.int32, (LANES, ATTN_TILE), 0)
        row = lax.broadcasted_iota(jnp.int32, (nblk, ATTN_TILE), 0)
        past = row < i
        for h in range(HEAD_GROUP):
            pr, sub = divmod(h, 2)
            q_t = qt_ref[i, pr * LANES:(pr + 1) * LANES, :]
            in_head = (feat >= sub * A_HEAD_DIM) & (feat < (sub + 1) * A_HEAD_DIM)
            qh_t = jnp.where(in_head, q_t, jnp.zeros_like(q_t))
            kmean = kmean_ref[:, pr * LANES:(pr + 1) * LANES].astype(BF16)
            gate = jnp.dot(kmean, qh_t, preferred_element_type=F32)
            gate = jnp.where(past, gate, -jnp.inf)
            rank = jnp.zeros((nblk, ATTN_TILE), F32)
            for c in range(nblk):
                gc = gate[c:c + 1, :]
                beats = (gc > gate) | ((gc == gate) & (c < row))
                rank = rank + beats.astype(F32)
            keep = (past & (rank < MOBA_TOPK)) | (row == i)
            mask_t = jnp.where(keep, 0.0, NEG).astype(BF16)
            qaug_ref[i, h] = jnp.concatenate(
                [qh_t, mask_t, jnp.zeros((LANES - nblk, ATTN_TILE), BF16)], axis=0)
        return carry

    lax.fori_loop(0, nblk, select_blocks, 0)

    def stage(offset, bias_cols, first):
        def score(n, buf):
            qi, kj = tq_ref[offset + n], tk_ref[offset + n]
            for h in range(HEAD_GROUP):
                s_t = jnp.dot(kaug_ref[h // 2, _tile_rows(kj), :], qaug_ref[qi, h],
                              preferred_element_type=F32)
                if bias_cols is not None:
                    s_t = s_t + bias_ref[h, :, bias_cols]
                _store_scores(buf, h, s_t)

        def attend(n, buf):
            qi, kj = tq_ref[offset + n], tk_ref[offset + n]
            for h in range(HEAD_GROUP):
                _softmax_step(buf, h, vt_ref[kj, h * A_HEAD_DIM:(h + 1) * A_HEAD_DIM, :],
                              m_ref.at[qi, h], acc_ref.at[qi, h], first)

        return score, attend

    bufs = ((s0_ref, t0_ref), (s1_ref, t1_ref))
    n_own, n_prev, n_far = phases
    _pipeline(n_own, *stage(0, slice(0, MOBA_BLOCK), True), bufs)
    _pipeline(n_prev, *stage(n_own, slice(MOBA_BLOCK, 2 * MOBA_BLOCK), False), bufs)
    _pipeline(n_far, *stage(n_own + n_prev, None, False), bufs)
    _write_heads(acc_ref, o_ref, nblk, A_HEAD_DIM)


def _moba(qat, ka, vat, kmean, bias):
    b, seq, _ = ka.shape
    nq = seq // ATTN_TILE
    gw = HEAD_GROUP * A_HEAD_DIM
    tq, tk, phases = _causal_items(nq, 2)
    per_group = lambda bi, g, *_: (bi, 0, g)
    tiles = lambda bi, g, *_: (bi, 0, g, 0)
    return pl.pallas_call(
        functools.partial(_moba_kernel, seq=seq, phases=phases),
        grid_spec=pltpu.PrefetchScalarGridSpec(
            num_scalar_prefetch=2,
            grid=(b, A_HEADS // HEAD_GROUP),
            in_specs=[
                pl.BlockSpec((None, nq, gw, ATTN_TILE), tiles),
                pl.BlockSpec((None, seq, gw), per_group),
                pl.BlockSpec((None, nq, gw, MOBA_BLOCK), tiles),
                pl.BlockSpec((None, nq, gw), per_group),
                pl.BlockSpec((HEAD_GROUP, MOBA_BLOCK, 2 * MOBA_BLOCK), lambda bi, g, *_: (g, 0, 0)),
            ],
            out_specs=pl.BlockSpec((None, seq, gw), per_group),
            scratch_shapes=[
                pltpu.VMEM((HEAD_GROUP // 2, seq, 2 * LANES), BF16),
                pltpu.VMEM((nq, HEAD_GROUP, 2 * LANES, ATTN_TILE), BF16),
                pltpu.VMEM((HEAD_GROUP, ATTN_TILE, ATTN_TILE), F32),
                pltpu.VMEM((HEAD_GROUP, ATTN_TILE, ATTN_TILE), F32),
                pltpu.VMEM((HEAD_GROUP, 1, ATTN_TILE), F32),
                pltpu.VMEM((HEAD_GROUP, 1, ATTN_TILE), F32),
                pltpu.VMEM((nq, HEAD_GROUP, 1, ATTN_TILE), F32),
                pltpu.VMEM((nq, HEAD_GROUP, A_HEAD_DIM + SUM_ROWS, ATTN_TILE), F32),
            ]),
        out_shape=jax.ShapeDtypeStruct((b, seq, A_WIDTH), BF16),
        compiler_params=pltpu.CompilerParams(
            dimension_semantics=("arbitrary", "arbitrary"), vmem_limit_bytes=VMEM_LIMIT),
        name="moba_attention",
    )(tq, tk, qat, ka, vat, kmean, bias)


def _mla_kernel(tq_ref, tk_ref, qt_ref, k_ref, vt_ref, o_ref, s0_ref, s1_ref, t0_ref, t1_ref,
                m_ref, acc_ref, *, nq, phases):
    def stage(offset, diagonal):
        def score(n, buf):
            qi, kj = tq_ref[offset + n], tk_ref[offset + n]
            for h in range(HEAD_GROUP):
                sl = slice(h * MLA_HEAD_PAD, (h + 1) * MLA_HEAD_PAD)
                s_t = jnp.dot(k_ref[_tile_rows(kj), sl], qt_ref[qi, sl, :],
                              preferred_element_type=F32)
                if diagonal:
                    key = lax.broadcasted_iota(jnp.int32, (ATTN_TILE, ATTN_TILE), 0)
                    qry = lax.broadcasted_iota(jnp.int32, (ATTN_TILE, ATTN_TILE), 1)
                    s_t = jnp.where(key <= qry, s_t, NEG)
                _store_scores(buf, h, s_t)

        def attend(n, buf):
            qi, kj = tq_ref[offset + n], tk_ref[offset + n]
            for h in range(HEAD_GROUP):
                _softmax_step(buf, h, vt_ref[kj, h * V_HEAD:(h + 1) * V_HEAD, :],
                              m_ref.at[qi, h], acc_ref.at[qi, h], diagonal)

        return score, attend

    bufs = ((s0_ref, t0_ref), (s1_ref, t1_ref))
    n_diag, n_past = phases
    _pipeline(n_diag, *stage(0, True), bufs)
    _pipeline(n_past, *stage(n_diag, False), bufs)
    _write_heads(acc_ref, o_ref, nq, V_HEAD)


def _mla(qmt, km, vbt):
    b, seq, _ = km.shape
    nq = seq // ATTN_TILE
    qw = HEAD_GROUP * MLA_HEAD_PAD
    vw = HEAD_GROUP * V_HEAD
    tq, tk, phases = _causal_items(nq, 1)
    per_group = lambda bi, g, *_: (bi, 0, g)
    tiles = lambda bi, g, *_: (bi, 0, g, 0)
    return pl.pallas_call(
        functools.partial(_mla_kernel, nq=nq, phases=phases),
        grid_spec=pltpu.PrefetchScalarGridSpec(
            num_scalar_prefetch=2,
            grid=(b, B_HEADS // HEAD_GROUP),
            in_specs=[
                pl.BlockSpec((None, nq, qw, ATTN_TILE), tiles),
                pl.BlockSpec((None, seq, qw), per_group),
                pl.BlockSpec((None, nq, vw, ATTN_TILE), tiles),
            ],
            out_specs=pl.BlockSpec((None, seq, vw), per_group),
            scratch_shapes=[
                pltpu.VMEM((HEAD_GROUP, ATTN_TILE, ATTN_TILE), F32),
                pltpu.VMEM((HEAD_GROUP, ATTN_TILE, ATTN_TILE), F32),
                pltpu.VMEM((HEAD_GROUP, 1, ATTN_TILE), F32),
                pltpu.VMEM((HEAD_GROUP, 1, ATTN_TILE), F32),
                pltpu.VMEM((nq, HEAD_GROUP, 1, ATTN_TILE), F32),
                pltpu.VMEM((nq, HEAD_GROUP, V_HEAD + SUM_ROWS, ATTN_TILE), F32),
            ]),
        out_shape=jax.ShapeDtypeStruct((b, seq, B_WIDTH), BF16),
        compiler_params=pltpu.CompilerParams(
            dimension_semantics=("arbitrary", "arbitrary"), vmem_limit_bytes=VMEM_LIMIT),
        name="mla_attention",
    )(tq, tk, qmt, km, vbt)


def _out_kernel(x_ref, oa_ref, ob_ref, gattn_ref, wg_ref, wpa_ref, wpb_ref, wout_ref,
                gmlp_ref, wup_ref, wdn_ref, gfin_ref, o_ref):
    x = x_ref[...]
    n = _rms(x, gattn_ref[...]).astype(BF16)
    gates = jnp.dot(n, wg_ref[...], preferred_element_type=F32)
    pa = jnp.dot(oa_ref[...], wpa_ref[...], preferred_element_type=F32)
    pb = jnp.dot(ob_ref[...], wpb_ref[...], preferred_element_type=F32)
    merged = (jax.nn.sigmoid(gates[:, :D_MODEL]) * pa
              + jax.nn.sigmoid(gates[:, D_MODEL:]) * pb).astype(BF16)
    h = x + jnp.dot(merged, wout_ref[...], preferred_element_type=F32)
    m = _rms(h, gmlp_ref[...]).astype(BF16)
    for c in range(D_FF // FF_CHUNK):
        cols = slice(c * FF_CHUNK, (c + 1) * FF_CHUNK)
        up = jnp.dot(m, wup_ref[:, cols], preferred_element_type=F32)
        act = jnp.square(jnp.maximum(up, 0.0)).astype(BF16)
        h = h + jnp.dot(act, wdn_ref[cols, :], preferred_element_type=F32)
    o_ref[...] = _rms(h, gfin_ref[...])


def _out_block(x2, oa, ob, g_attn, wg, wpa, wpb, wout, g_mlp, wup, wdn, g_fin):
    n_rows = x2.shape[0]
    tm = ROW_TILE
    row = lambda i: (i, 0)
    return pl.pallas_call(
        _out_kernel,
        grid=(n_rows // tm,),
        in_specs=[
            pl.BlockSpec((tm, D_MODEL), row),
            pl.BlockSpec((tm, A_WIDTH), row), pl.BlockSpec((tm, B_WIDTH), row),
            _const_spec(g_attn.shape), _const_spec(wg.shape), _const_spec(wpa.shape),
            _const_spec(wpb.shape), _const_spec(wout.shape), _const_spec(g_mlp.shape),
            _const_spec(wup.shape), _const_spec(wdn.shape), _const_spec(g_fin.shape),
        ],
        out_specs=pl.BlockSpec((tm, D_MODEL), row),
        out_shape=jax.ShapeDtypeStruct((n_rows, D_MODEL), F32),
        compiler_params=pltpu.CompilerParams(
            dimension_semantics=("arbitrary",), vmem_limit_bytes=VMEM_LIMIT),
        name="out_mlp",
    )(x2, oa, ob, g_attn, wg, wpa, wpb, wout, g_mlp, wup, wdn, g_fin)


def _rope_tables(seq):
    half = QK_ROPE // 2
    inv_freq = ROPE_THETA ** (-jnp.arange(half, dtype=F32) / half)
    ang = jnp.arange(seq).astype(F32)[:, None] * inv_freq[None, :]
    cos2 = jnp.tile(jnp.cos(ang), (1, 2))
    sin2 = jnp.tile(jnp.sin(ang), (1, 2))
    pad = jnp.zeros((seq, MLA_HEAD_PAD - QK_NOPE - QK_ROPE), F32)
    cos_t = jnp.concatenate([jnp.ones((seq, QK_NOPE), F32), cos2, pad], axis=1)
    sin_t = jnp.concatenate([jnp.zeros((seq, QK_NOPE), F32), sin2, pad], axis=1)
    return cos_t, sin_t


def _rotate_half_cols(w):
    half = QK_ROPE // 2
    return jnp.concatenate([-w[..., half:], w[..., :half]], axis=-1)


def _pad_heads(w_nope, w_rope):
    k, h, _ = w_nope.shape
    if w_rope is None:
        w_rope = jnp.zeros((k, h, QK_ROPE), w_nope.dtype)
    pad = jnp.zeros((k, h, MLA_HEAD_PAD - QK_NOPE - QK_ROPE), w_nope.dtype)
    return jnp.concatenate([w_nope, w_rope, pad], axis=-1).reshape(k, h * MLA_HEAD_PAD)


def kernel(x, w_in, rel_bias, mla_q_norm, w_uq, mla_kv_norm, w_uk, w_uv, w_proj_a, w_proj_b,
           w_out, norm_attn, norm_mlp, w_mlp_up, w_mlp_down, norm_final):
    b, seq, d = x.shape
    assert d == D_MODEL and seq % ROW_TILE == 0 and seq // MOBA_BLOCK <= LANES
    assert w_in.shape[0] == 1, "single-layer block"
    x2 = x.reshape(b * seq, d)

    w = w_in[0]
    o_k = A_WIDTH
    o_v = 2 * A_WIDTH
    o_cq = 3 * A_WIDTH
    o_ckv = o_cq + Q_LORA
    o_kr = o_ckv + KV_LORA
    o_g = o_kr + QK_ROPE
    wk = w[:, o_k:o_v].astype(BF16)
    wqvt = jnp.concatenate([w[:, :o_k], w[:, o_v:o_cq]], axis=1).T.astype(BF16)
    w_kr = w[:, o_kr:o_g]
    lane_pad = lambda a: jnp.pad(a, ((0, 0), (QK_NOPE, MLA_HEAD_PAD - QK_NOPE - QK_ROPE)))
    wc = jnp.concatenate(
        [w[:, o_cq:o_kr], lane_pad(w_kr), lane_pad(_rotate_half_cols(w_kr))], axis=1).astype(BF16)
    wg = w[:, o_g:].astype(BF16)

    uq = w_uq[0].reshape(Q_LORA, B_HEADS, QK_NOPE + QK_ROPE)
    uq_nope, uq_rope = uq[..., :QK_NOPE], uq[..., QK_NOPE:]
    wuqt = _pad_heads(uq_nope, uq_rope).T.astype(BF16)
    wuqrt = _pad_heads(jnp.zeros_like(uq_nope), _rotate_half_cols(uq_rope)).T.astype(BF16)
    wuk = _pad_heads(w_uk[0].reshape(KV_LORA, B_HEADS, QK_NOPE), None).astype(BF16)
    wuvt = w_uv[0].T.astype(BF16)

    cos_t, sin_t = _rope_tables(seq)
    q_scale = (QK_NOPE + QK_ROPE) ** -0.5 * LOG2E
    row2 = lambda a: a.reshape(1, -1)

    qat, ka, vat, kmean, qmt, km, vbt = _inproj(
        x2, row2(norm_attn[0]), wk, wqvt, wc, row2(mla_q_norm[0]), row2(mla_kv_norm[0]),
        wuqt, wuqrt, wuk, wuvt, (cos_t * q_scale).T, (sin_t * q_scale).T, cos_t, sin_t, b, seq)

    as_seq = lambda a: a.reshape(b, seq, a.shape[-1])
    bias = _bias_tiles(rel_bias)
    oa = _moba(qat, as_seq(ka), vat, kmean.reshape(b, seq // MOBA_BLOCK, A_WIDTH), bias)
    ob = _mla(qmt, as_seq(km), vbt)

    out = _out_block(
        x2, oa.reshape(b * seq, A_WIDTH), ob.reshape(b * seq, B_WIDTH), row2(norm_attn[0]),
        wg, w_proj_a[0].astype(BF16), w_proj_b[0].astype(BF16), w_out[0].astype(BF16),
        row2(norm_mlp[0]), w_mlp_up[0].astype(BF16), w_mlp_down[0].astype(BF16),
        row2(norm_final))
    return out.reshape(b, seq, d)
```

```python
import functools
import math

import jax
import jax.numpy as jnp
from jax import lax
from jax.experimental import pallas as pl
from jax.experimental.pallas import tpu as pltpu

D_MODEL = 1024
A_HEADS = 8
A_HEAD_DIM = 64
A_WIDTH = A_HEADS * A_HEAD_DIM
MOBA_BLOCK = 256
MOBA_TOPK = 3
REL_BUCKETS = 32
REL_MAX_DIST = 128
B_HEADS = 8
QK_NOPE = 64
QK_ROPE = 32
V_HEAD = 64
B_WIDTH = B_HEADS * V_HEAD
Q_LORA = 384
KV_LORA = 256
ROPE_THETA = 10000.0
D_FF = 4 * D_MODEL
EPS = 1e-6
NEG = -1e30
LOG2E = math.log2(math.e)

LANES = 128
MLA_HEAD_PAD = 128
ATTN_TILE = 256
HEAD_GROUP = 4
SUM_ROWS = 16
PIPELINE_UNROLL = 8
ROW_TILE = 512
FF_CHUNK = 1024
VMEM_LIMIT = 56 * 1024 * 1024

F32 = jnp.float32
BF16 = jnp.bfloat16
NT_DIMS = (((1,), (1,)), ((), ()))


def _rms(xf, g):
    y = xf * lax.rsqrt(jnp.mean(xf * xf, axis=-1, keepdims=True) + EPS)
    return y * g


def _const_spec(shape):
    zeros = (0,) * len(shape)
    return pl.BlockSpec(shape, lambda *_: zeros, pipeline_mode=pl.Buffered(1))


def _bias_kernel(rel_ref, o_ref):
    h = pl.program_id(0)
    shape = (MOBA_BLOCK, 2 * MOBA_BLOCK)
    r = lax.broadcasted_iota(jnp.int32, shape, 0)
    c = lax.broadcasted_iota(jnp.int32, shape, 1)
    d = c - r
    dist = jnp.maximum(d, 0)
    max_exact = REL_BUCKETS // 2
    df = jnp.maximum(dist, 1).astype(F32)
    large = max_exact + (jnp.log(df / max_exact) / math.log(REL_MAX_DIST / max_exact)
                         * (REL_BUCKETS - max_exact)).astype(jnp.int32)
    large = jnp.minimum(large, REL_BUCKETS - 1)
    bucket = jnp.where(dist < max_exact, dist, large)
    val = jnp.zeros(shape, F32)
    for b in range(REL_BUCKETS):
        val = jnp.where(bucket == b, rel_ref[b, h], val)
    o_ref[...] = jnp.where(d >= 0, (val - rel_ref[REL_BUCKETS - 1, h]) * LOG2E, NEG)


def _bias_tiles(rel_bias):
    return pl.pallas_call(
        _bias_kernel,
        grid=(A_HEADS,),
        in_specs=[pl.BlockSpec(memory_space=pltpu.SMEM)],
        out_specs=pl.BlockSpec((None, MOBA_BLOCK, 2 * MOBA_BLOCK), lambda h: (h, 0, 0)),
        out_shape=jax.ShapeDtypeStruct((A_HEADS, MOBA_BLOCK, 2 * MOBA_BLOCK), F32),
        name="moba_bias_tiles",
    )(rel_bias)


def _inproj_kernel(x_ref, g_ref, wk_ref, wqvt_ref, wc_ref, qn_ref, kvn_ref, wuqt_ref, wuqrt_ref,
                   wuk_ref, wuvt_ref, cqt_ref, sqt_ref, ck_ref, sk_ref,
                   qat_ref, ka_ref, vat_ref, kmean_ref, qmt_ref, km_ref, vbt_ref):
    n = _rms(x_ref[...], g_ref[...]).astype(BF16)
    k = jnp.dot(n, wk_ref[...], preferred_element_type=F32)
    ka_ref[...] = k.astype(BF16)
    nblk = ROW_TILE // ATTN_TILE
    kmean_ref[0] = jnp.sum(k.reshape(nblk, MOBA_BLOCK, A_WIDTH), axis=1) * (1.0 / MOBA_BLOCK)

    c = jnp.dot(n, wc_ref[...], preferred_element_type=F32)
    cq = _rms(c[:, :Q_LORA], qn_ref[...]).astype(BF16)
    ckv = _rms(c[:, Q_LORA:Q_LORA + KV_LORA], kvn_ref[...]).astype(BF16)
    kr = c[:, Q_LORA + KV_LORA:Q_LORA + KV_LORA + LANES]
    krr = c[:, Q_LORA + KV_LORA + LANES:]
    for t in range(nblk):
        rows = slice(t * ATTN_TILE, (t + 1) * ATTN_TILE)
        qv_t = lax.dot_general(wqvt_ref[...], n[rows], NT_DIMS,
                               preferred_element_type=F32)
        qat_ref[t] = (qv_t[:A_WIDTH] * (A_HEAD_DIM ** -0.5 * LOG2E)).astype(BF16)
        vat_ref[t] = qv_t[A_WIDTH:].astype(BF16)
        vbt_ref[t] = lax.dot_general(wuvt_ref[...], ckv[rows], NT_DIMS,
                                     preferred_element_type=F32).astype(BF16)
        q_t = lax.dot_general(wuqt_ref[...], cq[rows], NT_DIMS,
                              preferred_element_type=F32)
        qr_t = lax.dot_general(wuqrt_ref[...], cq[rows], NT_DIMS, preferred_element_type=F32)
        cq_t, sq_t = cqt_ref[:, rows], sqt_ref[:, rows]
        for h in range(B_HEADS):
            sl = slice(h * MLA_HEAD_PAD, (h + 1) * MLA_HEAD_PAD)
            qmt_ref[t, sl, :] = (q_t[sl] * cq_t + qr_t[sl] * sq_t).astype(BF16)
    kn = jnp.dot(ckv, wuk_ref[...], preferred_element_type=F32)
    k_rope = kr * ck_ref[...] + krr * sk_ref[...]
    for h in range(B_HEADS):
        sl = slice(h * MLA_HEAD_PAD, (h + 1) * MLA_HEAD_PAD)
        km_ref[:, sl] = (kn[:, sl] + k_rope).astype(BF16)


def _inproj(x2, g_attn, wk, wqvt, wc, qn, kvn, wuqt, wuqrt, wuk, wuvt, cq_tt, sq_tt, ck_t, sk_t,
            batch, seq):
    n_rows = x2.shape[0]
    tm = ROW_TILE
    steps = n_rows // tm
    per_seq = seq // tm
    nblk = tm // ATTN_TILE
    row = lambda i: (i, 0)
    tab = lambda i: (i % per_seq, 0)
    tab_t = lambda i: (0, i % per_seq)
    tile_map = lambda i: (i // per_seq, i % per_seq, 0, 0)
    bf = lambda w: jax.ShapeDtypeStruct((n_rows, w), BF16)
    tiles = lambda w: jax.ShapeDtypeStruct((batch, seq // ATTN_TILE, w, ATTN_TILE), BF16)
    tile_spec = lambda w: pl.BlockSpec((None, nblk, w, ATTN_TILE), tile_map)
    return pl.pallas_call(
        _inproj_kernel,
        grid=(steps,),
        in_specs=[
            pl.BlockSpec((tm, D_MODEL), row),
            _const_spec(g_attn.shape), _const_spec(wk.shape), _const_spec(wqvt.shape),
            _const_spec(wc.shape), _const_spec(qn.shape), _const_spec(kvn.shape),
            _const_spec(wuqt.shape), _const_spec(wuqrt.shape), _const_spec(wuk.shape),
            _const_spec(wuvt.shape),
            pl.BlockSpec((LANES, tm), tab_t), pl.BlockSpec((LANES, tm), tab_t),
            pl.BlockSpec((tm, LANES), tab), pl.BlockSpec((tm, LANES), tab),
        ],
        out_specs=[
            tile_spec(A_WIDTH), pl.BlockSpec((tm, A_WIDTH), row), tile_spec(A_WIDTH),
            pl.BlockSpec((1, nblk, A_WIDTH), lambda i: (i, 0, 0)),
            tile_spec(B_HEADS * MLA_HEAD_PAD),
            pl.BlockSpec((tm, B_HEADS * MLA_HEAD_PAD), row),
            tile_spec(B_WIDTH),
        ],
        out_shape=[
            tiles(A_WIDTH), bf(A_WIDTH), tiles(A_WIDTH),
            jax.ShapeDtypeStruct((steps, nblk, A_WIDTH), F32),
            tiles(B_HEADS * MLA_HEAD_PAD), bf(B_HEADS * MLA_HEAD_PAD), tiles(B_WIDTH),
        ],
        compiler_params=pltpu.CompilerParams(
            dimension_semantics=("arbitrary",), vmem_limit_bytes=VMEM_LIMIT),
        name="in_proj",
    )(x2, g_attn, wk, wqvt, wc, qn, kvn, wuqt, wuqrt, wuk, wuvt, cq_tt, sq_tt, ck_t, sk_t)


def _sum_rows():
    r = lax.broadcasted_iota(jnp.int32, (SUM_ROWS, ATTN_TILE), 0)
    return (r == 0).astype(BF16)


def _store_scores(buf, h, s_t):
    s_ref, max_ref = buf
    for half in range(ATTN_TILE // LANES):
        s_ref[h, half] = s_t[:, half * LANES:(half + 1) * LANES]
    max_ref[h] = jnp.max(s_t, axis=0, keepdims=True)


def _softmax_step(buf, h, v_t, m_ref, acc_ref, first):
    s_ref, max_ref = buf[0].at[h], buf[1].at[h]
    s_max = max_ref[...]
    if first:
        m_new = s_max
    else:
        m_old = m_ref[...]
        m_new = jnp.maximum(m_old, s_max)
    v_aug = jnp.concatenate([v_t, _sum_rows()], axis=0)
    s_t = jnp.concatenate([s_ref[half] for half in range(ATTN_TILE // LANES)], axis=1)
    p_t = jnp.exp2(s_t - m_new).astype(BF16)
    pv = jnp.dot(v_aug, p_t, preferred_element_type=F32)
    if first:
        acc_ref[...] = pv
    else:
        acc_ref[...] = jnp.exp2(m_old - m_new) * acc_ref[...] + pv
    m_ref[...] = m_new


def _pipeline(n_items, score, attend, bufs):
    score(0, bufs[0])
    n_loops = (n_items - 1) // PIPELINE_UNROLL

    def body(p, carry):
        for t in range(PIPELINE_UNROLL):
            n = PIPELINE_UNROLL * p + t
            score(n + 1, bufs[(t + 1) % 2])
            attend(n, bufs[t % 2])
        return carry

    lax.fori_loop(0, n_loops, body, 0)
    done = PIPELINE_UNROLL * n_loops
    for n in range(done, n_items):
        if n + 1 < n_items:
            score(n + 1, bufs[(n + 1 - done) % 2])
        attend(n, bufs[(n - done) % 2])


def _tile_rows(j):
    return pl.ds(pl.multiple_of(j * ATTN_TILE, ATTN_TILE), ATTN_TILE)


def _write_heads(acc_ref, o_ref, nq, head_dim):
    def one(qi, carry):
        outs = []
        for h in range(HEAD_GROUP):
            acc = acc_ref[qi, h]
            outs.append(acc[:head_dim] / acc[head_dim:head_dim + 1])
        o_ref[_tile_rows(qi), :] = jnp.concatenate(outs, axis=0).T.astype(o_ref.dtype)
        return carry

    lax.fori_loop(0, nq, one, 0)


def _causal_items(nq, first_past):
    phases = [[(i, i - d) for i in range(d, nq)] for d in range(first_past)]
    phases.append([(i, j) for i in range(first_past, nq) for j in range(i - first_past + 1)])
    flat = [it for ph in phases for it in ph]
    tq = jnp.asarray([it[0] for it in flat], jnp.int32)
    tk = jnp.asarray([it[1] for it in flat], jnp.int32)
    return tq, tk, [len(ph) for ph in phases]


def _moba_kernel(tq_ref, tk_ref, qt_ref, k_ref, vt_ref, kmean_ref, bias_ref, o_ref,
                 kaug_ref, qaug_ref, s0_ref, s1_ref, t0_ref, t1_ref, m_ref, acc_ref,
                 *, seq, phases):
    nblk = seq // MOBA_BLOCK
    blk = lax.broadcasted_iota(jnp.int32, (seq, LANES), 0) // MOBA_BLOCK
    onehot = (blk == lax.broadcasted_iota(jnp.int32, (seq, LANES), 1)).astype(BF16)
    for pr in range(HEAD_GROUP // 2):
        kaug_ref[pr, :, :LANES] = k_ref[:, pr * LANES:(pr + 1) * LANES]
        kaug_ref[pr, :, LANES:] = onehot

    def select_blocks(i, carry):
        feat = lax.broadcasted_iota(jnp.int32, (LANES, ATTN_TILE), 0)
        row = lax.broadcasted_iota(jnp.int32, (nblk, ATTN_TILE), 0)
        past = row < i
        for h in range(HEAD_GROUP):
            pr, sub = divmod(h, 2)
            q_t = qt_ref[i, pr * LANES:(pr + 1) * LANES, :]
            in_head = (feat >= sub * A_HEAD_DIM) & (feat < (sub + 1) * A_HEAD_DIM)
            qh_t = jnp.where(in_head, q_t, jnp.zeros_like(q_t))
            kmean = kmean_ref[:, pr * LANES:(pr + 1) * LANES].astype(BF16)
            gate = jnp.dot(kmean, qh_t, preferred_element_type=F32)
            gate = jnp.where(past, gate, -jnp.inf)
            rank = jnp.zeros((nblk, ATTN_TILE), F32)
            for c in range(nblk):
                gc = gate[c:c + 1, :]
                beats = (gc > gate) | ((gc == gate) & (c < row))
                rank = rank + beats.astype(F32)
            keep = (past & (rank < MOBA_TOPK)) | (row == i)
            mask_t = jnp.where(keep, 0.0, NEG).astype(BF16)
            qaug_ref[i, h] = jnp.concatenate(
                [qh_t, mask_t, jnp.zeros((LANES - nblk, ATTN_TILE), BF16)], axis=0)
        return carry

    lax.fori_loop(0, nblk, select_blocks, 0)

    def stage(offset, bias_cols, first):
        def score(n, buf):
            qi, kj = tq_ref[offset + n], tk_ref[offset + n]
            for h in range(HEAD_GROUP):
                s_t = jnp.dot(kaug_ref[h // 2, _tile_rows(kj), :], qaug_ref[qi, h],
                              preferred_element_type=F32)
                if bias_cols is not None:
                    s_t = s_t + bias_ref[h, :, bias_cols]
                _store_scores(buf, h, s_t)

        def attend(n, buf):
            qi, kj = tq_ref[offset + n], tk_ref[offset + n]
            for h in range(HEAD_GROUP):
                _softmax_step(buf, h, vt_ref[kj, h * A_HEAD_DIM:(h + 1) * A_HEAD_DIM, :],
                              m_ref.at[qi, h], acc_ref.at[qi, h], first)

        return score, attend

    bufs = ((s0_ref, t0_ref), (s1_ref, t1_ref))
    n_own, n_prev, n_far = phases
    _pipeline(n_own, *stage(0, slice(0, MOBA_BLOCK), True), bufs)
    _pipeline(n_prev, *stage(n_own, slice(MOBA_BLOCK, 2 * MOBA_BLOCK), False), bufs)
    _pipeline(n_far, *stage(n_own + n_prev, None, False), bufs)
    _write_heads(acc_ref, o_ref, nblk, A_HEAD_DIM)


def _moba(qat, ka, vat, kmean, bias):
    b, seq, _ = ka.shape
    nq = seq // ATTN_TILE
    gw = HEAD_GROUP * A_HEAD_DIM
    tq, tk, phases = _causal_items(nq, 2)
    per_group = lambda bi, g, *_: (bi, 0, g)
    tiles = lambda bi, g, *_: (bi, 0, g, 0)
    return pl.pallas_call(
        functools.partial(_moba_kernel, seq=seq, phases=phases),
        grid_spec=pltpu.PrefetchScalarGridSpec(
            num_scalar_prefetch=2,
            grid=(b, A_HEADS // HEAD_GROUP),
            in_specs=[
                pl.BlockSpec((None, nq, gw, ATTN_TILE), tiles),
                pl.BlockSpec((None, seq, gw), per_group),
                pl.BlockSpec((None, nq, gw, MOBA_BLOCK), tiles),
                pl.BlockSpec((None, nq, gw), per_group),
                pl.BlockSpec((HEAD_GROUP, MOBA_BLOCK, 2 * MOBA_BLOCK), lambda bi, g, *_: (g, 0, 0)),
            ],
            out_specs=pl.BlockSpec((None, seq, gw), per_group),
            scratch_shapes=[
                pltpu.VMEM((HEAD_GROUP // 2, seq, 2 * LANES), BF16),
                pltpu.VMEM((nq, HEAD_GROUP, 2 * LANES, ATTN_TILE), BF16),
                pltpu.VMEM((HEAD_GROUP, ATTN_TILE // LANES, ATTN_TILE, LANES), F32),
                pltpu.VMEM((HEAD_GROUP, ATTN_TILE // LANES, ATTN_TILE, LANES), F32),
                pltpu.VMEM((HEAD_GROUP, 1, ATTN_TILE), F32),
                pltpu.VMEM((HEAD_GROUP, 1, ATTN_TILE), F32),
                pltpu.VMEM((nq, HEAD_GROUP, 1, ATTN_TILE), F32),
                pltpu.VMEM((nq, HEAD_GROUP, A_HEAD_DIM + SUM_ROWS, ATTN_TILE), F32),
            ]),
        out_shape=jax.ShapeDtypeStruct((b, seq, A_WIDTH), BF16),
        compiler_params=pltpu.CompilerParams(
            dimension_semantics=("arbitrary", "arbitrary"), vmem_limit_bytes=VMEM_LIMIT),
        name="moba_attention",
    )(tq, tk, qat, ka, vat, kmean, bias)


def _mla_kernel(tq_ref, tk_ref, qt_ref, k_ref, vt_ref, o_ref, s0_ref, s1_ref, t0_ref, t1_ref,
                m_ref, acc_ref, *, nq, phases):
    def stage(offset, diagonal):
        def score(n, buf):
            qi, kj = tq_ref[offset + n], tk_ref[offset + n]
            for h in range(HEAD_GROUP):
                sl = slice(h * MLA_HEAD_PAD, (h + 1) * MLA_HEAD_PAD)
                s_t = jnp.dot(k_ref[_tile_rows(kj), sl], qt_ref[qi, sl, :],
                              preferred_element_type=F32)
                if diagonal:
                    key = lax.broadcasted_iota(jnp.int32, (ATTN_TILE, ATTN_TILE), 0)
                    qry = lax.broadcasted_iota(jnp.int32, (ATTN_TILE, ATTN_TILE), 1)
                    s_t = jnp.where(key <= qry, s_t, NEG)
                _store_scores(buf, h, s_t)

        def attend(n, buf):
            qi, kj = tq_ref[offset + n], tk_ref[offset + n]
            for h in range(HEAD_GROUP):
                _softmax_step(buf, h, vt_ref[kj, h * V_HEAD:(h + 1) * V_HEAD, :],
                              m_ref.at[qi, h], acc_ref.at[qi, h], diagonal)

        return score, attend

    bufs = ((s0_ref, t0_ref), (s1_ref, t1_ref))
    n_diag, n_past = phases
    _pipeline(n_diag, *stage(0, True), bufs)
    _pipeline(n_past, *stage(n_diag, False), bufs)
    _write_heads(acc_ref, o_ref, nq, V_HEAD)


def _mla(qmt, km, vbt):
    b, seq, _ = km.shape
    nq = seq // ATTN_TILE
    qw = HEAD_GROUP * MLA_HEAD_PAD
    vw = HEAD_GROUP * V_HEAD
    tq, tk, phases = _causal_items(nq, 1)
    per_group = lambda bi, g, *_: (bi, 0, g)
    tiles = lambda bi, g, *_: (bi, 0, g, 0)
    return pl.pallas_call(
        functools.partial(_mla_kernel, nq=nq, phases=phases),
        grid_spec=pltpu.PrefetchScalarGridSpec(
            num_scalar_prefetch=2,
            grid=(b, B_HEADS // HEAD_GROUP),
            in_specs=[
                pl.BlockSpec((None, nq, qw, ATTN_TILE), tiles),
                pl.BlockSpec((None, seq, qw), per_group),
                pl.BlockSpec((None, nq, vw, ATTN_TILE), tiles),
            ],
            out_specs=pl.BlockSpec((None, seq, vw), per_group),
            scratch_shapes=[
                pltpu.VMEM((HEAD_GROUP, ATTN_TILE // LANES, ATTN_TILE, LANES), F32),
                pltpu.VMEM((HEAD_GROUP, ATTN_TILE // LANES, ATTN_TILE, LANES), F32),
                pltpu.VMEM((HEAD_GROUP, 1, ATTN_TILE), F32),
                pltpu.VMEM((HEAD_GROUP, 1, ATTN_TILE), F32),
                pltpu.VMEM((nq, HEAD_GROUP, 1, ATTN_TILE), F32),
                pltpu.VMEM((nq, HEAD_GROUP, V_HEAD + SUM_ROWS, ATTN_TILE), F32),
            ]),
        out_shape=jax.ShapeDtypeStruct((b, seq, B_WIDTH), BF16),
        compiler_params=pltpu.CompilerParams(
            dimension_semantics=("arbitrary", "arbitrary"), vmem_limit_bytes=VMEM_LIMIT),
        name="mla_attention",
    )(tq, tk, qmt, km, vbt)


def _out_kernel(x_ref, oa_ref, ob_ref, gattn_ref, wg_ref, wpa_ref, wpb_ref, wout_ref,
                gmlp_ref, wup_ref, wdn_ref, gfin_ref, o_ref):
    x = x_ref[...]
    n = _rms(x, gattn_ref[...]).astype(BF16)
    gates = jnp.dot(n, wg_ref[...], preferred_element_type=F32)
    pa = jnp.dot(oa_ref[...], wpa_ref[...], preferred_element_type=F32)
    pb = jnp.dot(ob_ref[...], wpb_ref[...], preferred_element_type=F32)
    merged = (jax.nn.sigmoid(gates[:, :D_MODEL]) * pa
              + jax.nn.sigmoid(gates[:, D_MODEL:]) * pb).astype(BF16)
    h = x + jnp.dot(merged, wout_ref[...], preferred_element_type=F32)
    m = _rms(h, gmlp_ref[...]).astype(BF16)
    for c in range(D_FF // FF_CHUNK):
        cols = slice(c * FF_CHUNK, (c + 1) * FF_CHUNK)
        up = jnp.dot(m, wup_ref[:, cols], preferred_element_type=F32)
        act = jnp.square(jnp.maximum(up, 0.0)).astype(BF16)
        h = h + jnp.dot(act, wdn_ref[cols, :], preferred_element_type=F32)
    o_ref[...] = _rms(h, gfin_ref[...])


def _out_block(x2, oa, ob, g_attn, wg, wpa, wpb, wout, g_mlp, wup, wdn, g_fin):
    n_rows = x2.shape[0]
    tm = ROW_TILE
    row = lambda i: (i, 0)
    return pl.pallas_call(
        _out_kernel,
        grid=(n_rows // tm,),
        in_specs=[
            pl.BlockSpec((tm, D_MODEL), row),
            pl.BlockSpec((tm, A_WIDTH), row), pl.BlockSpec((tm, B_WIDTH), row),
            _const_spec(g_attn.shape), _const_spec(wg.shape), _const_spec(wpa.shape),
            _const_spec(wpb.shape), _const_spec(wout.shape), _const_spec(g_mlp.shape),
            _const_spec(wup.shape), _const_spec(wdn.shape), _const_spec(g_fin.shape),
        ],
        out_specs=pl.BlockSpec((tm, D_MODEL), row),
        out_shape=jax.ShapeDtypeStruct((n_rows, D_MODEL), F32),
        compiler_params=pltpu.CompilerParams(
            dimension_semantics=("arbitrary",), vmem_limit_bytes=VMEM_LIMIT),
        name="out_mlp",
    )(x2, oa, ob, g_attn, wg, wpa, wpb, wout, g_mlp, wup, wdn, g_fin)


def _rope_tables(seq):
    half = QK_ROPE // 2
    inv_freq = ROPE_THETA ** (-jnp.arange(half, dtype=F32) / half)
    ang = jnp.arange(seq).astype(F32)[:, None] * inv_freq[None, :]
    cos2 = jnp.tile(jnp.cos(ang), (1, 2))
    sin2 = jnp.tile(jnp.sin(ang), (1, 2))
    pad = jnp.zeros((seq, MLA_HEAD_PAD - QK_NOPE - QK_ROPE), F32)
    cos_t = jnp.concatenate([jnp.ones((seq, QK_NOPE), F32), cos2, pad], axis=1)
    sin_t = jnp.concatenate([jnp.zeros((seq, QK_NOPE), F32), sin2, pad], axis=1)
    return cos_t, sin_t


def _rotate_half_cols(w):
    half = QK_ROPE // 2
    return jnp.concatenate([-w[..., half:], w[..., :half]], axis=-1)


def _pad_heads(w_nope, w_rope):
    k, h, _ = w_nope.shape
    if w_rope is None:
        w_rope = jnp.zeros((k, h, QK_ROPE), w_nope.dtype)
    pad = jnp.zeros((k, h, MLA_HEAD_PAD - QK_NOPE - QK_ROPE), w_nope.dtype)
    return jnp.concatenate([w_nope, w_rope, pad], axis=-1).reshape(k, h * MLA_HEAD_PAD)


def kernel(x, w_in, rel_bias, mla_q_norm, w_uq, mla_kv_norm, w_uk, w_uv, w_proj_a, w_proj_b,
           w_out, norm_attn, norm_mlp, w_mlp_up, w_mlp_down, norm_final):
    b, seq, d = x.shape
    assert d == D_MODEL and seq % ROW_TILE == 0 and seq // MOBA_BLOCK <= LANES
    assert w_in.shape[0] == 1, "single-layer block"
    x2 = x.reshape(b * seq, d)

    w = w_in[0]
    o_k = A_WIDTH
    o_v = 2 * A_WIDTH
    o_cq = 3 * A_WIDTH
    o_ckv = o_cq + Q_LORA
    o_kr = o_ckv + KV_LORA
    o_g = o_kr + QK_ROPE
    wk = w[:, o_k:o_v].astype(BF16)
    wqvt = jnp.concatenate([w[:, :o_k], w[:, o_v:o_cq]], axis=1).T.astype(BF16)
    w_kr = w[:, o_kr:o_g]
    lane_pad = lambda a: jnp.pad(a, ((0, 0), (QK_NOPE, MLA_HEAD_PAD - QK_NOPE - QK_ROPE)))
    wc = jnp.concatenate(
        [w[:, o_cq:o_kr], lane_pad(w_kr), lane_pad(_rotate_half_cols(w_kr))], axis=1).astype(BF16)
    wg = w[:, o_g:].astype(BF16)

    uq = w_uq[0].reshape(Q_LORA, B_HEADS, QK_NOPE + QK_ROPE)
    uq_nope, uq_rope = uq[..., :QK_NOPE], uq[..., QK_NOPE:]
    wuqt = _pad_heads(uq_nope, uq_rope).T.astype(BF16)
    wuqrt = _pad_heads(jnp.zeros_like(uq_nope), _rotate_half_cols(uq_rope)).T.astype(BF16)
    wuk = _pad_heads(w_uk[0].reshape(KV_LORA, B_HEADS, QK_NOPE), None).astype(BF16)
    wuvt = w_uv[0].T.astype(BF16)

    cos_t, sin_t = _rope_tables(seq)
    q_scale = (QK_NOPE + QK_ROPE) ** -0.5 * LOG2E
    row2 = lambda a: a.reshape(1, -1)

    qat, ka, vat, kmean, qmt, km, vbt = _inproj(
        x2, row2(norm_attn[0]), wk, wqvt, wc, row2(mla_q_norm[0]), row2(mla_kv_norm[0]),
        wuqt, wuqrt, wuk, wuvt, (cos_t * q_scale).T, (sin_t * q_scale).T, cos_t, sin_t, b, seq)

    as_seq = lambda a: a.reshape(b, seq, a.shape[-1])
    bias = _bias_tiles(rel_bias)
    oa = _moba(qat, as_seq(ka), vat, kmean.reshape(b, seq // MOBA_BLOCK, A_WIDTH), bias)
    ob = _mla(qmt, as_seq(km), vbt)

    out = _out_block(
        x2, oa.reshape(b * seq, A_WIDTH), ob.reshape(b * seq, B_WIDTH), row2(norm_attn[0]),
        wg, w_proj_a[0].astype(BF16), w_proj_b[0].astype(BF16), w_out[0].astype(BF16),
        row2(norm_mlp[0]), w_mlp_up[0].astype(BF16), w_mlp_down[0].astype(BF16),
        row2(norm_final))
    return out.reshape(b, seq, d)
```

```python
import functools
import math

import jax
import jax.numpy as jnp
from jax import lax
from jax.experimental import pallas as pl
from jax.experimental.pallas import tpu as pltpu

D_MODEL = 1024
A_HEADS = 8
A_HEAD_DIM = 64
A_WIDTH = A_HEADS * A_HEAD_DIM
MOBA_BLOCK = 256
MOBA_TOPK = 3
REL_BUCKETS = 32
REL_MAX_DIST = 128
B_HEADS = 8
QK_NOPE = 64
QK_ROPE = 32
V_HEAD = 64
B_WIDTH = B_HEADS * V_HEAD
Q_LORA = 384
KV_LORA = 256
ROPE_THETA = 10000.0
D_FF = 4 * D_MODEL
EPS = 1e-6
NEG = -1e30
LOG2E = math.log2(math.e)

LANES = 128
MLA_HEAD_PAD = 128
ATTN_TILE = 256
HEAD_GROUP = 4
SUM_ROWS = 16
PIPELINE_UNROLL = 8
ROW_TILE = 512
FF_CHUNK = 1024
VMEM_LIMIT = 56 * 1024 * 1024

F32 = jnp.float32
BF16 = jnp.bfloat16
NT_DIMS = (((1,), (1,)), ((), ()))


def _rms(xf, g):
    y = xf * lax.rsqrt(jnp.mean(xf * xf, axis=-1, keepdims=True) + EPS)
    return y * g


def _const_spec(shape):
    zeros = (0,) * len(shape)
    return pl.BlockSpec(shape, lambda *_: zeros, pipeline_mode=pl.Buffered(1))


def _bias_kernel(rel_ref, o_ref):
    h = pl.program_id(0)
    shape = (MOBA_BLOCK, 2 * MOBA_BLOCK)
    r = lax.broadcasted_iota(jnp.int32, shape, 0)
    c = lax.broadcasted_iota(jnp.int32, shape, 1)
    d = c - r
    dist = jnp.maximum(d, 0)
    max_exact = REL_BUCKETS // 2
    df = jnp.maximum(dist, 1).astype(F32)
    large = max_exact + (jnp.log(df / max_exact) / math.log(REL_MAX_DIST / max_exact)
                         * (REL_BUCKETS - max_exact)).astype(jnp.int32)
    large = jnp.minimum(large, REL_BUCKETS - 1)
    bucket = jnp.where(dist < max_exact, dist, large)
    val = jnp.zeros(shape, F32)
    for b in range(REL_BUCKETS):
        val = jnp.where(bucket == b, rel_ref[b, h], val)
    o_ref[...] = jnp.where(d >= 0, (val - rel_ref[REL_BUCKETS - 1, h]) * LOG2E, NEG)


def _bias_tiles(rel_bias):
    return pl.pallas_call(
        _bias_kernel,
        grid=(A_HEADS,),
        in_specs=[pl.BlockSpec(memory_space=pltpu.SMEM)],
        out_specs=pl.BlockSpec((None, MOBA_BLOCK, 2 * MOBA_BLOCK), lambda h: (h, 0, 0)),
        out_shape=jax.ShapeDtypeStruct((A_HEADS, MOBA_BLOCK, 2 * MOBA_BLOCK), F32),
        name="moba_bias_tiles",
    )(rel_bias)


def _inproj_kernel(x_ref, g_ref, wk_ref, wqvt_ref, wc_ref, qn_ref, kvn_ref, wuqt_ref,
                   wuk_ref, wuvt_ref, cqt_ref, sqt_ref, ck_ref, sk_ref,
                   qat_ref, ka_ref, vat_ref, kmean_ref, qmt_ref, km_ref, vbt_ref):
    n = _rms(x_ref[...], g_ref[...]).astype(BF16)
    k = jnp.dot(n, wk_ref[...], preferred_element_type=F32)
    ka_ref[...] = k.astype(BF16)
    nblk = ROW_TILE // ATTN_TILE
    kmean_ref[0] = jnp.sum(k.reshape(nblk, MOBA_BLOCK, A_WIDTH), axis=1) * (1.0 / MOBA_BLOCK)

    c = jnp.dot(n, wc_ref[...], preferred_element_type=F32)
    cq = _rms(c[:, :Q_LORA], qn_ref[...]).astype(BF16)
    ckv = _rms(c[:, Q_LORA:Q_LORA + KV_LORA], kvn_ref[...]).astype(BF16)
    kr = c[:, Q_LORA + KV_LORA:Q_LORA + KV_LORA + LANES]
    krr = c[:, Q_LORA + KV_LORA + LANES:]
    for t in range(nblk):
        rows = slice(t * ATTN_TILE, (t + 1) * ATTN_TILE)
        qv_t = lax.dot_general(wqvt_ref[...], n[rows], NT_DIMS,
                               preferred_element_type=F32)
        qat_ref[t] = (qv_t[:A_WIDTH] * (A_HEAD_DIM ** -0.5 * LOG2E)).astype(BF16)
        vat_ref[t] = qv_t[A_WIDTH:].astype(BF16)
        vbt_ref[t] = lax.dot_general(wuvt_ref[...], ckv[rows], NT_DIMS,
                                     preferred_element_type=F32).astype(BF16)
        q_t = lax.dot_general(wuqt_ref[...], cq[rows], NT_DIMS,
                              preferred_element_type=F32)
        cq_t, sq_t = cqt_ref[:, rows], sqt_ref[:, rows]
        half = QK_ROPE // 2
        for h in range(B_HEADS):
            q_h = q_t[h * MLA_HEAD_PAD:(h + 1) * MLA_HEAD_PAD]
            x1 = q_h[QK_NOPE:QK_NOPE + half]
            x2 = q_h[QK_NOPE + half:QK_NOPE + QK_ROPE]
            rot_h = jnp.concatenate(
                [jnp.zeros((QK_NOPE, ATTN_TILE), F32), -x2, x1,
                 jnp.zeros((MLA_HEAD_PAD - QK_NOPE - QK_ROPE, ATTN_TILE), F32)], axis=0)
            qmt_ref[t, h * MLA_HEAD_PAD:(h + 1) * MLA_HEAD_PAD, :] = (
                q_h * cq_t + rot_h * sq_t).astype(BF16)
    kn = jnp.dot(ckv, wuk_ref[...], preferred_element_type=F32)
    k_rope = kr * ck_ref[...] + krr * sk_ref[...]
    for h in range(B_HEADS):
        sl = slice(h * MLA_HEAD_PAD, (h + 1) * MLA_HEAD_PAD)
        km_ref[:, sl] = (kn[:, sl] + k_rope).astype(BF16)


def _inproj(x2, g_attn, wk, wqvt, wc, qn, kvn, wuqt, wuk, wuvt, cq_tt, sq_tt, ck_t, sk_t,
            batch, seq):
    n_rows = x2.shape[0]
    tm = ROW_TILE
    steps = n_rows // tm
    per_seq = seq // tm
    nblk = tm // ATTN_TILE
    row = lambda i: (i, 0)
    tab = lambda i: (i % per_seq, 0)
    tab_t = lambda i: (0, i % per_seq)
    tile_map = lambda i: (i // per_seq, i % per_seq, 0, 0)
    bf = lambda w: jax.ShapeDtypeStruct((n_rows, w), BF16)
    tiles = lambda w: jax.ShapeDtypeStruct((batch, seq // ATTN_TILE, w, ATTN_TILE), BF16)
    tile_spec = lambda w: pl.BlockSpec((None, nblk, w, ATTN_TILE), tile_map)
    return pl.pallas_call(
        _inproj_kernel,
        grid=(steps,),
        in_specs=[
            pl.BlockSpec((tm, D_MODEL), row),
            _const_spec(g_attn.shape), _const_spec(wk.shape), _const_spec(wqvt.shape),
            _const_spec(wc.shape), _const_spec(qn.shape), _const_spec(kvn.shape),
            _const_spec(wuqt.shape), _const_spec(wuk.shape),
            _const_spec(wuvt.shape),
            pl.BlockSpec((LANES, tm), tab_t), pl.BlockSpec((LANES, tm), tab_t),
            pl.BlockSpec((tm, LANES), tab), pl.BlockSpec((tm, LANES), tab),
        ],
        out_specs=[
            tile_spec(A_WIDTH), pl.BlockSpec((tm, A_WIDTH), row), tile_spec(A_WIDTH),
            pl.BlockSpec((1, nblk, A_WIDTH), lambda i: (i, 0, 0)),
            tile_spec(B_HEADS * MLA_HEAD_PAD),
            pl.BlockSpec((tm, B_HEADS * MLA_HEAD_PAD), row),
            tile_spec(B_WIDTH),
        ],
        out_shape=[
            tiles(A_WIDTH), bf(A_WIDTH), tiles(A_WIDTH),
            jax.ShapeDtypeStruct((steps, nblk, A_WIDTH), F32),
            tiles(B_HEADS * MLA_HEAD_PAD), bf(B_HEADS * MLA_HEAD_PAD), tiles(B_WIDTH),
        ],
        compiler_params=pltpu.CompilerParams(
            dimension_semantics=("arbitrary",), vmem_limit_bytes=VMEM_LIMIT),
        name="in_proj",
    )(x2, g_attn, wk, wqvt, wc, qn, kvn, wuqt, wuk, wuvt, cq_tt, sq_tt, ck_t, sk_t)


def _sum_rows():
    r = lax.broadcasted_iota(jnp.int32, (SUM_ROWS, ATTN_TILE), 0)
    return (r == 0).astype(BF16)


def _store_scores(buf, h, s_t):
    s_ref, max_ref = buf
    s_ref[h] = s_t
    max_ref[h] = jnp.max(s_t, axis=0, keepdims=True)


def _softmax_step(buf, h, v_t, m_ref, acc_ref, first, col_keep=None):
    s_ref, max_ref = buf[0].at[h], buf[1].at[h]
    s_max = max_ref[...]
    if col_keep is not None:
        s_max = jnp.where(col_keep > 0, s_max, NEG)
    if first:
        m_new = s_max
    else:
        m_old = m_ref[...]
        m_new = jnp.maximum(m_old, s_max)
    v_aug = jnp.concatenate([v_t, _sum_rows()], axis=0)
    m_exp = m_new if col_keep is None else jnp.maximum(m_new, max_ref[...])
    p_t = jnp.exp2(s_ref[...] - m_exp).astype(BF16)
    pv = jnp.dot(v_aug, p_t, preferred_element_type=F32)
    if col_keep is not None:
        pv = pv * col_keep
    if first:
        acc_ref[...] = pv
    else:
        acc_ref[...] = jnp.exp2(m_old - m_new) * acc_ref[...] + pv
    m_ref[...] = m_new


def _pipeline(n_items, score, attend, bufs):
    score(0, bufs[0])
    n_loops = (n_items - 1) // PIPELINE_UNROLL

    def body(p, carry):
        for t in range(PIPELINE_UNROLL):
            n = PIPELINE_UNROLL * p + t
            score(n + 1, bufs[(t + 1) % 2])
            attend(n, bufs[t % 2])
        return carry

    lax.fori_loop(0, n_loops, body, 0)
    done = PIPELINE_UNROLL * n_loops
    for n in range(done, n_items):
        if n + 1 < n_items:
            score(n + 1, bufs[(n + 1 - done) % 2])
        attend(n, bufs[(n - done) % 2])


def _tile_rows(j):
    return pl.ds(pl.multiple_of(j * ATTN_TILE, ATTN_TILE), ATTN_TILE)


def _write_heads(acc_ref, o_ref, nq, head_dim):
    def one(qi, carry):
        outs = []
        for h in range(HEAD_GROUP):
            acc = acc_ref[qi, h]
            outs.append(acc[:head_dim] / acc[head_dim:head_dim + 1])
        o_ref[_tile_rows(qi), :] = jnp.concatenate(outs, axis=0).T.astype(o_ref.dtype)
        return carry

    lax.fori_loop(0, nq, one, 0)


def _causal_items(nq, first_past):
    phases = [[(i, i - d) for i in range(d, nq)] for d in range(first_past)]
    phases.append([(i, j) for i in range(first_past, nq) for j in range(i - first_past + 1)])
    flat = [it for ph in phases for it in ph]
    tq = jnp.asarray([it[0] for it in flat], jnp.int32)
    tk = jnp.asarray([it[1] for it in flat], jnp.int32)
    return tq, tk, [len(ph) for ph in phases]


def _moba_kernel(tq_ref, tk_ref, qt_ref, k_ref, vt_ref, kmean_ref, bias_ref, o_ref,
                 keep_ref, s0_ref, s1_ref, t0_ref, t1_ref, m_ref, acc_ref, *, seq, phases):
    nblk = seq // MOBA_BLOCK

    def head_query(qi, h):
        pr, sub = divmod(h, 2)
        q_h = qt_ref[qi, pl.ds(pr * LANES + sub * A_HEAD_DIM, A_HEAD_DIM), :]
        zeros = jnp.zeros_like(q_h)
        return jnp.concatenate([q_h, zeros] if sub == 0 else [zeros, q_h], axis=0)

    def select_blocks(i, carry):
        row = lax.broadcasted_iota(jnp.int32, (nblk, ATTN_TILE), 0)
        rowf = row.astype(F32)
        past = row < i
        for h in range(HEAD_GROUP):
            kmean = kmean_ref[:, (h // 2) * LANES:(h // 2 + 1) * LANES].astype(BF16)
            gate = jnp.dot(kmean, head_query(i, h), preferred_element_type=F32)
            gate = jnp.where(past, gate, -jnp.inf)
            picked = jnp.zeros((nblk, ATTN_TILE), F32)
            for _ in range(MOBA_TOPK):
                top = jnp.max(gate, axis=0, keepdims=True)
                first = jnp.min(jnp.where(gate == top, rowf, float(nblk)), axis=0, keepdims=True)
                pick = rowf == first
                picked = jnp.where(pick, 1.0, picked)
                gate = jnp.where(pick, -jnp.inf, gate)
            keep_ref[i, h] = jnp.where(past, picked, 0.0)
        return carry

    lax.fori_loop(0, nblk, select_blocks, 0)

    def stage(offset, bias_cols, first):
        def score(n, buf):
            qi, kj = tq_ref[offset + n], tk_ref[offset + n]
            for h in range(HEAD_GROUP):
                k_t = k_ref[_tile_rows(kj), (h // 2) * LANES:(h // 2 + 1) * LANES]
                s_t = jnp.dot(k_t, head_query(qi, h), preferred_element_type=F32)
                if bias_cols is not None:
                    s_t = s_t + bias_ref[h, :, bias_cols]
                _store_scores(buf, h, s_t)

        def attend(n, buf):
            qi, kj = tq_ref[offset + n], tk_ref[offset + n]
            for h in range(HEAD_GROUP):
                col_keep = None if first else keep_ref[qi, h, pl.ds(kj, 1), :]
                _softmax_step(buf, h, vt_ref[kj, h * A_HEAD_DIM:(h + 1) * A_HEAD_DIM, :],
                              m_ref.at[qi, h], acc_ref.at[qi, h], first, col_keep)

        return score, attend

    bufs = ((s0_ref, t0_ref), (s1_ref, t1_ref))
    n_own, n_prev, n_far = phases
    _pipeline(n_own, *stage(0, slice(0, MOBA_BLOCK), True), bufs)
    _pipeline(n_prev, *stage(n_own, slice(MOBA_BLOCK, 2 * MOBA_BLOCK), False), bufs)
    _pipeline(n_far, *stage(n_own + n_prev, None, False), bufs)
    _write_heads(acc_ref, o_ref, nblk, A_HEAD_DIM)


def _moba(qat, ka, vat, kmean, bias):
    b, seq, _ = ka.shape
    nq = seq // ATTN_TILE
    gw = HEAD_GROUP * A_HEAD_DIM
    tq, tk, phases = _causal_items(nq, 2)
    per_group = lambda bi, g, *_: (bi, 0, g)
    tiles = lambda bi, g, *_: (bi, 0, g, 0)
    return pl.pallas_call(
        functools.partial(_moba_kernel, seq=seq, phases=phases),
        grid_spec=pltpu.PrefetchScalarGridSpec(
            num_scalar_prefetch=2,
            grid=(b, A_HEADS // HEAD_GROUP),
            in_specs=[
                pl.BlockSpec((None, nq, gw, ATTN_TILE), tiles),
                pl.BlockSpec((None, seq, gw), per_group),
                pl.BlockSpec((None, nq, gw, MOBA_BLOCK), tiles),
                pl.BlockSpec((None, nq, gw), per_group),
                pl.BlockSpec((HEAD_GROUP, MOBA_BLOCK, 2 * MOBA_BLOCK), lambda bi, g, *_: (g, 0, 0)),
            ],
            out_specs=pl.BlockSpec((None, seq, gw), per_group),
            scratch_shapes=[
                pltpu.VMEM((nq, HEAD_GROUP, nq, ATTN_TILE), F32),
                pltpu.VMEM((HEAD_GROUP, ATTN_TILE, ATTN_TILE), F32),
                pltpu.VMEM((HEAD_GROUP, ATTN_TILE, ATTN_TILE), F32),
                pltpu.VMEM((HEAD_GROUP, 1, ATTN_TILE), F32),
                pltpu.VMEM((HEAD_GROUP, 1, ATTN_TILE), F32),
                pltpu.VMEM((nq, HEAD_GROUP, 1, ATTN_TILE), F32),
                pltpu.VMEM((nq, HEAD_GROUP, A_HEAD_DIM + SUM_ROWS, ATTN_TILE), F32),
            ]),
        out_shape=jax.ShapeDtypeStruct((b, seq, A_WIDTH), BF16),
        compiler_params=pltpu.CompilerParams(
            dimension_semantics=("arbitrary", "arbitrary"), vmem_limit_bytes=VMEM_LIMIT),
        name="moba_attention",
    )(tq, tk, qat, ka, vat, kmean, bias)


def _mla_kernel(tq_ref, tk_ref, qt_ref, k_ref, vt_ref, o_ref, s0_ref, s1_ref, t0_ref, t1_ref,
                m_ref, acc_ref, *, nq, phases):
    def stage(offset, diagonal):
        def score(n, buf):
            qi, kj = tq_ref[offset + n], tk_ref[offset + n]
            for h in range(HEAD_GROUP):
                sl = slice(h * MLA_HEAD_PAD, (h + 1) * MLA_HEAD_PAD)
                s_t = jnp.dot(k_ref[_tile_rows(kj), sl], qt_ref[qi, sl, :],
                              preferred_element_type=F32)
                if diagonal:
                    key = lax.broadcasted_iota(jnp.int32, (ATTN_TILE, ATTN_TILE), 0)
                    qry = lax.broadcasted_iota(jnp.int32, (ATTN_TILE, ATTN_TILE), 1)
                    s_t = jnp.where(key <= qry, s_t, NEG)
                _store_scores(buf, h, s_t)

        def attend(n, buf):
            qi, kj = tq_ref[offset + n], tk_ref[offset + n]
            for h in range(HEAD_GROUP):
                _softmax_step(buf, h, vt_ref[kj, h * V_HEAD:(h + 1) * V_HEAD, :],
                              m_ref.at[qi, h], acc_ref.at[qi, h], diagonal)

        return score, attend

    bufs = ((s0_ref, t0_ref), (s1_ref, t1_ref))
    n_diag, n_past = phases
    _pipeline(n_diag, *stage(0, True), bufs)
    _pipeline(n_past, *stage(n_diag, False), bufs)
    _write_heads(acc_ref, o_ref, nq, V_HEAD)


def _mla(qmt, km, vbt):
    b, seq, _ = km.shape
    nq = seq // ATTN_TILE
    qw = HEAD_GROUP * MLA_HEAD_PAD
    vw = HEAD_GROUP * V_HEAD
    tq, tk, phases = _causal_items(nq, 1)
    per_group = lambda bi, g, *_: (bi, 0, g)
    tiles = lambda bi, g, *_: (bi, 0, g, 0)
    return pl.pallas_call(
        functools.partial(_mla_kernel, nq=nq, phases=phases),
        grid_spec=pltpu.PrefetchScalarGridSpec(
            num_scalar_prefetch=2,
            grid=(b, B_HEADS // HEAD_GROUP),
            in_specs=[
                pl.BlockSpec((None, nq, qw, ATTN_TILE), tiles),
                pl.BlockSpec((None, seq, qw), per_group),
                pl.BlockSpec((None, nq, vw, ATTN_TILE), tiles),
            ],
            out_specs=pl.BlockSpec((None, seq, vw), per_group),
            scratch_shapes=[
                pltpu.VMEM((HEAD_GROUP, ATTN_TILE, ATTN_TILE), F32),
                pltpu.VMEM((HEAD_GROUP, ATTN_TILE, ATTN_TILE), F32),
                pltpu.VMEM((HEAD_GROUP, 1, ATTN_TILE), F32),
                pltpu.VMEM((HEAD_GROUP, 1, ATTN_TILE), F32),
                pltpu.VMEM((nq, HEAD_GROUP, 1, ATTN_TILE), F32),
                pltpu.VMEM((nq, HEAD_GROUP, V_HEAD + SUM_ROWS, ATTN_TILE), F32),
            ]),
        out_shape=jax.ShapeDtypeStruct((b, seq, B_WIDTH), BF16),
        compiler_params=pltpu.CompilerParams(
            dimension_semantics=("arbitrary", "arbitrary"), vmem_limit_bytes=VMEM_LIMIT),
        name="mla_attention",
    )(tq, tk, qmt, km, vbt)


def _out_kernel(x_ref, oa_ref, ob_ref, gattn_ref, wg_ref, wpa_ref, wpb_ref, wout_ref,
                gmlp_ref, wup_ref, wdn_ref, gfin_ref, o_ref):
    x = x_ref[...]
    n = _rms(x, gattn_ref[...]).astype(BF16)
    gates = jnp.dot(n, wg_ref[...], preferred_element_type=F32)
    pa = jnp.dot(oa_ref[...], wpa_ref[...], preferred_element_type=F32)
    pb = jnp.dot(ob_ref[...], wpb_ref[...], preferred_element_type=F32)
    merged = (jax.nn.sigmoid(gates[:, :D_MODEL]) * pa
              + jax.nn.sigmoid(gates[:, D_MODEL:]) * pb).astype(BF16)
    h = x + jnp.dot(merged, wout_ref[...], preferred_element_type=F32)
    m = _rms(h, gmlp_ref[...]).astype(BF16)
    for c in range(D_FF // FF_CHUNK):
        cols = slice(c * FF_CHUNK, (c + 1) * FF_CHUNK)
        up = jnp.dot(m, wup_ref[:, cols], preferred_element_type=F32)
        act = jnp.square(jnp.maximum(up, 0.0)).astype(BF16)
        h = h + jnp.dot(act, wdn_ref[cols, :], preferred_element_type=F32)
    o_ref[...] = _rms(h, gfin_ref[...])


def _out_block(x2, oa, ob, g_attn, wg, wpa, wpb, wout, g_mlp, wup, wdn, g_fin):
    n_rows = x2.shape[0]
    tm = ROW_TILE
    row = lambda i: (i, 0)
    return pl.pallas_call(
        _out_kernel,
        grid=(n_rows // tm,),
        in_specs=[
            pl.BlockSpec((tm, D_MODEL), row),
            pl.BlockSpec((tm, A_WIDTH), row), pl.BlockSpec((tm, B_WIDTH), row),
            _const_spec(g_attn.shape), _const_spec(wg.shape), _const_spec(wpa.shape),
            _const_spec(wpb.shape), _const_spec(wout.shape), _const_spec(g_mlp.shape),
            _const_spec(wup.shape), _const_spec(wdn.shape), _const_spec(g_fin.shape),
        ],
        out_specs=pl.BlockSpec((tm, D_MODEL), row),
        out_shape=jax.ShapeDtypeStruct((n_rows, D_MODEL), F32),
        compiler_params=pltpu.CompilerParams(
            dimension_semantics=("arbitrary",), vmem_limit_bytes=VMEM_LIMIT),
        name="out_mlp",
    )(x2, oa, ob, g_attn, wg, wpa, wpb, wout, g_mlp, wup, wdn, g_fin)


def _rope_tables(seq):
    half = QK_ROPE // 2
    inv_freq = ROPE_THETA ** (-jnp.arange(half, dtype=F32) / half)
    ang = jnp.arange(seq).astype(F32)[:, None] * inv_freq[None, :]
    cos2 = jnp.tile(jnp.cos(ang), (1, 2))
    sin2 = jnp.tile(jnp.sin(ang), (1, 2))
    pad = jnp.zeros((seq, MLA_HEAD_PAD - QK_NOPE - QK_ROPE), F32)
    cos_t = jnp.concatenate([jnp.ones((seq, QK_NOPE), F32), cos2, pad], axis=1)
    sin_t = jnp.concatenate([jnp.zeros((seq, QK_NOPE), F32), sin2, pad], axis=1)
    return cos_t, sin_t


def _rotate_half_cols(w):
    half = QK_ROPE // 2
    return jnp.concatenate([-w[..., half:], w[..., :half]], axis=-1)


def _pad_heads(w_nope, w_rope):
    k, h, _ = w_nope.shape
    if w_rope is None:
        w_rope = jnp.zeros((k, h, QK_ROPE), w_nope.dtype)
    pad = jnp.zeros((k, h, MLA_HEAD_PAD - QK_NOPE - QK_ROPE), w_nope.dtype)
    return jnp.concatenate([w_nope, w_rope, pad], axis=-1).reshape(k, h * MLA_HEAD_PAD)


def kernel(x, w_in, rel_bias, mla_q_norm, w_uq, mla_kv_norm, w_uk, w_uv, w_proj_a, w_proj_b,
           w_out, norm_attn, norm_mlp, w_mlp_up, w_mlp_down, norm_final):
    b, seq, d = x.shape
    assert d == D_MODEL and seq % ROW_TILE == 0 and seq // MOBA_BLOCK <= LANES
    assert w_in.shape[0] == 1, "single-layer block"
    x2 = x.reshape(b * seq, d)

    w = w_in[0]
    o_k = A_WIDTH
    o_v = 2 * A_WIDTH
    o_cq = 3 * A_WIDTH
    o_ckv = o_cq + Q_LORA
    o_kr = o_ckv + KV_LORA
    o_g = o_kr + QK_ROPE
    wk = w[:, o_k:o_v].astype(BF16)
    wqvt = jnp.concatenate([w[:, :o_k], w[:, o_v:o_cq]], axis=1).T.astype(BF16)
    w_kr = w[:, o_kr:o_g]
    lane_pad = lambda a: jnp.pad(a, ((0, 0), (QK_NOPE, MLA_HEAD_PAD - QK_NOPE - QK_ROPE)))
    wc = jnp.concatenate(
        [w[:, o_cq:o_kr], lane_pad(w_kr), lane_pad(_rotate_half_cols(w_kr))], axis=1).astype(BF16)
    wg = w[:, o_g:].astype(BF16)

    uq = w_uq[0].reshape(Q_LORA, B_HEADS, QK_NOPE + QK_ROPE)
    uq_nope, uq_rope = uq[..., :QK_NOPE], uq[..., QK_NOPE:]
    wuqt = _pad_heads(uq_nope, uq_rope).T.astype(BF16)
    wuk = _pad_heads(w_uk[0].reshape(KV_LORA, B_HEADS, QK_NOPE), None).astype(BF16)
    wuvt = w_uv[0].T.astype(BF16)

    cos_t, sin_t = _rope_tables(seq)
    q_scale = (QK_NOPE + QK_ROPE) ** -0.5 * LOG2E
    row2 = lambda a: a.reshape(1, -1)

    qat, ka, vat, kmean, qmt, km, vbt = _inproj(
        x2, row2(norm_attn[0]), wk, wqvt, wc, row2(mla_q_norm[0]), row2(mla_kv_norm[0]),
        wuqt, wuk, wuvt, (cos_t * q_scale).T, (sin_t * q_scale).T, cos_t, sin_t, b, seq)

    as_seq = lambda a: a.reshape(b, seq, a.shape[-1])
    bias = _bias_tiles(rel_bias)
    oa = _moba(qat, as_seq(ka), vat, kmean.reshape(b, seq // MOBA_BLOCK, A_WIDTH), bias)
    ob = _mla(qmt, as_seq(km), vbt)

    out = _out_block(
        x2, oa.reshape(b * seq, A_WIDTH), ob.reshape(b * seq, B_WIDTH), row2(norm_attn[0]),
        wg, w_proj_a[0].astype(BF16), w_proj_b[0].astype(BF16), w_out[0].astype(BF16),
        row2(norm_mlp[0]), w_mlp_up[0].astype(BF16), w_mlp_down[0].astype(BF16),
        row2(norm_final))
    return out.reshape(b, seq, d)
```

```python
import functools
import math

import jax
import jax.numpy as jnp
from jax import lax
from jax.experimental import pallas as pl
from jax.experimental.pallas import tpu as pltpu

D_MODEL = 1024
A_HEADS = 8
A_HEAD_DIM = 64
A_WIDTH = A_HEADS * A_HEAD_DIM
MOBA_BLOCK = 256
MOBA_TOPK = 3
REL_BUCKETS = 32
REL_MAX_DIST = 128
B_HEADS = 8
QK_NOPE = 64
QK_ROPE = 32
V_HEAD = 64
B_WIDTH = B_HEADS * V_HEAD
Q_LORA = 384
KV_LORA = 256
ROPE_THETA = 10000.0
D_FF = 4 * D_MODEL
EPS = 1e-6
NEG = -1e30
LOG2E = math.log2(math.e)

LANES = 128
MLA_HEAD_PAD = 128
ATTN_TILE = 256
HEAD_GROUP = 4
SUM_ROWS = 16
SUBLANES = 8
SCORE_BUFFERS = (HEAD_GROUP, ATTN_TILE // SUBLANES, 2, SUBLANES, ATTN_TILE)
PIPELINE_UNROLL = 8
ROW_TILE = 512
FF_CHUNK = 1024
VMEM_LIMIT = 56 * 1024 * 1024

F32 = jnp.float32
BF16 = jnp.bfloat16
NT_DIMS = (((1,), (1,)), ((), ()))


def _rms(xf, g):
    y = xf * lax.rsqrt(jnp.mean(xf * xf, axis=-1, keepdims=True) + EPS)
    return y * g


def _const_spec(shape):
    zeros = (0,) * len(shape)
    return pl.BlockSpec(shape, lambda *_: zeros, pipeline_mode=pl.Buffered(1))


def _bias_kernel(rel_ref, o_ref):
    h = pl.program_id(0)
    shape = (MOBA_BLOCK, 2 * MOBA_BLOCK)
    r = lax.broadcasted_iota(jnp.int32, shape, 0)
    c = lax.broadcasted_iota(jnp.int32, shape, 1)
    d = c - r
    dist = jnp.maximum(d, 0)
    max_exact = REL_BUCKETS // 2
    df = jnp.maximum(dist, 1).astype(F32)
    large = max_exact + (jnp.log(df / max_exact) / math.log(REL_MAX_DIST / max_exact)
                         * (REL_BUCKETS - max_exact)).astype(jnp.int32)
    large = jnp.minimum(large, REL_BUCKETS - 1)
    bucket = jnp.where(dist < max_exact, dist, large)
    val = jnp.zeros(shape, F32)
    for b in range(REL_BUCKETS):
        val = jnp.where(bucket == b, rel_ref[b, h], val)
    o_ref[...] = jnp.where(d >= 0, (val - rel_ref[REL_BUCKETS - 1, h]) * LOG2E, NEG)


def _bias_tiles(rel_bias):
    return pl.pallas_call(
        _bias_kernel,
        grid=(A_HEADS,),
        in_specs=[pl.BlockSpec(memory_space=pltpu.SMEM)],
        out_specs=pl.BlockSpec((None, MOBA_BLOCK, 2 * MOBA_BLOCK), lambda h: (h, 0, 0)),
        out_shape=jax.ShapeDtypeStruct((A_HEADS, MOBA_BLOCK, 2 * MOBA_BLOCK), F32),
        name="moba_bias_tiles",
    )(rel_bias)


def _inproj_kernel(x_ref, g_ref, wk_ref, wqvt_ref, wc_ref, qn_ref, kvn_ref, wuqt_ref,
                   wuk_ref, wuvt_ref, cqt_ref, sqt_ref, ck_ref, sk_ref,
                   qat_ref, ka_ref, vat_ref, kmean_ref, qmt_ref, km_ref, vbt_ref):
    n = _rms(x_ref[...], g_ref[...]).astype(BF16)
    k = jnp.dot(n, wk_ref[...], preferred_element_type=F32)
    ka_ref[...] = k.astype(BF16)
    nblk = ROW_TILE // ATTN_TILE
    kmean_ref[0] = jnp.sum(k.reshape(nblk, MOBA_BLOCK, A_WIDTH), axis=1) * (1.0 / MOBA_BLOCK)

    c = jnp.dot(n, wc_ref[...], preferred_element_type=F32)
    cq = _rms(c[:, :Q_LORA], qn_ref[...]).astype(BF16)
    ckv = _rms(c[:, Q_LORA:Q_LORA + KV_LORA], kvn_ref[...]).astype(BF16)
    kr = c[:, Q_LORA + KV_LORA:Q_LORA + KV_LORA + LANES]
    krr = c[:, Q_LORA + KV_LORA + LANES:]
    for t in range(nblk):
        rows = slice(t * ATTN_TILE, (t + 1) * ATTN_TILE)
        qv_t = lax.dot_general(wqvt_ref[...], n[rows], NT_DIMS,
                               preferred_element_type=F32)
        qat_ref[t] = (qv_t[:A_WIDTH] * (A_HEAD_DIM ** -0.5 * LOG2E)).astype(BF16)
        vat_ref[t] = qv_t[A_WIDTH:].astype(BF16)
        vbt_ref[t] = lax.dot_general(wuvt_ref[...], ckv[rows], NT_DIMS,
                                     preferred_element_type=F32).astype(BF16)
        q_t = lax.dot_general(wuqt_ref[...], cq[rows], NT_DIMS,
                              preferred_element_type=F32)
        cq_t, sq_t = cqt_ref[:, rows], sqt_ref[:, rows]
        half = QK_ROPE // 2
        for h in range(B_HEADS):
            q_h = q_t[h * MLA_HEAD_PAD:(h + 1) * MLA_HEAD_PAD]
            x1 = q_h[QK_NOPE:QK_NOPE + half]
            x2 = q_h[QK_NOPE + half:QK_NOPE + QK_ROPE]
            rot_h = jnp.concatenate(
                [jnp.zeros((QK_NOPE, ATTN_TILE), F32), -x2, x1,
                 jnp.zeros((MLA_HEAD_PAD - QK_NOPE - QK_ROPE, ATTN_TILE), F32)], axis=0)
            qmt_ref[t, h * MLA_HEAD_PAD:(h + 1) * MLA_HEAD_PAD, :] = (
                q_h * cq_t + rot_h * sq_t).astype(BF16)
    kn = jnp.dot(ckv, wuk_ref[...], preferred_element_type=F32)
    k_rope = kr * ck_ref[...] + krr * sk_ref[...]
    for h in range(B_HEADS):
        sl = slice(h * MLA_HEAD_PAD, (h + 1) * MLA_HEAD_PAD)
        km_ref[:, sl] = (kn[:, sl] + k_rope).astype(BF16)


def _inproj(x2, g_attn, wk, wqvt, wc, qn, kvn, wuqt, wuk, wuvt, cq_tt, sq_tt, ck_t, sk_t,
            batch, seq):
    n_rows = x2.shape[0]
    tm = ROW_TILE
    steps = n_rows // tm
    per_seq = seq // tm
    nblk = tm // ATTN_TILE
    row = lambda i: (i, 0)
    tab = lambda i: (i % per_seq, 0)
    tab_t = lambda i: (0, i % per_seq)
    tile_map = lambda i: (i // per_seq, i % per_seq, 0, 0)
    bf = lambda w: jax.ShapeDtypeStruct((n_rows, w), BF16)
    tiles = lambda w: jax.ShapeDtypeStruct((batch, seq // ATTN_TILE, w, ATTN_TILE), BF16)
    tile_spec = lambda w: pl.BlockSpec((None, nblk, w, ATTN_TILE), tile_map)
    return pl.pallas_call(
        _inproj_kernel,
        grid=(steps,),
        in_specs=[
            pl.BlockSpec((tm, D_MODEL), row),
            _const_spec(g_attn.shape), _const_spec(wk.shape), _const_spec(wqvt.shape),
            _const_spec(wc.shape), _const_spec(qn.shape), _const_spec(kvn.shape),
            _const_spec(wuqt.shape), _const_spec(wuk.shape),
            _const_spec(wuvt.shape),
            pl.BlockSpec((LANES, tm), tab_t), pl.BlockSpec((LANES, tm), tab_t),
            pl.BlockSpec((tm, LANES), tab), pl.BlockSpec((tm, LANES), tab),
        ],
        out_specs=[
            tile_spec(A_WIDTH), pl.BlockSpec((tm, A_WIDTH), row), tile_spec(A_WIDTH),
            pl.BlockSpec((1, nblk, A_WIDTH), lambda i: (i, 0, 0)),
            tile_spec(B_HEADS * MLA_HEAD_PAD),
            pl.BlockSpec((tm, B_HEADS * MLA_HEAD_PAD), row),
            tile_spec(B_WIDTH),
        ],
        out_shape=[
            tiles(A_WIDTH), bf(A_WIDTH), tiles(A_WIDTH),
            jax.ShapeDtypeStruct((steps, nblk, A_WIDTH), F32),
            tiles(B_HEADS * MLA_HEAD_PAD), bf(B_HEADS * MLA_HEAD_PAD), tiles(B_WIDTH),
        ],
        compiler_params=pltpu.CompilerParams(
            dimension_semantics=("arbitrary",), vmem_limit_bytes=VMEM_LIMIT),
        name="in_proj",
    )(x2, g_attn, wk, wqvt, wc, qn, kvn, wuqt, wuk, wuvt, cq_tt, sq_tt, ck_t, sk_t)


def _sum_rows():
    r = lax.broadcasted_iota(jnp.int32, (SUM_ROWS, ATTN_TILE), 0)
    return (r == 0).astype(BF16)


def _store_scores(buf, h, s_t):
    s_ref, max_ref, slot = buf
    s_ref[h, :, slot] = s_t.reshape(ATTN_TILE // SUBLANES, SUBLANES, ATTN_TILE)
    max_ref[slot, h] = jnp.max(s_t, axis=0, keepdims=True)


def _softmax_step(buf, h, v_t, m_ref, acc_ref, first, col_keep=None):
    s_ref, max_ref, slot = buf
    s_t = s_ref[h, :, slot].reshape(ATTN_TILE, ATTN_TILE)
    s_max = max_ref[slot, h]
    if col_keep is not None:
        s_max = jnp.where(col_keep > 0, s_max, NEG)
    if first:
        m_new = s_max
    else:
        m_old = m_ref[...]
        m_new = jnp.maximum(m_old, s_max)
    v_aug = jnp.concatenate([v_t, _sum_rows()], axis=0)
    m_exp = m_new if col_keep is None else jnp.maximum(m_new, max_ref[slot, h])
    p_t = jnp.exp2(s_t - m_exp).astype(BF16)
    pv = jnp.dot(v_aug, p_t, preferred_element_type=F32)
    if col_keep is not None:
        pv = pv * col_keep
    if first:
        acc_ref[...] = pv
    else:
        acc_ref[...] = jnp.exp2(m_old - m_new) * acc_ref[...] + pv
    m_ref[...] = m_new


def _pipeline(n_items, score, attend, bufs):
    score(0, bufs[0])
    n_loops = (n_items - 1) // PIPELINE_UNROLL

    def body(p, carry):
        for t in range(PIPELINE_UNROLL):
            n = PIPELINE_UNROLL * p + t
            score(n + 1, bufs[(t + 1) % 2])
            attend(n, bufs[t % 2])
        return carry

    lax.fori_loop(0, n_loops, body, 0)
    done = PIPELINE_UNROLL * n_loops
    for n in range(done, n_items):
        if n + 1 < n_items:
            score(n + 1, bufs[(n + 1 - done) % 2])
        attend(n, bufs[(n - done) % 2])


def _tile_rows(j):
    return pl.ds(pl.multiple_of(j * ATTN_TILE, ATTN_TILE), ATTN_TILE)


def _write_heads(acc_ref, o_ref, nq, head_dim):
    def one(qi, carry):
        outs = []
        for h in range(HEAD_GROUP):
            acc = acc_ref[qi, h]
            outs.append(acc[:head_dim] / acc[head_dim:head_dim + 1])
        o_ref[_tile_rows(qi), :] = jnp.concatenate(outs, axis=0).T.astype(o_ref.dtype)
        return carry

    lax.fori_loop(0, nq, one, 0)


def _causal_items(nq, first_past):
    phases = [[(i, i - d) for i in range(d, nq)] for d in range(first_past)]
    phases.append([(i, j) for i in range(first_past, nq) for j in range(i - first_past + 1)])
    flat = [it for ph in phases for it in ph]
    tq = jnp.asarray([it[0] for it in flat], jnp.int32)
    tk = jnp.asarray([it[1] for it in flat], jnp.int32)
    return tq, tk, [len(ph) for ph in phases]


def _moba_kernel(tq_ref, tk_ref, qt_ref, k_ref, vt_ref, kmean_ref, bias_ref, o_ref,
                 keep_ref, s_ref, smax_ref, m_ref, acc_ref, *, seq, phases):
    nblk = seq // MOBA_BLOCK

    def head_query(qi, h):
        pr, sub = divmod(h, 2)
        q_h = qt_ref[qi, pl.ds(pr * LANES + sub * A_HEAD_DIM, A_HEAD_DIM), :]
        zeros = jnp.zeros_like(q_h)
        return jnp.concatenate([q_h, zeros] if sub == 0 else [zeros, q_h], axis=0)

    def select_blocks(i, carry):
        row = lax.broadcasted_iota(jnp.int32, (nblk, ATTN_TILE), 0)
        rowf = row.astype(F32)
        past = row < i
        for h in range(HEAD_GROUP):
            kmean = kmean_ref[:, (h // 2) * LANES:(h // 2 + 1) * LANES].astype(BF16)
            gate = jnp.dot(kmean, head_query(i, h), preferred_element_type=F32)
            gate = jnp.where(past, gate, -jnp.inf)
            picked = jnp.zeros((nblk, ATTN_TILE), F32)
            for _ in range(MOBA_TOPK):
                top = jnp.max(gate, axis=0, keepdims=True)
                first = jnp.min(jnp.where(gate == top, rowf, float(nblk)), axis=0, keepdims=True)
                pick = rowf == first
                picked = jnp.where(pick, 1.0, picked)
                gate = jnp.where(pick, -jnp.inf, gate)
            keep_ref[i, h] = jnp.where(past, picked, 0.0)
        return carry

    lax.fori_loop(0, nblk, select_blocks, 0)

    def stage(offset, bias_cols, first):
        def score(n, buf):
            qi, kj = tq_ref[offset + n], tk_ref[offset + n]
            for h in range(HEAD_GROUP):
                k_t = k_ref[_tile_rows(kj), (h // 2) * LANES:(h // 2 + 1) * LANES]
                s_t = jnp.dot(k_t, head_query(qi, h), preferred_element_type=F32)
                if bias_cols is not None:
                    s_t = s_t + bias_ref[h, :, bias_cols]
                _store_scores(buf, h, s_t)

        def attend(n, buf):
            qi, kj = tq_ref[offset + n], tk_ref[offset + n]
            for h in range(HEAD_GROUP):
                col_keep = None if first else keep_ref[qi, h, pl.ds(kj, 1), :]
                _softmax_step(buf, h, vt_ref[kj, h * A_HEAD_DIM:(h + 1) * A_HEAD_DIM, :],
                              m_ref.at[qi, h], acc_ref.at[qi, h], first, col_keep)

        return score, attend

    bufs = ((s_ref, smax_ref, 0), (s_ref, smax_ref, 1))
    n_own, n_prev, n_far = phases
    _pipeline(n_own, *stage(0, slice(0, MOBA_BLOCK), True), bufs)
    _pipeline(n_prev, *stage(n_own, slice(MOBA_BLOCK, 2 * MOBA_BLOCK), False), bufs)
    _pipeline(n_far, *stage(n_own + n_prev, None, False), bufs)
    _write_heads(acc_ref, o_ref, nblk, A_HEAD_DIM)


def _moba(qat, ka, vat, kmean, bias):
    b, seq, _ = ka.shape
    nq = seq // ATTN_TILE
    gw = HEAD_GROUP * A_HEAD_DIM
    tq, tk, phases = _causal_items(nq, 2)
    per_group = lambda bi, g, *_: (bi, 0, g)
    tiles = lambda bi, g, *_: (bi, 0, g, 0)
    return pl.pallas_call(
        functools.partial(_moba_kernel, seq=seq, phases=phases),
        grid_spec=pltpu.PrefetchScalarGridSpec(
            num_scalar_prefetch=2,
            grid=(b, A_HEADS // HEAD_GROUP),
            in_specs=[
                pl.BlockSpec((None, nq, gw, ATTN_TILE), tiles),
                pl.BlockSpec((None, seq, gw), per_group),
                pl.BlockSpec((None, nq, gw, MOBA_BLOCK), tiles),
                pl.BlockSpec((None, nq, gw), per_group),
                pl.BlockSpec((HEAD_GROUP, MOBA_BLOCK, 2 * MOBA_BLOCK), lambda bi, g, *_: (g, 0, 0)),
            ],
            out_specs=pl.BlockSpec((None, seq, gw), per_group),
            scratch_shapes=[
                pltpu.VMEM((nq, HEAD_GROUP, nq, ATTN_TILE), F32),
                pltpu.VMEM(SCORE_BUFFERS, F32),
                pltpu.VMEM((2, HEAD_GROUP, 1, ATTN_TILE), F32),
                pltpu.VMEM((nq, HEAD_GROUP, 1, ATTN_TILE), F32),
                pltpu.VMEM((nq, HEAD_GROUP, A_HEAD_DIM + SUM_ROWS, ATTN_TILE), F32),
            ]),
        out_shape=jax.ShapeDtypeStruct((b, seq, A_WIDTH), BF16),
        compiler_params=pltpu.CompilerParams(
            dimension_semantics=("arbitrary", "arbitrary"), vmem_limit_bytes=VMEM_LIMIT),
        name="moba_attention",
    )(tq, tk, qat, ka, vat, kmean, bias)


def _mla_kernel(tq_ref, tk_ref, qt_ref, k_ref, vt_ref, o_ref, s_ref, smax_ref, m_ref, acc_ref,
                *, nq, phases):
    def stage(offset, diagonal):
        def score(n, buf):
            qi, kj = tq_ref[offset + n], tk_ref[offset + n]
            for h in range(HEAD_GROUP):
                sl = slice(h * MLA_HEAD_PAD, (h + 1) * MLA_HEAD_PAD)
                s_t = jnp.dot(k_ref[_tile_rows(kj), sl], qt_ref[qi, sl, :],
                              preferred_element_type=F32)
                if diagonal:
                    key = lax.broadcasted_iota(jnp.int32, (ATTN_TILE, ATTN_TILE), 0)
                    qry = lax.broadcasted_iota(jnp.int32, (ATTN_TILE, ATTN_TILE), 1)
                    s_t = jnp.where(key <= qry, s_t, NEG)
                _store_scores(buf, h, s_t)

        def attend(n, buf):
            qi, kj = tq_ref[offset + n], tk_ref[offset + n]
            for h in range(HEAD_GROUP):
                _softmax_step(buf, h, vt_ref[kj, h * V_HEAD:(h + 1) * V_HEAD, :],
                              m_ref.at[qi, h], acc_ref.at[qi, h], diagonal)

        return score, attend

    bufs = ((s_ref, smax_ref, 0), (s_ref, smax_ref, 1))
    n_diag, n_past = phases
    _pipeline(n_diag, *stage(0, True), bufs)
    _pipeline(n_past, *stage(n_diag, False), bufs)
    _write_heads(acc_ref, o_ref, nq, V_HEAD)


def _mla(qmt, km, vbt):
    b, seq, _ = km.shape
    nq = seq // ATTN_TILE
    qw = HEAD_GROUP * MLA_HEAD_PAD
    vw = HEAD_GROUP * V_HEAD
    tq, tk, phases = _causal_items(nq, 1)
    per_group = lambda bi, g, *_: (bi, 0, g)
    tiles = lambda bi, g, *_: (bi, 0, g, 0)
    return pl.pallas_call(
        functools.partial(_mla_kernel, nq=nq, phases=phases),
        grid_spec=pltpu.PrefetchScalarGridSpec(
            num_scalar_prefetch=2,
            grid=(b, B_HEADS // HEAD_GROUP),
            in_specs=[
                pl.BlockSpec((None, nq, qw, ATTN_TILE), tiles),
                pl.BlockSpec((None, seq, qw), per_group),
                pl.BlockSpec((None, nq, vw, ATTN_TILE), tiles),
            ],
            out_specs=pl.BlockSpec((None, seq, vw), per_group),
            scratch_shapes=[
                pltpu.VMEM(SCORE_BUFFERS, F32),
                pltpu.VMEM((2, HEAD_GROUP, 1, ATTN_TILE), F32),
                pltpu.VMEM((nq, HEAD_GROUP, 1, ATTN_TILE), F32),
                pltpu.VMEM((nq, HEAD_GROUP, V_HEAD + SUM_ROWS, ATTN_TILE), F32),
            ]),
        out_shape=jax.ShapeDtypeStruct((b, seq, B_WIDTH), BF16),
        compiler_params=pltpu.CompilerParams(
            dimension_semantics=("arbitrary", "arbitrary"), vmem_limit_bytes=VMEM_LIMIT),
        name="mla_attention",
    )(tq, tk, qmt, km, vbt)


def _out_kernel(x_ref, oa_ref, ob_ref, gattn_ref, wg_ref, wpa_ref, wpb_ref, wout_ref,
                gmlp_ref, wup_ref, wdn_ref, gfin_ref, o_ref):
    x = x_ref[...]
    n = _rms(x, gattn_ref[...]).astype(BF16)
    gates = jnp.dot(n, wg_ref[...], preferred_element_type=F32)
    pa = jnp.dot(oa_ref[...], wpa_ref[...], preferred_element_type=F32)
    pb = jnp.dot(ob_ref[...], wpb_ref[...], preferred_element_type=F32)
    merged = (jax.nn.sigmoid(gates[:, :D_MODEL]) * pa
              + jax.nn.sigmoid(gates[:, D_MODEL:]) * pb).astype(BF16)
    h = x + jnp.dot(merged, wout_ref[...], preferred_element_type=F32)
    m = _rms(h, gmlp_ref[...]).astype(BF16)
    for c in range(D_FF // FF_CHUNK):
        cols = slice(c * FF_CHUNK, (c + 1) * FF_CHUNK)
        up = jnp.dot(m, wup_ref[:, cols], preferred_element_type=F32)
        act = jnp.square(jnp.maximum(up, 0.0)).astype(BF16)
        h = h + jnp.dot(act, wdn_ref[cols, :], preferred_element_type=F32)
    o_ref[...] = _rms(h, gfin_ref[...])


def _out_block(x2, oa, ob, g_attn, wg, wpa, wpb, wout, g_mlp, wup, wdn, g_fin):
    n_rows = x2.shape[0]
    tm = ROW_TILE
    row = lambda i: (i, 0)
    return pl.pallas_call(
        _out_kernel,
        grid=(n_rows // tm,),
        in_specs=[
            pl.BlockSpec((tm, D_MODEL), row),
            pl.BlockSpec((tm, A_WIDTH), row), pl.BlockSpec((tm, B_WIDTH), row),
            _const_spec(g_attn.shape), _const_spec(wg.shape), _const_spec(wpa.shape),
            _const_spec(wpb.shape), _const_spec(wout.shape), _const_spec(g_mlp.shape),
            _const_spec(wup.shape), _const_spec(wdn.shape), _const_spec(g_fin.shape),
        ],
        out_specs=pl.BlockSpec((tm, D_MODEL), row),
        out_shape=jax.ShapeDtypeStruct((n_rows, D_MODEL), F32),
        compiler_params=pltpu.CompilerParams(
            dimension_semantics=("arbitrary",), vmem_limit_bytes=VMEM_LIMIT),
        name="out_mlp",
    )(x2, oa, ob, g_attn, wg, wpa, wpb, wout, g_mlp, wup, wdn, g_fin)


def _rope_tables(seq):
    half = QK_ROPE // 2
    inv_freq = ROPE_THETA ** (-jnp.arange(half, dtype=F32) / half)
    ang = jnp.arange(seq).astype(F32)[:, None] * inv_freq[None, :]
    cos2 = jnp.tile(jnp.cos(ang), (1, 2))
    sin2 = jnp.tile(jnp.sin(ang), (1, 2))
    pad = jnp.zeros((seq, MLA_HEAD_PAD - QK_NOPE - QK_ROPE), F32)
    cos_t = jnp.concatenate([jnp.ones((seq, QK_NOPE), F32), cos2, pad], axis=1)
    sin_t = jnp.concatenate([jnp.zeros((seq, QK_NOPE), F32), sin2, pad], axis=1)
    return cos_t, sin_t


def _rotate_half_cols(w):
    half = QK_ROPE // 2
    return jnp.concatenate([-w[..., half:], w[..., :half]], axis=-1)


def _pad_heads(w_nope, w_rope):
    k, h, _ = w_nope.shape
    if w_rope is None:
        w_rope = jnp.zeros((k, h, QK_ROPE), w_nope.dtype)
    pad = jnp.zeros((k, h, MLA_HEAD_PAD - QK_NOPE - QK_ROPE), w_nope.dtype)
    return jnp.concatenate([w_nope, w_rope, pad], axis=-1).reshape(k, h * MLA_HEAD_PAD)


def kernel(x, w_in, rel_bias, mla_q_norm, w_uq, mla_kv_norm, w_uk, w_uv, w_proj_a, w_proj_b,
           w_out, norm_attn, norm_mlp, w_mlp_up, w_mlp_down, norm_final):
    b, seq, d = x.shape
    assert d == D_MODEL and seq % ROW_TILE == 0 and seq // MOBA_BLOCK <= LANES
    assert w_in.shape[0] == 1, "single-layer block"
    x2 = x.reshape(b * seq, d)

    w = w_in[0]
    o_k = A_WIDTH
    o_v = 2 * A_WIDTH
    o_cq = 3 * A_WIDTH
    o_ckv = o_cq + Q_LORA
    o_kr = o_ckv + KV_LORA
    o_g = o_kr + QK_ROPE
    wk = w[:, o_k:o_v].astype(BF16)
    wqvt = jnp.concatenate([w[:, :o_k], w[:, o_v:o_cq]], axis=1).T.astype(BF16)
    w_kr = w[:, o_kr:o_g]
    lane_pad = lambda a: jnp.pad(a, ((0, 0), (QK_NOPE, MLA_HEAD_PAD - QK_NOPE - QK_ROPE)))
    wc = jnp.concatenate(
        [w[:, o_cq:o_kr], lane_pad(w_kr), lane_pad(_rotate_half_cols(w_kr))], axis=1).astype(BF16)
    wg = w[:, o_g:].astype(BF16)

    uq = w_uq[0].reshape(Q_LORA, B_HEADS, QK_NOPE + QK_ROPE)
    uq_nope, uq_rope = uq[..., :QK_NOPE], uq[..., QK_NOPE:]
    wuqt = _pad_heads(uq_nope, uq_rope).T.astype(BF16)
    wuk = _pad_heads(w_uk[0].reshape(KV_LORA, B_HEADS, QK_NOPE), None).astype(BF16)
    wuvt = w_uv[0].T.astype(BF16)

    cos_t, sin_t = _rope_tables(seq)
    q_scale = (QK_NOPE + QK_ROPE) ** -0.5 * LOG2E
    row2 = lambda a: a.reshape(1, -1)

    qat, ka, vat, kmean, qmt, km, vbt = _inproj(
        x2, row2(norm_attn[0]), wk, wqvt, wc, row2(mla_q_norm[0]), row2(mla_kv_norm[0]),
        wuqt, wuk, wuvt, (cos_t * q_scale).T, (sin_t * q_scale).T, cos_t, sin_t, b, seq)

    as_seq = lambda a: a.reshape(b, seq, a.shape[-1])
    bias = _bias_tiles(rel_bias)
    oa = _moba(qat, as_seq(ka), vat, kmean.reshape(b, seq // MOBA_BLOCK, A_WIDTH), bias)
    ob = _mla(qmt, as_seq(km), vbt)

    out = _out_block(
        x2, oa.reshape(b * seq, A_WIDTH), ob.reshape(b * seq, B_WIDTH), row2(norm_attn[0]),
        wg, w_proj_a[0].astype(BF16), w_proj_b[0].astype(BF16), w_out[0].astype(BF16),
        row2(norm_mlp[0]), w_mlp_up[0].astype(BF16), w_mlp_down[0].astype(BF16),
        row2(norm_final))
    return out.reshape(b, seq, d)
```

```python
import functools
import math

import jax
import jax.numpy as jnp
from jax import lax
from jax.experimental import pallas as pl
from jax.experimental.pallas import tpu as pltpu

D_MODEL = 1024
A_HEADS = 8
A_HEAD_DIM = 64
A_WIDTH = A_HEADS * A_HEAD_DIM
MOBA_BLOCK = 256
MOBA_TOPK = 3
REL_BUCKETS = 32
REL_MAX_DIST = 128
B_HEADS = 8
QK_NOPE = 64
QK_ROPE = 32
V_HEAD = 64
B_WIDTH = B_HEADS * V_HEAD
Q_LORA = 384
KV_LORA = 256
ROPE_THETA = 10000.0
D_FF = 4 * D_MODEL
EPS = 1e-6
NEG = -1e30
LOG2E = math.log2(math.e)

LANES = 128
MLA_HEAD_PAD = 128
ATTN_TILE = 256
HEAD_GROUP = 4
SUM_ROWS = 16
PIPELINE_UNROLL = 8
ROW_TILE = 512
FF_CHUNK = 1024
VMEM_LIMIT = 56 * 1024 * 1024

F32 = jnp.float32
BF16 = jnp.bfloat16
NT_DIMS = (((1,), (1,)), ((), ()))


def _rms(xf, g):
    y = xf * lax.rsqrt(jnp.mean(xf * xf, axis=-1, keepdims=True) + EPS)
    return y * g


def _const_spec(shape):
    zeros = (0,) * len(shape)
    return pl.BlockSpec(shape, lambda *_: zeros, pipeline_mode=pl.Buffered(1))


def _bias_kernel(rel_ref, o_ref):
    h = pl.program_id(0)
    shape = (MOBA_BLOCK, 2 * MOBA_BLOCK)
    r = lax.broadcasted_iota(jnp.int32, shape, 0)
    c = lax.broadcasted_iota(jnp.int32, shape, 1)
    d = c - r
    dist = jnp.maximum(d, 0)
    max_exact = REL_BUCKETS // 2
    df = jnp.maximum(dist, 1).astype(F32)
    large = max_exact + (jnp.log(df / max_exact) / math.log(REL_MAX_DIST / max_exact)
                         * (REL_BUCKETS - max_exact)).astype(jnp.int32)
    large = jnp.minimum(large, REL_BUCKETS - 1)
    bucket = jnp.where(dist < max_exact, dist, large)
    val = jnp.zeros(shape, F32)
    for b in range(REL_BUCKETS):
        val = jnp.where(bucket == b, rel_ref[b, h], val)
    o_ref[...] = jnp.where(d >= 0, (val - rel_ref[REL_BUCKETS - 1, h]) * LOG2E, NEG)


def _bias_tiles(rel_bias):
    return pl.pallas_call(
        _bias_kernel,
        grid=(A_HEADS,),
        in_specs=[pl.BlockSpec(memory_space=pltpu.SMEM)],
        out_specs=pl.BlockSpec((None, MOBA_BLOCK, 2 * MOBA_BLOCK), lambda h: (h, 0, 0)),
        out_shape=jax.ShapeDtypeStruct((A_HEADS, MOBA_BLOCK, 2 * MOBA_BLOCK), F32),
        name="moba_bias_tiles",
    )(rel_bias)


def _inproj_kernel(x_ref, g_ref, wk_ref, wqvt_ref, wc_ref, qn_ref, kvn_ref, wuqt_ref,
                   wuk_ref, wuvt_ref, cqt_ref, sqt_ref, ck_ref, sk_ref,
                   qat_ref, ka_ref, vat_ref, kmean_ref, qmt_ref, km_ref, vbt_ref):
    n = _rms(x_ref[...], g_ref[...]).astype(BF16)
    k = jnp.dot(n, wk_ref[...], preferred_element_type=F32)
    ka_ref[...] = k.astype(BF16)
    nblk = ROW_TILE // ATTN_TILE
    kmean_ref[0] = jnp.sum(k.reshape(nblk, MOBA_BLOCK, A_WIDTH), axis=1) * (1.0 / MOBA_BLOCK)

    c = jnp.dot(n, wc_ref[...], preferred_element_type=F32)
    cq = _rms(c[:, :Q_LORA], qn_ref[...]).astype(BF16)
    ckv = _rms(c[:, Q_LORA:Q_LORA + KV_LORA], kvn_ref[...]).astype(BF16)
    kr = c[:, Q_LORA + KV_LORA:Q_LORA + KV_LORA + LANES]
    krr = c[:, Q_LORA + KV_LORA + LANES:]
    for t in range(nblk):
        rows = slice(t * ATTN_TILE, (t + 1) * ATTN_TILE)
        qv_t = lax.dot_general(wqvt_ref[...], n[rows], NT_DIMS,
                               preferred_element_type=F32)
        qat_ref[t] = (qv_t[:A_WIDTH] * (A_HEAD_DIM ** -0.5 * LOG2E)).astype(BF16)
        vat_ref[t] = qv_t[A_WIDTH:].astype(BF16)
        vbt_ref[t] = lax.dot_general(wuvt_ref[...], ckv[rows], NT_DIMS,
                                     preferred_element_type=F32).astype(BF16)
        q_t = lax.dot_general(wuqt_ref[...], cq[rows], NT_DIMS,
                              preferred_element_type=F32)
        cq_t, sq_t = cqt_ref[:, rows], sqt_ref[:, rows]
        half = QK_ROPE // 2
        for h in range(B_HEADS):
            q_h = q_t[h * MLA_HEAD_PAD:(h + 1) * MLA_HEAD_PAD]
            x1 = q_h[QK_NOPE:QK_NOPE + half]
            x2 = q_h[QK_NOPE + half:QK_NOPE + QK_ROPE]
            rot_h = jnp.concatenate(
                [jnp.zeros((QK_NOPE, ATTN_TILE), F32), -x2, x1,
                 jnp.zeros((MLA_HEAD_PAD - QK_NOPE - QK_ROPE, ATTN_TILE), F32)], axis=0)
            qmt_ref[t, h * MLA_HEAD_PAD:(h + 1) * MLA_HEAD_PAD, :] = (
                q_h * cq_t + rot_h * sq_t).astype(BF16)
    kn = jnp.dot(ckv, wuk_ref[...], preferred_element_type=F32)
    k_rope = kr * ck_ref[...] + krr * sk_ref[...]
    for h in range(B_HEADS):
        sl = slice(h * MLA_HEAD_PAD, (h + 1) * MLA_HEAD_PAD)
        km_ref[:, sl] = (kn[:, sl] + k_rope).astype(BF16)


def _inproj(x2, g_attn, wk, wqvt, wc, qn, kvn, wuqt, wuk, wuvt, cq_tt, sq_tt, ck_t, sk_t,
            batch, seq):
    n_rows = x2.shape[0]
    tm = ROW_TILE
    steps = n_rows // tm
    per_seq = seq // tm
    nblk = tm // ATTN_TILE
    row = lambda i: (i, 0)
    tab = lambda i: (i % per_seq, 0)
    tab_t = lambda i: (0, i % per_seq)
    tile_map = lambda i: (i // per_seq, i % per_seq, 0, 0)
    bf = lambda w: jax.ShapeDtypeStruct((n_rows, w), BF16)
    tiles = lambda w: jax.ShapeDtypeStruct((batch, seq // ATTN_TILE, w, ATTN_TILE), BF16)
    tile_spec = lambda w: pl.BlockSpec((None, nblk, w, ATTN_TILE), tile_map)
    return pl.pallas_call(
        _inproj_kernel,
        grid=(steps,),
        in_specs=[
            pl.BlockSpec((tm, D_MODEL), row),
            _const_spec(g_attn.shape), _const_spec(wk.shape), _const_spec(wqvt.shape),
            _const_spec(wc.shape), _const_spec(qn.shape), _const_spec(kvn.shape),
            _const_spec(wuqt.shape), _const_spec(wuk.shape),
            _const_spec(wuvt.shape),
            pl.BlockSpec((LANES, tm), tab_t), pl.BlockSpec((LANES, tm), tab_t),
            pl.BlockSpec((tm, LANES), tab), pl.BlockSpec((tm, LANES), tab),
        ],
        out_specs=[
            tile_spec(A_WIDTH), pl.BlockSpec((tm, A_WIDTH), row), tile_spec(A_WIDTH),
            pl.BlockSpec((1, nblk, A_WIDTH), lambda i: (i, 0, 0)),
            tile_spec(B_HEADS * MLA_HEAD_PAD),
            pl.BlockSpec((tm, B_HEADS * MLA_HEAD_PAD), row),
            tile_spec(B_WIDTH),
        ],
        out_shape=[
            tiles(A_WIDTH), bf(A_WIDTH), tiles(A_WIDTH),
            jax.ShapeDtypeStruct((steps, nblk, A_WIDTH), F32),
            tiles(B_HEADS * MLA_HEAD_PAD), bf(B_HEADS * MLA_HEAD_PAD), tiles(B_WIDTH),
        ],
        compiler_params=pltpu.CompilerParams(
            dimension_semantics=("arbitrary",), vmem_limit_bytes=VMEM_LIMIT),
        name="in_proj",
    )(x2, g_attn, wk, wqvt, wc, qn, kvn, wuqt, wuk, wuvt, cq_tt, sq_tt, ck_t, sk_t)


def _sum_rows():
    r = lax.broadcasted_iota(jnp.int32, (SUM_ROWS, ATTN_TILE), 0)
    return (r == 0).astype(BF16)


def _store_scores(buf, h, s_t):
    s_ref, max_ref = buf
    s_ref[h] = s_t
    max_ref[h] = jnp.max(s_t, axis=0, keepdims=True)


def _softmax_step(buf, h, v_t, m_ref, acc_ref, first, col_keep=None):
    s_ref, max_ref = buf[0].at[h], buf[1].at[h]
    s_max = max_ref[...]
    if col_keep is not None:
        s_max = jnp.where(col_keep > 0, s_max, NEG)
    if first:
        m_new = s_max
    else:
        m_old = m_ref[...]
        m_new = jnp.maximum(m_old, s_max)
    v_aug = jnp.concatenate([v_t, _sum_rows()], axis=0)
    m_exp = m_new if col_keep is None else jnp.maximum(m_new, max_ref[...])
    p_t = jnp.exp2(s_ref[...] - m_exp).astype(BF16)
    pv = jnp.dot(v_aug, p_t, preferred_element_type=F32)
    if col_keep is not None:
        pv = pv * col_keep
    if first:
        acc_ref[...] = pv
    else:
        acc_ref[...] = jnp.exp2(m_old - m_new) * acc_ref[...] + pv
    m_ref[...] = m_new


def _pipeline(n_items, score, attend, bufs):
    score(0, bufs[0])
    n_loops = (n_items - 1) // PIPELINE_UNROLL

    def body(p, carry):
        for t in range(PIPELINE_UNROLL):
            n = PIPELINE_UNROLL * p + t
            score(n + 1, bufs[(t + 1) % 2])
            attend(n, bufs[t % 2])
        return carry

    lax.fori_loop(0, n_loops, body, 0)
    done = PIPELINE_UNROLL * n_loops
    for n in range(done, n_items):
        if n + 1 < n_items:
            score(n + 1, bufs[(n + 1 - done) % 2])
        attend(n, bufs[(n - done) % 2])


def _tile_rows(j):
    return pl.ds(pl.multiple_of(j * ATTN_TILE, ATTN_TILE), ATTN_TILE)


def _write_heads(acc_ref, o_ref, nq, head_dim):
    def one(qi, carry):
        outs = []
        for h in range(HEAD_GROUP):
            acc = acc_ref[qi, h]
            outs.append(acc[:head_dim] / acc[head_dim:head_dim + 1])
        o_ref[_tile_rows(qi), :] = jnp.concatenate(outs, axis=0).T.astype(o_ref.dtype)
        return carry

    lax.fori_loop(0, nq, one, 0)


def _causal_items(nq, first_past):
    phases = [[(i, i - d) for i in range(d, nq)] for d in range(first_past)]
    phases.append([(i, j) for i in range(first_past, nq) for j in range(i - first_past + 1)])
    flat = [it for ph in phases for it in ph]
    tq = jnp.asarray([it[0] for it in flat], jnp.int32)
    tk = jnp.asarray([it[1] for it in flat], jnp.int32)
    return tq, tk, [len(ph) for ph in phases]


def _moba_kernel(tq_ref, tk_ref, qt_ref, k_ref, vt_ref, kmean_ref, bias_ref, o_ref,
                 keep_ref, s0_ref, s1_ref, t0_ref, t1_ref, m_ref, acc_ref, *, seq, phases):
    nblk = seq // MOBA_BLOCK

    def head_query(qi, h):
        pr, sub = divmod(h, 2)
        q_h = qt_ref[qi, pl.ds(pr * LANES + sub * A_HEAD_DIM, A_HEAD_DIM), :]
        zeros = jnp.zeros_like(q_h)
        return jnp.concatenate([q_h, zeros] if sub == 0 else [zeros, q_h], axis=0)

    def select_blocks(i):
        row = lax.broadcasted_iota(jnp.int32, (nblk, ATTN_TILE), 0)
        rowf = row.astype(F32)
        past = row < i
        for h in range(HEAD_GROUP):
            kmean = kmean_ref[:, (h // 2) * LANES:(h // 2 + 1) * LANES].astype(BF16)
            gate = jnp.dot(kmean, head_query(i, h), preferred_element_type=F32)
            gate = jnp.where(past, gate, -jnp.inf)
            picked = jnp.zeros((nblk, ATTN_TILE), F32)
            for _ in range(MOBA_TOPK):
                top = jnp.max(gate, axis=0, keepdims=True)
                first = jnp.min(jnp.where(gate == top, rowf, float(nblk)), axis=0, keepdims=True)
                pick = rowf == first
                picked = jnp.where(pick, 1.0, picked)
                gate = jnp.where(pick, -jnp.inf, gate)
            keep_ref[i, h] = jnp.where(past, picked, 0.0)

    def stage(offset, bias_cols, first):
        def score(n, buf):
            qi, kj = tq_ref[offset + n], tk_ref[offset + n]
            for h in range(HEAD_GROUP):
                k_t = k_ref[_tile_rows(kj), (h // 2) * LANES:(h // 2 + 1) * LANES]
                s_t = jnp.dot(k_t, head_query(qi, h), preferred_element_type=F32)
                if bias_cols is not None:
                    s_t = s_t + bias_ref[h, :, bias_cols]
                _store_scores(buf, h, s_t)

        def attend(n, buf):
            qi, kj = tq_ref[offset + n], tk_ref[offset + n]
            for h in range(HEAD_GROUP):
                col_keep = None if first else keep_ref[qi, h, pl.ds(kj, 1), :]
                _softmax_step(buf, h, vt_ref[kj, h * A_HEAD_DIM:(h + 1) * A_HEAD_DIM, :],
                              m_ref.at[qi, h], acc_ref.at[qi, h], first, col_keep)
            if first:
                select_blocks(qi)

        return score, attend

    bufs = ((s0_ref, t0_ref), (s1_ref, t1_ref))
    n_own, n_prev, n_far = phases
    _pipeline(n_own, *stage(0, slice(0, MOBA_BLOCK), True), bufs)
    _pipeline(n_prev, *stage(n_own, slice(MOBA_BLOCK, 2 * MOBA_BLOCK), False), bufs)
    _pipeline(n_far, *stage(n_own + n_prev, None, False), bufs)
    _write_heads(acc_ref, o_ref, nblk, A_HEAD_DIM)


def _moba(qat, ka, vat, kmean, bias):
    b, seq, _ = ka.shape
    nq = seq // ATTN_TILE
    gw = HEAD_GROUP * A_HEAD_DIM
    tq, tk, phases = _causal_items(nq, 2)
    per_group = lambda bi, g, *_: (bi, 0, g)
    tiles = lambda bi, g, *_: (bi, 0, g, 0)
    return pl.pallas_call(
        functools.partial(_moba_kernel, seq=seq, phases=phases),
        grid_spec=pltpu.PrefetchScalarGridSpec(
            num_scalar_prefetch=2,
            grid=(b, A_HEADS // HEAD_GROUP),
            in_specs=[
                pl.BlockSpec((None, nq, gw, ATTN_TILE), tiles),
                pl.BlockSpec((None, seq, gw), per_group),
                pl.BlockSpec((None, nq, gw, MOBA_BLOCK), tiles),
                pl.BlockSpec((None, nq, gw), per_group),
                pl.BlockSpec((HEAD_GROUP, MOBA_BLOCK, 2 * MOBA_BLOCK), lambda bi, g, *_: (g, 0, 0)),
            ],
            out_specs=pl.BlockSpec((None, seq, gw), per_group),
            scratch_shapes=[
                pltpu.VMEM((nq, HEAD_GROUP, nq, ATTN_TILE), F32),
                pltpu.VMEM((HEAD_GROUP, ATTN_TILE, ATTN_TILE), F32),
                pltpu.VMEM((HEAD_GROUP, ATTN_TILE, ATTN_TILE), F32),
                pltpu.VMEM((HEAD_GROUP, 1, ATTN_TILE), F32),
                pltpu.VMEM((HEAD_GROUP, 1, ATTN_TILE), F32),
                pltpu.VMEM((nq, HEAD_GROUP, 1, ATTN_TILE), F32),
                pltpu.VMEM((nq, HEAD_GROUP, A_HEAD_DIM + SUM_ROWS, ATTN_TILE), F32),
            ]),
        out_shape=jax.ShapeDtypeStruct((b, seq, A_WIDTH), BF16),
        compiler_params=pltpu.CompilerParams(
            dimension_semantics=("arbitrary", "arbitrary"), vmem_limit_bytes=VMEM_LIMIT),
        name="moba_attention",
    )(tq, tk, qat, ka, vat, kmean, bias)


def _mla_kernel(tq_ref, tk_ref, qt_ref, k_ref, vt_ref, o_ref, s0_ref, s1_ref, t0_ref, t1_ref,
                m_ref, acc_ref, *, nq, phases):
    def stage(offset, diagonal):
        def score(n, buf):
            qi, kj = tq_ref[offset + n], tk_ref[offset + n]
            for h in range(HEAD_GROUP):
                sl = slice(h * MLA_HEAD_PAD, (h + 1) * MLA_HEAD_PAD)
                s_t = jnp.dot(k_ref[_tile_rows(kj), sl], qt_ref[qi, sl, :],
                              preferred_element_type=F32)
                if diagonal:
                    key = lax.broadcasted_iota(jnp.int32, (ATTN_TILE, ATTN_TILE), 0)
                    qry = lax.broadcasted_iota(jnp.int32, (ATTN_TILE, ATTN_TILE), 1)
                    s_t = jnp.where(key <= qry, s_t, NEG)
                _store_scores(buf, h, s_t)

        def attend(n, buf):
            qi, kj = tq_ref[offset + n], tk_ref[offset + n]
            for h in range(HEAD_GROUP):
                _softmax_step(buf, h, vt_ref[kj, h * V_HEAD:(h + 1) * V_HEAD, :],
                              m_ref.at[qi, h], acc_ref.at[qi, h], diagonal)

        return score, attend

    bufs = ((s0_ref, t0_ref), (s1_ref, t1_ref))
    n_diag, n_past = phases
    _pipeline(n_diag, *stage(0, True), bufs)
    _pipeline(n_past, *stage(n_diag, False), bufs)
    _write_heads(acc_ref, o_ref, nq, V_HEAD)


def _mla(qmt, km, vbt):
    b, seq, _ = km.shape
    nq = seq // ATTN_TILE
    qw = HEAD_GROUP * MLA_HEAD_PAD
    vw = HEAD_GROUP * V_HEAD
    tq, tk, phases = _causal_items(nq, 1)
    per_group = lambda bi, g, *_: (bi, 0, g)
    tiles = lambda bi, g, *_: (bi, 0, g, 0)
    return pl.pallas_call(
        functools.partial(_mla_kernel, nq=nq, phases=phases),
        grid_spec=pltpu.PrefetchScalarGridSpec(
            num_scalar_prefetch=2,
            grid=(b, B_HEADS // HEAD_GROUP),
            in_specs=[
                pl.BlockSpec((None, nq, qw, ATTN_TILE), tiles),
                pl.BlockSpec((None, seq, qw), per_group),
                pl.BlockSpec((None, nq, vw, ATTN_TILE), tiles),
            ],
            out_specs=pl.BlockSpec((None, seq, vw), per_group),
            scratch_shapes=[
                pltpu.VMEM((HEAD_GROUP, ATTN_TILE, ATTN_TILE), F32),
                pltpu.VMEM((HEAD_GROUP, ATTN_TILE, ATTN_TILE), F32),
                pltpu.VMEM((HEAD_GROUP, 1, ATTN_TILE), F32),
                pltpu.VMEM((HEAD_GROUP, 1, ATTN_TILE), F32),
                pltpu.VMEM((nq, HEAD_GROUP, 1, ATTN_TILE), F32),
                pltpu.VMEM((nq, HEAD_GROUP, V_HEAD + SUM_ROWS, ATTN_TILE), F32),
            ]),
        out_shape=jax.ShapeDtypeStruct((b, seq, B_WIDTH), BF16),
        compiler_params=pltpu.CompilerParams(
            dimension_semantics=("arbitrary", "arbitrary"), vmem_limit_bytes=VMEM_LIMIT),
        name="mla_attention",
    )(tq, tk, qmt, km, vbt)


def _out_kernel(x_ref, oa_ref, ob_ref, gattn_ref, wg_ref, wpa_ref, wpb_ref, wout_ref,
                gmlp_ref, wup_ref, wdn_ref, gfin_ref, o_ref):
    x = x_ref[...]
    n = _rms(x, gattn_ref[...]).astype(BF16)
    gates = jnp.dot(n, wg_ref[...], preferred_element_type=F32)
    pa = jnp.dot(oa_ref[...], wpa_ref[...], preferred_element_type=F32)
    pb = jnp.dot(ob_ref[...], wpb_ref[...], preferred_element_type=F32)
    merged = (jax.nn.sigmoid(gates[:, :D_MODEL]) * pa
              + jax.nn.sigmoid(gates[:, D_MODEL:]) * pb).astype(BF16)
    h = x + jnp.dot(merged, wout_ref[...], preferred_element_type=F32)
    m = _rms(h, gmlp_ref[...]).astype(BF16)
    for c in range(D_FF // FF_CHUNK):
        cols = slice(c * FF_CHUNK, (c + 1) * FF_CHUNK)
        up = jnp.dot(m, wup_ref[:, cols], preferred_element_type=F32)
        act = jnp.square(jnp.maximum(up, 0.0)).astype(BF16)
        h = h + jnp.dot(act, wdn_ref[cols, :], preferred_element_type=F32)
    o_ref[...] = _rms(h, gfin_ref[...])


def _out_block(x2, oa, ob, g_attn, wg, wpa, wpb, wout, g_mlp, wup, wdn, g_fin):
    n_rows = x2.shape[0]
    tm = ROW_TILE
    row = lambda i: (i, 0)
    return pl.pallas_call(
        _out_kernel,
        grid=(n_rows // tm,),
        in_specs=[
            pl.BlockSpec((tm, D_MODEL), row),
            pl.BlockSpec((tm, A_WIDTH), row), pl.BlockSpec((tm, B_WIDTH), row),
            _const_spec(g_attn.shape), _const_spec(wg.shape), _const_spec(wpa.shape),
            _const_spec(wpb.shape), _const_spec(wout.shape), _const_spec(g_mlp.shape),
            _const_spec(wup.shape), _const_spec(wdn.shape), _const_spec(g_fin.shape),
        ],
        out_specs=pl.BlockSpec((tm, D_MODEL), row),
        out_shape=jax.ShapeDtypeStruct((n_rows, D_MODEL), F32),
        compiler_params=pltpu.CompilerParams(
            dimension_semantics=("arbitrary",), vmem_limit_bytes=VMEM_LIMIT),
        name="out_mlp",
    )(x2, oa, ob, g_attn, wg, wpa, wpb, wout, g_mlp, wup, wdn, g_fin)


def _rope_tables(seq):
    half = QK_ROPE // 2
    inv_freq = ROPE_THETA ** (-jnp.arange(half, dtype=F32) / half)
    ang = jnp.arange(seq).astype(F32)[:, None] * inv_freq[None, :]
    cos2 = jnp.tile(jnp.cos(ang), (1, 2))
    sin2 = jnp.tile(jnp.sin(ang), (1, 2))
    pad = jnp.zeros((seq, MLA_HEAD_PAD - QK_NOPE - QK_ROPE), F32)
    cos_t = jnp.concatenate([jnp.ones((seq, QK_NOPE), F32), cos2, pad], axis=1)
    sin_t = jnp.concatenate([jnp.zeros((seq, QK_NOPE), F32), sin2, pad], axis=1)
    return cos_t, sin_t


def _rotate_half_cols(w):
    half = QK_ROPE // 2
    return jnp.concatenate([-w[..., half:], w[..., :half]], axis=-1)


def _pad_heads(w_nope, w_rope):
    k, h, _ = w_nope.shape
    if w_rope is None:
        w_rope = jnp.zeros((k, h, QK_ROPE), w_nope.dtype)
    pad = jnp.zeros((k, h, MLA_HEAD_PAD - QK_NOPE - QK_ROPE), w_nope.dtype)
    return jnp.concatenate([w_nope, w_rope, pad], axis=-1).reshape(k, h * MLA_HEAD_PAD)


def kernel(x, w_in, rel_bias, mla_q_norm, w_uq, mla_kv_norm, w_uk, w_uv, w_proj_a, w_proj_b,
           w_out, norm_attn, norm_mlp, w_mlp_up, w_mlp_down, norm_final):
    b, seq, d = x.shape
    assert d == D_MODEL and seq % ROW_TILE == 0 and seq // MOBA_BLOCK <= LANES
    assert w_in.shape[0] == 1, "single-layer block"
    x2 = x.reshape(b * seq, d)

    w = w_in[0]
    o_k = A_WIDTH
    o_v = 2 * A_WIDTH
    o_cq = 3 * A_WIDTH
    o_ckv = o_cq + Q_LORA
    o_kr = o_ckv + KV_LORA
    o_g = o_kr + QK_ROPE
    wk = w[:, o_k:o_v].astype(BF16)
    wqvt = jnp.concatenate([w[:, :o_k], w[:, o_v:o_cq]], axis=1).T.astype(BF16)
    w_kr = w[:, o_kr:o_g]
    lane_pad = lambda a: jnp.pad(a, ((0, 0), (QK_NOPE, MLA_HEAD_PAD - QK_NOPE - QK_ROPE)))
    wc = jnp.concatenate(
        [w[:, o_cq:o_kr], lane_pad(w_kr), lane_pad(_rotate_half_cols(w_kr))], axis=1).astype(BF16)
    wg = w[:, o_g:].astype(BF16)

    uq = w_uq[0].reshape(Q_LORA, B_HEADS, QK_NOPE + QK_ROPE)
    uq_nope, uq_rope = uq[..., :QK_NOPE], uq[..., QK_NOPE:]
    wuqt = _pad_heads(uq_nope, uq_rope).T.astype(BF16)
    wuk = _pad_heads(w_uk[0].reshape(KV_LORA, B_HEADS, QK_NOPE), None).astype(BF16)
    wuvt = w_uv[0].T.astype(BF16)

    cos_t, sin_t = _rope_tables(seq)
    q_scale = (QK_NOPE + QK_ROPE) ** -0.5 * LOG2E
    row2 = lambda a: a.reshape(1, -1)

    qat, ka, vat, kmean, qmt, km, vbt = _inproj(
        x2, row2(norm_attn[0]), wk, wqvt, wc, row2(mla_q_norm[0]), row2(mla_kv_norm[0]),
        wuqt, wuk, wuvt, (cos_t * q_scale).T, (sin_t * q_scale).T, cos_t, sin_t, b, seq)

    as_seq = lambda a: a.reshape(b, seq, a.shape[-1])
    bias = _bias_tiles(rel_bias)
    oa = _moba(qat, as_seq(ka), vat, kmean.reshape(b, seq // MOBA_BLOCK, A_WIDTH), bias)
    ob = _mla(qmt, as_seq(km), vbt)

    out = _out_block(
        x2, oa.reshape(b * seq, A_WIDTH), ob.reshape(b * seq, B_WIDTH), row2(norm_attn[0]),
        wg, w_proj_a[0].astype(BF16), w_proj_b[0].astype(BF16), w_out[0].astype(BF16),
        row2(norm_mlp[0]), w_mlp_up[0].astype(BF16), w_mlp_down[0].astype(BF16),
        row2(norm_final))
    return out.reshape(b, seq, d)
```

```python
import functools
import math

import jax
import jax.numpy as jnp
from jax import lax
from jax.experimental import pallas as pl
from jax.experimental.pallas import tpu as pltpu

D_MODEL = 1024
A_HEADS = 8
A_HEAD_DIM = 64
A_WIDTH = A_HEADS * A_HEAD_DIM
MOBA_BLOCK = 256
MOBA_TOPK = 3
REL_BUCKETS = 32
REL_MAX_DIST = 128
B_HEADS = 8
QK_NOPE = 64
QK_ROPE = 32
V_HEAD = 64
B_WIDTH = B_HEADS * V_HEAD
Q_LORA = 384
KV_LORA = 256
ROPE_THETA = 10000.0
D_FF = 4 * D_MODEL
EPS = 1e-6
NEG = -1e30
LOG2E = math.log2(math.e)

LANES = 128
MLA_HEAD_PAD = 128
ATTN_TILE = 256
HEAD_GROUP = 4
SUM_ROWS = 16
PIPELINE_UNROLL = 8
ROW_TILE = 512
FF_CHUNK = 1024
VMEM_LIMIT = 56 * 1024 * 1024

F32 = jnp.float32
BF16 = jnp.bfloat16
NT_DIMS = (((1,), (1,)), ((), ()))


def _rms(xf, g):
    y = xf * lax.rsqrt(jnp.mean(xf * xf, axis=-1, keepdims=True) + EPS)
    return y * g


def _const_spec(shape):
    zeros = (0,) * len(shape)
    return pl.BlockSpec(shape, lambda *_: zeros, pipeline_mode=pl.Buffered(1))


def _bias_kernel(rel_ref, o_ref):
    h = pl.program_id(0)
    shape = (MOBA_BLOCK, 2 * MOBA_BLOCK)
    r = lax.broadcasted_iota(jnp.int32, shape, 0)
    c = lax.broadcasted_iota(jnp.int32, shape, 1)
    d = c - r
    dist = jnp.maximum(d, 0)
    max_exact = REL_BUCKETS // 2
    df = jnp.maximum(dist, 1).astype(F32)
    large = max_exact + (jnp.log(df / max_exact) / math.log(REL_MAX_DIST / max_exact)
                         * (REL_BUCKETS - max_exact)).astype(jnp.int32)
    large = jnp.minimum(large, REL_BUCKETS - 1)
    bucket = jnp.where(dist < max_exact, dist, large)
    val = jnp.zeros(shape, F32)
    for b in range(REL_BUCKETS):
        val = jnp.where(bucket == b, rel_ref[b, h], val)
    o_ref[...] = jnp.where(d >= 0, (val - rel_ref[REL_BUCKETS - 1, h]) * LOG2E, NEG)


def _bias_tiles(rel_bias):
    return pl.pallas_call(
        _bias_kernel,
        grid=(A_HEADS,),
        in_specs=[pl.BlockSpec(memory_space=pltpu.SMEM)],
        out_specs=pl.BlockSpec((None, MOBA_BLOCK, 2 * MOBA_BLOCK), lambda h: (h, 0, 0)),
        out_shape=jax.ShapeDtypeStruct((A_HEADS, MOBA_BLOCK, 2 * MOBA_BLOCK), F32),
        name="moba_bias_tiles",
    )(rel_bias)


def _inproj_kernel(x_ref, g_ref, wk_ref, wqvt_ref, wc_ref, qn_ref, kvn_ref, wuqt_ref,
                   wuk_ref, wuvt_ref, cqt_ref, sqt_ref, ck_ref, sk_ref,
                   qat_ref, ka_ref, vat_ref, kmean_ref, qmt_ref, km_ref, vbt_ref):
    n = _rms(x_ref[...], g_ref[...]).astype(BF16)
    k = jnp.dot(n, wk_ref[...], preferred_element_type=F32)
    ka_ref[...] = k.astype(BF16)
    nblk = ROW_TILE // ATTN_TILE
    kmean_ref[0] = jnp.sum(k.reshape(nblk, MOBA_BLOCK, A_WIDTH), axis=1) * (1.0 / MOBA_BLOCK)

    c = jnp.dot(n, wc_ref[...], preferred_element_type=F32)
    cq = _rms(c[:, :Q_LORA], qn_ref[...]).astype(BF16)
    ckv = _rms(c[:, Q_LORA:Q_LORA + KV_LORA], kvn_ref[...]).astype(BF16)
    kr = c[:, Q_LORA + KV_LORA:Q_LORA + KV_LORA + LANES]
    krr = c[:, Q_LORA + KV_LORA + LANES:]
    for t in range(nblk):
        rows = slice(t * ATTN_TILE, (t + 1) * ATTN_TILE)
        qv_t = lax.dot_general(wqvt_ref[...], n[rows], NT_DIMS,
                               preferred_element_type=F32)
        qat_ref[t] = (qv_t[:A_WIDTH] * (A_HEAD_DIM ** -0.5 * LOG2E)).astype(BF16)
        vat_ref[t] = qv_t[A_WIDTH:].astype(BF16)
        vbt_ref[t] = lax.dot_general(wuvt_ref[...], ckv[rows], NT_DIMS,
                                     preferred_element_type=F32).astype(BF16)
        q_t = lax.dot_general(wuqt_ref[...], cq[rows], NT_DIMS,
                              preferred_element_type=F32)
        cq_t, sq_t = cqt_ref[:, rows], sqt_ref[:, rows]
        half = QK_ROPE // 2
        for h in range(B_HEADS):
            q_h = q_t[h * MLA_HEAD_PAD:(h + 1) * MLA_HEAD_PAD]
            x1 = q_h[QK_NOPE:QK_NOPE + half]
            x2 = q_h[QK_NOPE + half:QK_NOPE + QK_ROPE]
            rot_h = jnp.concatenate(
                [jnp.zeros((QK_NOPE, ATTN_TILE), F32), -x2, x1,
                 jnp.zeros((MLA_HEAD_PAD - QK_NOPE - QK_ROPE, ATTN_TILE), F32)], axis=0)
            qmt_ref[t, h * MLA_HEAD_PAD:(h + 1) * MLA_HEAD_PAD, :] = (
                q_h * cq_t + rot_h * sq_t).astype(BF16)
    kn = jnp.dot(ckv, wuk_ref[...], preferred_element_type=F32)
    k_rope = kr * ck_ref[...] + krr * sk_ref[...]
    for h in range(B_HEADS):
        sl = slice(h * MLA_HEAD_PAD, (h + 1) * MLA_HEAD_PAD)
        km_ref[:, sl] = (kn[:, sl] + k_rope).astype(BF16)


def _inproj(x2, g_attn, wk, wqvt, wc, qn, kvn, wuqt, wuk, wuvt, cq_tt, sq_tt, ck_t, sk_t,
            batch, seq):
    n_rows = x2.shape[0]
    tm = ROW_TILE
    steps = n_rows // tm
    per_seq = seq // tm
    nblk = tm // ATTN_TILE
    row = lambda i: (i, 0)
    tab = lambda i: (i % per_seq, 0)
    tab_t = lambda i: (0, i % per_seq)
    tile_map = lambda i: (i // per_seq, i % per_seq, 0, 0)
    bf = lambda w: jax.ShapeDtypeStruct((n_rows, w), BF16)
    tiles = lambda w: jax.ShapeDtypeStruct((batch, seq // ATTN_TILE, w, ATTN_TILE), BF16)
    tile_spec = lambda w: pl.BlockSpec((None, nblk, w, ATTN_TILE), tile_map)
    return pl.pallas_call(
        _inproj_kernel,
        grid=(steps,),
        in_specs=[
            pl.BlockSpec((tm, D_MODEL), row),
            _const_spec(g_attn.shape), _const_spec(wk.shape), _const_spec(wqvt.shape),
            _const_spec(wc.shape), _const_spec(qn.shape), _const_spec(kvn.shape),
            _const_spec(wuqt.shape), _const_spec(wuk.shape),
            _const_spec(wuvt.shape),
            pl.BlockSpec((LANES, tm), tab_t), pl.BlockSpec((LANES, tm), tab_t),
            pl.BlockSpec((tm, LANES), tab), pl.BlockSpec((tm, LANES), tab),
        ],
        out_specs=[
            tile_spec(A_WIDTH), pl.BlockSpec((tm, A_WIDTH), row), tile_spec(A_WIDTH),
            pl.BlockSpec((1, nblk, A_WIDTH), lambda i: (i, 0, 0)),
            tile_spec(B_HEADS * MLA_HEAD_PAD),
            pl.BlockSpec((tm, B_HEADS * MLA_HEAD_PAD), row),
            tile_spec(B_WIDTH),
        ],
        out_shape=[
            tiles(A_WIDTH), bf(A_WIDTH), tiles(A_WIDTH),
            jax.ShapeDtypeStruct((steps, nblk, A_WIDTH), F32),
            tiles(B_HEADS * MLA_HEAD_PAD), bf(B_HEADS * MLA_HEAD_PAD), tiles(B_WIDTH),
        ],
        compiler_params=pltpu.CompilerParams(
            dimension_semantics=("arbitrary",), vmem_limit_bytes=VMEM_LIMIT),
        name="in_proj",
    )(x2, g_attn, wk, wqvt, wc, qn, kvn, wuqt, wuk, wuvt, cq_tt, sq_tt, ck_t, sk_t)


def _sum_rows():
    r = lax.broadcasted_iota(jnp.int32, (SUM_ROWS, ATTN_TILE), 0)
    return (r == 0).astype(BF16)


def _store_scores(buf, h, s_t):
    s_ref, max_ref = buf
    s_ref[h] = s_t
    max_ref[h] = jnp.max(s_t, axis=0, keepdims=True)


def _softmax_step(buf, h, v_t, m_ref, acc_ref, first, col_keep=None):
    s_ref, max_ref = buf[0].at[h], buf[1].at[h]
    s_max = max_ref[...]
    if col_keep is not None:
        s_max = jnp.where(col_keep > 0, s_max, NEG)
    if first:
        m_new = s_max
    else:
        m_old = m_ref[...]
        m_new = jnp.maximum(m_old, s_max)
    v_aug = jnp.concatenate([v_t, _sum_rows()], axis=0)
    m_exp = m_new if col_keep is None else jnp.maximum(m_new, max_ref[...])
    p_t = jnp.exp2(s_ref[...] - m_exp).astype(BF16)
    pv = jnp.dot(v_aug, p_t, preferred_element_type=F32)
    if col_keep is not None:
        pv = pv * col_keep
    if first:
        acc_ref[...] = pv
    else:
        acc_ref[...] = jnp.exp2(m_old - m_new) * acc_ref[...] + pv
    m_ref[...] = m_new


def _pipeline(n_items, score, attend, bufs):
    score(0, bufs[0])
    n_loops = (n_items - 1) // PIPELINE_UNROLL

    def body(p, carry):
        for t in range(PIPELINE_UNROLL):
            n = PIPELINE_UNROLL * p + t
            score(n + 1, bufs[(t + 1) % 2])
            attend(n, bufs[t % 2])
        return carry

    lax.fori_loop(0, n_loops, body, 0)
    done = PIPELINE_UNROLL * n_loops
    for n in range(done, n_items):
        if n + 1 < n_items:
            score(n + 1, bufs[(n + 1 - done) % 2])
        attend(n, bufs[(n - done) % 2])


def _tile_rows(j):
    return pl.ds(pl.multiple_of(j * ATTN_TILE, ATTN_TILE), ATTN_TILE)


def _write_heads(acc_ref, o_ref, nq, head_dim):
    def one(qi, carry):
        for h in range(HEAD_GROUP):
            acc = acc_ref[qi, h]
            o_ref[qi, h * head_dim:(h + 1) * head_dim, :] = (
                acc[:head_dim] / acc[head_dim:head_dim + 1]).astype(o_ref.dtype)
        return carry

    lax.fori_loop(0, nq, one, 0)


def _causal_items(nq, first_past):
    phases = [[(i, i - d) for i in range(d, nq)] for d in range(first_past)]
    phases.append([(i, j) for i in range(first_past, nq) for j in range(i - first_past + 1)])
    flat = [it for ph in phases for it in ph]
    tq = jnp.asarray([it[0] for it in flat], jnp.int32)
    tk = jnp.asarray([it[1] for it in flat], jnp.int32)
    return tq, tk, [len(ph) for ph in phases]


def _moba_kernel(tq_ref, tk_ref, qt_ref, k_ref, vt_ref, kmean_ref, bias_ref, o_ref,
                 keep_ref, s0_ref, s1_ref, t0_ref, t1_ref, m_ref, acc_ref, *, seq, phases):
    nblk = seq // MOBA_BLOCK

    def head_query(qi, h):
        pr, sub = divmod(h, 2)
        q_h = qt_ref[qi, pl.ds(pr * LANES + sub * A_HEAD_DIM, A_HEAD_DIM), :]
        zeros = jnp.zeros_like(q_h)
        return jnp.concatenate([q_h, zeros] if sub == 0 else [zeros, q_h], axis=0)

    def select_blocks(i):
        row = lax.broadcasted_iota(jnp.int32, (nblk, ATTN_TILE), 0)
        rowf = row.astype(F32)
        past = row < i
        for h in range(HEAD_GROUP):
            kmean = kmean_ref[:, (h // 2) * LANES:(h // 2 + 1) * LANES].astype(BF16)
            gate = jnp.dot(kmean, head_query(i, h), preferred_element_type=F32)
            gate = jnp.where(past, gate, -jnp.inf)
            picked = jnp.zeros((nblk, ATTN_TILE), F32)
            for _ in range(MOBA_TOPK):
                top = jnp.max(gate, axis=0, keepdims=True)
                first = jnp.min(jnp.where(gate == top, rowf, float(nblk)), axis=0, keepdims=True)
                pick = rowf == first
                picked = jnp.where(pick, 1.0, picked)
                gate = jnp.where(pick, -jnp.inf, gate)
            keep_ref[i, h] = jnp.where(past, picked, 0.0)

    def stage(offset, bias_cols, first):
        def score(n, buf):
            qi, kj = tq_ref[offset + n], tk_ref[offset + n]
            for h in range(HEAD_GROUP):
                k_t = k_ref[_tile_rows(kj), (h // 2) * LANES:(h // 2 + 1) * LANES]
                s_t = jnp.dot(k_t, head_query(qi, h), preferred_element_type=F32)
                if bias_cols is not None:
                    s_t = s_t + bias_ref[h, :, bias_cols]
                _store_scores(buf, h, s_t)

        def attend(n, buf):
            qi, kj = tq_ref[offset + n], tk_ref[offset + n]
            for h in range(HEAD_GROUP):
                col_keep = None if first else keep_ref[qi, h, pl.ds(kj, 1), :]
                _softmax_step(buf, h, vt_ref[kj, h * A_HEAD_DIM:(h + 1) * A_HEAD_DIM, :],
                              m_ref.at[qi, h], acc_ref.at[qi, h], first, col_keep)
            if first:
                select_blocks(qi)

        return score, attend

    bufs = ((s0_ref, t0_ref), (s1_ref, t1_ref))
    n_own, n_prev, n_far = phases
    _pipeline(n_own, *stage(0, slice(0, MOBA_BLOCK), True), bufs)
    _pipeline(n_prev, *stage(n_own, slice(MOBA_BLOCK, 2 * MOBA_BLOCK), False), bufs)
    _pipeline(n_far, *stage(n_own + n_prev, None, False), bufs)
    _write_heads(acc_ref, o_ref, nblk, A_HEAD_DIM)


def _moba(qat, ka, vat, kmean, bias):
    b, seq, _ = ka.shape
    nq = seq // ATTN_TILE
    gw = HEAD_GROUP * A_HEAD_DIM
    tq, tk, phases = _causal_items(nq, 2)
    per_group = lambda bi, g, *_: (bi, 0, g)
    tiles = lambda bi, g, *_: (bi, 0, g, 0)
    return pl.pallas_call(
        functools.partial(_moba_kernel, seq=seq, phases=phases),
        grid_spec=pltpu.PrefetchScalarGridSpec(
            num_scalar_prefetch=2,
            grid=(b, A_HEADS // HEAD_GROUP),
            in_specs=[
                pl.BlockSpec((None, nq, gw, ATTN_TILE), tiles),
                pl.BlockSpec((None, seq, gw), per_group),
                pl.BlockSpec((None, nq, gw, MOBA_BLOCK), tiles),
                pl.BlockSpec((None, nq, gw), per_group),
                pl.BlockSpec((HEAD_GROUP, MOBA_BLOCK, 2 * MOBA_BLOCK), lambda bi, g, *_: (g, 0, 0)),
            ],
            out_specs=pl.BlockSpec((None, nq, gw, ATTN_TILE), tiles),
            scratch_shapes=[
                pltpu.VMEM((nq, HEAD_GROUP, nq, ATTN_TILE), F32),
                pltpu.VMEM((HEAD_GROUP, ATTN_TILE, ATTN_TILE), F32),
                pltpu.VMEM((HEAD_GROUP, ATTN_TILE, ATTN_TILE), F32),
                pltpu.VMEM((HEAD_GROUP, 1, ATTN_TILE), F32),
                pltpu.VMEM((HEAD_GROUP, 1, ATTN_TILE), F32),
                pltpu.VMEM((nq, HEAD_GROUP, 1, ATTN_TILE), F32),
                pltpu.VMEM((nq, HEAD_GROUP, A_HEAD_DIM + SUM_ROWS, ATTN_TILE), F32),
            ]),
        out_shape=jax.ShapeDtypeStruct((b, nq, A_WIDTH, ATTN_TILE), BF16),
        compiler_params=pltpu.CompilerParams(
            dimension_semantics=("arbitrary", "arbitrary"), vmem_limit_bytes=VMEM_LIMIT),
        name="moba_attention",
    )(tq, tk, qat, ka, vat, kmean, bias)


def _mla_kernel(tq_ref, tk_ref, qt_ref, k_ref, vt_ref, o_ref, s0_ref, s1_ref, t0_ref, t1_ref,
                m_ref, acc_ref, *, nq, phases):
    def stage(offset, diagonal):
        def score(n, buf):
            qi, kj = tq_ref[offset + n], tk_ref[offset + n]
            for h in range(HEAD_GROUP):
                sl = slice(h * MLA_HEAD_PAD, (h + 1) * MLA_HEAD_PAD)
                s_t = jnp.dot(k_ref[_tile_rows(kj), sl], qt_ref[qi, sl, :],
                              preferred_element_type=F32)
                if diagonal:
                    key = lax.broadcasted_iota(jnp.int32, (ATTN_TILE, ATTN_TILE), 0)
                    qry = lax.broadcasted_iota(jnp.int32, (ATTN_TILE, ATTN_TILE), 1)
                    s_t = jnp.where(key <= qry, s_t, NEG)
                _store_scores(buf, h, s_t)

        def attend(n, buf):
            qi, kj = tq_ref[offset + n], tk_ref[offset + n]
            for h in range(HEAD_GROUP):
                _softmax_step(buf, h, vt_ref[kj, h * V_HEAD:(h + 1) * V_HEAD, :],
                              m_ref.at[qi, h], acc_ref.at[qi, h], diagonal)

        return score, attend

    bufs = ((s0_ref, t0_ref), (s1_ref, t1_ref))
    n_diag, n_past = phases
    _pipeline(n_diag, *stage(0, True), bufs)
    _pipeline(n_past, *stage(n_diag, False), bufs)
    _write_heads(acc_ref, o_ref, nq, V_HEAD)


def _mla(qmt, km, vbt):
    b, seq, _ = km.shape
    nq = seq // ATTN_TILE
    qw = HEAD_GROUP * MLA_HEAD_PAD
    vw = HEAD_GROUP * V_HEAD
    tq, tk, phases = _causal_items(nq, 1)
    per_group = lambda bi, g, *_: (bi, 0, g)
    tiles = lambda bi, g, *_: (bi, 0, g, 0)
    return pl.pallas_call(
        functools.partial(_mla_kernel, nq=nq, phases=phases),
        grid_spec=pltpu.PrefetchScalarGridSpec(
            num_scalar_prefetch=2,
            grid=(b, B_HEADS // HEAD_GROUP),
            in_specs=[
                pl.BlockSpec((None, nq, qw, ATTN_TILE), tiles),
                pl.BlockSpec((None, seq, qw), per_group),
                pl.BlockSpec((None, nq, vw, ATTN_TILE), tiles),
            ],
            out_specs=pl.BlockSpec((None, nq, vw, ATTN_TILE), tiles),
            scratch_shapes=[
                pltpu.VMEM((HEAD_GROUP, ATTN_TILE, ATTN_TILE), F32),
                pltpu.VMEM((HEAD_GROUP, ATTN_TILE, ATTN_TILE), F32),
                pltpu.VMEM((HEAD_GROUP, 1, ATTN_TILE), F32),
                pltpu.VMEM((HEAD_GROUP, 1, ATTN_TILE), F32),
                pltpu.VMEM((nq, HEAD_GROUP, 1, ATTN_TILE), F32),
                pltpu.VMEM((nq, HEAD_GROUP, V_HEAD + SUM_ROWS, ATTN_TILE), F32),
            ]),
        out_shape=jax.ShapeDtypeStruct((b, nq, B_WIDTH, ATTN_TILE), BF16),
        compiler_params=pltpu.CompilerParams(
            dimension_semantics=("arbitrary", "arbitrary"), vmem_limit_bytes=VMEM_LIMIT),
        name="mla_attention",
    )(tq, tk, qmt, km, vbt)


def _out_kernel(x_ref, oa_ref, ob_ref, gattn_ref, wg_ref, wpa_ref, wpb_ref, wout_ref,
                gmlp_ref, wup_ref, wdn_ref, gfin_ref, o_ref):
    x = x_ref[...]
    n = _rms(x, gattn_ref[...]).astype(BF16)
    gates = jnp.dot(n, wg_ref[...], preferred_element_type=F32)
    tn_dims = (((0,), (0,)), ((), ()))
    tiles = range(ROW_TILE // ATTN_TILE)
    pa = jnp.concatenate([lax.dot_general(oa_ref[t], wpa_ref[...], tn_dims,
                                          preferred_element_type=F32) for t in tiles], axis=0)
    pb = jnp.concatenate([lax.dot_general(ob_ref[t], wpb_ref[...], tn_dims,
                                          preferred_element_type=F32) for t in tiles], axis=0)
    merged = (jax.nn.sigmoid(gates[:, :D_MODEL]) * pa
              + jax.nn.sigmoid(gates[:, D_MODEL:]) * pb).astype(BF16)
    h = x + jnp.dot(merged, wout_ref[...], preferred_element_type=F32)
    m = _rms(h, gmlp_ref[...]).astype(BF16)
    for c in range(D_FF // FF_CHUNK):
        cols = slice(c * FF_CHUNK, (c + 1) * FF_CHUNK)
        up = jnp.dot(m, wup_ref[:, cols], preferred_element_type=F32)
        act = jnp.square(jnp.maximum(up, 0.0)).astype(BF16)
        h = h + jnp.dot(act, wdn_ref[cols, :], preferred_element_type=F32)
    o_ref[...] = _rms(h, gfin_ref[...])


def _out_block(x2, oa, ob, g_attn, wg, wpa, wpb, wout, g_mlp, wup, wdn, g_fin):
    n_rows = x2.shape[0]
    tm = ROW_TILE
    row = lambda i: (i, 0)
    return pl.pallas_call(
        _out_kernel,
        grid=(n_rows // tm,),
        in_specs=[
            pl.BlockSpec((tm, D_MODEL), row),
            pl.BlockSpec((tm // ATTN_TILE, A_WIDTH, ATTN_TILE), lambda i: (i, 0, 0)),
            pl.BlockSpec((tm // ATTN_TILE, B_WIDTH, ATTN_TILE), lambda i: (i, 0, 0)),
            _const_spec(g_attn.shape), _const_spec(wg.shape), _const_spec(wpa.shape),
            _const_spec(wpb.shape), _const_spec(wout.shape), _const_spec(g_mlp.shape),
            _const_spec(wup.shape), _const_spec(wdn.shape), _const_spec(g_fin.shape),
        ],
        out_specs=pl.BlockSpec((tm, D_MODEL), row),
        out_shape=jax.ShapeDtypeStruct((n_rows, D_MODEL), F32),
        compiler_params=pltpu.CompilerParams(
            dimension_semantics=("arbitrary",), vmem_limit_bytes=VMEM_LIMIT),
        name="out_mlp",
    )(x2, oa, ob, g_attn, wg, wpa, wpb, wout, g_mlp, wup, wdn, g_fin)


def _rope_tables(seq):
    half = QK_ROPE // 2
    inv_freq = ROPE_THETA ** (-jnp.arange(half, dtype=F32) / half)
    ang = jnp.arange(seq).astype(F32)[:, None] * inv_freq[None, :]
    cos2 = jnp.tile(jnp.cos(ang), (1, 2))
    sin2 = jnp.tile(jnp.sin(ang), (1, 2))
    pad = jnp.zeros((seq, MLA_HEAD_PAD - QK_NOPE - QK_ROPE), F32)
    cos_t = jnp.concatenate([jnp.ones((seq, QK_NOPE), F32), cos2, pad], axis=1)
    sin_t = jnp.concatenate([jnp.zeros((seq, QK_NOPE), F32), sin2, pad], axis=1)
    return cos_t, sin_t


def _rotate_half_cols(w):
    half = QK_ROPE // 2
    return jnp.concatenate([-w[..., half:], w[..., :half]], axis=-1)


def _pad_heads(w_nope, w_rope):
    k, h, _ = w_nope.shape
    if w_rope is None:
        w_rope = jnp.zeros((k, h, QK_ROPE), w_nope.dtype)
    pad = jnp.zeros((k, h, MLA_HEAD_PAD - QK_NOPE - QK_ROPE), w_nope.dtype)
    return jnp.concatenate([w_nope, w_rope, pad], axis=-1).reshape(k, h * MLA_HEAD_PAD)


def kernel(x, w_in, rel_bias, mla_q_norm, w_uq, mla_kv_norm, w_uk, w_uv, w_proj_a, w_proj_b,
           w_out, norm_attn, norm_mlp, w_mlp_up, w_mlp_down, norm_final):
    b, seq, d = x.shape
    assert d == D_MODEL and seq % ROW_TILE == 0 and seq // MOBA_BLOCK <= LANES
    assert w_in.shape[0] == 1, "single-layer block"
    x2 = x.reshape(b * seq, d)

    w = w_in[0]
    o_k = A_WIDTH
    o_v = 2 * A_WIDTH
    o_cq = 3 * A_WIDTH
    o_ckv = o_cq + Q_LORA
    o_kr = o_ckv + KV_LORA
    o_g = o_kr + QK_ROPE
    wk = w[:, o_k:o_v].astype(BF16)
    wqvt = jnp.concatenate([w[:, :o_k], w[:, o_v:o_cq]], axis=1).T.astype(BF16)
    w_kr = w[:, o_kr:o_g]
    lane_pad = lambda a: jnp.pad(a, ((0, 0), (QK_NOPE, MLA_HEAD_PAD - QK_NOPE - QK_ROPE)))
    wc = jnp.concatenate(
        [w[:, o_cq:o_kr], lane_pad(w_kr), lane_pad(_rotate_half_cols(w_kr))], axis=1).astype(BF16)
    wg = w[:, o_g:].astype(BF16)

    uq = w_uq[0].reshape(Q_LORA, B_HEADS, QK_NOPE + QK_ROPE)
    uq_nope, uq_rope = uq[..., :QK_NOPE], uq[..., QK_NOPE:]
    wuqt = _pad_heads(uq_nope, uq_rope).T.astype(BF16)
    wuk = _pad_heads(w_uk[0].reshape(KV_LORA, B_HEADS, QK_NOPE), None).astype(BF16)
    wuvt = w_uv[0].T.astype(BF16)

    cos_t, sin_t = _rope_tables(seq)
    q_scale = (QK_NOPE + QK_ROPE) ** -0.5 * LOG2E
    row2 = lambda a: a.reshape(1, -1)

    qat, ka, vat, kmean, qmt, km, vbt = _inproj(
        x2, row2(norm_attn[0]), wk, wqvt, wc, row2(mla_q_norm[0]), row2(mla_kv_norm[0]),
        wuqt, wuk, wuvt, (cos_t * q_scale).T, (sin_t * q_scale).T, cos_t, sin_t, b, seq)

    as_seq = lambda a: a.reshape(b, seq, a.shape[-1])
    bias = _bias_tiles(rel_bias)
    oa = _moba(qat, as_seq(ka), vat, kmean.reshape(b, seq // MOBA_BLOCK, A_WIDTH), bias)
    ob = _mla(qmt, as_seq(km), vbt)

    out = _out_block(
        x2, oa.reshape(-1, A_WIDTH, ATTN_TILE), ob.reshape(-1, B_WIDTH, ATTN_TILE),
        row2(norm_attn[0]),
        wg, w_proj_a[0].astype(BF16), w_proj_b[0].astype(BF16), w_out[0].astype(BF16),
        row2(norm_mlp[0]), w_mlp_up[0].astype(BF16), w_mlp_down[0].astype(BF16),
        row2(norm_final))
    return out.reshape(b, seq, d)
```

```python
import functools
import math

import jax
import jax.numpy as jnp
from jax import lax
from jax.experimental import pallas as pl
from jax.experimental.pallas import tpu as pltpu

D_MODEL = 1024
A_HEADS = 8
A_HEAD_DIM = 64
A_WIDTH = A_HEADS * A_HEAD_DIM
MOBA_BLOCK = 256
MOBA_TOPK = 3
REL_BUCKETS = 32
REL_MAX_DIST = 128
B_HEADS = 8
QK_NOPE = 64
QK_ROPE = 32
V_HEAD = 64
B_WIDTH = B_HEADS * V_HEAD
Q_LORA = 384
KV_LORA = 256
ROPE_THETA = 10000.0
D_FF = 4 * D_MODEL
EPS = 1e-6
NEG = -1e30
LOG2E = math.log2(math.e)

LANES = 128
MLA_HEAD_PAD = 128
ATTN_TILE = 256
HEAD_GROUP = 4
SUM_ROWS = 16
PIPELINE_UNROLL = 16
ROW_TILE = 512
FF_CHUNK = 1024
VMEM_LIMIT = 56 * 1024 * 1024

F32 = jnp.float32
BF16 = jnp.bfloat16
NT_DIMS = (((1,), (1,)), ((), ()))


def _rms(xf, g):
    y = xf * lax.rsqrt(jnp.mean(xf * xf, axis=-1, keepdims=True) + EPS)
    return y * g


def _const_spec(shape):
    zeros = (0,) * len(shape)
    return pl.BlockSpec(shape, lambda *_: zeros, pipeline_mode=pl.Buffered(1))


def _bias_kernel(rel_ref, o_ref):
    h = pl.program_id(0)
    shape = (MOBA_BLOCK, 2 * MOBA_BLOCK)
    r = lax.broadcasted_iota(jnp.int32, shape, 0)
    c = lax.broadcasted_iota(jnp.int32, shape, 1)
    d = c - r
    dist = jnp.maximum(d, 0)
    max_exact = REL_BUCKETS // 2
    df = jnp.maximum(dist, 1).astype(F32)
    large = max_exact + (jnp.log(df / max_exact) / math.log(REL_MAX_DIST / max_exact)
                         * (REL_BUCKETS - max_exact)).astype(jnp.int32)
    large = jnp.minimum(large, REL_BUCKETS - 1)
    bucket = jnp.where(dist < max_exact, dist, large)
    val = jnp.zeros(shape, F32)
    for b in range(REL_BUCKETS):
        val = jnp.where(bucket == b, rel_ref[b, h], val)
    o_ref[...] = jnp.where(d >= 0, (val - rel_ref[REL_BUCKETS - 1, h]) * LOG2E, NEG)


def _bias_tiles(rel_bias):
    return pl.pallas_call(
        _bias_kernel,
        grid=(A_HEADS,),
        in_specs=[pl.BlockSpec(memory_space=pltpu.SMEM)],
        out_specs=pl.BlockSpec((None, MOBA_BLOCK, 2 * MOBA_BLOCK), lambda h: (h, 0, 0)),
        out_shape=jax.ShapeDtypeStruct((A_HEADS, MOBA_BLOCK, 2 * MOBA_BLOCK), F32),
        name="moba_bias_tiles",
    )(rel_bias)


def _inproj_kernel(x_ref, g_ref, wk_ref, wqvt_ref, wc_ref, qn_ref, kvn_ref, wuqt_ref,
                   wuk_ref, wuvt_ref, cqt_ref, sqt_ref, ck_ref, sk_ref,
                   qat_ref, ka_ref, vat_ref, kmean_ref, qmt_ref, km_ref, vbt_ref):
    n = _rms(x_ref[...], g_ref[...]).astype(BF16)
    k = jnp.dot(n, wk_ref[...], preferred_element_type=F32)
    ka_ref[...] = k.astype(BF16)
    nblk = ROW_TILE // ATTN_TILE
    kmean_ref[0] = jnp.sum(k.reshape(nblk, MOBA_BLOCK, A_WIDTH), axis=1) * (1.0 / MOBA_BLOCK)

    c = jnp.dot(n, wc_ref[...], preferred_element_type=F32)
    cq = _rms(c[:, :Q_LORA], qn_ref[...]).astype(BF16)
    ckv = _rms(c[:, Q_LORA:Q_LORA + KV_LORA], kvn_ref[...]).astype(BF16)
    kr = c[:, Q_LORA + KV_LORA:Q_LORA + KV_LORA + LANES]
    krr = c[:, Q_LORA + KV_LORA + LANES:]
    for t in range(nblk):
        rows = slice(t * ATTN_TILE, (t + 1) * ATTN_TILE)
        qv_t = lax.dot_general(wqvt_ref[...], n[rows], NT_DIMS,
                               preferred_element_type=F32)
        qat_ref[t] = (qv_t[:A_WIDTH] * (A_HEAD_DIM ** -0.5 * LOG2E)).astype(BF16)
        vat_ref[t] = qv_t[A_WIDTH:].astype(BF16)
        vbt_ref[t] = lax.dot_general(wuvt_ref[...], ckv[rows], NT_DIMS,
                                     preferred_element_type=F32).astype(BF16)
        q_t = lax.dot_general(wuqt_ref[...], cq[rows], NT_DIMS,
                              preferred_element_type=F32)
        cq_t, sq_t = cqt_ref[:, rows], sqt_ref[:, rows]
        half = QK_ROPE // 2
        for h in range(B_HEADS):
            q_h = q_t[h * MLA_HEAD_PAD:(h + 1) * MLA_HEAD_PAD]
            x1 = q_h[QK_NOPE:QK_NOPE + half]
            x2 = q_h[QK_NOPE + half:QK_NOPE + QK_ROPE]
            rot_h = jnp.concatenate(
                [jnp.zeros((QK_NOPE, ATTN_TILE), F32), -x2, x1,
                 jnp.zeros((MLA_HEAD_PAD - QK_NOPE - QK_ROPE, ATTN_TILE), F32)], axis=0)
            qmt_ref[t, h * MLA_HEAD_PAD:(h + 1) * MLA_HEAD_PAD, :] = (
                q_h * cq_t + rot_h * sq_t).astype(BF16)
    kn = jnp.dot(ckv, wuk_ref[...], preferred_element_type=F32)
    k_rope = kr * ck_ref[...] + krr * sk_ref[...]
    for h in range(B_HEADS):
        sl = slice(h * MLA_HEAD_PAD, (h + 1) * MLA_HEAD_PAD)
        km_ref[:, sl] = (kn[:, sl] + k_rope).astype(BF16)


def _inproj(x2, g_attn, wk, wqvt, wc, qn, kvn, wuqt, wuk, wuvt, cq_tt, sq_tt, ck_t, sk_t,
            batch, seq):
    n_rows = x2.shape[0]
    tm = ROW_TILE
    steps = n_rows // tm
    per_seq = seq // tm
    nblk = tm // ATTN_TILE
    row = lambda i: (i, 0)
    tab = lambda i: (i % per_seq, 0)
    tab_t = lambda i: (0, i % per_seq)
    tile_map = lambda i: (i // per_seq, i % per_seq, 0, 0)
    bf = lambda w: jax.ShapeDtypeStruct((n_rows, w), BF16)
    tiles = lambda w: jax.ShapeDtypeStruct((batch, seq // ATTN_TILE, w, ATTN_TILE), BF16)
    tile_spec = lambda w: pl.BlockSpec((None, nblk, w, ATTN_TILE), tile_map)
    return pl.pallas_call(
        _inproj_kernel,
        grid=(steps,),
        in_specs=[
            pl.BlockSpec((tm, D_MODEL), row),
            _const_spec(g_attn.shape), _const_spec(wk.shape), _const_spec(wqvt.shape),
            _const_spec(wc.shape), _const_spec(qn.shape), _const_spec(kvn.shape),
            _const_spec(wuqt.shape), _const_spec(wuk.shape),
            _const_spec(wuvt.shape),
            pl.BlockSpec((LANES, tm), tab_t), pl.BlockSpec((LANES, tm), tab_t),
            pl.BlockSpec((tm, LANES), tab), pl.BlockSpec((tm, LANES), tab),
        ],
        out_specs=[
            tile_spec(A_WIDTH), pl.BlockSpec((tm, A_WIDTH), row), tile_spec(A_WIDTH),
            pl.BlockSpec((1, nblk, A_WIDTH), lambda i: (i, 0, 0)),
            tile_spec(B_HEADS * MLA_HEAD_PAD),
            pl.BlockSpec((tm, B_HEADS * MLA_HEAD_PAD), row),
            tile_spec(B_WIDTH),
        ],
        out_shape=[
            tiles(A_WIDTH), bf(A_WIDTH), tiles(A_WIDTH),
            jax.ShapeDtypeStruct((steps, nblk, A_WIDTH), F32),
            tiles(B_HEADS * MLA_HEAD_PAD), bf(B_HEADS * MLA_HEAD_PAD), tiles(B_WIDTH),
        ],
        compiler_params=pltpu.CompilerParams(
            dimension_semantics=("arbitrary",), vmem_limit_bytes=VMEM_LIMIT),
        name="in_proj",
    )(x2, g_attn, wk, wqvt, wc, qn, kvn, wuqt, wuk, wuvt, cq_tt, sq_tt, ck_t, sk_t)


def _sum_rows():
    r = lax.broadcasted_iota(jnp.int32, (SUM_ROWS, ATTN_TILE), 0)
    return (r == 0).astype(BF16)


def _store_scores(buf, h, s_t):
    s_ref, max_ref = buf
    s_ref[h] = s_t
    max_ref[h] = jnp.max(s_t, axis=0, keepdims=True)


def _softmax_step(buf, h, v_t, m_ref, acc_ref, first, col_keep=None):
    s_ref, max_ref = buf[0].at[h], buf[1].at[h]
    s_max = max_ref[...]
    if col_keep is not None:
        s_max = jnp.where(col_keep > 0, s_max, NEG)
    if first:
        m_new = s_max
    else:
        m_old = m_ref[...]
        m_new = jnp.maximum(m_old, s_max)
    v_aug = jnp.concatenate([v_t, _sum_rows()], axis=0)
    m_exp = m_new if col_keep is None else jnp.maximum(m_new, max_ref[...])
    p_t = jnp.exp2(s_ref[...] - m_exp).astype(BF16)
    pv = jnp.dot(v_aug, p_t, preferred_element_type=F32)
    if col_keep is not None:
        pv = pv * col_keep
    if first:
        acc_ref[...] = pv
    else:
        acc_ref[...] = jnp.exp2(m_old - m_new) * acc_ref[...] + pv
    m_ref[...] = m_new


def _pipeline(n_items, score, attend, bufs):
    score(0, bufs[0])
    n_loops = (n_items - 1) // PIPELINE_UNROLL

    def body(p, carry):
        for t in range(PIPELINE_UNROLL):
            n = PIPELINE_UNROLL * p + t
            score(n + 1, bufs[(t + 1) % 2])
            attend(n, bufs[t % 2])
        return carry

    lax.fori_loop(0, n_loops, body, 0)
    done = PIPELINE_UNROLL * n_loops
    for n in range(done, n_items):
        if n + 1 < n_items:
            score(n + 1, bufs[(n + 1 - done) % 2])
        attend(n, bufs[(n - done) % 2])


def _tile_rows(j):
    return pl.ds(pl.multiple_of(j * ATTN_TILE, ATTN_TILE), ATTN_TILE)


def _write_heads(acc_ref, o_ref, nq, head_dim):
    def one(qi, carry):
        for h in range(HEAD_GROUP):
            acc = acc_ref[qi, h]
            o_ref[qi, h * head_dim:(h + 1) * head_dim, :] = (
                acc[:head_dim] / acc[head_dim:head_dim + 1]).astype(o_ref.dtype)
        return carry

    lax.fori_loop(0, nq, one, 0)


def _causal_items(nq, first_past):
    phases = [[(i, i - d) for i in range(d, nq)] for d in range(first_past)]
    phases.append([(i, j) for i in range(first_past, nq) for j in range(i - first_past + 1)])
    flat = [it for ph in phases for it in ph]
    tq = jnp.asarray([it[0] for it in flat], jnp.int32)
    tk = jnp.asarray([it[1] for it in flat], jnp.int32)
    return tq, tk, [len(ph) for ph in phases]


def _moba_kernel(tq_ref, tk_ref, qt_ref, k_ref, vt_ref, kmean_ref, bias_ref, o_ref,
                 keep_ref, s0_ref, s1_ref, t0_ref, t1_ref, m_ref, acc_ref, *, seq, phases):
    nblk = seq // MOBA_BLOCK

    def head_query(qi, h):
        pr, sub = divmod(h, 2)
        q_h = qt_ref[qi, pl.ds(pr * LANES + sub * A_HEAD_DIM, A_HEAD_DIM), :]
        zeros = jnp.zeros_like(q_h)
        return jnp.concatenate([q_h, zeros] if sub == 0 else [zeros, q_h], axis=0)

    def select_blocks(i):
        row = lax.broadcasted_iota(jnp.int32, (nblk, ATTN_TILE), 0)
        rowf = row.astype(F32)
        past = row < i
        for h in range(HEAD_GROUP):
            kmean = kmean_ref[:, (h // 2) * LANES:(h // 2 + 1) * LANES].astype(BF16)
            gate = jnp.dot(kmean, head_query(i, h), preferred_element_type=F32)
            gate = jnp.where(past, gate, -jnp.inf)
            picked = jnp.zeros((nblk, ATTN_TILE), F32)
            for _ in range(MOBA_TOPK):
                top = jnp.max(gate, axis=0, keepdims=True)
                first = jnp.min(jnp.where(gate == top, rowf, float(nblk)), axis=0, keepdims=True)
                pick = rowf == first
                picked = jnp.where(pick, 1.0, picked)
                gate = jnp.where(pick, -jnp.inf, gate)
            keep_ref[i, h] = jnp.where(past, picked, 0.0)

    def stage(offset, bias_cols, first):
        def score(n, buf):
            qi, kj = tq_ref[offset + n], tk_ref[offset + n]
            for h in range(HEAD_GROUP):
                k_t = k_ref[_tile_rows(kj), (h // 2) * LANES:(h // 2 + 1) * LANES]
                s_t = jnp.dot(k_t, head_query(qi, h), preferred_element_type=F32)
                if bias_cols is not None:
                    s_t = s_t + bias_ref[h, :, bias_cols]
                _store_scores(buf, h, s_t)

        def attend(n, buf):
            qi, kj = tq_ref[offset + n], tk_ref[offset + n]
            for h in range(HEAD_GROUP):
                col_keep = None if first else keep_ref[qi, h, pl.ds(kj, 1), :]
                _softmax_step(buf, h, vt_ref[kj, h * A_HEAD_DIM:(h + 1) * A_HEAD_DIM, :],
                              m_ref.at[qi, h], acc_ref.at[qi, h], first, col_keep)
            if first:
                select_blocks(qi)

        return score, attend

    bufs = ((s0_ref, t0_ref), (s1_ref, t1_ref))
    n_own, n_prev, n_far = phases
    _pipeline(n_own, *stage(0, slice(0, MOBA_BLOCK), True), bufs)
    _pipeline(n_prev, *stage(n_own, slice(MOBA_BLOCK, 2 * MOBA_BLOCK), False), bufs)
    _pipeline(n_far, *stage(n_own + n_prev, None, False), bufs)
    _write_heads(acc_ref, o_ref, nblk, A_HEAD_DIM)


def _moba(qat, ka, vat, kmean, bias):
    b, seq, _ = ka.shape
    nq = seq // ATTN_TILE
    gw = HEAD_GROUP * A_HEAD_DIM
    tq, tk, phases = _causal_items(nq, 2)
    per_group = lambda bi, g, *_: (bi, 0, g)
    tiles = lambda bi, g, *_: (bi, 0, g, 0)
    return pl.pallas_call(
        functools.partial(_moba_kernel, seq=seq, phases=phases),
        grid_spec=pltpu.PrefetchScalarGridSpec(
            num_scalar_prefetch=2,
            grid=(b, A_HEADS // HEAD_GROUP),
            in_specs=[
                pl.BlockSpec((None, nq, gw, ATTN_TILE), tiles),
                pl.BlockSpec((None, seq, gw), per_group),
                pl.BlockSpec((None, nq, gw, MOBA_BLOCK), tiles),
                pl.BlockSpec((None, nq, gw), per_group),
                pl.BlockSpec((HEAD_GROUP, MOBA_BLOCK, 2 * MOBA_BLOCK), lambda bi, g, *_: (g, 0, 0)),
            ],
            out_specs=pl.BlockSpec((None, nq, gw, ATTN_TILE), tiles),
            scratch_shapes=[
                pltpu.VMEM((nq, HEAD_GROUP, nq, ATTN_TILE), F32),
                pltpu.VMEM((HEAD_GROUP, ATTN_TILE, ATTN_TILE), F32),
                pltpu.VMEM((HEAD_GROUP, ATTN_TILE, ATTN_TILE), F32),
                pltpu.VMEM((HEAD_GROUP, 1, ATTN_TILE), F32),
                pltpu.VMEM((HEAD_GROUP, 1, ATTN_TILE), F32),
                pltpu.VMEM((nq, HEAD_GROUP, 1, ATTN_TILE), F32),
                pltpu.VMEM((nq, HEAD_GROUP, A_HEAD_DIM + SUM_ROWS, ATTN_TILE), F32),
            ]),
        out_shape=jax.ShapeDtypeStruct((b, nq, A_WIDTH, ATTN_TILE), BF16),
        compiler_params=pltpu.CompilerParams(
            dimension_semantics=("arbitrary", "arbitrary"), vmem_limit_bytes=VMEM_LIMIT),
        name="moba_attention",
    )(tq, tk, qat, ka, vat, kmean, bias)


def _mla_kernel(tq_ref, tk_ref, qt_ref, k_ref, vt_ref, o_ref, s0_ref, s1_ref, t0_ref, t1_ref,
                m_ref, acc_ref, *, nq, phases):
    def stage(offset, diagonal):
        def score(n, buf):
            qi, kj = tq_ref[offset + n], tk_ref[offset + n]
            for h in range(HEAD_GROUP):
                sl = slice(h * MLA_HEAD_PAD, (h + 1) * MLA_HEAD_PAD)
                s_t = jnp.dot(k_ref[_tile_rows(kj), sl], qt_ref[qi, sl, :],
                              preferred_element_type=F32)
                if diagonal:
                    key = lax.broadcasted_iota(jnp.int32, (ATTN_TILE, ATTN_TILE), 0)
                    qry = lax.broadcasted_iota(jnp.int32, (ATTN_TILE, ATTN_TILE), 1)
                    s_t = jnp.where(key <= qry, s_t, NEG)
                _store_scores(buf, h, s_t)

        def attend(n, buf):
            qi, kj = tq_ref[offset + n], tk_ref[offset + n]
            for h in range(HEAD_GROUP):
                _softmax_step(buf, h, vt_ref[kj, h * V_HEAD:(h + 1) * V_HEAD, :],
                              m_ref.at[qi, h], acc_ref.at[qi, h], diagonal)

        return score, attend

    bufs = ((s0_ref, t0_ref), (s1_ref, t1_ref))
    n_diag, n_past = phases
    _pipeline(n_diag, *stage(0, True), bufs)
    _pipeline(n_past, *stage(n_diag, False), bufs)
    _write_heads(acc_ref, o_ref, nq, V_HEAD)


def _mla(qmt, km, vbt):
    b, seq, _ = km.shape
    nq = seq // ATTN_TILE
    qw = HEAD_GROUP * MLA_HEAD_PAD
    vw = HEAD_GROUP * V_HEAD
    tq, tk, phases = _causal_items(nq, 1)
    per_group = lambda bi, g, *_: (bi, 0, g)
    tiles = lambda bi, g, *_: (bi, 0, g, 0)
    return pl.pallas_call(
        functools.partial(_mla_kernel, nq=nq, phases=phases),
        grid_spec=pltpu.PrefetchScalarGridSpec(
            num_scalar_prefetch=2,
            grid=(b, B_HEADS // HEAD_GROUP),
            in_specs=[
                pl.BlockSpec((None, nq, qw, ATTN_TILE), tiles),
                pl.BlockSpec((None, seq, qw), per_group),
                pl.BlockSpec((None, nq, vw, ATTN_TILE), tiles),
            ],
            out_specs=pl.BlockSpec((None, nq, vw, ATTN_TILE), tiles),
            scratch_shapes=[
                pltpu.VMEM((HEAD_GROUP, ATTN_TILE, ATTN_TILE), F32),
                pltpu.VMEM((HEAD_GROUP, ATTN_TILE, ATTN_TILE), F32),
                pltpu.VMEM((HEAD_GROUP, 1, ATTN_TILE), F32),
                pltpu.VMEM((HEAD_GROUP, 1, ATTN_TILE), F32),
                pltpu.VMEM((nq, HEAD_GROUP, 1, ATTN_TILE), F32),
                pltpu.VMEM((nq, HEAD_GROUP, V_HEAD + SUM_ROWS, ATTN_TILE), F32),
            ]),
        out_shape=jax.ShapeDtypeStruct((b, nq, B_WIDTH, ATTN_TILE), BF16),
        compiler_params=pltpu.CompilerParams(
            dimension_semantics=("arbitrary", "arbitrary"), vmem_limit_bytes=VMEM_LIMIT),
        name="mla_attention",
    )(tq, tk, qmt, km, vbt)


def _out_kernel(x_ref, oa_ref, ob_ref, gattn_ref, wg_ref, wpa_ref, wpb_ref, wout_ref,
                gmlp_ref, wup_ref, wdn_ref, gfin_ref, o_ref):
    x = x_ref[...]
    n = _rms(x, gattn_ref[...]).astype(BF16)
    gates = jnp.dot(n, wg_ref[...], preferred_element_type=F32)
    tn_dims = (((0,), (0,)), ((), ()))
    tiles = range(ROW_TILE // ATTN_TILE)
    pa = jnp.concatenate([lax.dot_general(oa_ref[t], wpa_ref[...], tn_dims,
                                          preferred_element_type=F32) for t in tiles], axis=0)
    pb = jnp.concatenate([lax.dot_general(ob_ref[t], wpb_ref[...], tn_dims,
                                          preferred_element_type=F32) for t in tiles], axis=0)
    merged = (jax.nn.sigmoid(gates[:, :D_MODEL]) * pa
              + jax.nn.sigmoid(gates[:, D_MODEL:]) * pb).astype(BF16)
    h = x + jnp.dot(merged, wout_ref[...], preferred_element_type=F32)
    m = _rms(h, gmlp_ref[...]).astype(BF16)
    for c in range(D_FF // FF_CHUNK):
        cols = slice(c * FF_CHUNK, (c + 1) * FF_CHUNK)
        up = jnp.dot(m, wup_ref[:, cols], preferred_element_type=F32)
        act = jnp.square(jnp.maximum(up, 0.0)).astype(BF16)
        h = h + jnp.dot(act, wdn_ref[cols, :], preferred_element_type=F32)
    o_ref[...] = _rms(h, gfin_ref[...])


def _out_block(x2, oa, ob, g_attn, wg, wpa, wpb, wout, g_mlp, wup, wdn, g_fin):
    n_rows = x2.shape[0]
    tm = ROW_TILE
    row = lambda i: (i, 0)
    return pl.pallas_call(
        _out_kernel,
        grid=(n_rows // tm,),
        in_specs=[
            pl.BlockSpec((tm, D_MODEL), row),
            pl.BlockSpec((tm // ATTN_TILE, A_WIDTH, ATTN_TILE), lambda i: (i, 0, 0)),
            pl.BlockSpec((tm // ATTN_TILE, B_WIDTH, ATTN_TILE), lambda i: (i, 0, 0)),
            _const_spec(g_attn.shape), _const_spec(wg.shape), _const_spec(wpa.shape),
            _const_spec(wpb.shape), _const_spec(wout.shape), _const_spec(g_mlp.shape),
            _const_spec(wup.shape), _const_spec(wdn.shape), _const_spec(g_fin.shape),
        ],
        out_specs=pl.BlockSpec((tm, D_MODEL), row),
        out_shape=jax.ShapeDtypeStruct((n_rows, D_MODEL), F32),
        compiler_params=pltpu.CompilerParams(
            dimension_semantics=("arbitrary",), vmem_limit_bytes=VMEM_LIMIT),
        name="out_mlp",
    )(x2, oa, ob, g_attn, wg, wpa, wpb, wout, g_mlp, wup, wdn, g_fin)


def _rope_tables(seq):
    half = QK_ROPE // 2
    inv_freq = ROPE_THETA ** (-jnp.arange(half, dtype=F32) / half)
    ang = jnp.arange(seq).astype(F32)[:, None] * inv_freq[None, :]
    cos2 = jnp.tile(jnp.cos(ang), (1, 2))
    sin2 = jnp.tile(jnp.sin(ang), (1, 2))
    pad = jnp.zeros((seq, MLA_HEAD_PAD - QK_NOPE - QK_ROPE), F32)
    cos_t = jnp.concatenate([jnp.ones((seq, QK_NOPE), F32), cos2, pad], axis=1)
    sin_t = jnp.concatenate([jnp.zeros((seq, QK_NOPE), F32), sin2, pad], axis=1)
    return cos_t, sin_t


def _rotate_half_cols(w):
    half = QK_ROPE // 2
    return jnp.concatenate([-w[..., half:], w[..., :half]], axis=-1)


def _pad_heads(w_nope, w_rope):
    k, h, _ = w_nope.shape
    if w_rope is None:
        w_rope = jnp.zeros((k, h, QK_ROPE), w_nope.dtype)
    pad = jnp.zeros((k, h, MLA_HEAD_PAD - QK_NOPE - QK_ROPE), w_nope.dtype)
    return jnp.concatenate([w_nope, w_rope, pad], axis=-1).reshape(k, h * MLA_HEAD_PAD)


def kernel(x, w_in, rel_bias, mla_q_norm, w_uq, mla_kv_norm, w_uk, w_uv, w_proj_a, w_proj_b,
           w_out, norm_attn, norm_mlp, w_mlp_up, w_mlp_down, norm_final):
    b, seq, d = x.shape
    assert d == D_MODEL and seq % ROW_TILE == 0 and seq // MOBA_BLOCK <= LANES
    assert w_in.shape[0] == 1, "single-layer block"
    x2 = x.reshape(b * seq, d)

    w = w_in[0]
    o_k = A_WIDTH
    o_v = 2 * A_WIDTH
    o_cq = 3 * A_WIDTH
    o_ckv = o_cq + Q_LORA
    o_kr = o_ckv + KV_LORA
    o_g = o_kr + QK_ROPE
    wk = w[:, o_k:o_v].astype(BF16)
    wqvt = jnp.concatenate([w[:, :o_k], w[:, o_v:o_cq]], axis=1).T.astype(BF16)
    w_kr = w[:, o_kr:o_g]
    lane_pad = lambda a: jnp.pad(a, ((0, 0), (QK_NOPE, MLA_HEAD_PAD - QK_NOPE - QK_ROPE)))
    wc = jnp.concatenate(
        [w[:, o_cq:o_kr], lane_pad(w_kr), lane_pad(_rotate_half_cols(w_kr))], axis=1).astype(BF16)
    wg = w[:, o_g:].astype(BF16)

    uq = w_uq[0].reshape(Q_LORA, B_HEADS, QK_NOPE + QK_ROPE)
    uq_nope, uq_rope = uq[..., :QK_NOPE], uq[..., QK_NOPE:]
    wuqt = _pad_heads(uq_nope, uq_rope).T.astype(BF16)
    wuk = _pad_heads(w_uk[0].reshape(KV_LORA, B_HEADS, QK_NOPE), None).astype(BF16)
    wuvt = w_uv[0].T.astype(BF16)

    cos_t, sin_t = _rope_tables(seq)
    q_scale = (QK_NOPE + QK_ROPE) ** -0.5 * LOG2E
    row2 = lambda a: a.reshape(1, -1)

    qat, ka, vat, kmean, qmt, km, vbt = _inproj(
        x2, row2(norm_attn[0]), wk, wqvt, wc, row2(mla_q_norm[0]), row2(mla_kv_norm[0]),
        wuqt, wuk, wuvt, (cos_t * q_scale).T, (sin_t * q_scale).T, cos_t, sin_t, b, seq)

    as_seq = lambda a: a.reshape(b, seq, a.shape[-1])
    bias = _bias_tiles(rel_bias)
    oa = _moba(qat, as_seq(ka), vat, kmean.reshape(b, seq // MOBA_BLOCK, A_WIDTH), bias)
    ob = _mla(qmt, as_seq(km), vbt)

    out = _out_block(
        x2, oa.reshape(-1, A_WIDTH, ATTN_TILE), ob.reshape(-1, B_WIDTH, ATTN_TILE),
        row2(norm_attn[0]),
        wg, w_proj_a[0].astype(BF16), w_proj_b[0].astype(BF16), w_out[0].astype(BF16),
        row2(norm_mlp[0]), w_mlp_up[0].astype(BF16), w_mlp_down[0].astype(BF16),
        row2(norm_final))
    return out.reshape(b, seq, d)
```

```python
import functools
import math

import jax
import jax.numpy as jnp
from jax import lax
from jax.experimental import pallas as pl
from jax.experimental.pallas import tpu as pltpu

D_MODEL = 1024
A_HEADS = 8
A_HEAD_DIM = 64
A_WIDTH = A_HEADS * A_HEAD_DIM
MOBA_BLOCK = 256
MOBA_TOPK = 3
REL_BUCKETS = 32
REL_MAX_DIST = 128
B_HEADS = 8
QK_NOPE = 64
QK_ROPE = 32
V_HEAD = 64
B_WIDTH = B_HEADS * V_HEAD
Q_LORA = 384
KV_LORA = 256
ROPE_THETA = 10000.0
D_FF = 4 * D_MODEL
EPS = 1e-6
NEG = -1e30
LOG2E = math.log2(math.e)

LANES = 128
MLA_HEAD_PAD = 128
ATTN_TILE = 256
HEAD_GROUP = 4
SUM_ROWS = 16
ITEM_SPAN = 2
PIPELINE_UNROLL = 8
ROW_TILE = 512
FF_CHUNK = 1024
VMEM_LIMIT = 56 * 1024 * 1024

F32 = jnp.float32
BF16 = jnp.bfloat16
NT_DIMS = (((1,), (1,)), ((), ()))


def _rms(xf, g):
    y = xf * lax.rsqrt(jnp.mean(xf * xf, axis=-1, keepdims=True) + EPS)
    return y * g


def _const_spec(shape):
    zeros = (0,) * len(shape)
    return pl.BlockSpec(shape, lambda *_: zeros, pipeline_mode=pl.Buffered(1))


def _bias_kernel(rel_ref, o_ref):
    h = pl.program_id(0)
    shape = (MOBA_BLOCK, 2 * MOBA_BLOCK)
    r = lax.broadcasted_iota(jnp.int32, shape, 0)
    c = lax.broadcasted_iota(jnp.int32, shape, 1)
    d = c - r
    dist = jnp.maximum(d, 0)
    max_exact = REL_BUCKETS // 2
    df = jnp.maximum(dist, 1).astype(F32)
    large = max_exact + (jnp.log(df / max_exact) / math.log(REL_MAX_DIST / max_exact)
                         * (REL_BUCKETS - max_exact)).astype(jnp.int32)
    large = jnp.minimum(large, REL_BUCKETS - 1)
    bucket = jnp.where(dist < max_exact, dist, large)
    val = jnp.zeros(shape, F32)
    for b in range(REL_BUCKETS):
        val = jnp.where(bucket == b, rel_ref[b, h], val)
    o_ref[...] = jnp.where(d >= 0, (val - rel_ref[REL_BUCKETS - 1, h]) * LOG2E, NEG)


def _bias_tiles(rel_bias):
    return pl.pallas_call(
        _bias_kernel,
        grid=(A_HEADS,),
        in_specs=[pl.BlockSpec(memory_space=pltpu.SMEM)],
        out_specs=pl.BlockSpec((None, MOBA_BLOCK, 2 * MOBA_BLOCK), lambda h: (h, 0, 0)),
        out_shape=jax.ShapeDtypeStruct((A_HEADS, MOBA_BLOCK, 2 * MOBA_BLOCK), F32),
        name="moba_bias_tiles",
    )(rel_bias)


def _inproj_kernel(x_ref, g_ref, wk_ref, wqvt_ref, wc_ref, qn_ref, kvn_ref, wuqt_ref,
                   wuk_ref, wuvt_ref, cqt_ref, sqt_ref, ck_ref, sk_ref,
                   qat_ref, ka_ref, vat_ref, kmean_ref, qmt_ref, km_ref, vbt_ref):
    n = _rms(x_ref[...], g_ref[...]).astype(BF16)
    k = jnp.dot(n, wk_ref[...], preferred_element_type=F32)
    ka_ref[...] = k.astype(BF16)
    nblk = ROW_TILE // ATTN_TILE
    kmean_ref[0] = jnp.sum(k.reshape(nblk, MOBA_BLOCK, A_WIDTH), axis=1) * (1.0 / MOBA_BLOCK)

    c = jnp.dot(n, wc_ref[...], preferred_element_type=F32)
    cq = _rms(c[:, :Q_LORA], qn_ref[...]).astype(BF16)
    ckv = _rms(c[:, Q_LORA:Q_LORA + KV_LORA], kvn_ref[...]).astype(BF16)
    kr = c[:, Q_LORA + KV_LORA:Q_LORA + KV_LORA + LANES]
    krr = c[:, Q_LORA + KV_LORA + LANES:]
    for t in range(nblk):
        rows = slice(t * ATTN_TILE, (t + 1) * ATTN_TILE)
        qv_t = lax.dot_general(wqvt_ref[...], n[rows], NT_DIMS,
                               preferred_element_type=F32)
        qat_ref[t] = (qv_t[:A_WIDTH] * (A_HEAD_DIM ** -0.5 * LOG2E)).astype(BF16)
        vat_ref[t] = qv_t[A_WIDTH:].astype(BF16)
        vbt_ref[t] = lax.dot_general(wuvt_ref[...], ckv[rows], NT_DIMS,
                                     preferred_element_type=F32).astype(BF16)
        q_t = lax.dot_general(wuqt_ref[...], cq[rows], NT_DIMS,
                              preferred_element_type=F32)
        cq_t, sq_t = cqt_ref[:, rows], sqt_ref[:, rows]
        half = QK_ROPE // 2
        for h in range(B_HEADS):
            q_h = q_t[h * MLA_HEAD_PAD:(h + 1) * MLA_HEAD_PAD]
            x1 = q_h[QK_NOPE:QK_NOPE + half]
            x2 = q_h[QK_NOPE + half:QK_NOPE + QK_ROPE]
            rot_h = jnp.concatenate(
                [jnp.zeros((QK_NOPE, ATTN_TILE), F32), -x2, x1,
                 jnp.zeros((MLA_HEAD_PAD - QK_NOPE - QK_ROPE, ATTN_TILE), F32)], axis=0)
            qmt_ref[t, h * MLA_HEAD_PAD:(h + 1) * MLA_HEAD_PAD, :] = (
                q_h * cq_t + rot_h * sq_t).astype(BF16)
    kn = jnp.dot(ckv, wuk_ref[...], preferred_element_type=F32)
    k_rope = kr * ck_ref[...] + krr * sk_ref[...]
    for h in range(B_HEADS):
        sl = slice(h * MLA_HEAD_PAD, (h + 1) * MLA_HEAD_PAD)
        km_ref[:, sl] = (kn[:, sl] + k_rope).astype(BF16)


def _inproj(x2, g_attn, wk, wqvt, wc, qn, kvn, wuqt, wuk, wuvt, cq_tt, sq_tt, ck_t, sk_t,
            batch, seq):
    n_rows = x2.shape[0]
    tm = ROW_TILE
    steps = n_rows // tm
    per_seq = seq // tm
    nblk = tm // ATTN_TILE
    row = lambda i: (i, 0)
    tab = lambda i: (i % per_seq, 0)
    tab_t = lambda i: (0, i % per_seq)
    tile_map = lambda i: (i // per_seq, i % per_seq, 0, 0)
    bf = lambda w: jax.ShapeDtypeStruct((n_rows, w), BF16)
    tiles = lambda w: jax.ShapeDtypeStruct((batch, seq // ATTN_TILE, w, ATTN_TILE), BF16)
    tile_spec = lambda w: pl.BlockSpec((None, nblk, w, ATTN_TILE), tile_map)
    return pl.pallas_call(
        _inproj_kernel,
        grid=(steps,),
        in_specs=[
            pl.BlockSpec((tm, D_MODEL), row),
            _const_spec(g_attn.shape), _const_spec(wk.shape), _const_spec(wqvt.shape),
            _const_spec(wc.shape), _const_spec(qn.shape), _const_spec(kvn.shape),
            _const_spec(wuqt.shape), _const_spec(wuk.shape),
            _const_spec(wuvt.shape),
            pl.BlockSpec((LANES, tm), tab_t), pl.BlockSpec((LANES, tm), tab_t),
            pl.BlockSpec((tm, LANES), tab), pl.BlockSpec((tm, LANES), tab),
        ],
        out_specs=[
            tile_spec(A_WIDTH), pl.BlockSpec((tm, A_WIDTH), row), tile_spec(A_WIDTH),
            pl.BlockSpec((1, nblk, A_WIDTH), lambda i: (i, 0, 0)),
            tile_spec(B_HEADS * MLA_HEAD_PAD),
            pl.BlockSpec((tm, B_HEADS * MLA_HEAD_PAD), row),
            tile_spec(B_WIDTH),
        ],
        out_shape=[
            tiles(A_WIDTH), bf(A_WIDTH), tiles(A_WIDTH),
            jax.ShapeDtypeStruct((steps, nblk, A_WIDTH), F32),
            tiles(B_HEADS * MLA_HEAD_PAD), bf(B_HEADS * MLA_HEAD_PAD), tiles(B_WIDTH),
        ],
        compiler_params=pltpu.CompilerParams(
            dimension_semantics=("arbitrary",), vmem_limit_bytes=VMEM_LIMIT),
        name="in_proj",
    )(x2, g_attn, wk, wqvt, wc, qn, kvn, wuqt, wuk, wuvt, cq_tt, sq_tt, ck_t, sk_t)


def _store_scores(buf, h, parts):
    s_ref, max_ref = buf
    row = 0
    for idx, s_t in enumerate(parts):
        s_ref[h, row:row + s_t.shape[0]] = s_t
        max_ref[h, idx] = jnp.max(s_t, axis=0, keepdims=True)
        row += s_t.shape[0]


def _softmax_step(buf, h, parts, m_ref, acc_ref, first):
    s_ref, max_ref = buf
    maxes = [max_ref[h, idx] for idx in range(len(parts))]
    m_new = None if first else m_ref[...]
    for s_max, (_, _, col_keep) in zip(maxes, parts):
        if col_keep is not None:
            s_max = jnp.where(col_keep > 0, s_max, NEG)
        m_new = s_max if m_new is None else jnp.maximum(m_new, s_max)
    pv = None
    row = 0
    for s_max, (rows, v_t, col_keep) in zip(maxes, parts):
        ones = (lax.broadcasted_iota(jnp.int32, (SUM_ROWS, rows), 0) == 0).astype(BF16)
        v_aug = jnp.concatenate([v_t, ones], axis=0)
        m_exp = m_new if col_keep is None else jnp.maximum(m_new, s_max)
        p_t = jnp.exp2(s_ref[h, row:row + rows] - m_exp).astype(BF16)
        part = jnp.dot(v_aug, p_t, preferred_element_type=F32)
        if col_keep is not None:
            part = part * col_keep
        pv = part if pv is None else pv + part
        row += rows
    if first:
        acc_ref[...] = pv
    else:
        acc_ref[...] = jnp.exp2(m_ref[...] - m_new) * acc_ref[...] + pv
    m_ref[...] = m_new


def _pipeline(n_items, score, attend, bufs, unroll):
    if n_items == 0:
        return
    score(0, bufs[0])
    n_loops = (n_items - 1) // unroll

    def body(p, carry):
        for t in range(unroll):
            n = unroll * p + t
            score(n + 1, bufs[(t + 1) % 2])
            attend(n, bufs[t % 2])
        return carry

    lax.fori_loop(0, n_loops, body, 0)
    done = unroll * n_loops
    for n in range(done, n_items):
        if n + 1 < n_items:
            score(n + 1, bufs[(n + 1 - done) % 2])
        attend(n, bufs[(n - done) % 2])


def _tile_rows(j, span=1):
    return pl.ds(pl.multiple_of(j * ATTN_TILE, ATTN_TILE), span * ATTN_TILE)


def _write_heads(acc_ref, o_ref, nq, head_dim):
    def one(qi, carry):
        for h in range(HEAD_GROUP):
            acc = acc_ref[qi, h]
            o_ref[qi, h * head_dim:(h + 1) * head_dim, :] = (
                acc[:head_dim] / acc[head_dim:head_dim + 1]).astype(o_ref.dtype)
        return carry

    lax.fori_loop(0, nq, one, 0)


def _causal_items(nq, n_near):
    phases = [[(i, i - d) for i in range(d, nq)] for d in range(n_near)]
    pairs, singles = [], []
    for i in range(n_near, nq):
        n_far = i - n_near + 1
        pairs += [(i, j) for j in range(0, n_far - 1, 2)]
        if n_far % 2:
            singles.append((i, n_far - 1))
    phases += [pairs, singles]
    flat = [it for ph in phases for it in ph]
    tq = jnp.asarray([it[0] for it in flat], jnp.int32)
    tk = jnp.asarray([it[1] for it in flat], jnp.int32)
    return tq, tk, [len(ph) for ph in phases]


def _moba_kernel(tq_ref, tk_ref, qt_ref, k_ref, vt_ref, kmean_ref, bias_ref, o_ref,
                 keep_ref, s0_ref, s1_ref, t0_ref, t1_ref, m_ref, acc_ref, *, seq, phases):
    nblk = seq // MOBA_BLOCK

    def head_query(qi, h):
        pr, sub = divmod(h, 2)
        q_h = qt_ref[qi, pl.ds(pr * LANES + sub * A_HEAD_DIM, A_HEAD_DIM), :]
        zeros = jnp.zeros_like(q_h)
        return jnp.concatenate([q_h, zeros] if sub == 0 else [zeros, q_h], axis=0)

    def select_blocks(i):
        row = lax.broadcasted_iota(jnp.int32, (nblk, ATTN_TILE), 0)
        rowf = row.astype(F32)
        past = row < i
        for h in range(HEAD_GROUP):
            kmean = kmean_ref[:, (h // 2) * LANES:(h // 2 + 1) * LANES].astype(BF16)
            gate = jnp.dot(kmean, head_query(i, h), preferred_element_type=F32)
            gate = jnp.where(past, gate, -jnp.inf)
            picked = jnp.zeros((nblk, ATTN_TILE), F32)
            for _ in range(MOBA_TOPK):
                top = jnp.max(gate, axis=0, keepdims=True)
                first = jnp.min(jnp.where(gate == top, rowf, float(nblk)), axis=0, keepdims=True)
                pick = rowf == first
                picked = jnp.where(pick, 1.0, picked)
                gate = jnp.where(pick, -jnp.inf, gate)
            keep_ref[i, h] = jnp.where(past, picked, 0.0)

    def stage(offset, span, bias_cols, first):
        def score(n, buf):
            qi, kj = tq_ref[offset + n], tk_ref[offset + n]
            for h in range(HEAD_GROUP):
                k_t = k_ref[_tile_rows(kj, span), (h // 2) * LANES:(h // 2 + 1) * LANES]
                s_t = jnp.dot(k_t, head_query(qi, h), preferred_element_type=F32)
                if bias_cols is not None:
                    s_t = s_t + bias_ref[h, :, bias_cols]
                _store_scores(buf, h, [s_t[t * MOBA_BLOCK:(t + 1) * MOBA_BLOCK]
                                       for t in range(span)])

        def attend(n, buf):
            qi, kj = tq_ref[offset + n], tk_ref[offset + n]
            for h in range(HEAD_GROUP):
                parts = [(MOBA_BLOCK, vt_ref[kj + t, h * A_HEAD_DIM:(h + 1) * A_HEAD_DIM, :],
                          None if first else keep_ref[qi, h, pl.ds(kj + t, 1), :])
                         for t in range(span)]
                _softmax_step(buf, h, parts, m_ref.at[qi, h], acc_ref.at[qi, h], first)
            if first:
                select_blocks(qi)

        return score, attend

    bufs = ((s0_ref, t0_ref), (s1_ref, t1_ref))
    n_own, n_prev, n_pairs, n_single = phases
    _pipeline(n_own, *stage(0, 1, slice(0, MOBA_BLOCK), True), bufs, PIPELINE_UNROLL)
    offset = n_own
    _pipeline(n_prev, *stage(offset, 1, slice(MOBA_BLOCK, 2 * MOBA_BLOCK), False), bufs,
              PIPELINE_UNROLL)
    offset += n_prev
    _pipeline(n_pairs, *stage(offset, 2, None, False), bufs, PIPELINE_UNROLL // 2)
    offset += n_pairs
    _pipeline(n_single, *stage(offset, 1, None, False), bufs, PIPELINE_UNROLL)
    _write_heads(acc_ref, o_ref, nblk, A_HEAD_DIM)


def _moba(qat, ka, vat, kmean, bias):
    b, seq, _ = ka.shape
    nq = seq // ATTN_TILE
    gw = HEAD_GROUP * A_HEAD_DIM
    tq, tk, phases = _causal_items(nq, 2)
    per_group = lambda bi, g, *_: (bi, 0, g)
    tiles = lambda bi, g, *_: (bi, 0, g, 0)
    return pl.pallas_call(
        functools.partial(_moba_kernel, seq=seq, phases=phases),
        grid_spec=pltpu.PrefetchScalarGridSpec(
            num_scalar_prefetch=2,
            grid=(b, A_HEADS // HEAD_GROUP),
            in_specs=[
                pl.BlockSpec((None, nq, gw, ATTN_TILE), tiles),
                pl.BlockSpec((None, seq, gw), per_group),
                pl.BlockSpec((None, nq, gw, MOBA_BLOCK), tiles),
                pl.BlockSpec((None, nq, gw), per_group),
                pl.BlockSpec((HEAD_GROUP, MOBA_BLOCK, 2 * MOBA_BLOCK), lambda bi, g, *_: (g, 0, 0)),
            ],
            out_specs=pl.BlockSpec((None, nq, gw, ATTN_TILE), tiles),
            scratch_shapes=[
                pltpu.VMEM((nq, HEAD_GROUP, nq, ATTN_TILE), F32),
                pltpu.VMEM((HEAD_GROUP, ITEM_SPAN * ATTN_TILE, ATTN_TILE), F32),
                pltpu.VMEM((HEAD_GROUP, ITEM_SPAN * ATTN_TILE, ATTN_TILE), F32),
                pltpu.VMEM((HEAD_GROUP, ITEM_SPAN, 1, ATTN_TILE), F32),
                pltpu.VMEM((HEAD_GROUP, ITEM_SPAN, 1, ATTN_TILE), F32),
                pltpu.VMEM((nq, HEAD_GROUP, 1, ATTN_TILE), F32),
                pltpu.VMEM((nq, HEAD_GROUP, A_HEAD_DIM + SUM_ROWS, ATTN_TILE), F32),
            ]),
        out_shape=jax.ShapeDtypeStruct((b, nq, A_WIDTH, ATTN_TILE), BF16),
        compiler_params=pltpu.CompilerParams(
            dimension_semantics=("arbitrary", "arbitrary"), vmem_limit_bytes=VMEM_LIMIT),
        name="moba_attention",
    )(tq, tk, qat, ka, vat, kmean, bias)


def _mla_kernel(tq_ref, tk_ref, qt_ref, k_ref, vt_ref, o_ref, s0_ref, s1_ref, t0_ref, t1_ref,
                m_ref, acc_ref, *, nq, phases):
    def stage(offset, span, diagonal):
        def score(n, buf):
            qi, kj = tq_ref[offset + n], tk_ref[offset + n]
            for h in range(HEAD_GROUP):
                sl = slice(h * MLA_HEAD_PAD, (h + 1) * MLA_HEAD_PAD)
                s_t = jnp.dot(k_ref[_tile_rows(kj, span), sl], qt_ref[qi, sl, :],
                              preferred_element_type=F32)
                if diagonal:
                    key = lax.broadcasted_iota(jnp.int32, (ATTN_TILE, ATTN_TILE), 0)
                    qry = lax.broadcasted_iota(jnp.int32, (ATTN_TILE, ATTN_TILE), 1)
                    s_t = jnp.where(key <= qry, s_t, NEG)
                _store_scores(buf, h, [s_t])

        def attend(n, buf):
            qi, kj = tq_ref[offset + n], tk_ref[offset + n]
            for h in range(HEAD_GROUP):
                v_t = jnp.concatenate(
                    [vt_ref[kj + t, h * V_HEAD:(h + 1) * V_HEAD, :] for t in range(span)], axis=1)
                _softmax_step(buf, h, [(span * ATTN_TILE, v_t, None)],
                              m_ref.at[qi, h], acc_ref.at[qi, h], diagonal)

        return score, attend

    bufs = ((s0_ref, t0_ref), (s1_ref, t1_ref))
    n_diag, n_pairs, n_single = phases
    _pipeline(n_diag, *stage(0, 1, True), bufs, PIPELINE_UNROLL)
    _pipeline(n_pairs, *stage(n_diag, 2, False), bufs, PIPELINE_UNROLL // 2)
    _pipeline(n_single, *stage(n_diag + n_pairs, 1, False), bufs, PIPELINE_UNROLL)
    _write_heads(acc_ref, o_ref, nq, V_HEAD)


def _mla(qmt, km, vbt):
    b, seq, _ = km.shape
    nq = seq // ATTN_TILE
    qw = HEAD_GROUP * MLA_HEAD_PAD
    vw = HEAD_GROUP * V_HEAD
    tq, tk, phases = _causal_items(nq, 1)
    per_group = lambda bi, g, *_: (bi, 0, g)
    tiles = lambda bi, g, *_: (bi, 0, g, 0)
    return pl.pallas_call(
        functools.partial(_mla_kernel, nq=nq, phases=phases),
        grid_spec=pltpu.PrefetchScalarGridSpec(
            num_scalar_prefetch=2,
            grid=(b, B_HEADS // HEAD_GROUP),
            in_specs=[
                pl.BlockSpec((None, nq, qw, ATTN_TILE), tiles),
                pl.BlockSpec((None, seq, qw), per_group),
                pl.BlockSpec((None, nq, vw, ATTN_TILE), tiles),
            ],
            out_specs=pl.BlockSpec((None, nq, vw, ATTN_TILE), tiles),
            scratch_shapes=[
                pltpu.VMEM((HEAD_GROUP, ITEM_SPAN * ATTN_TILE, ATTN_TILE), F32),
                pltpu.VMEM((HEAD_GROUP, ITEM_SPAN * ATTN_TILE, ATTN_TILE), F32),
                pltpu.VMEM((HEAD_GROUP, ITEM_SPAN, 1, ATTN_TILE), F32),
                pltpu.VMEM((HEAD_GROUP, ITEM_SPAN, 1, ATTN_TILE), F32),
                pltpu.VMEM((nq, HEAD_GROUP, 1, ATTN_TILE), F32),
                pltpu.VMEM((nq, HEAD_GROUP, V_HEAD + SUM_ROWS, ATTN_TILE), F32),
            ]),
        out_shape=jax.ShapeDtypeStruct((b, nq, B_WIDTH, ATTN_TILE), BF16),
        compiler_params=pltpu.CompilerParams(
            dimension_semantics=("arbitrary", "arbitrary"), vmem_limit_bytes=VMEM_LIMIT),
        name="mla_attention",
    )(tq, tk, qmt, km, vbt)


def _out_kernel(x_ref, oa_ref, ob_ref, gattn_ref, wg_ref, wpa_ref, wpb_ref, wout_ref,
                gmlp_ref, wup_ref, wdn_ref, gfin_ref, o_ref):
    x = x_ref[...]
    n = _rms(x, gattn_ref[...]).astype(BF16)
    gates = jnp.dot(n, wg_ref[...], preferred_element_type=F32)
    tn_dims = (((0,), (0,)), ((), ()))
    tiles = range(ROW_TILE // ATTN_TILE)
    pa = jnp.concatenate([lax.dot_general(oa_ref[t], wpa_ref[...], tn_dims,
                                          preferred_element_type=F32) for t in tiles], axis=0)
    pb = jnp.concatenate([lax.dot_general(ob_ref[t], wpb_ref[...], tn_dims,
                                          preferred_element_type=F32) for t in tiles], axis=0)
    merged = (jax.nn.sigmoid(gates[:, :D_MODEL]) * pa
              + jax.nn.sigmoid(gates[:, D_MODEL:]) * pb).astype(BF16)
    h = x + jnp.dot(merged, wout_ref[...], preferred_element_type=F32)
    m = _rms(h, gmlp_ref[...]).astype(BF16)
    for c in range(D_FF // FF_CHUNK):
        cols = slice(c * FF_CHUNK, (c + 1) * FF_CHUNK)
        up = jnp.dot(m, wup_ref[:, cols], preferred_element_type=F32)
        act = jnp.square(jnp.maximum(up, 0.0)).astype(BF16)
        h = h + jnp.dot(act, wdn_ref[cols, :], preferred_element_type=F32)
    o_ref[...] = _rms(h, gfin_ref[...])


def _out_block(x2, oa, ob, g_attn, wg, wpa, wpb, wout, g_mlp, wup, wdn, g_fin):
    n_rows = x2.shape[0]
    tm = ROW_TILE
    row = lambda i: (i, 0)
    return pl.pallas_call(
        _out_kernel,
        grid=(n_rows // tm,),
        in_specs=[
            pl.BlockSpec((tm, D_MODEL), row),
            pl.BlockSpec((tm // ATTN_TILE, A_WIDTH, ATTN_TILE), lambda i: (i, 0, 0)),
            pl.BlockSpec((tm // ATTN_TILE, B_WIDTH, ATTN_TILE), lambda i: (i, 0, 0)),
            _const_spec(g_attn.shape), _const_spec(wg.shape), _const_spec(wpa.shape),
            _const_spec(wpb.shape), _const_spec(wout.shape), _const_spec(g_mlp.shape),
            _const_spec(wup.shape), _const_spec(wdn.shape), _const_spec(g_fin.shape),
        ],
        out_specs=pl.BlockSpec((tm, D_MODEL), row),
        out_shape=jax.ShapeDtypeStruct((n_rows, D_MODEL), F32),
        compiler_params=pltpu.CompilerParams(
            dimension_semantics=("arbitrary",), vmem_limit_bytes=VMEM_LIMIT),
        name="out_mlp",
    )(x2, oa, ob, g_attn, wg, wpa, wpb, wout, g_mlp, wup, wdn, g_fin)


def _rope_tables(seq):
    half = QK_ROPE // 2
    inv_freq = ROPE_THETA ** (-jnp.arange(half, dtype=F32) / half)
    ang = jnp.arange(seq).astype(F32)[:, None] * inv_freq[None, :]
    cos2 = jnp.tile(jnp.cos(ang), (1, 2))
    sin2 = jnp.tile(jnp.sin(ang), (1, 2))
    pad = jnp.zeros((seq, MLA_HEAD_PAD - QK_NOPE - QK_ROPE), F32)
    cos_t = jnp.concatenate([jnp.ones((seq, QK_NOPE), F32), cos2, pad], axis=1)
    sin_t = jnp.concatenate([jnp.zeros((seq, QK_NOPE), F32), sin2, pad], axis=1)
    return cos_t, sin_t


def _rotate_half_cols(w):
    half = QK_ROPE // 2
    return jnp.concatenate([-w[..., half:], w[..., :half]], axis=-1)


def _pad_heads(w_nope, w_rope):
    k, h, _ = w_nope.shape
    if w_rope is None:
        w_rope = jnp.zeros((k, h, QK_ROPE), w_nope.dtype)
    pad = jnp.zeros((k, h, MLA_HEAD_PAD - QK_NOPE - QK_ROPE), w_nope.dtype)
    return jnp.concatenate([w_nope, w_rope, pad], axis=-1).reshape(k, h * MLA_HEAD_PAD)


def kernel(x, w_in, rel_bias, mla_q_norm, w_uq, mla_kv_norm, w_uk, w_uv, w_proj_a, w_proj_b,
           w_out, norm_attn, norm_mlp, w_mlp_up, w_mlp_down, norm_final):
    b, seq, d = x.shape
    assert d == D_MODEL and seq % ROW_TILE == 0 and seq // MOBA_BLOCK <= LANES
    assert w_in.shape[0] == 1, "single-layer block"
    x2 = x.reshape(b * seq, d)

    w = w_in[0]
    o_k = A_WIDTH
    o_v = 2 * A_WIDTH
    o_cq = 3 * A_WIDTH
    o_ckv = o_cq + Q_LORA
    o_kr = o_ckv + KV_LORA
    o_g = o_kr + QK_ROPE
    wk = w[:, o_k:o_v].astype(BF16)
    wqvt = jnp.concatenate([w[:, :o_k], w[:, o_v:o_cq]], axis=1).T.astype(BF16)
    w_kr = w[:, o_kr:o_g]
    lane_pad = lambda a: jnp.pad(a, ((0, 0), (QK_NOPE, MLA_HEAD_PAD - QK_NOPE - QK_ROPE)))
    wc = jnp.concatenate(
        [w[:, o_cq:o_kr], lane_pad(w_kr), lane_pad(_rotate_half_cols(w_kr))], axis=1).astype(BF16)
    wg = w[:, o_g:].astype(BF16)

    uq = w_uq[0].reshape(Q_LORA, B_HEADS, QK_NOPE + QK_ROPE)
    uq_nope, uq_rope = uq[..., :QK_NOPE], uq[..., QK_NOPE:]
    wuqt = _pad_heads(uq_nope, uq_rope).T.astype(BF16)
    wuk = _pad_heads(w_uk[0].reshape(KV_LORA, B_HEADS, QK_NOPE), None).astype(BF16)
    wuvt = w_uv[0].T.astype(BF16)

    cos_t, sin_t = _rope_tables(seq)
    q_scale = (QK_NOPE + QK_ROPE) ** -0.5 * LOG2E
    row2 = lambda a: a.reshape(1, -1)

    qat, ka, vat, kmean, qmt, km, vbt = _inproj(
        x2, row2(norm_attn[0]), wk, wqvt, wc, row2(mla_q_norm[0]), row2(mla_kv_norm[0]),
        wuqt, wuk, wuvt, (cos_t * q_scale).T, (sin_t * q_scale).T, cos_t, sin_t, b, seq)

    as_seq = lambda a: a.reshape(b, seq, a.shape[-1])
    bias = _bias_tiles(rel_bias)
    oa = _moba(qat, as_seq(ka), vat, kmean.reshape(b, seq // MOBA_BLOCK, A_WIDTH), bias)
    ob = _mla(qmt, as_seq(km), vbt)

    out = _out_block(
        x2, oa.reshape(-1, A_WIDTH, ATTN_TILE), ob.reshape(-1, B_WIDTH, ATTN_TILE),
        row2(norm_attn[0]),
        wg, w_proj_a[0].astype(BF16), w_proj_b[0].astype(BF16), w_out[0].astype(BF16),
        row2(norm_mlp[0]), w_mlp_up[0].astype(BF16), w_mlp_down[0].astype(BF16),
        row2(norm_final))
    return out.reshape(b, seq, d)
```

```python
import functools
import math

import jax
import jax.numpy as jnp
from jax import lax
from jax.experimental import pallas as pl
from jax.experimental.pallas import tpu as pltpu

D_MODEL = 1024
A_HEADS = 8
A_HEAD_DIM = 64
A_WIDTH = A_HEADS * A_HEAD_DIM
MOBA_BLOCK = 256
MOBA_TOPK = 3
REL_BUCKETS = 32
REL_MAX_DIST = 128
B_HEADS = 8
QK_NOPE = 64
QK_ROPE = 32
V_HEAD = 64
B_WIDTH = B_HEADS * V_HEAD
Q_LORA = 384
KV_LORA = 256
ROPE_THETA = 10000.0
D_FF = 4 * D_MODEL
EPS = 1e-6
NEG = -1e30
LOG2E = math.log2(math.e)

LANES = 128
MLA_HEAD_PAD = 128
ATTN_TILE = 256
HEAD_GROUP = 4
SUM_ROWS = 16
ITEM_SPAN = 2
PIPELINE_UNROLL = 8
ROW_TILE = 512
FF_CHUNK = 1024
VMEM_LIMIT = 56 * 1024 * 1024

F32 = jnp.float32
BF16 = jnp.bfloat16
NT_DIMS = (((1,), (1,)), ((), ()))


def _rms(xf, g):
    y = xf * lax.rsqrt(jnp.mean(xf * xf, axis=-1, keepdims=True) + EPS)
    return y * g


def _const_spec(shape):
    zeros = (0,) * len(shape)
    return pl.BlockSpec(shape, lambda *_: zeros, pipeline_mode=pl.Buffered(1))


def _bias_kernel(rel_ref, o_ref):
    h = pl.program_id(0)
    shape = (MOBA_BLOCK, 2 * MOBA_BLOCK)
    r = lax.broadcasted_iota(jnp.int32, shape, 0)
    c = lax.broadcasted_iota(jnp.int32, shape, 1)
    d = c - r
    dist = jnp.maximum(d, 0)
    max_exact = REL_BUCKETS // 2
    df = jnp.maximum(dist, 1).astype(F32)
    large = max_exact + (jnp.log(df / max_exact) / math.log(REL_MAX_DIST / max_exact)
                         * (REL_BUCKETS - max_exact)).astype(jnp.int32)
    large = jnp.minimum(large, REL_BUCKETS - 1)
    bucket = jnp.where(dist < max_exact, dist, large)
    val = jnp.zeros(shape, F32)
    for b in range(REL_BUCKETS):
        val = jnp.where(bucket == b, rel_ref[b, h], val)
    o_ref[...] = jnp.where(d >= 0, (val - rel_ref[REL_BUCKETS - 1, h]) * LOG2E, NEG)


def _bias_tiles(rel_bias):
    return pl.pallas_call(
        _bias_kernel,
        grid=(A_HEADS,),
        in_specs=[pl.BlockSpec(memory_space=pltpu.SMEM)],
        out_specs=pl.BlockSpec((None, MOBA_BLOCK, 2 * MOBA_BLOCK), lambda h: (h, 0, 0)),
        out_shape=jax.ShapeDtypeStruct((A_HEADS, MOBA_BLOCK, 2 * MOBA_BLOCK), F32),
        name="moba_bias_tiles",
    )(rel_bias)


def _inproj_kernel(x_ref, g_ref, wk_ref, wqvt_ref, wc_ref, qn_ref, kvn_ref, wuqt_ref,
                   wuk_ref, wuvt_ref, cqt_ref, sqt_ref, ck_ref, sk_ref,
                   qat_ref, ka_ref, vat_ref, kmean_ref, qmt_ref, km_ref, vbt_ref):
    n = _rms(x_ref[...], g_ref[...]).astype(BF16)
    k = jnp.dot(n, wk_ref[...], preferred_element_type=F32)
    ka_ref[...] = k.astype(BF16)
    nblk = ROW_TILE // ATTN_TILE
    kmean_ref[0] = jnp.sum(k.reshape(nblk, MOBA_BLOCK, A_WIDTH), axis=1) * (1.0 / MOBA_BLOCK)

    c = jnp.dot(n, wc_ref[...], preferred_element_type=F32)
    cq = _rms(c[:, :Q_LORA], qn_ref[...]).astype(BF16)
    ckv = _rms(c[:, Q_LORA:Q_LORA + KV_LORA], kvn_ref[...]).astype(BF16)
    kr = c[:, Q_LORA + KV_LORA:Q_LORA + KV_LORA + LANES]
    krr = c[:, Q_LORA + KV_LORA + LANES:]
    for t in range(nblk):
        rows = slice(t * ATTN_TILE, (t + 1) * ATTN_TILE)
        qv_t = lax.dot_general(wqvt_ref[...], n[rows], NT_DIMS,
                               preferred_element_type=F32)
        qat_ref[t] = (qv_t[:A_WIDTH] * (A_HEAD_DIM ** -0.5 * LOG2E)).astype(BF16)
        vat_ref[t] = qv_t[A_WIDTH:].astype(BF16)
        vbt_ref[t] = lax.dot_general(wuvt_ref[...], ckv[rows], NT_DIMS,
                                     preferred_element_type=F32).astype(BF16)
        q_t = lax.dot_general(wuqt_ref[...], cq[rows], NT_DIMS,
                              preferred_element_type=F32)
        cq_t, sq_t = cqt_ref[:, rows], sqt_ref[:, rows]
        half = QK_ROPE // 2
        for h in range(B_HEADS):
            q_h = q_t[h * MLA_HEAD_PAD:(h + 1) * MLA_HEAD_PAD]
            x1 = q_h[QK_NOPE:QK_NOPE + half]
            x2 = q_h[QK_NOPE + half:QK_NOPE + QK_ROPE]
            rot_h = jnp.concatenate(
                [jnp.zeros((QK_NOPE, ATTN_TILE), F32), -x2, x1,
                 jnp.zeros((MLA_HEAD_PAD - QK_NOPE - QK_ROPE, ATTN_TILE), F32)], axis=0)
            qmt_ref[t, h * MLA_HEAD_PAD:(h + 1) * MLA_HEAD_PAD, :] = (
                q_h * cq_t + rot_h * sq_t).astype(BF16)
    kn = jnp.dot(ckv, wuk_ref[...], preferred_element_type=F32)
    k_rope = kr * ck_ref[...] + krr * sk_ref[...]
    for h in range(B_HEADS):
        sl = slice(h * MLA_HEAD_PAD, (h + 1) * MLA_HEAD_PAD)
        km_ref[:, sl] = (kn[:, sl] + k_rope).astype(BF16)


def _inproj(x2, g_attn, wk, wqvt, wc, qn, kvn, wuqt, wuk, wuvt, cq_tt, sq_tt, ck_t, sk_t,
            batch, seq):
    n_rows = x2.shape[0]
    tm = ROW_TILE
    steps = n_rows // tm
    per_seq = seq // tm
    nblk = tm // ATTN_TILE
    row = lambda i: (i, 0)
    tab = lambda i: (i % per_seq, 0)
    tab_t = lambda i: (0, i % per_seq)
    tile_map = lambda i: (i // per_seq, i % per_seq, 0, 0)
    bf = lambda w: jax.ShapeDtypeStruct((n_rows, w), BF16)
    tiles = lambda w: jax.ShapeDtypeStruct((batch, seq // ATTN_TILE, w, ATTN_TILE), BF16)
    tile_spec = lambda w: pl.BlockSpec((None, nblk, w, ATTN_TILE), tile_map)
    return pl.pallas_call(
        _inproj_kernel,
        grid=(steps,),
        in_specs=[
            pl.BlockSpec((tm, D_MODEL), row),
            _const_spec(g_attn.shape), _const_spec(wk.shape), _const_spec(wqvt.shape),
            _const_spec(wc.shape), _const_spec(qn.shape), _const_spec(kvn.shape),
            _const_spec(wuqt.shape), _const_spec(wuk.shape),
            _const_spec(wuvt.shape),
            pl.BlockSpec((LANES, tm), tab_t), pl.BlockSpec((LANES, tm), tab_t),
            pl.BlockSpec((tm, LANES), tab), pl.BlockSpec((tm, LANES), tab),
        ],
        out_specs=[
            tile_spec(A_WIDTH), pl.BlockSpec((tm, A_WIDTH), row), tile_spec(A_WIDTH),
            pl.BlockSpec((1, nblk, A_WIDTH), lambda i: (i, 0, 0)),
            tile_spec(B_HEADS * MLA_HEAD_PAD),
            pl.BlockSpec((tm, B_HEADS * MLA_HEAD_PAD), row),
            tile_spec(B_WIDTH),
        ],
        out_shape=[
            tiles(A_WIDTH), bf(A_WIDTH), tiles(A_WIDTH),
            jax.ShapeDtypeStruct((steps, nblk, A_WIDTH), F32),
            tiles(B_HEADS * MLA_HEAD_PAD), bf(B_HEADS * MLA_HEAD_PAD), tiles(B_WIDTH),
        ],
        compiler_params=pltpu.CompilerParams(
            dimension_semantics=("arbitrary",), vmem_limit_bytes=VMEM_LIMIT),
        name="in_proj",
    )(x2, g_attn, wk, wqvt, wc, qn, kvn, wuqt, wuk, wuvt, cq_tt, sq_tt, ck_t, sk_t)


def _store_scores(buf, h, parts):
    s_ref, max_ref = buf
    row = 0
    for idx, s_t in enumerate(parts):
        s_ref[h, row:row + s_t.shape[0]] = s_t
        max_ref[h, idx] = jnp.max(s_t, axis=0, keepdims=True)
        row += s_t.shape[0]


def _softmax_step(buf, h, parts, m_ref, acc_ref, first):
    s_ref, max_ref = buf
    maxes = [max_ref[h, idx] for idx in range(len(parts))]
    m_new = None if first else m_ref[...]
    for s_max, (_, _, col_keep) in zip(maxes, parts):
        if col_keep is not None:
            s_max = jnp.where(col_keep > 0, s_max, NEG)
        m_new = s_max if m_new is None else jnp.maximum(m_new, s_max)
    pv = None
    row = 0
    for s_max, (rows, v_t, col_keep) in zip(maxes, parts):
        ones = (lax.broadcasted_iota(jnp.int32, (SUM_ROWS, rows), 0) == 0).astype(BF16)
        v_aug = jnp.concatenate([v_t, ones], axis=0)
        m_exp = m_new if col_keep is None else jnp.maximum(m_new, s_max)
        p_t = jnp.exp2(s_ref[h, row:row + rows] - m_exp).astype(BF16)
        part = jnp.dot(v_aug, p_t, preferred_element_type=F32)
        if col_keep is not None:
            part = part * col_keep
        pv = part if pv is None else pv + part
        row += rows
    if first:
        acc_ref[...] = pv
    else:
        acc_ref[...] = jnp.exp2(m_ref[...] - m_new) * acc_ref[...] + pv
    m_ref[...] = m_new


def _pipeline(n_items, score, attend, bufs, unroll):
    if n_items == 0:
        return
    score(0, bufs[0])
    n_loops = (n_items - 1) // unroll

    def body(p, carry):
        for t in range(unroll):
            n = unroll * p + t
            score(n + 1, bufs[(t + 1) % 2])
            attend(n, bufs[t % 2])
        return carry

    lax.fori_loop(0, n_loops, body, 0)
    done = unroll * n_loops
    for n in range(done, n_items):
        if n + 1 < n_items:
            score(n + 1, bufs[(n + 1 - done) % 2])
        attend(n, bufs[(n - done) % 2])


def _tile_rows(j, span=1):
    return pl.ds(pl.multiple_of(j * ATTN_TILE, ATTN_TILE), span * ATTN_TILE)


def _write_heads(acc_ref, o_ref, nq, head_dim):
    def one(qi, carry):
        for h in range(HEAD_GROUP):
            acc = acc_ref[qi, h]
            o_ref[qi, h * head_dim:(h + 1) * head_dim, :] = (
                acc[:head_dim] / acc[head_dim:head_dim + 1]).astype(o_ref.dtype)
        return carry

    lax.fori_loop(0, nq, one, 0)


def _causal_items(nq, n_near):
    phases = [[(i, i - d) for i in range(d, nq)] for d in range(n_near)]
    pairs, singles = [], []
    for i in range(n_near, nq):
        n_far = i - n_near + 1
        pairs += [(i, j) for j in range(0, n_far - 1, 2)]
        if n_far % 2:
            singles.append((i, n_far - 1))
    phases += [pairs, singles]
    flat = [it for ph in phases for it in ph]
    tq = jnp.asarray([it[0] for it in flat], jnp.int32)
    tk = jnp.asarray([it[1] for it in flat], jnp.int32)
    return tq, tk, [len(ph) for ph in phases]


def _moba_kernel(tq_ref, tk_ref, qt_ref, k_ref, vt_ref, kmean_ref, bias_ref, o_ref,
                 keep_ref, s0_ref, s1_ref, t0_ref, t1_ref, m_ref, acc_ref, *, seq, phases):
    nblk = seq // MOBA_BLOCK

    def head_query(qi, h):
        pr, sub = divmod(h, 2)
        q_h = qt_ref[qi, pl.ds(pr * LANES + sub * A_HEAD_DIM, A_HEAD_DIM), :]
        zeros = jnp.zeros_like(q_h)
        return jnp.concatenate([q_h, zeros] if sub == 0 else [zeros, q_h], axis=0)

    def select_blocks(i):
        row = lax.broadcasted_iota(jnp.int32, (nblk, ATTN_TILE), 0)
        rowf = row.astype(F32)
        past = row < i
        for h in range(HEAD_GROUP):
            kmean = kmean_ref[:, (h // 2) * LANES:(h // 2 + 1) * LANES].astype(BF16)
            gate = jnp.dot(kmean, head_query(i, h), preferred_element_type=F32)
            gate = jnp.where(past, gate, -jnp.inf)
            picked = jnp.zeros((nblk, ATTN_TILE), F32)
            for _ in range(MOBA_TOPK):
                top = jnp.max(gate, axis=0, keepdims=True)
                first = jnp.min(jnp.where(gate == top, rowf, float(nblk)), axis=0, keepdims=True)
                pick = rowf == first
                picked = jnp.where(pick, 1.0, picked)
                gate = jnp.where(pick, -jnp.inf, gate)
            keep_ref[i, h] = jnp.where(past, picked, 0.0)

    def stage(offset, span, bias_cols, first):
        def score(n, buf):
            qi, kj = tq_ref[offset + n], tk_ref[offset + n]
            for h in range(HEAD_GROUP):
                k_t = k_ref[_tile_rows(kj, span), (h // 2) * LANES:(h // 2 + 1) * LANES]
                s_t = jnp.dot(k_t, head_query(qi, h), preferred_element_type=F32)
                if bias_cols is not None:
                    s_t = s_t + bias_ref[h, :, bias_cols]
                _store_scores(buf, h, [s_t[t * MOBA_BLOCK:(t + 1) * MOBA_BLOCK]
                                       for t in range(span)])

        def attend(n, buf):
            qi, kj = tq_ref[offset + n], tk_ref[offset + n]
            for h in range(HEAD_GROUP):
                parts = [(MOBA_BLOCK, vt_ref[kj + t, h * A_HEAD_DIM:(h + 1) * A_HEAD_DIM, :],
                          None if first else keep_ref[qi, h, pl.ds(kj + t, 1), :])
                         for t in range(span)]
                _softmax_step(buf, h, parts, m_ref.at[qi, h], acc_ref.at[qi, h], first)
            if first:
                select_blocks(qi)

        return score, attend

    bufs = ((s0_ref, t0_ref), (s1_ref, t1_ref))
    n_own, n_prev, n_pairs, n_single = phases
    _pipeline(n_own, *stage(0, 1, slice(0, MOBA_BLOCK), True), bufs, PIPELINE_UNROLL)
    offset = n_own
    _pipeline(n_prev, *stage(offset, 1, slice(MOBA_BLOCK, 2 * MOBA_BLOCK), False), bufs,
              PIPELINE_UNROLL)
    offset += n_prev
    _pipeline(n_pairs, *stage(offset, 2, None, False), bufs, PIPELINE_UNROLL // 2)
    offset += n_pairs
    _pipeline(n_single, *stage(offset, 1, None, False), bufs, PIPELINE_UNROLL)
    _write_heads(acc_ref, o_ref, nblk, A_HEAD_DIM)


def _moba(qat, ka, vat, kmean, bias):
    b, seq, _ = ka.shape
    nq = seq // ATTN_TILE
    gw = HEAD_GROUP * A_HEAD_DIM
    tq, tk, phases = _causal_items(nq, 2)
    per_group = lambda bi, g, *_: (bi, 0, g)
    tiles = lambda bi, g, *_: (bi, 0, g, 0)
    return pl.pallas_call(
        functools.partial(_moba_kernel, seq=seq, phases=phases),
        grid_spec=pltpu.PrefetchScalarGridSpec(
            num_scalar_prefetch=2,
            grid=(b, A_HEADS // HEAD_GROUP),
            in_specs=[
                pl.BlockSpec((None, nq, gw, ATTN_TILE), tiles),
                pl.BlockSpec((None, seq, gw), per_group),
                pl.BlockSpec((None, nq, gw, MOBA_BLOCK), tiles),
                pl.BlockSpec((None, nq, gw), per_group),
                pl.BlockSpec((HEAD_GROUP, MOBA_BLOCK, 2 * MOBA_BLOCK), lambda bi, g, *_: (g, 0, 0)),
            ],
            out_specs=pl.BlockSpec((None, nq, gw, ATTN_TILE), tiles),
            scratch_shapes=[
                pltpu.VMEM((nq, HEAD_GROUP, nq, ATTN_TILE), F32),
                pltpu.VMEM((HEAD_GROUP, ITEM_SPAN * ATTN_TILE, ATTN_TILE), F32),
                pltpu.VMEM((HEAD_GROUP, ITEM_SPAN * ATTN_TILE, ATTN_TILE), F32),
                pltpu.VMEM((HEAD_GROUP, ITEM_SPAN, 1, ATTN_TILE), F32),
                pltpu.VMEM((HEAD_GROUP, ITEM_SPAN, 1, ATTN_TILE), F32),
                pltpu.VMEM((nq, HEAD_GROUP, 1, ATTN_TILE), F32),
                pltpu.VMEM((nq, HEAD_GROUP, A_HEAD_DIM + SUM_ROWS, ATTN_TILE), F32),
            ]),
        out_shape=jax.ShapeDtypeStruct((b, nq, A_WIDTH, ATTN_TILE), BF16),
        compiler_params=pltpu.CompilerParams(
            dimension_semantics=("arbitrary", "arbitrary"), vmem_limit_bytes=VMEM_LIMIT),
        name="moba_attention",
    )(tq, tk, qat, ka, vat, kmean, bias)


def _mla_kernel(tq_ref, tk_ref, qt_ref, k_ref, vt_ref, o_ref, s0_ref, s1_ref, t0_ref, t1_ref,
                m_ref, acc_ref, *, nq, phases):
    def stage(offset, span, diagonal):
        def score(n, buf):
            qi, kj = tq_ref[offset + n], tk_ref[offset + n]
            for h in range(HEAD_GROUP):
                sl = slice(h * MLA_HEAD_PAD, (h + 1) * MLA_HEAD_PAD)
                s_t = jnp.dot(k_ref[_tile_rows(kj, span), sl], qt_ref[qi, sl, :],
                              preferred_element_type=F32)
                if diagonal:
                    key = lax.broadcasted_iota(jnp.int32, (ATTN_TILE, ATTN_TILE), 0)
                    qry = lax.broadcasted_iota(jnp.int32, (ATTN_TILE, ATTN_TILE), 1)
                    s_t = jnp.where(key <= qry, s_t, NEG)
                _store_scores(buf, h, [s_t[t * ATTN_TILE:(t + 1) * ATTN_TILE]
                                       for t in range(span)])

        def attend(n, buf):
            qi, kj = tq_ref[offset + n], tk_ref[offset + n]
            for h in range(HEAD_GROUP):
                parts = [(ATTN_TILE, vt_ref[kj + t, h * V_HEAD:(h + 1) * V_HEAD, :], None)
                         for t in range(span)]
                _softmax_step(buf, h, parts, m_ref.at[qi, h], acc_ref.at[qi, h], diagonal)

        return score, attend

    bufs = ((s0_ref, t0_ref), (s1_ref, t1_ref))
    n_diag, n_pairs, n_single = phases
    _pipeline(n_diag, *stage(0, 1, True), bufs, PIPELINE_UNROLL)
    _pipeline(n_pairs, *stage(n_diag, 2, False), bufs, PIPELINE_UNROLL // 2)
    _pipeline(n_single, *stage(n_diag + n_pairs, 1, False), bufs, PIPELINE_UNROLL)
    _write_heads(acc_ref, o_ref, nq, V_HEAD)


def _mla(qmt, km, vbt):
    b, seq, _ = km.shape
    nq = seq // ATTN_TILE
    qw = HEAD_GROUP * MLA_HEAD_PAD
    vw = HEAD_GROUP * V_HEAD
    tq, tk, phases = _causal_items(nq, 1)
    per_group = lambda bi, g, *_: (bi, 0, g)
    tiles = lambda bi, g, *_: (bi, 0, g, 0)
    return pl.pallas_call(
        functools.partial(_mla_kernel, nq=nq, phases=phases),
        grid_spec=pltpu.PrefetchScalarGridSpec(
            num_scalar_prefetch=2,
            grid=(b, B_HEADS // HEAD_GROUP),
            in_specs=[
                pl.BlockSpec((None, nq, qw, ATTN_TILE), tiles),
                pl.BlockSpec((None, seq, qw), per_group),
                pl.BlockSpec((None, nq, vw, ATTN_TILE), tiles),
            ],
            out_specs=pl.BlockSpec((None, nq, vw, ATTN_TILE), tiles),
            scratch_shapes=[
                pltpu.VMEM((HEAD_GROUP, ITEM_SPAN * ATTN_TILE, ATTN_TILE), F32),
                pltpu.VMEM((HEAD_GROUP, ITEM_SPAN * ATTN_TILE, ATTN_TILE), F32),
                pltpu.VMEM((HEAD_GROUP, ITEM_SPAN, 1, ATTN_TILE), F32),
                pltpu.VMEM((HEAD_GROUP, ITEM_SPAN, 1, ATTN_TILE), F32),
                pltpu.VMEM((nq, HEAD_GROUP, 1, ATTN_TILE), F32),
                pltpu.VMEM((nq, HEAD_GROUP, V_HEAD + SUM_ROWS, ATTN_TILE), F32),
            ]),
        out_shape=jax.ShapeDtypeStruct((b, nq, B_WIDTH, ATTN_TILE), BF16),
        compiler_params=pltpu.CompilerParams(
            dimension_semantics=("arbitrary", "arbitrary"), vmem_limit_bytes=VMEM_LIMIT),
        name="mla_attention",
    )(tq, tk, qmt, km, vbt)


def _out_kernel(x_ref, oa_ref, ob_ref, gattn_ref, wg_ref, wpa_ref, wpb_ref, wout_ref,
                gmlp_ref, wup_ref, wdn_ref, gfin_ref, o_ref):
    x = x_ref[...]
    n = _rms(x, gattn_ref[...]).astype(BF16)
    gates = jnp.dot(n, wg_ref[...], preferred_element_type=F32)
    tn_dims = (((0,), (0,)), ((), ()))
    tiles = range(ROW_TILE // ATTN_TILE)
    pa = jnp.concatenate([lax.dot_general(oa_ref[t], wpa_ref[...], tn_dims,
                                          preferred_element_type=F32) for t in tiles], axis=0)
    pb = jnp.concatenate([lax.dot_general(ob_ref[t], wpb_ref[...], tn_dims,
                                          preferred_element_type=F32) for t in tiles], axis=0)
    merged = (jax.nn.sigmoid(gates[:, :D_MODEL]) * pa
              + jax.nn.sigmoid(gates[:, D_MODEL:]) * pb).astype(BF16)
    h = x + jnp.dot(merged, wout_ref[...], preferred_element_type=F32)
    m = _rms(h, gmlp_ref[...]).astype(BF16)
    for c in range(D_FF // FF_CHUNK):
        cols = slice(c * FF_CHUNK, (c + 1) * FF_CHUNK)
        up = jnp.dot(m, wup_ref[:, cols], preferred_element_type=F32)
        act = jnp.square(jnp.maximum(up, 0.0)).astype(BF16)
        h = h + jnp.dot(act, wdn_ref[cols, :], preferred_element_type=F32)
    o_ref[...] = _rms(h, gfin_ref[...])


def _out_block(x2, oa, ob, g_attn, wg, wpa, wpb, wout, g_mlp, wup, wdn, g_fin):
    n_rows = x2.shape[0]
    tm = ROW_TILE
    row = lambda i: (i, 0)
    return pl.pallas_call(
        _out_kernel,
        grid=(n_rows // tm,),
        in_specs=[
            pl.BlockSpec((tm, D_MODEL), row),
            pl.BlockSpec((tm // ATTN_TILE, A_WIDTH, ATTN_TILE), lambda i: (i, 0, 0)),
            pl.BlockSpec((tm // ATTN_TILE, B_WIDTH, ATTN_TILE), lambda i: (i, 0, 0)),
            _const_spec(g_attn.shape), _const_spec(wg.shape), _const_spec(wpa.shape),
            _const_spec(wpb.shape), _const_spec(wout.shape), _const_spec(g_mlp.shape),
            _const_spec(wup.shape), _const_spec(wdn.shape), _const_spec(g_fin.shape),
        ],
        out_specs=pl.BlockSpec((tm, D_MODEL), row),
        out_shape=jax.ShapeDtypeStruct((n_rows, D_MODEL), F32),
        compiler_params=pltpu.CompilerParams(
            dimension_semantics=("arbitrary",), vmem_limit_bytes=VMEM_LIMIT),
        name="out_mlp",
    )(x2, oa, ob, g_attn, wg, wpa, wpb, wout, g_mlp, wup, wdn, g_fin)


def _rope_tables(seq):
    half = QK_ROPE // 2
    inv_freq = ROPE_THETA ** (-jnp.arange(half, dtype=F32) / half)
    ang = jnp.arange(seq).astype(F32)[:, None] * inv_freq[None, :]
    cos2 = jnp.tile(jnp.cos(ang), (1, 2))
    sin2 = jnp.tile(jnp.sin(ang), (1, 2))
    pad = jnp.zeros((seq, MLA_HEAD_PAD - QK_NOPE - QK_ROPE), F32)
    cos_t = jnp.concatenate([jnp.ones((seq, QK_NOPE), F32), cos2, pad], axis=1)
    sin_t = jnp.concatenate([jnp.zeros((seq, QK_NOPE), F32), sin2, pad], axis=1)
    return cos_t, sin_t


def _rotate_half_cols(w):
    half = QK_ROPE // 2
    return jnp.concatenate([-w[..., half:], w[..., :half]], axis=-1)


def _pad_heads(w_nope, w_rope):
    k, h, _ = w_nope.shape
    if w_rope is None:
        w_rope = jnp.zeros((k, h, QK_ROPE), w_nope.dtype)
    pad = jnp.zeros((k, h, MLA_HEAD_PAD - QK_NOPE - QK_ROPE), w_nope.dtype)
    return jnp.concatenate([w_nope, w_rope, pad], axis=-1).reshape(k, h * MLA_HEAD_PAD)


def kernel(x, w_in, rel_bias, mla_q_norm, w_uq, mla_kv_norm, w_uk, w_uv, w_proj_a, w_proj_b,
           w_out, norm_attn, norm_mlp, w_mlp_up, w_mlp_down, norm_final):
    b, seq, d = x.shape
    assert d == D_MODEL and seq % ROW_TILE == 0 and seq // MOBA_BLOCK <= LANES
    assert w_in.shape[0] == 1, "single-layer block"
    x2 = x.reshape(b * seq, d)

    w = w_in[0]
    o_k = A_WIDTH
    o_v = 2 * A_WIDTH
    o_cq = 3 * A_WIDTH
    o_ckv = o_cq + Q_LORA
    o_kr = o_ckv + KV_LORA
    o_g = o_kr + QK_ROPE
    wk = w[:, o_k:o_v].astype(BF16)
    wqvt = jnp.concatenate([w[:, :o_k], w[:, o_v:o_cq]], axis=1).T.astype(BF16)
    w_kr = w[:, o_kr:o_g]
    lane_pad = lambda a: jnp.pad(a, ((0, 0), (QK_NOPE, MLA_HEAD_PAD - QK_NOPE - QK_ROPE)))
    wc = jnp.concatenate(
        [w[:, o_cq:o_kr], lane_pad(w_kr), lane_pad(_rotate_half_cols(w_kr))], axis=1).astype(BF16)
    wg = w[:, o_g:].astype(BF16)

    uq = w_uq[0].reshape(Q_LORA, B_HEADS, QK_NOPE + QK_ROPE)
    uq_nope, uq_rope = uq[..., :QK_NOPE], uq[..., QK_NOPE:]
    wuqt = _pad_heads(uq_nope, uq_rope).T.astype(BF16)
    wuk = _pad_heads(w_uk[0].reshape(KV_LORA, B_HEADS, QK_NOPE), None).astype(BF16)
    wuvt = w_uv[0].T.astype(BF16)

    cos_t, sin_t = _rope_tables(seq)
    q_scale = (QK_NOPE + QK_ROPE) ** -0.5 * LOG2E
    row2 = lambda a: a.reshape(1, -1)

    qat, ka, vat, kmean, qmt, km, vbt = _inproj(
        x2, row2(norm_attn[0]), wk, wqvt, wc, row2(mla_q_norm[0]), row2(mla_kv_norm[0]),
        wuqt, wuk, wuvt, (cos_t * q_scale).T, (sin_t * q_scale).T, cos_t, sin_t, b, seq)

    as_seq = lambda a: a.reshape(b, seq, a.shape[-1])
    bias = _bias_tiles(rel_bias)
    oa = _moba(qat, as_seq(ka), vat, kmean.reshape(b, seq // MOBA_BLOCK, A_WIDTH), bias)
    ob = _mla(qmt, as_seq(km), vbt)

    out = _out_block(
        x2, oa.reshape(-1, A_WIDTH, ATTN_TILE), ob.reshape(-1, B_WIDTH, ATTN_TILE),
        row2(norm_attn[0]),
        wg, w_proj_a[0].astype(BF16), w_proj_b[0].astype(BF16), w_out[0].astype(BF16),
        row2(norm_mlp[0]), w_mlp_up[0].astype(BF16), w_mlp_down[0].astype(BF16),
        row2(norm_final))
    return out.reshape(b, seq, d)
```

```python
import functools
import math

import jax
import jax.numpy as jnp
from jax import lax
from jax.experimental import pallas as pl
from jax.experimental.pallas import tpu as pltpu

D_MODEL = 1024
A_HEADS = 8
A_HEAD_DIM = 64
A_WIDTH = A_HEADS * A_HEAD_DIM
MOBA_BLOCK = 256
MOBA_TOPK = 3
REL_BUCKETS = 32
REL_MAX_DIST = 128
B_HEADS = 8
QK_NOPE = 64
QK_ROPE = 32
V_HEAD = 64
B_WIDTH = B_HEADS * V_HEAD
Q_LORA = 384
KV_LORA = 256
ROPE_THETA = 10000.0
D_FF = 4 * D_MODEL
EPS = 1e-6
NEG = -1e30
LOG2E = math.log2(math.e)

LANES = 128
MLA_HEAD_PAD = 128
ATTN_TILE = 256
HEAD_GROUP = 4
SUM_ROWS = 16
ITEM_SPAN = 2
PIPELINE_UNROLL = 8
ROW_TILE = 512
FF_CHUNK = 1024
VMEM_LIMIT = 56 * 1024 * 1024

F32 = jnp.float32
BF16 = jnp.bfloat16
NT_DIMS = (((1,), (1,)), ((), ()))


def _rms(xf, g):
    y = xf * lax.rsqrt(jnp.mean(xf * xf, axis=-1, keepdims=True) + EPS)
    return y * g


def _const_spec(shape):
    zeros = (0,) * len(shape)
    return pl.BlockSpec(shape, lambda *_: zeros, pipeline_mode=pl.Buffered(1))


def _bias_kernel(rel_ref, o_ref):
    h = pl.program_id(0)
    shape = (MOBA_BLOCK, 2 * MOBA_BLOCK)
    r = lax.broadcasted_iota(jnp.int32, shape, 0)
    c = lax.broadcasted_iota(jnp.int32, shape, 1)
    d = c - r
    dist = jnp.maximum(d, 0)
    max_exact = REL_BUCKETS // 2
    df = jnp.maximum(dist, 1).astype(F32)
    large = max_exact + (jnp.log(df / max_exact) / math.log(REL_MAX_DIST / max_exact)
                         * (REL_BUCKETS - max_exact)).astype(jnp.int32)
    large = jnp.minimum(large, REL_BUCKETS - 1)
    bucket = jnp.where(dist < max_exact, dist, large)
    val = jnp.zeros(shape, F32)
    for b in range(REL_BUCKETS):
        val = jnp.where(bucket == b, rel_ref[b, h], val)
    o_ref[...] = jnp.where(d >= 0, (val - rel_ref[REL_BUCKETS - 1, h]) * LOG2E, NEG)


def _bias_tiles(rel_bias):
    return pl.pallas_call(
        _bias_kernel,
        grid=(A_HEADS,),
        in_specs=[pl.BlockSpec(memory_space=pltpu.SMEM)],
        out_specs=pl.BlockSpec((None, MOBA_BLOCK, 2 * MOBA_BLOCK), lambda h: (h, 0, 0)),
        out_shape=jax.ShapeDtypeStruct((A_HEADS, MOBA_BLOCK, 2 * MOBA_BLOCK), F32),
        name="moba_bias_tiles",
    )(rel_bias)


def _inproj_kernel(x_ref, g_ref, wk_ref, wqvt_ref, wc_ref, qn_ref, kvn_ref, wuqt_ref,
                   wuk_ref, wuvt_ref, cqt_ref, sqt_ref, ck_ref, sk_ref,
                   qat_ref, ka_ref, vat_ref, kmean_ref, qmt_ref, km_ref, vbt_ref):
    n = _rms(x_ref[...], g_ref[...]).astype(BF16)
    k = jnp.dot(n, wk_ref[...], preferred_element_type=F32)
    ka_ref[...] = k.astype(BF16)
    nblk = ROW_TILE // ATTN_TILE
    kmean_ref[0] = jnp.sum(k.reshape(nblk, MOBA_BLOCK, A_WIDTH), axis=1) * (1.0 / MOBA_BLOCK)

    c = jnp.dot(n, wc_ref[...], preferred_element_type=F32)
    cq = _rms(c[:, :Q_LORA], qn_ref[...]).astype(BF16)
    ckv = _rms(c[:, Q_LORA:Q_LORA + KV_LORA], kvn_ref[...]).astype(BF16)
    kr = c[:, Q_LORA + KV_LORA:Q_LORA + KV_LORA + LANES]
    krr = c[:, Q_LORA + KV_LORA + LANES:]
    for t in range(nblk):
        rows = slice(t * ATTN_TILE, (t + 1) * ATTN_TILE)
        qv_t = lax.dot_general(wqvt_ref[...], n[rows], NT_DIMS,
                               preferred_element_type=F32)
        qat_ref[t] = (qv_t[:A_WIDTH] * (A_HEAD_DIM ** -0.5 * LOG2E)).astype(BF16)
        vat_ref[t] = qv_t[A_WIDTH:].astype(BF16)
        vbt_ref[t] = lax.dot_general(wuvt_ref[...], ckv[rows], NT_DIMS,
                                     preferred_element_type=F32).astype(BF16)
        q_t = lax.dot_general(wuqt_ref[...], cq[rows], NT_DIMS,
                              preferred_element_type=F32)
        cq_t, sq_t = cqt_ref[:, rows], sqt_ref[:, rows]
        half = QK_ROPE // 2
        for h in range(B_HEADS):
            q_h = q_t[h * MLA_HEAD_PAD:(h + 1) * MLA_HEAD_PAD]
            x1 = q_h[QK_NOPE:QK_NOPE + half]
            x2 = q_h[QK_NOPE + half:QK_NOPE + QK_ROPE]
            rot_h = jnp.concatenate(
                [jnp.zeros((QK_NOPE, ATTN_TILE), F32), -x2, x1,
                 jnp.zeros((MLA_HEAD_PAD - QK_NOPE - QK_ROPE, ATTN_TILE), F32)], axis=0)
            qmt_ref[t, h * MLA_HEAD_PAD:(h + 1) * MLA_HEAD_PAD, :] = (
                q_h * cq_t + rot_h * sq_t).astype(BF16)
    kn = jnp.dot(ckv, wuk_ref[...], preferred_element_type=F32)
    k_rope = kr * ck_ref[...] + krr * sk_ref[...]
    for h in range(B_HEADS):
        sl = slice(h * MLA_HEAD_PAD, (h + 1) * MLA_HEAD_PAD)
        km_ref[:, sl] = (kn[:, sl] + k_rope).astype(BF16)


def _inproj(x2, g_attn, wk, wqvt, wc, qn, kvn, wuqt, wuk, wuvt, cq_tt, sq_tt, ck_t, sk_t,
            batch, seq):
    n_rows = x2.shape[0]
    tm = ROW_TILE
    steps = n_rows // tm
    per_seq = seq // tm
    nblk = tm // ATTN_TILE
    row = lambda i: (i, 0)
    tab = lambda i: (i % per_seq, 0)
    tab_t = lambda i: (0, i % per_seq)
    tile_map = lambda i: (i // per_seq, i % per_seq, 0, 0)
    bf = lambda w: jax.ShapeDtypeStruct((n_rows, w), BF16)
    tiles = lambda w: jax.ShapeDtypeStruct((batch, seq // ATTN_TILE, w, ATTN_TILE), BF16)
    tile_spec = lambda w: pl.BlockSpec((None, nblk, w, ATTN_TILE), tile_map)
    return pl.pallas_call(
        _inproj_kernel,
        grid=(steps,),
        in_specs=[
            pl.BlockSpec((tm, D_MODEL), row),
            _const_spec(g_attn.shape), _const_spec(wk.shape), _const_spec(wqvt.shape),
            _const_spec(wc.shape), _const_spec(qn.shape), _const_spec(kvn.shape),
            _const_spec(wuqt.shape), _const_spec(wuk.shape),
            _const_spec(wuvt.shape),
            pl.BlockSpec((LANES, tm), tab_t), pl.BlockSpec((LANES, tm), tab_t),
            pl.BlockSpec((tm, LANES), tab), pl.BlockSpec((tm, LANES), tab),
        ],
        out_specs=[
            tile_spec(A_WIDTH), pl.BlockSpec((tm, A_WIDTH), row), tile_spec(A_WIDTH),
            pl.BlockSpec((1, nblk, A_WIDTH), lambda i: (i, 0, 0)),
            tile_spec(B_HEADS * MLA_HEAD_PAD),
            pl.BlockSpec((tm, B_HEADS * MLA_HEAD_PAD), row),
            tile_spec(B_WIDTH),
        ],
        out_shape=[
            tiles(A_WIDTH), bf(A_WIDTH), tiles(A_WIDTH),
            jax.ShapeDtypeStruct((steps, nblk, A_WIDTH), F32),
            tiles(B_HEADS * MLA_HEAD_PAD), bf(B_HEADS * MLA_HEAD_PAD), tiles(B_WIDTH),
        ],
        compiler_params=pltpu.CompilerParams(
            dimension_semantics=("arbitrary",), vmem_limit_bytes=VMEM_LIMIT),
        name="in_proj",
    )(x2, g_attn, wk, wqvt, wc, qn, kvn, wuqt, wuk, wuvt, cq_tt, sq_tt, ck_t, sk_t)


def _store_scores(buf, h, parts):
    s_ref, max_ref = buf
    row = 0
    for idx, s_t in enumerate(parts):
        s_ref[h, row:row + s_t.shape[0]] = s_t
        max_ref[h, idx] = jnp.max(s_t, axis=0, keepdims=True)
        row += s_t.shape[0]


def _softmax_step(buf, h, parts, m_ref, acc_ref, first):
    s_ref, max_ref = buf
    maxes = [max_ref[h, idx] for idx in range(len(parts))]
    m_new = None if first else m_ref[...]
    for s_max, (_, _, col_keep) in zip(maxes, parts):
        if col_keep is not None:
            s_max = jnp.where(col_keep > 0, s_max, NEG)
        m_new = s_max if m_new is None else jnp.maximum(m_new, s_max)
    pv = None
    row = 0
    for s_max, (rows, v_t, col_keep) in zip(maxes, parts):
        ones = (lax.broadcasted_iota(jnp.int32, (SUM_ROWS, rows), 0) == 0).astype(BF16)
        v_aug = jnp.concatenate([v_t, ones], axis=0)
        m_exp = m_new if col_keep is None else jnp.maximum(m_new, s_max)
        p_t = jnp.exp2(s_ref[h, row:row + rows] - m_exp).astype(BF16)
        part = jnp.dot(v_aug, p_t, preferred_element_type=F32)
        if col_keep is not None:
            part = part * col_keep
        pv = part if pv is None else pv + part
        row += rows
    if first:
        acc_ref[...] = pv
    else:
        acc_ref[...] = jnp.exp2(m_ref[...] - m_new) * acc_ref[...] + pv
    m_ref[...] = m_new


def _pipeline(n_items, score, attend, bufs, unroll):
    if n_items == 0:
        return
    score(0, bufs[0])
    n_loops = (n_items - 1) // unroll

    def body(p, carry):
        for t in range(unroll):
            n = unroll * p + t
            score(n + 1, bufs[(t + 1) % 2])
            attend(n, bufs[t % 2])
        return carry

    lax.fori_loop(0, n_loops, body, 0)
    done = unroll * n_loops
    for n in range(done, n_items):
        if n + 1 < n_items:
            score(n + 1, bufs[(n + 1 - done) % 2])
        attend(n, bufs[(n - done) % 2])


def _tile_rows(j, span=1):
    return pl.ds(pl.multiple_of(j * ATTN_TILE, ATTN_TILE), span * ATTN_TILE)


def _write_heads(acc_ref, o_ref, nq, head_dim):
    def one(qi, carry):
        for h in range(HEAD_GROUP):
            acc = acc_ref[qi, h]
            o_ref[qi, h * head_dim:(h + 1) * head_dim, :] = (
                acc[:head_dim] / acc[head_dim:head_dim + 1]).astype(o_ref.dtype)
        return carry

    lax.fori_loop(0, nq, one, 0)


def _causal_items(nq, n_near, pair_far):
    phases = [[(i, i - d) for i in range(d, nq)] for d in range(n_near)]
    pairs, singles = [], []
    for i in range(n_near, nq):
        n_far = i - n_near + 1
        n_paired = n_far - n_far % 2 if pair_far else 0
        pairs += [(i, j) for j in range(0, n_paired, 2)]
        singles += [(i, j) for j in range(n_paired, n_far)]
    phases += [pairs, singles]
    flat = [it for ph in phases for it in ph]
    tq = jnp.asarray([it[0] for it in flat], jnp.int32)
    tk = jnp.asarray([it[1] for it in flat], jnp.int32)
    return tq, tk, [len(ph) for ph in phases]


def _moba_kernel(tq_ref, tk_ref, qt_ref, k_ref, vt_ref, kmean_ref, bias_ref, o_ref,
                 keep_ref, s0_ref, s1_ref, t0_ref, t1_ref, m_ref, acc_ref, *, seq, phases):
    nblk = seq // MOBA_BLOCK

    def head_query(qi, h):
        pr, sub = divmod(h, 2)
        q_h = qt_ref[qi, pl.ds(pr * LANES + sub * A_HEAD_DIM, A_HEAD_DIM), :]
        zeros = jnp.zeros_like(q_h)
        return jnp.concatenate([q_h, zeros] if sub == 0 else [zeros, q_h], axis=0)

    def select_blocks(i):
        row = lax.broadcasted_iota(jnp.int32, (nblk, ATTN_TILE), 0)
        rowf = row.astype(F32)
        past = row < i
        for h in range(HEAD_GROUP):
            kmean = kmean_ref[:, (h // 2) * LANES:(h // 2 + 1) * LANES].astype(BF16)
            gate = jnp.dot(kmean, head_query(i, h), preferred_element_type=F32)
            gate = jnp.where(past, gate, -jnp.inf)
            picked = jnp.zeros((nblk, ATTN_TILE), F32)
            for _ in range(MOBA_TOPK):
                top = jnp.max(gate, axis=0, keepdims=True)
                first = jnp.min(jnp.where(gate == top, rowf, float(nblk)), axis=0, keepdims=True)
                pick = rowf == first
                picked = jnp.where(pick, 1.0, picked)
                gate = jnp.where(pick, -jnp.inf, gate)
            keep_ref[i, h] = jnp.where(past, picked, 0.0)

    def stage(offset, span, bias_cols, first):
        def score(n, buf):
            qi, kj = tq_ref[offset + n], tk_ref[offset + n]
            for h in range(HEAD_GROUP):
                k_t = k_ref[_tile_rows(kj, span), (h // 2) * LANES:(h // 2 + 1) * LANES]
                s_t = jnp.dot(k_t, head_query(qi, h), preferred_element_type=F32)
                if bias_cols is not None:
                    s_t = s_t + bias_ref[h, :, bias_cols]
                _store_scores(buf, h, [s_t[t * MOBA_BLOCK:(t + 1) * MOBA_BLOCK]
                                       for t in range(span)])

        def attend(n, buf):
            qi, kj = tq_ref[offset + n], tk_ref[offset + n]
            for h in range(HEAD_GROUP):
                parts = [(MOBA_BLOCK, vt_ref[kj + t, h * A_HEAD_DIM:(h + 1) * A_HEAD_DIM, :],
                          None if first else keep_ref[qi, h, pl.ds(kj + t, 1), :])
                         for t in range(span)]
                _softmax_step(buf, h, parts, m_ref.at[qi, h], acc_ref.at[qi, h], first)
            if first:
                select_blocks(qi)

        return score, attend

    bufs = ((s0_ref, t0_ref), (s1_ref, t1_ref))
    n_own, n_prev, n_pairs, n_single = phases
    _pipeline(n_own, *stage(0, 1, slice(0, MOBA_BLOCK), True), bufs, PIPELINE_UNROLL)
    offset = n_own
    _pipeline(n_prev, *stage(offset, 1, slice(MOBA_BLOCK, 2 * MOBA_BLOCK), False), bufs,
              PIPELINE_UNROLL)
    offset += n_prev
    _pipeline(n_pairs, *stage(offset, 2, None, False), bufs, PIPELINE_UNROLL // 2)
    offset += n_pairs
    _pipeline(n_single, *stage(offset, 1, None, False), bufs, PIPELINE_UNROLL)
    _write_heads(acc_ref, o_ref, nblk, A_HEAD_DIM)


def _moba(qat, ka, vat, kmean, bias):
    b, seq, _ = ka.shape
    nq = seq // ATTN_TILE
    gw = HEAD_GROUP * A_HEAD_DIM
    tq, tk, phases = _causal_items(nq, 2, pair_far=True)
    per_group = lambda bi, g, *_: (bi, 0, g)
    tiles = lambda bi, g, *_: (bi, 0, g, 0)
    return pl.pallas_call(
        functools.partial(_moba_kernel, seq=seq, phases=phases),
        grid_spec=pltpu.PrefetchScalarGridSpec(
            num_scalar_prefetch=2,
            grid=(b, A_HEADS // HEAD_GROUP),
            in_specs=[
                pl.BlockSpec((None, nq, gw, ATTN_TILE), tiles),
                pl.BlockSpec((None, seq, gw), per_group),
                pl.BlockSpec((None, nq, gw, MOBA_BLOCK), tiles),
                pl.BlockSpec((None, nq, gw), per_group),
                pl.BlockSpec((HEAD_GROUP, MOBA_BLOCK, 2 * MOBA_BLOCK), lambda bi, g, *_: (g, 0, 0)),
            ],
            out_specs=pl.BlockSpec((None, nq, gw, ATTN_TILE), tiles),
            scratch_shapes=[
                pltpu.VMEM((nq, HEAD_GROUP, nq, ATTN_TILE), F32),
                pltpu.VMEM((HEAD_GROUP, ITEM_SPAN * ATTN_TILE, ATTN_TILE), F32),
                pltpu.VMEM((HEAD_GROUP, ITEM_SPAN * ATTN_TILE, ATTN_TILE), F32),
                pltpu.VMEM((HEAD_GROUP, ITEM_SPAN, 1, ATTN_TILE), F32),
                pltpu.VMEM((HEAD_GROUP, ITEM_SPAN, 1, ATTN_TILE), F32),
                pltpu.VMEM((nq, HEAD_GROUP, 1, ATTN_TILE), F32),
                pltpu.VMEM((nq, HEAD_GROUP, A_HEAD_DIM + SUM_ROWS, ATTN_TILE), F32),
            ]),
        out_shape=jax.ShapeDtypeStruct((b, nq, A_WIDTH, ATTN_TILE), BF16),
        compiler_params=pltpu.CompilerParams(
            dimension_semantics=("arbitrary", "arbitrary"), vmem_limit_bytes=VMEM_LIMIT),
        name="moba_attention",
    )(tq, tk, qat, ka, vat, kmean, bias)


def _mla_kernel(tq_ref, tk_ref, qt_ref, k_ref, vt_ref, o_ref, s0_ref, s1_ref, t0_ref, t1_ref,
                m_ref, acc_ref, *, nq, phases):
    def stage(offset, span, diagonal):
        def score(n, buf):
            qi, kj = tq_ref[offset + n], tk_ref[offset + n]
            for h in range(HEAD_GROUP):
                sl = slice(h * MLA_HEAD_PAD, (h + 1) * MLA_HEAD_PAD)
                s_t = jnp.dot(k_ref[_tile_rows(kj, span), sl], qt_ref[qi, sl, :],
                              preferred_element_type=F32)
                if diagonal:
                    key = lax.broadcasted_iota(jnp.int32, (ATTN_TILE, ATTN_TILE), 0)
                    qry = lax.broadcasted_iota(jnp.int32, (ATTN_TILE, ATTN_TILE), 1)
                    s_t = jnp.where(key <= qry, s_t, NEG)
                _store_scores(buf, h, [s_t[t * ATTN_TILE:(t + 1) * ATTN_TILE]
                                       for t in range(span)])

        def attend(n, buf):
            qi, kj = tq_ref[offset + n], tk_ref[offset + n]
            for h in range(HEAD_GROUP):
                parts = [(ATTN_TILE, vt_ref[kj + t, h * V_HEAD:(h + 1) * V_HEAD, :], None)
                         for t in range(span)]
                _softmax_step(buf, h, parts, m_ref.at[qi, h], acc_ref.at[qi, h], diagonal)

        return score, attend

    bufs = ((s0_ref, t0_ref), (s1_ref, t1_ref))
    n_diag, n_pairs, n_single = phases
    _pipeline(n_diag, *stage(0, 1, True), bufs, PIPELINE_UNROLL)
    _pipeline(n_pairs, *stage(n_diag, 2, False), bufs, PIPELINE_UNROLL // 2)
    _pipeline(n_single, *stage(n_diag + n_pairs, 1, False), bufs, PIPELINE_UNROLL)
    _write_heads(acc_ref, o_ref, nq, V_HEAD)


def _mla(qmt, km, vbt):
    b, seq, _ = km.shape
    nq = seq // ATTN_TILE
    qw = HEAD_GROUP * MLA_HEAD_PAD
    vw = HEAD_GROUP * V_HEAD
    tq, tk, phases = _causal_items(nq, 1, pair_far=False)
    per_group = lambda bi, g, *_: (bi, 0, g)
    tiles = lambda bi, g, *_: (bi, 0, g, 0)
    return pl.pallas_call(
        functools.partial(_mla_kernel, nq=nq, phases=phases),
        grid_spec=pltpu.PrefetchScalarGridSpec(
            num_scalar_prefetch=2,
            grid=(b, B_HEADS // HEAD_GROUP),
            in_specs=[
                pl.BlockSpec((None, nq, qw, ATTN_TILE), tiles),
                pl.BlockSpec((None, seq, qw), per_group),
                pl.BlockSpec((None, nq, vw, ATTN_TILE), tiles),
            ],
            out_specs=pl.BlockSpec((None, nq, vw, ATTN_TILE), tiles),
            scratch_shapes=[
                pltpu.VMEM((HEAD_GROUP, ITEM_SPAN * ATTN_TILE, ATTN_TILE), F32),
                pltpu.VMEM((HEAD_GROUP, ITEM_SPAN * ATTN_TILE, ATTN_TILE), F32),
                pltpu.VMEM((HEAD_GROUP, ITEM_SPAN, 1, ATTN_TILE), F32),
                pltpu.VMEM((HEAD_GROUP, ITEM_SPAN, 1, ATTN_TILE), F32),
                pltpu.VMEM((nq, HEAD_GROUP, 1, ATTN_TILE), F32),
                pltpu.VMEM((nq, HEAD_GROUP, V_HEAD + SUM_ROWS, ATTN_TILE), F32),
            ]),
        out_shape=jax.ShapeDtypeStruct((b, nq, B_WIDTH, ATTN_TILE), BF16),
        compiler_params=pltpu.CompilerParams(
            dimension_semantics=("arbitrary", "arbitrary"), vmem_limit_bytes=VMEM_LIMIT),
        name="mla_attention",
    )(tq, tk, qmt, km, vbt)


def _out_kernel(x_ref, oa_ref, ob_ref, gattn_ref, wg_ref, wpa_ref, wpb_ref, wout_ref,
                gmlp_ref, wup_ref, wdn_ref, gfin_ref, o_ref):
    x = x_ref[...]
    n = _rms(x, gattn_ref[...]).astype(BF16)
    gates = jnp.dot(n, wg_ref[...], preferred_element_type=F32)
    tn_dims = (((0,), (0,)), ((), ()))
    tiles = range(ROW_TILE // ATTN_TILE)
    pa = jnp.concatenate([lax.dot_general(oa_ref[t], wpa_ref[...], tn_dims,
                                          preferred_element_type=F32) for t in tiles], axis=0)
    pb = jnp.concatenate([lax.dot_general(ob_ref[t], wpb_ref[...], tn_dims,
                                          preferred_element_type=F32) for t in tiles], axis=0)
    merged = (jax.nn.sigmoid(gates[:, :D_MODEL]) * pa
              + jax.nn.sigmoid(gates[:, D_MODEL:]) * pb).astype(BF16)
    h = x + jnp.dot(merged, wout_ref[...], preferred_element_type=F32)
    m = _rms(h, gmlp_ref[...]).astype(BF16)
    for c in range(D_FF // FF_CHUNK):
        cols = slice(c * FF_CHUNK, (c + 1) * FF_CHUNK)
        up = jnp.dot(m, wup_ref[:, cols], preferred_element_type=F32)
        act = jnp.square(jnp.maximum(up, 0.0)).astype(BF16)
        h = h + jnp.dot(act, wdn_ref[cols, :], preferred_element_type=F32)
    o_ref[...] = _rms(h, gfin_ref[...])


def _out_block(x2, oa, ob, g_attn, wg, wpa, wpb, wout, g_mlp, wup, wdn, g_fin):
    n_rows = x2.shape[0]
    tm = ROW_TILE
    row = lambda i: (i, 0)
    return pl.pallas_call(
        _out_kernel,
        grid=(n_rows // tm,),
        in_specs=[
            pl.BlockSpec((tm, D_MODEL), row),
            pl.BlockSpec((tm // ATTN_TILE, A_WIDTH, ATTN_TILE), lambda i: (i, 0, 0)),
            pl.BlockSpec((tm // ATTN_TILE, B_WIDTH, ATTN_TILE), lambda i: (i, 0, 0)),
            _const_spec(g_attn.shape), _const_spec(wg.shape), _const_spec(wpa.shape),
            _const_spec(wpb.shape), _const_spec(wout.shape), _const_spec(g_mlp.shape),
            _const_spec(wup.shape), _const_spec(wdn.shape), _const_spec(g_fin.shape),
        ],
        out_specs=pl.BlockSpec((tm, D_MODEL), row),
        out_shape=jax.ShapeDtypeStruct((n_rows, D_MODEL), F32),
        compiler_params=pltpu.CompilerParams(
            dimension_semantics=("arbitrary",), vmem_limit_bytes=VMEM_LIMIT),
        name="out_mlp",
    )(x2, oa, ob, g_attn, wg, wpa, wpb, wout, g_mlp, wup, wdn, g_fin)


def _rope_tables(seq):
    half = QK_ROPE // 2
    inv_freq = ROPE_THETA ** (-jnp.arange(half, dtype=F32) / half)
    ang = jnp.arange(seq).astype(F32)[:, None] * inv_freq[None, :]
    cos2 = jnp.tile(jnp.cos(ang), (1, 2))
    sin2 = jnp.tile(jnp.sin(ang), (1, 2))
    pad = jnp.zeros((seq, MLA_HEAD_PAD - QK_NOPE - QK_ROPE), F32)
    cos_t = jnp.concatenate([jnp.ones((seq, QK_NOPE), F32), cos2, pad], axis=1)
    sin_t = jnp.concatenate([jnp.zeros((seq, QK_NOPE), F32), sin2, pad], axis=1)
    return cos_t, sin_t


def _rotate_half_cols(w):
    half = QK_ROPE // 2
    return jnp.concatenate([-w[..., half:], w[..., :half]], axis=-1)


def _pad_heads(w_nope, w_rope):
    k, h, _ = w_nope.shape
    if w_rope is None:
        w_rope = jnp.zeros((k, h, QK_ROPE), w_nope.dtype)
    pad = jnp.zeros((k, h, MLA_HEAD_PAD - QK_NOPE - QK_ROPE), w_nope.dtype)
    return jnp.concatenate([w_nope, w_rope, pad], axis=-1).reshape(k, h * MLA_HEAD_PAD)


def kernel(x, w_in, rel_bias, mla_q_norm, w_uq, mla_kv_norm, w_uk, w_uv, w_proj_a, w_proj_b,
           w_out, norm_attn, norm_mlp, w_mlp_up, w_mlp_down, norm_final):
    b, seq, d = x.shape
    assert d == D_MODEL and seq % ROW_TILE == 0 and seq // MOBA_BLOCK <= LANES
    assert w_in.shape[0] == 1, "single-layer block"
    x2 = x.reshape(b * seq, d)

    w = w_in[0]
    o_k = A_WIDTH
    o_v = 2 * A_WIDTH
    o_cq = 3 * A_WIDTH
    o_ckv = o_cq + Q_LORA
    o_kr = o_ckv + KV_LORA
    o_g = o_kr + QK_ROPE
    wk = w[:, o_k:o_v].astype(BF16)
    wqvt = jnp.concatenate([w[:, :o_k], w[:, o_v:o_cq]], axis=1).T.astype(BF16)
    w_kr = w[:, o_kr:o_g]
    lane_pad = lambda a: jnp.pad(a, ((0, 0), (QK_NOPE, MLA_HEAD_PAD - QK_NOPE - QK_ROPE)))
    wc = jnp.concatenate(
        [w[:, o_cq:o_kr], lane_pad(w_kr), lane_pad(_rotate_half_cols(w_kr))], axis=1).astype(BF16)
    wg = w[:, o_g:].astype(BF16)

    uq = w_uq[0].reshape(Q_LORA, B_HEADS, QK_NOPE + QK_ROPE)
    uq_nope, uq_rope = uq[..., :QK_NOPE], uq[..., QK_NOPE:]
    wuqt = _pad_heads(uq_nope, uq_rope).T.astype(BF16)
    wuk = _pad_heads(w_uk[0].reshape(KV_LORA, B_HEADS, QK_NOPE), None).astype(BF16)
    wuvt = w_uv[0].T.astype(BF16)

    cos_t, sin_t = _rope_tables(seq)
    q_scale = (QK_NOPE + QK_ROPE) ** -0.5 * LOG2E
    row2 = lambda a: a.reshape(1, -1)

    qat, ka, vat, kmean, qmt, km, vbt = _inproj(
        x2, row2(norm_attn[0]), wk, wqvt, wc, row2(mla_q_norm[0]), row2(mla_kv_norm[0]),
        wuqt, wuk, wuvt, (cos_t * q_scale).T, (sin_t * q_scale).T, cos_t, sin_t, b, seq)

    as_seq = lambda a: a.reshape(b, seq, a.shape[-1])
    bias = _bias_tiles(rel_bias)
    oa = _moba(qat, as_seq(ka), vat, kmean.reshape(b, seq // MOBA_BLOCK, A_WIDTH), bias)
    ob = _mla(qmt, as_seq(km), vbt)

    out = _out_block(
        x2, oa.reshape(-1, A_WIDTH, ATTN_TILE), ob.reshape(-1, B_WIDTH, ATTN_TILE),
        row2(norm_attn[0]),
        wg, w_proj_a[0].astype(BF16), w_proj_b[0].astype(BF16), w_out[0].astype(BF16),
        row2(norm_mlp[0]), w_mlp_up[0].astype(BF16), w_mlp_down[0].astype(BF16),
        row2(norm_final))
    return out.reshape(b, seq, d)
```

```python
import functools
import math

import jax
import jax.numpy as jnp
from jax import lax
from jax.experimental import pallas as pl
from jax.experimental.pallas import tpu as pltpu

D_MODEL = 1024
A_HEADS = 8
A_HEAD_DIM = 64
A_WIDTH = A_HEADS * A_HEAD_DIM
MOBA_BLOCK = 256
MOBA_TOPK = 3
REL_BUCKETS = 32
REL_MAX_DIST = 128
B_HEADS = 8
QK_NOPE = 64
QK_ROPE = 32
V_HEAD = 64
B_WIDTH = B_HEADS * V_HEAD
Q_LORA = 384
KV_LORA = 256
ROPE_THETA = 10000.0
D_FF = 4 * D_MODEL
EPS = 1e-6
NEG = -1e30
LOG2E = math.log2(math.e)

LANES = 128
MLA_HEAD_PAD = 128
ATTN_TILE = 256
HEAD_GROUP = 4
SUM_ROWS = 16
ITEM_SPAN = 2
PIPELINE_UNROLL = 8
ROW_TILE = 512
FF_CHUNK = 1024
VMEM_LIMIT = 56 * 1024 * 1024

F32 = jnp.float32
BF16 = jnp.bfloat16
TN_T_DIMS = (((0,), (1,)), ((), ()))


def _rms(xf, g):
    y = xf * lax.rsqrt(jnp.mean(xf * xf, axis=-1, keepdims=True) + EPS)
    return y * g


def _const_spec(shape):
    zeros = (0,) * len(shape)
    return pl.BlockSpec(shape, lambda *_: zeros, pipeline_mode=pl.Buffered(1))


def _bias_kernel(rel_ref, o_ref):
    h = pl.program_id(0)
    shape = (MOBA_BLOCK, 2 * MOBA_BLOCK)
    r = lax.broadcasted_iota(jnp.int32, shape, 0)
    c = lax.broadcasted_iota(jnp.int32, shape, 1)
    d = c - r
    dist = jnp.maximum(d, 0)
    max_exact = REL_BUCKETS // 2
    df = jnp.maximum(dist, 1).astype(F32)
    large = max_exact + (jnp.log(df / max_exact) / math.log(REL_MAX_DIST / max_exact)
                         * (REL_BUCKETS - max_exact)).astype(jnp.int32)
    large = jnp.minimum(large, REL_BUCKETS - 1)
    bucket = jnp.where(dist < max_exact, dist, large)
    val = jnp.zeros(shape, F32)
    for b in range(REL_BUCKETS):
        val = jnp.where(bucket == b, rel_ref[b, h], val)
    o_ref[...] = jnp.where(d >= 0, (val - rel_ref[REL_BUCKETS - 1, h]) * LOG2E, NEG)


def _bias_tiles(rel_bias):
    return pl.pallas_call(
        _bias_kernel,
        grid=(A_HEADS,),
        in_specs=[pl.BlockSpec(memory_space=pltpu.SMEM)],
        out_specs=pl.BlockSpec((None, MOBA_BLOCK, 2 * MOBA_BLOCK), lambda h: (h, 0, 0)),
        out_shape=jax.ShapeDtypeStruct((A_HEADS, MOBA_BLOCK, 2 * MOBA_BLOCK), F32),
        name="moba_bias_tiles",
    )(rel_bias)


def _inproj_kernel(x_ref, g_ref, wk_ref, wqv_ref, wc_ref, qn_ref, kvn_ref, wuq_ref,
                   wuk_ref, wuv_ref, cqt_ref, sqt_ref, ck_ref, sk_ref,
                   qat_ref, ka_ref, vat_ref, kmean_ref, qmt_ref, km_ref, vbt_ref):
    n = _rms(x_ref[...], g_ref[...]).astype(BF16)
    k = jnp.dot(n, wk_ref[...], preferred_element_type=F32)
    ka_ref[...] = k.astype(BF16)
    nblk = ROW_TILE // ATTN_TILE
    kmean_ref[0] = jnp.sum(k.reshape(nblk, MOBA_BLOCK, A_WIDTH), axis=1) * (1.0 / MOBA_BLOCK)

    c = jnp.dot(n, wc_ref[...], preferred_element_type=F32)
    cq = _rms(c[:, :Q_LORA], qn_ref[...]).astype(BF16)
    ckv = _rms(c[:, Q_LORA:Q_LORA + KV_LORA], kvn_ref[...]).astype(BF16)
    kr = c[:, Q_LORA + KV_LORA:Q_LORA + KV_LORA + LANES]
    krr = c[:, Q_LORA + KV_LORA + LANES:]
    for t in range(nblk):
        rows = slice(t * ATTN_TILE, (t + 1) * ATTN_TILE)
        qv_t = lax.dot_general(wqv_ref[...], n[rows], TN_T_DIMS,
                               preferred_element_type=F32)
        qat_ref[t] = (qv_t[:A_WIDTH] * (A_HEAD_DIM ** -0.5 * LOG2E)).astype(BF16)
        vat_ref[t] = qv_t[A_WIDTH:].astype(BF16)
        vbt_ref[t] = lax.dot_general(wuv_ref[...], ckv[rows], TN_T_DIMS,
                                     preferred_element_type=F32).astype(BF16)
        q_t = lax.dot_general(wuq_ref[...], cq[rows], TN_T_DIMS,
                              preferred_element_type=F32)
        cq_t, sq_t = cqt_ref[:, rows], sqt_ref[:, rows]
        half = QK_ROPE // 2
        for h in range(B_HEADS):
            q_h = q_t[h * MLA_HEAD_PAD:(h + 1) * MLA_HEAD_PAD]
            x1 = q_h[QK_NOPE:QK_NOPE + half]
            x2 = q_h[QK_NOPE + half:QK_NOPE + QK_ROPE]
            rot_h = jnp.concatenate(
                [jnp.zeros((QK_NOPE, ATTN_TILE), F32), -x2, x1,
                 jnp.zeros((MLA_HEAD_PAD - QK_NOPE - QK_ROPE, ATTN_TILE), F32)], axis=0)
            qmt_ref[t, h * MLA_HEAD_PAD:(h + 1) * MLA_HEAD_PAD, :] = (
                q_h * cq_t + rot_h * sq_t).astype(BF16)
    kn = jnp.dot(ckv, wuk_ref[...], preferred_element_type=F32)
    k_rope = kr * ck_ref[...] + krr * sk_ref[...]
    for h in range(B_HEADS):
        sl = slice(h * MLA_HEAD_PAD, (h + 1) * MLA_HEAD_PAD)
        km_ref[:, sl] = (kn[:, sl] + k_rope).astype(BF16)


def _inproj(x2, g_attn, wk, wqv, wc, qn, kvn, wuq, wuk, wuv, cq_tt, sq_tt, ck_t, sk_t,
            batch, seq):
    n_rows = x2.shape[0]
    tm = ROW_TILE
    steps = n_rows // tm
    per_seq = seq // tm
    nblk = tm // ATTN_TILE
    row = lambda i: (i, 0)
    tab = lambda i: (i % per_seq, 0)
    tab_t = lambda i: (0, i % per_seq)
    tile_map = lambda i: (i // per_seq, i % per_seq, 0, 0)
    bf = lambda w: jax.ShapeDtypeStruct((n_rows, w), BF16)
    tiles = lambda w: jax.ShapeDtypeStruct((batch, seq // ATTN_TILE, w, ATTN_TILE), BF16)
    tile_spec = lambda w: pl.BlockSpec((None, nblk, w, ATTN_TILE), tile_map)
    return pl.pallas_call(
        _inproj_kernel,
        grid=(steps,),
        in_specs=[
            pl.BlockSpec((tm, D_MODEL), row),
            _const_spec(g_attn.shape), _const_spec(wk.shape), _const_spec(wqv.shape),
            _const_spec(wc.shape), _const_spec(qn.shape), _const_spec(kvn.shape),
            _const_spec(wuq.shape), _const_spec(wuk.shape),
            _const_spec(wuv.shape),
            pl.BlockSpec((LANES, tm), tab_t), pl.BlockSpec((LANES, tm), tab_t),
            pl.BlockSpec((tm, LANES), tab), pl.BlockSpec((tm, LANES), tab),
        ],
        out_specs=[
            tile_spec(A_WIDTH), pl.BlockSpec((tm, A_WIDTH), row), tile_spec(A_WIDTH),
            pl.BlockSpec((1, nblk, A_WIDTH), lambda i: (i, 0, 0)),
            tile_spec(B_HEADS * MLA_HEAD_PAD),
            pl.BlockSpec((tm, B_HEADS * MLA_HEAD_PAD), row),
            tile_spec(B_WIDTH),
        ],
        out_shape=[
            tiles(A_WIDTH), bf(A_WIDTH), tiles(A_WIDTH),
            jax.ShapeDtypeStruct((steps, nblk, A_WIDTH), F32),
            tiles(B_HEADS * MLA_HEAD_PAD), bf(B_HEADS * MLA_HEAD_PAD), tiles(B_WIDTH),
        ],
        compiler_params=pltpu.CompilerParams(
            dimension_semantics=("arbitrary",), vmem_limit_bytes=VMEM_LIMIT),
        name="in_proj",
    )(x2, g_attn, wk, wqv, wc, qn, kvn, wuq, wuk, wuv, cq_tt, sq_tt, ck_t, sk_t)


def _store_scores(buf, h, parts):
    s_ref, max_ref = buf
    row = 0
    for idx, s_t in enumerate(parts):
        s_ref[h, row:row + s_t.shape[0]] = s_t
        max_ref[h, idx] = jnp.max(s_t, axis=0, keepdims=True)
        row += s_t.shape[0]


def _softmax_step(buf, h, parts, m_ref, acc_ref, first):
    s_ref, max_ref = buf
    maxes = [max_ref[h, idx] for idx in range(len(parts))]
    m_new = None if first else m_ref[...]
    for s_max, (_, _, col_keep) in zip(maxes, parts):
        if col_keep is not None:
            s_max = jnp.where(col_keep > 0, s_max, NEG)
        m_new = s_max if m_new is None else jnp.maximum(m_new, s_max)
    pv = None
    row = 0
    for s_max, (rows, v_t, col_keep) in zip(maxes, parts):
        ones = (lax.broadcasted_iota(jnp.int32, (SUM_ROWS, rows), 0) == 0).astype(BF16)
        v_aug = jnp.concatenate([v_t, ones], axis=0)
        m_exp = m_new if col_keep is None else jnp.maximum(m_new, s_max)
        p_t = jnp.exp2(s_ref[h, row:row + rows] - m_exp).astype(BF16)
        part = jnp.dot(v_aug, p_t, preferred_element_type=F32)
        if col_keep is not None:
            part = part * col_keep
        pv = part if pv is None else pv + part
        row += rows
    if first:
        acc_ref[...] = pv
    else:
        acc_ref[...] = jnp.exp2(m_ref[...] - m_new) * acc_ref[...] + pv
    m_ref[...] = m_new


def _pipeline(n_items, score, attend, bufs, unroll):
    if n_items == 0:
        return
    score(0, bufs[0])
    n_loops = (n_items - 1) // unroll

    def body(p, carry):
        for t in range(unroll):
            n = unroll * p + t
            score(n + 1, bufs[(t + 1) % 2])
            attend(n, bufs[t % 2])
        return carry

    lax.fori_loop(0, n_loops, body, 0)
    done = unroll * n_loops
    for n in range(done, n_items):
        if n + 1 < n_items:
            score(n + 1, bufs[(n + 1 - done) % 2])
        attend(n, bufs[(n - done) % 2])


def _tile_rows(j, span=1):
    return pl.ds(pl.multiple_of(j * ATTN_TILE, ATTN_TILE), span * ATTN_TILE)


def _write_heads(acc_ref, o_ref, nq, head_dim):
    def one(qi, carry):
        for h in range(HEAD_GROUP):
            acc = acc_ref[qi, h]
            o_ref[qi, h * head_dim:(h + 1) * head_dim, :] = (
                acc[:head_dim] / acc[head_dim:head_dim + 1]).astype(o_ref.dtype)
        return carry

    lax.fori_loop(0, nq, one, 0)


def _causal_items(nq, n_near, pair_far):
    phases = [[(i, i - d) for i in range(d, nq)] for d in range(n_near)]
    pairs, singles = [], []
    for i in range(n_near, nq):
        n_far = i - n_near + 1
        n_paired = n_far - n_far % 2 if pair_far else 0
        pairs += [(i, j) for j in range(0, n_paired, 2)]
        singles += [(i, j) for j in range(n_paired, n_far)]
    phases += [pairs, singles]
    flat = [it for ph in phases for it in ph]
    tq = jnp.asarray([it[0] for it in flat], jnp.int32)
    tk = jnp.asarray([it[1] for it in flat], jnp.int32)
    return tq, tk, [len(ph) for ph in phases]


def _moba_kernel(tq_ref, tk_ref, qt_ref, k_ref, vt_ref, kmean_ref, bias_ref, o_ref,
                 keep_ref, s0_ref, s1_ref, t0_ref, t1_ref, m_ref, acc_ref, *, seq, phases):
    nblk = seq // MOBA_BLOCK

    def head_query(qi, h):
        pr, sub = divmod(h, 2)
        q_h = qt_ref[qi, pl.ds(pr * LANES + sub * A_HEAD_DIM, A_HEAD_DIM), :]
        zeros = jnp.zeros_like(q_h)
        return jnp.concatenate([q_h, zeros] if sub == 0 else [zeros, q_h], axis=0)

    def select_blocks(i):
        row = lax.broadcasted_iota(jnp.int32, (nblk, ATTN_TILE), 0)
        rowf = row.astype(F32)
        past = row < i
        for h in range(HEAD_GROUP):
            kmean = kmean_ref[:, (h // 2) * LANES:(h // 2 + 1) * LANES].astype(BF16)
            gate = jnp.dot(kmean, head_query(i, h), preferred_element_type=F32)
            gate = jnp.where(past, gate, -jnp.inf)
            picked = jnp.zeros((nblk, ATTN_TILE), F32)
            for _ in range(MOBA_TOPK):
                top = jnp.max(gate, axis=0, keepdims=True)
                first = jnp.min(jnp.where(gate == top, rowf, float(nblk)), axis=0, keepdims=True)
                pick = rowf == first
                picked = jnp.where(pick, 1.0, picked)
                gate = jnp.where(pick, -jnp.inf, gate)
            keep_ref[i, h] = jnp.where(past, picked, 0.0)

    def stage(offset, span, bias_cols, first):
        def score(n, buf):
            qi, kj = tq_ref[offset + n], tk_ref[offset + n]
            for h in range(HEAD_GROUP):
                k_t = k_ref[_tile_rows(kj, span), (h // 2) * LANES:(h // 2 + 1) * LANES]
                s_t = jnp.dot(k_t, head_query(qi, h), preferred_element_type=F32)
                if bias_cols is not None:
                    s_t = s_t + bias_ref[h, :, bias_cols]
                _store_scores(buf, h, [s_t[t * MOBA_BLOCK:(t + 1) * MOBA_BLOCK]
                                       for t in range(span)])

        def attend(n, buf):
            qi, kj = tq_ref[offset + n], tk_ref[offset + n]
            for h in range(HEAD_GROUP):
                parts = [(MOBA_BLOCK, vt_ref[kj + t, h * A_HEAD_DIM:(h + 1) * A_HEAD_DIM, :],
                          None if first else keep_ref[qi, h, pl.ds(kj + t, 1), :])
                         for t in range(span)]
                _softmax_step(buf, h, parts, m_ref.at[qi, h], acc_ref.at[qi, h], first)
            if first:
                select_blocks(qi)

        return score, attend

    bufs = ((s0_ref, t0_ref), (s1_ref, t1_ref))
    n_own, n_prev, n_pairs, n_single = phases
    _pipeline(n_own, *stage(0, 1, slice(0, MOBA_BLOCK), True), bufs, PIPELINE_UNROLL)
    offset = n_own
    _pipeline(n_prev, *stage(offset, 1, slice(MOBA_BLOCK, 2 * MOBA_BLOCK), False), bufs,
              PIPELINE_UNROLL)
    offset += n_prev
    _pipeline(n_pairs, *stage(offset, 2, None, False), bufs, PIPELINE_UNROLL // 2)
    offset += n_pairs
    _pipeline(n_single, *stage(offset, 1, None, False), bufs, PIPELINE_UNROLL)
    _write_heads(acc_ref, o_ref, nblk, A_HEAD_DIM)


def _moba(qat, ka, vat, kmean, bias):
    b, seq, _ = ka.shape
    nq = seq // ATTN_TILE
    gw = HEAD_GROUP * A_HEAD_DIM
    tq, tk, phases = _causal_items(nq, 2, pair_far=True)
    per_group = lambda bi, g, *_: (bi, 0, g)
    tiles = lambda bi, g, *_: (bi, 0, g, 0)
    return pl.pallas_call(
        functools.partial(_moba_kernel, seq=seq, phases=phases),
        grid_spec=pltpu.PrefetchScalarGridSpec(
            num_scalar_prefetch=2,
            grid=(b, A_HEADS // HEAD_GROUP),
            in_specs=[
                pl.BlockSpec((None, nq, gw, ATTN_TILE), tiles),
                pl.BlockSpec((None, seq, gw), per_group),
                pl.BlockSpec((None, nq, gw, MOBA_BLOCK), tiles),
                pl.BlockSpec((None, nq, gw), per_group),
                pl.BlockSpec((HEAD_GROUP, MOBA_BLOCK, 2 * MOBA_BLOCK), lambda bi, g, *_: (g, 0, 0)),
            ],
            out_specs=pl.BlockSpec((None, nq, gw, ATTN_TILE), tiles),
            scratch_shapes=[
                pltpu.VMEM((nq, HEAD_GROUP, nq, ATTN_TILE), F32),
                pltpu.VMEM((HEAD_GROUP, ITEM_SPAN * ATTN_TILE, ATTN_TILE), F32),
                pltpu.VMEM((HEAD_GROUP, ITEM_SPAN * ATTN_TILE, ATTN_TILE), F32),
                pltpu.VMEM((HEAD_GROUP, ITEM_SPAN, 1, ATTN_TILE), F32),
                pltpu.VMEM((HEAD_GROUP, ITEM_SPAN, 1, ATTN_TILE), F32),
                pltpu.VMEM((nq, HEAD_GROUP, 1, ATTN_TILE), F32),
                pltpu.VMEM((nq, HEAD_GROUP, A_HEAD_DIM + SUM_ROWS, ATTN_TILE), F32),
            ]),
        out_shape=jax.ShapeDtypeStruct((b, nq, A_WIDTH, ATTN_TILE), BF16),
        compiler_params=pltpu.CompilerParams(
            dimension_semantics=("arbitrary", "arbitrary"), vmem_limit_bytes=VMEM_LIMIT),
        name="moba_attention",
    )(tq, tk, qat, ka, vat, kmean, bias)


def _mla_kernel(tq_ref, tk_ref, qt_ref, k_ref, vt_ref, o_ref, s0_ref, s1_ref, t0_ref, t1_ref,
                m_ref, acc_ref, *, nq, phases):
    def stage(offset, span, diagonal):
        def score(n, buf):
            qi, kj = tq_ref[offset + n], tk_ref[offset + n]
            for h in range(HEAD_GROUP):
                sl = slice(h * MLA_HEAD_PAD, (h + 1) * MLA_HEAD_PAD)
                s_t = jnp.dot(k_ref[_tile_rows(kj, span), sl], qt_ref[qi, sl, :],
                              preferred_element_type=F32)
                if diagonal:
                    key = lax.broadcasted_iota(jnp.int32, (ATTN_TILE, ATTN_TILE), 0)
                    qry = lax.broadcasted_iota(jnp.int32, (ATTN_TILE, ATTN_TILE), 1)
                    s_t = jnp.where(key <= qry, s_t, NEG)
                _store_scores(buf, h, [s_t[t * ATTN_TILE:(t + 1) * ATTN_TILE]
                                       for t in range(span)])

        def attend(n, buf):
            qi, kj = tq_ref[offset + n], tk_ref[offset + n]
            for h in range(HEAD_GROUP):
                parts = [(ATTN_TILE, vt_ref[kj + t, h * V_HEAD:(h + 1) * V_HEAD, :], None)
                         for t in range(span)]
                _softmax_step(buf, h, parts, m_ref.at[qi, h], acc_ref.at[qi, h], diagonal)

        return score, attend

    bufs = ((s0_ref, t0_ref), (s1_ref, t1_ref))
    n_diag, n_pairs, n_single = phases
    _pipeline(n_diag, *stage(0, 1, True), bufs, PIPELINE_UNROLL)
    _pipeline(n_pairs, *stage(n_diag, 2, False), bufs, PIPELINE_UNROLL // 2)
    _pipeline(n_single, *stage(n_diag + n_pairs, 1, False), bufs, PIPELINE_UNROLL)
    _write_heads(acc_ref, o_ref, nq, V_HEAD)


def _mla(qmt, km, vbt):
    b, seq, _ = km.shape
    nq = seq // ATTN_TILE
    qw = HEAD_GROUP * MLA_HEAD_PAD
    vw = HEAD_GROUP * V_HEAD
    tq, tk, phases = _causal_items(nq, 1, pair_far=False)
    per_group = lambda bi, g, *_: (bi, 0, g)
    tiles = lambda bi, g, *_: (bi, 0, g, 0)
    return pl.pallas_call(
        functools.partial(_mla_kernel, nq=nq, phases=phases),
        grid_spec=pltpu.PrefetchScalarGridSpec(
            num_scalar_prefetch=2,
            grid=(b, B_HEADS // HEAD_GROUP),
            in_specs=[
                pl.BlockSpec((None, nq, qw, ATTN_TILE), tiles),
                pl.BlockSpec((None, seq, qw), per_group),
                pl.BlockSpec((None, nq, vw, ATTN_TILE), tiles),
            ],
            out_specs=pl.BlockSpec((None, nq, vw, ATTN_TILE), tiles),
            scratch_shapes=[
                pltpu.VMEM((HEAD_GROUP, ITEM_SPAN * ATTN_TILE, ATTN_TILE), F32),
                pltpu.VMEM((HEAD_GROUP, ITEM_SPAN * ATTN_TILE, ATTN_TILE), F32),
                pltpu.VMEM((HEAD_GROUP, ITEM_SPAN, 1, ATTN_TILE), F32),
                pltpu.VMEM((HEAD_GROUP, ITEM_SPAN, 1, ATTN_TILE), F32),
                pltpu.VMEM((nq, HEAD_GROUP, 1, ATTN_TILE), F32),
                pltpu.VMEM((nq, HEAD_GROUP, V_HEAD + SUM_ROWS, ATTN_TILE), F32),
            ]),
        out_shape=jax.ShapeDtypeStruct((b, nq, B_WIDTH, ATTN_TILE), BF16),
        compiler_params=pltpu.CompilerParams(
            dimension_semantics=("arbitrary", "arbitrary"), vmem_limit_bytes=VMEM_LIMIT),
        name="mla_attention",
    )(tq, tk, qmt, km, vbt)


def _out_kernel(x_ref, oa_ref, ob_ref, gattn_ref, wg_ref, wpa_ref, wpb_ref, wout_ref,
                gmlp_ref, wup_ref, wdn_ref, gfin_ref, o_ref):
    x = x_ref[...]
    n = _rms(x, gattn_ref[...]).astype(BF16)
    gates = jnp.dot(n, wg_ref[...], preferred_element_type=F32)
    tn_dims = (((0,), (0,)), ((), ()))
    tiles = range(ROW_TILE // ATTN_TILE)
    pa = jnp.concatenate([lax.dot_general(oa_ref[t], wpa_ref[...], tn_dims,
                                          preferred_element_type=F32) for t in tiles], axis=0)
    pb = jnp.concatenate([lax.dot_general(ob_ref[t], wpb_ref[...], tn_dims,
                                          preferred_element_type=F32) for t in tiles], axis=0)
    merged = (jax.nn.sigmoid(gates[:, :D_MODEL]) * pa
              + jax.nn.sigmoid(gates[:, D_MODEL:]) * pb).astype(BF16)
    h = x + jnp.dot(merged, wout_ref[...], preferred_element_type=F32)
    m = _rms(h, gmlp_ref[...]).astype(BF16)
    for c in range(D_FF // FF_CHUNK):
        cols = slice(c * FF_CHUNK, (c + 1) * FF_CHUNK)
        up = jnp.dot(m, wup_ref[:, cols], preferred_element_type=F32)
        act = jnp.square(jnp.maximum(up, 0.0)).astype(BF16)
        h = h + jnp.dot(act, wdn_ref[cols, :], preferred_element_type=F32)
    o_ref[...] = _rms(h, gfin_ref[...])


def _out_block(x2, oa, ob, g_attn, wg, wpa, wpb, wout, g_mlp, wup, wdn, g_fin):
    n_rows = x2.shape[0]
    tm = ROW_TILE
    row = lambda i: (i, 0)
    return pl.pallas_call(
        _out_kernel,
        grid=(n_rows // tm,),
        in_specs=[
            pl.BlockSpec((tm, D_MODEL), row),
            pl.BlockSpec((tm // ATTN_TILE, A_WIDTH, ATTN_TILE), lambda i: (i, 0, 0)),
            pl.BlockSpec((tm // ATTN_TILE, B_WIDTH, ATTN_TILE), lambda i: (i, 0, 0)),
            _const_spec(g_attn.shape), _const_spec(wg.shape), _const_spec(wpa.shape),
            _const_spec(wpb.shape), _const_spec(wout.shape), _const_spec(g_mlp.shape),
            _const_spec(wup.shape), _const_spec(wdn.shape), _const_spec(g_fin.shape),
        ],
        out_specs=pl.BlockSpec((tm, D_MODEL), row),
        out_shape=jax.ShapeDtypeStruct((n_rows, D_MODEL), F32),
        compiler_params=pltpu.CompilerParams(
            dimension_semantics=("arbitrary",), vmem_limit_bytes=VMEM_LIMIT),
        name="out_mlp",
    )(x2, oa, ob, g_attn, wg, wpa, wpb, wout, g_mlp, wup, wdn, g_fin)


def _rope_tables(seq):
    half = QK_ROPE // 2
    inv_freq = ROPE_THETA ** (-jnp.arange(half, dtype=F32) / half)
    ang = jnp.arange(seq).astype(F32)[:, None] * inv_freq[None, :]
    cos2 = jnp.tile(jnp.cos(ang), (1, 2))
    sin2 = jnp.tile(jnp.sin(ang), (1, 2))
    pad = jnp.zeros((seq, MLA_HEAD_PAD - QK_NOPE - QK_ROPE), F32)
    cos_t = jnp.concatenate([jnp.ones((seq, QK_NOPE), F32), cos2, pad], axis=1)
    sin_t = jnp.concatenate([jnp.zeros((seq, QK_NOPE), F32), sin2, pad], axis=1)
    return cos_t, sin_t


def _rotate_half_cols(w):
    half = QK_ROPE // 2
    return jnp.concatenate([-w[..., half:], w[..., :half]], axis=-1)


def _pad_heads(w_nope, w_rope):
    k, h, _ = w_nope.shape
    if w_rope is None:
        w_rope = jnp.zeros((k, h, QK_ROPE), w_nope.dtype)
    pad = jnp.zeros((k, h, MLA_HEAD_PAD - QK_NOPE - QK_ROPE), w_nope.dtype)
    return jnp.concatenate([w_nope, w_rope, pad], axis=-1).reshape(k, h * MLA_HEAD_PAD)


def kernel(x, w_in, rel_bias, mla_q_norm, w_uq, mla_kv_norm, w_uk, w_uv, w_proj_a, w_proj_b,
           w_out, norm_attn, norm_mlp, w_mlp_up, w_mlp_down, norm_final):
    b, seq, d = x.shape
    assert d == D_MODEL and seq % ROW_TILE == 0 and seq // MOBA_BLOCK <= LANES
    assert w_in.shape[0] == 1, "single-layer block"
    x2 = x.reshape(b * seq, d)

    w = w_in[0]
    o_k = A_WIDTH
    o_v = 2 * A_WIDTH
    o_cq = 3 * A_WIDTH
    o_ckv = o_cq + Q_LORA
    o_kr = o_ckv + KV_LORA
    o_g = o_kr + QK_ROPE
    wk = w[:, o_k:o_v].astype(BF16)
    wqv = jnp.concatenate([w[:, :o_k], w[:, o_v:o_cq]], axis=1).astype(BF16)
    w_kr = w[:, o_kr:o_g]
    lane_pad = lambda a: jnp.pad(a, ((0, 0), (QK_NOPE, MLA_HEAD_PAD - QK_NOPE - QK_ROPE)))
    wc = jnp.concatenate(
        [w[:, o_cq:o_kr], lane_pad(w_kr), lane_pad(_rotate_half_cols(w_kr))], axis=1).astype(BF16)
    wg = w[:, o_g:].astype(BF16)

    uq = w_uq[0].reshape(Q_LORA, B_HEADS, QK_NOPE + QK_ROPE)
    uq_nope, uq_rope = uq[..., :QK_NOPE], uq[..., QK_NOPE:]
    wuq = _pad_heads(uq_nope, uq_rope).astype(BF16)
    wuk = _pad_heads(w_uk[0].reshape(KV_LORA, B_HEADS, QK_NOPE), None).astype(BF16)
    wuv = w_uv[0].astype(BF16)

    cos_t, sin_t = _rope_tables(seq)
    q_scale = (QK_NOPE + QK_ROPE) ** -0.5 * LOG2E
    row2 = lambda a: a.reshape(1, -1)

    qat, ka, vat, kmean, qmt, km, vbt = _inproj(
        x2, row2(norm_attn[0]), wk, wqv, wc, row2(mla_q_norm[0]), row2(mla_kv_norm[0]),
        wuq, wuk, wuv, (cos_t * q_scale).T, (sin_t * q_scale).T, cos_t, sin_t, b, seq)

    as_seq = lambda a: a.reshape(b, seq, a.shape[-1])
    bias = _bias_tiles(rel_bias)
    oa = _moba(qat, as_seq(ka), vat, kmean.reshape(b, seq // MOBA_BLOCK, A_WIDTH), bias)
    ob = _mla(qmt, as_seq(km), vbt)

    out = _out_block(
        x2, oa.reshape(-1, A_WIDTH, ATTN_TILE), ob.reshape(-1, B_WIDTH, ATTN_TILE),
        row2(norm_attn[0]),
        wg, w_proj_a[0].astype(BF16), w_proj_b[0].astype(BF16), w_out[0].astype(BF16),
        row2(norm_mlp[0]), w_mlp_up[0].astype(BF16), w_mlp_down[0].astype(BF16),
        row2(norm_final))
    return out.reshape(b, seq, d)
```

```python
import functools
import math

import jax
import jax.numpy as jnp
from jax import lax
from jax.experimental import pallas as pl
from jax.experimental.pallas import tpu as pltpu

D_MODEL = 1024
A_HEADS = 8
A_HEAD_DIM = 64
A_WIDTH = A_HEADS * A_HEAD_DIM
MOBA_BLOCK = 256
MOBA_TOPK = 3
REL_BUCKETS = 32
REL_MAX_DIST = 128
B_HEADS = 8
QK_NOPE = 64
QK_ROPE = 32
V_HEAD = 64
B_WIDTH = B_HEADS * V_HEAD
Q_LORA = 384
KV_LORA = 256
ROPE_THETA = 10000.0
D_FF = 4 * D_MODEL
EPS = 1e-6
NEG = -1e30
LOG2E = math.log2(math.e)

LANES = 128
MLA_HEAD_PAD = 128
ATTN_TILE = 256
HEAD_GROUP = 4
SUM_ROWS = 16
ITEM_SPAN = 2
PIPELINE_UNROLL = 8
ROW_TILE = 512
FF_CHUNK = 1024
VMEM_LIMIT = 56 * 1024 * 1024

F32 = jnp.float32
BF16 = jnp.bfloat16
NT_DIMS = (((1,), (1,)), ((), ()))


def _rms(xf, g):
    y = xf * lax.rsqrt(jnp.mean(xf * xf, axis=-1, keepdims=True) + EPS)
    return y * g


def _const_spec(shape):
    zeros = (0,) * len(shape)
    return pl.BlockSpec(shape, lambda *_: zeros, pipeline_mode=pl.Buffered(1))


def _bias_kernel(rel_ref, o_ref):
    h = pl.program_id(0)
    shape = (MOBA_BLOCK, 2 * MOBA_BLOCK)
    r = lax.broadcasted_iota(jnp.int32, shape, 0)
    c = lax.broadcasted_iota(jnp.int32, shape, 1)
    d = c - r
    dist = jnp.maximum(d, 0)
    max_exact = REL_BUCKETS // 2
    df = jnp.maximum(dist, 1).astype(F32)
    large = max_exact + (jnp.log(df / max_exact) / math.log(REL_MAX_DIST / max_exact)
                         * (REL_BUCKETS - max_exact)).astype(jnp.int32)
    large = jnp.minimum(large, REL_BUCKETS - 1)
    bucket = jnp.where(dist < max_exact, dist, large)
    val = jnp.zeros(shape, F32)
    for b in range(REL_BUCKETS):
        val = jnp.where(bucket == b, rel_ref[b, h], val)
    o_ref[...] = jnp.where(d >= 0, (val - rel_ref[REL_BUCKETS - 1, h]) * LOG2E, NEG)


def _bias_tiles(rel_bias):
    return pl.pallas_call(
        _bias_kernel,
        grid=(A_HEADS,),
        in_specs=[pl.BlockSpec(memory_space=pltpu.SMEM)],
        out_specs=pl.BlockSpec((None, MOBA_BLOCK, 2 * MOBA_BLOCK), lambda h: (h, 0, 0)),
        out_shape=jax.ShapeDtypeStruct((A_HEADS, MOBA_BLOCK, 2 * MOBA_BLOCK), F32),
        name="moba_bias_tiles",
    )(rel_bias)


def _inproj_kernel(x_ref, g_ref, wk_ref, wqv_ref, wc_ref, qn_ref, kvn_ref, wuq_ref,
                   wuk_ref, wuv_ref, cqt_ref, sqt_ref, ck_ref, sk_ref,
                   qat_ref, ka_ref, vat_ref, kmean_ref, qmt_ref, km_ref, vbt_ref):
    n = _rms(x_ref[...], g_ref[...]).astype(BF16)
    k = lax.dot_general(n, wk_ref[...], NT_DIMS, preferred_element_type=F32)
    ka_ref[...] = k.astype(BF16)
    nblk = ROW_TILE // ATTN_TILE
    kmean_ref[0] = jnp.sum(k.reshape(nblk, MOBA_BLOCK, A_WIDTH), axis=1) * (1.0 / MOBA_BLOCK)

    c = lax.dot_general(n, wc_ref[...], NT_DIMS, preferred_element_type=F32)
    cq = _rms(c[:, :Q_LORA], qn_ref[...]).astype(BF16)
    ckv = _rms(c[:, Q_LORA:Q_LORA + KV_LORA], kvn_ref[...]).astype(BF16)
    kr = c[:, Q_LORA + KV_LORA:Q_LORA + KV_LORA + LANES]
    krr = c[:, Q_LORA + KV_LORA + LANES:]
    for t in range(nblk):
        rows = slice(t * ATTN_TILE, (t + 1) * ATTN_TILE)
        qv_t = lax.dot_general(wqv_ref[...], n[rows], NT_DIMS,
                               preferred_element_type=F32)
        qat_ref[t] = (qv_t[:A_WIDTH] * (A_HEAD_DIM ** -0.5 * LOG2E)).astype(BF16)
        vat_ref[t] = qv_t[A_WIDTH:].astype(BF16)
        vbt_ref[t] = lax.dot_general(wuv_ref[...], ckv[rows], NT_DIMS,
                                     preferred_element_type=F32).astype(BF16)
        q_t = lax.dot_general(wuq_ref[...], cq[rows], NT_DIMS,
                              preferred_element_type=F32)
        cq_t, sq_t = cqt_ref[:, rows], sqt_ref[:, rows]
        half = QK_ROPE // 2
        for h in range(B_HEADS):
            q_h = q_t[h * MLA_HEAD_PAD:(h + 1) * MLA_HEAD_PAD]
            x1 = q_h[QK_NOPE:QK_NOPE + half]
            x2 = q_h[QK_NOPE + half:QK_NOPE + QK_ROPE]
            rot_h = jnp.concatenate(
                [jnp.zeros((QK_NOPE, ATTN_TILE), F32), -x2, x1,
                 jnp.zeros((MLA_HEAD_PAD - QK_NOPE - QK_ROPE, ATTN_TILE), F32)], axis=0)
            qmt_ref[t, h * MLA_HEAD_PAD:(h + 1) * MLA_HEAD_PAD, :] = (
                q_h * cq_t + rot_h * sq_t).astype(BF16)
    kn = jnp.dot(ckv, wuk_ref[...], preferred_element_type=F32)
    k_rope = kr * ck_ref[...] + krr * sk_ref[...]
    for h in range(B_HEADS):
        sl = slice(h * MLA_HEAD_PAD, (h + 1) * MLA_HEAD_PAD)
        km_ref[:, sl] = (kn[:, sl] + k_rope).astype(BF16)


def _inproj(x2, g_attn, wk, wqv, wc, qn, kvn, wuq, wuk, wuv, cq_tt, sq_tt, ck_t, sk_t,
            batch, seq):
    n_rows = x2.shape[0]
    tm = ROW_TILE
    steps = n_rows // tm
    per_seq = seq // tm
    nblk = tm // ATTN_TILE
    row = lambda i: (i, 0)
    tab = lambda i: (i % per_seq, 0)
    tab_t = lambda i: (0, i % per_seq)
    tile_map = lambda i: (i // per_seq, i % per_seq, 0, 0)
    bf = lambda w: jax.ShapeDtypeStruct((n_rows, w), BF16)
    tiles = lambda w: jax.ShapeDtypeStruct((batch, seq // ATTN_TILE, w, ATTN_TILE), BF16)
    tile_spec = lambda w: pl.BlockSpec((None, nblk, w, ATTN_TILE), tile_map)
    return pl.pallas_call(
        _inproj_kernel,
        grid=(steps,),
        in_specs=[
            pl.BlockSpec((tm, D_MODEL), row),
            _const_spec(g_attn.shape), _const_spec(wk.shape), _const_spec(wqv.shape),
            _const_spec(wc.shape), _const_spec(qn.shape), _const_spec(kvn.shape),
            _const_spec(wuq.shape), _const_spec(wuk.shape),
            _const_spec(wuv.shape),
            pl.BlockSpec((LANES, tm), tab_t), pl.BlockSpec((LANES, tm), tab_t),
            pl.BlockSpec((tm, LANES), tab), pl.BlockSpec((tm, LANES), tab),
        ],
        out_specs=[
            tile_spec(A_WIDTH), pl.BlockSpec((tm, A_WIDTH), row), tile_spec(A_WIDTH),
            pl.BlockSpec((1, nblk, A_WIDTH), lambda i: (i, 0, 0)),
            tile_spec(B_HEADS * MLA_HEAD_PAD),
            pl.BlockSpec((tm, B_HEADS * MLA_HEAD_PAD), row),
            tile_spec(B_WIDTH),
        ],
        out_shape=[
            tiles(A_WIDTH), bf(A_WIDTH), tiles(A_WIDTH),
            jax.ShapeDtypeStruct((steps, nblk, A_WIDTH), F32),
            tiles(B_HEADS * MLA_HEAD_PAD), bf(B_HEADS * MLA_HEAD_PAD), tiles(B_WIDTH),
        ],
        compiler_params=pltpu.CompilerParams(
            dimension_semantics=("arbitrary",), vmem_limit_bytes=VMEM_LIMIT),
        name="in_proj",
    )(x2, g_attn, wk, wqv, wc, qn, kvn, wuq, wuk, wuv, cq_tt, sq_tt, ck_t, sk_t)


def _store_scores(buf, h, parts):
    s_ref, max_ref = buf
    row = 0
    for idx, s_t in enumerate(parts):
        s_ref[h, row:row + s_t.shape[0]] = s_t
        max_ref[h, idx] = jnp.max(s_t, axis=0, keepdims=True)
        row += s_t.shape[0]


def _softmax_step(buf, h, parts, m_ref, acc_ref, first):
    s_ref, max_ref = buf
    maxes = [max_ref[h, idx] for idx in range(len(parts))]
    m_new = None if first else m_ref[...]
    for s_max, (_, _, col_keep) in zip(maxes, parts):
        if col_keep is not None:
            s_max = jnp.where(col_keep > 0, s_max, NEG)
        m_new = s_max if m_new is None else jnp.maximum(m_new, s_max)
    pv = None
    row = 0
    for s_max, (rows, v_t, col_keep) in zip(maxes, parts):
        ones = (lax.broadcasted_iota(jnp.int32, (SUM_ROWS, rows), 0) == 0).astype(BF16)
        v_aug = jnp.concatenate([v_t, ones], axis=0)
        m_exp = m_new if col_keep is None else jnp.maximum(m_new, s_max)
        p_t = jnp.exp2(s_ref[h, row:row + rows] - m_exp).astype(BF16)
        part = jnp.dot(v_aug, p_t, preferred_element_type=F32)
        if col_keep is not None:
            part = part * col_keep
        pv = part if pv is None else pv + part
        row += rows
    if first:
        acc_ref[...] = pv
    else:
        acc_ref[...] = jnp.exp2(m_ref[...] - m_new) * acc_ref[...] + pv
    m_ref[...] = m_new


def _pipeline(n_items, score, attend, bufs, unroll):
    if n_items == 0:
        return
    score(0, bufs[0])
    n_loops = (n_items - 1) // unroll

    def body(p, carry):
        for t in range(unroll):
            n = unroll * p + t
            score(n + 1, bufs[(t + 1) % 2])
            attend(n, bufs[t % 2])
        return carry

    lax.fori_loop(0, n_loops, body, 0)
    done = unroll * n_loops
    for n in range(done, n_items):
        if n + 1 < n_items:
            score(n + 1, bufs[(n + 1 - done) % 2])
        attend(n, bufs[(n - done) % 2])


def _tile_rows(j, span=1):
    return pl.ds(pl.multiple_of(j * ATTN_TILE, ATTN_TILE), span * ATTN_TILE)


def _write_heads(acc_ref, o_ref, nq, head_dim):
    def one(qi, carry):
        for h in range(HEAD_GROUP):
            acc = acc_ref[qi, h]
            o_ref[qi, h * head_dim:(h + 1) * head_dim, :] = (
                acc[:head_dim] / acc[head_dim:head_dim + 1]).astype(o_ref.dtype)
        return carry

    lax.fori_loop(0, nq, one, 0)


def _causal_items(nq, n_near, pair_far):
    phases = [[(i, i - d) for i in range(d, nq)] for d in range(n_near)]
    pairs, singles = [], []
    for i in range(n_near, nq):
        n_far = i - n_near + 1
        n_paired = n_far - n_far % 2 if pair_far else 0
        pairs += [(i, j) for j in range(0, n_paired, 2)]
        singles += [(i, j) for j in range(n_paired, n_far)]
    phases += [pairs, singles]
    flat = [it for ph in phases for it in ph]
    tq = jnp.asarray([it[0] for it in flat], jnp.int32)
    tk = jnp.asarray([it[1] for it in flat], jnp.int32)
    return tq, tk, [len(ph) for ph in phases]


def _moba_kernel(tq_ref, tk_ref, qt_ref, k_ref, vt_ref, kmean_ref, bias_ref, o_ref,
                 keep_ref, s0_ref, s1_ref, t0_ref, t1_ref, m_ref, acc_ref, *, seq, phases):
    nblk = seq // MOBA_BLOCK

    def head_query(qi, h):
        pr, sub = divmod(h, 2)
        q_h = qt_ref[qi, pl.ds(pr * LANES + sub * A_HEAD_DIM, A_HEAD_DIM), :]
        zeros = jnp.zeros_like(q_h)
        return jnp.concatenate([q_h, zeros] if sub == 0 else [zeros, q_h], axis=0)

    def select_blocks(i):
        row = lax.broadcasted_iota(jnp.int32, (nblk, ATTN_TILE), 0)
        rowf = row.astype(F32)
        past = row < i
        for h in range(HEAD_GROUP):
            kmean = kmean_ref[:, (h // 2) * LANES:(h // 2 + 1) * LANES].astype(BF16)
            gate = jnp.dot(kmean, head_query(i, h), preferred_element_type=F32)
            gate = jnp.where(past, gate, -jnp.inf)
            picked = jnp.zeros((nblk, ATTN_TILE), F32)
            for _ in range(MOBA_TOPK):
                top = jnp.max(gate, axis=0, keepdims=True)
                first = jnp.min(jnp.where(gate == top, rowf, float(nblk)), axis=0, keepdims=True)
                pick = rowf == first
                picked = jnp.where(pick, 1.0, picked)
                gate = jnp.where(pick, -jnp.inf, gate)
            keep_ref[i, h] = jnp.where(past, picked, 0.0)

    def stage(offset, span, bias_cols, first):
        def score(n, buf):
            qi, kj = tq_ref[offset + n], tk_ref[offset + n]
            for h in range(HEAD_GROUP):
                k_t = k_ref[_tile_rows(kj, span), (h // 2) * LANES:(h // 2 + 1) * LANES]
                s_t = jnp.dot(k_t, head_query(qi, h), preferred_element_type=F32)
                if bias_cols is not None:
                    s_t = s_t + bias_ref[h, :, bias_cols]
                _store_scores(buf, h, [s_t[t * MOBA_BLOCK:(t + 1) * MOBA_BLOCK]
                                       for t in range(span)])

        def attend(n, buf):
            qi, kj = tq_ref[offset + n], tk_ref[offset + n]
            for h in range(HEAD_GROUP):
                parts = [(MOBA_BLOCK, vt_ref[kj + t, h * A_HEAD_DIM:(h + 1) * A_HEAD_DIM, :],
                          None if first else keep_ref[qi, h, pl.ds(kj + t, 1), :])
                         for t in range(span)]
                _softmax_step(buf, h, parts, m_ref.at[qi, h], acc_ref.at[qi, h], first)
            if first:
                select_blocks(qi)

        return score, attend

    bufs = ((s0_ref, t0_ref), (s1_ref, t1_ref))
    n_own, n_prev, n_pairs, n_single = phases
    _pipeline(n_own, *stage(0, 1, slice(0, MOBA_BLOCK), True), bufs, PIPELINE_UNROLL)
    offset = n_own
    _pipeline(n_prev, *stage(offset, 1, slice(MOBA_BLOCK, 2 * MOBA_BLOCK), False), bufs,
              PIPELINE_UNROLL)
    offset += n_prev
    _pipeline(n_pairs, *stage(offset, 2, None, False), bufs, PIPELINE_UNROLL // 2)
    offset += n_pairs
    _pipeline(n_single, *stage(offset, 1, None, False), bufs, PIPELINE_UNROLL)
    _write_heads(acc_ref, o_ref, nblk, A_HEAD_DIM)


def _moba(qat, ka, vat, kmean, bias):
    b, seq, _ = ka.shape
    nq = seq // ATTN_TILE
    gw = HEAD_GROUP * A_HEAD_DIM
    tq, tk, phases = _causal_items(nq, 2, pair_far=True)
    per_group = lambda bi, g, *_: (bi, 0, g)
    tiles = lambda bi, g, *_: (bi, 0, g, 0)
    return pl.pallas_call(
        functools.partial(_moba_kernel, seq=seq, phases=phases),
        grid_spec=pltpu.PrefetchScalarGridSpec(
            num_scalar_prefetch=2,
            grid=(b, A_HEADS // HEAD_GROUP),
            in_specs=[
                pl.BlockSpec((None, nq, gw, ATTN_TILE), tiles),
                pl.BlockSpec((None, seq, gw), per_group),
                pl.BlockSpec((None, nq, gw, MOBA_BLOCK), tiles),
                pl.BlockSpec((None, nq, gw), per_group),
                pl.BlockSpec((HEAD_GROUP, MOBA_BLOCK, 2 * MOBA_BLOCK), lambda bi, g, *_: (g, 0, 0)),
            ],
            out_specs=pl.BlockSpec((None, nq, gw, ATTN_TILE), tiles),
            scratch_shapes=[
                pltpu.VMEM((nq, HEAD_GROUP, nq, ATTN_TILE), F32),
                pltpu.VMEM((HEAD_GROUP, ITEM_SPAN * ATTN_TILE, ATTN_TILE), F32),
                pltpu.VMEM((HEAD_GROUP, ITEM_SPAN * ATTN_TILE, ATTN_TILE), F32),
                pltpu.VMEM((HEAD_GROUP, ITEM_SPAN, 1, ATTN_TILE), F32),
                pltpu.VMEM((HEAD_GROUP, ITEM_SPAN, 1, ATTN_TILE), F32),
                pltpu.VMEM((nq, HEAD_GROUP, 1, ATTN_TILE), F32),
                pltpu.VMEM((nq, HEAD_GROUP, A_HEAD_DIM + SUM_ROWS, ATTN_TILE), F32),
            ]),
        out_shape=jax.ShapeDtypeStruct((b, nq, A_WIDTH, ATTN_TILE), BF16),
        compiler_params=pltpu.CompilerParams(
            dimension_semantics=("arbitrary", "arbitrary"), vmem_limit_bytes=VMEM_LIMIT),
        name="moba_attention",
    )(tq, tk, qat, ka, vat, kmean, bias)


def _mla_kernel(tq_ref, tk_ref, qt_ref, k_ref, vt_ref, o_ref, s0_ref, s1_ref, t0_ref, t1_ref,
                m_ref, acc_ref, *, nq, phases):
    def stage(offset, span, diagonal):
        def score(n, buf):
            qi, kj = tq_ref[offset + n], tk_ref[offset + n]
            for h in range(HEAD_GROUP):
                sl = slice(h * MLA_HEAD_PAD, (h + 1) * MLA_HEAD_PAD)
                s_t = jnp.dot(k_ref[_tile_rows(kj, span), sl], qt_ref[qi, sl, :],
                              preferred_element_type=F32)
                if diagonal:
                    key = lax.broadcasted_iota(jnp.int32, (ATTN_TILE, ATTN_TILE), 0)
                    qry = lax.broadcasted_iota(jnp.int32, (ATTN_TILE, ATTN_TILE), 1)
                    s_t = jnp.where(key <= qry, s_t, NEG)
                _store_scores(buf, h, [s_t[t * ATTN_TILE:(t + 1) * ATTN_TILE]
                                       for t in range(span)])

        def attend(n, buf):
            qi, kj = tq_ref[offset + n], tk_ref[offset + n]
            for h in range(HEAD_GROUP):
                parts = [(ATTN_TILE, vt_ref[kj + t, h * V_HEAD:(h + 1) * V_HEAD, :], None)
                         for t in range(span)]
                _softmax_step(buf, h, parts, m_ref.at[qi, h], acc_ref.at[qi, h], diagonal)

        return score, attend

    bufs = ((s0_ref, t0_ref), (s1_ref, t1_ref))
    n_diag, n_pairs, n_single = phases
    _pipeline(n_diag, *stage(0, 1, True), bufs, PIPELINE_UNROLL)
    _pipeline(n_pairs, *stage(n_diag, 2, False), bufs, PIPELINE_UNROLL // 2)
    _pipeline(n_single, *stage(n_diag + n_pairs, 1, False), bufs, PIPELINE_UNROLL)
    _write_heads(acc_ref, o_ref, nq, V_HEAD)


def _mla(qmt, km, vbt):
    b, seq, _ = km.shape
    nq = seq // ATTN_TILE
    qw = HEAD_GROUP * MLA_HEAD_PAD
    vw = HEAD_GROUP * V_HEAD
    tq, tk, phases = _causal_items(nq, 1, pair_far=False)
    per_group = lambda bi, g, *_: (bi, 0, g)
    tiles = lambda bi, g, *_: (bi, 0, g, 0)
    return pl.pallas_call(
        functools.partial(_mla_kernel, nq=nq, phases=phases),
        grid_spec=pltpu.PrefetchScalarGridSpec(
            num_scalar_prefetch=2,
            grid=(b, B_HEADS // HEAD_GROUP),
            in_specs=[
                pl.BlockSpec((None, nq, qw, ATTN_TILE), tiles),
                pl.BlockSpec((None, seq, qw), per_group),
                pl.BlockSpec((None, nq, vw, ATTN_TILE), tiles),
            ],
            out_specs=pl.BlockSpec((None, nq, vw, ATTN_TILE), tiles),
            scratch_shapes=[
                pltpu.VMEM((HEAD_GROUP, ITEM_SPAN * ATTN_TILE, ATTN_TILE), F32),
                pltpu.VMEM((HEAD_GROUP, ITEM_SPAN * ATTN_TILE, ATTN_TILE), F32),
                pltpu.VMEM((HEAD_GROUP, ITEM_SPAN, 1, ATTN_TILE), F32),
                pltpu.VMEM((HEAD_GROUP, ITEM_SPAN, 1, ATTN_TILE), F32),
                pltpu.VMEM((nq, HEAD_GROUP, 1, ATTN_TILE), F32),
                pltpu.VMEM((nq, HEAD_GROUP, V_HEAD + SUM_ROWS, ATTN_TILE), F32),
            ]),
        out_shape=jax.ShapeDtypeStruct((b, nq, B_WIDTH, ATTN_TILE), BF16),
        compiler_params=pltpu.CompilerParams(
            dimension_semantics=("arbitrary", "arbitrary"), vmem_limit_bytes=VMEM_LIMIT),
        name="mla_attention",
    )(tq, tk, qmt, km, vbt)


def _out_kernel(x_ref, oa_ref, ob_ref, gattn_ref, wg_ref, wpa_ref, wpb_ref, wout_ref,
                gmlp_ref, wup_ref, wdn_ref, gfin_ref, o_ref):
    x = x_ref[...]
    n = _rms(x, gattn_ref[...]).astype(BF16)
    gates = lax.dot_general(n, wg_ref[...], NT_DIMS, preferred_element_type=F32)
    tn_dims = (((0,), (0,)), ((), ()))
    tiles = range(ROW_TILE // ATTN_TILE)
    pa = jnp.concatenate([lax.dot_general(oa_ref[t], wpa_ref[...], tn_dims,
                                          preferred_element_type=F32) for t in tiles], axis=0)
    pb = jnp.concatenate([lax.dot_general(ob_ref[t], wpb_ref[...], tn_dims,
                                          preferred_element_type=F32) for t in tiles], axis=0)
    merged = (jax.nn.sigmoid(gates[:, :D_MODEL]) * pa
              + jax.nn.sigmoid(gates[:, D_MODEL:]) * pb).astype(BF16)
    h = x + jnp.dot(merged, wout_ref[...], preferred_element_type=F32)
    m = _rms(h, gmlp_ref[...]).astype(BF16)
    for c in range(D_FF // FF_CHUNK):
        cols = slice(c * FF_CHUNK, (c + 1) * FF_CHUNK)
        up = jnp.dot(m, wup_ref[:, cols], preferred_element_type=F32)
        act = jnp.square(jnp.maximum(up, 0.0)).astype(BF16)
        h = h + jnp.dot(act, wdn_ref[cols, :], preferred_element_type=F32)
    o_ref[...] = _rms(h, gfin_ref[...])


def _out_block(x2, oa, ob, g_attn, wg, wpa, wpb, wout, g_mlp, wup, wdn, g_fin):
    n_rows = x2.shape[0]
    tm = ROW_TILE
    row = lambda i: (i, 0)
    return pl.pallas_call(
        _out_kernel,
        grid=(n_rows // tm,),
        in_specs=[
            pl.BlockSpec((tm, D_MODEL), row),
            pl.BlockSpec((tm // ATTN_TILE, A_WIDTH, ATTN_TILE), lambda i: (i, 0, 0)),
            pl.BlockSpec((tm // ATTN_TILE, B_WIDTH, ATTN_TILE), lambda i: (i, 0, 0)),
            _const_spec(g_attn.shape), _const_spec(wg.shape), _const_spec(wpa.shape),
            _const_spec(wpb.shape), _const_spec(wout.shape), _const_spec(g_mlp.shape),
            _const_spec(wup.shape), _const_spec(wdn.shape), _const_spec(g_fin.shape),
        ],
        out_specs=pl.BlockSpec((tm, D_MODEL), row),
        out_shape=jax.ShapeDtypeStruct((n_rows, D_MODEL), F32),
        compiler_params=pltpu.CompilerParams(
            dimension_semantics=("arbitrary",), vmem_limit_bytes=VMEM_LIMIT),
        name="out_mlp",
    )(x2, oa, ob, g_attn, wg, wpa, wpb, wout, g_mlp, wup, wdn, g_fin)


def _rope_tables(seq):
    half = QK_ROPE // 2
    inv_freq = ROPE_THETA ** (-jnp.arange(half, dtype=F32) / half)
    ang = jnp.arange(seq).astype(F32)[:, None] * inv_freq[None, :]
    cos2 = jnp.tile(jnp.cos(ang), (1, 2))
    sin2 = jnp.tile(jnp.sin(ang), (1, 2))
    pad = jnp.zeros((seq, MLA_HEAD_PAD - QK_NOPE - QK_ROPE), F32)
    cos_t = jnp.concatenate([jnp.ones((seq, QK_NOPE), F32), cos2, pad], axis=1)
    sin_t = jnp.concatenate([jnp.zeros((seq, QK_NOPE), F32), sin2, pad], axis=1)
    return cos_t, sin_t


def _rotate_half_cols(w):
    half = QK_ROPE // 2
    return jnp.concatenate([-w[..., half:], w[..., :half]], axis=-1)


def _pad_heads(w_nope, w_rope):
    k, h, _ = w_nope.shape
    if w_rope is None:
        w_rope = jnp.zeros((k, h, QK_ROPE), w_nope.dtype)
    pad = jnp.zeros((k, h, MLA_HEAD_PAD - QK_NOPE - QK_ROPE), w_nope.dtype)
    return jnp.concatenate([w_nope, w_rope, pad], axis=-1).reshape(k, h * MLA_HEAD_PAD)


def kernel(x, w_in, rel_bias, mla_q_norm, w_uq, mla_kv_norm, w_uk, w_uv, w_proj_a, w_proj_b,
           w_out, norm_attn, norm_mlp, w_mlp_up, w_mlp_down, norm_final):
    b, seq, d = x.shape
    assert d == D_MODEL and seq % ROW_TILE == 0 and seq // MOBA_BLOCK <= LANES
    assert w_in.shape[0] == 1, "single-layer block"
    x2 = x.reshape(b * seq, d)

    w_t = jnp.swapaxes(w_in[0], 0, 1)
    o_k = A_WIDTH
    o_v = 2 * A_WIDTH
    o_cq = 3 * A_WIDTH
    o_ckv = o_cq + Q_LORA
    o_kr = o_ckv + KV_LORA
    o_g = o_kr + QK_ROPE
    wk = w_t[o_k:o_v].astype(BF16)
    wqv = jnp.concatenate([w_t[:o_k], w_t[o_v:o_cq]], axis=0).astype(BF16)
    w_kr = w_t[o_kr:o_g]
    half = QK_ROPE // 2
    w_kr_rot = jnp.concatenate([-w_kr[half:], w_kr[:half]], axis=0)
    row_pad = lambda a: jnp.pad(a, ((QK_NOPE, MLA_HEAD_PAD - QK_NOPE - QK_ROPE), (0, 0)))
    wc = jnp.concatenate(
        [w_t[o_cq:o_kr], row_pad(w_kr), row_pad(w_kr_rot)], axis=0).astype(BF16)
    wg = w_t[o_g:].astype(BF16)

    uq = w_uq[0].reshape(Q_LORA, B_HEADS, QK_NOPE + QK_ROPE)
    uq_nope, uq_rope = uq[..., :QK_NOPE], uq[..., QK_NOPE:]
    wuq = _pad_heads(uq_nope, uq_rope).T.astype(BF16)
    wuk = _pad_heads(w_uk[0].reshape(KV_LORA, B_HEADS, QK_NOPE), None).astype(BF16)
    wuv = w_uv[0].T.astype(BF16)

    cos_t, sin_t = _rope_tables(seq)
    q_scale = (QK_NOPE + QK_ROPE) ** -0.5 * LOG2E
    row2 = lambda a: a.reshape(1, -1)

    qat, ka, vat, kmean, qmt, km, vbt = _inproj(
        x2, row2(norm_attn[0]), wk, wqv, wc, row2(mla_q_norm[0]), row2(mla_kv_norm[0]),
        wuq, wuk, wuv, (cos_t * q_scale).T, (sin_t * q_scale).T, cos_t, sin_t, b, seq)

    as_seq = lambda a: a.reshape(b, seq, a.shape[-1])
    bias = _bias_tiles(rel_bias)
    oa = _moba(qat, as_seq(ka), vat, kmean.reshape(b, seq // MOBA_BLOCK, A_WIDTH), bias)
    ob = _mla(qmt, as_seq(km), vbt)

    out = _out_block(
        x2, oa.reshape(-1, A_WIDTH, ATTN_TILE), ob.reshape(-1, B_WIDTH, ATTN_TILE),
        row2(norm_attn[0]),
        wg, w_proj_a[0].astype(BF16), w_proj_b[0].astype(BF16), w_out[0].astype(BF16),
        row2(norm_mlp[0]), w_mlp_up[0].astype(BF16), w_mlp_down[0].astype(BF16),
        row2(norm_final))
    return out.reshape(b, seq, d)
```

```python
import functools
import math

import jax
import jax.numpy as jnp
from jax import lax
from jax.experimental import pallas as pl
from jax.experimental.pallas import tpu as pltpu

D_MODEL = 1024
A_HEADS = 8
A_HEAD_DIM = 64
A_WIDTH = A_HEADS * A_HEAD_DIM
MOBA_BLOCK = 256
MOBA_TOPK = 3
REL_BUCKETS = 32
REL_MAX_DIST = 128
B_HEADS = 8
QK_NOPE = 64
QK_ROPE = 32
V_HEAD = 64
B_WIDTH = B_HEADS * V_HEAD
Q_LORA = 384
KV_LORA = 256
ROPE_THETA = 10000.0
D_FF = 4 * D_MODEL
EPS = 1e-6
NEG = -1e30
LOG2E = math.log2(math.e)

LANES = 128
MLA_HEAD_PAD = 128
ATTN_TILE = 256
HEAD_GROUP = 4
SUM_ROWS = 16
ITEM_SPAN = 2
PIPELINE_UNROLL = 16
ROW_TILE = 512
FF_CHUNK = 1024
VMEM_LIMIT = 56 * 1024 * 1024

F32 = jnp.float32
BF16 = jnp.bfloat16
NT_DIMS = (((1,), (1,)), ((), ()))


def _rms(xf, g):
    y = xf * lax.rsqrt(jnp.mean(xf * xf, axis=-1, keepdims=True) + EPS)
    return y * g


def _const_spec(shape):
    zeros = (0,) * len(shape)
    return pl.BlockSpec(shape, lambda *_: zeros, pipeline_mode=pl.Buffered(1))


def _bias_kernel(rel_ref, o_ref):
    h = pl.program_id(0)
    shape = (MOBA_BLOCK, 2 * MOBA_BLOCK)
    r = lax.broadcasted_iota(jnp.int32, shape, 0)
    c = lax.broadcasted_iota(jnp.int32, shape, 1)
    d = c - r
    dist = jnp.maximum(d, 0)
    max_exact = REL_BUCKETS // 2
    df = jnp.maximum(dist, 1).astype(F32)
    large = max_exact + (jnp.log(df / max_exact) / math.log(REL_MAX_DIST / max_exact)
                         * (REL_BUCKETS - max_exact)).astype(jnp.int32)
    large = jnp.minimum(large, REL_BUCKETS - 1)
    bucket = jnp.where(dist < max_exact, dist, large)
    val = jnp.zeros(shape, F32)
    for b in range(REL_BUCKETS):
        val = jnp.where(bucket == b, rel_ref[b, h], val)
    o_ref[...] = jnp.where(d >= 0, (val - rel_ref[REL_BUCKETS - 1, h]) * LOG2E, NEG)


def _bias_tiles(rel_bias):
    return pl.pallas_call(
        _bias_kernel,
        grid=(A_HEADS,),
        in_specs=[pl.BlockSpec(memory_space=pltpu.SMEM)],
        out_specs=pl.BlockSpec((None, MOBA_BLOCK, 2 * MOBA_BLOCK), lambda h: (h, 0, 0)),
        out_shape=jax.ShapeDtypeStruct((A_HEADS, MOBA_BLOCK, 2 * MOBA_BLOCK), F32),
        name="moba_bias_tiles",
    )(rel_bias)


def _inproj_kernel(x_ref, g_ref, wk_ref, wqv_ref, wc_ref, qn_ref, kvn_ref, wuq_ref,
                   wuk_ref, wuv_ref, cqt_ref, sqt_ref, ck_ref, sk_ref,
                   qat_ref, ka_ref, vat_ref, kmean_ref, qmt_ref, km_ref, vbt_ref):
    n = _rms(x_ref[...], g_ref[...]).astype(BF16)
    k = lax.dot_general(n, wk_ref[...], NT_DIMS, preferred_element_type=F32)
    ka_ref[...] = k.astype(BF16)
    nblk = ROW_TILE // ATTN_TILE
    kmean_ref[0] = jnp.sum(k.reshape(nblk, MOBA_BLOCK, A_WIDTH), axis=1) * (1.0 / MOBA_BLOCK)

    c = lax.dot_general(n, wc_ref[...], NT_DIMS, preferred_element_type=F32)
    cq = _rms(c[:, :Q_LORA], qn_ref[...]).astype(BF16)
    ckv = _rms(c[:, Q_LORA:Q_LORA + KV_LORA], kvn_ref[...]).astype(BF16)
    kr = c[:, Q_LORA + KV_LORA:Q_LORA + KV_LORA + LANES]
    krr = c[:, Q_LORA + KV_LORA + LANES:]
    for t in range(nblk):
        rows = slice(t * ATTN_TILE, (t + 1) * ATTN_TILE)
        qv_t = lax.dot_general(wqv_ref[...], n[rows], NT_DIMS,
                               preferred_element_type=F32)
        qat_ref[t] = (qv_t[:A_WIDTH] * (A_HEAD_DIM ** -0.5 * LOG2E)).astype(BF16)
        vat_ref[t] = qv_t[A_WIDTH:].astype(BF16)
        vbt_ref[t] = lax.dot_general(wuv_ref[...], ckv[rows], NT_DIMS,
                                     preferred_element_type=F32).astype(BF16)
        q_t = lax.dot_general(wuq_ref[...], cq[rows], NT_DIMS,
                              preferred_element_type=F32)
        cq_t, sq_t = cqt_ref[:, rows], sqt_ref[:, rows]
        half = QK_ROPE // 2
        for h in range(B_HEADS):
            q_h = q_t[h * MLA_HEAD_PAD:(h + 1) * MLA_HEAD_PAD]
            x1 = q_h[QK_NOPE:QK_NOPE + half]
            x2 = q_h[QK_NOPE + half:QK_NOPE + QK_ROPE]
            rot_h = jnp.concatenate(
                [jnp.zeros((QK_NOPE, ATTN_TILE), F32), -x2, x1,
                 jnp.zeros((MLA_HEAD_PAD - QK_NOPE - QK_ROPE, ATTN_TILE), F32)], axis=0)
            qmt_ref[t, h * MLA_HEAD_PAD:(h + 1) * MLA_HEAD_PAD, :] = (
                q_h * cq_t + rot_h * sq_t).astype(BF16)
    kn = jnp.dot(ckv, wuk_ref[...], preferred_element_type=F32)
    k_rope = kr * ck_ref[...] + krr * sk_ref[...]
    for h in range(B_HEADS):
        sl = slice(h * MLA_HEAD_PAD, (h + 1) * MLA_HEAD_PAD)
        km_ref[:, sl] = (kn[:, sl] + k_rope).astype(BF16)


def _inproj(x2, g_attn, wk, wqv, wc, qn, kvn, wuq, wuk, wuv, cq_tt, sq_tt, ck_t, sk_t,
            batch, seq):
    n_rows = x2.shape[0]
    tm = ROW_TILE
    steps = n_rows // tm
    per_seq = seq // tm
    nblk = tm // ATTN_TILE
    row = lambda i: (i, 0)
    tab = lambda i: (i % per_seq, 0)
    tab_t = lambda i: (0, i % per_seq)
    tile_map = lambda i: (i // per_seq, i % per_seq, 0, 0)
    bf = lambda w: jax.ShapeDtypeStruct((n_rows, w), BF16)
    tiles = lambda w: jax.ShapeDtypeStruct((batch, seq // ATTN_TILE, w, ATTN_TILE), BF16)
    tile_spec = lambda w: pl.BlockSpec((None, nblk, w, ATTN_TILE), tile_map)
    return pl.pallas_call(
        _inproj_kernel,
        grid=(steps,),
        in_specs=[
            pl.BlockSpec((tm, D_MODEL), row),
            _const_spec(g_attn.shape), _const_spec(wk.shape), _const_spec(wqv.shape),
            _const_spec(wc.shape), _const_spec(qn.shape), _const_spec(kvn.shape),
            _const_spec(wuq.shape), _const_spec(wuk.shape),
            _const_spec(wuv.shape),
            pl.BlockSpec((LANES, tm), tab_t), pl.BlockSpec((LANES, tm), tab_t),
            pl.BlockSpec((tm, LANES), tab), pl.BlockSpec((tm, LANES), tab),
        ],
        out_specs=[
            tile_spec(A_WIDTH), pl.BlockSpec((tm, A_WIDTH), row), tile_spec(A_WIDTH),
            pl.BlockSpec((1, nblk, A_WIDTH), lambda i: (i, 0, 0)),
            tile_spec(B_HEADS * MLA_HEAD_PAD),
            pl.BlockSpec((tm, B_HEADS * MLA_HEAD_PAD), row),
            tile_spec(B_WIDTH),
        ],
        out_shape=[
            tiles(A_WIDTH), bf(A_WIDTH), tiles(A_WIDTH),
            jax.ShapeDtypeStruct((steps, nblk, A_WIDTH), F32),
            tiles(B_HEADS * MLA_HEAD_PAD), bf(B_HEADS * MLA_HEAD_PAD), tiles(B_WIDTH),
        ],
        compiler_params=pltpu.CompilerParams(
            dimension_semantics=("arbitrary",), vmem_limit_bytes=VMEM_LIMIT),
        name="in_proj",
    )(x2, g_attn, wk, wqv, wc, qn, kvn, wuq, wuk, wuv, cq_tt, sq_tt, ck_t, sk_t)


def _store_scores(buf, h, parts):
    s_ref, max_ref = buf
    row = 0
    for idx, s_t in enumerate(parts):
        s_ref[h, row:row + s_t.shape[0]] = s_t
        max_ref[h, idx] = jnp.max(s_t, axis=0, keepdims=True)
        row += s_t.shape[0]


def _softmax_step(buf, h, parts, m_ref, acc_ref, first):
    s_ref, max_ref = buf
    maxes = [max_ref[h, idx] for idx in range(len(parts))]
    m_new = None if first else m_ref[...]
    for s_max, (_, _, col_keep) in zip(maxes, parts):
        if col_keep is not None:
            s_max = jnp.where(col_keep > 0, s_max, NEG)
        m_new = s_max if m_new is None else jnp.maximum(m_new, s_max)
    pv = None
    row = 0
    for s_max, (rows, v_t, col_keep) in zip(maxes, parts):
        ones = (lax.broadcasted_iota(jnp.int32, (SUM_ROWS, rows), 0) == 0).astype(BF16)
        v_aug = jnp.concatenate([v_t, ones], axis=0)
        m_exp = m_new if col_keep is None else jnp.maximum(m_new, s_max)
        p_t = jnp.exp2(s_ref[h, row:row + rows] - m_exp).astype(BF16)
        part = jnp.dot(v_aug, p_t, preferred_element_type=F32)
        if col_keep is not None:
            part = part * col_keep
        pv = part if pv is None else pv + part
        row += rows
    if first:
        acc_ref[...] = pv
    else:
        acc_ref[...] = jnp.exp2(m_ref[...] - m_new) * acc_ref[...] + pv
    m_ref[...] = m_new


def _pipeline(n_items, score, attend, bufs, unroll):
    if n_items == 0:
        return
    score(0, bufs[0])
    n_loops = (n_items - 1) // unroll

    def body(p, carry):
        for t in range(unroll):
            n = unroll * p + t
            score(n + 1, bufs[(t + 1) % 2])
            attend(n, bufs[t % 2])
        return carry

    lax.fori_loop(0, n_loops, body, 0)
    done = unroll * n_loops
    for n in range(done, n_items):
        if n + 1 < n_items:
            score(n + 1, bufs[(n + 1 - done) % 2])
        attend(n, bufs[(n - done) % 2])


def _tile_rows(j, span=1):
    return pl.ds(pl.multiple_of(j * ATTN_TILE, ATTN_TILE), span * ATTN_TILE)


def _write_heads(acc_ref, o_ref, nq, head_dim):
    def one(qi, carry):
        for h in range(HEAD_GROUP):
            acc = acc_ref[qi, h]
            o_ref[qi, h * head_dim:(h + 1) * head_dim, :] = (
                acc[:head_dim] / acc[head_dim:head_dim + 1]).astype(o_ref.dtype)
        return carry

    lax.fori_loop(0, nq, one, 0)


def _causal_items(nq, n_near, pair_far):
    phases = [[(i, i - d) for i in range(d, nq)] for d in range(n_near)]
    pairs, singles = [], []
    for i in range(n_near, nq):
        n_far = i - n_near + 1
        n_paired = n_far - n_far % 2 if pair_far else 0
        pairs += [(i, j) for j in range(0, n_paired, 2)]
        singles += [(i, j) for j in range(n_paired, n_far)]
    phases += [pairs, singles]
    flat = [it for ph in phases for it in ph]
    tq = jnp.asarray([it[0] for it in flat], jnp.int32)
    tk = jnp.asarray([it[1] for it in flat], jnp.int32)
    return tq, tk, [len(ph) for ph in phases]


def _moba_kernel(tq_ref, tk_ref, qt_ref, k_ref, vt_ref, kmean_ref, bias_ref, o_ref,
                 keep_ref, s0_ref, s1_ref, t0_ref, t1_ref, m_ref, acc_ref, *, seq, phases):
    nblk = seq // MOBA_BLOCK

    def head_query(qi, h):
        pr, sub = divmod(h, 2)
        q_h = qt_ref[qi, pl.ds(pr * LANES + sub * A_HEAD_DIM, A_HEAD_DIM), :]
        zeros = jnp.zeros_like(q_h)
        return jnp.concatenate([q_h, zeros] if sub == 0 else [zeros, q_h], axis=0)

    def select_blocks(i):
        row = lax.broadcasted_iota(jnp.int32, (nblk, ATTN_TILE), 0)
        rowf = row.astype(F32)
        past = row < i
        for h in range(HEAD_GROUP):
            kmean = kmean_ref[:, (h // 2) * LANES:(h // 2 + 1) * LANES].astype(BF16)
            gate = jnp.dot(kmean, head_query(i, h), preferred_element_type=F32)
            gate = jnp.where(past, gate, -jnp.inf)
            picked = jnp.zeros((nblk, ATTN_TILE), F32)
            for _ in range(MOBA_TOPK):
                top = jnp.max(gate, axis=0, keepdims=True)
                first = jnp.min(jnp.where(gate == top, rowf, float(nblk)), axis=0, keepdims=True)
                pick = rowf == first
                picked = jnp.where(pick, 1.0, picked)
                gate = jnp.where(pick, -jnp.inf, gate)
            keep_ref[i, h] = jnp.where(past, picked, 0.0)

    def stage(offset, span, bias_cols, first):
        def score(n, buf):
            qi, kj = tq_ref[offset + n], tk_ref[offset + n]
            for h in range(HEAD_GROUP):
                k_t = k_ref[_tile_rows(kj, span), (h // 2) * LANES:(h // 2 + 1) * LANES]
                s_t = jnp.dot(k_t, head_query(qi, h), preferred_element_type=F32)
                if bias_cols is not None:
                    s_t = s_t + bias_ref[h, :, bias_cols]
                _store_scores(buf, h, [s_t[t * MOBA_BLOCK:(t + 1) * MOBA_BLOCK]
                                       for t in range(span)])

        def attend(n, buf):
            qi, kj = tq_ref[offset + n], tk_ref[offset + n]
            for h in range(HEAD_GROUP):
                parts = [(MOBA_BLOCK, vt_ref[kj + t, h * A_HEAD_DIM:(h + 1) * A_HEAD_DIM, :],
                          None if first else keep_ref[qi, h, pl.ds(kj + t, 1), :])
                         for t in range(span)]
                _softmax_step(buf, h, parts, m_ref.at[qi, h], acc_ref.at[qi, h], first)
            if first:
                select_blocks(qi)

        return score, attend

    bufs = ((s0_ref, t0_ref), (s1_ref, t1_ref))
    n_own, n_prev, n_pairs, n_single = phases
    _pipeline(n_own, *stage(0, 1, slice(0, MOBA_BLOCK), True), bufs, PIPELINE_UNROLL)
    offset = n_own
    _pipeline(n_prev, *stage(offset, 1, slice(MOBA_BLOCK, 2 * MOBA_BLOCK), False), bufs,
              PIPELINE_UNROLL)
    offset += n_prev
    _pipeline(n_pairs, *stage(offset, 2, None, False), bufs, PIPELINE_UNROLL // 2)
    offset += n_pairs
    _pipeline(n_single, *stage(offset, 1, None, False), bufs, PIPELINE_UNROLL)
    _write_heads(acc_ref, o_ref, nblk, A_HEAD_DIM)


def _moba(qat, ka, vat, kmean, bias):
    b, seq, _ = ka.shape
    nq = seq // ATTN_TILE
    gw = HEAD_GROUP * A_HEAD_DIM
    tq, tk, phases = _causal_items(nq, 2, pair_far=True)
    per_group = lambda bi, g, *_: (bi, 0, g)
    tiles = lambda bi, g, *_: (bi, 0, g, 0)
    return pl.pallas_call(
        functools.partial(_moba_kernel, seq=seq, phases=phases),
        grid_spec=pltpu.PrefetchScalarGridSpec(
            num_scalar_prefetch=2,
            grid=(b, A_HEADS // HEAD_GROUP),
            in_specs=[
                pl.BlockSpec((None, nq, gw, ATTN_TILE), tiles),
                pl.BlockSpec((None, seq, gw), per_group),
                pl.BlockSpec((None, nq, gw, MOBA_BLOCK), tiles),
                pl.BlockSpec((None, nq, gw), per_group),
                pl.BlockSpec((HEAD_GROUP, MOBA_BLOCK, 2 * MOBA_BLOCK), lambda bi, g, *_: (g, 0, 0)),
            ],
            out_specs=pl.BlockSpec((None, nq, gw, ATTN_TILE), tiles),
            scratch_shapes=[
                pltpu.VMEM((nq, HEAD_GROUP, nq, ATTN_TILE), F32),
                pltpu.VMEM((HEAD_GROUP, ITEM_SPAN * ATTN_TILE, ATTN_TILE), F32),
                pltpu.VMEM((HEAD_GROUP, ITEM_SPAN * ATTN_TILE, ATTN_TILE), F32),
                pltpu.VMEM((HEAD_GROUP, ITEM_SPAN, 1, ATTN_TILE), F32),
                pltpu.VMEM((HEAD_GROUP, ITEM_SPAN, 1, ATTN_TILE), F32),
                pltpu.VMEM((nq, HEAD_GROUP, 1, ATTN_TILE), F32),
                pltpu.VMEM((nq, HEAD_GROUP, A_HEAD_DIM + SUM_ROWS, ATTN_TILE), F32),
            ]),
        out_shape=jax.ShapeDtypeStruct((b, nq, A_WIDTH, ATTN_TILE), BF16),
        compiler_params=pltpu.CompilerParams(
            dimension_semantics=("arbitrary", "arbitrary"), vmem_limit_bytes=VMEM_LIMIT),
        name="moba_attention",
    )(tq, tk, qat, ka, vat, kmean, bias)


def _mla_kernel(tq_ref, tk_ref, qt_ref, k_ref, vt_ref, o_ref, s0_ref, s1_ref, t0_ref, t1_ref,
                m_ref, acc_ref, *, nq, phases):
    def stage(offset, span, diagonal):
        def score(n, buf):
            qi, kj = tq_ref[offset + n], tk_ref[offset + n]
            for h in range(HEAD_GROUP):
                sl = slice(h * MLA_HEAD_PAD, (h + 1) * MLA_HEAD_PAD)
                s_t = jnp.dot(k_ref[_tile_rows(kj, span), sl], qt_ref[qi, sl, :],
                              preferred_element_type=F32)
                if diagonal:
                    key = lax.broadcasted_iota(jnp.int32, (ATTN_TILE, ATTN_TILE), 0)
                    qry = lax.broadcasted_iota(jnp.int32, (ATTN_TILE, ATTN_TILE), 1)
                    s_t = jnp.where(key <= qry, s_t, NEG)
                _store_scores(buf, h, [s_t[t * ATTN_TILE:(t + 1) * ATTN_TILE]
                                       for t in range(span)])

        def attend(n, buf):
            qi, kj = tq_ref[offset + n], tk_ref[offset + n]
            for h in range(HEAD_GROUP):
                parts = [(ATTN_TILE, vt_ref[kj + t, h * V_HEAD:(h + 1) * V_HEAD, :], None)
                         for t in range(span)]
                _softmax_step(buf, h, parts, m_ref.at[qi, h], acc_ref.at[qi, h], diagonal)

        return score, attend

    bufs = ((s0_ref, t0_ref), (s1_ref, t1_ref))
    n_diag, n_pairs, n_single = phases
    _pipeline(n_diag, *stage(0, 1, True), bufs, PIPELINE_UNROLL)
    _pipeline(n_pairs, *stage(n_diag, 2, False), bufs, PIPELINE_UNROLL // 2)
    _pipeline(n_single, *stage(n_diag + n_pairs, 1, False), bufs, PIPELINE_UNROLL)
    _write_heads(acc_ref, o_ref, nq, V_HEAD)


def _mla(qmt, km, vbt):
    b, seq, _ = km.shape
    nq = seq // ATTN_TILE
    qw = HEAD_GROUP * MLA_HEAD_PAD
    vw = HEAD_GROUP * V_HEAD
    tq, tk, phases = _causal_items(nq, 1, pair_far=False)
    per_group = lambda bi, g, *_: (bi, 0, g)
    tiles = lambda bi, g, *_: (bi, 0, g, 0)
    return pl.pallas_call(
        functools.partial(_mla_kernel, nq=nq, phases=phases),
        grid_spec=pltpu.PrefetchScalarGridSpec(
            num_scalar_prefetch=2,
            grid=(b, B_HEADS // HEAD_GROUP),
            in_specs=[
                pl.BlockSpec((None, nq, qw, ATTN_TILE), tiles),
                pl.BlockSpec((None, seq, qw), per_group),
                pl.BlockSpec((None, nq, vw, ATTN_TILE), tiles),
            ],
            out_specs=pl.BlockSpec((None, nq, vw, ATTN_TILE), tiles),
            scratch_shapes=[
                pltpu.VMEM((HEAD_GROUP, ITEM_SPAN * ATTN_TILE, ATTN_TILE), F32),
                pltpu.VMEM((HEAD_GROUP, ITEM_SPAN * ATTN_TILE, ATTN_TILE), F32),
                pltpu.VMEM((HEAD_GROUP, ITEM_SPAN, 1, ATTN_TILE), F32),
                pltpu.VMEM((HEAD_GROUP, ITEM_SPAN, 1, ATTN_TILE), F32),
                pltpu.VMEM((nq, HEAD_GROUP, 1, ATTN_TILE), F32),
                pltpu.VMEM((nq, HEAD_GROUP, V_HEAD + SUM_ROWS, ATTN_TILE), F32),
            ]),
        out_shape=jax.ShapeDtypeStruct((b, nq, B_WIDTH, ATTN_TILE), BF16),
        compiler_params=pltpu.CompilerParams(
            dimension_semantics=("arbitrary", "arbitrary"), vmem_limit_bytes=VMEM_LIMIT),
        name="mla_attention",
    )(tq, tk, qmt, km, vbt)


def _out_kernel(x_ref, oa_ref, ob_ref, gattn_ref, wg_ref, wpa_ref, wpb_ref, wout_ref,
                gmlp_ref, wup_ref, wdn_ref, gfin_ref, o_ref):
    x = x_ref[...]
    n = _rms(x, gattn_ref[...]).astype(BF16)
    gates = lax.dot_general(n, wg_ref[...], NT_DIMS, preferred_element_type=F32)
    tn_dims = (((0,), (0,)), ((), ()))
    tiles = range(ROW_TILE // ATTN_TILE)
    pa = jnp.concatenate([lax.dot_general(oa_ref[t], wpa_ref[...], tn_dims,
                                          preferred_element_type=F32) for t in tiles], axis=0)
    pb = jnp.concatenate([lax.dot_general(ob_ref[t], wpb_ref[...], tn_dims,
                                          preferred_element_type=F32) for t in tiles], axis=0)
    merged = (jax.nn.sigmoid(gates[:, :D_MODEL]) * pa
              + jax.nn.sigmoid(gates[:, D_MODEL:]) * pb).astype(BF16)
    h = x + jnp.dot(merged, wout_ref[...], preferred_element_type=F32)
    m = _rms(h, gmlp_ref[...]).astype(BF16)
    for c in range(D_FF // FF_CHUNK):
        cols = slice(c * FF_CHUNK, (c + 1) * FF_CHUNK)
        up = jnp.dot(m, wup_ref[:, cols], preferred_element_type=F32)
        act = jnp.square(jnp.maximum(up, 0.0)).astype(BF16)
        h = h + jnp.dot(act, wdn_ref[cols, :], preferred_element_type=F32)
    o_ref[...] = _rms(h, gfin_ref[...])


def _out_block(x2, oa, ob, g_attn, wg, wpa, wpb, wout, g_mlp, wup, wdn, g_fin):
    n_rows = x2.shape[0]
    tm = ROW_TILE
    row = lambda i: (i, 0)
    return pl.pallas_call(
        _out_kernel,
        grid=(n_rows // tm,),
        in_specs=[
            pl.BlockSpec((tm, D_MODEL), row),
            pl.BlockSpec((tm // ATTN_TILE, A_WIDTH, ATTN_TILE), lambda i: (i, 0, 0)),
            pl.BlockSpec((tm // ATTN_TILE, B_WIDTH, ATTN_TILE), lambda i: (i, 0, 0)),
            _const_spec(g_attn.shape), _const_spec(wg.shape), _const_spec(wpa.shape),
            _const_spec(wpb.shape), _const_spec(wout.shape), _const_spec(g_mlp.shape),
            _const_spec(wup.shape), _const_spec(wdn.shape), _const_spec(g_fin.shape),
        ],
        out_specs=pl.BlockSpec((tm, D_MODEL), row),
        out_shape=jax.ShapeDtypeStruct((n_rows, D_MODEL), F32),
        compiler_params=pltpu.CompilerParams(
            dimension_semantics=("arbitrary",), vmem_limit_bytes=VMEM_LIMIT),
        name="out_mlp",
    )(x2, oa, ob, g_attn, wg, wpa, wpb, wout, g_mlp, wup, wdn, g_fin)


def _rope_tables(seq):
    half = QK_ROPE // 2
    inv_freq = ROPE_THETA ** (-jnp.arange(half, dtype=F32) / half)
    ang = jnp.arange(seq).astype(F32)[:, None] * inv_freq[None, :]
    cos2 = jnp.tile(jnp.cos(ang), (1, 2))
    sin2 = jnp.tile(jnp.sin(ang), (1, 2))
    pad = jnp.zeros((seq, MLA_HEAD_PAD - QK_NOPE - QK_ROPE), F32)
    cos_t = jnp.concatenate([jnp.ones((seq, QK_NOPE), F32), cos2, pad], axis=1)
    sin_t = jnp.concatenate([jnp.zeros((seq, QK_NOPE), F32), sin2, pad], axis=1)
    return cos_t, sin_t


def _pad_heads(w_nope, w_rope):
    k, h, _ = w_nope.shape
    if w_rope is None:
        w_rope = jnp.zeros((k, h, QK_ROPE), w_nope.dtype)
    pad = jnp.zeros((k, h, MLA_HEAD_PAD - QK_NOPE - QK_ROPE), w_nope.dtype)
    return jnp.concatenate([w_nope, w_rope, pad], axis=-1).reshape(k, h * MLA_HEAD_PAD)


def kernel(x, w_in, rel_bias, mla_q_norm, w_uq, mla_kv_norm, w_uk, w_uv, w_proj_a, w_proj_b,
           w_out, norm_attn, norm_mlp, w_mlp_up, w_mlp_down, norm_final):
    b, seq, d = x.shape
    assert d == D_MODEL and seq % ROW_TILE == 0 and seq // MOBA_BLOCK <= LANES
    assert w_in.shape[0] == 1, "single-layer block"
    x2 = x.reshape(b * seq, d)

    w_t = jnp.swapaxes(w_in[0], 0, 1)
    o_k = A_WIDTH
    o_v = 2 * A_WIDTH
    o_cq = 3 * A_WIDTH
    o_ckv = o_cq + Q_LORA
    o_kr = o_ckv + KV_LORA
    o_g = o_kr + QK_ROPE
    wk = w_t[o_k:o_v].astype(BF16)
    wqv = jnp.concatenate([w_t[:o_k], w_t[o_v:o_cq]], axis=0).astype(BF16)
    w_kr = w_t[o_kr:o_g]
    half = QK_ROPE // 2
    w_kr_rot = jnp.concatenate([-w_kr[half:], w_kr[:half]], axis=0)
    row_pad = lambda a: jnp.pad(a, ((QK_NOPE, MLA_HEAD_PAD - QK_NOPE - QK_ROPE), (0, 0)))
    wc = jnp.concatenate(
        [w_t[o_cq:o_kr], row_pad(w_kr), row_pad(w_kr_rot)], axis=0).astype(BF16)
    wg = w_t[o_g:].astype(BF16)

    uq = w_uq[0].reshape(Q_LORA, B_HEADS, QK_NOPE + QK_ROPE)
    uq_nope, uq_rope = uq[..., :QK_NOPE], uq[..., QK_NOPE:]
    wuq = _pad_heads(uq_nope, uq_rope).T.astype(BF16)
    wuk = _pad_heads(w_uk[0].reshape(KV_LORA, B_HEADS, QK_NOPE), None).astype(BF16)
    wuv = w_uv[0].T.astype(BF16)

    cos_t, sin_t = _rope_tables(seq)
    q_scale = (QK_NOPE + QK_ROPE) ** -0.5 * LOG2E
    row2 = lambda a: a.reshape(1, -1)

    qat, ka, vat, kmean, qmt, km, vbt = _inproj(
        x2, row2(norm_attn[0]), wk, wqv, wc, row2(mla_q_norm[0]), row2(mla_kv_norm[0]),
        wuq, wuk, wuv, (cos_t * q_scale).T, (sin_t * q_scale).T, cos_t, sin_t, b, seq)

    as_seq = lambda a: a.reshape(b, seq, a.shape[-1])
    bias = _bias_tiles(rel_bias)
    oa = _moba(qat, as_seq(ka), vat, kmean.reshape(b, seq // MOBA_BLOCK, A_WIDTH), bias)
    ob = _mla(qmt, as_seq(km), vbt)

    out = _out_block(
        x2, oa.reshape(-1, A_WIDTH, ATTN_TILE), ob.reshape(-1, B_WIDTH, ATTN_TILE),
        row2(norm_attn[0]),
        wg, w_proj_a[0].astype(BF16), w_proj_b[0].astype(BF16), w_out[0].astype(BF16),
        row2(norm_mlp[0]), w_mlp_up[0].astype(BF16), w_mlp_down[0].astype(BF16),
        row2(norm_final))
    return out.reshape(b, seq, d)
```

```python
import functools
import math

import jax
import jax.numpy as jnp
from jax import lax
from jax.experimental import pallas as pl
from jax.experimental.pallas import tpu as pltpu

D_MODEL = 1024
A_HEADS = 8
A_HEAD_DIM = 64
A_WIDTH = A_HEADS * A_HEAD_DIM
MOBA_BLOCK = 256
MOBA_TOPK = 3
REL_BUCKETS = 32
REL_MAX_DIST = 128
B_HEADS = 8
QK_NOPE = 64
QK_ROPE = 32
V_HEAD = 64
B_WIDTH = B_HEADS * V_HEAD
Q_LORA = 384
KV_LORA = 256
ROPE_THETA = 10000.0
D_FF = 4 * D_MODEL
EPS = 1e-6
NEG = -1e30
LOG2E = math.log2(math.e)

LANES = 128
MLA_HEAD_PAD = 128
ATTN_TILE = 256
HEAD_GROUP = 4
SUM_ROWS = 16
ITEM_SPAN = 2
PIPELINE_UNROLL = 32
ROW_TILE = 512
FF_CHUNK = 1024
VMEM_LIMIT = 56 * 1024 * 1024

F32 = jnp.float32
BF16 = jnp.bfloat16
NT_DIMS = (((1,), (1,)), ((), ()))


def _rms(xf, g):
    y = xf * lax.rsqrt(jnp.mean(xf * xf, axis=-1, keepdims=True) + EPS)
    return y * g


def _const_spec(shape):
    zeros = (0,) * len(shape)
    return pl.BlockSpec(shape, lambda *_: zeros, pipeline_mode=pl.Buffered(1))


def _bias_kernel(rel_ref, o_ref):
    h = pl.program_id(0)
    shape = (MOBA_BLOCK, 2 * MOBA_BLOCK)
    r = lax.broadcasted_iota(jnp.int32, shape, 0)
    c = lax.broadcasted_iota(jnp.int32, shape, 1)
    d = c - r
    dist = jnp.maximum(d, 0)
    max_exact = REL_BUCKETS // 2
    df = jnp.maximum(dist, 1).astype(F32)
    large = max_exact + (jnp.log(df / max_exact) / math.log(REL_MAX_DIST / max_exact)
                         * (REL_BUCKETS - max_exact)).astype(jnp.int32)
    large = jnp.minimum(large, REL_BUCKETS - 1)
    bucket = jnp.where(dist < max_exact, dist, large)
    val = jnp.zeros(shape, F32)
    for b in range(REL_BUCKETS):
        val = jnp.where(bucket == b, rel_ref[b, h], val)
    o_ref[...] = jnp.where(d >= 0, (val - rel_ref[REL_BUCKETS - 1, h]) * LOG2E, NEG)


def _bias_tiles(rel_bias):
    return pl.pallas_call(
        _bias_kernel,
        grid=(A_HEADS,),
        in_specs=[pl.BlockSpec(memory_space=pltpu.SMEM)],
        out_specs=pl.BlockSpec((None, MOBA_BLOCK, 2 * MOBA_BLOCK), lambda h: (h, 0, 0)),
        out_shape=jax.ShapeDtypeStruct((A_HEADS, MOBA_BLOCK, 2 * MOBA_BLOCK), F32),
        name="moba_bias_tiles",
    )(rel_bias)


def _inproj_kernel(x_ref, g_ref, wk_ref, wqv_ref, wc_ref, qn_ref, kvn_ref, wuq_ref,
                   wuk_ref, wuv_ref, cqt_ref, sqt_ref, ck_ref, sk_ref,
                   qat_ref, ka_ref, vat_ref, kmean_ref, qmt_ref, km_ref, vbt_ref):
    n = _rms(x_ref[...], g_ref[...]).astype(BF16)
    k = lax.dot_general(n, wk_ref[...], NT_DIMS, preferred_element_type=F32)
    ka_ref[...] = k.astype(BF16)
    nblk = ROW_TILE // ATTN_TILE
    kmean_ref[0] = jnp.sum(k.reshape(nblk, MOBA_BLOCK, A_WIDTH), axis=1) * (1.0 / MOBA_BLOCK)

    c = lax.dot_general(n, wc_ref[...], NT_DIMS, preferred_element_type=F32)
    cq = _rms(c[:, :Q_LORA], qn_ref[...]).astype(BF16)
    ckv = _rms(c[:, Q_LORA:Q_LORA + KV_LORA], kvn_ref[...]).astype(BF16)
    kr = c[:, Q_LORA + KV_LORA:Q_LORA + KV_LORA + LANES]
    krr = c[:, Q_LORA + KV_LORA + LANES:]
    for t in range(nblk):
        rows = slice(t * ATTN_TILE, (t + 1) * ATTN_TILE)
        qv_t = lax.dot_general(wqv_ref[...], n[rows], NT_DIMS,
                               preferred_element_type=F32)
        qat_ref[t] = (qv_t[:A_WIDTH] * (A_HEAD_DIM ** -0.5 * LOG2E)).astype(BF16)
        vat_ref[t] = qv_t[A_WIDTH:].astype(BF16)
        vbt_ref[t] = lax.dot_general(wuv_ref[...], ckv[rows], NT_DIMS,
                                     preferred_element_type=F32).astype(BF16)
        q_t = lax.dot_general(wuq_ref[...], cq[rows], NT_DIMS,
                              preferred_element_type=F32)
        cq_t, sq_t = cqt_ref[:, rows], sqt_ref[:, rows]
        half = QK_ROPE // 2
        for h in range(B_HEADS):
            q_h = q_t[h * MLA_HEAD_PAD:(h + 1) * MLA_HEAD_PAD]
            x1 = q_h[QK_NOPE:QK_NOPE + half]
            x2 = q_h[QK_NOPE + half:QK_NOPE + QK_ROPE]
            rot_h = jnp.concatenate(
                [jnp.zeros((QK_NOPE, ATTN_TILE), F32), -x2, x1,
                 jnp.zeros((MLA_HEAD_PAD - QK_NOPE - QK_ROPE, ATTN_TILE), F32)], axis=0)
            qmt_ref[t, h * MLA_HEAD_PAD:(h + 1) * MLA_HEAD_PAD, :] = (
                q_h * cq_t + rot_h * sq_t).astype(BF16)
    kn = jnp.dot(ckv, wuk_ref[...], preferred_element_type=F32)
    k_rope = kr * ck_ref[...] + krr * sk_ref[...]
    for h in range(B_HEADS):
        sl = slice(h * MLA_HEAD_PAD, (h + 1) * MLA_HEAD_PAD)
        km_ref[:, sl] = (kn[:, sl] + k_rope).astype(BF16)


def _inproj(x2, g_attn, wk, wqv, wc, qn, kvn, wuq, wuk, wuv, cq_tt, sq_tt, ck_t, sk_t,
            batch, seq):
    n_rows = x2.shape[0]
    tm = ROW_TILE
    steps = n_rows // tm
    per_seq = seq // tm
    nblk = tm // ATTN_TILE
    row = lambda i: (i, 0)
    tab = lambda i: (i % per_seq, 0)
    tab_t = lambda i: (0, i % per_seq)
    tile_map = lambda i: (i // per_seq, i % per_seq, 0, 0)
    bf = lambda w: jax.ShapeDtypeStruct((n_rows, w), BF16)
    tiles = lambda w: jax.ShapeDtypeStruct((batch, seq // ATTN_TILE, w, ATTN_TILE), BF16)
    tile_spec = lambda w: pl.BlockSpec((None, nblk, w, ATTN_TILE), tile_map)
    return pl.pallas_call(
        _inproj_kernel,
        grid=(steps,),
        in_specs=[
            pl.BlockSpec((tm, D_MODEL), row),
            _const_spec(g_attn.shape), _const_spec(wk.shape), _const_spec(wqv.shape),
            _const_spec(wc.shape), _const_spec(qn.shape), _const_spec(kvn.shape),
            _const_spec(wuq.shape), _const_spec(wuk.shape),
            _const_spec(wuv.shape),
            pl.BlockSpec((LANES, tm), tab_t), pl.BlockSpec((LANES, tm), tab_t),
            pl.BlockSpec((tm, LANES), tab), pl.BlockSpec((tm, LANES), tab),
        ],
        out_specs=[
            tile_spec(A_WIDTH), pl.BlockSpec((tm, A_WIDTH), row), tile_spec(A_WIDTH),
            pl.BlockSpec((1, nblk, A_WIDTH), lambda i: (i, 0, 0)),
            tile_spec(B_HEADS * MLA_HEAD_PAD),
            pl.BlockSpec((tm, B_HEADS * MLA_HEAD_PAD), row),
            tile_spec(B_WIDTH),
        ],
        out_shape=[
            tiles(A_WIDTH), bf(A_WIDTH), tiles(A_WIDTH),
            jax.ShapeDtypeStruct((steps, nblk, A_WIDTH), F32),
            tiles(B_HEADS * MLA_HEAD_PAD), bf(B_HEADS * MLA_HEAD_PAD), tiles(B_WIDTH),
        ],
        compiler_params=pltpu.CompilerParams(
            dimension_semantics=("arbitrary",), vmem_limit_bytes=VMEM_LIMIT),
        name="in_proj",
    )(x2, g_attn, wk, wqv, wc, qn, kvn, wuq, wuk, wuv, cq_tt, sq_tt, ck_t, sk_t)


def _store_scores(buf, h, parts):
    s_ref, max_ref = buf
    row = 0
    for idx, s_t in enumerate(parts):
        s_ref[h, row:row + s_t.shape[0]] = s_t
        max_ref[h, idx] = jnp.max(s_t, axis=0, keepdims=True)
        row += s_t.shape[0]


def _softmax_step(buf, h, parts, m_ref, acc_ref, first):
    s_ref, max_ref = buf
    maxes = [max_ref[h, idx] for idx in range(len(parts))]
    m_new = None if first else m_ref[...]
    for s_max, (_, _, col_keep) in zip(maxes, parts):
        if col_keep is not None:
            s_max = jnp.where(col_keep > 0, s_max, NEG)
        m_new = s_max if m_new is None else jnp.maximum(m_new, s_max)
    pv = None
    row = 0
    for s_max, (rows, v_t, col_keep) in zip(maxes, parts):
        ones = (lax.broadcasted_iota(jnp.int32, (SUM_ROWS, rows), 0) == 0).astype(BF16)
        v_aug = jnp.concatenate([v_t, ones], axis=0)
        m_exp = m_new if col_keep is None else jnp.maximum(m_new, s_max)
        p_t = jnp.exp2(s_ref[h, row:row + rows] - m_exp).astype(BF16)
        part = jnp.dot(v_aug, p_t, preferred_element_type=F32)
        if col_keep is not None:
            part = part * col_keep
        pv = part if pv is None else pv + part
        row += rows
    if first:
        acc_ref[...] = pv
    else:
        acc_ref[...] = jnp.exp2(m_ref[...] - m_new) * acc_ref[...] + pv
    m_ref[...] = m_new


def _pipeline(n_items, score, attend, bufs, unroll):
    if n_items == 0:
        return
    score(0, bufs[0])
    n_loops = (n_items - 1) // unroll

    def body(p, carry):
        for t in range(unroll):
            n = unroll * p + t
            score(n + 1, bufs[(t + 1) % 2])
            attend(n, bufs[t % 2])
        return carry

    lax.fori_loop(0, n_loops, body, 0)
    done = unroll * n_loops
    for n in range(done, n_items):
        if n + 1 < n_items:
            score(n + 1, bufs[(n + 1 - done) % 2])
        attend(n, bufs[(n - done) % 2])


def _tile_rows(j, span=1):
    return pl.ds(pl.multiple_of(j * ATTN_TILE, ATTN_TILE), span * ATTN_TILE)


def _write_heads(acc_ref, o_ref, nq, head_dim):
    def one(qi, carry):
        for h in range(HEAD_GROUP):
            acc = acc_ref[qi, h]
            o_ref[qi, h * head_dim:(h + 1) * head_dim, :] = (
                acc[:head_dim] / acc[head_dim:head_dim + 1]).astype(o_ref.dtype)
        return carry

    lax.fori_loop(0, nq, one, 0)


def _causal_items(nq, n_near, pair_far):
    phases = [[(i, i - d) for i in range(d, nq)] for d in range(n_near)]
    pairs, singles = [], []
    for i in range(n_near, nq):
        n_far = i - n_near + 1
        n_paired = n_far - n_far % 2 if pair_far else 0
        pairs += [(i, j) for j in range(0, n_paired, 2)]
        singles += [(i, j) for j in range(n_paired, n_far)]
    phases += [pairs, singles]
    flat = [it for ph in phases for it in ph]
    tq = jnp.asarray([it[0] for it in flat], jnp.int32)
    tk = jnp.asarray([it[1] for it in flat], jnp.int32)
    return tq, tk, [len(ph) for ph in phases]


def _moba_kernel(tq_ref, tk_ref, qt_ref, k_ref, vt_ref, kmean_ref, bias_ref, o_ref,
                 keep_ref, s0_ref, s1_ref, t0_ref, t1_ref, m_ref, acc_ref, *, seq, phases):
    nblk = seq // MOBA_BLOCK

    def head_query(qi, h):
        pr, sub = divmod(h, 2)
        q_h = qt_ref[qi, pl.ds(pr * LANES + sub * A_HEAD_DIM, A_HEAD_DIM), :]
        zeros = jnp.zeros_like(q_h)
        return jnp.concatenate([q_h, zeros] if sub == 0 else [zeros, q_h], axis=0)

    def select_blocks(i):
        row = lax.broadcasted_iota(jnp.int32, (nblk, ATTN_TILE), 0)
        rowf = row.astype(F32)
        past = row < i
        for h in range(HEAD_GROUP):
            kmean = kmean_ref[:, (h // 2) * LANES:(h // 2 + 1) * LANES].astype(BF16)
            gate = jnp.dot(kmean, head_query(i, h), preferred_element_type=F32)
            gate = jnp.where(past, gate, -jnp.inf)
            picked = jnp.zeros((nblk, ATTN_TILE), F32)
            for _ in range(MOBA_TOPK):
                top = jnp.max(gate, axis=0, keepdims=True)
                first = jnp.min(jnp.where(gate == top, rowf, float(nblk)), axis=0, keepdims=True)
                pick = rowf == first
                picked = jnp.where(pick, 1.0, picked)
                gate = jnp.where(pick, -jnp.inf, gate)
            keep_ref[i, h] = jnp.where(past, picked, 0.0)

    def stage(offset, span, bias_cols, first):
        def score(n, buf):
            qi, kj = tq_ref[offset + n], tk_ref[offset + n]
            for h in range(HEAD_GROUP):
                k_t = k_ref[_tile_rows(kj, span), (h // 2) * LANES:(h // 2 + 1) * LANES]
                s_t = jnp.dot(k_t, head_query(qi, h), preferred_element_type=F32)
                if bias_cols is not None:
                    s_t = s_t + bias_ref[h, :, bias_cols]
                _store_scores(buf, h, [s_t[t * MOBA_BLOCK:(t + 1) * MOBA_BLOCK]
                                       for t in range(span)])

        def attend(n, buf):
            qi, kj = tq_ref[offset + n], tk_ref[offset + n]
            for h in range(HEAD_GROUP):
                parts = [(MOBA_BLOCK, vt_ref[kj + t, h * A_HEAD_DIM:(h + 1) * A_HEAD_DIM, :],
                          None if first else keep_ref[qi, h, pl.ds(kj + t, 1), :])
                         for t in range(span)]
                _softmax_step(buf, h, parts, m_ref.at[qi, h], acc_ref.at[qi, h], first)
            if first:
                select_blocks(qi)

        return score, attend

    bufs = ((s0_ref, t0_ref), (s1_ref, t1_ref))
    n_own, n_prev, n_pairs, n_single = phases
    _pipeline(n_own, *stage(0, 1, slice(0, MOBA_BLOCK), True), bufs, PIPELINE_UNROLL)
    offset = n_own
    _pipeline(n_prev, *stage(offset, 1, slice(MOBA_BLOCK, 2 * MOBA_BLOCK), False), bufs,
              PIPELINE_UNROLL)
    offset += n_prev
    _pipeline(n_pairs, *stage(offset, 2, None, False), bufs, PIPELINE_UNROLL // 2)
    offset += n_pairs
    _pipeline(n_single, *stage(offset, 1, None, False), bufs, PIPELINE_UNROLL)
    _write_heads(acc_ref, o_ref, nblk, A_HEAD_DIM)


def _moba(qat, ka, vat, kmean, bias):
    b, seq, _ = ka.shape
    nq = seq // ATTN_TILE
    gw = HEAD_GROUP * A_HEAD_DIM
    tq, tk, phases = _causal_items(nq, 2, pair_far=True)
    per_group = lambda bi, g, *_: (bi, 0, g)
    tiles = lambda bi, g, *_: (bi, 0, g, 0)
    return pl.pallas_call(
        functools.partial(_moba_kernel, seq=seq, phases=phases),
        grid_spec=pltpu.PrefetchScalarGridSpec(
            num_scalar_prefetch=2,
            grid=(b, A_HEADS // HEAD_GROUP),
            in_specs=[
                pl.BlockSpec((None, nq, gw, ATTN_TILE), tiles),
                pl.BlockSpec((None, seq, gw), per_group),
                pl.BlockSpec((None, nq, gw, MOBA_BLOCK), tiles),
                pl.BlockSpec((None, nq, gw), per_group),
                pl.BlockSpec((HEAD_GROUP, MOBA_BLOCK, 2 * MOBA_BLOCK), lambda bi, g, *_: (g, 0, 0)),
            ],
            out_specs=pl.BlockSpec((None, nq, gw, ATTN_TILE), tiles),
            scratch_shapes=[
                pltpu.VMEM((nq, HEAD_GROUP, nq, ATTN_TILE), F32),
                pltpu.VMEM((HEAD_GROUP, ITEM_SPAN * ATTN_TILE, ATTN_TILE), F32),
                pltpu.VMEM((HEAD_GROUP, ITEM_SPAN * ATTN_TILE, ATTN_TILE), F32),
                pltpu.VMEM((HEAD_GROUP, ITEM_SPAN, 1, ATTN_TILE), F32),
                pltpu.VMEM((HEAD_GROUP, ITEM_SPAN, 1, ATTN_TILE), F32),
                pltpu.VMEM((nq, HEAD_GROUP, 1, ATTN_TILE), F32),
                pltpu.VMEM((nq, HEAD_GROUP, A_HEAD_DIM + SUM_ROWS, ATTN_TILE), F32),
            ]),
        out_shape=jax.ShapeDtypeStruct((b, nq, A_WIDTH, ATTN_TILE), BF16),
        compiler_params=pltpu.CompilerParams(
            dimension_semantics=("arbitrary", "arbitrary"), vmem_limit_bytes=VMEM_LIMIT),
        name="moba_attention",
    )(tq, tk, qat, ka, vat, kmean, bias)


def _mla_kernel(tq_ref, tk_ref, qt_ref, k_ref, vt_ref, o_ref, s0_ref, s1_ref, t0_ref, t1_ref,
                m_ref, acc_ref, *, nq, phases):
    def stage(offset, span, diagonal):
        def score(n, buf):
            qi, kj = tq_ref[offset + n], tk_ref[offset + n]
            for h in range(HEAD_GROUP):
                sl = slice(h * MLA_HEAD_PAD, (h + 1) * MLA_HEAD_PAD)
                s_t = jnp.dot(k_ref[_tile_rows(kj, span), sl], qt_ref[qi, sl, :],
                              preferred_element_type=F32)
                if diagonal:
                    key = lax.broadcasted_iota(jnp.int32, (ATTN_TILE, ATTN_TILE), 0)
                    qry = lax.broadcasted_iota(jnp.int32, (ATTN_TILE, ATTN_TILE), 1)
                    s_t = jnp.where(key <= qry, s_t, NEG)
                _store_scores(buf, h, [s_t[t * ATTN_TILE:(t + 1) * ATTN_TILE]
                                       for t in range(span)])

        def attend(n, buf):
            qi, kj = tq_ref[offset + n], tk_ref[offset + n]
            for h in range(HEAD_GROUP):
                parts = [(ATTN_TILE, vt_ref[kj + t, h * V_HEAD:(h + 1) * V_HEAD, :], None)
                         for t in range(span)]
                _softmax_step(buf, h, parts, m_ref.at[qi, h], acc_ref.at[qi, h], diagonal)

        return score, attend

    bufs = ((s0_ref, t0_ref), (s1_ref, t1_ref))
    n_diag, n_pairs, n_single = phases
    _pipeline(n_diag, *stage(0, 1, True), bufs, PIPELINE_UNROLL)
    _pipeline(n_pairs, *stage(n_diag, 2, False), bufs, PIPELINE_UNROLL // 2)
    _pipeline(n_single, *stage(n_diag + n_pairs, 1, False), bufs, PIPELINE_UNROLL)
    _write_heads(acc_ref, o_ref, nq, V_HEAD)


def _mla(qmt, km, vbt):
    b, seq, _ = km.shape
    nq = seq // ATTN_TILE
    qw = HEAD_GROUP * MLA_HEAD_PAD
    vw = HEAD_GROUP * V_HEAD
    tq, tk, phases = _causal_items(nq, 1, pair_far=False)
    per_group = lambda bi, g, *_: (bi, 0, g)
    tiles = lambda bi, g, *_: (bi, 0, g, 0)
    return pl.pallas_call(
        functools.partial(_mla_kernel, nq=nq, phases=phases),
        grid_spec=pltpu.PrefetchScalarGridSpec(
            num_scalar_prefetch=2,
            grid=(b, B_HEADS // HEAD_GROUP),
            in_specs=[
                pl.BlockSpec((None, nq, qw, ATTN_TILE), tiles),
                pl.BlockSpec((None, seq, qw), per_group),
                pl.BlockSpec((None, nq, vw, ATTN_TILE), tiles),
            ],
            out_specs=pl.BlockSpec((None, nq, vw, ATTN_TILE), tiles),
            scratch_shapes=[
                pltpu.VMEM((HEAD_GROUP, ITEM_SPAN * ATTN_TILE, ATTN_TILE), F32),
                pltpu.VMEM((HEAD_GROUP, ITEM_SPAN * ATTN_TILE, ATTN_TILE), F32),
                pltpu.VMEM((HEAD_GROUP, ITEM_SPAN, 1, ATTN_TILE), F32),
                pltpu.VMEM((HEAD_GROUP, ITEM_SPAN, 1, ATTN_TILE), F32),
                pltpu.VMEM((nq, HEAD_GROUP, 1, ATTN_TILE), F32),
                pltpu.VMEM((nq, HEAD_GROUP, V_HEAD + SUM_ROWS, ATTN_TILE), F32),
            ]),
        out_shape=jax.ShapeDtypeStruct((b, nq, B_WIDTH, ATTN_TILE), BF16),
        compiler_params=pltpu.CompilerParams(
            dimension_semantics=("arbitrary", "arbitrary"), vmem_limit_bytes=VMEM_LIMIT),
        name="mla_attention",
    )(tq, tk, qmt, km, vbt)


def _out_kernel(x_ref, oa_ref, ob_ref, gattn_ref, wg_ref, wpa_ref, wpb_ref, wout_ref,
                gmlp_ref, wup_ref, wdn_ref, gfin_ref, o_ref):
    x = x_ref[...]
    n = _rms(x, gattn_ref[...]).astype(BF16)
    gates = lax.dot_general(n, wg_ref[...], NT_DIMS, preferred_element_type=F32)
    tn_dims = (((0,), (0,)), ((), ()))
    tiles = range(ROW_TILE // ATTN_TILE)
    pa = jnp.concatenate([lax.dot_general(oa_ref[t], wpa_ref[...], tn_dims,
                                          preferred_element_type=F32) for t in tiles], axis=0)
    pb = jnp.concatenate([lax.dot_general(ob_ref[t], wpb_ref[...], tn_dims,
                                          preferred_element_type=F32) for t in tiles], axis=0)
    merged = (jax.nn.sigmoid(gates[:, :D_MODEL]) * pa
              + jax.nn.sigmoid(gates[:, D_MODEL:]) * pb).astype(BF16)
    h = x + jnp.dot(merged, wout_ref[...], preferred_element_type=F32)
    m = _rms(h, gmlp_ref[...]).astype(BF16)
    for c in range(D_FF // FF_CHUNK):
        cols = slice(c * FF_CHUNK, (c + 1) * FF_CHUNK)
        up = jnp.dot(m, wup_ref[:, cols], preferred_element_type=F32)
        act = jnp.square(jnp.maximum(up, 0.0)).astype(BF16)
        h = h + jnp.dot(act, wdn_ref[cols, :], preferred_element_type=F32)
    o_ref[...] = _rms(h, gfin_ref[...])


def _out_block(x2, oa, ob, g_attn, wg, wpa, wpb, wout, g_mlp, wup, wdn, g_fin):
    n_rows = x2.shape[0]
    tm = ROW_TILE
    row = lambda i: (i, 0)
    return pl.pallas_call(
        _out_kernel,
        grid=(n_rows // tm,),
        in_specs=[
            pl.BlockSpec((tm, D_MODEL), row),
            pl.BlockSpec((tm // ATTN_TILE, A_WIDTH, ATTN_TILE), lambda i: (i, 0, 0)),
            pl.BlockSpec((tm // ATTN_TILE, B_WIDTH, ATTN_TILE), lambda i: (i, 0, 0)),
            _const_spec(g_attn.shape), _const_spec(wg.shape), _const_spec(wpa.shape),
            _const_spec(wpb.shape), _const_spec(wout.shape), _const_spec(g_mlp.shape),
            _const_spec(wup.shape), _const_spec(wdn.shape), _const_spec(g_fin.shape),
        ],
        out_specs=pl.BlockSpec((tm, D_MODEL), row),
        out_shape=jax.ShapeDtypeStruct((n_rows, D_MODEL), F32),
        compiler_params=pltpu.CompilerParams(
            dimension_semantics=("arbitrary",), vmem_limit_bytes=VMEM_LIMIT),
        name="out_mlp",
    )(x2, oa, ob, g_attn, wg, wpa, wpb, wout, g_mlp, wup, wdn, g_fin)


def _rope_tables(seq):
    half = QK_ROPE // 2
    inv_freq = ROPE_THETA ** (-jnp.arange(half, dtype=F32) / half)
    ang = jnp.arange(seq).astype(F32)[:, None] * inv_freq[None, :]
    cos2 = jnp.tile(jnp.cos(ang), (1, 2))
    sin2 = jnp.tile(jnp.sin(ang), (1, 2))
    pad = jnp.zeros((seq, MLA_HEAD_PAD - QK_NOPE - QK_ROPE), F32)
    cos_t = jnp.concatenate([jnp.ones((seq, QK_NOPE), F32), cos2, pad], axis=1)
    sin_t = jnp.concatenate([jnp.zeros((seq, QK_NOPE), F32), sin2, pad], axis=1)
    return cos_t, sin_t


def _pad_heads(w_nope, w_rope):
    k, h, _ = w_nope.shape
    if w_rope is None:
        w_rope = jnp.zeros((k, h, QK_ROPE), w_nope.dtype)
    pad = jnp.zeros((k, h, MLA_HEAD_PAD - QK_NOPE - QK_ROPE), w_nope.dtype)
    return jnp.concatenate([w_nope, w_rope, pad], axis=-1).reshape(k, h * MLA_HEAD_PAD)


def kernel(x, w_in, rel_bias, mla_q_norm, w_uq, mla_kv_norm, w_uk, w_uv, w_proj_a, w_proj_b,
           w_out, norm_attn, norm_mlp, w_mlp_up, w_mlp_down, norm_final):
    b, seq, d = x.shape
    assert d == D_MODEL and seq % ROW_TILE == 0 and seq // MOBA_BLOCK <= LANES
    assert w_in.shape[0] == 1, "single-layer block"
    x2 = x.reshape(b * seq, d)

    w_t = jnp.swapaxes(w_in[0], 0, 1)
    o_k = A_WIDTH
    o_v = 2 * A_WIDTH
    o_cq = 3 * A_WIDTH
    o_ckv = o_cq + Q_LORA
    o_kr = o_ckv + KV_LORA
    o_g = o_kr + QK_ROPE
    wk = w_t[o_k:o_v].astype(BF16)
    wqv = jnp.concatenate([w_t[:o_k], w_t[o_v:o_cq]], axis=0).astype(BF16)
    w_kr = w_t[o_kr:o_g]
    half = QK_ROPE // 2
    w_kr_rot = jnp.concatenate([-w_kr[half:], w_kr[:half]], axis=0)
    row_pad = lambda a: jnp.pad(a, ((QK_NOPE, MLA_HEAD_PAD - QK_NOPE - QK_ROPE), (0, 0)))
    wc = jnp.concatenate(
        [w_t[o_cq:o_kr], row_pad(w_kr), row_pad(w_kr_rot)], axis=0).astype(BF16)
    wg = w_t[o_g:].astype(BF16)

    uq = w_uq[0].reshape(Q_LORA, B_HEADS, QK_NOPE + QK_ROPE)
    uq_nope, uq_rope = uq[..., :QK_NOPE], uq[..., QK_NOPE:]
    wuq = _pad_heads(uq_nope, uq_rope).T.astype(BF16)
    wuk = _pad_heads(w_uk[0].reshape(KV_LORA, B_HEADS, QK_NOPE), None).astype(BF16)
    wuv = w_uv[0].T.astype(BF16)

    cos_t, sin_t = _rope_tables(seq)
    q_scale = (QK_NOPE + QK_ROPE) ** -0.5 * LOG2E
    row2 = lambda a: a.reshape(1, -1)

    qat, ka, vat, kmean, qmt, km, vbt = _inproj(
        x2, row2(norm_attn[0]), wk, wqv, wc, row2(mla_q_norm[0]), row2(mla_kv_norm[0]),
        wuq, wuk, wuv, (cos_t * q_scale).T, (sin_t * q_scale).T, cos_t, sin_t, b, seq)

    as_seq = lambda a: a.reshape(b, seq, a.shape[-1])
    bias = _bias_tiles(rel_bias)
    oa = _moba(qat, as_seq(ka), vat, kmean.reshape(b, seq // MOBA_BLOCK, A_WIDTH), bias)
    ob = _mla(qmt, as_seq(km), vbt)

    out = _out_block(
        x2, oa.reshape(-1, A_WIDTH, ATTN_TILE), ob.reshape(-1, B_WIDTH, ATTN_TILE),
        row2(norm_attn[0]),
        wg, w_proj_a[0].astype(BF16), w_proj_b[0].astype(BF16), w_out[0].astype(BF16),
        row2(norm_mlp[0]), w_mlp_up[0].astype(BF16), w_mlp_down[0].astype(BF16),
        row2(norm_final))
    return out.reshape(b, seq, d)
```

```python
import functools
import math

import jax
import jax.numpy as jnp
from jax import lax
from jax.experimental import pallas as pl
from jax.experimental.pallas import tpu as pltpu

D_MODEL = 1024
A_HEADS = 8
A_HEAD_DIM = 64
A_WIDTH = A_HEADS * A_HEAD_DIM
MOBA_BLOCK = 256
MOBA_TOPK = 3
REL_BUCKETS = 32
REL_MAX_DIST = 128
B_HEADS = 8
QK_NOPE = 64
QK_ROPE = 32
V_HEAD = 64
B_WIDTH = B_HEADS * V_HEAD
Q_LORA = 384
KV_LORA = 256
ROPE_THETA = 10000.0
D_FF = 4 * D_MODEL
EPS = 1e-6
NEG = -1e30
LOG2E = math.log2(math.e)

LANES = 128
MLA_HEAD_PAD = 128
ATTN_TILE = 256
HEAD_GROUP = 4
SUM_ROWS = 16
ITEM_SPAN = 2
PIPELINE_UNROLL = 16
ROW_TILE = 512
OUT_ROW_TILE = 1024
FF_CHUNK = 512
VMEM_LIMIT = 60 * 1024 * 1024

F32 = jnp.float32
BF16 = jnp.bfloat16
NT_DIMS = (((1,), (1,)), ((), ()))


def _rms(xf, g):
    y = xf * lax.rsqrt(jnp.mean(xf * xf, axis=-1, keepdims=True) + EPS)
    return y * g


def _const_spec(shape):
    zeros = (0,) * len(shape)
    return pl.BlockSpec(shape, lambda *_: zeros, pipeline_mode=pl.Buffered(1))


def _bias_kernel(rel_ref, o_ref):
    h = pl.program_id(0)
    shape = (MOBA_BLOCK, 2 * MOBA_BLOCK)
    r = lax.broadcasted_iota(jnp.int32, shape, 0)
    c = lax.broadcasted_iota(jnp.int32, shape, 1)
    d = c - r
    dist = jnp.maximum(d, 0)
    max_exact = REL_BUCKETS // 2
    df = jnp.maximum(dist, 1).astype(F32)
    large = max_exact + (jnp.log(df / max_exact) / math.log(REL_MAX_DIST / max_exact)
                         * (REL_BUCKETS - max_exact)).astype(jnp.int32)
    large = jnp.minimum(large, REL_BUCKETS - 1)
    bucket = jnp.where(dist < max_exact, dist, large)
    val = jnp.zeros(shape, F32)
    for b in range(REL_BUCKETS):
        val = jnp.where(bucket == b, rel_ref[b, h], val)
    o_ref[...] = jnp.where(d >= 0, (val - rel_ref[REL_BUCKETS - 1, h]) * LOG2E, NEG)


def _bias_tiles(rel_bias):
    return pl.pallas_call(
        _bias_kernel,
        grid=(A_HEADS,),
        in_specs=[pl.BlockSpec(memory_space=pltpu.SMEM)],
        out_specs=pl.BlockSpec((None, MOBA_BLOCK, 2 * MOBA_BLOCK), lambda h: (h, 0, 0)),
        out_shape=jax.ShapeDtypeStruct((A_HEADS, MOBA_BLOCK, 2 * MOBA_BLOCK), F32),
        name="moba_bias_tiles",
    )(rel_bias)


def _inproj_kernel(x_ref, g_ref, wk_ref, wqv_ref, wc_ref, qn_ref, kvn_ref, wuq_ref,
                   wuk_ref, wuv_ref, cqt_ref, sqt_ref, ck_ref, sk_ref,
                   qat_ref, ka_ref, vat_ref, kmean_ref, qmt_ref, km_ref, vbt_ref):
    n = _rms(x_ref[...], g_ref[...]).astype(BF16)
    k = lax.dot_general(n, wk_ref[...], NT_DIMS, preferred_element_type=F32)
    ka_ref[...] = k.astype(BF16)
    nblk = ROW_TILE // ATTN_TILE
    kmean_ref[0] = jnp.sum(k.reshape(nblk, MOBA_BLOCK, A_WIDTH), axis=1) * (1.0 / MOBA_BLOCK)

    c = lax.dot_general(n, wc_ref[...], NT_DIMS, preferred_element_type=F32)
    cq = _rms(c[:, :Q_LORA], qn_ref[...]).astype(BF16)
    ckv = _rms(c[:, Q_LORA:Q_LORA + KV_LORA], kvn_ref[...]).astype(BF16)
    kr = c[:, Q_LORA + KV_LORA:Q_LORA + KV_LORA + LANES]
    krr = c[:, Q_LORA + KV_LORA + LANES:]
    for t in range(nblk):
        rows = slice(t * ATTN_TILE, (t + 1) * ATTN_TILE)
        qv_t = lax.dot_general(wqv_ref[...], n[rows], NT_DIMS,
                               preferred_element_type=F32)
        qat_ref[t] = (qv_t[:A_WIDTH] * (A_HEAD_DIM ** -0.5 * LOG2E)).astype(BF16)
        vat_ref[t] = qv_t[A_WIDTH:].astype(BF16)
        vbt_ref[t] = lax.dot_general(wuv_ref[...], ckv[rows], NT_DIMS,
                                     preferred_element_type=F32).astype(BF16)
        q_t = lax.dot_general(wuq_ref[...], cq[rows], NT_DIMS,
                              preferred_element_type=F32)
        cq_t, sq_t = cqt_ref[:, rows], sqt_ref[:, rows]
        half = QK_ROPE // 2
        for h in range(B_HEADS):
            q_h = q_t[h * MLA_HEAD_PAD:(h + 1) * MLA_HEAD_PAD]
            x1 = q_h[QK_NOPE:QK_NOPE + half]
            x2 = q_h[QK_NOPE + half:QK_NOPE + QK_ROPE]
            rot_h = jnp.concatenate(
                [jnp.zeros((QK_NOPE, ATTN_TILE), F32), -x2, x1,
                 jnp.zeros((MLA_HEAD_PAD - QK_NOPE - QK_ROPE, ATTN_TILE), F32)], axis=0)
            qmt_ref[t, h * MLA_HEAD_PAD:(h + 1) * MLA_HEAD_PAD, :] = (
                q_h * cq_t + rot_h * sq_t).astype(BF16)
    kn = jnp.dot(ckv, wuk_ref[...], preferred_element_type=F32)
    k_rope = kr * ck_ref[...] + krr * sk_ref[...]
    for h in range(B_HEADS):
        sl = slice(h * MLA_HEAD_PAD, (h + 1) * MLA_HEAD_PAD)
        km_ref[:, sl] = (kn[:, sl] + k_rope).astype(BF16)


def _inproj(x2, g_attn, wk, wqv, wc, qn, kvn, wuq, wuk, wuv, cq_tt, sq_tt, ck_t, sk_t,
            batch, seq):
    n_rows = x2.shape[0]
    tm = ROW_TILE
    steps = n_rows // tm
    per_seq = seq // tm
    nblk = tm // ATTN_TILE
    row = lambda i: (i, 0)
    tab = lambda i: (i % per_seq, 0)
    tab_t = lambda i: (0, i % per_seq)
    tile_map = lambda i: (i // per_seq, i % per_seq, 0, 0)
    bf = lambda w: jax.ShapeDtypeStruct((n_rows, w), BF16)
    tiles = lambda w: jax.ShapeDtypeStruct((batch, seq // ATTN_TILE, w, ATTN_TILE), BF16)
    tile_spec = lambda w: pl.BlockSpec((None, nblk, w, ATTN_TILE), tile_map)
    return pl.pallas_call(
        _inproj_kernel,
        grid=(steps,),
        in_specs=[
            pl.BlockSpec((tm, D_MODEL), row),
            _const_spec(g_attn.shape), _const_spec(wk.shape), _const_spec(wqv.shape),
            _const_spec(wc.shape), _const_spec(qn.shape), _const_spec(kvn.shape),
            _const_spec(wuq.shape), _const_spec(wuk.shape),
            _const_spec(wuv.shape),
            pl.BlockSpec((LANES, tm), tab_t), pl.BlockSpec((LANES, tm), tab_t),
            pl.BlockSpec((tm, LANES), tab), pl.BlockSpec((tm, LANES), tab),
        ],
        out_specs=[
            tile_spec(A_WIDTH), pl.BlockSpec((tm, A_WIDTH), row), tile_spec(A_WIDTH),
            pl.BlockSpec((1, nblk, A_WIDTH), lambda i: (i, 0, 0)),
            tile_spec(B_HEADS * MLA_HEAD_PAD),
            pl.BlockSpec((tm, B_HEADS * MLA_HEAD_PAD), row),
            tile_spec(B_WIDTH),
        ],
        out_shape=[
            tiles(A_WIDTH), bf(A_WIDTH), tiles(A_WIDTH),
            jax.ShapeDtypeStruct((steps, nblk, A_WIDTH), F32),
            tiles(B_HEADS * MLA_HEAD_PAD), bf(B_HEADS * MLA_HEAD_PAD), tiles(B_WIDTH),
        ],
        compiler_params=pltpu.CompilerParams(
            dimension_semantics=("arbitrary",), vmem_limit_bytes=VMEM_LIMIT),
        name="in_proj",
    )(x2, g_attn, wk, wqv, wc, qn, kvn, wuq, wuk, wuv, cq_tt, sq_tt, ck_t, sk_t)


def _store_scores(buf, h, parts):
    s_ref, max_ref = buf
    row = 0
    for idx, s_t in enumerate(parts):
        s_ref[h, row:row + s_t.shape[0]] = s_t
        max_ref[h, idx] = jnp.max(s_t, axis=0, keepdims=True)
        row += s_t.shape[0]


def _softmax_step(buf, h, parts, m_ref, acc_ref, first):
    s_ref, max_ref = buf
    maxes = [max_ref[h, idx] for idx in range(len(parts))]
    m_new = None if first else m_ref[...]
    for s_max, (_, _, col_keep) in zip(maxes, parts):
        if col_keep is not None:
            s_max = jnp.where(col_keep > 0, s_max, NEG)
        m_new = s_max if m_new is None else jnp.maximum(m_new, s_max)
    pv = None
    row = 0
    for s_max, (rows, v_t, col_keep) in zip(maxes, parts):
        ones = (lax.broadcasted_iota(jnp.int32, (SUM_ROWS, rows), 0) == 0).astype(BF16)
        v_aug = jnp.concatenate([v_t, ones], axis=0)
        m_exp = m_new if col_keep is None else jnp.maximum(m_new, s_max)
        p_t = jnp.exp2(s_ref[h, row:row + rows] - m_exp).astype(BF16)
        part = jnp.dot(v_aug, p_t, preferred_element_type=F32)
        if col_keep is not None:
            part = part * col_keep
        pv = part if pv is None else pv + part
        row += rows
    if first:
        acc_ref[...] = pv
    else:
        acc_ref[...] = jnp.exp2(m_ref[...] - m_new) * acc_ref[...] + pv
    m_ref[...] = m_new


def _pipeline(n_items, score, attend, bufs, unroll):
    if n_items == 0:
        return
    score(0, bufs[0])
    n_loops = (n_items - 1) // unroll

    def body(p, carry):
        for t in range(unroll):
            n = unroll * p + t
            score(n + 1, bufs[(t + 1) % 2])
            attend(n, bufs[t % 2])
        return carry

    lax.fori_loop(0, n_loops, body, 0)
    done = unroll * n_loops
    for n in range(done, n_items):
        if n + 1 < n_items:
            score(n + 1, bufs[(n + 1 - done) % 2])
        attend(n, bufs[(n - done) % 2])


def _tile_rows(j, span=1):
    return pl.ds(pl.multiple_of(j * ATTN_TILE, ATTN_TILE), span * ATTN_TILE)


def _write_heads(acc_ref, o_ref, nq, head_dim):
    def one(qi, carry):
        for h in range(HEAD_GROUP):
            acc = acc_ref[qi, h]
            o_ref[qi, h * head_dim:(h + 1) * head_dim, :] = (
                acc[:head_dim] / acc[head_dim:head_dim + 1]).astype(o_ref.dtype)
        return carry

    lax.fori_loop(0, nq, one, 0)


def _causal_items(nq, n_near, pair_far):
    phases = [[(i, i - d) for i in range(d, nq)] for d in range(n_near)]
    pairs, singles = [], []
    for i in range(n_near, nq):
        n_far = i - n_near + 1
        n_paired = n_far - n_far % 2 if pair_far else 0
        pairs += [(i, j) for j in range(0, n_paired, 2)]
        singles += [(i, j) for j in range(n_paired, n_far)]
    phases += [pairs, singles]
    flat = [it for ph in phases for it in ph]
    tq = jnp.asarray([it[0] for it in flat], jnp.int32)
    tk = jnp.asarray([it[1] for it in flat], jnp.int32)
    return tq, tk, [len(ph) for ph in phases]


def _moba_kernel(tq_ref, tk_ref, qt_ref, k_ref, vt_ref, kmean_ref, bias_ref, o_ref,
                 keep_ref, s0_ref, s1_ref, t0_ref, t1_ref, m_ref, acc_ref, *, seq, phases):
    nblk = seq // MOBA_BLOCK

    def head_query(qi, h):
        pr, sub = divmod(h, 2)
        q_h = qt_ref[qi, pl.ds(pr * LANES + sub * A_HEAD_DIM, A_HEAD_DIM), :]
        zeros = jnp.zeros_like(q_h)
        return jnp.concatenate([q_h, zeros] if sub == 0 else [zeros, q_h], axis=0)

    def select_blocks(i):
        row = lax.broadcasted_iota(jnp.int32, (nblk, ATTN_TILE), 0)
        rowf = row.astype(F32)
        past = row < i
        for h in range(HEAD_GROUP):
            kmean = kmean_ref[:, (h // 2) * LANES:(h // 2 + 1) * LANES].astype(BF16)
            gate = jnp.dot(kmean, head_query(i, h), preferred_element_type=F32)
            gate = jnp.where(past, gate, -jnp.inf)
            picked = jnp.zeros((nblk, ATTN_TILE), F32)
            for _ in range(MOBA_TOPK):
                top = jnp.max(gate, axis=0, keepdims=True)
                first = jnp.min(jnp.where(gate == top, rowf, float(nblk)), axis=0, keepdims=True)
                pick = rowf == first
                picked = jnp.where(pick, 1.0, picked)
                gate = jnp.where(pick, -jnp.inf, gate)
            keep_ref[i, h] = jnp.where(past, picked, 0.0)

    def stage(offset, span, bias_cols, first):
        def score(n, buf):
            qi, kj = tq_ref[offset + n], tk_ref[offset + n]
            for h in range(HEAD_GROUP):
                k_t = k_ref[_tile_rows(kj, span), (h // 2) * LANES:(h // 2 + 1) * LANES]
                s_t = jnp.dot(k_t, head_query(qi, h), preferred_element_type=F32)
                if bias_cols is not None:
                    s_t = s_t + bias_ref[h, :, bias_cols]
                _store_scores(buf, h, [s_t[t * MOBA_BLOCK:(t + 1) * MOBA_BLOCK]
                                       for t in range(span)])

        def attend(n, buf):
            qi, kj = tq_ref[offset + n], tk_ref[offset + n]
            for h in range(HEAD_GROUP):
                parts = [(MOBA_BLOCK, vt_ref[kj + t, h * A_HEAD_DIM:(h + 1) * A_HEAD_DIM, :],
                          None if first else keep_ref[qi, h, pl.ds(kj + t, 1), :])
                         for t in range(span)]
                _softmax_step(buf, h, parts, m_ref.at[qi, h], acc_ref.at[qi, h], first)
            if first:
                select_blocks(qi)

        return score, attend

    bufs = ((s0_ref, t0_ref), (s1_ref, t1_ref))
    n_own, n_prev, n_pairs, n_single = phases
    _pipeline(n_own, *stage(0, 1, slice(0, MOBA_BLOCK), True), bufs, PIPELINE_UNROLL)
    offset = n_own
    _pipeline(n_prev, *stage(offset, 1, slice(MOBA_BLOCK, 2 * MOBA_BLOCK), False), bufs,
              PIPELINE_UNROLL)
    offset += n_prev
    _pipeline(n_pairs, *stage(offset, 2, None, False), bufs, PIPELINE_UNROLL // 2)
    offset += n_pairs
    _pipeline(n_single, *stage(offset, 1, None, False), bufs, PIPELINE_UNROLL)
    _write_heads(acc_ref, o_ref, nblk, A_HEAD_DIM)


def _moba(qat, ka, vat, kmean, bias):
    b, seq, _ = ka.shape
    nq = seq // ATTN_TILE
    gw = HEAD_GROUP * A_HEAD_DIM
    tq, tk, phases = _causal_items(nq, 2, pair_far=True)
    per_group = lambda bi, g, *_: (bi, 0, g)
    tiles = lambda bi, g, *_: (bi, 0, g, 0)
    return pl.pallas_call(
        functools.partial(_moba_kernel, seq=seq, phases=phases),
        grid_spec=pltpu.PrefetchScalarGridSpec(
            num_scalar_prefetch=2,
            grid=(b, A_HEADS // HEAD_GROUP),
            in_specs=[
                pl.BlockSpec((None, nq, gw, ATTN_TILE), tiles),
                pl.BlockSpec((None, seq, gw), per_group),
                pl.BlockSpec((None, nq, gw, MOBA_BLOCK), tiles),
                pl.BlockSpec((None, nq, gw), per_group),
                pl.BlockSpec((HEAD_GROUP, MOBA_BLOCK, 2 * MOBA_BLOCK), lambda bi, g, *_: (g, 0, 0)),
            ],
            out_specs=pl.BlockSpec((None, nq, gw, ATTN_TILE), tiles),
            scratch_shapes=[
                pltpu.VMEM((nq, HEAD_GROUP, nq, ATTN_TILE), F32),
                pltpu.VMEM((HEAD_GROUP, ITEM_SPAN * ATTN_TILE, ATTN_TILE), F32),
                pltpu.VMEM((HEAD_GROUP, ITEM_SPAN * ATTN_TILE, ATTN_TILE), F32),
                pltpu.VMEM((HEAD_GROUP, ITEM_SPAN, 1, ATTN_TILE), F32),
                pltpu.VMEM((HEAD_GROUP, ITEM_SPAN, 1, ATTN_TILE), F32),
                pltpu.VMEM((nq, HEAD_GROUP, 1, ATTN_TILE), F32),
                pltpu.VMEM((nq, HEAD_GROUP, A_HEAD_DIM + SUM_ROWS, ATTN_TILE), F32),
            ]),
        out_shape=jax.ShapeDtypeStruct((b, nq, A_WIDTH, ATTN_TILE), BF16),
        compiler_params=pltpu.CompilerParams(
            dimension_semantics=("arbitrary", "arbitrary"), vmem_limit_bytes=VMEM_LIMIT),
        name="moba_attention",
    )(tq, tk, qat, ka, vat, kmean, bias)


def _mla_kernel(tq_ref, tk_ref, qt_ref, k_ref, vt_ref, o_ref, s0_ref, s1_ref, t0_ref, t1_ref,
                m_ref, acc_ref, *, nq, phases):
    def stage(offset, span, diagonal):
        def score(n, buf):
            qi, kj = tq_ref[offset + n], tk_ref[offset + n]
            for h in range(HEAD_GROUP):
                sl = slice(h * MLA_HEAD_PAD, (h + 1) * MLA_HEAD_PAD)
                s_t = jnp.dot(k_ref[_tile_rows(kj, span), sl], qt_ref[qi, sl, :],
                              preferred_element_type=F32)
                if diagonal:
                    key = lax.broadcasted_iota(jnp.int32, (ATTN_TILE, ATTN_TILE), 0)
                    qry = lax.broadcasted_iota(jnp.int32, (ATTN_TILE, ATTN_TILE), 1)
                    s_t = jnp.where(key <= qry, s_t, NEG)
                _store_scores(buf, h, [s_t[t * ATTN_TILE:(t + 1) * ATTN_TILE]
                                       for t in range(span)])

        def attend(n, buf):
            qi, kj = tq_ref[offset + n], tk_ref[offset + n]
            for h in range(HEAD_GROUP):
                parts = [(ATTN_TILE, vt_ref[kj + t, h * V_HEAD:(h + 1) * V_HEAD, :], None)
                         for t in range(span)]
                _softmax_step(buf, h, parts, m_ref.at[qi, h], acc_ref.at[qi, h], diagonal)

        return score, attend

    bufs = ((s0_ref, t0_ref), (s1_ref, t1_ref))
    n_diag, n_pairs, n_single = phases
    _pipeline(n_diag, *stage(0, 1, True), bufs, PIPELINE_UNROLL)
    _pipeline(n_pairs, *stage(n_diag, 2, False), bufs, PIPELINE_UNROLL // 2)
    _pipeline(n_single, *stage(n_diag + n_pairs, 1, False), bufs, PIPELINE_UNROLL)
    _write_heads(acc_ref, o_ref, nq, V_HEAD)


def _mla(qmt, km, vbt):
    b, seq, _ = km.shape
    nq = seq // ATTN_TILE
    qw = HEAD_GROUP * MLA_HEAD_PAD
    vw = HEAD_GROUP * V_HEAD
    tq, tk, phases = _causal_items(nq, 1, pair_far=False)
    per_group = lambda bi, g, *_: (bi, 0, g)
    tiles = lambda bi, g, *_: (bi, 0, g, 0)
    return pl.pallas_call(
        functools.partial(_mla_kernel, nq=nq, phases=phases),
        grid_spec=pltpu.PrefetchScalarGridSpec(
            num_scalar_prefetch=2,
            grid=(b, B_HEADS // HEAD_GROUP),
            in_specs=[
                pl.BlockSpec((None, nq, qw, ATTN_TILE), tiles),
                pl.BlockSpec((None, seq, qw), per_group),
                pl.BlockSpec((None, nq, vw, ATTN_TILE), tiles),
            ],
            out_specs=pl.BlockSpec((None, nq, vw, ATTN_TILE), tiles),
            scratch_shapes=[
                pltpu.VMEM((HEAD_GROUP, ITEM_SPAN * ATTN_TILE, ATTN_TILE), F32),
                pltpu.VMEM((HEAD_GROUP, ITEM_SPAN * ATTN_TILE, ATTN_TILE), F32),
                pltpu.VMEM((HEAD_GROUP, ITEM_SPAN, 1, ATTN_TILE), F32),
                pltpu.VMEM((HEAD_GROUP, ITEM_SPAN, 1, ATTN_TILE), F32),
                pltpu.VMEM((nq, HEAD_GROUP, 1, ATTN_TILE), F32),
                pltpu.VMEM((nq, HEAD_GROUP, V_HEAD + SUM_ROWS, ATTN_TILE), F32),
            ]),
        out_shape=jax.ShapeDtypeStruct((b, nq, B_WIDTH, ATTN_TILE), BF16),
        compiler_params=pltpu.CompilerParams(
            dimension_semantics=("arbitrary", "arbitrary"), vmem_limit_bytes=VMEM_LIMIT),
        name="mla_attention",
    )(tq, tk, qmt, km, vbt)


def _out_kernel(x_ref, oa_ref, ob_ref, gattn_ref, wg_ref, wpa_ref, wpb_ref, wout_ref,
                gmlp_ref, wup_ref, wdn_ref, gfin_ref, o_ref):
    x = x_ref[...]
    n = _rms(x, gattn_ref[...]).astype(BF16)
    tn_dims = (((0,), (0,)), ((), ()))
    tiles = range(OUT_ROW_TILE // ATTN_TILE)

    def gated(o_ref_, w_ref, gate_rows):
        proj = jnp.concatenate([lax.dot_general(o_ref_[t], w_ref[...], tn_dims,
                                                preferred_element_type=F32) for t in tiles], axis=0)
        gate = lax.dot_general(n, wg_ref[gate_rows, :], NT_DIMS, preferred_element_type=F32)
        return jax.nn.sigmoid(gate) * proj

    merged = (gated(oa_ref, wpa_ref, slice(0, D_MODEL))
              + gated(ob_ref, wpb_ref, slice(D_MODEL, 2 * D_MODEL))).astype(BF16)
    h = x + jnp.dot(merged, wout_ref[...], preferred_element_type=F32)
    m = _rms(h, gmlp_ref[...]).astype(BF16)
    for c in range(D_FF // FF_CHUNK):
        cols = slice(c * FF_CHUNK, (c + 1) * FF_CHUNK)
        up = jnp.dot(m, wup_ref[:, cols], preferred_element_type=F32)
        act = jnp.square(jnp.maximum(up, 0.0)).astype(BF16)
        h = h + jnp.dot(act, wdn_ref[cols, :], preferred_element_type=F32)
    o_ref[...] = _rms(h, gfin_ref[...])


def _out_block(x2, oa, ob, g_attn, wg, wpa, wpb, wout, g_mlp, wup, wdn, g_fin):
    n_rows = x2.shape[0]
    tm = OUT_ROW_TILE
    row = lambda i: (i, 0)
    return pl.pallas_call(
        _out_kernel,
        grid=(n_rows // tm,),
        in_specs=[
            pl.BlockSpec((tm, D_MODEL), row),
            pl.BlockSpec((tm // ATTN_TILE, A_WIDTH, ATTN_TILE), lambda i: (i, 0, 0)),
            pl.BlockSpec((tm // ATTN_TILE, B_WIDTH, ATTN_TILE), lambda i: (i, 0, 0)),
            _const_spec(g_attn.shape), _const_spec(wg.shape), _const_spec(wpa.shape),
            _const_spec(wpb.shape), _const_spec(wout.shape), _const_spec(g_mlp.shape),
            _const_spec(wup.shape), _const_spec(wdn.shape), _const_spec(g_fin.shape),
        ],
        out_specs=pl.BlockSpec((tm, D_MODEL), row),
        out_shape=jax.ShapeDtypeStruct((n_rows, D_MODEL), F32),
        compiler_params=pltpu.CompilerParams(
            dimension_semantics=("arbitrary",), vmem_limit_bytes=VMEM_LIMIT),
        name="out_mlp",
    )(x2, oa, ob, g_attn, wg, wpa, wpb, wout, g_mlp, wup, wdn, g_fin)


def _rope_tables(seq):
    half = QK_ROPE // 2
    inv_freq = ROPE_THETA ** (-jnp.arange(half, dtype=F32) / half)
    ang = jnp.arange(seq).astype(F32)[:, None] * inv_freq[None, :]
    cos2 = jnp.tile(jnp.cos(ang), (1, 2))
    sin2 = jnp.tile(jnp.sin(ang), (1, 2))
    pad = jnp.zeros((seq, MLA_HEAD_PAD - QK_NOPE - QK_ROPE), F32)
    cos_t = jnp.concatenate([jnp.ones((seq, QK_NOPE), F32), cos2, pad], axis=1)
    sin_t = jnp.concatenate([jnp.zeros((seq, QK_NOPE), F32), sin2, pad], axis=1)
    return cos_t, sin_t


def _pad_heads(w_nope, w_rope):
    k, h, _ = w_nope.shape
    if w_rope is None:
        w_rope = jnp.zeros((k, h, QK_ROPE), w_nope.dtype)
    pad = jnp.zeros((k, h, MLA_HEAD_PAD - QK_NOPE - QK_ROPE), w_nope.dtype)
    return jnp.concatenate([w_nope, w_rope, pad], axis=-1).reshape(k, h * MLA_HEAD_PAD)


def kernel(x, w_in, rel_bias, mla_q_norm, w_uq, mla_kv_norm, w_uk, w_uv, w_proj_a, w_proj_b,
           w_out, norm_attn, norm_mlp, w_mlp_up, w_mlp_down, norm_final):
    b, seq, d = x.shape
    assert d == D_MODEL and seq % ROW_TILE == 0 and seq // MOBA_BLOCK <= LANES
    assert w_in.shape[0] == 1, "single-layer block"
    x2 = x.reshape(b * seq, d)

    w_t = jnp.swapaxes(w_in[0], 0, 1)
    o_k = A_WIDTH
    o_v = 2 * A_WIDTH
    o_cq = 3 * A_WIDTH
    o_ckv = o_cq + Q_LORA
    o_kr = o_ckv + KV_LORA
    o_g = o_kr + QK_ROPE
    wk = w_t[o_k:o_v].astype(BF16)
    wqv = jnp.concatenate([w_t[:o_k], w_t[o_v:o_cq]], axis=0).astype(BF16)
    w_kr = w_t[o_kr:o_g]
    half = QK_ROPE // 2
    w_kr_rot = jnp.concatenate([-w_kr[half:], w_kr[:half]], axis=0)
    row_pad = lambda a: jnp.pad(a, ((QK_NOPE, MLA_HEAD_PAD - QK_NOPE - QK_ROPE), (0, 0)))
    wc = jnp.concatenate(
        [w_t[o_cq:o_kr], row_pad(w_kr), row_pad(w_kr_rot)], axis=0).astype(BF16)
    wg = w_t[o_g:].astype(BF16)

    uq = w_uq[0].reshape(Q_LORA, B_HEADS, QK_NOPE + QK_ROPE)
    uq_nope, uq_rope = uq[..., :QK_NOPE], uq[..., QK_NOPE:]
    wuq = _pad_heads(uq_nope, uq_rope).T.astype(BF16)
    wuk = _pad_heads(w_uk[0].reshape(KV_LORA, B_HEADS, QK_NOPE), None).astype(BF16)
    wuv = w_uv[0].T.astype(BF16)

    cos_t, sin_t = _rope_tables(seq)
    q_scale = (QK_NOPE + QK_ROPE) ** -0.5 * LOG2E
    row2 = lambda a: a.reshape(1, -1)

    qat, ka, vat, kmean, qmt, km, vbt = _inproj(
        x2, row2(norm_attn[0]), wk, wqv, wc, row2(mla_q_norm[0]), row2(mla_kv_norm[0]),
        wuq, wuk, wuv, (cos_t * q_scale).T, (sin_t * q_scale).T, cos_t, sin_t, b, seq)

    as_seq = lambda a: a.reshape(b, seq, a.shape[-1])
    bias = _bias_tiles(rel_bias)
    oa = _moba(qat, as_seq(ka), vat, kmean.reshape(b, seq // MOBA_BLOCK, A_WIDTH), bias)
    ob = _mla(qmt, as_seq(km), vbt)

    out = _out_block(
        x2, oa.reshape(-1, A_WIDTH, ATTN_TILE), ob.reshape(-1, B_WIDTH, ATTN_TILE),
        row2(norm_attn[0]),
        wg, w_proj_a[0].astype(BF16), w_proj_b[0].astype(BF16), w_out[0].astype(BF16),
        row2(norm_mlp[0]), w_mlp_up[0].astype(BF16), w_mlp_down[0].astype(BF16),
        row2(norm_final))
    return out.reshape(b, seq, d)
```

```python
import functools
import math

import jax
import jax.numpy as jnp
from jax import lax
from jax.experimental import pallas as pl
from jax.experimental.pallas import tpu as pltpu

D_MODEL = 1024
A_HEADS = 8
A_HEAD_DIM = 64
A_WIDTH = A_HEADS * A_HEAD_DIM
MOBA_BLOCK = 256
MOBA_TOPK = 3
REL_BUCKETS = 32
REL_MAX_DIST = 128
B_HEADS = 8
QK_NOPE = 64
QK_ROPE = 32
V_HEAD = 64
B_WIDTH = B_HEADS * V_HEAD
Q_LORA = 384
KV_LORA = 256
ROPE_THETA = 10000.0
D_FF = 4 * D_MODEL
EPS = 1e-6
NEG = -1e30
LOG2E = math.log2(math.e)

LANES = 128
MLA_HEAD_PAD = 128
ATTN_TILE = 256
HEAD_GROUP = 4
SUM_ROWS = 16
ITEM_SPAN = 2
PIPELINE_UNROLL = 16
ROW_TILE = 512
FF_CHUNK = 2048
VMEM_LIMIT = 56 * 1024 * 1024

F32 = jnp.float32
BF16 = jnp.bfloat16
NT_DIMS = (((1,), (1,)), ((), ()))


def _rms(xf, g):
    y = xf * lax.rsqrt(jnp.mean(xf * xf, axis=-1, keepdims=True) + EPS)
    return y * g


def _const_spec(shape):
    zeros = (0,) * len(shape)
    return pl.BlockSpec(shape, lambda *_: zeros, pipeline_mode=pl.Buffered(1))


def _bias_kernel(rel_ref, o_ref):
    h = pl.program_id(0)
    shape = (MOBA_BLOCK, 2 * MOBA_BLOCK)
    r = lax.broadcasted_iota(jnp.int32, shape, 0)
    c = lax.broadcasted_iota(jnp.int32, shape, 1)
    d = c - r
    dist = jnp.maximum(d, 0)
    max_exact = REL_BUCKETS // 2
    df = jnp.maximum(dist, 1).astype(F32)
    large = max_exact + (jnp.log(df / max_exact) / math.log(REL_MAX_DIST / max_exact)
                         * (REL_BUCKETS - max_exact)).astype(jnp.int32)
    large = jnp.minimum(large, REL_BUCKETS - 1)
    bucket = jnp.where(dist < max_exact, dist, large)
    val = jnp.zeros(shape, F32)
    for b in range(REL_BUCKETS):
        val = jnp.where(bucket == b, rel_ref[b, h], val)
    o_ref[...] = jnp.where(d >= 0, (val - rel_ref[REL_BUCKETS - 1, h]) * LOG2E, NEG)


def _bias_tiles(rel_bias):
    return pl.pallas_call(
        _bias_kernel,
        grid=(A_HEADS,),
        in_specs=[pl.BlockSpec(memory_space=pltpu.SMEM)],
        out_specs=pl.BlockSpec((None, MOBA_BLOCK, 2 * MOBA_BLOCK), lambda h: (h, 0, 0)),
        out_shape=jax.ShapeDtypeStruct((A_HEADS, MOBA_BLOCK, 2 * MOBA_BLOCK), F32),
        name="moba_bias_tiles",
    )(rel_bias)


def _inproj_kernel(x_ref, g_ref, wk_ref, wqv_ref, wc_ref, qn_ref, kvn_ref, wuq_ref,
                   wuk_ref, wuv_ref, cqt_ref, sqt_ref, ck_ref, sk_ref,
                   qat_ref, ka_ref, vat_ref, kmean_ref, qmt_ref, km_ref, vbt_ref):
    n = _rms(x_ref[...], g_ref[...]).astype(BF16)
    k = lax.dot_general(n, wk_ref[...], NT_DIMS, preferred_element_type=F32)
    ka_ref[...] = k.astype(BF16)
    nblk = ROW_TILE // ATTN_TILE
    kmean_ref[0] = jnp.sum(k.reshape(nblk, MOBA_BLOCK, A_WIDTH), axis=1) * (1.0 / MOBA_BLOCK)

    c = lax.dot_general(n, wc_ref[...], NT_DIMS, preferred_element_type=F32)
    cq = _rms(c[:, :Q_LORA], qn_ref[...]).astype(BF16)
    ckv = _rms(c[:, Q_LORA:Q_LORA + KV_LORA], kvn_ref[...]).astype(BF16)
    kr = c[:, Q_LORA + KV_LORA:Q_LORA + KV_LORA + LANES]
    krr = c[:, Q_LORA + KV_LORA + LANES:]
    for t in range(nblk):
        rows = slice(t * ATTN_TILE, (t + 1) * ATTN_TILE)
        qv_t = lax.dot_general(wqv_ref[...], n[rows], NT_DIMS,
                               preferred_element_type=F32)
        qat_ref[t] = (qv_t[:A_WIDTH] * (A_HEAD_DIM ** -0.5 * LOG2E)).astype(BF16)
        vat_ref[t] = qv_t[A_WIDTH:].astype(BF16)
        vbt_ref[t] = lax.dot_general(wuv_ref[...], ckv[rows], NT_DIMS,
                                     preferred_element_type=F32).astype(BF16)
        q_t = lax.dot_general(wuq_ref[...], cq[rows], NT_DIMS,
                              preferred_element_type=F32)
        cq_t, sq_t = cqt_ref[:, rows], sqt_ref[:, rows]
        half = QK_ROPE // 2
        for h in range(B_HEADS):
            q_h = q_t[h * MLA_HEAD_PAD:(h + 1) * MLA_HEAD_PAD]
            x1 = q_h[QK_NOPE:QK_NOPE + half]
            x2 = q_h[QK_NOPE + half:QK_NOPE + QK_ROPE]
            rot_h = jnp.concatenate(
                [jnp.zeros((QK_NOPE, ATTN_TILE), F32), -x2, x1,
                 jnp.zeros((MLA_HEAD_PAD - QK_NOPE - QK_ROPE, ATTN_TILE), F32)], axis=0)
            qmt_ref[t, h * MLA_HEAD_PAD:(h + 1) * MLA_HEAD_PAD, :] = (
                q_h * cq_t + rot_h * sq_t).astype(BF16)
    kn = jnp.dot(ckv, wuk_ref[...], preferred_element_type=F32)
    k_rope = kr * ck_ref[...] + krr * sk_ref[...]
    for h in range(B_HEADS):
        sl = slice(h * MLA_HEAD_PAD, (h + 1) * MLA_HEAD_PAD)
        km_ref[:, sl] = (kn[:, sl] + k_rope).astype(BF16)


def _inproj(x2, g_attn, wk, wqv, wc, qn, kvn, wuq, wuk, wuv, cq_tt, sq_tt, ck_t, sk_t,
            batch, seq):
    n_rows = x2.shape[0]
    tm = ROW_TILE
    steps = n_rows // tm
    per_seq = seq // tm
    nblk = tm // ATTN_TILE
    row = lambda i: (i, 0)
    tab = lambda i: (i % per_seq, 0)
    tab_t = lambda i: (0, i % per_seq)
    tile_map = lambda i: (i // per_seq, i % per_seq, 0, 0)
    bf = lambda w: jax.ShapeDtypeStruct((n_rows, w), BF16)
    tiles = lambda w: jax.ShapeDtypeStruct((batch, seq // ATTN_TILE, w, ATTN_TILE), BF16)
    tile_spec = lambda w: pl.BlockSpec((None, nblk, w, ATTN_TILE), tile_map)
    return pl.pallas_call(
        _inproj_kernel,
        grid=(steps,),
        in_specs=[
            pl.BlockSpec((tm, D_MODEL), row),
            _const_spec(g_attn.shape), _const_spec(wk.shape), _const_spec(wqv.shape),
            _const_spec(wc.shape), _const_spec(qn.shape), _const_spec(kvn.shape),
            _const_spec(wuq.shape), _const_spec(wuk.shape),
            _const_spec(wuv.shape),
            pl.BlockSpec((LANES, tm), tab_t), pl.BlockSpec((LANES, tm), tab_t),
            pl.BlockSpec((tm, LANES), tab), pl.BlockSpec((tm, LANES), tab),
        ],
        out_specs=[
            tile_spec(A_WIDTH), pl.BlockSpec((tm, A_WIDTH), row), tile_spec(A_WIDTH),
            pl.BlockSpec((1, nblk, A_WIDTH), lambda i: (i, 0, 0)),
            tile_spec(B_HEADS * MLA_HEAD_PAD),
            pl.BlockSpec((tm, B_HEADS * MLA_HEAD_PAD), row),
            tile_spec(B_WIDTH),
        ],
        out_shape=[
            tiles(A_WIDTH), bf(A_WIDTH), tiles(A_WIDTH),
            jax.ShapeDtypeStruct((steps, nblk, A_WIDTH), F32),
            tiles(B_HEADS * MLA_HEAD_PAD), bf(B_HEADS * MLA_HEAD_PAD), tiles(B_WIDTH),
        ],
        compiler_params=pltpu.CompilerParams(
            dimension_semantics=("arbitrary",), vmem_limit_bytes=VMEM_LIMIT),
        name="in_proj",
    )(x2, g_attn, wk, wqv, wc, qn, kvn, wuq, wuk, wuv, cq_tt, sq_tt, ck_t, sk_t)


def _store_scores(buf, h, parts):
    s_ref, max_ref = buf
    row = 0
    for idx, s_t in enumerate(parts):
        s_ref[h, row:row + s_t.shape[0]] = s_t
        max_ref[h, idx] = jnp.max(s_t, axis=0, keepdims=True)
        row += s_t.shape[0]


def _softmax_step(buf, h, parts, m_ref, acc_ref, first):
    s_ref, max_ref = buf
    maxes = [max_ref[h, idx] for idx in range(len(parts))]
    m_new = None if first else m_ref[...]
    for s_max, (_, _, col_keep) in zip(maxes, parts):
        if col_keep is not None:
            s_max = jnp.where(col_keep > 0, s_max, NEG)
        m_new = s_max if m_new is None else jnp.maximum(m_new, s_max)
    pv = None
    row = 0
    for s_max, (rows, v_t, col_keep) in zip(maxes, parts):
        ones = (lax.broadcasted_iota(jnp.int32, (SUM_ROWS, rows), 0) == 0).astype(BF16)
        v_aug = jnp.concatenate([v_t, ones], axis=0)
        m_exp = m_new if col_keep is None else jnp.maximum(m_new, s_max)
        p_t = jnp.exp2(s_ref[h, row:row + rows] - m_exp).astype(BF16)
        part = jnp.dot(v_aug, p_t, preferred_element_type=F32)
        if col_keep is not None:
            part = part * col_keep
        pv = part if pv is None else pv + part
        row += rows
    if first:
        acc_ref[...] = pv
    else:
        acc_ref[...] = jnp.exp2(m_ref[...] - m_new) * acc_ref[...] + pv
    m_ref[...] = m_new


def _pipeline(n_items, score, attend, bufs, unroll):
    if n_items == 0:
        return
    score(0, bufs[0])
    n_loops = (n_items - 1) // unroll

    def body(p, carry):
        for t in range(unroll):
            n = unroll * p + t
            score(n + 1, bufs[(t + 1) % 2])
            attend(n, bufs[t % 2])
        return carry

    lax.fori_loop(0, n_loops, body, 0)
    done = unroll * n_loops
    for n in range(done, n_items):
        if n + 1 < n_items:
            score(n + 1, bufs[(n + 1 - done) % 2])
        attend(n, bufs[(n - done) % 2])


def _tile_rows(j, span=1):
    return pl.ds(pl.multiple_of(j * ATTN_TILE, ATTN_TILE), span * ATTN_TILE)


def _write_heads(acc_ref, o_ref, nq, head_dim):
    def one(qi, carry):
        for h in range(HEAD_GROUP):
            acc = acc_ref[qi, h]
            o_ref[qi, h * head_dim:(h + 1) * head_dim, :] = (
                acc[:head_dim] / acc[head_dim:head_dim + 1]).astype(o_ref.dtype)
        return carry

    lax.fori_loop(0, nq, one, 0)


def _causal_items(nq, n_near, pair_far):
    phases = [[(i, i - d) for i in range(d, nq)] for d in range(n_near)]
    pairs, singles = [], []
    for i in range(n_near, nq):
        n_far = i - n_near + 1
        n_paired = n_far - n_far % 2 if pair_far else 0
        pairs += [(i, j) for j in range(0, n_paired, 2)]
        singles += [(i, j) for j in range(n_paired, n_far)]
    phases += [pairs, singles]
    flat = [it for ph in phases for it in ph]
    tq = jnp.asarray([it[0] for it in flat], jnp.int32)
    tk = jnp.asarray([it[1] for it in flat], jnp.int32)
    return tq, tk, [len(ph) for ph in phases]


def _moba_kernel(tq_ref, tk_ref, qt_ref, k_ref, vt_ref, kmean_ref, bias_ref, o_ref,
                 keep_ref, s0_ref, s1_ref, t0_ref, t1_ref, m_ref, acc_ref, *, seq, phases):
    nblk = seq // MOBA_BLOCK

    def head_query(qi, h):
        pr, sub = divmod(h, 2)
        q_h = qt_ref[qi, pl.ds(pr * LANES + sub * A_HEAD_DIM, A_HEAD_DIM), :]
        zeros = jnp.zeros_like(q_h)
        return jnp.concatenate([q_h, zeros] if sub == 0 else [zeros, q_h], axis=0)

    def select_blocks(i):
        row = lax.broadcasted_iota(jnp.int32, (nblk, ATTN_TILE), 0)
        rowf = row.astype(F32)
        past = row < i
        for h in range(HEAD_GROUP):
            kmean = kmean_ref[:, (h // 2) * LANES:(h // 2 + 1) * LANES].astype(BF16)
            gate = jnp.dot(kmean, head_query(i, h), preferred_element_type=F32)
            gate = jnp.where(past, gate, -jnp.inf)
            picked = jnp.zeros((nblk, ATTN_TILE), F32)
            for _ in range(MOBA_TOPK):
                top = jnp.max(gate, axis=0, keepdims=True)
                first = jnp.min(jnp.where(gate == top, rowf, float(nblk)), axis=0, keepdims=True)
                pick = rowf == first
                picked = jnp.where(pick, 1.0, picked)
                gate = jnp.where(pick, -jnp.inf, gate)
            keep_ref[i, h] = jnp.where(past, picked, 0.0)

    def stage(offset, span, bias_cols, first):
        def score(n, buf):
            qi, kj = tq_ref[offset + n], tk_ref[offset + n]
            for h in range(HEAD_GROUP):
                k_t = k_ref[_tile_rows(kj, span), (h // 2) * LANES:(h // 2 + 1) * LANES]
                s_t = jnp.dot(k_t, head_query(qi, h), preferred_element_type=F32)
                if bias_cols is not None:
                    s_t = s_t + bias_ref[h, :, bias_cols]
                _store_scores(buf, h, [s_t[t * MOBA_BLOCK:(t + 1) * MOBA_BLOCK]
                                       for t in range(span)])

        def attend(n, buf):
            qi, kj = tq_ref[offset + n], tk_ref[offset + n]
            for h in range(HEAD_GROUP):
                parts = [(MOBA_BLOCK, vt_ref[kj + t, h * A_HEAD_DIM:(h + 1) * A_HEAD_DIM, :],
                          None if first else keep_ref[qi, h, pl.ds(kj + t, 1), :])
                         for t in range(span)]
                _softmax_step(buf, h, parts, m_ref.at[qi, h], acc_ref.at[qi, h], first)
            if first:
                select_blocks(qi)

        return score, attend

    bufs = ((s0_ref, t0_ref), (s1_ref, t1_ref))
    n_own, n_prev, n_pairs, n_single = phases
    _pipeline(n_own, *stage(0, 1, slice(0, MOBA_BLOCK), True), bufs, PIPELINE_UNROLL)
    offset = n_own
    _pipeline(n_prev, *stage(offset, 1, slice(MOBA_BLOCK, 2 * MOBA_BLOCK), False), bufs,
              PIPELINE_UNROLL)
    offset += n_prev
    _pipeline(n_pairs, *stage(offset, 2, None, False), bufs, PIPELINE_UNROLL // 2)
    offset += n_pairs
    _pipeline(n_single, *stage(offset, 1, None, False), bufs, PIPELINE_UNROLL)
    _write_heads(acc_ref, o_ref, nblk, A_HEAD_DIM)


def _moba(qat, ka, vat, kmean, bias):
    b, seq, _ = ka.shape
    nq = seq // ATTN_TILE
    gw = HEAD_GROUP * A_HEAD_DIM
    tq, tk, phases = _causal_items(nq, 2, pair_far=True)
    per_group = lambda bi, g, *_: (bi, 0, g)
    tiles = lambda bi, g, *_: (bi, 0, g, 0)
    return pl.pallas_call(
        functools.partial(_moba_kernel, seq=seq, phases=phases),
        grid_spec=pltpu.PrefetchScalarGridSpec(
            num_scalar_prefetch=2,
            grid=(b, A_HEADS // HEAD_GROUP),
            in_specs=[
                pl.BlockSpec((None, nq, gw, ATTN_TILE), tiles),
                pl.BlockSpec((None, seq, gw), per_group),
                pl.BlockSpec((None, nq, gw, MOBA_BLOCK), tiles),
                pl.BlockSpec((None, nq, gw), per_group),
                pl.BlockSpec((HEAD_GROUP, MOBA_BLOCK, 2 * MOBA_BLOCK), lambda bi, g, *_: (g, 0, 0)),
            ],
            out_specs=pl.BlockSpec((None, nq, gw, ATTN_TILE), tiles),
            scratch_shapes=[
                pltpu.VMEM((nq, HEAD_GROUP, nq, ATTN_TILE), F32),
                pltpu.VMEM((HEAD_GROUP, ITEM_SPAN * ATTN_TILE, ATTN_TILE), F32),
                pltpu.VMEM((HEAD_GROUP, ITEM_SPAN * ATTN_TILE, ATTN_TILE), F32),
                pltpu.VMEM((HEAD_GROUP, ITEM_SPAN, 1, ATTN_TILE), F32),
                pltpu.VMEM((HEAD_GROUP, ITEM_SPAN, 1, ATTN_TILE), F32),
                pltpu.VMEM((nq, HEAD_GROUP, 1, ATTN_TILE), F32),
                pltpu.VMEM((nq, HEAD_GROUP, A_HEAD_DIM + SUM_ROWS, ATTN_TILE), F32),
            ]),
        out_shape=jax.ShapeDtypeStruct((b, nq, A_WIDTH, ATTN_TILE), BF16),
        compiler_params=pltpu.CompilerParams(
            dimension_semantics=("arbitrary", "arbitrary"), vmem_limit_bytes=VMEM_LIMIT),
        name="moba_attention",
    )(tq, tk, qat, ka, vat, kmean, bias)


def _mla_kernel(tq_ref, tk_ref, qt_ref, k_ref, vt_ref, o_ref, s0_ref, s1_ref, t0_ref, t1_ref,
                m_ref, acc_ref, *, nq, phases):
    def stage(offset, span, diagonal):
        def score(n, buf):
            qi, kj = tq_ref[offset + n], tk_ref[offset + n]
            for h in range(HEAD_GROUP):
                sl = slice(h * MLA_HEAD_PAD, (h + 1) * MLA_HEAD_PAD)
                s_t = jnp.dot(k_ref[_tile_rows(kj, span), sl], qt_ref[qi, sl, :],
                              preferred_element_type=F32)
                if diagonal:
                    key = lax.broadcasted_iota(jnp.int32, (ATTN_TILE, ATTN_TILE), 0)
                    qry = lax.broadcasted_iota(jnp.int32, (ATTN_TILE, ATTN_TILE), 1)
                    s_t = jnp.where(key <= qry, s_t, NEG)
                _store_scores(buf, h, [s_t[t * ATTN_TILE:(t + 1) * ATTN_TILE]
                                       for t in range(span)])

        def attend(n, buf):
            qi, kj = tq_ref[offset + n], tk_ref[offset + n]
            for h in range(HEAD_GROUP):
                parts = [(ATTN_TILE, vt_ref[kj + t, h * V_HEAD:(h + 1) * V_HEAD, :], None)
                         for t in range(span)]
                _softmax_step(buf, h, parts, m_ref.at[qi, h], acc_ref.at[qi, h], diagonal)

        return score, attend

    bufs = ((s0_ref, t0_ref), (s1_ref, t1_ref))
    n_diag, n_pairs, n_single = phases
    _pipeline(n_diag, *stage(0, 1, True), bufs, PIPELINE_UNROLL)
    _pipeline(n_pairs, *stage(n_diag, 2, False), bufs, PIPELINE_UNROLL // 2)
    _pipeline(n_single, *stage(n_diag + n_pairs, 1, False), bufs, PIPELINE_UNROLL)
    _write_heads(acc_ref, o_ref, nq, V_HEAD)


def _mla(qmt, km, vbt):
    b, seq, _ = km.shape
    nq = seq // ATTN_TILE
    qw = HEAD_GROUP * MLA_HEAD_PAD
    vw = HEAD_GROUP * V_HEAD
    tq, tk, phases = _causal_items(nq, 1, pair_far=False)
    per_group = lambda bi, g, *_: (bi, 0, g)
    tiles = lambda bi, g, *_: (bi, 0, g, 0)
    return pl.pallas_call(
        functools.partial(_mla_kernel, nq=nq, phases=phases),
        grid_spec=pltpu.PrefetchScalarGridSpec(
            num_scalar_prefetch=2,
            grid=(b, B_HEADS // HEAD_GROUP),
            in_specs=[
                pl.BlockSpec((None, nq, qw, ATTN_TILE), tiles),
                pl.BlockSpec((None, seq, qw), per_group),
                pl.BlockSpec((None, nq, vw, ATTN_TILE), tiles),
            ],
            out_specs=pl.BlockSpec((None, nq, vw, ATTN_TILE), tiles),
            scratch_shapes=[
                pltpu.VMEM((HEAD_GROUP, ITEM_SPAN * ATTN_TILE, ATTN_TILE), F32),
                pltpu.VMEM((HEAD_GROUP, ITEM_SPAN * ATTN_TILE, ATTN_TILE), F32),
                pltpu.VMEM((HEAD_GROUP, ITEM_SPAN, 1, ATTN_TILE), F32),
                pltpu.VMEM((HEAD_GROUP, ITEM_SPAN, 1, ATTN_TILE), F32),
                pltpu.VMEM((nq, HEAD_GROUP, 1, ATTN_TILE), F32),
                pltpu.VMEM((nq, HEAD_GROUP, V_HEAD + SUM_ROWS, ATTN_TILE), F32),
            ]),
        out_shape=jax.ShapeDtypeStruct((b, nq, B_WIDTH, ATTN_TILE), BF16),
        compiler_params=pltpu.CompilerParams(
            dimension_semantics=("arbitrary", "arbitrary"), vmem_limit_bytes=VMEM_LIMIT),
        name="mla_attention",
    )(tq, tk, qmt, km, vbt)


def _out_kernel(x_ref, oa_ref, ob_ref, gattn_ref, wg_ref, wpa_ref, wpb_ref, wout_ref,
                gmlp_ref, wup_ref, wdn_ref, gfin_ref, o_ref):
    x = x_ref[...]
    n = _rms(x, gattn_ref[...]).astype(BF16)
    gates = lax.dot_general(n, wg_ref[...], NT_DIMS, preferred_element_type=F32)
    tn_dims = (((0,), (0,)), ((), ()))
    tiles = range(ROW_TILE // ATTN_TILE)
    pa = jnp.concatenate([lax.dot_general(oa_ref[t], wpa_ref[...], tn_dims,
                                          preferred_element_type=F32) for t in tiles], axis=0)
    pb = jnp.concatenate([lax.dot_general(ob_ref[t], wpb_ref[...], tn_dims,
                                          preferred_element_type=F32) for t in tiles], axis=0)
    merged = (jax.nn.sigmoid(gates[:, :D_MODEL]) * pa
              + jax.nn.sigmoid(gates[:, D_MODEL:]) * pb).astype(BF16)
    h = x + jnp.dot(merged, wout_ref[...], preferred_element_type=F32)
    m = _rms(h, gmlp_ref[...]).astype(BF16)
    for c in range(D_FF // FF_CHUNK):
        cols = slice(c * FF_CHUNK, (c + 1) * FF_CHUNK)
        up = jnp.dot(m, wup_ref[:, cols], preferred_element_type=F32)
        act = jnp.square(jnp.maximum(up, 0.0)).astype(BF16)
        h = h + jnp.dot(act, wdn_ref[cols, :], preferred_element_type=F32)
    o_ref[...] = _rms(h, gfin_ref[...])


def _out_block(x2, oa, ob, g_attn, wg, wpa, wpb, wout, g_mlp, wup, wdn, g_fin):
    n_rows = x2.shape[0]
    tm = ROW_TILE
    row = lambda i: (i, 0)
    return pl.pallas_call(
        _out_kernel,
        grid=(n_rows // tm,),
        in_specs=[
            pl.BlockSpec((tm, D_MODEL), row),
            pl.BlockSpec((tm // ATTN_TILE, A_WIDTH, ATTN_TILE), lambda i: (i, 0, 0)),
            pl.BlockSpec((tm // ATTN_TILE, B_WIDTH, ATTN_TILE), lambda i: (i, 0, 0)),
            _const_spec(g_attn.shape), _const_spec(wg.shape), _const_spec(wpa.shape),
            _const_spec(wpb.shape), _const_spec(wout.shape), _const_spec(g_mlp.shape),
            _const_spec(wup.shape), _const_spec(wdn.shape), _const_spec(g_fin.shape),
        ],
        out_specs=pl.BlockSpec((tm, D_MODEL), row),
        out_shape=jax.ShapeDtypeStruct((n_rows, D_MODEL), F32),
        compiler_params=pltpu.CompilerParams(
            dimension_semantics=("arbitrary",), vmem_limit_bytes=VMEM_LIMIT),
        name="out_mlp",
    )(x2, oa, ob, g_attn, wg, wpa, wpb, wout, g_mlp, wup, wdn, g_fin)


def _rope_tables(seq):
    half = QK_ROPE // 2
    inv_freq = ROPE_THETA ** (-jnp.arange(half, dtype=F32) / half)
    ang = jnp.arange(seq).astype(F32)[:, None] * inv_freq[None, :]
    cos2 = jnp.tile(jnp.cos(ang), (1, 2))
    sin2 = jnp.tile(jnp.sin(ang), (1, 2))
    pad = jnp.zeros((seq, MLA_HEAD_PAD - QK_NOPE - QK_ROPE), F32)
    cos_t = jnp.concatenate([jnp.ones((seq, QK_NOPE), F32), cos2, pad], axis=1)
    sin_t = jnp.concatenate([jnp.zeros((seq, QK_NOPE), F32), sin2, pad], axis=1)
    return cos_t, sin_t


def _pad_heads(w_nope, w_rope):
    k, h, _ = w_nope.shape
    if w_rope is None:
        w_rope = jnp.zeros((k, h, QK_ROPE), w_nope.dtype)
    pad = jnp.zeros((k, h, MLA_HEAD_PAD - QK_NOPE - QK_ROPE), w_nope.dtype)
    return jnp.concatenate([w_nope, w_rope, pad], axis=-1).reshape(k, h * MLA_HEAD_PAD)


def kernel(x, w_in, rel_bias, mla_q_norm, w_uq, mla_kv_norm, w_uk, w_uv, w_proj_a, w_proj_b,
           w_out, norm_attn, norm_mlp, w_mlp_up, w_mlp_down, norm_final):
    b, seq, d = x.shape
    assert d == D_MODEL and seq % ROW_TILE == 0 and seq // MOBA_BLOCK <= LANES
    assert w_in.shape[0] == 1, "single-layer block"
    x2 = x.reshape(b * seq, d)

    w_t = jnp.swapaxes(w_in[0], 0, 1)
    o_k = A_WIDTH
    o_v = 2 * A_WIDTH
    o_cq = 3 * A_WIDTH
    o_ckv = o_cq + Q_LORA
    o_kr = o_ckv + KV_LORA
    o_g = o_kr + QK_ROPE
    wk = w_t[o_k:o_v].astype(BF16)
    wqv = jnp.concatenate([w_t[:o_k], w_t[o_v:o_cq]], axis=0).astype(BF16)
    w_kr = w_t[o_kr:o_g]
    half = QK_ROPE // 2
    w_kr_rot = jnp.concatenate([-w_kr[half:], w_kr[:half]], axis=0)
    row_pad = lambda a: jnp.pad(a, ((QK_NOPE, MLA_HEAD_PAD - QK_NOPE - QK_ROPE), (0, 0)))
    wc = jnp.concatenate(
        [w_t[o_cq:o_kr], row_pad(w_kr), row_pad(w_kr_rot)], axis=0).astype(BF16)
    wg = w_t[o_g:].astype(BF16)

    uq = w_uq[0].reshape(Q_LORA, B_HEADS, QK_NOPE + QK_ROPE)
    uq_nope, uq_rope = uq[..., :QK_NOPE], uq[..., QK_NOPE:]
    wuq = _pad_heads(uq_nope, uq_rope).T.astype(BF16)
    wuk = _pad_heads(w_uk[0].reshape(KV_LORA, B_HEADS, QK_NOPE), None).astype(BF16)
    wuv = w_uv[0].T.astype(BF16)

    cos_t, sin_t = _rope_tables(seq)
    q_scale = (QK_NOPE + QK_ROPE) ** -0.5 * LOG2E
    row2 = lambda a: a.reshape(1, -1)

    qat, ka, vat, kmean, qmt, km, vbt = _inproj(
        x2, row2(norm_attn[0]), wk, wqv, wc, row2(mla_q_norm[0]), row2(mla_kv_norm[0]),
        wuq, wuk, wuv, (cos_t * q_scale).T, (sin_t * q_scale).T, cos_t, sin_t, b, seq)

    as_seq = lambda a: a.reshape(b, seq, a.shape[-1])
    bias = _bias_tiles(rel_bias)
    oa = _moba(qat, as_seq(ka), vat, kmean.reshape(b, seq // MOBA_BLOCK, A_WIDTH), bias)
    ob = _mla(qmt, as_seq(km), vbt)

    out = _out_block(
        x2, oa.reshape(-1, A_WIDTH, ATTN_TILE), ob.reshape(-1, B_WIDTH, ATTN_TILE),
        row2(norm_attn[0]),
        wg, w_proj_a[0].astype(BF16), w_proj_b[0].astype(BF16), w_out[0].astype(BF16),
        row2(norm_mlp[0]), w_mlp_up[0].astype(BF16), w_mlp_down[0].astype(BF16),
        row2(norm_final))
    return out.reshape(b, seq, d)
```

```python
import functools
import math

import jax
import jax.numpy as jnp
from jax import lax
from jax.experimental import pallas as pl
from jax.experimental.pallas import tpu as pltpu

D_MODEL = 1024
A_HEADS = 8
A_HEAD_DIM = 64
A_WIDTH = A_HEADS * A_HEAD_DIM
MOBA_BLOCK = 256
MOBA_TOPK = 3
REL_BUCKETS = 32
REL_MAX_DIST = 128
B_HEADS = 8
QK_NOPE = 64
QK_ROPE = 32
V_HEAD = 64
B_WIDTH = B_HEADS * V_HEAD
Q_LORA = 384
KV_LORA = 256
ROPE_THETA = 10000.0
D_FF = 4 * D_MODEL
EPS = 1e-6
NEG = -1e30
LOG2E = math.log2(math.e)

LANES = 128
MLA_HEAD_PAD = 128
ATTN_TILE = 256
HEAD_GROUP = 4
SUM_ROWS = 16
ITEM_SPAN = 2
PIPELINE_UNROLL = 16
ROW_TILE = 512
FF_CHUNK = 1024
VMEM_LIMIT = 56 * 1024 * 1024

F32 = jnp.float32
BF16 = jnp.bfloat16
NT_DIMS = (((1,), (1,)), ((), ()))


def _rms(xf, g):
    y = xf * lax.rsqrt(jnp.mean(xf * xf, axis=-1, keepdims=True) + EPS)
    return y * g


def _const_spec(shape):
    zeros = (0,) * len(shape)
    return pl.BlockSpec(shape, lambda *_: zeros, pipeline_mode=pl.Buffered(1))


def _bias_kernel(rel_ref, o_ref):
    h = pl.program_id(0)
    shape = (MOBA_BLOCK, 2 * MOBA_BLOCK)
    r = lax.broadcasted_iota(jnp.int32, shape, 0)
    c = lax.broadcasted_iota(jnp.int32, shape, 1)
    d = c - r
    dist = jnp.maximum(d, 0)
    max_exact = REL_BUCKETS // 2
    df = jnp.maximum(dist, 1).astype(F32)
    large = max_exact + (jnp.log(df / max_exact) / math.log(REL_MAX_DIST / max_exact)
                         * (REL_BUCKETS - max_exact)).astype(jnp.int32)
    large = jnp.minimum(large, REL_BUCKETS - 1)
    bucket = jnp.where(dist < max_exact, dist, large)
    val = jnp.zeros(shape, F32)
    for b in range(REL_BUCKETS):
        val = jnp.where(bucket == b, rel_ref[b, h], val)
    o_ref[...] = jnp.where(d >= 0, (val - rel_ref[REL_BUCKETS - 1, h]) * LOG2E, NEG)


def _bias_tiles(rel_bias):
    return pl.pallas_call(
        _bias_kernel,
        grid=(A_HEADS,),
        in_specs=[pl.BlockSpec(memory_space=pltpu.SMEM)],
        out_specs=pl.BlockSpec((None, MOBA_BLOCK, 2 * MOBA_BLOCK), lambda h: (h, 0, 0)),
        out_shape=jax.ShapeDtypeStruct((A_HEADS, MOBA_BLOCK, 2 * MOBA_BLOCK), F32),
        name="moba_bias_tiles",
    )(rel_bias)


def _inproj_kernel(x_ref, g_ref, wk_ref, wqv_ref, wc_ref, qn_ref, kvn_ref, wuq_ref,
                   wuk_ref, wuv_ref, cqt_ref, sqt_ref, ck_ref, sk_ref,
                   qat_ref, ka_ref, vat_ref, kmean_ref, qmt_ref, km_ref, vbt_ref):
    n = _rms(x_ref[...], g_ref[...]).astype(BF16)
    k = lax.dot_general(n, wk_ref[...], NT_DIMS, preferred_element_type=F32)
    ka_ref[...] = k.astype(BF16)
    nblk = ROW_TILE // ATTN_TILE
    kmean_ref[0] = jnp.sum(k.reshape(nblk, MOBA_BLOCK, A_WIDTH), axis=1) * (1.0 / MOBA_BLOCK)

    c = lax.dot_general(n, wc_ref[...], NT_DIMS, preferred_element_type=F32)
    cq = _rms(c[:, :Q_LORA], qn_ref[...]).astype(BF16)
    ckv = _rms(c[:, Q_LORA:Q_LORA + KV_LORA], kvn_ref[...]).astype(BF16)
    kr = c[:, Q_LORA + KV_LORA:Q_LORA + KV_LORA + LANES]
    krr = c[:, Q_LORA + KV_LORA + LANES:]
    for t in range(nblk):
        rows = slice(t * ATTN_TILE, (t + 1) * ATTN_TILE)
        qv_t = lax.dot_general(wqv_ref[...], n[rows], NT_DIMS,
                               preferred_element_type=F32)
        qat_ref[t] = (qv_t[:A_WIDTH] * (A_HEAD_DIM ** -0.5 * LOG2E)).astype(BF16)
        vat_ref[t] = qv_t[A_WIDTH:].astype(BF16)
        vbt_ref[t] = lax.dot_general(wuv_ref[...], ckv[rows], NT_DIMS,
                                     preferred_element_type=F32).astype(BF16)
        q_t = lax.dot_general(wuq_ref[...], cq[rows], NT_DIMS,
                              preferred_element_type=F32)
        cq_t, sq_t = cqt_ref[:, rows], sqt_ref[:, rows]
        half = QK_ROPE // 2
        for h in range(B_HEADS):
            q_h = q_t[h * MLA_HEAD_PAD:(h + 1) * MLA_HEAD_PAD]
            x1 = q_h[QK_NOPE:QK_NOPE + half]
            x2 = q_h[QK_NOPE + half:QK_NOPE + QK_ROPE]
            rot_h = jnp.concatenate(
                [jnp.zeros((QK_NOPE, ATTN_TILE), F32), -x2, x1,
                 jnp.zeros((MLA_HEAD_PAD - QK_NOPE - QK_ROPE, ATTN_TILE), F32)], axis=0)
            qmt_ref[t, h * MLA_HEAD_PAD:(h + 1) * MLA_HEAD_PAD, :] = (
                q_h * cq_t + rot_h * sq_t).astype(BF16)
    kn = jnp.dot(ckv, wuk_ref[...], preferred_element_type=F32)
    k_rope = kr * ck_ref[...] + krr * sk_ref[...]
    for h in range(B_HEADS):
        sl = slice(h * MLA_HEAD_PAD, (h + 1) * MLA_HEAD_PAD)
        km_ref[:, sl] = (kn[:, sl] + k_rope).astype(BF16)


def _inproj(x2, g_attn, wk, wqv, wc, qn, kvn, wuq, wuk, wuv, cq_tt, sq_tt, ck_t, sk_t,
            batch, seq):
    n_rows = x2.shape[0]
    tm = ROW_TILE
    steps = n_rows // tm
    per_seq = seq // tm
    nblk = tm // ATTN_TILE
    row = lambda i: (i, 0)
    tab = lambda i: (i % per_seq, 0)
    tab_t = lambda i: (0, i % per_seq)
    tile_map = lambda i: (i // per_seq, i % per_seq, 0, 0)
    bf = lambda w: jax.ShapeDtypeStruct((n_rows, w), BF16)
    tiles = lambda w: jax.ShapeDtypeStruct((batch, seq // ATTN_TILE, w, ATTN_TILE), BF16)
    tile_spec = lambda w: pl.BlockSpec((None, nblk, w, ATTN_TILE), tile_map)
    return pl.pallas_call(
        _inproj_kernel,
        grid=(steps,),
        in_specs=[
            pl.BlockSpec((tm, D_MODEL), row),
            _const_spec(g_attn.shape), _const_spec(wk.shape), _const_spec(wqv.shape),
            _const_spec(wc.shape), _const_spec(qn.shape), _const_spec(kvn.shape),
            _const_spec(wuq.shape), _const_spec(wuk.shape),
            _const_spec(wuv.shape),
            pl.BlockSpec((LANES, tm), tab_t), pl.BlockSpec((LANES, tm), tab_t),
            pl.BlockSpec((tm, LANES), tab), pl.BlockSpec((tm, LANES), tab),
        ],
        out_specs=[
            tile_spec(A_WIDTH), pl.BlockSpec((tm, A_WIDTH), row), tile_spec(A_WIDTH),
            pl.BlockSpec((1, nblk, A_WIDTH), lambda i: (i, 0, 0)),
            tile_spec(B_HEADS * MLA_HEAD_PAD),
            pl.BlockSpec((tm, B_HEADS * MLA_HEAD_PAD), row),
            tile_spec(B_WIDTH),
        ],
        out_shape=[
            tiles(A_WIDTH), bf(A_WIDTH), tiles(A_WIDTH),
            jax.ShapeDtypeStruct((steps, nblk, A_WIDTH), F32),
            tiles(B_HEADS * MLA_HEAD_PAD), bf(B_HEADS * MLA_HEAD_PAD), tiles(B_WIDTH),
        ],
        compiler_params=pltpu.CompilerParams(
            dimension_semantics=("arbitrary",), vmem_limit_bytes=VMEM_LIMIT),
        name="in_proj",
    )(x2, g_attn, wk, wqv, wc, qn, kvn, wuq, wuk, wuv, cq_tt, sq_tt, ck_t, sk_t)


def _store_scores(buf, h, parts):
    s_ref, max_ref = buf
    row = 0
    for idx, s_t in enumerate(parts):
        s_ref[h, row:row + s_t.shape[0]] = s_t
        max_ref[h, idx] = jnp.max(s_t, axis=0, keepdims=True)
        row += s_t.shape[0]


def _softmax_step(buf, h, parts, m_ref, acc_ref, first):
    s_ref, max_ref = buf
    maxes = [max_ref[h, idx] for idx in range(len(parts))]
    m_new = None if first else m_ref[...]
    for s_max, (_, _, col_keep) in zip(maxes, parts):
        if col_keep is not None:
            s_max = jnp.where(col_keep > 0, s_max, NEG)
        m_new = s_max if m_new is None else jnp.maximum(m_new, s_max)
    pv = None
    row = 0
    for s_max, (rows, v_t, col_keep) in zip(maxes, parts):
        ones = (lax.broadcasted_iota(jnp.int32, (SUM_ROWS, rows), 0) == 0).astype(BF16)
        v_aug = jnp.concatenate([v_t, ones], axis=0)
        m_exp = m_new if col_keep is None else jnp.maximum(m_new, s_max)
        p_t = jnp.exp2(s_ref[h, row:row + rows] - m_exp).astype(BF16)
        part = jnp.dot(v_aug, p_t, preferred_element_type=F32)
        if col_keep is not None:
            part = part * col_keep
        pv = part if pv is None else pv + part
        row += rows
    if first:
        acc_ref[...] = pv
    else:
        acc_ref[...] = jnp.exp2(m_ref[...] - m_new) * acc_ref[...] + pv
    m_ref[...] = m_new


def _pipeline(n_items, score, attend, bufs, unroll):
    if n_items == 0:
        return
    score(0, bufs[0])
    n_loops = (n_items - 1) // unroll

    def body(p, carry):
        for t in range(unroll):
            n = unroll * p + t
            score(n + 1, bufs[(t + 1) % 2])
            attend(n, bufs[t % 2])
        return carry

    lax.fori_loop(0, n_loops, body, 0)
    done = unroll * n_loops
    for n in range(done, n_items):
        if n + 1 < n_items:
            score(n + 1, bufs[(n + 1 - done) % 2])
        attend(n, bufs[(n - done) % 2])


def _tile_rows(j, span=1):
    return pl.ds(pl.multiple_of(j * ATTN_TILE, ATTN_TILE), span * ATTN_TILE)


def _write_heads(acc_ref, o_ref, nq, head_dim):
    def one(qi, carry):
        for h in range(HEAD_GROUP):
            acc = acc_ref[qi, h]
            o_ref[qi, h * head_dim:(h + 1) * head_dim, :] = (
                acc[:head_dim] / acc[head_dim:head_dim + 1]).astype(o_ref.dtype)
        return carry

    lax.fori_loop(0, nq, one, 0)


def _causal_items(nq, n_near, pair_far):
    phases = [[(i, i - d) for i in range(d, nq)] for d in range(n_near)]
    pairs, singles = [], []
    for i in range(n_near, nq):
        n_far = i - n_near + 1
        n_paired = n_far - n_far % 2 if pair_far else 0
        pairs += [(i, j) for j in range(0, n_paired, 2)]
        singles += [(i, j) for j in range(n_paired, n_far)]
    phases += [pairs, singles]
    flat = [it for ph in phases for it in ph]
    tq = jnp.asarray([it[0] for it in flat], jnp.int32)
    tk = jnp.asarray([it[1] for it in flat], jnp.int32)
    return tq, tk, [len(ph) for ph in phases]


def _moba_kernel(tq_ref, tk_ref, qt_ref, k_ref, vt_ref, kmean_ref, bias_ref, o_ref,
                 keep_ref, s0_ref, s1_ref, t0_ref, t1_ref, m_ref, acc_ref, *, seq, phases):
    nblk = seq // MOBA_BLOCK

    def head_query(qi, h):
        pr, sub = divmod(h, 2)
        q_h = qt_ref[qi, pl.ds(pr * LANES + sub * A_HEAD_DIM, A_HEAD_DIM), :]
        zeros = jnp.zeros_like(q_h)
        return jnp.concatenate([q_h, zeros] if sub == 0 else [zeros, q_h], axis=0)

    def select_blocks(i):
        row = lax.broadcasted_iota(jnp.int32, (nblk, ATTN_TILE), 0)
        rowf = row.astype(F32)
        past = row < i
        for h in range(HEAD_GROUP):
            kmean = kmean_ref[:, (h // 2) * LANES:(h // 2 + 1) * LANES].astype(BF16)
            gate = jnp.dot(kmean, head_query(i, h), preferred_element_type=F32)
            gate = jnp.where(past, gate, -jnp.inf)
            picked = jnp.zeros((nblk, ATTN_TILE), F32)
            for _ in range(MOBA_TOPK):
                top = jnp.max(gate, axis=0, keepdims=True)
                first = jnp.min(jnp.where(gate == top, rowf, float(nblk)), axis=0, keepdims=True)
                pick = rowf == first
                picked = jnp.where(pick, 1.0, picked)
                gate = jnp.where(pick, -jnp.inf, gate)
            keep_ref[i, h] = jnp.where(past, picked, 0.0)

    def stage(offset, span, bias_cols, first):
        def score(n, buf):
            qi, kj = tq_ref[offset + n], tk_ref[offset + n]
            for h in range(HEAD_GROUP):
                k_t = k_ref[_tile_rows(kj, span), (h // 2) * LANES:(h // 2 + 1) * LANES]
                s_t = jnp.dot(k_t, head_query(qi, h), preferred_element_type=F32)
                if bias_cols is not None:
                    s_t = s_t + bias_ref[h, :, bias_cols]
                _store_scores(buf, h, [s_t[t * MOBA_BLOCK:(t + 1) * MOBA_BLOCK]
                                       for t in range(span)])

        def attend(n, buf):
            qi, kj = tq_ref[offset + n], tk_ref[offset + n]
            for h in range(HEAD_GROUP):
                parts = [(MOBA_BLOCK, vt_ref[kj + t, h * A_HEAD_DIM:(h + 1) * A_HEAD_DIM, :],
                          None if first else keep_ref[qi, h, pl.ds(kj + t, 1), :])
                         for t in range(span)]
                _softmax_step(buf, h, parts, m_ref.at[qi, h], acc_ref.at[qi, h], first)
            if first:
                select_blocks(qi)

        return score, attend

    bufs = ((s0_ref, t0_ref), (s1_ref, t1_ref))
    n_own, n_prev, n_pairs, n_single = phases
    _pipeline(n_own, *stage(0, 1, slice(0, MOBA_BLOCK), True), bufs, PIPELINE_UNROLL)
    offset = n_own
    _pipeline(n_prev, *stage(offset, 1, slice(MOBA_BLOCK, 2 * MOBA_BLOCK), False), bufs,
              PIPELINE_UNROLL)
    offset += n_prev
    _pipeline(n_pairs, *stage(offset, 2, None, False), bufs, PIPELINE_UNROLL // 2)
    offset += n_pairs
    _pipeline(n_single, *stage(offset, 1, None, False), bufs, PIPELINE_UNROLL)
    _write_heads(acc_ref, o_ref, nblk, A_HEAD_DIM)


def _moba(qat, ka, vat, kmean, bias):
    b, seq, _ = ka.shape
    nq = seq // ATTN_TILE
    gw = HEAD_GROUP * A_HEAD_DIM
    tq, tk, phases = _causal_items(nq, 2, pair_far=True)
    per_group = lambda bi, g, *_: (bi, 0, g)
    tiles = lambda bi, g, *_: (bi, 0, g, 0)
    return pl.pallas_call(
        functools.partial(_moba_kernel, seq=seq, phases=phases),
        grid_spec=pltpu.PrefetchScalarGridSpec(
            num_scalar_prefetch=2,
            grid=(b, A_HEADS // HEAD_GROUP),
            in_specs=[
                pl.BlockSpec((None, nq, gw, ATTN_TILE), tiles),
                pl.BlockSpec((None, seq, gw), per_group),
                pl.BlockSpec((None, nq, gw, MOBA_BLOCK), tiles),
                pl.BlockSpec((None, nq, gw), per_group),
                pl.BlockSpec((HEAD_GROUP, MOBA_BLOCK, 2 * MOBA_BLOCK), lambda bi, g, *_: (g, 0, 0)),
            ],
            out_specs=pl.BlockSpec((None, nq, gw, ATTN_TILE), tiles),
            scratch_shapes=[
                pltpu.VMEM((nq, HEAD_GROUP, nq, ATTN_TILE), F32),
                pltpu.VMEM((HEAD_GROUP, ITEM_SPAN * ATTN_TILE, ATTN_TILE), F32),
                pltpu.VMEM((HEAD_GROUP, ITEM_SPAN * ATTN_TILE, ATTN_TILE), F32),
                pltpu.VMEM((HEAD_GROUP, ITEM_SPAN, 1, ATTN_TILE), F32),
                pltpu.VMEM((HEAD_GROUP, ITEM_SPAN, 1, ATTN_TILE), F32),
                pltpu.VMEM((nq, HEAD_GROUP, 1, ATTN_TILE), F32),
                pltpu.VMEM((nq, HEAD_GROUP, A_HEAD_DIM + SUM_ROWS, ATTN_TILE), F32),
            ]),
        out_shape=jax.ShapeDtypeStruct((b, nq, A_WIDTH, ATTN_TILE), BF16),
        compiler_params=pltpu.CompilerParams(
            dimension_semantics=("arbitrary", "arbitrary"), vmem_limit_bytes=VMEM_LIMIT),
        name="moba_attention",
    )(tq, tk, qat, ka, vat, kmean, bias)


def _mla_kernel(tq_ref, tk_ref, qt_ref, k_ref, vt_ref, o_ref, s0_ref, s1_ref, t0_ref, t1_ref,
                m_ref, acc_ref, *, nq, phases):
    def stage(offset, span, diagonal):
        def score(n, buf):
            qi, kj = tq_ref[offset + n], tk_ref[offset + n]
            for h in range(HEAD_GROUP):
                sl = slice(h * MLA_HEAD_PAD, (h + 1) * MLA_HEAD_PAD)
                s_t = jnp.dot(k_ref[_tile_rows(kj, span), sl], qt_ref[qi, sl, :],
                              preferred_element_type=F32)
                if diagonal:
                    key = lax.broadcasted_iota(jnp.int32, (ATTN_TILE, ATTN_TILE), 0)
                    qry = lax.broadcasted_iota(jnp.int32, (ATTN_TILE, ATTN_TILE), 1)
                    s_t = jnp.where(key <= qry, s_t, NEG)
                _store_scores(buf, h, [s_t[t * ATTN_TILE:(t + 1) * ATTN_TILE]
                                       for t in range(span)])

        def attend(n, buf):
            qi, kj = tq_ref[offset + n], tk_ref[offset + n]
            for h in range(HEAD_GROUP):
                parts = [(ATTN_TILE, vt_ref[kj + t, h * V_HEAD:(h + 1) * V_HEAD, :], None)
                         for t in range(span)]
                _softmax_step(buf, h, parts, m_ref.at[qi, h], acc_ref.at[qi, h], diagonal)

        return score, attend

    bufs = ((s0_ref, t0_ref), (s1_ref, t1_ref))
    n_diag, n_pairs, n_single = phases
    _pipeline(n_diag, *stage(0, 1, True), bufs, PIPELINE_UNROLL)
    _pipeline(n_pairs, *stage(n_diag, 2, False), bufs, PIPELINE_UNROLL // 2)
    _pipeline(n_single, *stage(n_diag + n_pairs, 1, False), bufs, PIPELINE_UNROLL)
    _write_heads(acc_ref, o_ref, nq, V_HEAD)


def _mla(qmt, km, vbt):
    b, seq, _ = km.shape
    nq = seq // ATTN_TILE
    qw = HEAD_GROUP * MLA_HEAD_PAD
    vw = HEAD_GROUP * V_HEAD
    tq, tk, phases = _causal_items(nq, 1, pair_far=False)
    per_group = lambda bi, g, *_: (bi, 0, g)
    tiles = lambda bi, g, *_: (bi, 0, g, 0)
    return pl.pallas_call(
        functools.partial(_mla_kernel, nq=nq, phases=phases),
        grid_spec=pltpu.PrefetchScalarGridSpec(
            num_scalar_prefetch=2,
            grid=(b, B_HEADS // HEAD_GROUP),
            in_specs=[
                pl.BlockSpec((None, nq, qw, ATTN_TILE), tiles),
                pl.BlockSpec((None, seq, qw), per_group),
                pl.BlockSpec((None, nq, vw, ATTN_TILE), tiles),
            ],
            out_specs=pl.BlockSpec((None, nq, vw, ATTN_TILE), tiles),
            scratch_shapes=[
                pltpu.VMEM((HEAD_GROUP, ATTN_TILE, ATTN_TILE), F32),
                pltpu.VMEM((HEAD_GROUP, ATTN_TILE, ATTN_TILE), F32),
                pltpu.VMEM((HEAD_GROUP, 1, 1, ATTN_TILE), F32),
                pltpu.VMEM((HEAD_GROUP, 1, 1, ATTN_TILE), F32),
                pltpu.VMEM((nq, HEAD_GROUP, 1, ATTN_TILE), F32),
                pltpu.VMEM((nq, HEAD_GROUP, V_HEAD + SUM_ROWS, ATTN_TILE), F32),
            ]),
        out_shape=jax.ShapeDtypeStruct((b, nq, B_WIDTH, ATTN_TILE), BF16),
        compiler_params=pltpu.CompilerParams(
            dimension_semantics=("arbitrary", "arbitrary"), vmem_limit_bytes=VMEM_LIMIT),
        name="mla_attention",
    )(tq, tk, qmt, km, vbt)


def _out_kernel(x_ref, oa_ref, ob_ref, gattn_ref, wg_ref, wpa_ref, wpb_ref, wout_ref,
                gmlp_ref, wup_ref, wdn_ref, gfin_ref, o_ref):
    x = x_ref[...]
    n = _rms(x, gattn_ref[...]).astype(BF16)
    gates = lax.dot_general(n, wg_ref[...], NT_DIMS, preferred_element_type=F32)
    tn_dims = (((0,), (0,)), ((), ()))
    tiles = range(ROW_TILE // ATTN_TILE)
    pa = jnp.concatenate([lax.dot_general(oa_ref[t], wpa_ref[...], tn_dims,
                                          preferred_element_type=F32) for t in tiles], axis=0)
    pb = jnp.concatenate([lax.dot_general(ob_ref[t], wpb_ref[...], tn_dims,
                                          preferred_element_type=F32) for t in tiles], axis=0)
    merged = (jax.nn.sigmoid(gates[:, :D_MODEL]) * pa
              + jax.nn.sigmoid(gates[:, D_MODEL:]) * pb).astype(BF16)
    h = x + jnp.dot(merged, wout_ref[...], preferred_element_type=F32)
    m = _rms(h, gmlp_ref[...]).astype(BF16)
    for c in range(D_FF // FF_CHUNK):
        cols = slice(c * FF_CHUNK, (c + 1) * FF_CHUNK)
        up = jnp.dot(m, wup_ref[:, cols], preferred_element_type=F32)
        act = jnp.square(jnp.maximum(up, 0.0)).astype(BF16)
        h = h + jnp.dot(act, wdn_ref[cols, :], preferred_element_type=F32)
    o_ref[...] = _rms(h, gfin_ref[...])


def _out_block(x2, oa, ob, g_attn, wg, wpa, wpb, wout, g_mlp, wup, wdn, g_fin):
    n_rows = x2.shape[0]
    tm = ROW_TILE
    row = lambda i: (i, 0)
    return pl.pallas_call(
        _out_kernel,
        grid=(n_rows // tm,),
        in_specs=[
            pl.BlockSpec((tm, D_MODEL), row),
            pl.BlockSpec((tm // ATTN_TILE, A_WIDTH, ATTN_TILE), lambda i: (i, 0, 0)),
            pl.BlockSpec((tm // ATTN_TILE, B_WIDTH, ATTN_TILE), lambda i: (i, 0, 0)),
            _const_spec(g_attn.shape), _const_spec(wg.shape), _const_spec(wpa.shape),
            _const_spec(wpb.shape), _const_spec(wout.shape), _const_spec(g_mlp.shape),
            _const_spec(wup.shape), _const_spec(wdn.shape), _const_spec(g_fin.shape),
        ],
        out_specs=pl.BlockSpec((tm, D_MODEL), row),
        out_shape=jax.ShapeDtypeStruct((n_rows, D_MODEL), F32),
        compiler_params=pltpu.CompilerParams(
            dimension_semantics=("arbitrary",), vmem_limit_bytes=VMEM_LIMIT),
        name="out_mlp",
    )(x2, oa, ob, g_attn, wg, wpa, wpb, wout, g_mlp, wup, wdn, g_fin)


def _rope_tables(seq):
    half = QK_ROPE // 2
    inv_freq = ROPE_THETA ** (-jnp.arange(half, dtype=F32) / half)
    ang = jnp.arange(seq).astype(F32)[:, None] * inv_freq[None, :]
    cos2 = jnp.tile(jnp.cos(ang), (1, 2))
    sin2 = jnp.tile(jnp.sin(ang), (1, 2))
    pad = jnp.zeros((seq, MLA_HEAD_PAD - QK_NOPE - QK_ROPE), F32)
    cos_t = jnp.concatenate([jnp.ones((seq, QK_NOPE), F32), cos2, pad], axis=1)
    sin_t = jnp.concatenate([jnp.zeros((seq, QK_NOPE), F32), sin2, pad], axis=1)
    return cos_t, sin_t


def _pad_heads(w_nope, w_rope):
    k, h, _ = w_nope.shape
    if w_rope is None:
        w_rope = jnp.zeros((k, h, QK_ROPE), w_nope.dtype)
    pad = jnp.zeros((k, h, MLA_HEAD_PAD - QK_NOPE - QK_ROPE), w_nope.dtype)
    return jnp.concatenate([w_nope, w_rope, pad], axis=-1).reshape(k, h * MLA_HEAD_PAD)


def kernel(x, w_in, rel_bias, mla_q_norm, w_uq, mla_kv_norm, w_uk, w_uv, w_proj_a, w_proj_b,
           w_out, norm_attn, norm_mlp, w_mlp_up, w_mlp_down, norm_final):
    b, seq, d = x.shape
    assert d == D_MODEL and seq % ROW_TILE == 0 and seq // MOBA_BLOCK <= LANES
    assert w_in.shape[0] == 1, "single-layer block"
    x2 = x.reshape(b * seq, d)

    w_t = jnp.swapaxes(w_in[0], 0, 1)
    o_k = A_WIDTH
    o_v = 2 * A_WIDTH
    o_cq = 3 * A_WIDTH
    o_ckv = o_cq + Q_LORA
    o_kr = o_ckv + KV_LORA
    o_g = o_kr + QK_ROPE
    wk = w_t[o_k:o_v].astype(BF16)
    wqv = jnp.concatenate([w_t[:o_k], w_t[o_v:o_cq]], axis=0).astype(BF16)
    w_kr = w_t[o_kr:o_g]
    half = QK_ROPE // 2
    w_kr_rot = jnp.concatenate([-w_kr[half:], w_kr[:half]], axis=0)
    row_pad = lambda a: jnp.pad(a, ((QK_NOPE, MLA_HEAD_PAD - QK_NOPE - QK_ROPE), (0, 0)))
    wc = jnp.concatenate(
        [w_t[o_cq:o_kr], row_pad(w_kr), row_pad(w_kr_rot)], axis=0).astype(BF16)
    wg = w_t[o_g:].astype(BF16)

    uq = w_uq[0].reshape(Q_LORA, B_HEADS, QK_NOPE + QK_ROPE)
    uq_nope, uq_rope = uq[..., :QK_NOPE], uq[..., QK_NOPE:]
    wuq = _pad_heads(uq_nope, uq_rope).T.astype(BF16)
    wuk = _pad_heads(w_uk[0].reshape(KV_LORA, B_HEADS, QK_NOPE), None).astype(BF16)
    wuv = w_uv[0].T.astype(BF16)

    cos_t, sin_t = _rope_tables(seq)
    q_scale = (QK_NOPE + QK_ROPE) ** -0.5 * LOG2E
    row2 = lambda a: a.reshape(1, -1)

    qat, ka, vat, kmean, qmt, km, vbt = _inproj(
        x2, row2(norm_attn[0]), wk, wqv, wc, row2(mla_q_norm[0]), row2(mla_kv_norm[0]),
        wuq, wuk, wuv, (cos_t * q_scale).T, (sin_t * q_scale).T, cos_t, sin_t, b, seq)

    as_seq = lambda a: a.reshape(b, seq, a.shape[-1])
    bias = _bias_tiles(rel_bias)
    oa = _moba(qat, as_seq(ka), vat, kmean.reshape(b, seq // MOBA_BLOCK, A_WIDTH), bias)
    ob = _mla(qmt, as_seq(km), vbt)

    out = _out_block(
        x2, oa.reshape(-1, A_WIDTH, ATTN_TILE), ob.reshape(-1, B_WIDTH, ATTN_TILE),
        row2(norm_attn[0]),
        wg, w_proj_a[0].astype(BF16), w_proj_b[0].astype(BF16), w_out[0].astype(BF16),
        row2(norm_mlp[0]), w_mlp_up[0].astype(BF16), w_mlp_down[0].astype(BF16),
        row2(norm_final))
    return out.reshape(b, seq, d)
```

```python
import functools
import math

import jax
import jax.numpy as jnp
from jax import lax
from jax.experimental import pallas as pl
from jax.experimental.pallas import tpu as pltpu

D_MODEL = 1024
A_HEADS = 8
A_HEAD_DIM = 64
A_WIDTH = A_HEADS * A_HEAD_DIM
MOBA_BLOCK = 256
MOBA_TOPK = 3
REL_BUCKETS = 32
REL_MAX_DIST = 128
B_HEADS = 8
QK_NOPE = 64
QK_ROPE = 32
V_HEAD = 64
B_WIDTH = B_HEADS * V_HEAD
Q_LORA = 384
KV_LORA = 256
ROPE_THETA = 10000.0
D_FF = 4 * D_MODEL
EPS = 1e-6
NEG = -1e30
LOG2E = math.log2(math.e)

LANES = 128
MLA_HEAD_PAD = 128
ATTN_TILE = 256
HEAD_GROUP = 4
SUM_ROWS = 16
ITEM_SPAN = 2
PIPELINE_UNROLL = 16
ROW_TILE = 512
FF_CHUNK = 1024
VMEM_LIMIT = 56 * 1024 * 1024

F32 = jnp.float32
BF16 = jnp.bfloat16
NT_DIMS = (((1,), (1,)), ((), ()))


def _rms(xf, g):
    y = xf * lax.rsqrt(jnp.mean(xf * xf, axis=-1, keepdims=True) + EPS)
    return y * g


def _const_spec(shape):
    zeros = (0,) * len(shape)
    return pl.BlockSpec(shape, lambda *_: zeros, pipeline_mode=pl.Buffered(1))


def _bias_kernel(rel_ref, o_ref):
    h = pl.program_id(0)
    shape = (MOBA_BLOCK, 2 * MOBA_BLOCK)
    r = lax.broadcasted_iota(jnp.int32, shape, 0)
    c = lax.broadcasted_iota(jnp.int32, shape, 1)
    d = c - r
    dist = jnp.maximum(d, 0)
    max_exact = REL_BUCKETS // 2
    df = jnp.maximum(dist, 1).astype(F32)
    large = max_exact + (jnp.log(df / max_exact) / math.log(REL_MAX_DIST / max_exact)
                         * (REL_BUCKETS - max_exact)).astype(jnp.int32)
    large = jnp.minimum(large, REL_BUCKETS - 1)
    bucket = jnp.where(dist < max_exact, dist, large)
    val = jnp.zeros(shape, F32)
    for b in range(REL_BUCKETS):
        val = jnp.where(bucket == b, rel_ref[b, h], val)
    o_ref[...] = jnp.where(d >= 0, (val - rel_ref[REL_BUCKETS - 1, h]) * LOG2E, NEG)


def _bias_tiles(rel_bias):
    return pl.pallas_call(
        _bias_kernel,
        grid=(A_HEADS,),
        in_specs=[pl.BlockSpec(memory_space=pltpu.SMEM)],
        out_specs=pl.BlockSpec((None, MOBA_BLOCK, 2 * MOBA_BLOCK), lambda h: (h, 0, 0)),
        out_shape=jax.ShapeDtypeStruct((A_HEADS, MOBA_BLOCK, 2 * MOBA_BLOCK), F32),
        name="moba_bias_tiles",
    )(rel_bias)


def _inproj_kernel(x_ref, g_ref, wk_ref, wqv_ref, wc_ref, qn_ref, kvn_ref, wuq_ref,
                   wuk_ref, wuv_ref, cqt_ref, sqt_ref, ck_ref, sk_ref,
                   qat_ref, ka_ref, vat_ref, kmean_ref, qmt_ref, km_ref, vbt_ref):
    n = _rms(x_ref[...], g_ref[...]).astype(BF16)
    k = lax.dot_general(n, wk_ref[...], NT_DIMS, preferred_element_type=F32)
    ka_ref[...] = k.astype(BF16)
    nblk = ROW_TILE // ATTN_TILE
    kmean_ref[0] = jnp.sum(k.reshape(nblk, MOBA_BLOCK, A_WIDTH), axis=1) * (1.0 / MOBA_BLOCK)

    c = lax.dot_general(n, wc_ref[...], NT_DIMS, preferred_element_type=F32)
    cq = _rms(c[:, :Q_LORA], qn_ref[...]).astype(BF16)
    ckv = _rms(c[:, Q_LORA:Q_LORA + KV_LORA], kvn_ref[...]).astype(BF16)
    kr = c[:, Q_LORA + KV_LORA:Q_LORA + KV_LORA + LANES]
    krr = c[:, Q_LORA + KV_LORA + LANES:]
    for t in range(nblk):
        rows = slice(t * ATTN_TILE, (t + 1) * ATTN_TILE)
        qv_t = lax.dot_general(wqv_ref[...], n[rows], NT_DIMS,
                               preferred_element_type=F32)
        qat_ref[t] = (qv_t[:A_WIDTH] * (A_HEAD_DIM ** -0.5 * LOG2E)).astype(BF16)
        vat_ref[t] = qv_t[A_WIDTH:].astype(BF16)
        vbt_ref[t] = lax.dot_general(wuv_ref[...], ckv[rows], NT_DIMS,
                                     preferred_element_type=F32).astype(BF16)
        q_t = lax.dot_general(wuq_ref[...], cq[rows], NT_DIMS,
                              preferred_element_type=F32)
        cq_t, sq_t = cqt_ref[:, rows], sqt_ref[:, rows]
        half = QK_ROPE // 2
        for h in range(B_HEADS):
            q_h = q_t[h * MLA_HEAD_PAD:(h + 1) * MLA_HEAD_PAD]
            x1 = q_h[QK_NOPE:QK_NOPE + half]
            x2 = q_h[QK_NOPE + half:QK_NOPE + QK_ROPE]
            rot_h = jnp.concatenate(
                [jnp.zeros((QK_NOPE, ATTN_TILE), F32), -x2, x1,
                 jnp.zeros((MLA_HEAD_PAD - QK_NOPE - QK_ROPE, ATTN_TILE), F32)], axis=0)
            qmt_ref[t, h * MLA_HEAD_PAD:(h + 1) * MLA_HEAD_PAD, :] = (
                q_h * cq_t + rot_h * sq_t).astype(BF16)
    kn = jnp.dot(ckv, wuk_ref[...], preferred_element_type=F32)
    k_rope = kr * ck_ref[...] + krr * sk_ref[...]
    for h in range(B_HEADS):
        sl = slice(h * MLA_HEAD_PAD, (h + 1) * MLA_HEAD_PAD)
        km_ref[:, sl] = (kn[:, sl] + k_rope).astype(BF16)


def _inproj(x2, g_attn, wk, wqv, wc, qn, kvn, wuq, wuk, wuv, cq_tt, sq_tt, ck_t, sk_t,
            batch, seq):
    n_rows = x2.shape[0]
    tm = ROW_TILE
    steps = n_rows // tm
    per_seq = seq // tm
    nblk = tm // ATTN_TILE
    row = lambda i: (i, 0)
    tab = lambda i: (i % per_seq, 0)
    tab_t = lambda i: (0, i % per_seq)
    tile_map = lambda i: (i // per_seq, i % per_seq, 0, 0)
    bf = lambda w: jax.ShapeDtypeStruct((n_rows, w), BF16)
    tiles = lambda w: jax.ShapeDtypeStruct((batch, seq // ATTN_TILE, w, ATTN_TILE), BF16)
    tile_spec = lambda w: pl.BlockSpec((None, nblk, w, ATTN_TILE), tile_map)
    return pl.pallas_call(
        _inproj_kernel,
        grid=(steps,),
        in_specs=[
            pl.BlockSpec((tm, D_MODEL), row),
            _const_spec(g_attn.shape), _const_spec(wk.shape), _const_spec(wqv.shape),
            _const_spec(wc.shape), _const_spec(qn.shape), _const_spec(kvn.shape),
            _const_spec(wuq.shape), _const_spec(wuk.shape),
            _const_spec(wuv.shape),
            pl.BlockSpec((LANES, tm), tab_t), pl.BlockSpec((LANES, tm), tab_t),
            pl.BlockSpec((tm, LANES), tab), pl.BlockSpec((tm, LANES), tab),
        ],
        out_specs=[
            tile_spec(A_WIDTH), pl.BlockSpec((tm, A_WIDTH), row), tile_spec(A_WIDTH),
            pl.BlockSpec((1, nblk, A_WIDTH), lambda i: (i, 0, 0)),
            tile_spec(B_HEADS * MLA_HEAD_PAD),
            pl.BlockSpec((tm, B_HEADS * MLA_HEAD_PAD), row),
            tile_spec(B_WIDTH),
        ],
        out_shape=[
            tiles(A_WIDTH), bf(A_WIDTH), tiles(A_WIDTH),
            jax.ShapeDtypeStruct((steps, nblk, A_WIDTH), F32),
            tiles(B_HEADS * MLA_HEAD_PAD), bf(B_HEADS * MLA_HEAD_PAD), tiles(B_WIDTH),
        ],
        compiler_params=pltpu.CompilerParams(
            dimension_semantics=("arbitrary",), vmem_limit_bytes=VMEM_LIMIT),
        name="in_proj",
    )(x2, g_attn, wk, wqv, wc, qn, kvn, wuq, wuk, wuv, cq_tt, sq_tt, ck_t, sk_t)


def _store_scores(buf, h, parts):
    s_ref, max_ref = buf
    row = 0
    for idx, s_t in enumerate(parts):
        s_ref[h, row:row + s_t.shape[0]] = s_t
        max_ref[h, idx] = jnp.max(s_t, axis=0, keepdims=True)
        row += s_t.shape[0]


def _softmax_step(buf, h, parts, m_ref, acc_ref, first):
    s_ref, max_ref = buf
    maxes = [max_ref[h, idx] for idx in range(len(parts))]
    m_new = None if first else m_ref[...]
    for s_max, (_, _, col_keep) in zip(maxes, parts):
        if col_keep is not None:
            s_max = jnp.where(col_keep > 0, s_max, NEG)
        m_new = s_max if m_new is None else jnp.maximum(m_new, s_max)
    pv = None
    row = 0
    for s_max, (rows, v_t, col_keep) in zip(maxes, parts):
        ones = (lax.broadcasted_iota(jnp.int32, (SUM_ROWS, rows), 0) == 0).astype(BF16)
        v_aug = jnp.concatenate([v_t, ones], axis=0)
        m_exp = m_new if col_keep is None else jnp.maximum(m_new, s_max)
        p_t = jnp.exp2(s_ref[h, row:row + rows] - m_exp).astype(BF16)
        part = jnp.dot(v_aug, p_t, preferred_element_type=F32)
        if col_keep is not None:
            part = part * col_keep
        pv = part if pv is None else pv + part
        row += rows
    if first:
        acc_ref[...] = pv
    else:
        acc_ref[...] = jnp.exp2(m_ref[...] - m_new) * acc_ref[...] + pv
    m_ref[...] = m_new


def _pipeline(n_items, score, attend, bufs, unroll):
    if n_items == 0:
        return
    score(0, bufs[0])
    n_loops = (n_items - 1) // unroll

    def body(p, carry):
        for t in range(unroll):
            n = unroll * p + t
            score(n + 1, bufs[(t + 1) % 2])
            attend(n, bufs[t % 2])
        return carry

    lax.fori_loop(0, n_loops, body, 0)
    done = unroll * n_loops
    for n in range(done, n_items):
        if n + 1 < n_items:
            score(n + 1, bufs[(n + 1 - done) % 2])
        attend(n, bufs[(n - done) % 2])


def _tile_rows(j, span=1):
    return pl.ds(pl.multiple_of(j * ATTN_TILE, ATTN_TILE), span * ATTN_TILE)


def _write_heads(acc_ref, o_ref, nq, head_dim):
    def one(qi, carry):
        for h in range(HEAD_GROUP):
            acc = acc_ref[qi, h]
            o_ref[qi, h * head_dim:(h + 1) * head_dim, :] = (
                acc[:head_dim] / acc[head_dim:head_dim + 1]).astype(o_ref.dtype)
        return carry

    lax.fori_loop(0, nq, one, 0)


def _causal_items(nq, n_near, pair_far):
    phases = [[(i, i - d) for i in range(d, nq)] for d in range(n_near)]
    pairs, singles = [], []
    for i in range(n_near, nq):
        n_far = i - n_near + 1
        n_paired = n_far - n_far % 2 if pair_far else 0
        pairs += [(i, j) for j in range(0, n_paired, 2)]
        singles += [(i, j) for j in range(n_paired, n_far)]
    phases += [pairs, singles]
    flat = [it for ph in phases for it in ph]
    tq = jnp.asarray([it[0] for it in flat], jnp.int32)
    tk = jnp.asarray([it[1] for it in flat], jnp.int32)
    return tq, tk, [len(ph) for ph in phases]


def _moba_kernel(tq_ref, tk_ref, qt_ref, k_ref, vt_ref, kmean_ref, bias_ref, o_ref,
                 keep_ref, s0_ref, s1_ref, t0_ref, t1_ref, m_ref, acc_ref, *, seq, phases):
    nblk = seq // MOBA_BLOCK

    def head_query(qi, h):
        pr, sub = divmod(h, 2)
        q_h = qt_ref[qi, pl.ds(pr * LANES + sub * A_HEAD_DIM, A_HEAD_DIM), :]
        zeros = jnp.zeros_like(q_h)
        return jnp.concatenate([q_h, zeros] if sub == 0 else [zeros, q_h], axis=0)

    def select_blocks(i):
        row = lax.broadcasted_iota(jnp.int32, (nblk, ATTN_TILE), 0)
        rowf = row.astype(F32)
        past = row < i
        for h in range(HEAD_GROUP):
            kmean = kmean_ref[:, (h // 2) * LANES:(h // 2 + 1) * LANES].astype(BF16)
            gate = jnp.dot(kmean, head_query(i, h), preferred_element_type=F32)
            gate = jnp.where(past, gate, -jnp.inf)
            picked = jnp.zeros((nblk, ATTN_TILE), F32)
            for _ in range(MOBA_TOPK):
                top = jnp.max(gate, axis=0, keepdims=True)
                first = jnp.min(jnp.where(gate == top, rowf, float(nblk)), axis=0, keepdims=True)
                pick = rowf == first
                picked = jnp.where(pick, 1.0, picked)
                gate = jnp.where(pick, -jnp.inf, gate)
            keep_ref[i, h] = jnp.where(past, picked, 0.0)

    def stage(offset, span, bias_cols, first):
        def score(n, buf):
            qi, kj = tq_ref[offset + n], tk_ref[offset + n]
            for h in range(HEAD_GROUP):
                k_t = k_ref[_tile_rows(kj, span), (h // 2) * LANES:(h // 2 + 1) * LANES]
                s_t = jnp.dot(k_t, head_query(qi, h), preferred_element_type=F32)
                if bias_cols is not None:
                    s_t = s_t + bias_ref[h, :, bias_cols]
                _store_scores(buf, h, [s_t[t * MOBA_BLOCK:(t + 1) * MOBA_BLOCK]
                                       for t in range(span)])

        def attend(n, buf):
            qi, kj = tq_ref[offset + n], tk_ref[offset + n]
            for h in range(HEAD_GROUP):
                parts = [(MOBA_BLOCK, vt_ref[kj + t, h * A_HEAD_DIM:(h + 1) * A_HEAD_DIM, :],
                          None if first else keep_ref[qi, h, pl.ds(kj + t, 1), :])
                         for t in range(span)]
                _softmax_step(buf, h, parts, m_ref.at[qi, h], acc_ref.at[qi, h], first)
            if first:
                select_blocks(qi)

        return score, attend

    bufs = ((s0_ref, t0_ref), (s1_ref, t1_ref))
    n_own, n_prev, n_pairs, n_single = phases
    _pipeline(n_own, *stage(0, 1, slice(0, MOBA_BLOCK), True), bufs, PIPELINE_UNROLL)
    offset = n_own
    _pipeline(n_prev, *stage(offset, 1, slice(MOBA_BLOCK, 2 * MOBA_BLOCK), False), bufs,
              PIPELINE_UNROLL)
    offset += n_prev
    _pipeline(n_pairs, *stage(offset, 2, None, False), bufs, PIPELINE_UNROLL // 2)
    offset += n_pairs
    _pipeline(n_single, *stage(offset, 1, None, False), bufs, PIPELINE_UNROLL)
    _write_heads(acc_ref, o_ref, nblk, A_HEAD_DIM)


def _moba(qat, ka, vat, kmean, bias):
    b, seq, _ = ka.shape
    nq = seq // ATTN_TILE
    gw = HEAD_GROUP * A_HEAD_DIM
    tq, tk, phases = _causal_items(nq, 2, pair_far=True)
    per_group = lambda bi, g, *_: (bi, 0, g)
    tiles = lambda bi, g, *_: (bi, 0, g, 0)
    return pl.pallas_call(
        functools.partial(_moba_kernel, seq=seq, phases=phases),
        grid_spec=pltpu.PrefetchScalarGridSpec(
            num_scalar_prefetch=2,
            grid=(b, A_HEADS // HEAD_GROUP),
            in_specs=[
                pl.BlockSpec((None, nq, gw, ATTN_TILE), tiles),
                pl.BlockSpec((None, seq, gw), per_group),
                pl.BlockSpec((None, nq, gw, MOBA_BLOCK), tiles),
                pl.BlockSpec((None, nq, gw), per_group),
                pl.BlockSpec((HEAD_GROUP, MOBA_BLOCK, 2 * MOBA_BLOCK), lambda bi, g, *_: (g, 0, 0)),
            ],
            out_specs=pl.BlockSpec((None, nq, gw, ATTN_TILE), tiles),
            scratch_shapes=[
                pltpu.VMEM((nq, HEAD_GROUP, nq, ATTN_TILE), F32),
                pltpu.VMEM((HEAD_GROUP, ITEM_SPAN * ATTN_TILE + 8, ATTN_TILE), F32),
                pltpu.VMEM((HEAD_GROUP, ITEM_SPAN * ATTN_TILE + 8, ATTN_TILE), F32),
                pltpu.VMEM((HEAD_GROUP, ITEM_SPAN, 1, ATTN_TILE), F32),
                pltpu.VMEM((HEAD_GROUP, ITEM_SPAN, 1, ATTN_TILE), F32),
                pltpu.VMEM((nq, HEAD_GROUP, 1, ATTN_TILE), F32),
                pltpu.VMEM((nq, HEAD_GROUP, A_HEAD_DIM + SUM_ROWS, ATTN_TILE), F32),
            ]),
        out_shape=jax.ShapeDtypeStruct((b, nq, A_WIDTH, ATTN_TILE), BF16),
        compiler_params=pltpu.CompilerParams(
            dimension_semantics=("arbitrary", "arbitrary"), vmem_limit_bytes=VMEM_LIMIT),
        name="moba_attention",
    )(tq, tk, qat, ka, vat, kmean, bias)


def _mla_kernel(tq_ref, tk_ref, qt_ref, k_ref, vt_ref, o_ref, s0_ref, s1_ref, t0_ref, t1_ref,
                m_ref, acc_ref, *, nq, phases):
    def stage(offset, span, diagonal):
        def score(n, buf):
            qi, kj = tq_ref[offset + n], tk_ref[offset + n]
            for h in range(HEAD_GROUP):
                sl = slice(h * MLA_HEAD_PAD, (h + 1) * MLA_HEAD_PAD)
                s_t = jnp.dot(k_ref[_tile_rows(kj, span), sl], qt_ref[qi, sl, :],
                              preferred_element_type=F32)
                if diagonal:
                    key = lax.broadcasted_iota(jnp.int32, (ATTN_TILE, ATTN_TILE), 0)
                    qry = lax.broadcasted_iota(jnp.int32, (ATTN_TILE, ATTN_TILE), 1)
                    s_t = jnp.where(key <= qry, s_t, NEG)
                _store_scores(buf, h, [s_t[t * ATTN_TILE:(t + 1) * ATTN_TILE]
                                       for t in range(span)])

        def attend(n, buf):
            qi, kj = tq_ref[offset + n], tk_ref[offset + n]
            for h in range(HEAD_GROUP):
                parts = [(ATTN_TILE, vt_ref[kj + t, h * V_HEAD:(h + 1) * V_HEAD, :], None)
                         for t in range(span)]
                _softmax_step(buf, h, parts, m_ref.at[qi, h], acc_ref.at[qi, h], diagonal)

        return score, attend

    bufs = ((s0_ref, t0_ref), (s1_ref, t1_ref))
    n_diag, n_pairs, n_single = phases
    _pipeline(n_diag, *stage(0, 1, True), bufs, PIPELINE_UNROLL)
    _pipeline(n_pairs, *stage(n_diag, 2, False), bufs, PIPELINE_UNROLL // 2)
    _pipeline(n_single, *stage(n_diag + n_pairs, 1, False), bufs, PIPELINE_UNROLL)
    _write_heads(acc_ref, o_ref, nq, V_HEAD)


def _mla(qmt, km, vbt):
    b, seq, _ = km.shape
    nq = seq // ATTN_TILE
    qw = HEAD_GROUP * MLA_HEAD_PAD
    vw = HEAD_GROUP * V_HEAD
    tq, tk, phases = _causal_items(nq, 1, pair_far=False)
    per_group = lambda bi, g, *_: (bi, 0, g)
    tiles = lambda bi, g, *_: (bi, 0, g, 0)
    return pl.pallas_call(
        functools.partial(_mla_kernel, nq=nq, phases=phases),
        grid_spec=pltpu.PrefetchScalarGridSpec(
            num_scalar_prefetch=2,
            grid=(b, B_HEADS // HEAD_GROUP),
            in_specs=[
                pl.BlockSpec((None, nq, qw, ATTN_TILE), tiles),
                pl.BlockSpec((None, seq, qw), per_group),
                pl.BlockSpec((None, nq, vw, ATTN_TILE), tiles),
            ],
            out_specs=pl.BlockSpec((None, nq, vw, ATTN_TILE), tiles),
            scratch_shapes=[
                pltpu.VMEM((HEAD_GROUP, ITEM_SPAN * ATTN_TILE + 8, ATTN_TILE), F32),
                pltpu.VMEM((HEAD_GROUP, ITEM_SPAN * ATTN_TILE + 8, ATTN_TILE), F32),
                pltpu.VMEM((HEAD_GROUP, ITEM_SPAN, 1, ATTN_TILE), F32),
                pltpu.VMEM((HEAD_GROUP, ITEM_SPAN, 1, ATTN_TILE), F32),
                pltpu.VMEM((nq, HEAD_GROUP, 1, ATTN_TILE), F32),
                pltpu.VMEM((nq, HEAD_GROUP, V_HEAD + SUM_ROWS, ATTN_TILE), F32),
            ]),
        out_shape=jax.ShapeDtypeStruct((b, nq, B_WIDTH, ATTN_TILE), BF16),
        compiler_params=pltpu.CompilerParams(
            dimension_semantics=("arbitrary", "arbitrary"), vmem_limit_bytes=VMEM_LIMIT),
        name="mla_attention",
    )(tq, tk, qmt, km, vbt)


def _out_kernel(x_ref, oa_ref, ob_ref, gattn_ref, wg_ref, wpa_ref, wpb_ref, wout_ref,
                gmlp_ref, wup_ref, wdn_ref, gfin_ref, o_ref):
    x = x_ref[...]
    n = _rms(x, gattn_ref[...]).astype(BF16)
    gates = lax.dot_general(n, wg_ref[...], NT_DIMS, preferred_element_type=F32)
    tn_dims = (((0,), (0,)), ((), ()))
    tiles = range(ROW_TILE // ATTN_TILE)
    pa = jnp.concatenate([lax.dot_general(oa_ref[t], wpa_ref[...], tn_dims,
                                          preferred_element_type=F32) for t in tiles], axis=0)
    pb = jnp.concatenate([lax.dot_general(ob_ref[t], wpb_ref[...], tn_dims,
                                          preferred_element_type=F32) for t in tiles], axis=0)
    merged = (jax.nn.sigmoid(gates[:, :D_MODEL]) * pa
              + jax.nn.sigmoid(gates[:, D_MODEL:]) * pb).astype(BF16)
    h = x + jnp.dot(merged, wout_ref[...], preferred_element_type=F32)
    m = _rms(h, gmlp_ref[...]).astype(BF16)
    for c in range(D_FF // FF_CHUNK):
        cols = slice(c * FF_CHUNK, (c + 1) * FF_CHUNK)
        up = jnp.dot(m, wup_ref[:, cols], preferred_element_type=F32)
        act = jnp.square(jnp.maximum(up, 0.0)).astype(BF16)
        h = h + jnp.dot(act, wdn_ref[cols, :], preferred_element_type=F32)
    o_ref[...] = _rms(h, gfin_ref[...])


def _out_block(x2, oa, ob, g_attn, wg, wpa, wpb, wout, g_mlp, wup, wdn, g_fin):
    n_rows = x2.shape[0]
    tm = ROW_TILE
    row = lambda i: (i, 0)
    return pl.pallas_call(
        _out_kernel,
        grid=(n_rows // tm,),
        in_specs=[
            pl.BlockSpec((tm, D_MODEL), row),
            pl.BlockSpec((tm // ATTN_TILE, A_WIDTH, ATTN_TILE), lambda i: (i, 0, 0)),
            pl.BlockSpec((tm // ATTN_TILE, B_WIDTH, ATTN_TILE), lambda i: (i, 0, 0)),
            _const_spec(g_attn.shape), _const_spec(wg.shape), _const_spec(wpa.shape),
            _const_spec(wpb.shape), _const_spec(wout.shape), _const_spec(g_mlp.shape),
            _const_spec(wup.shape), _const_spec(wdn.shape), _const_spec(g_fin.shape),
        ],
        out_specs=pl.BlockSpec((tm, D_MODEL), row),
        out_shape=jax.ShapeDtypeStruct((n_rows, D_MODEL), F32),
        compiler_params=pltpu.CompilerParams(
            dimension_semantics=("arbitrary",), vmem_limit_bytes=VMEM_LIMIT),
        name="out_mlp",
    )(x2, oa, ob, g_attn, wg, wpa, wpb, wout, g_mlp, wup, wdn, g_fin)


def _rope_tables(seq):
    half = QK_ROPE // 2
    inv_freq = ROPE_THETA ** (-jnp.arange(half, dtype=F32) / half)
    ang = jnp.arange(seq).astype(F32)[:, None] * inv_freq[None, :]
    cos2 = jnp.tile(jnp.cos(ang), (1, 2))
    sin2 = jnp.tile(jnp.sin(ang), (1, 2))
    pad = jnp.zeros((seq, MLA_HEAD_PAD - QK_NOPE - QK_ROPE), F32)
    cos_t = jnp.concatenate([jnp.ones((seq, QK_NOPE), F32), cos2, pad], axis=1)
    sin_t = jnp.concatenate([jnp.zeros((seq, QK_NOPE), F32), sin2, pad], axis=1)
    return cos_t, sin_t


def _pad_heads(w_nope, w_rope):
    k, h, _ = w_nope.shape
    if w_rope is None:
        w_rope = jnp.zeros((k, h, QK_ROPE), w_nope.dtype)
    pad = jnp.zeros((k, h, MLA_HEAD_PAD - QK_NOPE - QK_ROPE), w_nope.dtype)
    return jnp.concatenate([w_nope, w_rope, pad], axis=-1).reshape(k, h * MLA_HEAD_PAD)


def kernel(x, w_in, rel_bias, mla_q_norm, w_uq, mla_kv_norm, w_uk, w_uv, w_proj_a, w_proj_b,
           w_out, norm_attn, norm_mlp, w_mlp_up, w_mlp_down, norm_final):
    b, seq, d = x.shape
    assert d == D_MODEL and seq % ROW_TILE == 0 and seq // MOBA_BLOCK <= LANES
    assert w_in.shape[0] == 1, "single-layer block"
    x2 = x.reshape(b * seq, d)

    w_t = jnp.swapaxes(w_in[0], 0, 1)
    o_k = A_WIDTH
    o_v = 2 * A_WIDTH
    o_cq = 3 * A_WIDTH
    o_ckv = o_cq + Q_LORA
    o_kr = o_ckv + KV_LORA
    o_g = o_kr + QK_ROPE
    wk = w_t[o_k:o_v].astype(BF16)
    wqv = jnp.concatenate([w_t[:o_k], w_t[o_v:o_cq]], axis=0).astype(BF16)
    w_kr = w_t[o_kr:o_g]
    half = QK_ROPE // 2
    w_kr_rot = jnp.concatenate([-w_kr[half:], w_kr[:half]], axis=0)
    row_pad = lambda a: jnp.pad(a, ((QK_NOPE, MLA_HEAD_PAD - QK_NOPE - QK_ROPE), (0, 0)))
    wc = jnp.concatenate(
        [w_t[o_cq:o_kr], row_pad(w_kr), row_pad(w_kr_rot)], axis=0).astype(BF16)
    wg = w_t[o_g:].astype(BF16)

    uq = w_uq[0].reshape(Q_LORA, B_HEADS, QK_NOPE + QK_ROPE)
    uq_nope, uq_rope = uq[..., :QK_NOPE], uq[..., QK_NOPE:]
    wuq = _pad_heads(uq_nope, uq_rope).T.astype(BF16)
    wuk = _pad_heads(w_uk[0].reshape(KV_LORA, B_HEADS, QK_NOPE), None).astype(BF16)
    wuv = w_uv[0].T.astype(BF16)

    cos_t, sin_t = _rope_tables(seq)
    q_scale = (QK_NOPE + QK_ROPE) ** -0.5 * LOG2E
    row2 = lambda a: a.reshape(1, -1)

    qat, ka, vat, kmean, qmt, km, vbt = _inproj(
        x2, row2(norm_attn[0]), wk, wqv, wc, row2(mla_q_norm[0]), row2(mla_kv_norm[0]),
        wuq, wuk, wuv, (cos_t * q_scale).T, (sin_t * q_scale).T, cos_t, sin_t, b, seq)

    as_seq = lambda a: a.reshape(b, seq, a.shape[-1])
    bias = _bias_tiles(rel_bias)
    oa = _moba(qat, as_seq(ka), vat, kmean.reshape(b, seq // MOBA_BLOCK, A_WIDTH), bias)
    ob = _mla(qmt, as_seq(km), vbt)

    out = _out_block(
        x2, oa.reshape(-1, A_WIDTH, ATTN_TILE), ob.reshape(-1, B_WIDTH, ATTN_TILE),
        row2(norm_attn[0]),
        wg, w_proj_a[0].astype(BF16), w_proj_b[0].astype(BF16), w_out[0].astype(BF16),
        row2(norm_mlp[0]), w_mlp_up[0].astype(BF16), w_mlp_down[0].astype(BF16),
        row2(norm_final))
    return out.reshape(b, seq, d)
```

```python
import functools
import math

import jax
import jax.numpy as jnp
from jax import lax
from jax.experimental import pallas as pl
from jax.experimental.pallas import tpu as pltpu

D_MODEL = 1024
A_HEADS = 8
A_HEAD_DIM = 64
A_WIDTH = A_HEADS * A_HEAD_DIM
MOBA_BLOCK = 256
MOBA_TOPK = 3
REL_BUCKETS = 32
REL_MAX_DIST = 128
B_HEADS = 8
QK_NOPE = 64
QK_ROPE = 32
V_HEAD = 64
B_WIDTH = B_HEADS * V_HEAD
Q_LORA = 384
KV_LORA = 256
ROPE_THETA = 10000.0
D_FF = 4 * D_MODEL
EPS = 1e-6
NEG = -1e30
LOG2E = math.log2(math.e)

LANES = 128
MLA_HEAD_PAD = 128
ATTN_TILE = 256
HEAD_GROUP = 4
SUM_ROWS = 16
ITEM_SPAN = 2
PIPELINE_UNROLL = 16
ROW_TILE = 512
FF_CHUNK = 1024
VMEM_LIMIT = 56 * 1024 * 1024

F32 = jnp.float32
BF16 = jnp.bfloat16
NT_DIMS = (((1,), (1,)), ((), ()))


def _rms(xf, g):
    y = xf * lax.rsqrt(jnp.mean(xf * xf, axis=-1, keepdims=True) + EPS)
    return y * g


def _const_spec(shape):
    zeros = (0,) * len(shape)
    return pl.BlockSpec(shape, lambda *_: zeros, pipeline_mode=pl.Buffered(1))


def _bias_kernel(rel_ref, o_ref):
    h = pl.program_id(0)
    shape = (MOBA_BLOCK, 2 * MOBA_BLOCK)
    r = lax.broadcasted_iota(jnp.int32, shape, 0)
    c = lax.broadcasted_iota(jnp.int32, shape, 1)
    d = c - r
    dist = jnp.maximum(d, 0)
    max_exact = REL_BUCKETS // 2
    df = jnp.maximum(dist, 1).astype(F32)
    large = max_exact + (jnp.log(df / max_exact) / math.log(REL_MAX_DIST / max_exact)
                         * (REL_BUCKETS - max_exact)).astype(jnp.int32)
    large = jnp.minimum(large, REL_BUCKETS - 1)
    bucket = jnp.where(dist < max_exact, dist, large)
    val = jnp.zeros(shape, F32)
    for b in range(REL_BUCKETS):
        val = jnp.where(bucket == b, rel_ref[b, h], val)
    o_ref[...] = jnp.where(d >= 0, (val - rel_ref[REL_BUCKETS - 1, h]) * LOG2E, NEG)


def _bias_tiles(rel_bias):
    return pl.pallas_call(
        _bias_kernel,
        grid=(A_HEADS,),
        in_specs=[pl.BlockSpec(memory_space=pltpu.SMEM)],
        out_specs=pl.BlockSpec((None, MOBA_BLOCK, 2 * MOBA_BLOCK), lambda h: (h, 0, 0)),
        out_shape=jax.ShapeDtypeStruct((A_HEADS, MOBA_BLOCK, 2 * MOBA_BLOCK), F32),
        name="moba_bias_tiles",
    )(rel_bias)


def _inproj_kernel(x_ref, g_ref, wk_ref, wqv_ref, wc_ref, qn_ref, kvn_ref, wuq_ref,
                   wuk_ref, wuv_ref, cqt_ref, sqt_ref, ck_ref, sk_ref,
                   qat_ref, ka_ref, vat_ref, kmean_ref, qmt_ref, km_ref, vbt_ref):
    n = _rms(x_ref[...], g_ref[...]).astype(BF16)
    k = lax.dot_general(n, wk_ref[...], NT_DIMS, preferred_element_type=F32)
    ka_ref[...] = k.astype(BF16)
    nblk = ROW_TILE // ATTN_TILE
    kmean_ref[0] = jnp.sum(k.reshape(nblk, MOBA_BLOCK, A_WIDTH), axis=1) * (1.0 / MOBA_BLOCK)

    c = lax.dot_general(n, wc_ref[...], NT_DIMS, preferred_element_type=F32)
    cq = _rms(c[:, :Q_LORA], qn_ref[...]).astype(BF16)
    ckv = _rms(c[:, Q_LORA:Q_LORA + KV_LORA], kvn_ref[...]).astype(BF16)
    kr = c[:, Q_LORA + KV_LORA:Q_LORA + KV_LORA + LANES]
    krr = c[:, Q_LORA + KV_LORA + LANES:]
    for t in range(nblk):
        rows = slice(t * ATTN_TILE, (t + 1) * ATTN_TILE)
        qv_t = lax.dot_general(wqv_ref[...], n[rows], NT_DIMS,
                               preferred_element_type=F32)
        qat_ref[t] = (qv_t[:A_WIDTH] * (A_HEAD_DIM ** -0.5 * LOG2E)).astype(BF16)
        vat_ref[t] = qv_t[A_WIDTH:].astype(BF16)
        vbt_ref[t] = lax.dot_general(wuv_ref[...], ckv[rows], NT_DIMS,
                                     preferred_element_type=F32).astype(BF16)
        q_t = lax.dot_general(wuq_ref[...], cq[rows], NT_DIMS,
                              preferred_element_type=F32)
        cq_t, sq_t = cqt_ref[:, rows], sqt_ref[:, rows]
        half = QK_ROPE // 2
        for h in range(B_HEADS):
            q_h = q_t[h * MLA_HEAD_PAD:(h + 1) * MLA_HEAD_PAD]
            x1 = q_h[QK_NOPE:QK_NOPE + half]
            x2 = q_h[QK_NOPE + half:QK_NOPE + QK_ROPE]
            rot_h = jnp.concatenate(
                [jnp.zeros((QK_NOPE, ATTN_TILE), F32), -x2, x1,
                 jnp.zeros((MLA_HEAD_PAD - QK_NOPE - QK_ROPE, ATTN_TILE), F32)], axis=0)
            qmt_ref[t, h * MLA_HEAD_PAD:(h + 1) * MLA_HEAD_PAD, :] = (
                q_h * cq_t + rot_h * sq_t).astype(BF16)
    kn = jnp.dot(ckv, wuk_ref[...], preferred_element_type=F32)
    k_rope = kr * ck_ref[...] + krr * sk_ref[...]
    for h in range(B_HEADS):
        sl = slice(h * MLA_HEAD_PAD, (h + 1) * MLA_HEAD_PAD)
        km_ref[:, sl] = (kn[:, sl] + k_rope).astype(BF16)


def _inproj(x2, g_attn, wk, wqv, wc, qn, kvn, wuq, wuk, wuv, cq_tt, sq_tt, ck_t, sk_t,
            batch, seq):
    n_rows = x2.shape[0]
    tm = ROW_TILE
    steps = n_rows // tm
    per_seq = seq // tm
    nblk = tm // ATTN_TILE
    row = lambda i: (i, 0)
    tab = lambda i: (i % per_seq, 0)
    tab_t = lambda i: (0, i % per_seq)
    tile_map = lambda i: (i // per_seq, i % per_seq, 0, 0)
    bf = lambda w: jax.ShapeDtypeStruct((n_rows, w), BF16)
    tiles = lambda w: jax.ShapeDtypeStruct((batch, seq // ATTN_TILE, w, ATTN_TILE), BF16)
    tile_spec = lambda w: pl.BlockSpec((None, nblk, w, ATTN_TILE), tile_map)
    return pl.pallas_call(
        _inproj_kernel,
        grid=(steps,),
        in_specs=[
            pl.BlockSpec((tm, D_MODEL), row),
            _const_spec(g_attn.shape), _const_spec(wk.shape), _const_spec(wqv.shape),
            _const_spec(wc.shape), _const_spec(qn.shape), _const_spec(kvn.shape),
            _const_spec(wuq.shape), _const_spec(wuk.shape),
            _const_spec(wuv.shape),
            pl.BlockSpec((LANES, tm), tab_t), pl.BlockSpec((LANES, tm), tab_t),
            pl.BlockSpec((tm, LANES), tab), pl.BlockSpec((tm, LANES), tab),
        ],
        out_specs=[
            tile_spec(A_WIDTH), pl.BlockSpec((tm, A_WIDTH), row), tile_spec(A_WIDTH),
            pl.BlockSpec((1, nblk, A_WIDTH), lambda i: (i, 0, 0)),
            tile_spec(B_HEADS * MLA_HEAD_PAD),
            pl.BlockSpec((tm, B_HEADS * MLA_HEAD_PAD), row),
            tile_spec(B_WIDTH),
        ],
        out_shape=[
            tiles(A_WIDTH), bf(A_WIDTH), tiles(A_WIDTH),
            jax.ShapeDtypeStruct((steps, nblk, A_WIDTH), F32),
            tiles(B_HEADS * MLA_HEAD_PAD), bf(B_HEADS * MLA_HEAD_PAD), tiles(B_WIDTH),
        ],
        compiler_params=pltpu.CompilerParams(
            dimension_semantics=("arbitrary",), vmem_limit_bytes=VMEM_LIMIT),
        name="in_proj",
    )(x2, g_attn, wk, wqv, wc, qn, kvn, wuq, wuk, wuv, cq_tt, sq_tt, ck_t, sk_t)


def _store_scores(buf, h, parts):
    s_ref, max_ref = buf
    row = 0
    for idx, s_t in enumerate(parts):
        s_ref[h, row:row + s_t.shape[0]] = s_t
        max_ref[h, idx] = jnp.max(s_t, axis=0, keepdims=True)
        row += s_t.shape[0]


def _softmax_step(buf, h, parts, m_ref, acc_ref, first):
    s_ref, max_ref = buf
    maxes = [max_ref[h, idx] for idx in range(len(parts))]
    m_new = None if first else m_ref[...]
    for s_max, (_, _, col_keep) in zip(maxes, parts):
        if col_keep is not None:
            s_max = jnp.where(col_keep > 0, s_max, NEG)
        m_new = s_max if m_new is None else jnp.maximum(m_new, s_max)
    pv = None
    row = 0
    for s_max, (rows, v_t, col_keep) in zip(maxes, parts):
        ones = (lax.broadcasted_iota(jnp.int32, (SUM_ROWS, rows), 0) == 0).astype(BF16)
        v_aug = jnp.concatenate([v_t, ones], axis=0)
        m_exp = m_new if col_keep is None else jnp.maximum(m_new, s_max)
        p_t = jnp.exp2(s_ref[h, row:row + rows] - m_exp).astype(BF16)
        part = jnp.dot(v_aug, p_t, preferred_element_type=F32)
        if col_keep is not None:
            part = part * col_keep
        pv = part if pv is None else pv + part
        row += rows
    if first:
        acc_ref[...] = pv
    else:
        acc_ref[...] = jnp.exp2(m_ref[...] - m_new) * acc_ref[...] + pv
    m_ref[...] = m_new


def _pipeline(n_items, score, attend, bufs, unroll):
    if n_items == 0:
        return
    score(0, bufs[0])
    n_loops = (n_items - 1) // unroll

    def body(p, carry):
        for t in range(unroll):
            n = unroll * p + t
            score(n + 1, bufs[(t + 1) % 2])
            attend(n, bufs[t % 2])
        return carry

    lax.fori_loop(0, n_loops, body, 0)
    done = unroll * n_loops
    for n in range(done, n_items):
        if n + 1 < n_items:
            score(n + 1, bufs[(n + 1 - done) % 2])
        attend(n, bufs[(n - done) % 2])


def _tile_rows(j, span=1):
    return pl.ds(pl.multiple_of(j * ATTN_TILE, ATTN_TILE), span * ATTN_TILE)


def _write_heads(acc_ref, o_ref, nq, head_dim):
    def one(qi, carry):
        for h in range(HEAD_GROUP):
            acc = acc_ref[qi, h]
            o_ref[qi, h * head_dim:(h + 1) * head_dim, :] = (
                acc[:head_dim] / acc[head_dim:head_dim + 1]).astype(o_ref.dtype)
        return carry

    lax.fori_loop(0, nq, one, 0)


def _causal_items(nq, n_near, pair_far):
    phases = [[(i, i - d) for i in range(d, nq)] for d in range(n_near)]
    pairs, singles = [], []
    for i in range(n_near, nq):
        n_far = i - n_near + 1
        n_paired = n_far - n_far % 2 if pair_far else 0
        pairs += [(i, j) for j in range(0, n_paired, 2)]
        singles += [(i, j) for j in range(n_paired, n_far)]
    phases += [pairs, singles]
    flat = [it for ph in phases for it in ph]
    tq = jnp.asarray([it[0] for it in flat], jnp.int32)
    tk = jnp.asarray([it[1] for it in flat], jnp.int32)
    return tq, tk, [len(ph) for ph in phases]


def _moba_kernel(tq_ref, tk_ref, qt_ref, k_ref, vt_ref, kmean_ref, bias_ref, o_ref,
                 keep_ref, s0_ref, s1_ref, t0_ref, t1_ref, m_ref, acc_ref, *, seq, phases):
    nblk = seq // MOBA_BLOCK

    def head_query(qi, h):
        pr, sub = divmod(h, 2)
        q_h = qt_ref[qi, pl.ds(pr * LANES + sub * A_HEAD_DIM, A_HEAD_DIM), :]
        zeros = jnp.zeros_like(q_h)
        return jnp.concatenate([q_h, zeros] if sub == 0 else [zeros, q_h], axis=0)

    def select_blocks(i):
        row = lax.broadcasted_iota(jnp.int32, (nblk, ATTN_TILE), 0)
        rowf = row.astype(F32)
        past = row < i
        for h in range(HEAD_GROUP):
            kmean = kmean_ref[:, (h // 2) * LANES:(h // 2 + 1) * LANES].astype(BF16)
            gate = jnp.dot(kmean, head_query(i, h), preferred_element_type=F32)
            gate = jnp.where(past, gate, -jnp.inf)
            picked = jnp.zeros((nblk, ATTN_TILE), F32)
            for _ in range(MOBA_TOPK):
                top = jnp.max(gate, axis=0, keepdims=True)
                first = jnp.min(jnp.where(gate == top, rowf, float(nblk)), axis=0, keepdims=True)
                pick = rowf == first
                picked = jnp.where(pick, 1.0, picked)
                gate = jnp.where(pick, -jnp.inf, gate)
            keep_ref[i, h] = jnp.where(past, picked, 0.0)

    def stage(offset, span, bias_cols, first):
        def score(n, buf):
            qi, kj = tq_ref[offset + n], tk_ref[offset + n]
            for h in range(HEAD_GROUP):
                k_t = k_ref[_tile_rows(kj, span), (h // 2) * LANES:(h // 2 + 1) * LANES]
                s_t = jnp.dot(k_t, head_query(qi, h), preferred_element_type=F32)
                if bias_cols is not None:
                    s_t = s_t + bias_ref[h, :, bias_cols]
                _store_scores(buf, h, [s_t[t * MOBA_BLOCK:(t + 1) * MOBA_BLOCK]
                                       for t in range(span)])

        def attend(n, buf):
            qi, kj = tq_ref[offset + n], tk_ref[offset + n]
            for h in range(HEAD_GROUP):
                parts = [(MOBA_BLOCK, vt_ref[kj + t, h * A_HEAD_DIM:(h + 1) * A_HEAD_DIM, :],
                          None if first else keep_ref[qi, h, pl.ds(kj + t, 1), :])
                         for t in range(span)]
                _softmax_step(buf, h, parts, m_ref.at[qi, h], acc_ref.at[qi, h], first)
            if first:
                select_blocks(qi)

        return score, attend

    bufs = ((s0_ref, t0_ref), (s1_ref, t1_ref))
    n_own, n_prev, n_pairs, n_single = phases
    _pipeline(n_own, *stage(0, 1, slice(0, MOBA_BLOCK), True), bufs, PIPELINE_UNROLL)
    offset = n_own
    _pipeline(n_prev, *stage(offset, 1, slice(MOBA_BLOCK, 2 * MOBA_BLOCK), False), bufs,
              PIPELINE_UNROLL)
    offset += n_prev
    _pipeline(n_pairs, *stage(offset, 2, None, False), bufs, PIPELINE_UNROLL // 2)
    offset += n_pairs
    _pipeline(n_single, *stage(offset, 1, None, False), bufs, PIPELINE_UNROLL)
    _write_heads(acc_ref, o_ref, nblk, A_HEAD_DIM)


def _moba(qat, ka, vat, kmean, bias):
    b, seq, _ = ka.shape
    nq = seq // ATTN_TILE
    gw = HEAD_GROUP * A_HEAD_DIM
    tq, tk, phases = _causal_items(nq, 2, pair_far=True)
    per_group = lambda bi, g, *_: (bi, 0, g)
    tiles = lambda bi, g, *_: (bi, 0, g, 0)
    return pl.pallas_call(
        functools.partial(_moba_kernel, seq=seq, phases=phases),
        grid_spec=pltpu.PrefetchScalarGridSpec(
            num_scalar_prefetch=2,
            grid=(b, A_HEADS // HEAD_GROUP),
            in_specs=[
                pl.BlockSpec((None, nq, gw, ATTN_TILE), tiles),
                pl.BlockSpec((None, seq, gw), per_group),
                pl.BlockSpec((None, nq, gw, MOBA_BLOCK), tiles),
                pl.BlockSpec((None, nq, gw), per_group),
                pl.BlockSpec((HEAD_GROUP, MOBA_BLOCK, 2 * MOBA_BLOCK), lambda bi, g, *_: (g, 0, 0)),
            ],
            out_specs=pl.BlockSpec((None, nq, gw, ATTN_TILE), tiles),
            scratch_shapes=[
                pltpu.VMEM((nq, HEAD_GROUP, nq, ATTN_TILE), F32),
                pltpu.VMEM((HEAD_GROUP, ITEM_SPAN * ATTN_TILE, ATTN_TILE), F32),
                pltpu.VMEM((HEAD_GROUP, ITEM_SPAN * ATTN_TILE, ATTN_TILE), F32),
                pltpu.VMEM((HEAD_GROUP, ITEM_SPAN, 1, ATTN_TILE), F32),
                pltpu.VMEM((HEAD_GROUP, ITEM_SPAN, 1, ATTN_TILE), F32),
                pltpu.VMEM((nq, HEAD_GROUP, 1, ATTN_TILE), F32),
                pltpu.VMEM((nq, HEAD_GROUP, A_HEAD_DIM + SUM_ROWS, ATTN_TILE), F32),
            ]),
        out_shape=jax.ShapeDtypeStruct((b, nq, A_WIDTH, ATTN_TILE), BF16),
        compiler_params=pltpu.CompilerParams(
            dimension_semantics=("arbitrary", "arbitrary"), vmem_limit_bytes=VMEM_LIMIT),
        name="moba_attention",
    )(tq, tk, qat, ka, vat, kmean, bias)


def _mla_kernel(tq_ref, tk_ref, qt_ref, k_ref, vt_ref, o_ref, s0_ref, s1_ref, t0_ref, t1_ref,
                m_ref, acc_ref, *, nq, phases):
    def stage(offset, span, diagonal):
        def score(n, buf):
            qi, kj = tq_ref[offset + n], tk_ref[offset + n]
            for h in range(HEAD_GROUP):
                sl = slice(h * MLA_HEAD_PAD, (h + 1) * MLA_HEAD_PAD)
                s_t = jnp.dot(k_ref[_tile_rows(kj, span), sl], qt_ref[qi, sl, :],
                              preferred_element_type=F32)
                if diagonal:
                    key = lax.broadcasted_iota(jnp.int32, (ATTN_TILE, ATTN_TILE), 0)
                    qry = lax.broadcasted_iota(jnp.int32, (ATTN_TILE, ATTN_TILE), 1)
                    s_t = jnp.where(key <= qry, s_t, NEG)
                _store_scores(buf, h, [s_t[t * ATTN_TILE:(t + 1) * ATTN_TILE]
                                       for t in range(span)])

        def attend(n, buf):
            qi, kj = tq_ref[offset + n], tk_ref[offset + n]
            for h in range(HEAD_GROUP):
                parts = [(ATTN_TILE, vt_ref[kj + t, h * V_HEAD:(h + 1) * V_HEAD, :], None)
                         for t in range(span)]
                _softmax_step(buf, h, parts, m_ref.at[qi, h], acc_ref.at[qi, h], diagonal)

        return score, attend

    bufs = ((s0_ref, t0_ref), (s1_ref, t1_ref))
    n_diag, n_pairs, n_single = phases
    _pipeline(n_diag, *stage(0, 1, True), bufs, PIPELINE_UNROLL)
    _pipeline(n_pairs, *stage(n_diag, 2, False), bufs, PIPELINE_UNROLL // 2)
    _pipeline(n_single, *stage(n_diag + n_pairs, 1, False), bufs, PIPELINE_UNROLL)
    _write_heads(acc_ref, o_ref, nq, V_HEAD)


def _mla(qmt, km, vbt):
    b, seq, _ = km.shape
    nq = seq // ATTN_TILE
    qw = HEAD_GROUP * MLA_HEAD_PAD
    vw = HEAD_GROUP * V_HEAD
    tq, tk, phases = _causal_items(nq, 1, pair_far=False)
    per_group = lambda bi, g, *_: (bi, 0, g)
    tiles = lambda bi, g, *_: (bi, 0, g, 0)
    return pl.pallas_call(
        functools.partial(_mla_kernel, nq=nq, phases=phases),
        grid_spec=pltpu.PrefetchScalarGridSpec(
            num_scalar_prefetch=2,
            grid=(b, B_HEADS // HEAD_GROUP),
            in_specs=[
                pl.BlockSpec((None, nq, qw, ATTN_TILE), tiles),
                pl.BlockSpec((None, seq, qw), per_group),
                pl.BlockSpec((None, nq, vw, ATTN_TILE), tiles),
            ],
            out_specs=pl.BlockSpec((None, nq, vw, ATTN_TILE), tiles),
            scratch_shapes=[
                pltpu.VMEM((HEAD_GROUP, ITEM_SPAN * ATTN_TILE, ATTN_TILE), F32),
                pltpu.VMEM((HEAD_GROUP, ITEM_SPAN * ATTN_TILE, ATTN_TILE), F32),
                pltpu.VMEM((HEAD_GROUP, ITEM_SPAN, 1, ATTN_TILE), F32),
                pltpu.VMEM((HEAD_GROUP, ITEM_SPAN, 1, ATTN_TILE), F32),
                pltpu.VMEM((nq, HEAD_GROUP, 1, ATTN_TILE), F32),
                pltpu.VMEM((nq, HEAD_GROUP, V_HEAD + SUM_ROWS, ATTN_TILE), F32),
            ]),
        out_shape=jax.ShapeDtypeStruct((b, nq, B_WIDTH, ATTN_TILE), BF16),
        compiler_params=pltpu.CompilerParams(
            dimension_semantics=("arbitrary", "arbitrary"), vmem_limit_bytes=VMEM_LIMIT),
        name="mla_attention",
    )(tq, tk, qmt, km, vbt)


def _out_kernel(x_ref, oa_ref, ob_ref, gattn_ref, wg_ref, wpa_ref, wpb_ref, wout_ref,
                gmlp_ref, wup_ref, wdn_ref, gfin_ref, o_ref):
    x = x_ref[...]
    n = _rms(x, gattn_ref[...]).astype(BF16)
    gates = lax.dot_general(n, wg_ref[...], NT_DIMS, preferred_element_type=F32)
    tn_dims = (((0,), (0,)), ((), ()))
    tiles = range(ROW_TILE // ATTN_TILE)
    pa = jnp.concatenate([lax.dot_general(oa_ref[t], wpa_ref[...], tn_dims,
                                          preferred_element_type=F32) for t in tiles], axis=0)
    pb = jnp.concatenate([lax.dot_general(ob_ref[t], wpb_ref[...], tn_dims,
                                          preferred_element_type=F32) for t in tiles], axis=0)
    merged = (jax.nn.sigmoid(gates[:, :D_MODEL]) * pa
              + jax.nn.sigmoid(gates[:, D_MODEL:]) * pb).astype(BF16)
    h = x + jnp.dot(merged, wout_ref[...], preferred_element_type=F32)
    m = _rms(h, gmlp_ref[...]).astype(BF16)
    for c in range(D_FF // FF_CHUNK):
        cols = slice(c * FF_CHUNK, (c + 1) * FF_CHUNK)
        up = jnp.dot(m, wup_ref[:, cols], preferred_element_type=F32)
        act = jnp.square(jnp.maximum(up, 0.0)).astype(BF16)
        h = h + jnp.dot(act, wdn_ref[cols, :], preferred_element_type=F32)
    o_ref[...] = _rms(h, gfin_ref[...])


def _out_block(x2, oa, ob, g_attn, wg, wpa, wpb, wout, g_mlp, wup, wdn, g_fin):
    n_rows = x2.shape[0]
    tm = ROW_TILE
    row = lambda i: (i, 0)
    return pl.pallas_call(
        _out_kernel,
        grid=(n_rows // tm,),
        in_specs=[
            pl.BlockSpec((tm, D_MODEL), row),
            pl.BlockSpec((tm // ATTN_TILE, A_WIDTH, ATTN_TILE), lambda i: (i, 0, 0)),
            pl.BlockSpec((tm // ATTN_TILE, B_WIDTH, ATTN_TILE), lambda i: (i, 0, 0)),
            _const_spec(g_attn.shape), _const_spec(wg.shape), _const_spec(wpa.shape),
            _const_spec(wpb.shape), _const_spec(wout.shape), _const_spec(g_mlp.shape),
            _const_spec(wup.shape), _const_spec(wdn.shape), _const_spec(g_fin.shape),
        ],
        out_specs=pl.BlockSpec((tm, D_MODEL), row),
        out_shape=jax.ShapeDtypeStruct((n_rows, D_MODEL), F32),
        compiler_params=pltpu.CompilerParams(
            dimension_semantics=("arbitrary",), vmem_limit_bytes=VMEM_LIMIT),
        name="out_mlp",
    )(x2, oa, ob, g_attn, wg, wpa, wpb, wout, g_mlp, wup, wdn, g_fin)


def _rope_tables(seq):
    half = QK_ROPE // 2
    inv_freq = ROPE_THETA ** (-jnp.arange(half, dtype=F32) / half)
    ang = jnp.arange(seq).astype(F32)[:, None] * inv_freq[None, :]
    cos2 = jnp.tile(jnp.cos(ang), (1, 2))
    sin2 = jnp.tile(jnp.sin(ang), (1, 2))
    pad = jnp.zeros((seq, MLA_HEAD_PAD - QK_NOPE - QK_ROPE), F32)
    cos_t = jnp.concatenate([jnp.ones((seq, QK_NOPE), F32), cos2, pad], axis=1)
    sin_t = jnp.concatenate([jnp.zeros((seq, QK_NOPE), F32), sin2, pad], axis=1)
    return cos_t, sin_t


def _pad_heads(w_nope, w_rope):
    k, h, _ = w_nope.shape
    if w_rope is None:
        w_rope = jnp.zeros((k, h, QK_ROPE), w_nope.dtype)
    pad = jnp.zeros((k, h, MLA_HEAD_PAD - QK_NOPE - QK_ROPE), w_nope.dtype)
    return jnp.concatenate([w_nope, w_rope, pad], axis=-1).reshape(k, h * MLA_HEAD_PAD)


def kernel(x, w_in, rel_bias, mla_q_norm, w_uq, mla_kv_norm, w_uk, w_uv, w_proj_a, w_proj_b,
           w_out, norm_attn, norm_mlp, w_mlp_up, w_mlp_down, norm_final):
    b, seq, d = x.shape
    assert d == D_MODEL and seq % ROW_TILE == 0 and seq // MOBA_BLOCK <= LANES
    assert w_in.shape[0] == 1, "single-layer block"
    x2 = x.reshape(b * seq, d)

    w_t = jnp.swapaxes(w_in[0], 0, 1)
    o_k = A_WIDTH
    o_v = 2 * A_WIDTH
    o_cq = 3 * A_WIDTH
    o_ckv = o_cq + Q_LORA
    o_kr = o_ckv + KV_LORA
    o_g = o_kr + QK_ROPE
    wk = w_t[o_k:o_v].astype(BF16)
    wqv = jnp.concatenate([w_t[:o_k], w_t[o_v:o_cq]], axis=0).astype(BF16)
    w_kr = w_t[o_kr:o_g]
    half = QK_ROPE // 2
    w_kr_rot = jnp.concatenate([-w_kr[half:], w_kr[:half]], axis=0)
    row_pad = lambda a: jnp.pad(a, ((QK_NOPE, MLA_HEAD_PAD - QK_NOPE - QK_ROPE), (0, 0)))
    wc = jnp.concatenate(
        [w_t[o_cq:o_kr], row_pad(w_kr), row_pad(w_kr_rot)], axis=0).astype(BF16)
    wg = w_t[o_g:].astype(BF16)

    uq = w_uq[0].reshape(Q_LORA, B_HEADS, QK_NOPE + QK_ROPE)
    uq_nope, uq_rope = uq[..., :QK_NOPE], uq[..., QK_NOPE:]
    wuq = _pad_heads(uq_nope, uq_rope).T.astype(BF16)
    wuk = _pad_heads(w_uk[0].reshape(KV_LORA, B_HEADS, QK_NOPE), None).astype(BF16)
    wuv = w_uv[0].T.astype(BF16)

    cos_t, sin_t = _rope_tables(seq)
    q_scale = (QK_NOPE + QK_ROPE) ** -0.5 * LOG2E
    row2 = lambda a: a.reshape(1, -1)

    qat, ka, vat, kmean, qmt, km, vbt = _inproj(
        x2, row2(norm_attn[0]), wk, wqv, wc, row2(mla_q_norm[0]), row2(mla_kv_norm[0]),
        wuq, wuk, wuv, (cos_t * q_scale).T, (sin_t * q_scale).T, cos_t, sin_t, b, seq)

    as_seq = lambda a: a.reshape(b, seq, a.shape[-1])
    bias = _bias_tiles(rel_bias)
    oa = _moba(qat, as_seq(ka), vat, kmean.reshape(b, seq // MOBA_BLOCK, A_WIDTH), bias)
    ob = _mla(qmt, as_seq(km), vbt)

    out = _out_block(
        x2, oa.reshape(-1, A_WIDTH, ATTN_TILE), ob.reshape(-1, B_WIDTH, ATTN_TILE),
        row2(norm_attn[0]),
        wg, w_proj_a[0].astype(BF16), w_proj_b[0].astype(BF16), w_out[0].astype(BF16),
        row2(norm_mlp[0]), w_mlp_up[0].astype(BF16), w_mlp_down[0].astype(BF16),
        row2(norm_final))
    return out.reshape(b, seq, d)
```

```python
import functools
import math

import jax
import jax.numpy as jnp
from jax import lax
from jax.experimental import pallas as pl
from jax.experimental.pallas import tpu as pltpu

D_MODEL = 1024
A_HEADS = 8
A_HEAD_DIM = 64
A_WIDTH = A_HEADS * A_HEAD_DIM
MOBA_BLOCK = 256
MOBA_TOPK = 3
REL_BUCKETS = 32
REL_MAX_DIST = 128
B_HEADS = 8
QK_NOPE = 64
QK_ROPE = 32
V_HEAD = 64
B_WIDTH = B_HEADS * V_HEAD
Q_LORA = 384
KV_LORA = 256
ROPE_THETA = 10000.0
D_FF = 4 * D_MODEL
EPS = 1e-6
NEG = -1e30
LOG2E = math.log2(math.e)

LANES = 128
MLA_HEAD_PAD = 128
ATTN_TILE = 256
HEAD_GROUP = 4
SUM_ROWS = 16
ITEM_SPAN = 2
PIPELINE_UNROLL = 16
ROW_TILE = 512
FF_CHUNK = 1024
VMEM_LIMIT = 56 * 1024 * 1024

F32 = jnp.float32
BF16 = jnp.bfloat16
NT_DIMS = (((1,), (1,)), ((), ()))


def _rms(xf, g):
    y = xf * lax.rsqrt(jnp.mean(xf * xf, axis=-1, keepdims=True) + EPS)
    return y * g


def _const_spec(shape):
    zeros = (0,) * len(shape)
    return pl.BlockSpec(shape, lambda *_: zeros, pipeline_mode=pl.Buffered(1))


def _bias_kernel(rel_ref, o_ref):
    h = pl.program_id(0)
    shape = (MOBA_BLOCK, 2 * MOBA_BLOCK)
    r = lax.broadcasted_iota(jnp.int32, shape, 0)
    c = lax.broadcasted_iota(jnp.int32, shape, 1)
    d = c - r
    dist = jnp.maximum(d, 0)
    max_exact = REL_BUCKETS // 2
    df = jnp.maximum(dist, 1).astype(F32)
    large = max_exact + (jnp.log(df / max_exact) / math.log(REL_MAX_DIST / max_exact)
                         * (REL_BUCKETS - max_exact)).astype(jnp.int32)
    large = jnp.minimum(large, REL_BUCKETS - 1)
    bucket = jnp.where(dist < max_exact, dist, large)
    val = jnp.zeros(shape, F32)
    for b in range(REL_BUCKETS):
        val = jnp.where(bucket == b, rel_ref[b, h], val)
    o_ref[...] = jnp.where(d >= 0, (val - rel_ref[REL_BUCKETS - 1, h]) * LOG2E, NEG)


def _bias_tiles(rel_bias):
    return pl.pallas_call(
        _bias_kernel,
        grid=(A_HEADS,),
        in_specs=[pl.BlockSpec(memory_space=pltpu.SMEM)],
        out_specs=pl.BlockSpec((None, MOBA_BLOCK, 2 * MOBA_BLOCK), lambda h: (h, 0, 0)),
        out_shape=jax.ShapeDtypeStruct((A_HEADS, MOBA_BLOCK, 2 * MOBA_BLOCK), F32),
        name="moba_bias_tiles",
    )(rel_bias)


def _inproj_kernel(x_ref, g_ref, wk_ref, wqv_ref, wc_ref, qn_ref, kvn_ref, wuq_ref,
                   wuk_ref, wuv_ref, cqt_ref, sqt_ref, ck_ref, sk_ref,
                   qat_ref, ka_ref, vat_ref, kmean_ref, qmt_ref, km_ref, vbt_ref):
    n = _rms(x_ref[...], g_ref[...]).astype(BF16)
    k = lax.dot_general(n, wk_ref[...], NT_DIMS, preferred_element_type=F32)
    ka_ref[...] = k.astype(BF16)
    nblk = ROW_TILE // ATTN_TILE
    kmean_ref[0] = jnp.sum(k.reshape(nblk, MOBA_BLOCK, A_WIDTH), axis=1) * (1.0 / MOBA_BLOCK)

    c = lax.dot_general(n, wc_ref[...], NT_DIMS, preferred_element_type=F32)
    cq = _rms(c[:, :Q_LORA], qn_ref[...]).astype(BF16)
    ckv = _rms(c[:, Q_LORA:Q_LORA + KV_LORA], kvn_ref[...]).astype(BF16)
    kr = c[:, Q_LORA + KV_LORA:Q_LORA + KV_LORA + LANES]
    krr = c[:, Q_LORA + KV_LORA + LANES:]
    for t in range(nblk):
        rows = slice(t * ATTN_TILE, (t + 1) * ATTN_TILE)
        qv_t = lax.dot_general(wqv_ref[...], n[rows], NT_DIMS,
                               preferred_element_type=F32)
        qat_ref[t] = (qv_t[:A_WIDTH] * (A_HEAD_DIM ** -0.5 * LOG2E)).astype(BF16)
        vat_ref[t] = qv_t[A_WIDTH:].astype(BF16)
        vbt_ref[t] = lax.dot_general(wuv_ref[...], ckv[rows], NT_DIMS,
                                     preferred_element_type=F32).astype(BF16)
        q_t = lax.dot_general(wuq_ref[...], cq[rows], NT_DIMS,
                              preferred_element_type=F32)
        cq_t, sq_t = cqt_ref[:, rows], sqt_ref[:, rows]
        half = QK_ROPE // 2
        for h in range(B_HEADS):
            q_h = q_t[h * MLA_HEAD_PAD:(h + 1) * MLA_HEAD_PAD]
            x1 = q_h[QK_NOPE:QK_NOPE + half]
            x2 = q_h[QK_NOPE + half:QK_NOPE + QK_ROPE]
            rot_h = jnp.concatenate(
                [jnp.zeros((QK_NOPE, ATTN_TILE), F32), -x2, x1,
                 jnp.zeros((MLA_HEAD_PAD - QK_NOPE - QK_ROPE, ATTN_TILE), F32)], axis=0)
            qmt_ref[t, h * MLA_HEAD_PAD:(h + 1) * MLA_HEAD_PAD, :] = (
                q_h * cq_t + rot_h * sq_t).astype(BF16)
    kn = jnp.dot(ckv, wuk_ref[...], preferred_element_type=F32)
    k_rope = kr * ck_ref[...] + krr * sk_ref[...]
    for h in range(B_HEADS):
        sl = slice(h * MLA_HEAD_PAD, (h + 1) * MLA_HEAD_PAD)
        km_ref[:, sl] = (kn[:, sl] + k_rope).astype(BF16)


def _inproj(x2, g_attn, wk, wqv, wc, qn, kvn, wuq, wuk, wuv, cq_tt, sq_tt, ck_t, sk_t,
            batch, seq):
    n_rows = x2.shape[0]
    tm = ROW_TILE
    steps = n_rows // tm
    per_seq = seq // tm
    nblk = tm // ATTN_TILE
    row = lambda i: (i, 0)
    tab = lambda i: (i % per_seq, 0)
    tab_t = lambda i: (0, i % per_seq)
    tile_map = lambda i: (i // per_seq, i % per_seq, 0, 0)
    bf = lambda w: jax.ShapeDtypeStruct((n_rows, w), BF16)
    tiles = lambda w: jax.ShapeDtypeStruct((batch, seq // ATTN_TILE, w, ATTN_TILE), BF16)
    tile_spec = lambda w: pl.BlockSpec((None, nblk, w, ATTN_TILE), tile_map)
    return pl.pallas_call(
        _inproj_kernel,
        grid=(steps,),
        in_specs=[
            pl.BlockSpec((tm, D_MODEL), row),
            _const_spec(g_attn.shape), _const_spec(wk.shape), _const_spec(wqv.shape),
            _const_spec(wc.shape), _const_spec(qn.shape), _const_spec(kvn.shape),
            _const_spec(wuq.shape), _const_spec(wuk.shape),
            _const_spec(wuv.shape),
            pl.BlockSpec((LANES, tm), tab_t), pl.BlockSpec((LANES, tm), tab_t),
            pl.BlockSpec((tm, LANES), tab), pl.BlockSpec((tm, LANES), tab),
        ],
        out_specs=[
            tile_spec(A_WIDTH), pl.BlockSpec((tm, A_WIDTH), row), tile_spec(A_WIDTH),
            pl.BlockSpec((1, nblk, A_WIDTH), lambda i: (i, 0, 0)),
            tile_spec(B_HEADS * MLA_HEAD_PAD),
            pl.BlockSpec((tm, B_HEADS * MLA_HEAD_PAD), row),
            tile_spec(B_WIDTH),
        ],
        out_shape=[
            tiles(A_WIDTH), bf(A_WIDTH), tiles(A_WIDTH),
            jax.ShapeDtypeStruct((steps, nblk, A_WIDTH), F32),
            tiles(B_HEADS * MLA_HEAD_PAD), bf(B_HEADS * MLA_HEAD_PAD), tiles(B_WIDTH),
        ],
        compiler_params=pltpu.CompilerParams(
            dimension_semantics=("arbitrary",), vmem_limit_bytes=VMEM_LIMIT),
        name="in_proj",
    )(x2, g_attn, wk, wqv, wc, qn, kvn, wuq, wuk, wuv, cq_tt, sq_tt, ck_t, sk_t)


def _store_scores(buf, h, parts):
    s_ref, max_ref = buf
    row = 0
    for idx, s_t in enumerate(parts):
        s_ref[h, row:row + s_t.shape[0]] = s_t
        max_ref[h, idx] = jnp.max(s_t, axis=0, keepdims=True)
        row += s_t.shape[0]


def _softmax_step(buf, h, parts, m_ref, acc_ref, first):
    s_ref, max_ref = buf
    maxes = [max_ref[h, idx] for idx in range(len(parts))]
    m_new = None if first else m_ref[...]
    for s_max, (_, _, col_keep) in zip(maxes, parts):
        if col_keep is not None:
            s_max = jnp.where(col_keep > 0, s_max, NEG)
        m_new = s_max if m_new is None else jnp.maximum(m_new, s_max)
    pv = None
    row = 0
    for s_max, (rows, v_t, col_keep) in zip(maxes, parts):
        ones = (lax.broadcasted_iota(jnp.int32, (SUM_ROWS, rows), 0) == 0).astype(BF16)
        v_aug = jnp.concatenate([v_t, ones], axis=0)
        m_exp = m_new if col_keep is None else jnp.maximum(m_new, s_max)
        p_t = jnp.exp2(s_ref[h, row:row + rows] - m_exp).astype(BF16)
        part = jnp.dot(v_aug, p_t, preferred_element_type=F32)
        if col_keep is not None:
            part = part * col_keep
        pv = part if pv is None else pv + part
        row += rows
    if first:
        acc_ref[...] = pv
    else:
        acc_ref[...] = jnp.exp2(m_ref[...] - m_new) * acc_ref[...] + pv
    m_ref[...] = m_new


def _pipeline(n_items, score, attend, bufs, unroll):
    if n_items == 0:
        return
    score(0, bufs[0])
    n_loops = (n_items - 1) // unroll

    def body(p, carry):
        for t in range(unroll):
            n = unroll * p + t
            score(n + 1, bufs[(t + 1) % 2])
            attend(n, bufs[t % 2])
        return carry

    lax.fori_loop(0, n_loops, body, 0)
    done = unroll * n_loops
    for n in range(done, n_items):
        if n + 1 < n_items:
            score(n + 1, bufs[(n + 1 - done) % 2])
        attend(n, bufs[(n - done) % 2])


def _tile_rows(j, span=1):
    return pl.ds(pl.multiple_of(j * ATTN_TILE, ATTN_TILE), span * ATTN_TILE)


def _write_heads(acc_ref, o_ref, nq, head_dim):
    def one(qi, carry):
        for h in range(HEAD_GROUP):
            acc = acc_ref[qi, h]
            o_ref[qi, h * head_dim:(h + 1) * head_dim, :] = (
                acc[:head_dim] / acc[head_dim:head_dim + 1]).astype(o_ref.dtype)
        return carry

    lax.fori_loop(0, nq, one, 0)


def _causal_items(nq, n_near, pair_far):
    phases = [[(i, i - d) for i in range(d, nq)] for d in range(n_near)]
    pairs, singles = [], []
    for i in range(n_near, nq):
        n_far = i - n_near + 1
        n_paired = n_far - n_far % 2 if pair_far else 0
        pairs += [(i, j) for j in range(0, n_paired, 2)]
        singles += [(i, j) for j in range(n_paired, n_far)]
    phases += [pairs, singles]
    flat = [it for ph in phases for it in ph]
    tq = jnp.asarray([it[0] for it in flat], jnp.int32)
    tk = jnp.asarray([it[1] for it in flat], jnp.int32)
    return tq, tk, [len(ph) for ph in phases]


def _moba_kernel(tq_ref, tk_ref, qt_ref, k_ref, vt_ref, kmean_ref, bias_ref, o_ref,
                 keep_ref, s0_ref, s1_ref, t0_ref, t1_ref, m_ref, acc_ref, *, seq, phases):
    nblk = seq // MOBA_BLOCK

    def head_query(qi, h):
        pr, sub = divmod(h, 2)
        q_h = qt_ref[qi, pl.ds(pr * LANES + sub * A_HEAD_DIM, A_HEAD_DIM), :]
        if sub == 0:
            return q_h, slice(pr * LANES, pr * LANES + A_HEAD_DIM)
        return jnp.concatenate([jnp.zeros_like(q_h), q_h], axis=0), slice(pr * LANES, (pr + 1) * LANES)

    def select_blocks(i):
        row = lax.broadcasted_iota(jnp.int32, (nblk, ATTN_TILE), 0)
        rowf = row.astype(F32)
        past = row < i
        for h in range(HEAD_GROUP):
            q_h, lanes = head_query(i, h)
            gate = jnp.dot(kmean_ref[:, lanes].astype(BF16), q_h,
                           preferred_element_type=F32)
            gate = jnp.where(past, gate, -jnp.inf)
            picked = jnp.zeros((nblk, ATTN_TILE), F32)
            for _ in range(MOBA_TOPK):
                top = jnp.max(gate, axis=0, keepdims=True)
                first = jnp.min(jnp.where(gate == top, rowf, float(nblk)), axis=0, keepdims=True)
                pick = rowf == first
                picked = jnp.where(pick, 1.0, picked)
                gate = jnp.where(pick, -jnp.inf, gate)
            keep_ref[i, h] = jnp.where(past, picked, 0.0)

    def stage(offset, span, bias_cols, first):
        def score(n, buf):
            qi, kj = tq_ref[offset + n], tk_ref[offset + n]
            for h in range(HEAD_GROUP):
                q_h, lanes = head_query(qi, h)
                s_t = jnp.dot(k_ref[_tile_rows(kj, span), lanes], q_h,
                              preferred_element_type=F32)
                if bias_cols is not None:
                    s_t = s_t + bias_ref[h, :, bias_cols]
                _store_scores(buf, h, [s_t[t * MOBA_BLOCK:(t + 1) * MOBA_BLOCK]
                                       for t in range(span)])

        def attend(n, buf):
            qi, kj = tq_ref[offset + n], tk_ref[offset + n]
            for h in range(HEAD_GROUP):
                parts = [(MOBA_BLOCK, vt_ref[kj + t, h * A_HEAD_DIM:(h + 1) * A_HEAD_DIM, :],
                          None if first else keep_ref[qi, h, pl.ds(kj + t, 1), :])
                         for t in range(span)]
                _softmax_step(buf, h, parts, m_ref.at[qi, h], acc_ref.at[qi, h], first)
            if first:
                select_blocks(qi)

        return score, attend

    bufs = ((s0_ref, t0_ref), (s1_ref, t1_ref))
    n_own, n_prev, n_pairs, n_single = phases
    _pipeline(n_own, *stage(0, 1, slice(0, MOBA_BLOCK), True), bufs, PIPELINE_UNROLL)
    offset = n_own
    _pipeline(n_prev, *stage(offset, 1, slice(MOBA_BLOCK, 2 * MOBA_BLOCK), False), bufs,
              PIPELINE_UNROLL)
    offset += n_prev
    _pipeline(n_pairs, *stage(offset, 2, None, False), bufs, PIPELINE_UNROLL // 2)
    offset += n_pairs
    _pipeline(n_single, *stage(offset, 1, None, False), bufs, PIPELINE_UNROLL)
    _write_heads(acc_ref, o_ref, nblk, A_HEAD_DIM)


def _moba(qat, ka, vat, kmean, bias):
    b, seq, _ = ka.shape
    nq = seq // ATTN_TILE
    gw = HEAD_GROUP * A_HEAD_DIM
    tq, tk, phases = _causal_items(nq, 2, pair_far=True)
    per_group = lambda bi, g, *_: (bi, 0, g)
    tiles = lambda bi, g, *_: (bi, 0, g, 0)
    return pl.pallas_call(
        functools.partial(_moba_kernel, seq=seq, phases=phases),
        grid_spec=pltpu.PrefetchScalarGridSpec(
            num_scalar_prefetch=2,
            grid=(b, A_HEADS // HEAD_GROUP),
            in_specs=[
                pl.BlockSpec((None, nq, gw, ATTN_TILE), tiles),
                pl.BlockSpec((None, seq, gw), per_group),
                pl.BlockSpec((None, nq, gw, MOBA_BLOCK), tiles),
                pl.BlockSpec((None, nq, gw), per_group),
                pl.BlockSpec((HEAD_GROUP, MOBA_BLOCK, 2 * MOBA_BLOCK), lambda bi, g, *_: (g, 0, 0)),
            ],
            out_specs=pl.BlockSpec((None, nq, gw, ATTN_TILE), tiles),
            scratch_shapes=[
                pltpu.VMEM((nq, HEAD_GROUP, nq, ATTN_TILE), F32),
                pltpu.VMEM((HEAD_GROUP, ITEM_SPAN * ATTN_TILE, ATTN_TILE), F32),
                pltpu.VMEM((HEAD_GROUP, ITEM_SPAN * ATTN_TILE, ATTN_TILE), F32),
                pltpu.VMEM((HEAD_GROUP, ITEM_SPAN, 1, ATTN_TILE), F32),
                pltpu.VMEM((HEAD_GROUP, ITEM_SPAN, 1, ATTN_TILE), F32),
                pltpu.VMEM((nq, HEAD_GROUP, 1, ATTN_TILE), F32),
                pltpu.VMEM((nq, HEAD_GROUP, A_HEAD_DIM + SUM_ROWS, ATTN_TILE), F32),
            ]),
        out_shape=jax.ShapeDtypeStruct((b, nq, A_WIDTH, ATTN_TILE), BF16),
        compiler_params=pltpu.CompilerParams(
            dimension_semantics=("arbitrary", "arbitrary"), vmem_limit_bytes=VMEM_LIMIT),
        name="moba_attention",
    )(tq, tk, qat, ka, vat, kmean, bias)


def _mla_kernel(tq_ref, tk_ref, qt_ref, k_ref, vt_ref, o_ref, s0_ref, s1_ref, t0_ref, t1_ref,
                m_ref, acc_ref, *, nq, phases):
    def stage(offset, span, diagonal):
        def score(n, buf):
            qi, kj = tq_ref[offset + n], tk_ref[offset + n]
            for h in range(HEAD_GROUP):
                sl = slice(h * MLA_HEAD_PAD, h * MLA_HEAD_PAD + QK_NOPE + QK_ROPE)
                s_t = jnp.dot(k_ref[_tile_rows(kj, span), sl], qt_ref[qi, sl, :],
                              preferred_element_type=F32)
                if diagonal:
                    key = lax.broadcasted_iota(jnp.int32, (ATTN_TILE, ATTN_TILE), 0)
                    qry = lax.broadcasted_iota(jnp.int32, (ATTN_TILE, ATTN_TILE), 1)
                    s_t = jnp.where(key <= qry, s_t, NEG)
                _store_scores(buf, h, [s_t[t * ATTN_TILE:(t + 1) * ATTN_TILE]
                                       for t in range(span)])

        def attend(n, buf):
            qi, kj = tq_ref[offset + n], tk_ref[offset + n]
            for h in range(HEAD_GROUP):
                parts = [(ATTN_TILE, vt_ref[kj + t, h * V_HEAD:(h + 1) * V_HEAD, :], None)
                         for t in range(span)]
                _softmax_step(buf, h, parts, m_ref.at[qi, h], acc_ref.at[qi, h], diagonal)

        return score, attend

    bufs = ((s0_ref, t0_ref), (s1_ref, t1_ref))
    n_diag, n_pairs, n_single = phases
    _pipeline(n_diag, *stage(0, 1, True), bufs, PIPELINE_UNROLL)
    _pipeline(n_pairs, *stage(n_diag, 2, False), bufs, PIPELINE_UNROLL // 2)
    _pipeline(n_single, *stage(n_diag + n_pairs, 1, False), bufs, PIPELINE_UNROLL)
    _write_heads(acc_ref, o_ref, nq, V_HEAD)


def _mla(qmt, km, vbt):
    b, seq, _ = km.shape
    nq = seq // ATTN_TILE
    qw = HEAD_GROUP * MLA_HEAD_PAD
    vw = HEAD_GROUP * V_HEAD
    tq, tk, phases = _causal_items(nq, 1, pair_far=False)
    per_group = lambda bi, g, *_: (bi, 0, g)
    tiles = lambda bi, g, *_: (bi, 0, g, 0)
    return pl.pallas_call(
        functools.partial(_mla_kernel, nq=nq, phases=phases),
        grid_spec=pltpu.PrefetchScalarGridSpec(
            num_scalar_prefetch=2,
            grid=(b, B_HEADS // HEAD_GROUP),
            in_specs=[
                pl.BlockSpec((None, nq, qw, ATTN_TILE), tiles),
                pl.BlockSpec((None, seq, qw), per_group),
                pl.BlockSpec((None, nq, vw, ATTN_TILE), tiles),
            ],
            out_specs=pl.BlockSpec((None, nq, vw, ATTN_TILE), tiles),
            scratch_shapes=[
                pltpu.VMEM((HEAD_GROUP, ITEM_SPAN * ATTN_TILE, ATTN_TILE), F32),
                pltpu.VMEM((HEAD_GROUP, ITEM_SPAN * ATTN_TILE, ATTN_TILE), F32),
                pltpu.VMEM((HEAD_GROUP, ITEM_SPAN, 1, ATTN_TILE), F32),
                pltpu.VMEM((HEAD_GROUP, ITEM_SPAN, 1, ATTN_TILE), F32),
                pltpu.VMEM((nq, HEAD_GROUP, 1, ATTN_TILE), F32),
                pltpu.VMEM((nq, HEAD_GROUP, V_HEAD + SUM_ROWS, ATTN_TILE), F32),
            ]),
        out_shape=jax.ShapeDtypeStruct((b, nq, B_WIDTH, ATTN_TILE), BF16),
        compiler_params=pltpu.CompilerParams(
            dimension_semantics=("arbitrary", "arbitrary"), vmem_limit_bytes=VMEM_LIMIT),
        name="mla_attention",
    )(tq, tk, qmt, km, vbt)


def _out_kernel(x_ref, oa_ref, ob_ref, gattn_ref, wg_ref, wpa_ref, wpb_ref, wout_ref,
                gmlp_ref, wup_ref, wdn_ref, gfin_ref, o_ref):
    x = x_ref[...]
    n = _rms(x, gattn_ref[...]).astype(BF16)
    gates = lax.dot_general(n, wg_ref[...], NT_DIMS, preferred_element_type=F32)
    tn_dims = (((0,), (0,)), ((), ()))
    tiles = range(ROW_TILE // ATTN_TILE)
    pa = jnp.concatenate([lax.dot_general(oa_ref[t], wpa_ref[...], tn_dims,
                                          preferred_element_type=F32) for t in tiles], axis=0)
    pb = jnp.concatenate([lax.dot_general(ob_ref[t], wpb_ref[...], tn_dims,
                                          preferred_element_type=F32) for t in tiles], axis=0)
    merged = (jax.nn.sigmoid(gates[:, :D_MODEL]) * pa
              + jax.nn.sigmoid(gates[:, D_MODEL:]) * pb).astype(BF16)
    h = x + jnp.dot(merged, wout_ref[...], preferred_element_type=F32)
    m = _rms(h, gmlp_ref[...]).astype(BF16)
    for c in range(D_FF // FF_CHUNK):
        cols = slice(c * FF_CHUNK, (c + 1) * FF_CHUNK)
        up = jnp.dot(m, wup_ref[:, cols], preferred_element_type=F32)
        act = jnp.square(jnp.maximum(up, 0.0)).astype(BF16)
        h = h + jnp.dot(act, wdn_ref[cols, :], preferred_element_type=F32)
    o_ref[...] = _rms(h, gfin_ref[...])


def _out_block(x2, oa, ob, g_attn, wg, wpa, wpb, wout, g_mlp, wup, wdn, g_fin):
    n_rows = x2.shape[0]
    tm = ROW_TILE
    row = lambda i: (i, 0)
    return pl.pallas_call(
        _out_kernel,
        grid=(n_rows // tm,),
        in_specs=[
            pl.BlockSpec((tm, D_MODEL), row),
            pl.BlockSpec((tm // ATTN_TILE, A_WIDTH, ATTN_TILE), lambda i: (i, 0, 0)),
            pl.BlockSpec((tm // ATTN_TILE, B_WIDTH, ATTN_TILE), lambda i: (i, 0, 0)),
            _const_spec(g_attn.shape), _const_spec(wg.shape), _const_spec(wpa.shape),
            _const_spec(wpb.shape), _const_spec(wout.shape), _const_spec(g_mlp.shape),
            _const_spec(wup.shape), _const_spec(wdn.shape), _const_spec(g_fin.shape),
        ],
        out_specs=pl.BlockSpec((tm, D_MODEL), row),
        out_shape=jax.ShapeDtypeStruct((n_rows, D_MODEL), F32),
        compiler_params=pltpu.CompilerParams(
            dimension_semantics=("arbitrary",), vmem_limit_bytes=VMEM_LIMIT),
        name="out_mlp",
    )(x2, oa, ob, g_attn, wg, wpa, wpb, wout, g_mlp, wup, wdn, g_fin)


def _rope_tables(seq):
    half = QK_ROPE // 2
    inv_freq = ROPE_THETA ** (-jnp.arange(half, dtype=F32) / half)
    ang = jnp.arange(seq).astype(F32)[:, None] * inv_freq[None, :]
    cos2 = jnp.tile(jnp.cos(ang), (1, 2))
    sin2 = jnp.tile(jnp.sin(ang), (1, 2))
    pad = jnp.zeros((seq, MLA_HEAD_PAD - QK_NOPE - QK_ROPE), F32)
    cos_t = jnp.concatenate([jnp.ones((seq, QK_NOPE), F32), cos2, pad], axis=1)
    sin_t = jnp.concatenate([jnp.zeros((seq, QK_NOPE), F32), sin2, pad], axis=1)
    return cos_t, sin_t


def _pad_heads(w_nope, w_rope):
    k, h, _ = w_nope.shape
    if w_rope is None:
        w_rope = jnp.zeros((k, h, QK_ROPE), w_nope.dtype)
    pad = jnp.zeros((k, h, MLA_HEAD_PAD - QK_NOPE - QK_ROPE), w_nope.dtype)
    return jnp.concatenate([w_nope, w_rope, pad], axis=-1).reshape(k, h * MLA_HEAD_PAD)


def kernel(x, w_in, rel_bias, mla_q_norm, w_uq, mla_kv_norm, w_uk, w_uv, w_proj_a, w_proj_b,
           w_out, norm_attn, norm_mlp, w_mlp_up, w_mlp_down, norm_final):
    b, seq, d = x.shape
    assert d == D_MODEL and seq % ROW_TILE == 0 and seq // MOBA_BLOCK <= LANES
    assert w_in.shape[0] == 1, "single-layer block"
    x2 = x.reshape(b * seq, d)

    w_t = jnp.swapaxes(w_in[0], 0, 1)
    o_k = A_WIDTH
    o_v = 2 * A_WIDTH
    o_cq = 3 * A_WIDTH
    o_ckv = o_cq + Q_LORA
    o_kr = o_ckv + KV_LORA
    o_g = o_kr + QK_ROPE
    wk = w_t[o_k:o_v].astype(BF16)
    wqv = jnp.concatenate([w_t[:o_k], w_t[o_v:o_cq]], axis=0).astype(BF16)
    w_kr = w_t[o_kr:o_g]
    half = QK_ROPE // 2
    w_kr_rot = jnp.concatenate([-w_kr[half:], w_kr[:half]], axis=0)
    row_pad = lambda a: jnp.pad(a, ((QK_NOPE, MLA_HEAD_PAD - QK_NOPE - QK_ROPE), (0, 0)))
    wc = jnp.concatenate(
        [w_t[o_cq:o_kr], row_pad(w_kr), row_pad(w_kr_rot)], axis=0).astype(BF16)
    wg = w_t[o_g:].astype(BF16)

    uq = w_uq[0].reshape(Q_LORA, B_HEADS, QK_NOPE + QK_ROPE)
    uq_nope, uq_rope = uq[..., :QK_NOPE], uq[..., QK_NOPE:]
    wuq = _pad_heads(uq_nope, uq_rope).T.astype(BF16)
    wuk = _pad_heads(w_uk[0].reshape(KV_LORA, B_HEADS, QK_NOPE), None).astype(BF16)
    wuv = w_uv[0].T.astype(BF16)

    cos_t, sin_t = _rope_tables(seq)
    q_scale = (QK_NOPE + QK_ROPE) ** -0.5 * LOG2E
    row2 = lambda a: a.reshape(1, -1)

    qat, ka, vat, kmean, qmt, km, vbt = _inproj(
        x2, row2(norm_attn[0]), wk, wqv, wc, row2(mla_q_norm[0]), row2(mla_kv_norm[0]),
        wuq, wuk, wuv, (cos_t * q_scale).T, (sin_t * q_scale).T, cos_t, sin_t, b, seq)

    as_seq = lambda a: a.reshape(b, seq, a.shape[-1])
    bias = _bias_tiles(rel_bias)
    oa = _moba(qat, as_seq(ka), vat, kmean.reshape(b, seq // MOBA_BLOCK, A_WIDTH), bias)
    ob = _mla(qmt, as_seq(km), vbt)

    out = _out_block(
        x2, oa.reshape(-1, A_WIDTH, ATTN_TILE), ob.reshape(-1, B_WIDTH, ATTN_TILE),
        row2(norm_attn[0]),
        wg, w_proj_a[0].astype(BF16), w_proj_b[0].astype(BF16), w_out[0].astype(BF16),
        row2(norm_mlp[0]), w_mlp_up[0].astype(BF16), w_mlp_down[0].astype(BF16),
        row2(norm_final))
    return out.reshape(b, seq, d)
```

```python
import functools
import math

import jax
import jax.numpy as jnp
from jax import lax
from jax.experimental import pallas as pl
from jax.experimental.pallas import tpu as pltpu

D_MODEL = 1024
A_HEADS = 8
A_HEAD_DIM = 64
A_WIDTH = A_HEADS * A_HEAD_DIM
MOBA_BLOCK = 256
MOBA_TOPK = 3
REL_BUCKETS = 32
REL_MAX_DIST = 128
B_HEADS = 8
QK_NOPE = 64
QK_ROPE = 32
V_HEAD = 64
B_WIDTH = B_HEADS * V_HEAD
Q_LORA = 384
KV_LORA = 256
ROPE_THETA = 10000.0
D_FF = 4 * D_MODEL
EPS = 1e-6
NEG = -1e30
LOG2E = math.log2(math.e)

LANES = 128
MLA_HEAD_PAD = 128
ATTN_TILE = 256
HEAD_GROUP = 4
SUM_ROWS = 16
ITEM_SPAN = 2
PIPELINE_UNROLL = 16
ROW_TILE = 512
FF_CHUNK = 1024
VMEM_LIMIT = 56 * 1024 * 1024

F32 = jnp.float32
BF16 = jnp.bfloat16
NT_DIMS = (((1,), (1,)), ((), ()))


def _rms(xf, g):
    y = xf * lax.rsqrt(jnp.mean(xf * xf, axis=-1, keepdims=True) + EPS)
    return y * g


def _const_spec(shape):
    zeros = (0,) * len(shape)
    return pl.BlockSpec(shape, lambda *_: zeros, pipeline_mode=pl.Buffered(1))


def _bias_kernel(rel_ref, o_ref):
    h = pl.program_id(0)
    shape = (MOBA_BLOCK, 2 * MOBA_BLOCK)
    r = lax.broadcasted_iota(jnp.int32, shape, 0)
    c = lax.broadcasted_iota(jnp.int32, shape, 1)
    d = c - r
    dist = jnp.maximum(d, 0)
    max_exact = REL_BUCKETS // 2
    df = jnp.maximum(dist, 1).astype(F32)
    large = max_exact + (jnp.log(df / max_exact) / math.log(REL_MAX_DIST / max_exact)
                         * (REL_BUCKETS - max_exact)).astype(jnp.int32)
    large = jnp.minimum(large, REL_BUCKETS - 1)
    bucket = jnp.where(dist < max_exact, dist, large)
    val = jnp.zeros(shape, F32)
    for b in range(REL_BUCKETS):
        val = jnp.where(bucket == b, rel_ref[b, h], val)
    o_ref[...] = jnp.where(d >= 0, (val - rel_ref[REL_BUCKETS - 1, h]) * LOG2E, NEG)


def _bias_tiles(rel_bias):
    return pl.pallas_call(
        _bias_kernel,
        grid=(A_HEADS,),
        in_specs=[pl.BlockSpec(memory_space=pltpu.SMEM)],
        out_specs=pl.BlockSpec((None, MOBA_BLOCK, 2 * MOBA_BLOCK), lambda h: (h, 0, 0)),
        out_shape=jax.ShapeDtypeStruct((A_HEADS, MOBA_BLOCK, 2 * MOBA_BLOCK), F32),
        name="moba_bias_tiles",
    )(rel_bias)


def _inproj_kernel(x_ref, g_ref, wk_ref, wqv_ref, wc_ref, qn_ref, kvn_ref, wuq_ref,
                   wuk_ref, wuv_ref, cqt_ref, sqt_ref, ck_ref, sk_ref,
                   qat_ref, ka_ref, vat_ref, kmean_ref, qmt_ref, km_ref, vbt_ref):
    n = _rms(x_ref[...], g_ref[...]).astype(BF16)
    k = lax.dot_general(n, wk_ref[...], NT_DIMS, preferred_element_type=F32)
    ka_ref[...] = k.astype(BF16)
    nblk = ROW_TILE // ATTN_TILE
    kmean_ref[0] = jnp.sum(k.reshape(nblk, MOBA_BLOCK, A_WIDTH), axis=1) * (1.0 / MOBA_BLOCK)

    c = lax.dot_general(n, wc_ref[...], NT_DIMS, preferred_element_type=F32)
    cq = _rms(c[:, :Q_LORA], qn_ref[...]).astype(BF16)
    ckv = _rms(c[:, Q_LORA:Q_LORA + KV_LORA], kvn_ref[...]).astype(BF16)
    kr = c[:, Q_LORA + KV_LORA:Q_LORA + KV_LORA + LANES]
    krr = c[:, Q_LORA + KV_LORA + LANES:]
    for t in range(nblk):
        rows = slice(t * ATTN_TILE, (t + 1) * ATTN_TILE)
        qv_t = lax.dot_general(wqv_ref[...], n[rows], NT_DIMS,
                               preferred_element_type=F32)
        qat_ref[t] = (qv_t[:A_WIDTH] * (A_HEAD_DIM ** -0.5 * LOG2E)).astype(BF16)
        vat_ref[t] = qv_t[A_WIDTH:].astype(BF16)
        vbt_ref[t] = lax.dot_general(wuv_ref[...], ckv[rows], NT_DIMS,
                                     preferred_element_type=F32).astype(BF16)
        q_t = lax.dot_general(wuq_ref[...], cq[rows], NT_DIMS,
                              preferred_element_type=F32)
        cq_t, sq_t = cqt_ref[:, rows], sqt_ref[:, rows]
        half = QK_ROPE // 2
        for h in range(B_HEADS):
            q_h = q_t[h * MLA_HEAD_PAD:(h + 1) * MLA_HEAD_PAD]
            x1 = q_h[QK_NOPE:QK_NOPE + half]
            x2 = q_h[QK_NOPE + half:QK_NOPE + QK_ROPE]
            rot_h = jnp.concatenate(
                [jnp.zeros((QK_NOPE, ATTN_TILE), F32), -x2, x1,
                 jnp.zeros((MLA_HEAD_PAD - QK_NOPE - QK_ROPE, ATTN_TILE), F32)], axis=0)
            qmt_ref[t, h * MLA_HEAD_PAD:(h + 1) * MLA_HEAD_PAD, :] = (
                q_h * cq_t + rot_h * sq_t).astype(BF16)
    kn = jnp.dot(ckv, wuk_ref[...], preferred_element_type=F32)
    k_rope = kr * ck_ref[...] + krr * sk_ref[...]
    for h in range(B_HEADS):
        sl = slice(h * MLA_HEAD_PAD, (h + 1) * MLA_HEAD_PAD)
        km_ref[:, sl] = (kn[:, sl] + k_rope).astype(BF16)


def _inproj(x2, g_attn, wk, wqv, wc, qn, kvn, wuq, wuk, wuv, cq_tt, sq_tt, ck_t, sk_t,
            batch, seq):
    n_rows = x2.shape[0]
    tm = ROW_TILE
    steps = n_rows // tm
    per_seq = seq // tm
    nblk = tm // ATTN_TILE
    row = lambda i: (i, 0)
    tab = lambda i: (i % per_seq, 0)
    tab_t = lambda i: (0, i % per_seq)
    tile_map = lambda i: (i // per_seq, i % per_seq, 0, 0)
    bf = lambda w: jax.ShapeDtypeStruct((n_rows, w), BF16)
    tiles = lambda w: jax.ShapeDtypeStruct((batch, seq // ATTN_TILE, w, ATTN_TILE), BF16)
    tile_spec = lambda w: pl.BlockSpec((None, nblk, w, ATTN_TILE), tile_map)
    return pl.pallas_call(
        _inproj_kernel,
        grid=(steps,),
        in_specs=[
            pl.BlockSpec((tm, D_MODEL), row),
            _const_spec(g_attn.shape), _const_spec(wk.shape), _const_spec(wqv.shape),
            _const_spec(wc.shape), _const_spec(qn.shape), _const_spec(kvn.shape),
            _const_spec(wuq.shape), _const_spec(wuk.shape),
            _const_spec(wuv.shape),
            pl.BlockSpec((LANES, tm), tab_t), pl.BlockSpec((LANES, tm), tab_t),
            pl.BlockSpec((tm, LANES), tab), pl.BlockSpec((tm, LANES), tab),
        ],
        out_specs=[
            tile_spec(A_WIDTH), pl.BlockSpec((tm, A_WIDTH), row), tile_spec(A_WIDTH),
            pl.BlockSpec((1, nblk, A_WIDTH), lambda i: (i, 0, 0)),
            tile_spec(B_HEADS * MLA_HEAD_PAD),
            pl.BlockSpec((tm, B_HEADS * MLA_HEAD_PAD), row),
            tile_spec(B_WIDTH),
        ],
        out_shape=[
            tiles(A_WIDTH), bf(A_WIDTH), tiles(A_WIDTH),
            jax.ShapeDtypeStruct((steps, nblk, A_WIDTH), F32),
            tiles(B_HEADS * MLA_HEAD_PAD), bf(B_HEADS * MLA_HEAD_PAD), tiles(B_WIDTH),
        ],
        compiler_params=pltpu.CompilerParams(
            dimension_semantics=("arbitrary",), vmem_limit_bytes=VMEM_LIMIT),
        name="in_proj",
    )(x2, g_attn, wk, wqv, wc, qn, kvn, wuq, wuk, wuv, cq_tt, sq_tt, ck_t, sk_t)


def _store_scores(buf, h, parts):
    s_ref, max_ref = buf
    row = 0
    for idx, s_t in enumerate(parts):
        s_ref[h, row:row + s_t.shape[0]] = s_t
        max_ref[h, idx] = jnp.max(s_t, axis=0, keepdims=True)
        row += s_t.shape[0]


def _softmax_step(buf, h, parts, m_ref, acc_ref, first):
    s_ref, max_ref = buf
    maxes = [max_ref[h, idx] for idx in range(len(parts))]
    m_new = None if first else m_ref[...]
    for s_max, (_, _, col_keep) in zip(maxes, parts):
        if col_keep is not None:
            s_max = jnp.where(col_keep > 0, s_max, NEG)
        m_new = s_max if m_new is None else jnp.maximum(m_new, s_max)
    pv = None
    row = 0
    for s_max, (rows, v_t, col_keep) in zip(maxes, parts):
        ones = (lax.broadcasted_iota(jnp.int32, (SUM_ROWS, rows), 0) == 0).astype(BF16)
        v_aug = jnp.concatenate([v_t, ones], axis=0)
        m_exp = m_new if col_keep is None else jnp.maximum(m_new, s_max)
        p_t = jnp.exp2(s_ref[h, row:row + rows] - m_exp).astype(BF16)
        part = jnp.dot(v_aug, p_t, preferred_element_type=F32)
        if col_keep is not None:
            part = part * col_keep
        pv = part if pv is None else pv + part
        row += rows
    if first:
        acc_ref[...] = pv
    else:
        acc_ref[...] = jnp.exp2(m_ref[...] - m_new) * acc_ref[...] + pv
    m_ref[...] = m_new


def _pipeline(n_items, score, attend, bufs, unroll):
    if n_items == 0:
        return
    score(0, bufs[0])
    n_loops = (n_items - 1) // unroll

    def body(p, carry):
        for t in range(unroll):
            n = unroll * p + t
            score(n + 1, bufs[(t + 1) % 2])
            attend(n, bufs[t % 2])
        return carry

    lax.fori_loop(0, n_loops, body, 0)
    done = unroll * n_loops
    for n in range(done, n_items):
        if n + 1 < n_items:
            score(n + 1, bufs[(n + 1 - done) % 2])
        attend(n, bufs[(n - done) % 2])


def _tile_rows(j, span=1):
    return pl.ds(pl.multiple_of(j * ATTN_TILE, ATTN_TILE), span * ATTN_TILE)


def _write_heads(acc_ref, o_ref, nq, head_dim):
    def one(qi, carry):
        for h in range(HEAD_GROUP):
            acc = acc_ref[qi, h]
            o_ref[qi, h * head_dim:(h + 1) * head_dim, :] = (
                acc[:head_dim] / acc[head_dim:head_dim + 1]).astype(o_ref.dtype)
        return carry

    lax.fori_loop(0, nq, one, 0)


def _causal_items(nq, n_near, pair_far):
    phases = [[(i, i - d) for i in range(d, nq)] for d in range(n_near)]
    pairs, singles = [], []
    for i in range(n_near, nq):
        n_far = i - n_near + 1
        n_paired = n_far - n_far % 2 if pair_far else 0
        pairs += [(i, j) for j in range(0, n_paired, 2)]
        singles += [(i, j) for j in range(n_paired, n_far)]
    phases += [pairs, singles]
    flat = [it for ph in phases for it in ph]
    tq = jnp.asarray([it[0] for it in flat], jnp.int32)
    tk = jnp.asarray([it[1] for it in flat], jnp.int32)
    return tq, tk, [len(ph) for ph in phases]


def _moba_kernel(tq_ref, tk_ref, qt_ref, k_ref, vt_ref, kmean_ref, bias_ref, o_ref,
                 keep_ref, s0_ref, s1_ref, t0_ref, t1_ref, m_ref, acc_ref, *, seq, phases):
    nblk = seq // MOBA_BLOCK

    def head_query(qi, h):
        pr, sub = divmod(h, 2)
        q_h = qt_ref[qi, pl.ds(pr * LANES + sub * A_HEAD_DIM, A_HEAD_DIM), :]
        return q_h, slice(h * A_HEAD_DIM, (h + 1) * A_HEAD_DIM)

    def select_blocks(i):
        row = lax.broadcasted_iota(jnp.int32, (nblk, ATTN_TILE), 0)
        rowf = row.astype(F32)
        past = row < i
        for h in range(HEAD_GROUP):
            q_h, lanes = head_query(i, h)
            gate = jnp.dot(kmean_ref[:, lanes].astype(BF16), q_h,
                           preferred_element_type=F32)
            gate = jnp.where(past, gate, -jnp.inf)
            picked = jnp.zeros((nblk, ATTN_TILE), F32)
            for _ in range(MOBA_TOPK):
                top = jnp.max(gate, axis=0, keepdims=True)
                first = jnp.min(jnp.where(gate == top, rowf, float(nblk)), axis=0, keepdims=True)
                pick = rowf == first
                picked = jnp.where(pick, 1.0, picked)
                gate = jnp.where(pick, -jnp.inf, gate)
            keep_ref[i, h] = jnp.where(past, picked, 0.0)

    def stage(offset, span, bias_cols, first):
        def score(n, buf):
            qi, kj = tq_ref[offset + n], tk_ref[offset + n]
            for h in range(HEAD_GROUP):
                q_h, lanes = head_query(qi, h)
                s_t = jnp.dot(k_ref[_tile_rows(kj, span), lanes], q_h,
                              preferred_element_type=F32)
                if bias_cols is not None:
                    s_t = s_t + bias_ref[h, :, bias_cols]
                _store_scores(buf, h, [s_t[t * MOBA_BLOCK:(t + 1) * MOBA_BLOCK]
                                       for t in range(span)])

        def attend(n, buf):
            qi, kj = tq_ref[offset + n], tk_ref[offset + n]
            for h in range(HEAD_GROUP):
                parts = [(MOBA_BLOCK, vt_ref[kj + t, h * A_HEAD_DIM:(h + 1) * A_HEAD_DIM, :],
                          None if first else keep_ref[qi, h, pl.ds(kj + t, 1), :])
                         for t in range(span)]
                _softmax_step(buf, h, parts, m_ref.at[qi, h], acc_ref.at[qi, h], first)
            if first:
                select_blocks(qi)

        return score, attend

    bufs = ((s0_ref, t0_ref), (s1_ref, t1_ref))
    n_own, n_prev, n_pairs, n_single = phases
    _pipeline(n_own, *stage(0, 1, slice(0, MOBA_BLOCK), True), bufs, PIPELINE_UNROLL)
    offset = n_own
    _pipeline(n_prev, *stage(offset, 1, slice(MOBA_BLOCK, 2 * MOBA_BLOCK), False), bufs,
              PIPELINE_UNROLL)
    offset += n_prev
    _pipeline(n_pairs, *stage(offset, 2, None, False), bufs, PIPELINE_UNROLL // 2)
    offset += n_pairs
    _pipeline(n_single, *stage(offset, 1, None, False), bufs, PIPELINE_UNROLL)
    _write_heads(acc_ref, o_ref, nblk, A_HEAD_DIM)


def _moba(qat, ka, vat, kmean, bias):
    b, seq, _ = ka.shape
    nq = seq // ATTN_TILE
    gw = HEAD_GROUP * A_HEAD_DIM
    tq, tk, phases = _causal_items(nq, 2, pair_far=True)
    per_group = lambda bi, g, *_: (bi, 0, g)
    tiles = lambda bi, g, *_: (bi, 0, g, 0)
    return pl.pallas_call(
        functools.partial(_moba_kernel, seq=seq, phases=phases),
        grid_spec=pltpu.PrefetchScalarGridSpec(
            num_scalar_prefetch=2,
            grid=(b, A_HEADS // HEAD_GROUP),
            in_specs=[
                pl.BlockSpec((None, nq, gw, ATTN_TILE), tiles),
                pl.BlockSpec((None, seq, gw), per_group),
                pl.BlockSpec((None, nq, gw, MOBA_BLOCK), tiles),
                pl.BlockSpec((None, nq, gw), per_group),
                pl.BlockSpec((HEAD_GROUP, MOBA_BLOCK, 2 * MOBA_BLOCK), lambda bi, g, *_: (g, 0, 0)),
            ],
            out_specs=pl.BlockSpec((None, nq, gw, ATTN_TILE), tiles),
            scratch_shapes=[
                pltpu.VMEM((nq, HEAD_GROUP, nq, ATTN_TILE), F32),
                pltpu.VMEM((HEAD_GROUP, ITEM_SPAN * ATTN_TILE, ATTN_TILE), F32),
                pltpu.VMEM((HEAD_GROUP, ITEM_SPAN * ATTN_TILE, ATTN_TILE), F32),
                pltpu.VMEM((HEAD_GROUP, ITEM_SPAN, 1, ATTN_TILE), F32),
                pltpu.VMEM((HEAD_GROUP, ITEM_SPAN, 1, ATTN_TILE), F32),
                pltpu.VMEM((nq, HEAD_GROUP, 1, ATTN_TILE), F32),
                pltpu.VMEM((nq, HEAD_GROUP, A_HEAD_DIM + SUM_ROWS, ATTN_TILE), F32),
            ]),
        out_shape=jax.ShapeDtypeStruct((b, nq, A_WIDTH, ATTN_TILE), BF16),
        compiler_params=pltpu.CompilerParams(
            dimension_semantics=("arbitrary", "arbitrary"), vmem_limit_bytes=VMEM_LIMIT),
        name="moba_attention",
    )(tq, tk, qat, ka, vat, kmean, bias)


def _mla_kernel(tq_ref, tk_ref, qt_ref, k_ref, vt_ref, o_ref, s0_ref, s1_ref, t0_ref, t1_ref,
                m_ref, acc_ref, *, nq, phases):
    def stage(offset, span, diagonal):
        def score(n, buf):
            qi, kj = tq_ref[offset + n], tk_ref[offset + n]
            for h in range(HEAD_GROUP):
                sl = slice(h * MLA_HEAD_PAD, h * MLA_HEAD_PAD + QK_NOPE + QK_ROPE)
                s_t = jnp.dot(k_ref[_tile_rows(kj, span), sl], qt_ref[qi, sl, :],
                              preferred_element_type=F32)
                if diagonal:
                    key = lax.broadcasted_iota(jnp.int32, (ATTN_TILE, ATTN_TILE), 0)
                    qry = lax.broadcasted_iota(jnp.int32, (ATTN_TILE, ATTN_TILE), 1)
                    s_t = jnp.where(key <= qry, s_t, NEG)
                _store_scores(buf, h, [s_t[t * ATTN_TILE:(t + 1) * ATTN_TILE]
                                       for t in range(span)])

        def attend(n, buf):
            qi, kj = tq_ref[offset + n], tk_ref[offset + n]
            for h in range(HEAD_GROUP):
                parts = [(ATTN_TILE, vt_ref[kj + t, h * V_HEAD:(h + 1) * V_HEAD, :], None)
                         for t in range(span)]
                _softmax_step(buf, h, parts, m_ref.at[qi, h], acc_ref.at[qi, h], diagonal)

        return score, attend

    bufs = ((s0_ref, t0_ref), (s1_ref, t1_ref))
    n_diag, n_pairs, n_single = phases
    _pipeline(n_diag, *stage(0, 1, True), bufs, PIPELINE_UNROLL)
    _pipeline(n_pairs, *stage(n_diag, 2, False), bufs, PIPELINE_UNROLL // 2)
    _pipeline(n_single, *stage(n_diag + n_pairs, 1, False), bufs, PIPELINE_UNROLL)
    _write_heads(acc_ref, o_ref, nq, V_HEAD)


def _mla(qmt, km, vbt):
    b, seq, _ = km.shape
    nq = seq // ATTN_TILE
    qw = HEAD_GROUP * MLA_HEAD_PAD
    vw = HEAD_GROUP * V_HEAD
    tq, tk, phases = _causal_items(nq, 1, pair_far=False)
    per_group = lambda bi, g, *_: (bi, 0, g)
    tiles = lambda bi, g, *_: (bi, 0, g, 0)
    return pl.pallas_call(
        functools.partial(_mla_kernel, nq=nq, phases=phases),
        grid_spec=pltpu.PrefetchScalarGridSpec(
            num_scalar_prefetch=2,
            grid=(b, B_HEADS // HEAD_GROUP),
            in_specs=[
                pl.BlockSpec((None, nq, qw, ATTN_TILE), tiles),
                pl.BlockSpec((None, seq, qw), per_group),
                pl.BlockSpec((None, nq, vw, ATTN_TILE), tiles),
            ],
            out_specs=pl.BlockSpec((None, nq, vw, ATTN_TILE), tiles),
            scratch_shapes=[
                pltpu.VMEM((HEAD_GROUP, ITEM_SPAN * ATTN_TILE, ATTN_TILE), F32),
                pltpu.VMEM((HEAD_GROUP, ITEM_SPAN * ATTN_TILE, ATTN_TILE), F32),
                pltpu.VMEM((HEAD_GROUP, ITEM_SPAN, 1, ATTN_TILE), F32),
                pltpu.VMEM((HEAD_GROUP, ITEM_SPAN, 1, ATTN_TILE), F32),
                pltpu.VMEM((nq, HEAD_GROUP, 1, ATTN_TILE), F32),
                pltpu.VMEM((nq, HEAD_GROUP, V_HEAD + SUM_ROWS, ATTN_TILE), F32),
            ]),
        out_shape=jax.ShapeDtypeStruct((b, nq, B_WIDTH, ATTN_TILE), BF16),
        compiler_params=pltpu.CompilerParams(
            dimension_semantics=("arbitrary", "arbitrary"), vmem_limit_bytes=VMEM_LIMIT),
        name="mla_attention",
    )(tq, tk, qmt, km, vbt)


def _out_kernel(x_ref, oa_ref, ob_ref, gattn_ref, wg_ref, wpa_ref, wpb_ref, wout_ref,
                gmlp_ref, wup_ref, wdn_ref, gfin_ref, o_ref):
    x = x_ref[...]
    n = _rms(x, gattn_ref[...]).astype(BF16)
    gates = lax.dot_general(n, wg_ref[...], NT_DIMS, preferred_element_type=F32)
    tn_dims = (((0,), (0,)), ((), ()))
    tiles = range(ROW_TILE // ATTN_TILE)
    pa = jnp.concatenate([lax.dot_general(oa_ref[t], wpa_ref[...], tn_dims,
                                          preferred_element_type=F32) for t in tiles], axis=0)
    pb = jnp.concatenate([lax.dot_general(ob_ref[t], wpb_ref[...], tn_dims,
                                          preferred_element_type=F32) for t in tiles], axis=0)
    merged = (jax.nn.sigmoid(gates[:, :D_MODEL]) * pa
              + jax.nn.sigmoid(gates[:, D_MODEL:]) * pb).astype(BF16)
    h = x + jnp.dot(merged, wout_ref[...], preferred_element_type=F32)
    m = _rms(h, gmlp_ref[...]).astype(BF16)
    for c in range(D_FF // FF_CHUNK):
        cols = slice(c * FF_CHUNK, (c + 1) * FF_CHUNK)
        up = jnp.dot(m, wup_ref[:, cols], preferred_element_type=F32)
        act = jnp.square(jnp.maximum(up, 0.0)).astype(BF16)
        h = h + jnp.dot(act, wdn_ref[cols, :], preferred_element_type=F32)
    o_ref[...] = _rms(h, gfin_ref[...])


def _out_block(x2, oa, ob, g_attn, wg, wpa, wpb, wout, g_mlp, wup, wdn, g_fin):
    n_rows = x2.shape[0]
    tm = ROW_TILE
    row = lambda i: (i, 0)
    return pl.pallas_call(
        _out_kernel,
        grid=(n_rows // tm,),
        in_specs=[
            pl.BlockSpec((tm, D_MODEL), row),
            pl.BlockSpec((tm // ATTN_TILE, A_WIDTH, ATTN_TILE), lambda i: (i, 0, 0)),
            pl.BlockSpec((tm // ATTN_TILE, B_WIDTH, ATTN_TILE), lambda i: (i, 0, 0)),
            _const_spec(g_attn.shape), _const_spec(wg.shape), _const_spec(wpa.shape),
            _const_spec(wpb.shape), _const_spec(wout.shape), _const_spec(g_mlp.shape),
            _const_spec(wup.shape), _const_spec(wdn.shape), _const_spec(g_fin.shape),
        ],
        out_specs=pl.BlockSpec((tm, D_MODEL), row),
        out_shape=jax.ShapeDtypeStruct((n_rows, D_MODEL), F32),
        compiler_params=pltpu.CompilerParams(
            dimension_semantics=("arbitrary",), vmem_limit_bytes=VMEM_LIMIT),
        name="out_mlp",
    )(x2, oa, ob, g_attn, wg, wpa, wpb, wout, g_mlp, wup, wdn, g_fin)


def _rope_tables(seq):
    half = QK_ROPE // 2
    inv_freq = ROPE_THETA ** (-jnp.arange(half, dtype=F32) / half)
    ang = jnp.arange(seq).astype(F32)[:, None] * inv_freq[None, :]
    cos2 = jnp.tile(jnp.cos(ang), (1, 2))
    sin2 = jnp.tile(jnp.sin(ang), (1, 2))
    pad = jnp.zeros((seq, MLA_HEAD_PAD - QK_NOPE - QK_ROPE), F32)
    cos_t = jnp.concatenate([jnp.ones((seq, QK_NOPE), F32), cos2, pad], axis=1)
    sin_t = jnp.concatenate([jnp.zeros((seq, QK_NOPE), F32), sin2, pad], axis=1)
    return cos_t, sin_t


def _pad_heads(w_nope, w_rope):
    k, h, _ = w_nope.shape
    if w_rope is None:
        w_rope = jnp.zeros((k, h, QK_ROPE), w_nope.dtype)
    pad = jnp.zeros((k, h, MLA_HEAD_PAD - QK_NOPE - QK_ROPE), w_nope.dtype)
    return jnp.concatenate([w_nope, w_rope, pad], axis=-1).reshape(k, h * MLA_HEAD_PAD)


def kernel(x, w_in, rel_bias, mla_q_norm, w_uq, mla_kv_norm, w_uk, w_uv, w_proj_a, w_proj_b,
           w_out, norm_attn, norm_mlp, w_mlp_up, w_mlp_down, norm_final):
    b, seq, d = x.shape
    assert d == D_MODEL and seq % ROW_TILE == 0 and seq // MOBA_BLOCK <= LANES
    assert w_in.shape[0] == 1, "single-layer block"
    x2 = x.reshape(b * seq, d)

    w_t = jnp.swapaxes(w_in[0], 0, 1)
    o_k = A_WIDTH
    o_v = 2 * A_WIDTH
    o_cq = 3 * A_WIDTH
    o_ckv = o_cq + Q_LORA
    o_kr = o_ckv + KV_LORA
    o_g = o_kr + QK_ROPE
    wk = w_t[o_k:o_v].astype(BF16)
    wqv = jnp.concatenate([w_t[:o_k], w_t[o_v:o_cq]], axis=0).astype(BF16)
    w_kr = w_t[o_kr:o_g]
    half = QK_ROPE // 2
    w_kr_rot = jnp.concatenate([-w_kr[half:], w_kr[:half]], axis=0)
    row_pad = lambda a: jnp.pad(a, ((QK_NOPE, MLA_HEAD_PAD - QK_NOPE - QK_ROPE), (0, 0)))
    wc = jnp.concatenate(
        [w_t[o_cq:o_kr], row_pad(w_kr), row_pad(w_kr_rot)], axis=0).astype(BF16)
    wg = w_t[o_g:].astype(BF16)

    uq = w_uq[0].reshape(Q_LORA, B_HEADS, QK_NOPE + QK_ROPE)
    uq_nope, uq_rope = uq[..., :QK_NOPE], uq[..., QK_NOPE:]
    wuq = _pad_heads(uq_nope, uq_rope).T.astype(BF16)
    wuk = _pad_heads(w_uk[0].reshape(KV_LORA, B_HEADS, QK_NOPE), None).astype(BF16)
    wuv = w_uv[0].T.astype(BF16)

    cos_t, sin_t = _rope_tables(seq)
    q_scale = (QK_NOPE + QK_ROPE) ** -0.5 * LOG2E
    row2 = lambda a: a.reshape(1, -1)

    qat, ka, vat, kmean, qmt, km, vbt = _inproj(
        x2, row2(norm_attn[0]), wk, wqv, wc, row2(mla_q_norm[0]), row2(mla_kv_norm[0]),
        wuq, wuk, wuv, (cos_t * q_scale).T, (sin_t * q_scale).T, cos_t, sin_t, b, seq)

    as_seq = lambda a: a.reshape(b, seq, a.shape[-1])
    bias = _bias_tiles(rel_bias)
    oa = _moba(qat, as_seq(ka), vat, kmean.reshape(b, seq // MOBA_BLOCK, A_WIDTH), bias)
    ob = _mla(qmt, as_seq(km), vbt)

    out = _out_block(
        x2, oa.reshape(-1, A_WIDTH, ATTN_TILE), ob.reshape(-1, B_WIDTH, ATTN_TILE),
        row2(norm_attn[0]),
        wg, w_proj_a[0].astype(BF16), w_proj_b[0].astype(BF16), w_out[0].astype(BF16),
        row2(norm_mlp[0]), w_mlp_up[0].astype(BF16), w_mlp_down[0].astype(BF16),
        row2(norm_final))
    return out.reshape(b, seq, d)
```

```python
import functools
import math

import jax
import jax.numpy as jnp
from jax import lax
from jax.experimental import pallas as pl
from jax.experimental.pallas import tpu as pltpu

D_MODEL = 1024
A_HEADS = 8
A_HEAD_DIM = 64
A_WIDTH = A_HEADS * A_HEAD_DIM
MOBA_BLOCK = 256
MOBA_TOPK = 3
REL_BUCKETS = 32
REL_MAX_DIST = 128
B_HEADS = 8
QK_NOPE = 64
QK_ROPE = 32
V_HEAD = 64
B_WIDTH = B_HEADS * V_HEAD
Q_LORA = 384
KV_LORA = 256
ROPE_THETA = 10000.0
D_FF = 4 * D_MODEL
EPS = 1e-6
NEG = -1e30
LOG2E = math.log2(math.e)

LANES = 128
MLA_HEAD_PAD = 128
ATTN_TILE = 256
HEAD_GROUP = 4
SUM_ROWS = 16
ITEM_SPAN = 2
PIPELINE_UNROLL = 16
ROW_TILE = 512
FF_CHUNK = 1024
VMEM_LIMIT = 56 * 1024 * 1024

F32 = jnp.float32
BF16 = jnp.bfloat16
NT_DIMS = (((1,), (1,)), ((), ()))


def _rms(xf, g):
    y = xf * lax.rsqrt(jnp.mean(xf * xf, axis=-1, keepdims=True) + EPS)
    return y * g


def _const_spec(shape):
    zeros = (0,) * len(shape)
    return pl.BlockSpec(shape, lambda *_: zeros, pipeline_mode=pl.Buffered(1))


def _bias_kernel(rel_ref, o_ref):
    h = pl.program_id(0)
    shape = (MOBA_BLOCK, 2 * MOBA_BLOCK)
    r = lax.broadcasted_iota(jnp.int32, shape, 0)
    c = lax.broadcasted_iota(jnp.int32, shape, 1)
    d = c - r
    dist = jnp.maximum(d, 0)
    max_exact = REL_BUCKETS // 2
    df = jnp.maximum(dist, 1).astype(F32)
    large = max_exact + (jnp.log(df / max_exact) / math.log(REL_MAX_DIST / max_exact)
                         * (REL_BUCKETS - max_exact)).astype(jnp.int32)
    large = jnp.minimum(large, REL_BUCKETS - 1)
    bucket = jnp.where(dist < max_exact, dist, large)
    val = jnp.zeros(shape, F32)
    for b in range(REL_BUCKETS):
        val = jnp.where(bucket == b, rel_ref[b, h], val)
    o_ref[...] = jnp.where(d >= 0, (val - rel_ref[REL_BUCKETS - 1, h]) * LOG2E, NEG)


def _bias_tiles(rel_bias):
    return pl.pallas_call(
        _bias_kernel,
        grid=(A_HEADS,),
        in_specs=[pl.BlockSpec(memory_space=pltpu.SMEM)],
        out_specs=pl.BlockSpec((None, MOBA_BLOCK, 2 * MOBA_BLOCK), lambda h: (h, 0, 0)),
        out_shape=jax.ShapeDtypeStruct((A_HEADS, MOBA_BLOCK, 2 * MOBA_BLOCK), F32),
        name="moba_bias_tiles",
    )(rel_bias)


def _inproj_kernel(x_ref, g_ref, wk_ref, wqv_ref, wc_ref, qn_ref, kvn_ref, wuq_ref,
                   wuk_ref, wuv_ref, cqt_ref, sqt_ref, ck_ref, sk_ref,
                   qat_ref, ka_ref, vat_ref, kmean_ref, qmt_ref, km_ref, vbt_ref):
    n = _rms(x_ref[...], g_ref[...]).astype(BF16)
    k = lax.dot_general(n, wk_ref[...], NT_DIMS, preferred_element_type=F32)
    ka_ref[...] = k.astype(BF16)
    nblk = ROW_TILE // ATTN_TILE
    kmean_ref[0] = jnp.sum(k.reshape(nblk, MOBA_BLOCK, A_WIDTH), axis=1) * (1.0 / MOBA_BLOCK)

    c = lax.dot_general(n, wc_ref[...], NT_DIMS, preferred_element_type=F32)
    cq = _rms(c[:, :Q_LORA], qn_ref[...]).astype(BF16)
    ckv = _rms(c[:, Q_LORA:Q_LORA + KV_LORA], kvn_ref[...]).astype(BF16)
    kr = c[:, Q_LORA + KV_LORA:Q_LORA + KV_LORA + LANES]
    krr = c[:, Q_LORA + KV_LORA + LANES:]
    for t in range(nblk):
        rows = slice(t * ATTN_TILE, (t + 1) * ATTN_TILE)
        qv_t = lax.dot_general(wqv_ref[...], n[rows], NT_DIMS,
                               preferred_element_type=F32)
        qat_ref[t] = (qv_t[:A_WIDTH] * (A_HEAD_DIM ** -0.5 * LOG2E)).astype(BF16)
        vat_ref[t] = qv_t[A_WIDTH:].astype(BF16)
        vbt_ref[t] = lax.dot_general(wuv_ref[...], ckv[rows], NT_DIMS,
                                     preferred_element_type=F32).astype(BF16)
        q_t = lax.dot_general(wuq_ref[...], cq[rows], NT_DIMS,
                              preferred_element_type=F32)
        cq_t, sq_t = cqt_ref[:, rows], sqt_ref[:, rows]
        half = QK_ROPE // 2
        for h in range(B_HEADS):
            q_h = q_t[h * MLA_HEAD_PAD:(h + 1) * MLA_HEAD_PAD]
            x1 = q_h[QK_NOPE:QK_NOPE + half]
            x2 = q_h[QK_NOPE + half:QK_NOPE + QK_ROPE]
            rot_h = jnp.concatenate(
                [jnp.zeros((QK_NOPE, ATTN_TILE), F32), -x2, x1,
                 jnp.zeros((MLA_HEAD_PAD - QK_NOPE - QK_ROPE, ATTN_TILE), F32)], axis=0)
            qmt_ref[t, h * MLA_HEAD_PAD:(h + 1) * MLA_HEAD_PAD, :] = (
                q_h * cq_t + rot_h * sq_t).astype(BF16)
    kn = jnp.dot(ckv, wuk_ref[...], preferred_element_type=F32)
    k_rope = kr * ck_ref[...] + krr * sk_ref[...]
    for h in range(B_HEADS):
        sl = slice(h * MLA_HEAD_PAD, (h + 1) * MLA_HEAD_PAD)
        km_ref[:, sl] = (kn[:, sl] + k_rope).astype(BF16)


def _inproj(x2, g_attn, wk, wqv, wc, qn, kvn, wuq, wuk, wuv, cq_tt, sq_tt, ck_t, sk_t,
            batch, seq):
    n_rows = x2.shape[0]
    tm = ROW_TILE
    steps = n_rows // tm
    per_seq = seq // tm
    nblk = tm // ATTN_TILE
    row = lambda i: (i, 0)
    tab = lambda i: (i % per_seq, 0)
    tab_t = lambda i: (0, i % per_seq)
    tile_map = lambda i: (i // per_seq, i % per_seq, 0, 0)
    bf = lambda w: jax.ShapeDtypeStruct((n_rows, w), BF16)
    tiles = lambda w: jax.ShapeDtypeStruct((batch, seq // ATTN_TILE, w, ATTN_TILE), BF16)
    tile_spec = lambda w: pl.BlockSpec((None, nblk, w, ATTN_TILE), tile_map)
    return pl.pallas_call(
        _inproj_kernel,
        grid=(steps,),
        in_specs=[
            pl.BlockSpec((tm, D_MODEL), row),
            _const_spec(g_attn.shape), _const_spec(wk.shape), _const_spec(wqv.shape),
            _const_spec(wc.shape), _const_spec(qn.shape), _const_spec(kvn.shape),
            _const_spec(wuq.shape), _const_spec(wuk.shape),
            _const_spec(wuv.shape),
            pl.BlockSpec((LANES, tm), tab_t), pl.BlockSpec((LANES, tm), tab_t),
            pl.BlockSpec((tm, LANES), tab), pl.BlockSpec((tm, LANES), tab),
        ],
        out_specs=[
            tile_spec(A_WIDTH), pl.BlockSpec((tm, A_WIDTH), row), tile_spec(A_WIDTH),
            pl.BlockSpec((1, nblk, A_WIDTH), lambda i: (i, 0, 0)),
            tile_spec(B_HEADS * MLA_HEAD_PAD),
            pl.BlockSpec((tm, B_HEADS * MLA_HEAD_PAD), row),
            tile_spec(B_WIDTH),
        ],
        out_shape=[
            tiles(A_WIDTH), bf(A_WIDTH), tiles(A_WIDTH),
            jax.ShapeDtypeStruct((steps, nblk, A_WIDTH), F32),
            tiles(B_HEADS * MLA_HEAD_PAD), bf(B_HEADS * MLA_HEAD_PAD), tiles(B_WIDTH),
        ],
        compiler_params=pltpu.CompilerParams(
            dimension_semantics=("arbitrary",), vmem_limit_bytes=VMEM_LIMIT),
        name="in_proj",
    )(x2, g_attn, wk, wqv, wc, qn, kvn, wuq, wuk, wuv, cq_tt, sq_tt, ck_t, sk_t)


def _store_scores(buf, h, parts):
    s_ref, max_ref = buf
    row = 0
    for idx, s_t in enumerate(parts):
        s_ref[h, row:row + s_t.shape[0]] = s_t
        max_ref[h, idx] = jnp.max(s_t, axis=0, keepdims=True)
        row += s_t.shape[0]


def _softmax_step(buf, h, parts, m_ref, acc_ref, first):
    s_ref, max_ref = buf
    maxes = [max_ref[h, idx] for idx in range(len(parts))]
    m_new = None if first else m_ref[...]
    for s_max, (_, _, col_keep) in zip(maxes, parts):
        if col_keep is not None:
            s_max = jnp.where(col_keep > 0, s_max, NEG)
        m_new = s_max if m_new is None else jnp.maximum(m_new, s_max)
    pv = None
    row = 0
    for s_max, (rows, v_t, col_keep) in zip(maxes, parts):
        ones = (lax.broadcasted_iota(jnp.int32, (SUM_ROWS, rows), 0) == 0).astype(BF16)
        v_aug = jnp.concatenate([v_t, ones], axis=0)
        m_exp = m_new if col_keep is None else jnp.maximum(m_new, s_max)
        p_t = jnp.exp2(s_ref[h, row:row + rows] - m_exp).astype(BF16)
        part = jnp.dot(v_aug, p_t, preferred_element_type=F32)
        if col_keep is not None:
            part = part * col_keep
        pv = part if pv is None else pv + part
        row += rows
    if first:
        acc_ref[...] = pv
    else:
        acc_ref[...] = jnp.exp2(m_ref[...] - m_new) * acc_ref[...] + pv
    m_ref[...] = m_new


def _pipeline(n_items, score, attend, bufs, unroll):
    if n_items == 0:
        return
    score(0, bufs[0])
    n_loops = (n_items - 1) // unroll

    def body(p, carry):
        for t in range(unroll):
            n = unroll * p + t
            score(n + 1, bufs[(t + 1) % 2])
            attend(n, bufs[t % 2])
        return carry

    lax.fori_loop(0, n_loops, body, 0)
    done = unroll * n_loops
    for n in range(done, n_items):
        if n + 1 < n_items:
            score(n + 1, bufs[(n + 1 - done) % 2])
        attend(n, bufs[(n - done) % 2])


def _tile_rows(j, span=1):
    return pl.ds(pl.multiple_of(j * ATTN_TILE, ATTN_TILE), span * ATTN_TILE)


def _write_heads(acc_ref, o_ref, nq, head_dim):
    def one(qi, carry):
        for h in range(HEAD_GROUP):
            acc = acc_ref[qi, h]
            o_ref[qi, h * head_dim:(h + 1) * head_dim, :] = (
                acc[:head_dim] / acc[head_dim:head_dim + 1]).astype(o_ref.dtype)
        return carry

    lax.fori_loop(0, nq, one, 0)


def _causal_items(nq, n_near, pair_far):
    phases = [[(i, i - d) for i in range(d, nq)] for d in range(n_near)]
    pairs, singles = [], []
    for i in range(n_near, nq):
        n_far = i - n_near + 1
        n_paired = n_far - n_far % 2 if pair_far else 0
        pairs += [(i, j) for j in range(0, n_paired, 2)]
        singles += [(i, j) for j in range(n_paired, n_far)]
    phases += [pairs, singles]
    flat = [it for ph in phases for it in ph]
    tq = jnp.asarray([it[0] for it in flat], jnp.int32)
    tk = jnp.asarray([it[1] for it in flat], jnp.int32)
    return tq, tk, [len(ph) for ph in phases]


def _moba_kernel(tq_ref, tk_ref, qt_ref, k_ref, vt_ref, kmean_ref, bias_ref, o_ref,
                 keep_ref, s0_ref, s1_ref, t0_ref, t1_ref, m_ref, acc_ref, *, seq, phases):
    nblk = seq // MOBA_BLOCK

    def head_query(qi, h):
        pr, sub = divmod(h, 2)
        q_h = qt_ref[qi, pl.ds(pr * LANES + sub * A_HEAD_DIM, A_HEAD_DIM), :]
        return q_h, slice(h * A_HEAD_DIM, (h + 1) * A_HEAD_DIM)

    def select_blocks(i):
        row = lax.broadcasted_iota(jnp.int32, (nblk, ATTN_TILE), 0)
        rowf = row.astype(F32)
        past = row < i
        for h in range(HEAD_GROUP):
            q_h, lanes = head_query(i, h)
            gate = jnp.dot(kmean_ref[:, lanes].astype(BF16), q_h,
                           preferred_element_type=F32)
            gate = jnp.where(past, gate, -jnp.inf)
            picked = jnp.zeros((nblk, ATTN_TILE), F32)
            for _ in range(MOBA_TOPK):
                top = jnp.max(gate, axis=0, keepdims=True)
                first = jnp.min(jnp.where(gate == top, rowf, float(nblk)), axis=0, keepdims=True)
                pick = rowf == first
                picked = jnp.where(pick, 1.0, picked)
                gate = jnp.where(pick, -jnp.inf, gate)
            keep_ref[i, h] = jnp.where(past, picked, 0.0)

    def stage(offset, span, bias_cols, first):
        def score(n, buf):
            qi, kj = tq_ref[offset + n], tk_ref[offset + n]
            for h in range(HEAD_GROUP):
                q_h, lanes = head_query(qi, h)
                s_t = jnp.dot(k_ref[_tile_rows(kj, span), lanes], q_h,
                              preferred_element_type=F32)
                if bias_cols is not None:
                    s_t = s_t + bias_ref[h, :, bias_cols]
                _store_scores(buf, h, [s_t[t * MOBA_BLOCK:(t + 1) * MOBA_BLOCK]
                                       for t in range(span)])

        def attend(n, buf):
            qi, kj = tq_ref[offset + n], tk_ref[offset + n]
            for h in range(HEAD_GROUP):
                parts = [(MOBA_BLOCK, vt_ref[kj + t, h * A_HEAD_DIM:(h + 1) * A_HEAD_DIM, :],
                          None if first else keep_ref[qi, h, pl.ds(kj + t, 1), :])
                         for t in range(span)]
                _softmax_step(buf, h, parts, m_ref.at[qi, h], acc_ref.at[qi, h], first)
            if first:
                select_blocks(qi)

        return score, attend

    bufs = ((s0_ref, t0_ref), (s1_ref, t1_ref))
    n_own, n_prev, n_pairs, n_single = phases
    _pipeline(n_own, *stage(0, 1, slice(0, MOBA_BLOCK), True), bufs, PIPELINE_UNROLL)
    offset = n_own
    _pipeline(n_prev, *stage(offset, 1, slice(MOBA_BLOCK, 2 * MOBA_BLOCK), False), bufs,
              PIPELINE_UNROLL)
    offset += n_prev
    _pipeline(n_pairs, *stage(offset, 2, None, False), bufs, PIPELINE_UNROLL // 2)
    offset += n_pairs
    _pipeline(n_single, *stage(offset, 1, None, False), bufs, PIPELINE_UNROLL)
    _write_heads(acc_ref, o_ref, nblk, A_HEAD_DIM)


def _moba(qat, ka, vat, kmean, bias):
    b, seq, _ = ka.shape
    nq = seq // ATTN_TILE
    gw = HEAD_GROUP * A_HEAD_DIM
    tq, tk, phases = _causal_items(nq, 2, pair_far=True)
    per_group = lambda bi, g, *_: (bi, 0, g)
    tiles = lambda bi, g, *_: (bi, 0, g, 0)
    return pl.pallas_call(
        functools.partial(_moba_kernel, seq=seq, phases=phases),
        grid_spec=pltpu.PrefetchScalarGridSpec(
            num_scalar_prefetch=2,
            grid=(b, A_HEADS // HEAD_GROUP),
            in_specs=[
                pl.BlockSpec((None, nq, gw, ATTN_TILE), tiles),
                pl.BlockSpec((None, seq, gw), per_group),
                pl.BlockSpec((None, nq, gw, MOBA_BLOCK), tiles),
                pl.BlockSpec((None, nq, gw), per_group),
                pl.BlockSpec((HEAD_GROUP, MOBA_BLOCK, 2 * MOBA_BLOCK), lambda bi, g, *_: (g, 0, 0)),
            ],
            out_specs=pl.BlockSpec((None, nq, gw, ATTN_TILE), tiles),
            scratch_shapes=[
                pltpu.VMEM((nq, HEAD_GROUP, nq, ATTN_TILE), F32),
                pltpu.VMEM((HEAD_GROUP, ITEM_SPAN * ATTN_TILE, ATTN_TILE), F32),
                pltpu.VMEM((HEAD_GROUP, ITEM_SPAN * ATTN_TILE, ATTN_TILE), F32),
                pltpu.VMEM((HEAD_GROUP, ITEM_SPAN, 1, ATTN_TILE), F32),
                pltpu.VMEM((HEAD_GROUP, ITEM_SPAN, 1, ATTN_TILE), F32),
                pltpu.VMEM((nq, HEAD_GROUP, 1, ATTN_TILE), F32),
                pltpu.VMEM((nq, HEAD_GROUP, A_HEAD_DIM + SUM_ROWS, ATTN_TILE), F32),
            ]),
        out_shape=jax.ShapeDtypeStruct((b, nq, A_WIDTH, ATTN_TILE), BF16),
        compiler_params=pltpu.CompilerParams(
            dimension_semantics=("arbitrary", "arbitrary"), vmem_limit_bytes=VMEM_LIMIT),
        name="moba_attention",
    )(tq, tk, qat, ka, vat, kmean, bias)


def _mla_kernel(tq_ref, tk_ref, qt_ref, k_ref, vt_ref, o_ref, s0_ref, s1_ref, t0_ref, t1_ref,
                m_ref, acc_ref, *, nq, phases):
    def stage(offset, span, diagonal):
        def score(n, buf):
            qi, kj = tq_ref[offset + n], tk_ref[offset + n]
            for h in range(HEAD_GROUP):
                sl = slice(h * MLA_HEAD_PAD, h * MLA_HEAD_PAD + QK_NOPE + QK_ROPE)
                s_t = jnp.dot(k_ref[_tile_rows(kj, span), sl], qt_ref[qi, sl, :],
                              preferred_element_type=F32)
                if diagonal:
                    key = lax.broadcasted_iota(jnp.int32, (ATTN_TILE, ATTN_TILE), 0)
                    qry = lax.broadcasted_iota(jnp.int32, (ATTN_TILE, ATTN_TILE), 1)
                    s_t = jnp.where(key <= qry, s_t, NEG)
                _store_scores(buf, h, [s_t[t * ATTN_TILE:(t + 1) * ATTN_TILE]
                                       for t in range(span)])

        def attend(n, buf):
            qi, kj = tq_ref[offset + n], tk_ref[offset + n]
            for h in range(HEAD_GROUP):
                parts = [(ATTN_TILE, vt_ref[kj + t, h * V_HEAD:(h + 1) * V_HEAD, :], None)
                         for t in range(span)]
                _softmax_step(buf, h, parts, m_ref.at[qi, h], acc_ref.at[qi, h], diagonal)

        return score, attend

    bufs = ((s0_ref, t0_ref), (s1_ref, t1_ref))
    n_diag, n_pairs, n_single = phases
    _pipeline(n_diag, *stage(0, 1, True), bufs, PIPELINE_UNROLL)
    _pipeline(n_pairs, *stage(n_diag, 2, False), bufs, PIPELINE_UNROLL // 2)
    _pipeline(n_single, *stage(n_diag + n_pairs, 1, False), bufs, PIPELINE_UNROLL)
    _write_heads(acc_ref, o_ref, nq, V_HEAD)


def _mla(qmt, km, vbt):
    b, seq, _ = km.shape
    nq = seq // ATTN_TILE
    qw = HEAD_GROUP * MLA_HEAD_PAD
    vw = HEAD_GROUP * V_HEAD
    tq, tk, phases = _causal_items(nq, 1, pair_far=True)
    per_group = lambda bi, g, *_: (bi, 0, g)
    tiles = lambda bi, g, *_: (bi, 0, g, 0)
    return pl.pallas_call(
        functools.partial(_mla_kernel, nq=nq, phases=phases),
        grid_spec=pltpu.PrefetchScalarGridSpec(
            num_scalar_prefetch=2,
            grid=(b, B_HEADS // HEAD_GROUP),
            in_specs=[
                pl.BlockSpec((None, nq, qw, ATTN_TILE), tiles),
                pl.BlockSpec((None, seq, qw), per_group),
                pl.BlockSpec((None, nq, vw, ATTN_TILE), tiles),
            ],
            out_specs=pl.BlockSpec((None, nq, vw, ATTN_TILE), tiles),
            scratch_shapes=[
                pltpu.VMEM((HEAD_GROUP, ITEM_SPAN * ATTN_TILE, ATTN_TILE), F32),
                pltpu.VMEM((HEAD_GROUP, ITEM_SPAN * ATTN_TILE, ATTN_TILE), F32),
                pltpu.VMEM((HEAD_GROUP, ITEM_SPAN, 1, ATTN_TILE), F32),
                pltpu.VMEM((HEAD_GROUP, ITEM_SPAN, 1, ATTN_TILE), F32),
                pltpu.VMEM((nq, HEAD_GROUP, 1, ATTN_TILE), F32),
                pltpu.VMEM((nq, HEAD_GROUP, V_HEAD + SUM_ROWS, ATTN_TILE), F32),
            ]),
        out_shape=jax.ShapeDtypeStruct((b, nq, B_WIDTH, ATTN_TILE), BF16),
        compiler_params=pltpu.CompilerParams(
            dimension_semantics=("arbitrary", "arbitrary"), vmem_limit_bytes=VMEM_LIMIT),
        name="mla_attention",
    )(tq, tk, qmt, km, vbt)


def _out_kernel(x_ref, oa_ref, ob_ref, gattn_ref, wg_ref, wpa_ref, wpb_ref, wout_ref,
                gmlp_ref, wup_ref, wdn_ref, gfin_ref, o_ref):
    x = x_ref[...]
    n = _rms(x, gattn_ref[...]).astype(BF16)
    gates = lax.dot_general(n, wg_ref[...], NT_DIMS, preferred_element_type=F32)
    tn_dims = (((0,), (0,)), ((), ()))
    tiles = range(ROW_TILE // ATTN_TILE)
    pa = jnp.concatenate([lax.dot_general(oa_ref[t], wpa_ref[...], tn_dims,
                                          preferred_element_type=F32) for t in tiles], axis=0)
    pb = jnp.concatenate([lax.dot_general(ob_ref[t], wpb_ref[...], tn_dims,
                                          preferred_element_type=F32) for t in tiles], axis=0)
    merged = (jax.nn.sigmoid(gates[:, :D_MODEL]) * pa
              + jax.nn.sigmoid(gates[:, D_MODEL:]) * pb).astype(BF16)
    h = x + jnp.dot(merged, wout_ref[...], preferred_element_type=F32)
    m = _rms(h, gmlp_ref[...]).astype(BF16)
    for c in range(D_FF // FF_CHUNK):
        cols = slice(c * FF_CHUNK, (c + 1) * FF_CHUNK)
        up = jnp.dot(m, wup_ref[:, cols], preferred_element_type=F32)
        act = jnp.square(jnp.maximum(up, 0.0)).astype(BF16)
        h = h + jnp.dot(act, wdn_ref[cols, :], preferred_element_type=F32)
    o_ref[...] = _rms(h, gfin_ref[...])


def _out_block(x2, oa, ob, g_attn, wg, wpa, wpb, wout, g_mlp, wup, wdn, g_fin):
    n_rows = x2.shape[0]
    tm = ROW_TILE
    row = lambda i: (i, 0)
    return pl.pallas_call(
        _out_kernel,
        grid=(n_rows // tm,),
        in_specs=[
            pl.BlockSpec((tm, D_MODEL), row),
            pl.BlockSpec((tm // ATTN_TILE, A_WIDTH, ATTN_TILE), lambda i: (i, 0, 0)),
            pl.BlockSpec((tm // ATTN_TILE, B_WIDTH, ATTN_TILE), lambda i: (i, 0, 0)),
            _const_spec(g_attn.shape), _const_spec(wg.shape), _const_spec(wpa.shape),
            _const_spec(wpb.shape), _const_spec(wout.shape), _const_spec(g_mlp.shape),
            _const_spec(wup.shape), _const_spec(wdn.shape), _const_spec(g_fin.shape),
        ],
        out_specs=pl.BlockSpec((tm, D_MODEL), row),
        out_shape=jax.ShapeDtypeStruct((n_rows, D_MODEL), F32),
        compiler_params=pltpu.CompilerParams(
            dimension_semantics=("arbitrary",), vmem_limit_bytes=VMEM_LIMIT),
        name="out_mlp",
    )(x2, oa, ob, g_attn, wg, wpa, wpb, wout, g_mlp, wup, wdn, g_fin)


def _rope_tables(seq):
    half = QK_ROPE // 2
    inv_freq = ROPE_THETA ** (-jnp.arange(half, dtype=F32) / half)
    ang = jnp.arange(seq).astype(F32)[:, None] * inv_freq[None, :]
    cos2 = jnp.tile(jnp.cos(ang), (1, 2))
    sin2 = jnp.tile(jnp.sin(ang), (1, 2))
    pad = jnp.zeros((seq, MLA_HEAD_PAD - QK_NOPE - QK_ROPE), F32)
    cos_t = jnp.concatenate([jnp.ones((seq, QK_NOPE), F32), cos2, pad], axis=1)
    sin_t = jnp.concatenate([jnp.zeros((seq, QK_NOPE), F32), sin2, pad], axis=1)
    return cos_t, sin_t


def _pad_heads(w_nope, w_rope):
    k, h, _ = w_nope.shape
    if w_rope is None:
        w_rope = jnp.zeros((k, h, QK_ROPE), w_nope.dtype)
    pad = jnp.zeros((k, h, MLA_HEAD_PAD - QK_NOPE - QK_ROPE), w_nope.dtype)
    return jnp.concatenate([w_nope, w_rope, pad], axis=-1).reshape(k, h * MLA_HEAD_PAD)


def kernel(x, w_in, rel_bias, mla_q_norm, w_uq, mla_kv_norm, w_uk, w_uv, w_proj_a, w_proj_b,
           w_out, norm_attn, norm_mlp, w_mlp_up, w_mlp_down, norm_final):
    b, seq, d = x.shape
    assert d == D_MODEL and seq % ROW_TILE == 0 and seq // MOBA_BLOCK <= LANES
    assert w_in.shape[0] == 1, "single-layer block"
    x2 = x.reshape(b * seq, d)

    w_t = jnp.swapaxes(w_in[0], 0, 1)
    o_k = A_WIDTH
    o_v = 2 * A_WIDTH
    o_cq = 3 * A_WIDTH
    o_ckv = o_cq + Q_LORA
    o_kr = o_ckv + KV_LORA
    o_g = o_kr + QK_ROPE
    wk = w_t[o_k:o_v].astype(BF16)
    wqv = jnp.concatenate([w_t[:o_k], w_t[o_v:o_cq]], axis=0).astype(BF16)
    w_kr = w_t[o_kr:o_g]
    half = QK_ROPE // 2
    w_kr_rot = jnp.concatenate([-w_kr[half:], w_kr[:half]], axis=0)
    row_pad = lambda a: jnp.pad(a, ((QK_NOPE, MLA_HEAD_PAD - QK_NOPE - QK_ROPE), (0, 0)))
    wc = jnp.concatenate(
        [w_t[o_cq:o_kr], row_pad(w_kr), row_pad(w_kr_rot)], axis=0).astype(BF16)
    wg = w_t[o_g:].astype(BF16)

    uq = w_uq[0].reshape(Q_LORA, B_HEADS, QK_NOPE + QK_ROPE)
    uq_nope, uq_rope = uq[..., :QK_NOPE], uq[..., QK_NOPE:]
    wuq = _pad_heads(uq_nope, uq_rope).T.astype(BF16)
    wuk = _pad_heads(w_uk[0].reshape(KV_LORA, B_HEADS, QK_NOPE), None).astype(BF16)
    wuv = w_uv[0].T.astype(BF16)

    cos_t, sin_t = _rope_tables(seq)
    q_scale = (QK_NOPE + QK_ROPE) ** -0.5 * LOG2E
    row2 = lambda a: a.reshape(1, -1)

    qat, ka, vat, kmean, qmt, km, vbt = _inproj(
        x2, row2(norm_attn[0]), wk, wqv, wc, row2(mla_q_norm[0]), row2(mla_kv_norm[0]),
        wuq, wuk, wuv, (cos_t * q_scale).T, (sin_t * q_scale).T, cos_t, sin_t, b, seq)

    as_seq = lambda a: a.reshape(b, seq, a.shape[-1])
    bias = _bias_tiles(rel_bias)
    oa = _moba(qat, as_seq(ka), vat, kmean.reshape(b, seq // MOBA_BLOCK, A_WIDTH), bias)
    ob = _mla(qmt, as_seq(km), vbt)

    out = _out_block(
        x2, oa.reshape(-1, A_WIDTH, ATTN_TILE), ob.reshape(-1, B_WIDTH, ATTN_TILE),
        row2(norm_attn[0]),
        wg, w_proj_a[0].astype(BF16), w_proj_b[0].astype(BF16), w_out[0].astype(BF16),
        row2(norm_mlp[0]), w_mlp_up[0].astype(BF16), w_mlp_down[0].astype(BF16),
        row2(norm_final))
    return out.reshape(b, seq, d)
```

```python
import functools
import math

import jax
import jax.numpy as jnp
from jax import lax
from jax.experimental import pallas as pl
from jax.experimental.pallas import tpu as pltpu

D_MODEL = 1024
A_HEADS = 8
A_HEAD_DIM = 64
A_WIDTH = A_HEADS * A_HEAD_DIM
MOBA_BLOCK = 256
MOBA_TOPK = 3
REL_BUCKETS = 32
REL_MAX_DIST = 128
B_HEADS = 8
QK_NOPE = 64
QK_ROPE = 32
V_HEAD = 64
B_WIDTH = B_HEADS * V_HEAD
Q_LORA = 384
KV_LORA = 256
ROPE_THETA = 10000.0
D_FF = 4 * D_MODEL
EPS = 1e-6
NEG = -1e30
LOG2E = math.log2(math.e)

LANES = 128
MLA_HEAD_PAD = 128
ATTN_TILE = 256
HEAD_GROUP = 4
SUM_ROWS = 16
ITEM_SPAN = 2
PIPELINE_UNROLL = 16
ROW_TILE = 512
FF_CHUNK = 1024
VMEM_LIMIT = 56 * 1024 * 1024

F32 = jnp.float32
BF16 = jnp.bfloat16
NT_DIMS = (((1,), (1,)), ((), ()))


def _rms(xf, g):
    y = xf * lax.rsqrt(jnp.mean(xf * xf, axis=-1, keepdims=True) + EPS)
    return y * g


def _const_spec(shape):
    zeros = (0,) * len(shape)
    return pl.BlockSpec(shape, lambda *_: zeros, pipeline_mode=pl.Buffered(1))


def _bias_kernel(rel_ref, o_ref):
    h = pl.program_id(0)
    shape = (MOBA_BLOCK, 2 * MOBA_BLOCK)
    r = lax.broadcasted_iota(jnp.int32, shape, 0)
    c = lax.broadcasted_iota(jnp.int32, shape, 1)
    d = c - r
    dist = jnp.maximum(d, 0)
    max_exact = REL_BUCKETS // 2
    df = jnp.maximum(dist, 1).astype(F32)
    large = max_exact + (jnp.log(df / max_exact) / math.log(REL_MAX_DIST / max_exact)
                         * (REL_BUCKETS - max_exact)).astype(jnp.int32)
    large = jnp.minimum(large, REL_BUCKETS - 1)
    bucket = jnp.where(dist < max_exact, dist, large)
    val = jnp.zeros(shape, F32)
    for b in range(REL_BUCKETS):
        val = jnp.where(bucket == b, rel_ref[b, h], val)
    o_ref[...] = jnp.where(d >= 0, (val - rel_ref[REL_BUCKETS - 1, h]) * LOG2E, NEG)


def _bias_tiles(rel_bias):
    return pl.pallas_call(
        _bias_kernel,
        grid=(A_HEADS,),
        in_specs=[pl.BlockSpec(memory_space=pltpu.SMEM)],
        out_specs=pl.BlockSpec((None, MOBA_BLOCK, 2 * MOBA_BLOCK), lambda h: (h, 0, 0)),
        out_shape=jax.ShapeDtypeStruct((A_HEADS, MOBA_BLOCK, 2 * MOBA_BLOCK), F32),
        name="moba_bias_tiles",
    )(rel_bias)


def _inproj_kernel(x_ref, g_ref, wk_ref, wqv_ref, wc_ref, qn_ref, kvn_ref, wuq_ref,
                   wuk_ref, wuv_ref, cqt_ref, sqt_ref, ck_ref, sk_ref,
                   qat_ref, ka_ref, vat_ref, kmean_ref, qmt_ref, km_ref, vbt_ref):
    n = _rms(x_ref[...], g_ref[...]).astype(BF16)
    k = lax.dot_general(n, wk_ref[...], NT_DIMS, preferred_element_type=F32)
    ka_ref[...] = k.astype(BF16)
    nblk = ROW_TILE // ATTN_TILE
    kmean_ref[0] = jnp.sum(k.reshape(nblk, MOBA_BLOCK, A_WIDTH), axis=1) * (1.0 / MOBA_BLOCK)

    c = lax.dot_general(n, wc_ref[...], NT_DIMS, preferred_element_type=F32)
    cq = _rms(c[:, :Q_LORA], qn_ref[...]).astype(BF16)
    ckv = _rms(c[:, Q_LORA:Q_LORA + KV_LORA], kvn_ref[...]).astype(BF16)
    kr = c[:, Q_LORA + KV_LORA:Q_LORA + KV_LORA + LANES]
    krr = c[:, Q_LORA + KV_LORA + LANES:]
    for t in range(nblk):
        rows = slice(t * ATTN_TILE, (t + 1) * ATTN_TILE)
        qv_t = lax.dot_general(wqv_ref[...], n[rows], NT_DIMS,
                               preferred_element_type=F32)
        qat_ref[t] = (qv_t[:A_WIDTH] * (A_HEAD_DIM ** -0.5 * LOG2E)).astype(BF16)
        vat_ref[t] = qv_t[A_WIDTH:].astype(BF16)
        vbt_ref[t] = lax.dot_general(wuv_ref[...], ckv[rows], NT_DIMS,
                                     preferred_element_type=F32).astype(BF16)
        q_t = lax.dot_general(wuq_ref[...], cq[rows], NT_DIMS,
                              preferred_element_type=F32)
        cq_t, sq_t = cqt_ref[:, rows], sqt_ref[:, rows]
        half = QK_ROPE // 2
        for h in range(B_HEADS):
            q_h = q_t[h * MLA_HEAD_PAD:(h + 1) * MLA_HEAD_PAD]
            x1 = q_h[QK_NOPE:QK_NOPE + half]
            x2 = q_h[QK_NOPE + half:QK_NOPE + QK_ROPE]
            rot_h = jnp.concatenate(
                [jnp.zeros((QK_NOPE, ATTN_TILE), F32), -x2, x1,
                 jnp.zeros((MLA_HEAD_PAD - QK_NOPE - QK_ROPE, ATTN_TILE), F32)], axis=0)
            qmt_ref[t, h * MLA_HEAD_PAD:(h + 1) * MLA_HEAD_PAD, :] = (
                q_h * cq_t + rot_h * sq_t).astype(BF16)
    kn = jnp.dot(ckv, wuk_ref[...], preferred_element_type=F32)
    k_rope = kr * ck_ref[...] + krr * sk_ref[...]
    for h in range(B_HEADS):
        sl = slice(h * MLA_HEAD_PAD, (h + 1) * MLA_HEAD_PAD)
        km_ref[:, sl] = (kn[:, sl] + k_rope).astype(BF16)


def _inproj(x2, g_attn, wk, wqv, wc, qn, kvn, wuq, wuk, wuv, cq_tt, sq_tt, ck_t, sk_t,
            batch, seq):
    n_rows = x2.shape[0]
    tm = ROW_TILE
    steps = n_rows // tm
    per_seq = seq // tm
    nblk = tm // ATTN_TILE
    row = lambda i: (i, 0)
    tab = lambda i: (i % per_seq, 0)
    tab_t = lambda i: (0, i % per_seq)
    tile_map = lambda i: (i // per_seq, i % per_seq, 0, 0)
    bf = lambda w: jax.ShapeDtypeStruct((n_rows, w), BF16)
    tiles = lambda w: jax.ShapeDtypeStruct((batch, seq // ATTN_TILE, w, ATTN_TILE), BF16)
    tile_spec = lambda w: pl.BlockSpec((None, nblk, w, ATTN_TILE), tile_map)
    return pl.pallas_call(
        _inproj_kernel,
        grid=(steps,),
        in_specs=[
            pl.BlockSpec((tm, D_MODEL), row),
            _const_spec(g_attn.shape), _const_spec(wk.shape), _const_spec(wqv.shape),
            _const_spec(wc.shape), _const_spec(qn.shape), _const_spec(kvn.shape),
            _const_spec(wuq.shape), _const_spec(wuk.shape),
            _const_spec(wuv.shape),
            pl.BlockSpec((LANES, tm), tab_t), pl.BlockSpec((LANES, tm), tab_t),
            pl.BlockSpec((tm, LANES), tab), pl.BlockSpec((tm, LANES), tab),
        ],
        out_specs=[
            tile_spec(A_WIDTH), pl.BlockSpec((tm, A_WIDTH), row), tile_spec(A_WIDTH),
            pl.BlockSpec((1, nblk, A_WIDTH), lambda i: (i, 0, 0)),
            tile_spec(B_HEADS * MLA_HEAD_PAD),
            pl.BlockSpec((tm, B_HEADS * MLA_HEAD_PAD), row),
            tile_spec(B_WIDTH),
        ],
        out_shape=[
            tiles(A_WIDTH), bf(A_WIDTH), tiles(A_WIDTH),
            jax.ShapeDtypeStruct((steps, nblk, A_WIDTH), F32),
            tiles(B_HEADS * MLA_HEAD_PAD), bf(B_HEADS * MLA_HEAD_PAD), tiles(B_WIDTH),
        ],
        compiler_params=pltpu.CompilerParams(
            dimension_semantics=("arbitrary",), vmem_limit_bytes=VMEM_LIMIT),
        name="in_proj",
    )(x2, g_attn, wk, wqv, wc, qn, kvn, wuq, wuk, wuv, cq_tt, sq_tt, ck_t, sk_t)


def _store_scores(buf, h, parts):
    s_ref, max_ref = buf
    row = 0
    for idx, s_t in enumerate(parts):
        s_ref[h, row:row + s_t.shape[0]] = s_t
        max_ref[h, idx] = jnp.max(s_t, axis=0, keepdims=True)
        row += s_t.shape[0]


def _softmax_step(buf, h, parts, m_ref, acc_ref, first):
    s_ref, max_ref = buf
    maxes = [max_ref[h, idx] for idx in range(len(parts))]
    m_new = None if first else m_ref[...]
    for s_max, (_, _, col_keep) in zip(maxes, parts):
        if col_keep is not None:
            s_max = jnp.where(col_keep > 0, s_max, NEG)
        m_new = s_max if m_new is None else jnp.maximum(m_new, s_max)
    pv = None
    row = 0
    for s_max, (rows, v_t, col_keep) in zip(maxes, parts):
        ones = (lax.broadcasted_iota(jnp.int32, (SUM_ROWS, rows), 0) == 0).astype(BF16)
        v_aug = jnp.concatenate([v_t, ones], axis=0)
        m_exp = m_new if col_keep is None else jnp.maximum(m_new, s_max)
        p_t = jnp.exp2(s_ref[h, row:row + rows] - m_exp).astype(BF16)
        part = jnp.dot(v_aug, p_t, preferred_element_type=F32)
        if col_keep is not None:
            part = part * col_keep
        pv = part if pv is None else pv + part
        row += rows
    if first:
        acc_ref[...] = pv
    else:
        acc_ref[...] = jnp.exp2(m_ref[...] - m_new) * acc_ref[...] + pv
    m_ref[...] = m_new


def _pipeline(n_items, score, attend, bufs, unroll):
    if n_items == 0:
        return
    score(0, bufs[0])
    n_loops = (n_items - 1) // unroll

    def body(p, carry):
        for t in range(unroll):
            n = unroll * p + t
            score(n + 1, bufs[(t + 1) % 2])
            attend(n, bufs[t % 2])
        return carry

    lax.fori_loop(0, n_loops, body, 0)
    done = unroll * n_loops
    for n in range(done, n_items):
        if n + 1 < n_items:
            score(n + 1, bufs[(n + 1 - done) % 2])
        attend(n, bufs[(n - done) % 2])


def _tile_rows(j, span=1):
    return pl.ds(pl.multiple_of(j * ATTN_TILE, ATTN_TILE), span * ATTN_TILE)


def _write_heads(acc_ref, o_ref, nq, head_dim):
    def one(qi, carry):
        for h in range(HEAD_GROUP):
            acc = acc_ref[qi, h]
            o_ref[qi, h * head_dim:(h + 1) * head_dim, :] = (
                acc[:head_dim] / acc[head_dim:head_dim + 1]).astype(o_ref.dtype)
        return carry

    lax.fori_loop(0, nq, one, 0)


def _causal_items(nq, n_near, pair_far):
    phases = [[(i, i - d) for i in range(d, nq)] for d in range(n_near)]
    pairs, singles = [], []
    for i in range(n_near, nq):
        n_far = i - n_near + 1
        n_paired = n_far - n_far % 2 if pair_far else 0
        pairs += [(i, j) for j in range(0, n_paired, 2)]
        singles += [(i, j) for j in range(n_paired, n_far)]
    phases += [pairs, singles]
    flat = [it for ph in phases for it in ph]
    tq = jnp.asarray([it[0] for it in flat], jnp.int32)
    tk = jnp.asarray([it[1] for it in flat], jnp.int32)
    return tq, tk, [len(ph) for ph in phases]


def _moba_kernel(tq_ref, tk_ref, qt_ref, k_ref, vt_ref, kmean_ref, bias_ref, o_ref,
                 keep_ref, s0_ref, s1_ref, t0_ref, t1_ref, m_ref, acc_ref, *, seq, phases):
    nblk = seq // MOBA_BLOCK

    def head_query(qi, h):
        pr, sub = divmod(h, 2)
        q_h = qt_ref[qi, pl.ds(pr * LANES + sub * A_HEAD_DIM, A_HEAD_DIM), :]
        return q_h, slice(h * A_HEAD_DIM, (h + 1) * A_HEAD_DIM)

    def select_blocks(i):
        row = lax.broadcasted_iota(jnp.int32, (nblk, ATTN_TILE), 0)
        rowf = row.astype(F32)
        past = row < i
        for h in range(HEAD_GROUP):
            q_h, lanes = head_query(i, h)
            gate = jnp.dot(kmean_ref[:, lanes].astype(BF16), q_h,
                           preferred_element_type=F32)
            gate = jnp.where(past, gate, -jnp.inf)
            picked = jnp.zeros((nblk, ATTN_TILE), F32)
            for _ in range(MOBA_TOPK):
                top = jnp.max(gate, axis=0, keepdims=True)
                first = jnp.min(jnp.where(gate == top, rowf, float(nblk)), axis=0, keepdims=True)
                pick = rowf == first
                picked = jnp.where(pick, 1.0, picked)
                gate = jnp.where(pick, -jnp.inf, gate)
            keep_ref[i, h] = jnp.where(past, picked, 0.0)

    def stage(offset, span, bias_cols, first):
        def score(n, buf):
            qi, kj = tq_ref[offset + n], tk_ref[offset + n]
            for h in range(HEAD_GROUP):
                q_h, lanes = head_query(qi, h)
                s_t = jnp.dot(k_ref[_tile_rows(kj, span), lanes], q_h,
                              preferred_element_type=F32)
                if bias_cols is not None:
                    s_t = s_t + bias_ref[h, :, bias_cols]
                _store_scores(buf, h, [s_t[t * MOBA_BLOCK:(t + 1) * MOBA_BLOCK]
                                       for t in range(span)])

        def attend(n, buf):
            qi, kj = tq_ref[offset + n], tk_ref[offset + n]
            for h in range(HEAD_GROUP):
                parts = [(MOBA_BLOCK, vt_ref[kj + t, h * A_HEAD_DIM:(h + 1) * A_HEAD_DIM, :],
                          None if first else keep_ref[qi, h, pl.ds(kj + t, 1), :])
                         for t in range(span)]
                _softmax_step(buf, h, parts, m_ref.at[qi, h], acc_ref.at[qi, h], first)
            if first:
                select_blocks(qi)

        return score, attend

    bufs = ((s0_ref, t0_ref), (s1_ref, t1_ref))
    n_own, n_prev, n_pairs, n_single = phases
    _pipeline(n_own, *stage(0, 1, slice(0, MOBA_BLOCK), True), bufs, PIPELINE_UNROLL)
    offset = n_own
    _pipeline(n_prev, *stage(offset, 1, slice(MOBA_BLOCK, 2 * MOBA_BLOCK), False), bufs,
              PIPELINE_UNROLL)
    offset += n_prev
    _pipeline(n_pairs, *stage(offset, 2, None, False), bufs, PIPELINE_UNROLL // 2)
    offset += n_pairs
    _pipeline(n_single, *stage(offset, 1, None, False), bufs, PIPELINE_UNROLL)
    _write_heads(acc_ref, o_ref, nblk, A_HEAD_DIM)


def _moba(qat, ka, vat, kmean, bias):
    b, seq, _ = ka.shape
    nq = seq // ATTN_TILE
    gw = HEAD_GROUP * A_HEAD_DIM
    tq, tk, phases = _causal_items(nq, 2, pair_far=False)
    per_group = lambda bi, g, *_: (bi, 0, g)
    tiles = lambda bi, g, *_: (bi, 0, g, 0)
    return pl.pallas_call(
        functools.partial(_moba_kernel, seq=seq, phases=phases),
        grid_spec=pltpu.PrefetchScalarGridSpec(
            num_scalar_prefetch=2,
            grid=(b, A_HEADS // HEAD_GROUP),
            in_specs=[
                pl.BlockSpec((None, nq, gw, ATTN_TILE), tiles),
                pl.BlockSpec((None, seq, gw), per_group),
                pl.BlockSpec((None, nq, gw, MOBA_BLOCK), tiles),
                pl.BlockSpec((None, nq, gw), per_group),
                pl.BlockSpec((HEAD_GROUP, MOBA_BLOCK, 2 * MOBA_BLOCK), lambda bi, g, *_: (g, 0, 0)),
            ],
            out_specs=pl.BlockSpec((None, nq, gw, ATTN_TILE), tiles),
            scratch_shapes=[
                pltpu.VMEM((nq, HEAD_GROUP, nq, ATTN_TILE), F32),
                pltpu.VMEM((HEAD_GROUP, ITEM_SPAN * ATTN_TILE, ATTN_TILE), F32),
                pltpu.VMEM((HEAD_GROUP, ITEM_SPAN * ATTN_TILE, ATTN_TILE), F32),
                pltpu.VMEM((HEAD_GROUP, ITEM_SPAN, 1, ATTN_TILE), F32),
                pltpu.VMEM((HEAD_GROUP, ITEM_SPAN, 1, ATTN_TILE), F32),
                pltpu.VMEM((nq, HEAD_GROUP, 1, ATTN_TILE), F32),
                pltpu.VMEM((nq, HEAD_GROUP, A_HEAD_DIM + SUM_ROWS, ATTN_TILE), F32),
            ]),
        out_shape=jax.ShapeDtypeStruct((b, nq, A_WIDTH, ATTN_TILE), BF16),
        compiler_params=pltpu.CompilerParams(
            dimension_semantics=("arbitrary", "arbitrary"), vmem_limit_bytes=VMEM_LIMIT),
        name="moba_attention",
    )(tq, tk, qat, ka, vat, kmean, bias)


def _mla_kernel(tq_ref, tk_ref, qt_ref, k_ref, vt_ref, o_ref, s0_ref, s1_ref, t0_ref, t1_ref,
                m_ref, acc_ref, *, nq, phases):
    def stage(offset, span, diagonal):
        def score(n, buf):
            qi, kj = tq_ref[offset + n], tk_ref[offset + n]
            for h in range(HEAD_GROUP):
                sl = slice(h * MLA_HEAD_PAD, h * MLA_HEAD_PAD + QK_NOPE + QK_ROPE)
                s_t = jnp.dot(k_ref[_tile_rows(kj, span), sl], qt_ref[qi, sl, :],
                              preferred_element_type=F32)
                if diagonal:
                    key = lax.broadcasted_iota(jnp.int32, (ATTN_TILE, ATTN_TILE), 0)
                    qry = lax.broadcasted_iota(jnp.int32, (ATTN_TILE, ATTN_TILE), 1)
                    s_t = jnp.where(key <= qry, s_t, NEG)
                _store_scores(buf, h, [s_t[t * ATTN_TILE:(t + 1) * ATTN_TILE]
                                       for t in range(span)])

        def attend(n, buf):
            qi, kj = tq_ref[offset + n], tk_ref[offset + n]
            for h in range(HEAD_GROUP):
                parts = [(ATTN_TILE, vt_ref[kj + t, h * V_HEAD:(h + 1) * V_HEAD, :], None)
                         for t in range(span)]
                _softmax_step(buf, h, parts, m_ref.at[qi, h], acc_ref.at[qi, h], diagonal)

        return score, attend

    bufs = ((s0_ref, t0_ref), (s1_ref, t1_ref))
    n_diag, n_pairs, n_single = phases
    _pipeline(n_diag, *stage(0, 1, True), bufs, PIPELINE_UNROLL)
    _pipeline(n_pairs, *stage(n_diag, 2, False), bufs, PIPELINE_UNROLL // 2)
    _pipeline(n_single, *stage(n_diag + n_pairs, 1, False), bufs, PIPELINE_UNROLL)
    _write_heads(acc_ref, o_ref, nq, V_HEAD)


def _mla(qmt, km, vbt):
    b, seq, _ = km.shape
    nq = seq // ATTN_TILE
    qw = HEAD_GROUP * MLA_HEAD_PAD
    vw = HEAD_GROUP * V_HEAD
    tq, tk, phases = _causal_items(nq, 1, pair_far=False)
    per_group = lambda bi, g, *_: (bi, 0, g)
    tiles = lambda bi, g, *_: (bi, 0, g, 0)
    return pl.pallas_call(
        functools.partial(_mla_kernel, nq=nq, phases=phases),
        grid_spec=pltpu.PrefetchScalarGridSpec(
            num_scalar_prefetch=2,
            grid=(b, B_HEADS // HEAD_GROUP),
            in_specs=[
                pl.BlockSpec((None, nq, qw, ATTN_TILE), tiles),
                pl.BlockSpec((None, seq, qw), per_group),
                pl.BlockSpec((None, nq, vw, ATTN_TILE), tiles),
            ],
            out_specs=pl.BlockSpec((None, nq, vw, ATTN_TILE), tiles),
            scratch_shapes=[
                pltpu.VMEM((HEAD_GROUP, ITEM_SPAN * ATTN_TILE, ATTN_TILE), F32),
                pltpu.VMEM((HEAD_GROUP, ITEM_SPAN * ATTN_TILE, ATTN_TILE), F32),
                pltpu.VMEM((HEAD_GROUP, ITEM_SPAN, 1, ATTN_TILE), F32),
                pltpu.VMEM((HEAD_GROUP, ITEM_SPAN, 1, ATTN_TILE), F32),
                pltpu.VMEM((nq, HEAD_GROUP, 1, ATTN_TILE), F32),
                pltpu.VMEM((nq, HEAD_GROUP, V_HEAD + SUM_ROWS, ATTN_TILE), F32),
            ]),
        out_shape=jax.ShapeDtypeStruct((b, nq, B_WIDTH, ATTN_TILE), BF16),
        compiler_params=pltpu.CompilerParams(
            dimension_semantics=("arbitrary", "arbitrary"), vmem_limit_bytes=VMEM_LIMIT),
        name="mla_attention",
    )(tq, tk, qmt, km, vbt)


def _out_kernel(x_ref, oa_ref, ob_ref, gattn_ref, wg_ref, wpa_ref, wpb_ref, wout_ref,
                gmlp_ref, wup_ref, wdn_ref, gfin_ref, o_ref):
    x = x_ref[...]
    n = _rms(x, gattn_ref[...]).astype(BF16)
    gates = lax.dot_general(n, wg_ref[...], NT_DIMS, preferred_element_type=F32)
    tn_dims = (((0,), (0,)), ((), ()))
    tiles = range(ROW_TILE // ATTN_TILE)
    pa = jnp.concatenate([lax.dot_general(oa_ref[t], wpa_ref[...], tn_dims,
                                          preferred_element_type=F32) for t in tiles], axis=0)
    pb = jnp.concatenate([lax.dot_general(ob_ref[t], wpb_ref[...], tn_dims,
                                          preferred_element_type=F32) for t in tiles], axis=0)
    merged = (jax.nn.sigmoid(gates[:, :D_MODEL]) * pa
              + jax.nn.sigmoid(gates[:, D_MODEL:]) * pb).astype(BF16)
    h = x + jnp.dot(merged, wout_ref[...], preferred_element_type=F32)
    m = _rms(h, gmlp_ref[...]).astype(BF16)
    for c in range(D_FF // FF_CHUNK):
        cols = slice(c * FF_CHUNK, (c + 1) * FF_CHUNK)
        up = jnp.dot(m, wup_ref[:, cols], preferred_element_type=F32)
        act = jnp.square(jnp.maximum(up, 0.0)).astype(BF16)
        h = h + jnp.dot(act, wdn_ref[cols, :], preferred_element_type=F32)
    o_ref[...] = _rms(h, gfin_ref[...])


def _out_block(x2, oa, ob, g_attn, wg, wpa, wpb, wout, g_mlp, wup, wdn, g_fin):
    n_rows = x2.shape[0]
    tm = ROW_TILE
    row = lambda i: (i, 0)
    return pl.pallas_call(
        _out_kernel,
        grid=(n_rows // tm,),
        in_specs=[
            pl.BlockSpec((tm, D_MODEL), row),
            pl.BlockSpec((tm // ATTN_TILE, A_WIDTH, ATTN_TILE), lambda i: (i, 0, 0)),
            pl.BlockSpec((tm // ATTN_TILE, B_WIDTH, ATTN_TILE), lambda i: (i, 0, 0)),
            _const_spec(g_attn.shape), _const_spec(wg.shape), _const_spec(wpa.shape),
            _const_spec(wpb.shape), _const_spec(wout.shape), _const_spec(g_mlp.shape),
            _const_spec(wup.shape), _const_spec(wdn.shape), _const_spec(g_fin.shape),
        ],
        out_specs=pl.BlockSpec((tm, D_MODEL), row),
        out_shape=jax.ShapeDtypeStruct((n_rows, D_MODEL), F32),
        compiler_params=pltpu.CompilerParams(
            dimension_semantics=("arbitrary",), vmem_limit_bytes=VMEM_LIMIT),
        name="out_mlp",
    )(x2, oa, ob, g_attn, wg, wpa, wpb, wout, g_mlp, wup, wdn, g_fin)


def _rope_tables(seq):
    half = QK_ROPE // 2
    inv_freq = ROPE_THETA ** (-jnp.arange(half, dtype=F32) / half)
    ang = jnp.arange(seq).astype(F32)[:, None] * inv_freq[None, :]
    cos2 = jnp.tile(jnp.cos(ang), (1, 2))
    sin2 = jnp.tile(jnp.sin(ang), (1, 2))
    pad = jnp.zeros((seq, MLA_HEAD_PAD - QK_NOPE - QK_ROPE), F32)
    cos_t = jnp.concatenate([jnp.ones((seq, QK_NOPE), F32), cos2, pad], axis=1)
    sin_t = jnp.concatenate([jnp.zeros((seq, QK_NOPE), F32), sin2, pad], axis=1)
    return cos_t, sin_t


def _pad_heads(w_nope, w_rope):
    k, h, _ = w_nope.shape
    if w_rope is None:
        w_rope = jnp.zeros((k, h, QK_ROPE), w_nope.dtype)
    pad = jnp.zeros((k, h, MLA_HEAD_PAD - QK_NOPE - QK_ROPE), w_nope.dtype)
    return jnp.concatenate([w_nope, w_rope, pad], axis=-1).reshape(k, h * MLA_HEAD_PAD)


def kernel(x, w_in, rel_bias, mla_q_norm, w_uq, mla_kv_norm, w_uk, w_uv, w_proj_a, w_proj_b,
           w_out, norm_attn, norm_mlp, w_mlp_up, w_mlp_down, norm_final):
    b, seq, d = x.shape
    assert d == D_MODEL and seq % ROW_TILE == 0 and seq // MOBA_BLOCK <= LANES
    assert w_in.shape[0] == 1, "single-layer block"
    x2 = x.reshape(b * seq, d)

    w_t = jnp.swapaxes(w_in[0], 0, 1)
    o_k = A_WIDTH
    o_v = 2 * A_WIDTH
    o_cq = 3 * A_WIDTH
    o_ckv = o_cq + Q_LORA
    o_kr = o_ckv + KV_LORA
    o_g = o_kr + QK_ROPE
    wk = w_t[o_k:o_v].astype(BF16)
    wqv = jnp.concatenate([w_t[:o_k], w_t[o_v:o_cq]], axis=0).astype(BF16)
    w_kr = w_t[o_kr:o_g]
    half = QK_ROPE // 2
    w_kr_rot = jnp.concatenate([-w_kr[half:], w_kr[:half]], axis=0)
    row_pad = lambda a: jnp.pad(a, ((QK_NOPE, MLA_HEAD_PAD - QK_NOPE - QK_ROPE), (0, 0)))
    wc = jnp.concatenate(
        [w_t[o_cq:o_kr], row_pad(w_kr), row_pad(w_kr_rot)], axis=0).astype(BF16)
    wg = w_t[o_g:].astype(BF16)

    uq = w_uq[0].reshape(Q_LORA, B_HEADS, QK_NOPE + QK_ROPE)
    uq_nope, uq_rope = uq[..., :QK_NOPE], uq[..., QK_NOPE:]
    wuq = _pad_heads(uq_nope, uq_rope).T.astype(BF16)
    wuk = _pad_heads(w_uk[0].reshape(KV_LORA, B_HEADS, QK_NOPE), None).astype(BF16)
    wuv = w_uv[0].T.astype(BF16)

    cos_t, sin_t = _rope_tables(seq)
    q_scale = (QK_NOPE + QK_ROPE) ** -0.5 * LOG2E
    row2 = lambda a: a.reshape(1, -1)

    qat, ka, vat, kmean, qmt, km, vbt = _inproj(
        x2, row2(norm_attn[0]), wk, wqv, wc, row2(mla_q_norm[0]), row2(mla_kv_norm[0]),
        wuq, wuk, wuv, (cos_t * q_scale).T, (sin_t * q_scale).T, cos_t, sin_t, b, seq)

    as_seq = lambda a: a.reshape(b, seq, a.shape[-1])
    bias = _bias_tiles(rel_bias)
    oa = _moba(qat, as_seq(ka), vat, kmean.reshape(b, seq // MOBA_BLOCK, A_WIDTH), bias)
    ob = _mla(qmt, as_seq(km), vbt)

    out = _out_block(
        x2, oa.reshape(-1, A_WIDTH, ATTN_TILE), ob.reshape(-1, B_WIDTH, ATTN_TILE),
        row2(norm_attn[0]),
        wg, w_proj_a[0].astype(BF16), w_proj_b[0].astype(BF16), w_out[0].astype(BF16),
        row2(norm_mlp[0]), w_mlp_up[0].astype(BF16), w_mlp_down[0].astype(BF16),
        row2(norm_final))
    return out.reshape(b, seq, d)
```

```python
import functools
import math

import jax
import jax.numpy as jnp
from jax import lax
from jax.experimental import pallas as pl
from jax.experimental.pallas import tpu as pltpu

D_MODEL = 1024
A_HEADS = 8
A_HEAD_DIM = 64
A_WIDTH = A_HEADS * A_HEAD_DIM
MOBA_BLOCK = 256
MOBA_TOPK = 3
REL_BUCKETS = 32
REL_MAX_DIST = 128
B_HEADS = 8
QK_NOPE = 64
QK_ROPE = 32
V_HEAD = 64
B_WIDTH = B_HEADS * V_HEAD
Q_LORA = 384
KV_LORA = 256
ROPE_THETA = 10000.0
D_FF = 4 * D_MODEL
EPS = 1e-6
NEG = -1e30
LOG2E = math.log2(math.e)

LANES = 128
MLA_HEAD_PAD = 128
ATTN_TILE = 256
HEAD_GROUP = 4
SUM_ROWS = 16
ITEM_SPAN = 2
PIPELINE_UNROLL = 24
ROW_TILE = 512
FF_CHUNK = 1024
VMEM_LIMIT = 56 * 1024 * 1024

F32 = jnp.float32
BF16 = jnp.bfloat16
NT_DIMS = (((1,), (1,)), ((), ()))


def _rms(xf, g):
    y = xf * lax.rsqrt(jnp.mean(xf * xf, axis=-1, keepdims=True) + EPS)
    return y * g


def _const_spec(shape):
    zeros = (0,) * len(shape)
    return pl.BlockSpec(shape, lambda *_: zeros, pipeline_mode=pl.Buffered(1))


def _bias_kernel(rel_ref, o_ref):
    h = pl.program_id(0)
    shape = (MOBA_BLOCK, 2 * MOBA_BLOCK)
    r = lax.broadcasted_iota(jnp.int32, shape, 0)
    c = lax.broadcasted_iota(jnp.int32, shape, 1)
    d = c - r
    dist = jnp.maximum(d, 0)
    max_exact = REL_BUCKETS // 2
    df = jnp.maximum(dist, 1).astype(F32)
    large = max_exact + (jnp.log(df / max_exact) / math.log(REL_MAX_DIST / max_exact)
                         * (REL_BUCKETS - max_exact)).astype(jnp.int32)
    large = jnp.minimum(large, REL_BUCKETS - 1)
    bucket = jnp.where(dist < max_exact, dist, large)
    val = jnp.zeros(shape, F32)
    for b in range(REL_BUCKETS):
        val = jnp.where(bucket == b, rel_ref[b, h], val)
    o_ref[...] = jnp.where(d >= 0, (val - rel_ref[REL_BUCKETS - 1, h]) * LOG2E, NEG)


def _bias_tiles(rel_bias):
    return pl.pallas_call(
        _bias_kernel,
        grid=(A_HEADS,),
        in_specs=[pl.BlockSpec(memory_space=pltpu.SMEM)],
        out_specs=pl.BlockSpec((None, MOBA_BLOCK, 2 * MOBA_BLOCK), lambda h: (h, 0, 0)),
        out_shape=jax.ShapeDtypeStruct((A_HEADS, MOBA_BLOCK, 2 * MOBA_BLOCK), F32),
        name="moba_bias_tiles",
    )(rel_bias)


def _inproj_kernel(x_ref, g_ref, wk_ref, wqv_ref, wc_ref, qn_ref, kvn_ref, wuq_ref,
                   wuk_ref, wuv_ref, cqt_ref, sqt_ref, ck_ref, sk_ref,
                   qat_ref, ka_ref, vat_ref, kmean_ref, qmt_ref, km_ref, vbt_ref):
    n = _rms(x_ref[...], g_ref[...]).astype(BF16)
    k = lax.dot_general(n, wk_ref[...], NT_DIMS, preferred_element_type=F32)
    ka_ref[...] = k.astype(BF16)
    nblk = ROW_TILE // ATTN_TILE
    kmean_ref[0] = jnp.sum(k.reshape(nblk, MOBA_BLOCK, A_WIDTH), axis=1) * (1.0 / MOBA_BLOCK)

    c = lax.dot_general(n, wc_ref[...], NT_DIMS, preferred_element_type=F32)
    cq = _rms(c[:, :Q_LORA], qn_ref[...]).astype(BF16)
    ckv = _rms(c[:, Q_LORA:Q_LORA + KV_LORA], kvn_ref[...]).astype(BF16)
    kr = c[:, Q_LORA + KV_LORA:Q_LORA + KV_LORA + LANES]
    krr = c[:, Q_LORA + KV_LORA + LANES:]
    for t in range(nblk):
        rows = slice(t * ATTN_TILE, (t + 1) * ATTN_TILE)
        qv_t = lax.dot_general(wqv_ref[...], n[rows], NT_DIMS,
                               preferred_element_type=F32)
        qat_ref[t] = (qv_t[:A_WIDTH] * (A_HEAD_DIM ** -0.5 * LOG2E)).astype(BF16)
        vat_ref[t] = qv_t[A_WIDTH:].astype(BF16)
        vbt_ref[t] = lax.dot_general(wuv_ref[...], ckv[rows], NT_DIMS,
                                     preferred_element_type=F32).astype(BF16)
        q_t = lax.dot_general(wuq_ref[...], cq[rows], NT_DIMS,
                              preferred_element_type=F32)
        cq_t, sq_t = cqt_ref[:, rows], sqt_ref[:, rows]
        half = QK_ROPE // 2
        for h in range(B_HEADS):
            q_h = q_t[h * MLA_HEAD_PAD:(h + 1) * MLA_HEAD_PAD]
            x1 = q_h[QK_NOPE:QK_NOPE + half]
            x2 = q_h[QK_NOPE + half:QK_NOPE + QK_ROPE]
            rot_h = jnp.concatenate(
                [jnp.zeros((QK_NOPE, ATTN_TILE), F32), -x2, x1,
                 jnp.zeros((MLA_HEAD_PAD - QK_NOPE - QK_ROPE, ATTN_TILE), F32)], axis=0)
            qmt_ref[t, h * MLA_HEAD_PAD:(h + 1) * MLA_HEAD_PAD, :] = (
                q_h * cq_t + rot_h * sq_t).astype(BF16)
    kn = jnp.dot(ckv, wuk_ref[...], preferred_element_type=F32)
    k_rope = kr * ck_ref[...] + krr * sk_ref[...]
    for h in range(B_HEADS):
        sl = slice(h * MLA_HEAD_PAD, (h + 1) * MLA_HEAD_PAD)
        km_ref[:, sl] = (kn[:, sl] + k_rope).astype(BF16)


def _inproj(x2, g_attn, wk, wqv, wc, qn, kvn, wuq, wuk, wuv, cq_tt, sq_tt, ck_t, sk_t,
            batch, seq):
    n_rows = x2.shape[0]
    tm = ROW_TILE
    steps = n_rows // tm
    per_seq = seq // tm
    nblk = tm // ATTN_TILE
    row = lambda i: (i, 0)
    tab = lambda i: (i % per_seq, 0)
    tab_t = lambda i: (0, i % per_seq)
    tile_map = lambda i: (i // per_seq, i % per_seq, 0, 0)
    bf = lambda w: jax.ShapeDtypeStruct((n_rows, w), BF16)
    tiles = lambda w: jax.ShapeDtypeStruct((batch, seq // ATTN_TILE, w, ATTN_TILE), BF16)
    tile_spec = lambda w: pl.BlockSpec((None, nblk, w, ATTN_TILE), tile_map)
    return pl.pallas_call(
        _inproj_kernel,
        grid=(steps,),
        in_specs=[
            pl.BlockSpec((tm, D_MODEL), row),
            _const_spec(g_attn.shape), _const_spec(wk.shape), _const_spec(wqv.shape),
            _const_spec(wc.shape), _const_spec(qn.shape), _const_spec(kvn.shape),
            _const_spec(wuq.shape), _const_spec(wuk.shape),
            _const_spec(wuv.shape),
            pl.BlockSpec((LANES, tm), tab_t), pl.BlockSpec((LANES, tm), tab_t),
            pl.BlockSpec((tm, LANES), tab), pl.BlockSpec((tm, LANES), tab),
        ],
        out_specs=[
            tile_spec(A_WIDTH), pl.BlockSpec((tm, A_WIDTH), row), tile_spec(A_WIDTH),
            pl.BlockSpec((1, nblk, A_WIDTH), lambda i: (i, 0, 0)),
            tile_spec(B_HEADS * MLA_HEAD_PAD),
            pl.BlockSpec((tm, B_HEADS * MLA_HEAD_PAD), row),
            tile_spec(B_WIDTH),
        ],
        out_shape=[
            tiles(A_WIDTH), bf(A_WIDTH), tiles(A_WIDTH),
            jax.ShapeDtypeStruct((steps, nblk, A_WIDTH), F32),
            tiles(B_HEADS * MLA_HEAD_PAD), bf(B_HEADS * MLA_HEAD_PAD), tiles(B_WIDTH),
        ],
        compiler_params=pltpu.CompilerParams(
            dimension_semantics=("arbitrary",), vmem_limit_bytes=VMEM_LIMIT),
        name="in_proj",
    )(x2, g_attn, wk, wqv, wc, qn, kvn, wuq, wuk, wuv, cq_tt, sq_tt, ck_t, sk_t)


def _store_scores(buf, h, parts):
    s_ref, max_ref = buf
    row = 0
    for idx, s_t in enumerate(parts):
        s_ref[h, row:row + s_t.shape[0]] = s_t
        max_ref[h, idx] = jnp.max(s_t, axis=0, keepdims=True)
        row += s_t.shape[0]


def _softmax_step(buf, h, parts, m_ref, acc_ref, first):
    s_ref, max_ref = buf
    maxes = [max_ref[h, idx] for idx in range(len(parts))]
    m_new = None if first else m_ref[...]
    for s_max, (_, _, col_keep) in zip(maxes, parts):
        if col_keep is not None:
            s_max = jnp.where(col_keep > 0, s_max, NEG)
        m_new = s_max if m_new is None else jnp.maximum(m_new, s_max)
    pv = None
    row = 0
    for s_max, (rows, v_t, col_keep) in zip(maxes, parts):
        ones = (lax.broadcasted_iota(jnp.int32, (SUM_ROWS, rows), 0) == 0).astype(BF16)
        v_aug = jnp.concatenate([v_t, ones], axis=0)
        m_exp = m_new if col_keep is None else jnp.maximum(m_new, s_max)
        p_t = jnp.exp2(s_ref[h, row:row + rows] - m_exp).astype(BF16)
        part = jnp.dot(v_aug, p_t, preferred_element_type=F32)
        if col_keep is not None:
            part = part * col_keep
        pv = part if pv is None else pv + part
        row += rows
    if first:
        acc_ref[...] = pv
    else:
        acc_ref[...] = jnp.exp2(m_ref[...] - m_new) * acc_ref[...] + pv
    m_ref[...] = m_new


def _pipeline(n_items, score, attend, bufs, unroll):
    if n_items == 0:
        return
    score(0, bufs[0])
    n_loops = (n_items - 1) // unroll

    def body(p, carry):
        for t in range(unroll):
            n = unroll * p + t
            score(n + 1, bufs[(t + 1) % 2])
            attend(n, bufs[t % 2])
        return carry

    lax.fori_loop(0, n_loops, body, 0)
    done = unroll * n_loops
    for n in range(done, n_items):
        if n + 1 < n_items:
            score(n + 1, bufs[(n + 1 - done) % 2])
        attend(n, bufs[(n - done) % 2])


def _tile_rows(j, span=1):
    return pl.ds(pl.multiple_of(j * ATTN_TILE, ATTN_TILE), span * ATTN_TILE)


def _write_heads(acc_ref, o_ref, nq, head_dim):
    def one(qi, carry):
        for h in range(HEAD_GROUP):
            acc = acc_ref[qi, h]
            o_ref[qi, h * head_dim:(h + 1) * head_dim, :] = (
                acc[:head_dim] / acc[head_dim:head_dim + 1]).astype(o_ref.dtype)
        return carry

    lax.fori_loop(0, nq, one, 0)


def _causal_items(nq, n_near, pair_far):
    phases = [[(i, i - d) for i in range(d, nq)] for d in range(n_near)]
    pairs, singles = [], []
    for i in range(n_near, nq):
        n_far = i - n_near + 1
        n_paired = n_far - n_far % 2 if pair_far else 0
        pairs += [(i, j) for j in range(0, n_paired, 2)]
        singles += [(i, j) for j in range(n_paired, n_far)]
    phases += [pairs, singles]
    flat = [it for ph in phases for it in ph]
    tq = jnp.asarray([it[0] for it in flat], jnp.int32)
    tk = jnp.asarray([it[1] for it in flat], jnp.int32)
    return tq, tk, [len(ph) for ph in phases]


def _moba_kernel(tq_ref, tk_ref, qt_ref, k_ref, vt_ref, kmean_ref, bias_ref, o_ref,
                 keep_ref, s0_ref, s1_ref, t0_ref, t1_ref, m_ref, acc_ref, *, seq, phases):
    nblk = seq // MOBA_BLOCK

    def head_query(qi, h):
        pr, sub = divmod(h, 2)
        q_h = qt_ref[qi, pl.ds(pr * LANES + sub * A_HEAD_DIM, A_HEAD_DIM), :]
        return q_h, slice(h * A_HEAD_DIM, (h + 1) * A_HEAD_DIM)

    def select_blocks(i):
        row = lax.broadcasted_iota(jnp.int32, (nblk, ATTN_TILE), 0)
        rowf = row.astype(F32)
        past = row < i
        for h in range(HEAD_GROUP):
            q_h, lanes = head_query(i, h)
            gate = jnp.dot(kmean_ref[:, lanes].astype(BF16), q_h,
                           preferred_element_type=F32)
            gate = jnp.where(past, gate, -jnp.inf)
            picked = jnp.zeros((nblk, ATTN_TILE), F32)
            for _ in range(MOBA_TOPK):
                top = jnp.max(gate, axis=0, keepdims=True)
                first = jnp.min(jnp.where(gate == top, rowf, float(nblk)), axis=0, keepdims=True)
                pick = rowf == first
                picked = jnp.where(pick, 1.0, picked)
                gate = jnp.where(pick, -jnp.inf, gate)
            keep_ref[i, h] = jnp.where(past, picked, 0.0)

    def stage(offset, span, bias_cols, first):
        def score(n, buf):
            qi, kj = tq_ref[offset + n], tk_ref[offset + n]
            for h in range(HEAD_GROUP):
                q_h, lanes = head_query(qi, h)
                s_t = jnp.dot(k_ref[_tile_rows(kj, span), lanes], q_h,
                              preferred_element_type=F32)
                if bias_cols is not None:
                    s_t = s_t + bias_ref[h, :, bias_cols]
                _store_scores(buf, h, [s_t[t * MOBA_BLOCK:(t + 1) * MOBA_BLOCK]
                                       for t in range(span)])

        def attend(n, buf):
            qi, kj = tq_ref[offset + n], tk_ref[offset + n]
            for h in range(HEAD_GROUP):
                parts = [(MOBA_BLOCK, vt_ref[kj + t, h * A_HEAD_DIM:(h + 1) * A_HEAD_DIM, :],
                          None if first else keep_ref[qi, h, pl.ds(kj + t, 1), :])
                         for t in range(span)]
                _softmax_step(buf, h, parts, m_ref.at[qi, h], acc_ref.at[qi, h], first)
            if first:
                select_blocks(qi)

        return score, attend

    bufs = ((s0_ref, t0_ref), (s1_ref, t1_ref))
    n_own, n_prev, n_pairs, n_single = phases
    _pipeline(n_own, *stage(0, 1, slice(0, MOBA_BLOCK), True), bufs, PIPELINE_UNROLL)
    offset = n_own
    _pipeline(n_prev, *stage(offset, 1, slice(MOBA_BLOCK, 2 * MOBA_BLOCK), False), bufs,
              PIPELINE_UNROLL)
    offset += n_prev
    _pipeline(n_pairs, *stage(offset, 2, None, False), bufs, PIPELINE_UNROLL // 2)
    offset += n_pairs
    _pipeline(n_single, *stage(offset, 1, None, False), bufs, PIPELINE_UNROLL)
    _write_heads(acc_ref, o_ref, nblk, A_HEAD_DIM)


def _moba(qat, ka, vat, kmean, bias):
    b, seq, _ = ka.shape
    nq = seq // ATTN_TILE
    gw = HEAD_GROUP * A_HEAD_DIM
    tq, tk, phases = _causal_items(nq, 2, pair_far=False)
    per_group = lambda bi, g, *_: (bi, 0, g)
    tiles = lambda bi, g, *_: (bi, 0, g, 0)
    return pl.pallas_call(
        functools.partial(_moba_kernel, seq=seq, phases=phases),
        grid_spec=pltpu.PrefetchScalarGridSpec(
            num_scalar_prefetch=2,
            grid=(b, A_HEADS // HEAD_GROUP),
            in_specs=[
                pl.BlockSpec((None, nq, gw, ATTN_TILE), tiles),
                pl.BlockSpec((None, seq, gw), per_group),
                pl.BlockSpec((None, nq, gw, MOBA_BLOCK), tiles),
                pl.BlockSpec((None, nq, gw), per_group),
                pl.BlockSpec((HEAD_GROUP, MOBA_BLOCK, 2 * MOBA_BLOCK), lambda bi, g, *_: (g, 0, 0)),
            ],
            out_specs=pl.BlockSpec((None, nq, gw, ATTN_TILE), tiles),
            scratch_shapes=[
                pltpu.VMEM((nq, HEAD_GROUP, nq, ATTN_TILE), F32),
                pltpu.VMEM((HEAD_GROUP, ITEM_SPAN * ATTN_TILE, ATTN_TILE), F32),
                pltpu.VMEM((HEAD_GROUP, ITEM_SPAN * ATTN_TILE, ATTN_TILE), F32),
                pltpu.VMEM((HEAD_GROUP, ITEM_SPAN, 1, ATTN_TILE), F32),
                pltpu.VMEM((HEAD_GROUP, ITEM_SPAN, 1, ATTN_TILE), F32),
                pltpu.VMEM((nq, HEAD_GROUP, 1, ATTN_TILE), F32),
                pltpu.VMEM((nq, HEAD_GROUP, A_HEAD_DIM + SUM_ROWS, ATTN_TILE), F32),
            ]),
        out_shape=jax.ShapeDtypeStruct((b, nq, A_WIDTH, ATTN_TILE), BF16),
        compiler_params=pltpu.CompilerParams(
            dimension_semantics=("arbitrary", "arbitrary"), vmem_limit_bytes=VMEM_LIMIT),
        name="moba_attention",
    )(tq, tk, qat, ka, vat, kmean, bias)


def _mla_kernel(tq_ref, tk_ref, qt_ref, k_ref, vt_ref, o_ref, s0_ref, s1_ref, t0_ref, t1_ref,
                m_ref, acc_ref, *, nq, phases):
    def stage(offset, span, diagonal):
        def score(n, buf):
            qi, kj = tq_ref[offset + n], tk_ref[offset + n]
            for h in range(HEAD_GROUP):
                sl = slice(h * MLA_HEAD_PAD, h * MLA_HEAD_PAD + QK_NOPE + QK_ROPE)
                s_t = jnp.dot(k_ref[_tile_rows(kj, span), sl], qt_ref[qi, sl, :],
                              preferred_element_type=F32)
                if diagonal:
                    key = lax.broadcasted_iota(jnp.int32, (ATTN_TILE, ATTN_TILE), 0)
                    qry = lax.broadcasted_iota(jnp.int32, (ATTN_TILE, ATTN_TILE), 1)
                    s_t = jnp.where(key <= qry, s_t, NEG)
                _store_scores(buf, h, [s_t[t * ATTN_TILE:(t + 1) * ATTN_TILE]
                                       for t in range(span)])

        def attend(n, buf):
            qi, kj = tq_ref[offset + n], tk_ref[offset + n]
            for h in range(HEAD_GROUP):
                parts = [(ATTN_TILE, vt_ref[kj + t, h * V_HEAD:(h + 1) * V_HEAD, :], None)
                         for t in range(span)]
                _softmax_step(buf, h, parts, m_ref.at[qi, h], acc_ref.at[qi, h], diagonal)

        return score, attend

    bufs = ((s0_ref, t0_ref), (s1_ref, t1_ref))
    n_diag, n_pairs, n_single = phases
    _pipeline(n_diag, *stage(0, 1, True), bufs, PIPELINE_UNROLL)
    _pipeline(n_pairs, *stage(n_diag, 2, False), bufs, PIPELINE_UNROLL // 2)
    _pipeline(n_single, *stage(n_diag + n_pairs, 1, False), bufs, PIPELINE_UNROLL)
    _write_heads(acc_ref, o_ref, nq, V_HEAD)


def _mla(qmt, km, vbt):
    b, seq, _ = km.shape
    nq = seq // ATTN_TILE
    qw = HEAD_GROUP * MLA_HEAD_PAD
    vw = HEAD_GROUP * V_HEAD
    tq, tk, phases = _causal_items(nq, 1, pair_far=False)
    per_group = lambda bi, g, *_: (bi, 0, g)
    tiles = lambda bi, g, *_: (bi, 0, g, 0)
    return pl.pallas_call(
        functools.partial(_mla_kernel, nq=nq, phases=phases),
        grid_spec=pltpu.PrefetchScalarGridSpec(
            num_scalar_prefetch=2,
            grid=(b, B_HEADS // HEAD_GROUP),
            in_specs=[
                pl.BlockSpec((None, nq, qw, ATTN_TILE), tiles),
                pl.BlockSpec((None, seq, qw), per_group),
                pl.BlockSpec((None, nq, vw, ATTN_TILE), tiles),
            ],
            out_specs=pl.BlockSpec((None, nq, vw, ATTN_TILE), tiles),
            scratch_shapes=[
                pltpu.VMEM((HEAD_GROUP, ITEM_SPAN * ATTN_TILE, ATTN_TILE), F32),
                pltpu.VMEM((HEAD_GROUP, ITEM_SPAN * ATTN_TILE, ATTN_TILE), F32),
                pltpu.VMEM((HEAD_GROUP, ITEM_SPAN, 1, ATTN_TILE), F32),
                pltpu.VMEM((HEAD_GROUP, ITEM_SPAN, 1, ATTN_TILE), F32),
                pltpu.VMEM((nq, HEAD_GROUP, 1, ATTN_TILE), F32),
                pltpu.VMEM((nq, HEAD_GROUP, V_HEAD + SUM_ROWS, ATTN_TILE), F32),
            ]),
        out_shape=jax.ShapeDtypeStruct((b, nq, B_WIDTH, ATTN_TILE), BF16),
        compiler_params=pltpu.CompilerParams(
            dimension_semantics=("arbitrary", "arbitrary"), vmem_limit_bytes=VMEM_LIMIT),
        name="mla_attention",
    )(tq, tk, qmt, km, vbt)


def _out_kernel(x_ref, oa_ref, ob_ref, gattn_ref, wg_ref, wpa_ref, wpb_ref, wout_ref,
                gmlp_ref, wup_ref, wdn_ref, gfin_ref, o_ref):
    x = x_ref[...]
    n = _rms(x, gattn_ref[...]).astype(BF16)
    gates = lax.dot_general(n, wg_ref[...], NT_DIMS, preferred_element_type=F32)
    tn_dims = (((0,), (0,)), ((), ()))
    tiles = range(ROW_TILE // ATTN_TILE)
    pa = jnp.concatenate([lax.dot_general(oa_ref[t], wpa_ref[...], tn_dims,
                                          preferred_element_type=F32) for t in tiles], axis=0)
    pb = jnp.concatenate([lax.dot_general(ob_ref[t], wpb_ref[...], tn_dims,
                                          preferred_element_type=F32) for t in tiles], axis=0)
    merged = (jax.nn.sigmoid(gates[:, :D_MODEL]) * pa
              + jax.nn.sigmoid(gates[:, D_MODEL:]) * pb).astype(BF16)
    h = x + jnp.dot(merged, wout_ref[...], preferred_element_type=F32)
    m = _rms(h, gmlp_ref[...]).astype(BF16)
    for c in range(D_FF // FF_CHUNK):
        cols = slice(c * FF_CHUNK, (c + 1) * FF_CHUNK)
        up = jnp.dot(m, wup_ref[:, cols], preferred_element_type=F32)
        act = jnp.square(jnp.maximum(up, 0.0)).astype(BF16)
        h = h + jnp.dot(act, wdn_ref[cols, :], preferred_element_type=F32)
    o_ref[...] = _rms(h, gfin_ref[...])


def _out_block(x2, oa, ob, g_attn, wg, wpa, wpb, wout, g_mlp, wup, wdn, g_fin):
    n_rows = x2.shape[0]
    tm = ROW_TILE
    row = lambda i: (i, 0)
    return pl.pallas_call(
        _out_kernel,
        grid=(n_rows // tm,),
        in_specs=[
            pl.BlockSpec((tm, D_MODEL), row),
            pl.BlockSpec((tm // ATTN_TILE, A_WIDTH, ATTN_TILE), lambda i: (i, 0, 0)),
            pl.BlockSpec((tm // ATTN_TILE, B_WIDTH, ATTN_TILE), lambda i: (i, 0, 0)),
            _const_spec(g_attn.shape), _const_spec(wg.shape), _const_spec(wpa.shape),
            _const_spec(wpb.shape), _const_spec(wout.shape), _const_spec(g_mlp.shape),
            _const_spec(wup.shape), _const_spec(wdn.shape), _const_spec(g_fin.shape),
        ],
        out_specs=pl.BlockSpec((tm, D_MODEL), row),
        out_shape=jax.ShapeDtypeStruct((n_rows, D_MODEL), F32),
        compiler_params=pltpu.CompilerParams(
            dimension_semantics=("arbitrary",), vmem_limit_bytes=VMEM_LIMIT),
        name="out_mlp",
    )(x2, oa, ob, g_attn, wg, wpa, wpb, wout, g_mlp, wup, wdn, g_fin)


def _rope_tables(seq):
    half = QK_ROPE // 2
    inv_freq = ROPE_THETA ** (-jnp.arange(half, dtype=F32) / half)
    ang = jnp.arange(seq).astype(F32)[:, None] * inv_freq[None, :]
    cos2 = jnp.tile(jnp.cos(ang), (1, 2))
    sin2 = jnp.tile(jnp.sin(ang), (1, 2))
    pad = jnp.zeros((seq, MLA_HEAD_PAD - QK_NOPE - QK_ROPE), F32)
    cos_t = jnp.concatenate([jnp.ones((seq, QK_NOPE), F32), cos2, pad], axis=1)
    sin_t = jnp.concatenate([jnp.zeros((seq, QK_NOPE), F32), sin2, pad], axis=1)
    return cos_t, sin_t


def _pad_heads(w_nope, w_rope):
    k, h, _ = w_nope.shape
    if w_rope is None:
        w_rope = jnp.zeros((k, h, QK_ROPE), w_nope.dtype)
    pad = jnp.zeros((k, h, MLA_HEAD_PAD - QK_NOPE - QK_ROPE), w_nope.dtype)
    return jnp.concatenate([w_nope, w_rope, pad], axis=-1).reshape(k, h * MLA_HEAD_PAD)


def kernel(x, w_in, rel_bias, mla_q_norm, w_uq, mla_kv_norm, w_uk, w_uv, w_proj_a, w_proj_b,
           w_out, norm_attn, norm_mlp, w_mlp_up, w_mlp_down, norm_final):
    b, seq, d = x.shape
    assert d == D_MODEL and seq % ROW_TILE == 0 and seq // MOBA_BLOCK <= LANES
    assert w_in.shape[0] == 1, "single-layer block"
    x2 = x.reshape(b * seq, d)

    w_t = jnp.swapaxes(w_in[0], 0, 1)
    o_k = A_WIDTH
    o_v = 2 * A_WIDTH
    o_cq = 3 * A_WIDTH
    o_ckv = o_cq + Q_LORA
    o_kr = o_ckv + KV_LORA
    o_g = o_kr + QK_ROPE
    wk = w_t[o_k:o_v].astype(BF16)
    wqv = jnp.concatenate([w_t[:o_k], w_t[o_v:o_cq]], axis=0).astype(BF16)
    w_kr = w_t[o_kr:o_g]
    half = QK_ROPE // 2
    w_kr_rot = jnp.concatenate([-w_kr[half:], w_kr[:half]], axis=0)
    row_pad = lambda a: jnp.pad(a, ((QK_NOPE, MLA_HEAD_PAD - QK_NOPE - QK_ROPE), (0, 0)))
    wc = jnp.concatenate(
        [w_t[o_cq:o_kr], row_pad(w_kr), row_pad(w_kr_rot)], axis=0).astype(BF16)
    wg = w_t[o_g:].astype(BF16)

    uq = w_uq[0].reshape(Q_LORA, B_HEADS, QK_NOPE + QK_ROPE)
    uq_nope, uq_rope = uq[..., :QK_NOPE], uq[..., QK_NOPE:]
    wuq = _pad_heads(uq_nope, uq_rope).T.astype(BF16)
    wuk = _pad_heads(w_uk[0].reshape(KV_LORA, B_HEADS, QK_NOPE), None).astype(BF16)
    wuv = w_uv[0].T.astype(BF16)

    cos_t, sin_t = _rope_tables(seq)
    q_scale = (QK_NOPE + QK_ROPE) ** -0.5 * LOG2E
    row2 = lambda a: a.reshape(1, -1)

    qat, ka, vat, kmean, qmt, km, vbt = _inproj(
        x2, row2(norm_attn[0]), wk, wqv, wc, row2(mla_q_norm[0]), row2(mla_kv_norm[0]),
        wuq, wuk, wuv, (cos_t * q_scale).T, (sin_t * q_scale).T, cos_t, sin_t, b, seq)

    as_seq = lambda a: a.reshape(b, seq, a.shape[-1])
    bias = _bias_tiles(rel_bias)
    oa = _moba(qat, as_seq(ka), vat, kmean.reshape(b, seq // MOBA_BLOCK, A_WIDTH), bias)
    ob = _mla(qmt, as_seq(km), vbt)

    out = _out_block(
        x2, oa.reshape(-1, A_WIDTH, ATTN_TILE), ob.reshape(-1, B_WIDTH, ATTN_TILE),
        row2(norm_attn[0]),
        wg, w_proj_a[0].astype(BF16), w_proj_b[0].astype(BF16), w_out[0].astype(BF16),
        row2(norm_mlp[0]), w_mlp_up[0].astype(BF16), w_mlp_down[0].astype(BF16),
        row2(norm_final))
    return out.reshape(b, seq, d)
```

```python
import functools
import math

import jax
import jax.numpy as jnp
from jax import lax
from jax.experimental import pallas as pl
from jax.experimental.pallas import tpu as pltpu

D_MODEL = 1024
A_HEADS = 8
A_HEAD_DIM = 64
A_WIDTH = A_HEADS * A_HEAD_DIM
MOBA_BLOCK = 256
MOBA_TOPK = 3
REL_BUCKETS = 32
REL_MAX_DIST = 128
B_HEADS = 8
QK_NOPE = 64
QK_ROPE = 32
V_HEAD = 64
B_WIDTH = B_HEADS * V_HEAD
Q_LORA = 384
KV_LORA = 256
ROPE_THETA = 10000.0
D_FF = 4 * D_MODEL
EPS = 1e-6
NEG = -1e30
LOG2E = math.log2(math.e)

LANES = 128
MLA_HEAD_PAD = 128
ATTN_TILE = 256
HEAD_GROUP = 4
SUM_ROWS = 16
ITEM_SPAN = 2
PIPELINE_UNROLL = 16
ROW_TILE = 512
FF_CHUNK = 1024
VMEM_LIMIT = 56 * 1024 * 1024

F32 = jnp.float32
BF16 = jnp.bfloat16
NT_DIMS = (((1,), (1,)), ((), ()))


def _rms(xf, g):
    y = xf * lax.rsqrt(jnp.mean(xf * xf, axis=-1, keepdims=True) + EPS)
    return y * g


def _const_spec(shape):
    zeros = (0,) * len(shape)
    return pl.BlockSpec(shape, lambda *_: zeros, pipeline_mode=pl.Buffered(1))


def _bias_kernel(rel_ref, o_ref):
    h = pl.program_id(0)
    shape = (MOBA_BLOCK, 2 * MOBA_BLOCK)
    r = lax.broadcasted_iota(jnp.int32, shape, 0)
    c = lax.broadcasted_iota(jnp.int32, shape, 1)
    d = c - r
    dist = jnp.maximum(d, 0)
    max_exact = REL_BUCKETS // 2
    df = jnp.maximum(dist, 1).astype(F32)
    large = max_exact + (jnp.log(df / max_exact) / math.log(REL_MAX_DIST / max_exact)
                         * (REL_BUCKETS - max_exact)).astype(jnp.int32)
    large = jnp.minimum(large, REL_BUCKETS - 1)
    bucket = jnp.where(dist < max_exact, dist, large)
    val = jnp.zeros(shape, F32)
    for b in range(REL_BUCKETS):
        val = jnp.where(bucket == b, rel_ref[b, h], val)
    o_ref[...] = jnp.where(d >= 0, (val - rel_ref[REL_BUCKETS - 1, h]) * LOG2E, NEG)


def _bias_tiles(rel_bias):
    return pl.pallas_call(
        _bias_kernel,
        grid=(A_HEADS,),
        in_specs=[pl.BlockSpec(memory_space=pltpu.SMEM)],
        out_specs=pl.BlockSpec((None, MOBA_BLOCK, 2 * MOBA_BLOCK), lambda h: (h, 0, 0)),
        out_shape=jax.ShapeDtypeStruct((A_HEADS, MOBA_BLOCK, 2 * MOBA_BLOCK), F32),
        name="moba_bias_tiles",
    )(rel_bias)


def _inproj_kernel(x_ref, g_ref, wk_ref, wqv_ref, wc_ref, qn_ref, kvn_ref, wuq_ref,
                   wuk_ref, wuv_ref, cqt_ref, sqt_ref, ck_ref, sk_ref,
                   qat_ref, ka_ref, vat_ref, kmean_ref, qmt_ref, km_ref, vbt_ref):
    n = _rms(x_ref[...], g_ref[...]).astype(BF16)
    k = lax.dot_general(n, wk_ref[...], NT_DIMS, preferred_element_type=F32)
    ka_ref[...] = k.astype(BF16)
    nblk = ROW_TILE // ATTN_TILE
    kmean_ref[0] = jnp.sum(k.reshape(nblk, MOBA_BLOCK, A_WIDTH), axis=1) * (1.0 / MOBA_BLOCK)

    c = lax.dot_general(n, wc_ref[...], NT_DIMS, preferred_element_type=F32)
    cq = _rms(c[:, :Q_LORA], qn_ref[...]).astype(BF16)
    ckv = _rms(c[:, Q_LORA:Q_LORA + KV_LORA], kvn_ref[...]).astype(BF16)
    kr = c[:, Q_LORA + KV_LORA:Q_LORA + KV_LORA + LANES]
    krr = c[:, Q_LORA + KV_LORA + LANES:]
    for t in range(nblk):
        rows = slice(t * ATTN_TILE, (t + 1) * ATTN_TILE)
        qv_t = lax.dot_general(wqv_ref[...], n[rows], NT_DIMS,
                               preferred_element_type=F32)
        qat_ref[t] = (qv_t[:A_WIDTH] * (A_HEAD_DIM ** -0.5 * LOG2E)).astype(BF16)
        vat_ref[t] = qv_t[A_WIDTH:].astype(BF16)
        vbt_ref[t] = lax.dot_general(wuv_ref[...], ckv[rows], NT_DIMS,
                                     preferred_element_type=F32).astype(BF16)
        q_t = lax.dot_general(wuq_ref[...], cq[rows], NT_DIMS,
                              preferred_element_type=F32)
        cq_t, sq_t = cqt_ref[:, rows], sqt_ref[:, rows]
        half = QK_ROPE // 2
        for h in range(B_HEADS):
            q_h = q_t[h * MLA_HEAD_PAD:(h + 1) * MLA_HEAD_PAD]
            x1 = q_h[QK_NOPE:QK_NOPE + half]
            x2 = q_h[QK_NOPE + half:QK_NOPE + QK_ROPE]
            rot_h = jnp.concatenate(
                [jnp.zeros((QK_NOPE, ATTN_TILE), F32), -x2, x1,
                 jnp.zeros((MLA_HEAD_PAD - QK_NOPE - QK_ROPE, ATTN_TILE), F32)], axis=0)
            qmt_ref[t, h * MLA_HEAD_PAD:(h + 1) * MLA_HEAD_PAD, :] = (
                q_h * cq_t + rot_h * sq_t).astype(BF16)
    kn = jnp.dot(ckv, wuk_ref[...], preferred_element_type=F32)
    k_rope = kr * ck_ref[...] + krr * sk_ref[...]
    for h in range(B_HEADS):
        sl = slice(h * MLA_HEAD_PAD, (h + 1) * MLA_HEAD_PAD)
        km_ref[:, sl] = (kn[:, sl] + k_rope).astype(BF16)


def _inproj(x2, g_attn, wk, wqv, wc, qn, kvn, wuq, wuk, wuv, cq_tt, sq_tt, ck_t, sk_t,
            batch, seq):
    n_rows = x2.shape[0]
    tm = ROW_TILE
    steps = n_rows // tm
    per_seq = seq // tm
    nblk = tm // ATTN_TILE
    row = lambda i: (i, 0)
    tab = lambda i: (i % per_seq, 0)
    tab_t = lambda i: (0, i % per_seq)
    tile_map = lambda i: (i // per_seq, i % per_seq, 0, 0)
    bf = lambda w: jax.ShapeDtypeStruct((n_rows, w), BF16)
    tiles = lambda w: jax.ShapeDtypeStruct((batch, seq // ATTN_TILE, w, ATTN_TILE), BF16)
    tile_spec = lambda w: pl.BlockSpec((None, nblk, w, ATTN_TILE), tile_map)
    return pl.pallas_call(
        _inproj_kernel,
        grid=(steps,),
        in_specs=[
            pl.BlockSpec((tm, D_MODEL), row),
            _const_spec(g_attn.shape), _const_spec(wk.shape), _const_spec(wqv.shape),
            _const_spec(wc.shape), _const_spec(qn.shape), _const_spec(kvn.shape),
            _const_spec(wuq.shape), _const_spec(wuk.shape),
            _const_spec(wuv.shape),
            pl.BlockSpec((LANES, tm), tab_t), pl.BlockSpec((LANES, tm), tab_t),
            pl.BlockSpec((tm, LANES), tab), pl.BlockSpec((tm, LANES), tab),
        ],
        out_specs=[
            tile_spec(A_WIDTH), pl.BlockSpec((tm, A_WIDTH), row), tile_spec(A_WIDTH),
            pl.BlockSpec((1, nblk, A_WIDTH), lambda i: (i, 0, 0)),
            tile_spec(B_HEADS * MLA_HEAD_PAD),
            pl.BlockSpec((tm, B_HEADS * MLA_HEAD_PAD), row),
            tile_spec(B_WIDTH),
        ],
        out_shape=[
            tiles(A_WIDTH), bf(A_WIDTH), tiles(A_WIDTH),
            jax.ShapeDtypeStruct((steps, nblk, A_WIDTH), F32),
            tiles(B_HEADS * MLA_HEAD_PAD), bf(B_HEADS * MLA_HEAD_PAD), tiles(B_WIDTH),
        ],
        compiler_params=pltpu.CompilerParams(
            dimension_semantics=("arbitrary",), vmem_limit_bytes=VMEM_LIMIT),
        name="in_proj",
    )(x2, g_attn, wk, wqv, wc, qn, kvn, wuq, wuk, wuv, cq_tt, sq_tt, ck_t, sk_t)


def _store_scores(buf, h, s_t):
    s_ref, max_ref = buf
    s_ref[h, :ATTN_TILE] = s_t
    max_ref[h, 0] = jnp.max(s_t, axis=0, keepdims=True)


def _softmax_step(buf, h, v_t, m_ref, acc_ref, first, col_keep=None):
    s_ref, max_ref = buf
    s_max = max_ref[h, 0]
    m_new = s_max if col_keep is None else jnp.where(col_keep > 0, s_max, NEG)
    if not first:
        m_new = jnp.maximum(m_ref[...], m_new)
    ones = (lax.broadcasted_iota(jnp.int32, (SUM_ROWS, ATTN_TILE), 0) == 0).astype(BF16)
    v_aug = jnp.concatenate([v_t, ones], axis=0)
    m_exp = m_new if col_keep is None else jnp.maximum(m_new, s_max)
    p_t = jnp.exp2(s_ref[h, :ATTN_TILE] - m_exp).astype(BF16)
    pv = jnp.dot(v_aug, p_t, preferred_element_type=F32)
    if col_keep is not None:
        pv = pv * col_keep
    if first:
        acc_ref[...] = pv
    else:
        acc_ref[...] = jnp.exp2(m_ref[...] - m_new) * acc_ref[...] + pv
    m_ref[...] = m_new


def _pipeline(n_items, score, attend, bufs):
    unroll = PIPELINE_UNROLL
    score(0, bufs[0])
    n_loops = (n_items - 1) // unroll

    def body(p, carry):
        for t in range(unroll):
            n = unroll * p + t
            score(n + 1, bufs[(t + 1) % 2])
            attend(n, bufs[t % 2])
        return carry

    lax.fori_loop(0, n_loops, body, 0)
    done = unroll * n_loops
    for n in range(done, n_items):
        if n + 1 < n_items:
            score(n + 1, bufs[(n + 1 - done) % 2])
        attend(n, bufs[(n - done) % 2])


def _tile_rows(j):
    return pl.ds(pl.multiple_of(j * ATTN_TILE, ATTN_TILE), ATTN_TILE)


def _write_heads(acc_ref, o_ref, nq, head_dim):
    def one(qi, carry):
        for h in range(HEAD_GROUP):
            acc = acc_ref[qi, h]
            o_ref[qi, h * head_dim:(h + 1) * head_dim, :] = (
                acc[:head_dim] / acc[head_dim:head_dim + 1]).astype(o_ref.dtype)
        return carry

    lax.fori_loop(0, nq, one, 0)


def _causal_items(nq, n_near):
    phases = [[(i, i - d) for i in range(d, nq)] for d in range(n_near)]
    phases.append([(i, j) for i in range(n_near, nq) for j in range(i - n_near + 1)])
    flat = [it for ph in phases for it in ph]
    tq = jnp.asarray([it[0] for it in flat], jnp.int32)
    tk = jnp.asarray([it[1] for it in flat], jnp.int32)
    return tq, tk, [len(ph) for ph in phases]


def _moba_kernel(tq_ref, tk_ref, qt_ref, k_ref, vt_ref, kmean_ref, bias_ref, o_ref,
                 keep_ref, s0_ref, s1_ref, t0_ref, t1_ref, m_ref, acc_ref, *, seq, phases):
    nblk = seq // MOBA_BLOCK

    def head_query(qi, h):
        dims = slice(h * A_HEAD_DIM, (h + 1) * A_HEAD_DIM)
        return qt_ref[qi, dims, :], dims

    def select_blocks(i):
        row = lax.broadcasted_iota(jnp.int32, (nblk, ATTN_TILE), 0)
        rowf = row.astype(F32)
        past = row < i
        for h in range(HEAD_GROUP):
            q_h, lanes = head_query(i, h)
            gate = jnp.dot(kmean_ref[:, lanes].astype(BF16), q_h,
                           preferred_element_type=F32)
            gate = jnp.where(past, gate, -jnp.inf)
            picked = jnp.zeros((nblk, ATTN_TILE), F32)
            for _ in range(MOBA_TOPK):
                top = jnp.max(gate, axis=0, keepdims=True)
                first = jnp.min(jnp.where(gate == top, rowf, float(nblk)), axis=0, keepdims=True)
                pick = rowf == first
                picked = jnp.where(pick, 1.0, picked)
                gate = jnp.where(pick, -jnp.inf, gate)
            keep_ref[i, h] = jnp.where(past, picked, 0.0)

    def stage(offset, bias_cols, first):
        def score(n, buf):
            qi, kj = tq_ref[offset + n], tk_ref[offset + n]
            for h in range(HEAD_GROUP):
                q_h, lanes = head_query(qi, h)
                s_t = jnp.dot(k_ref[_tile_rows(kj), lanes], q_h,
                              preferred_element_type=F32)
                if bias_cols is not None:
                    s_t = s_t + bias_ref[h, :, bias_cols]
                _store_scores(buf, h, s_t)

        def attend(n, buf):
            qi, kj = tq_ref[offset + n], tk_ref[offset + n]
            for h in range(HEAD_GROUP):
                col_keep = None if first else keep_ref[qi, h, pl.ds(kj, 1), :]
                _softmax_step(buf, h, vt_ref[kj, h * A_HEAD_DIM:(h + 1) * A_HEAD_DIM, :],
                              m_ref.at[qi, h], acc_ref.at[qi, h], first, col_keep)
            if first:
                select_blocks(qi)

        return score, attend

    bufs = ((s0_ref, t0_ref), (s1_ref, t1_ref))
    n_own, n_prev, n_far = phases
    _pipeline(n_own, *stage(0, slice(0, MOBA_BLOCK), True), bufs)
    _pipeline(n_prev, *stage(n_own, slice(MOBA_BLOCK, 2 * MOBA_BLOCK), False), bufs)
    _pipeline(n_far, *stage(n_own + n_prev, None, False), bufs)
    _write_heads(acc_ref, o_ref, nblk, A_HEAD_DIM)


def _moba(qat, ka, vat, kmean, bias):
    b, seq, _ = ka.shape
    nq = seq // ATTN_TILE
    gw = HEAD_GROUP * A_HEAD_DIM
    tq, tk, phases = _causal_items(nq, 2)
    per_group = lambda bi, g, *_: (bi, 0, g)
    tiles = lambda bi, g, *_: (bi, 0, g, 0)
    return pl.pallas_call(
        functools.partial(_moba_kernel, seq=seq, phases=phases),
        grid_spec=pltpu.PrefetchScalarGridSpec(
            num_scalar_prefetch=2,
            grid=(b, A_HEADS // HEAD_GROUP),
            in_specs=[
                pl.BlockSpec((None, nq, gw, ATTN_TILE), tiles),
                pl.BlockSpec((None, seq, gw), per_group),
                pl.BlockSpec((None, nq, gw, MOBA_BLOCK), tiles),
                pl.BlockSpec((None, nq, gw), per_group),
                pl.BlockSpec((HEAD_GROUP, MOBA_BLOCK, 2 * MOBA_BLOCK), lambda bi, g, *_: (g, 0, 0)),
            ],
            out_specs=pl.BlockSpec((None, nq, gw, ATTN_TILE), tiles),
            scratch_shapes=[
                pltpu.VMEM((nq, HEAD_GROUP, nq, ATTN_TILE), F32),
                pltpu.VMEM((HEAD_GROUP, ATTN_TILE, ATTN_TILE), F32),
                pltpu.VMEM((HEAD_GROUP, ATTN_TILE, ATTN_TILE), F32),
                pltpu.VMEM((HEAD_GROUP, ITEM_SPAN, 1, ATTN_TILE), F32),
                pltpu.VMEM((HEAD_GROUP, ITEM_SPAN, 1, ATTN_TILE), F32),
                pltpu.VMEM((nq, HEAD_GROUP, 1, ATTN_TILE), F32),
                pltpu.VMEM((nq, HEAD_GROUP, A_HEAD_DIM + SUM_ROWS, ATTN_TILE), F32),
            ]),
        out_shape=jax.ShapeDtypeStruct((b, nq, A_WIDTH, ATTN_TILE), BF16),
        compiler_params=pltpu.CompilerParams(
            dimension_semantics=("arbitrary", "arbitrary"), vmem_limit_bytes=VMEM_LIMIT),
        name="moba_attention",
    )(tq, tk, qat, ka, vat, kmean, bias)


def _mla_kernel(tq_ref, tk_ref, qt_ref, k_ref, vt_ref, o_ref, s0_ref, s1_ref, t0_ref, t1_ref,
                m_ref, acc_ref, *, nq, phases):
    def stage(offset, diagonal):
        def score(n, buf):
            qi, kj = tq_ref[offset + n], tk_ref[offset + n]
            for h in range(HEAD_GROUP):
                sl = slice(h * MLA_HEAD_PAD, h * MLA_HEAD_PAD + QK_NOPE + QK_ROPE)
                s_t = jnp.dot(k_ref[_tile_rows(kj), sl], qt_ref[qi, sl, :],
                              preferred_element_type=F32)
                if diagonal:
                    key = lax.broadcasted_iota(jnp.int32, (ATTN_TILE, ATTN_TILE), 0)
                    qry = lax.broadcasted_iota(jnp.int32, (ATTN_TILE, ATTN_TILE), 1)
                    s_t = jnp.where(key <= qry, s_t, NEG)
                _store_scores(buf, h, s_t)

        def attend(n, buf):
            qi, kj = tq_ref[offset + n], tk_ref[offset + n]
            for h in range(HEAD_GROUP):
                _softmax_step(buf, h, vt_ref[kj, h * V_HEAD:(h + 1) * V_HEAD, :],
                              m_ref.at[qi, h], acc_ref.at[qi, h], diagonal)

        return score, attend

    bufs = ((s0_ref, t0_ref), (s1_ref, t1_ref))
    n_diag, n_past = phases
    _pipeline(n_diag, *stage(0, True), bufs)
    _pipeline(n_past, *stage(n_diag, False), bufs)
    _write_heads(acc_ref, o_ref, nq, V_HEAD)


def _mla(qmt, km, vbt):
    b, seq, _ = km.shape
    nq = seq // ATTN_TILE
    qw = HEAD_GROUP * MLA_HEAD_PAD
    vw = HEAD_GROUP * V_HEAD
    tq, tk, phases = _causal_items(nq, 1)
    per_group = lambda bi, g, *_: (bi, 0, g)
    tiles = lambda bi, g, *_: (bi, 0, g, 0)
    return pl.pallas_call(
        functools.partial(_mla_kernel, nq=nq, phases=phases),
        grid_spec=pltpu.PrefetchScalarGridSpec(
            num_scalar_prefetch=2,
            grid=(b, B_HEADS // HEAD_GROUP),
            in_specs=[
                pl.BlockSpec((None, nq, qw, ATTN_TILE), tiles),
                pl.BlockSpec((None, seq, qw), per_group),
                pl.BlockSpec((None, nq, vw, ATTN_TILE), tiles),
            ],
            out_specs=pl.BlockSpec((None, nq, vw, ATTN_TILE), tiles),
            scratch_shapes=[
                pltpu.VMEM((HEAD_GROUP, ITEM_SPAN * ATTN_TILE, ATTN_TILE), F32),
                pltpu.VMEM((HEAD_GROUP, ITEM_SPAN * ATTN_TILE, ATTN_TILE), F32),
                pltpu.VMEM((HEAD_GROUP, ITEM_SPAN, 1, ATTN_TILE), F32),
                pltpu.VMEM((HEAD_GROUP, ITEM_SPAN, 1, ATTN_TILE), F32),
                pltpu.VMEM((nq, HEAD_GROUP, 1, ATTN_TILE), F32),
                pltpu.VMEM((nq, HEAD_GROUP, V_HEAD + SUM_ROWS, ATTN_TILE), F32),
            ]),
        out_shape=jax.ShapeDtypeStruct((b, nq, B_WIDTH, ATTN_TILE), BF16),
        compiler_params=pltpu.CompilerParams(
            dimension_semantics=("arbitrary", "arbitrary"), vmem_limit_bytes=VMEM_LIMIT),
        name="mla_attention",
    )(tq, tk, qmt, km, vbt)


def _out_kernel(x_ref, oa_ref, ob_ref, gattn_ref, wg_ref, wpa_ref, wpb_ref, wout_ref,
                gmlp_ref, wup_ref, wdn_ref, gfin_ref, o_ref):
    x = x_ref[...]
    n = _rms(x, gattn_ref[...]).astype(BF16)
    gates = lax.dot_general(n, wg_ref[...], NT_DIMS, preferred_element_type=F32)
    tn_dims = (((0,), (0,)), ((), ()))
    tiles = range(ROW_TILE // ATTN_TILE)
    pa = jnp.concatenate([lax.dot_general(oa_ref[t], wpa_ref[...], tn_dims,
                                          preferred_element_type=F32) for t in tiles], axis=0)
    pb = jnp.concatenate([lax.dot_general(ob_ref[t], wpb_ref[...], tn_dims,
                                          preferred_element_type=F32) for t in tiles], axis=0)
    merged = (jax.nn.sigmoid(gates[:, :D_MODEL]) * pa
              + jax.nn.sigmoid(gates[:, D_MODEL:]) * pb).astype(BF16)
    h = x + jnp.dot(merged, wout_ref[...], preferred_element_type=F32)
    m = _rms(h, gmlp_ref[...]).astype(BF16)
    for c in range(D_FF // FF_CHUNK):
        cols = slice(c * FF_CHUNK, (c + 1) * FF_CHUNK)
        up = jnp.dot(m, wup_ref[:, cols], preferred_element_type=F32)
        act = jnp.square(jnp.maximum(up, 0.0)).astype(BF16)
        h = h + jnp.dot(act, wdn_ref[cols, :], preferred_element_type=F32)
    o_ref[...] = _rms(h, gfin_ref[...])


def _out_block(x2, oa, ob, g_attn, wg, wpa, wpb, wout, g_mlp, wup, wdn, g_fin):
    n_rows = x2.shape[0]
    tm = ROW_TILE
    row = lambda i: (i, 0)
    return pl.pallas_call(
        _out_kernel,
        grid=(n_rows // tm,),
        in_specs=[
            pl.BlockSpec((tm, D_MODEL), row),
            pl.BlockSpec((tm // ATTN_TILE, A_WIDTH, ATTN_TILE), lambda i: (i, 0, 0)),
            pl.BlockSpec((tm // ATTN_TILE, B_WIDTH, ATTN_TILE), lambda i: (i, 0, 0)),
            _const_spec(g_attn.shape), _const_spec(wg.shape), _const_spec(wpa.shape),
            _const_spec(wpb.shape), _const_spec(wout.shape), _const_spec(g_mlp.shape),
            _const_spec(wup.shape), _const_spec(wdn.shape), _const_spec(g_fin.shape),
        ],
        out_specs=pl.BlockSpec((tm, D_MODEL), row),
        out_shape=jax.ShapeDtypeStruct((n_rows, D_MODEL), F32),
        compiler_params=pltpu.CompilerParams(
            dimension_semantics=("arbitrary",), vmem_limit_bytes=VMEM_LIMIT),
        name="out_mlp",
    )(x2, oa, ob, g_attn, wg, wpa, wpb, wout, g_mlp, wup, wdn, g_fin)


def _rope_tables(seq):
    half = QK_ROPE // 2
    inv_freq = ROPE_THETA ** (-jnp.arange(half, dtype=F32) / half)
    ang = jnp.arange(seq).astype(F32)[:, None] * inv_freq[None, :]
    cos2 = jnp.tile(jnp.cos(ang), (1, 2))
    sin2 = jnp.tile(jnp.sin(ang), (1, 2))
    pad = jnp.zeros((seq, MLA_HEAD_PAD - QK_NOPE - QK_ROPE), F32)
    cos_t = jnp.concatenate([jnp.ones((seq, QK_NOPE), F32), cos2, pad], axis=1)
    sin_t = jnp.concatenate([jnp.zeros((seq, QK_NOPE), F32), sin2, pad], axis=1)
    return cos_t, sin_t


def _pad_heads(w_nope, w_rope):
    k, h, _ = w_nope.shape
    if w_rope is None:
        w_rope = jnp.zeros((k, h, QK_ROPE), w_nope.dtype)
    pad = jnp.zeros((k, h, MLA_HEAD_PAD - QK_NOPE - QK_ROPE), w_nope.dtype)
    return jnp.concatenate([w_nope, w_rope, pad], axis=-1).reshape(k, h * MLA_HEAD_PAD)


def kernel(x, w_in, rel_bias, mla_q_norm, w_uq, mla_kv_norm, w_uk, w_uv, w_proj_a, w_proj_b,
           w_out, norm_attn, norm_mlp, w_mlp_up, w_mlp_down, norm_final):
    b, seq, d = x.shape
    assert d == D_MODEL and seq % ROW_TILE == 0
    assert w_in.shape[0] == 1, "single-layer block"
    x2 = x.reshape(b * seq, d)

    w_t = jnp.swapaxes(w_in[0], 0, 1)
    o_k = A_WIDTH
    o_v = 2 * A_WIDTH
    o_cq = 3 * A_WIDTH
    o_ckv = o_cq + Q_LORA
    o_kr = o_ckv + KV_LORA
    o_g = o_kr + QK_ROPE
    wk = w_t[o_k:o_v].astype(BF16)
    wqv = jnp.concatenate([w_t[:o_k], w_t[o_v:o_cq]], axis=0).astype(BF16)
    w_kr = w_t[o_kr:o_g]
    half = QK_ROPE // 2
    w_kr_rot = jnp.concatenate([-w_kr[half:], w_kr[:half]], axis=0)
    row_pad = lambda a: jnp.pad(a, ((QK_NOPE, MLA_HEAD_PAD - QK_NOPE - QK_ROPE), (0, 0)))
    wc = jnp.concatenate(
        [w_t[o_cq:o_kr], row_pad(w_kr), row_pad(w_kr_rot)], axis=0).astype(BF16)
    wg = w_t[o_g:].astype(BF16)

    uq = w_uq[0].reshape(Q_LORA, B_HEADS, QK_NOPE + QK_ROPE)
    uq_nope, uq_rope = uq[..., :QK_NOPE], uq[..., QK_NOPE:]
    wuq = _pad_heads(uq_nope, uq_rope).T.astype(BF16)
    wuk = _pad_heads(w_uk[0].reshape(KV_LORA, B_HEADS, QK_NOPE), None).astype(BF16)
    wuv = w_uv[0].T.astype(BF16)

    cos_t, sin_t = _rope_tables(seq)
    q_scale = (QK_NOPE + QK_ROPE) ** -0.5 * LOG2E
    row2 = lambda a: a.reshape(1, -1)

    qat, ka, vat, kmean, qmt, km, vbt = _inproj(
        x2, row2(norm_attn[0]), wk, wqv, wc, row2(mla_q_norm[0]), row2(mla_kv_norm[0]),
        wuq, wuk, wuv, (cos_t * q_scale).T, (sin_t * q_scale).T, cos_t, sin_t, b, seq)

    as_seq = lambda a: a.reshape(b, seq, a.shape[-1])
    bias = _bias_tiles(rel_bias)
    oa = _moba(qat, as_seq(ka), vat, kmean.reshape(b, seq // MOBA_BLOCK, A_WIDTH), bias)
    ob = _mla(qmt, as_seq(km), vbt)

    out = _out_block(
        x2, oa.reshape(-1, A_WIDTH, ATTN_TILE), ob.reshape(-1, B_WIDTH, ATTN_TILE),
        row2(norm_attn[0]),
        wg, w_proj_a[0].astype(BF16), w_proj_b[0].astype(BF16), w_out[0].astype(BF16),
        row2(norm_mlp[0]), w_mlp_up[0].astype(BF16), w_mlp_down[0].astype(BF16),
        row2(norm_final))
    return out.reshape(b, seq, d)
```

```python
import functools
import math

import jax
import jax.numpy as jnp
from jax import lax
from jax.experimental import pallas as pl
from jax.experimental.pallas import tpu as pltpu

D_MODEL = 1024
A_HEADS = 8
A_HEAD_DIM = 64
A_WIDTH = A_HEADS * A_HEAD_DIM
MOBA_BLOCK = 256
MOBA_TOPK = 3
REL_BUCKETS = 32
REL_MAX_DIST = 128
B_HEADS = 8
QK_NOPE = 64
QK_ROPE = 32
V_HEAD = 64
B_WIDTH = B_HEADS * V_HEAD
Q_LORA = 384
KV_LORA = 256
ROPE_THETA = 10000.0
D_FF = 4 * D_MODEL
EPS = 1e-6
NEG = -1e30
LOG2E = math.log2(math.e)

LANES = 128
MLA_HEAD_PAD = 128
ATTN_TILE = 256
HEAD_GROUP = 4
SUM_ROWS = 16
ITEM_SPAN = 2
PIPELINE_UNROLL = 16
IN_ROW_TILE = 1024
ROW_TILE = 512
FF_CHUNK = 1024
VMEM_LIMIT = 56 * 1024 * 1024

F32 = jnp.float32
BF16 = jnp.bfloat16
NT_DIMS = (((1,), (1,)), ((), ()))


def _rms(xf, g):
    y = xf * lax.rsqrt(jnp.mean(xf * xf, axis=-1, keepdims=True) + EPS)
    return y * g


def _const_spec(shape):
    zeros = (0,) * len(shape)
    return pl.BlockSpec(shape, lambda *_: zeros, pipeline_mode=pl.Buffered(1))


def _bias_kernel(rel_ref, o_ref):
    h = pl.program_id(0)
    shape = (MOBA_BLOCK, 2 * MOBA_BLOCK)
    r = lax.broadcasted_iota(jnp.int32, shape, 0)
    c = lax.broadcasted_iota(jnp.int32, shape, 1)
    d = c - r
    dist = jnp.maximum(d, 0)
    max_exact = REL_BUCKETS // 2
    df = jnp.maximum(dist, 1).astype(F32)
    large = max_exact + (jnp.log(df / max_exact) / math.log(REL_MAX_DIST / max_exact)
                         * (REL_BUCKETS - max_exact)).astype(jnp.int32)
    large = jnp.minimum(large, REL_BUCKETS - 1)
    bucket = jnp.where(dist < max_exact, dist, large)
    val = jnp.zeros(shape, F32)
    for b in range(REL_BUCKETS):
        val = jnp.where(bucket == b, rel_ref[b, h], val)
    o_ref[...] = jnp.where(d >= 0, (val - rel_ref[REL_BUCKETS - 1, h]) * LOG2E, NEG)


def _bias_tiles(rel_bias):
    return pl.pallas_call(
        _bias_kernel,
        grid=(A_HEADS,),
        in_specs=[pl.BlockSpec(memory_space=pltpu.SMEM)],
        out_specs=pl.BlockSpec((None, MOBA_BLOCK, 2 * MOBA_BLOCK), lambda h: (h, 0, 0)),
        out_shape=jax.ShapeDtypeStruct((A_HEADS, MOBA_BLOCK, 2 * MOBA_BLOCK), F32),
        name="moba_bias_tiles",
    )(rel_bias)


def _inproj_kernel(x_ref, g_ref, wk_ref, wqv_ref, wc_ref, qn_ref, kvn_ref, wuq_ref,
                   wuk_ref, wuv_ref, cqt_ref, sqt_ref, ck_ref, sk_ref,
                   qat_ref, ka_ref, vat_ref, kmean_ref, qmt_ref, km_ref, vbt_ref):
    n = _rms(x_ref[...], g_ref[...]).astype(BF16)
    k = lax.dot_general(n, wk_ref[...], NT_DIMS, preferred_element_type=F32)
    ka_ref[...] = k.astype(BF16)
    nblk = IN_ROW_TILE // ATTN_TILE
    kmean_ref[0] = jnp.sum(k.reshape(nblk, MOBA_BLOCK, A_WIDTH), axis=1) * (1.0 / MOBA_BLOCK)

    c = lax.dot_general(n, wc_ref[...], NT_DIMS, preferred_element_type=F32)
    cq = _rms(c[:, :Q_LORA], qn_ref[...]).astype(BF16)
    ckv = _rms(c[:, Q_LORA:Q_LORA + KV_LORA], kvn_ref[...]).astype(BF16)
    kr = c[:, Q_LORA + KV_LORA:Q_LORA + KV_LORA + LANES]
    krr = c[:, Q_LORA + KV_LORA + LANES:]
    for t in range(nblk):
        rows = slice(t * ATTN_TILE, (t + 1) * ATTN_TILE)
        qv_t = lax.dot_general(wqv_ref[...], n[rows], NT_DIMS,
                               preferred_element_type=F32)
        qat_ref[t] = (qv_t[:A_WIDTH] * (A_HEAD_DIM ** -0.5 * LOG2E)).astype(BF16)
        vat_ref[t] = qv_t[A_WIDTH:].astype(BF16)
        vbt_ref[t] = lax.dot_general(wuv_ref[...], ckv[rows], NT_DIMS,
                                     preferred_element_type=F32).astype(BF16)
        q_t = lax.dot_general(wuq_ref[...], cq[rows], NT_DIMS,
                              preferred_element_type=F32)
        cq_t, sq_t = cqt_ref[:, rows], sqt_ref[:, rows]
        half = QK_ROPE // 2
        for h in range(B_HEADS):
            q_h = q_t[h * MLA_HEAD_PAD:(h + 1) * MLA_HEAD_PAD]
            x1 = q_h[QK_NOPE:QK_NOPE + half]
            x2 = q_h[QK_NOPE + half:QK_NOPE + QK_ROPE]
            rot_h = jnp.concatenate(
                [jnp.zeros((QK_NOPE, ATTN_TILE), F32), -x2, x1,
                 jnp.zeros((MLA_HEAD_PAD - QK_NOPE - QK_ROPE, ATTN_TILE), F32)], axis=0)
            qmt_ref[t, h * MLA_HEAD_PAD:(h + 1) * MLA_HEAD_PAD, :] = (
                q_h * cq_t + rot_h * sq_t).astype(BF16)
    kn = jnp.dot(ckv, wuk_ref[...], preferred_element_type=F32)
    k_rope = kr * ck_ref[...] + krr * sk_ref[...]
    for h in range(B_HEADS):
        sl = slice(h * MLA_HEAD_PAD, (h + 1) * MLA_HEAD_PAD)
        km_ref[:, sl] = (kn[:, sl] + k_rope).astype(BF16)


def _inproj(x2, g_attn, wk, wqv, wc, qn, kvn, wuq, wuk, wuv, cq_tt, sq_tt, ck_t, sk_t,
            batch, seq):
    n_rows = x2.shape[0]
    tm = IN_ROW_TILE
    steps = n_rows // tm
    per_seq = seq // tm
    nblk = tm // ATTN_TILE
    row = lambda i: (i, 0)
    tab = lambda i: (i % per_seq, 0)
    tab_t = lambda i: (0, i % per_seq)
    tile_map = lambda i: (i // per_seq, i % per_seq, 0, 0)
    bf = lambda w: jax.ShapeDtypeStruct((n_rows, w), BF16)
    tiles = lambda w: jax.ShapeDtypeStruct((batch, seq // ATTN_TILE, w, ATTN_TILE), BF16)
    tile_spec = lambda w: pl.BlockSpec((None, nblk, w, ATTN_TILE), tile_map)
    return pl.pallas_call(
        _inproj_kernel,
        grid=(steps,),
        in_specs=[
            pl.BlockSpec((tm, D_MODEL), row),
            _const_spec(g_attn.shape), _const_spec(wk.shape), _const_spec(wqv.shape),
            _const_spec(wc.shape), _const_spec(qn.shape), _const_spec(kvn.shape),
            _const_spec(wuq.shape), _const_spec(wuk.shape),
            _const_spec(wuv.shape),
            pl.BlockSpec((LANES, tm), tab_t), pl.BlockSpec((LANES, tm), tab_t),
            pl.BlockSpec((tm, LANES), tab), pl.BlockSpec((tm, LANES), tab),
        ],
        out_specs=[
            tile_spec(A_WIDTH), pl.BlockSpec((tm, A_WIDTH), row), tile_spec(A_WIDTH),
            pl.BlockSpec((1, nblk, A_WIDTH), lambda i: (i, 0, 0)),
            tile_spec(B_HEADS * MLA_HEAD_PAD),
            pl.BlockSpec((tm, B_HEADS * MLA_HEAD_PAD), row),
            tile_spec(B_WIDTH),
        ],
        out_shape=[
            tiles(A_WIDTH), bf(A_WIDTH), tiles(A_WIDTH),
            jax.ShapeDtypeStruct((steps, nblk, A_WIDTH), F32),
            tiles(B_HEADS * MLA_HEAD_PAD), bf(B_HEADS * MLA_HEAD_PAD), tiles(B_WIDTH),
        ],
        compiler_params=pltpu.CompilerParams(
            dimension_semantics=("arbitrary",), vmem_limit_bytes=VMEM_LIMIT),
        name="in_proj",
    )(x2, g_attn, wk, wqv, wc, qn, kvn, wuq, wuk, wuv, cq_tt, sq_tt, ck_t, sk_t)


def _store_scores(buf, h, s_t):
    s_ref, max_ref = buf
    s_ref[h, :ATTN_TILE] = s_t
    max_ref[h, 0] = jnp.max(s_t, axis=0, keepdims=True)


def _softmax_step(buf, h, v_t, m_ref, acc_ref, first, col_keep=None):
    s_ref, max_ref = buf
    s_max = max_ref[h, 0]
    m_new = s_max if col_keep is None else jnp.where(col_keep > 0, s_max, NEG)
    if not first:
        m_new = jnp.maximum(m_ref[...], m_new)
    ones = (lax.broadcasted_iota(jnp.int32, (SUM_ROWS, ATTN_TILE), 0) == 0).astype(BF16)
    v_aug = jnp.concatenate([v_t, ones], axis=0)
    m_exp = m_new if col_keep is None else jnp.maximum(m_new, s_max)
    p_t = jnp.exp2(s_ref[h, :ATTN_TILE] - m_exp).astype(BF16)
    pv = jnp.dot(v_aug, p_t, preferred_element_type=F32)
    if col_keep is not None:
        pv = pv * col_keep
    if first:
        acc_ref[...] = pv
    else:
        acc_ref[...] = jnp.exp2(m_ref[...] - m_new) * acc_ref[...] + pv
    m_ref[...] = m_new


def _pipeline(n_items, score, attend, bufs):
    unroll = PIPELINE_UNROLL
    score(0, bufs[0])
    n_loops = (n_items - 1) // unroll

    def body(p, carry):
        for t in range(unroll):
            n = unroll * p + t
            score(n + 1, bufs[(t + 1) % 2])
            attend(n, bufs[t % 2])
        return carry

    lax.fori_loop(0, n_loops, body, 0)
    done = unroll * n_loops
    for n in range(done, n_items):
        if n + 1 < n_items:
            score(n + 1, bufs[(n + 1 - done) % 2])
        attend(n, bufs[(n - done) % 2])


def _tile_rows(j):
    return pl.ds(pl.multiple_of(j * ATTN_TILE, ATTN_TILE), ATTN_TILE)


def _write_heads(acc_ref, o_ref, nq, head_dim):
    def one(qi, carry):
        for h in range(HEAD_GROUP):
            acc = acc_ref[qi, h]
            o_ref[qi, h * head_dim:(h + 1) * head_dim, :] = (
                acc[:head_dim] / acc[head_dim:head_dim + 1]).astype(o_ref.dtype)
        return carry

    lax.fori_loop(0, nq, one, 0)


def _causal_items(nq, n_near):
    phases = [[(i, i - d) for i in range(d, nq)] for d in range(n_near)]
    phases.append([(i, j) for i in range(n_near, nq) for j in range(i - n_near + 1)])
    flat = [it for ph in phases for it in ph]
    tq = jnp.asarray([it[0] for it in flat], jnp.int32)
    tk = jnp.asarray([it[1] for it in flat], jnp.int32)
    return tq, tk, [len(ph) for ph in phases]


def _moba_kernel(tq_ref, tk_ref, qt_ref, k_ref, vt_ref, kmean_ref, bias_ref, o_ref,
                 keep_ref, s0_ref, s1_ref, t0_ref, t1_ref, m_ref, acc_ref, *, seq, phases):
    nblk = seq // MOBA_BLOCK

    def head_query(qi, h):
        dims = slice(h * A_HEAD_DIM, (h + 1) * A_HEAD_DIM)
        return qt_ref[qi, dims, :], dims

    def select_blocks(i):
        row = lax.broadcasted_iota(jnp.int32, (nblk, ATTN_TILE), 0)
        rowf = row.astype(F32)
        past = row < i
        for h in range(HEAD_GROUP):
            q_h, lanes = head_query(i, h)
            gate = jnp.dot(kmean_ref[:, lanes].astype(BF16), q_h,
                           preferred_element_type=F32)
            gate = jnp.where(past, gate, -jnp.inf)
            picked = jnp.zeros((nblk, ATTN_TILE), F32)
            for _ in range(MOBA_TOPK):
                top = jnp.max(gate, axis=0, keepdims=True)
                first = jnp.min(jnp.where(gate == top, rowf, float(nblk)), axis=0, keepdims=True)
                pick = rowf == first
                picked = jnp.where(pick, 1.0, picked)
                gate = jnp.where(pick, -jnp.inf, gate)
            keep_ref[i, h] = jnp.where(past, picked, 0.0)

    def stage(offset, bias_cols, first):
        def score(n, buf):
            qi, kj = tq_ref[offset + n], tk_ref[offset + n]
            for h in range(HEAD_GROUP):
                q_h, lanes = head_query(qi, h)
                s_t = jnp.dot(k_ref[_tile_rows(kj), lanes], q_h,
                              preferred_element_type=F32)
                if bias_cols is not None:
                    s_t = s_t + bias_ref[h, :, bias_cols]
                _store_scores(buf, h, s_t)

        def attend(n, buf):
            qi, kj = tq_ref[offset + n], tk_ref[offset + n]
            for h in range(HEAD_GROUP):
                col_keep = None if first else keep_ref[qi, h, pl.ds(kj, 1), :]
                _softmax_step(buf, h, vt_ref[kj, h * A_HEAD_DIM:(h + 1) * A_HEAD_DIM, :],
                              m_ref.at[qi, h], acc_ref.at[qi, h], first, col_keep)
            if first:
                select_blocks(qi)

        return score, attend

    bufs = ((s0_ref, t0_ref), (s1_ref, t1_ref))
    n_own, n_prev, n_far = phases
    _pipeline(n_own, *stage(0, slice(0, MOBA_BLOCK), True), bufs)
    _pipeline(n_prev, *stage(n_own, slice(MOBA_BLOCK, 2 * MOBA_BLOCK), False), bufs)
    _pipeline(n_far, *stage(n_own + n_prev, None, False), bufs)
    _write_heads(acc_ref, o_ref, nblk, A_HEAD_DIM)


def _moba(qat, ka, vat, kmean, bias):
    b, seq, _ = ka.shape
    nq = seq // ATTN_TILE
    gw = HEAD_GROUP * A_HEAD_DIM
    tq, tk, phases = _causal_items(nq, 2)
    per_group = lambda bi, g, *_: (bi, 0, g)
    tiles = lambda bi, g, *_: (bi, 0, g, 0)
    return pl.pallas_call(
        functools.partial(_moba_kernel, seq=seq, phases=phases),
        grid_spec=pltpu.PrefetchScalarGridSpec(
            num_scalar_prefetch=2,
            grid=(b, A_HEADS // HEAD_GROUP),
            in_specs=[
                pl.BlockSpec((None, nq, gw, ATTN_TILE), tiles),
                pl.BlockSpec((None, seq, gw), per_group),
                pl.BlockSpec((None, nq, gw, MOBA_BLOCK), tiles),
                pl.BlockSpec((None, nq, gw), per_group),
                pl.BlockSpec((HEAD_GROUP, MOBA_BLOCK, 2 * MOBA_BLOCK), lambda bi, g, *_: (g, 0, 0)),
            ],
            out_specs=pl.BlockSpec((None, nq, gw, ATTN_TILE), tiles),
            scratch_shapes=[
                pltpu.VMEM((nq, HEAD_GROUP, nq, ATTN_TILE), F32),
                pltpu.VMEM((HEAD_GROUP, ITEM_SPAN * ATTN_TILE, ATTN_TILE), F32),
                pltpu.VMEM((HEAD_GROUP, ITEM_SPAN * ATTN_TILE, ATTN_TILE), F32),
                pltpu.VMEM((HEAD_GROUP, ITEM_SPAN, 1, ATTN_TILE), F32),
                pltpu.VMEM((HEAD_GROUP, ITEM_SPAN, 1, ATTN_TILE), F32),
                pltpu.VMEM((nq, HEAD_GROUP, 1, ATTN_TILE), F32),
                pltpu.VMEM((nq, HEAD_GROUP, A_HEAD_DIM + SUM_ROWS, ATTN_TILE), F32),
            ]),
        out_shape=jax.ShapeDtypeStruct((b, nq, A_WIDTH, ATTN_TILE), BF16),
        compiler_params=pltpu.CompilerParams(
            dimension_semantics=("arbitrary", "arbitrary"), vmem_limit_bytes=VMEM_LIMIT),
        name="moba_attention",
    )(tq, tk, qat, ka, vat, kmean, bias)


def _mla_kernel(tq_ref, tk_ref, qt_ref, k_ref, vt_ref, o_ref, s0_ref, s1_ref, t0_ref, t1_ref,
                m_ref, acc_ref, *, nq, phases):
    def stage(offset, diagonal):
        def score(n, buf):
            qi, kj = tq_ref[offset + n], tk_ref[offset + n]
            for h in range(HEAD_GROUP):
                sl = slice(h * MLA_HEAD_PAD, h * MLA_HEAD_PAD + QK_NOPE + QK_ROPE)
                s_t = jnp.dot(k_ref[_tile_rows(kj), sl], qt_ref[qi, sl, :],
                              preferred_element_type=F32)
                if diagonal:
                    key = lax.broadcasted_iota(jnp.int32, (ATTN_TILE, ATTN_TILE), 0)
                    qry = lax.broadcasted_iota(jnp.int32, (ATTN_TILE, ATTN_TILE), 1)
                    s_t = jnp.where(key <= qry, s_t, NEG)
                _store_scores(buf, h, s_t)

        def attend(n, buf):
            qi, kj = tq_ref[offset + n], tk_ref[offset + n]
            for h in range(HEAD_GROUP):
                _softmax_step(buf, h, vt_ref[kj, h * V_HEAD:(h + 1) * V_HEAD, :],
                              m_ref.at[qi, h], acc_ref.at[qi, h], diagonal)

        return score, attend

    bufs = ((s0_ref, t0_ref), (s1_ref, t1_ref))
    n_diag, n_past = phases
    _pipeline(n_diag, *stage(0, True), bufs)
    _pipeline(n_past, *stage(n_diag, False), bufs)
    _write_heads(acc_ref, o_ref, nq, V_HEAD)


def _mla(qmt, km, vbt):
    b, seq, _ = km.shape
    nq = seq // ATTN_TILE
    qw = HEAD_GROUP * MLA_HEAD_PAD
    vw = HEAD_GROUP * V_HEAD
    tq, tk, phases = _causal_items(nq, 1)
    per_group = lambda bi, g, *_: (bi, 0, g)
    tiles = lambda bi, g, *_: (bi, 0, g, 0)
    return pl.pallas_call(
        functools.partial(_mla_kernel, nq=nq, phases=phases),
        grid_spec=pltpu.PrefetchScalarGridSpec(
            num_scalar_prefetch=2,
            grid=(b, B_HEADS // HEAD_GROUP),
            in_specs=[
                pl.BlockSpec((None, nq, qw, ATTN_TILE), tiles),
                pl.BlockSpec((None, seq, qw), per_group),
                pl.BlockSpec((None, nq, vw, ATTN_TILE), tiles),
            ],
            out_specs=pl.BlockSpec((None, nq, vw, ATTN_TILE), tiles),
            scratch_shapes=[
                pltpu.VMEM((HEAD_GROUP, ITEM_SPAN * ATTN_TILE, ATTN_TILE), F32),
                pltpu.VMEM((HEAD_GROUP, ITEM_SPAN * ATTN_TILE, ATTN_TILE), F32),
                pltpu.VMEM((HEAD_GROUP, ITEM_SPAN, 1, ATTN_TILE), F32),
                pltpu.VMEM((HEAD_GROUP, ITEM_SPAN, 1, ATTN_TILE), F32),
                pltpu.VMEM((nq, HEAD_GROUP, 1, ATTN_TILE), F32),
                pltpu.VMEM((nq, HEAD_GROUP, V_HEAD + SUM_ROWS, ATTN_TILE), F32),
            ]),
        out_shape=jax.ShapeDtypeStruct((b, nq, B_WIDTH, ATTN_TILE), BF16),
        compiler_params=pltpu.CompilerParams(
            dimension_semantics=("arbitrary", "arbitrary"), vmem_limit_bytes=VMEM_LIMIT),
        name="mla_attention",
    )(tq, tk, qmt, km, vbt)


def _out_kernel(x_ref, oa_ref, ob_ref, gattn_ref, wg_ref, wpa_ref, wpb_ref, wout_ref,
                gmlp_ref, wup_ref, wdn_ref, gfin_ref, o_ref):
    x = x_ref[...]
    n = _rms(x, gattn_ref[...]).astype(BF16)
    gates = lax.dot_general(n, wg_ref[...], NT_DIMS, preferred_element_type=F32)
    tn_dims = (((0,), (0,)), ((), ()))
    tiles = range(ROW_TILE // ATTN_TILE)
    pa = jnp.concatenate([lax.dot_general(oa_ref[t], wpa_ref[...], tn_dims,
                                          preferred_element_type=F32) for t in tiles], axis=0)
    pb = jnp.concatenate([lax.dot_general(ob_ref[t], wpb_ref[...], tn_dims,
                                          preferred_element_type=F32) for t in tiles], axis=0)
    merged = (jax.nn.sigmoid(gates[:, :D_MODEL]) * pa
              + jax.nn.sigmoid(gates[:, D_MODEL:]) * pb).astype(BF16)
    h = x + jnp.dot(merged, wout_ref[...], preferred_element_type=F32)
    m = _rms(h, gmlp_ref[...]).astype(BF16)
    for c in range(D_FF // FF_CHUNK):
        cols = slice(c * FF_CHUNK, (c + 1) * FF_CHUNK)
        up = jnp.dot(m, wup_ref[:, cols], preferred_element_type=F32)
        act = jnp.square(jnp.maximum(up, 0.0)).astype(BF16)
        h = h + jnp.dot(act, wdn_ref[cols, :], preferred_element_type=F32)
    o_ref[...] = _rms(h, gfin_ref[...])


def _out_block(x2, oa, ob, g_attn, wg, wpa, wpb, wout, g_mlp, wup, wdn, g_fin):
    n_rows = x2.shape[0]
    tm = ROW_TILE
    row = lambda i: (i, 0)
    return pl.pallas_call(
        _out_kernel,
        grid=(n_rows // tm,),
        in_specs=[
            pl.BlockSpec((tm, D_MODEL), row),
            pl.BlockSpec((tm // ATTN_TILE, A_WIDTH, ATTN_TILE), lambda i: (i, 0, 0)),
            pl.BlockSpec((tm // ATTN_TILE, B_WIDTH, ATTN_TILE), lambda i: (i, 0, 0)),
            _const_spec(g_attn.shape), _const_spec(wg.shape), _const_spec(wpa.shape),
            _const_spec(wpb.shape), _const_spec(wout.shape), _const_spec(g_mlp.shape),
            _const_spec(wup.shape), _const_spec(wdn.shape), _const_spec(g_fin.shape),
        ],
        out_specs=pl.BlockSpec((tm, D_MODEL), row),
        out_shape=jax.ShapeDtypeStruct((n_rows, D_MODEL), F32),
        compiler_params=pltpu.CompilerParams(
            dimension_semantics=("arbitrary",), vmem_limit_bytes=VMEM_LIMIT),
        name="out_mlp",
    )(x2, oa, ob, g_attn, wg, wpa, wpb, wout, g_mlp, wup, wdn, g_fin)


def _rope_tables(seq):
    half = QK_ROPE // 2
    inv_freq = ROPE_THETA ** (-jnp.arange(half, dtype=F32) / half)
    ang = jnp.arange(seq).astype(F32)[:, None] * inv_freq[None, :]
    cos2 = jnp.tile(jnp.cos(ang), (1, 2))
    sin2 = jnp.tile(jnp.sin(ang), (1, 2))
    pad = jnp.zeros((seq, MLA_HEAD_PAD - QK_NOPE - QK_ROPE), F32)
    cos_t = jnp.concatenate([jnp.ones((seq, QK_NOPE), F32), cos2, pad], axis=1)
    sin_t = jnp.concatenate([jnp.zeros((seq, QK_NOPE), F32), sin2, pad], axis=1)
    return cos_t, sin_t


def _pad_heads(w_nope, w_rope):
    k, h, _ = w_nope.shape
    if w_rope is None:
        w_rope = jnp.zeros((k, h, QK_ROPE), w_nope.dtype)
    pad = jnp.zeros((k, h, MLA_HEAD_PAD - QK_NOPE - QK_ROPE), w_nope.dtype)
    return jnp.concatenate([w_nope, w_rope, pad], axis=-1).reshape(k, h * MLA_HEAD_PAD)


def kernel(x, w_in, rel_bias, mla_q_norm, w_uq, mla_kv_norm, w_uk, w_uv, w_proj_a, w_proj_b,
           w_out, norm_attn, norm_mlp, w_mlp_up, w_mlp_down, norm_final):
    b, seq, d = x.shape
    assert d == D_MODEL and seq % IN_ROW_TILE == 0 and seq % ROW_TILE == 0
    assert w_in.shape[0] == 1, "single-layer block"
    x2 = x.reshape(b * seq, d)

    w_t = jnp.swapaxes(w_in[0], 0, 1)
    o_k = A_WIDTH
    o_v = 2 * A_WIDTH
    o_cq = 3 * A_WIDTH
    o_ckv = o_cq + Q_LORA
    o_kr = o_ckv + KV_LORA
    o_g = o_kr + QK_ROPE
    wk = w_t[o_k:o_v].astype(BF16)
    wqv = jnp.concatenate([w_t[:o_k], w_t[o_v:o_cq]], axis=0).astype(BF16)
    w_kr = w_t[o_kr:o_g]
    half = QK_ROPE // 2
    w_kr_rot = jnp.concatenate([-w_kr[half:], w_kr[:half]], axis=0)
    row_pad = lambda a: jnp.pad(a, ((QK_NOPE, MLA_HEAD_PAD - QK_NOPE - QK_ROPE), (0, 0)))
    wc = jnp.concatenate(
        [w_t[o_cq:o_kr], row_pad(w_kr), row_pad(w_kr_rot)], axis=0).astype(BF16)
    wg = w_t[o_g:].astype(BF16)

    uq = w_uq[0].reshape(Q_LORA, B_HEADS, QK_NOPE + QK_ROPE)
    uq_nope, uq_rope = uq[..., :QK_NOPE], uq[..., QK_NOPE:]
    wuq = _pad_heads(uq_nope, uq_rope).T.astype(BF16)
    wuk = _pad_heads(w_uk[0].reshape(KV_LORA, B_HEADS, QK_NOPE), None).astype(BF16)
    wuv = w_uv[0].T.astype(BF16)

    cos_t, sin_t = _rope_tables(seq)
    q_scale = (QK_NOPE + QK_ROPE) ** -0.5 * LOG2E
    row2 = lambda a: a.reshape(1, -1)

    qat, ka, vat, kmean, qmt, km, vbt = _inproj(
        x2, row2(norm_attn[0]), wk, wqv, wc, row2(mla_q_norm[0]), row2(mla_kv_norm[0]),
        wuq, wuk, wuv, (cos_t * q_scale).T, (sin_t * q_scale).T, cos_t, sin_t, b, seq)

    as_seq = lambda a: a.reshape(b, seq, a.shape[-1])
    bias = _bias_tiles(rel_bias)
    oa = _moba(qat, as_seq(ka), vat, kmean.reshape(b, seq // MOBA_BLOCK, A_WIDTH), bias)
    ob = _mla(qmt, as_seq(km), vbt)

    out = _out_block(
        x2, oa.reshape(-1, A_WIDTH, ATTN_TILE), ob.reshape(-1, B_WIDTH, ATTN_TILE),
        row2(norm_attn[0]),
        wg, w_proj_a[0].astype(BF16), w_proj_b[0].astype(BF16), w_out[0].astype(BF16),
        row2(norm_mlp[0]), w_mlp_up[0].astype(BF16), w_mlp_down[0].astype(BF16),
        row2(norm_final))
    return out.reshape(b, seq, d)
```
